```python
import math
import jax
import jax.numpy as jnp
from jax import lax
import numpy as np

D_MODEL = 1024
BATCH = 1
SEQ = 16384
DEPTH = 2
DEC_BATCH = 128
DEC_SEQ = 1
PAST_LEN = 16384
PAGE_SIZE = 128

HEAD_DIM = 64
N_RET_HEADS = 8
N_SWA_HEADS = 8
N_KV_HEADS = 2
GQA_GROUP = N_SWA_HEADS // N_KV_HEADS
RET_W = N_RET_HEADS * HEAD_DIM
SWA_QW = N_SWA_HEADS * HEAD_DIM
SWA_KVW = N_KV_HEADS * HEAD_DIM
MIX_WIDTH = RET_W + SWA_QW
PROJ_SPLITS = (RET_W, 2 * RET_W, 3 * RET_W, 4 * RET_W, 4 * RET_W + SWA_QW, 4 * RET_W + SWA_QW + SWA_KVW)
PROJ_W = 4 * RET_W + SWA_QW + 2 * SWA_KVW
RET_CHUNK = 128
ROPE_BASE = 10000.0
WINDOW = 128
N_BUCKETS = 32
MAX_DISTANCE = 128
N_EXPERTS = 16
N_GROUPS = 4
EXPERTS_PER_GROUP = N_EXPERTS // N_GROUPS
TOP_K = 2
D_FF_EXPERT = 512
LN_EPS = 1e-5
DN_ALPHA = (2 * DEPTH) ** 0.25
DN_BETA = (8 * DEPTH) ** -0.25

kernel_name = "hymba_retention_swa_sink_moe_step"

F32 = jnp.float32


def layer_norm(x, g, b):
    xf = x.astype(F32)
    mu = xf.mean(-1, keepdims=True)
    var = jnp.mean(jnp.square(xf - mu), -1, keepdims=True)
    return ((xf - mu) * lax.rsqrt(var + LN_EPS) * g.astype(F32) + b.astype(F32)).astype(x.dtype)


def rotary(x, pos):
    half = HEAD_DIM // 2
    inv = ROPE_BASE ** (-jnp.arange(half, dtype=F32) / half)
    ang = pos.astype(F32)[:, None] * inv[None, :]
    cos = jnp.cos(ang)[None, :, None, :]
    sin = jnp.sin(ang)[None, :, None, :]
    x1 = x[..., :half].astype(F32)
    x2 = x[..., half:].astype(F32)
    return jnp.concatenate([x1 * cos - x2 * sin, x2 * cos + x1 * sin], -1).astype(x.dtype)


def ret_log_gamma():
    return jnp.log(1.0 - 2.0 ** (-5.0 - jnp.arange(N_RET_HEADS, dtype=F32)))


def retention_chunk(state, q, k, v):
    L = q.shape[1]
    lg = ret_log_gamma()
    idx = jnp.arange(L, dtype=F32)
    diff = idx[:, None] - idx[None, :]
    decay = jnp.where(diff >= 0, jnp.exp(jnp.maximum(diff, 0.0)[None] * lg[:, None, None]), 0.0)
    qf, kf, vf = q.astype(F32), k.astype(F32), v.astype(F32)
    scores = jnp.einsum('bqhd,bkhd->bhqk', qf, kf) * decay[None]
    intra = jnp.einsum('bhqk,bkhe->bqhe', scores, vf)
    xi = jnp.exp((idx + 1.0)[:, None] * lg[None, :])
    inter = jnp.einsum('bqhd,bhde->bqhe', qf, state) * xi[None, :, :, None]
    zeta = jnp.exp((L - 1.0 - idx)[:, None] * lg[None, :])
    new_state = jnp.exp(L * lg)[None, :, None, None] * state + jnp.einsum('bkhd,bkhe->bhde', kf * zeta[None, :, :, None], vf)
    return new_state, intra + inter


def retention_prompt(q, k, v):
    B, T = q.shape[:2]
    nc = T // RET_CHUNK

    def to_chunks(a):
        return a.reshape(B, nc, RET_CHUNK, N_RET_HEADS, HEAD_DIM).swapaxes(0, 1)

    s0 = jnp.zeros((B, N_RET_HEADS, HEAD_DIM, HEAD_DIM), F32)

    def step(s, qkv):
        return retention_chunk(s, qkv[0], qkv[1], qkv[2])

    s_fin, o = lax.scan(step, s0, (to_chunks(q), to_chunks(k), to_chunks(v)))
    return o.swapaxes(0, 1).reshape(B, T, N_RET_HEADS, HEAD_DIM), s_fin


def t5_bucket(rel):
    max_exact = N_BUCKETS // 2
    relf = jnp.maximum(rel, 1).astype(F32)
    large = max_exact + (jnp.log(relf / max_exact) / math.log(MAX_DISTANCE / max_exact) * (N_BUCKETS - max_exact)).astype(jnp.int32)
    large = jnp.minimum(large, N_BUCKETS - 1)
    return jnp.where(rel < max_exact, rel, large)


def rel_bias(rel, bias_table):
    b = bias_table.astype(F32)[t5_bucket(jnp.maximum(rel, 0))]
    return b.transpose(2, 0, 1).reshape(N_KV_HEADS, GQA_GROUP, rel.shape[0], rel.shape[1])


def sink_probs(s, sinks, mask):
    s = jnp.where(mask, s, -jnp.inf)
    sk = sinks.astype(F32).reshape(N_KV_HEADS, GQA_GROUP)[:, :, None, None]
    m = jnp.maximum(s.max(-1, keepdims=True), sk)
    p = jnp.exp(s - m)
    return p / (p.sum(-1, keepdims=True) + jnp.exp(sk - m))


def swa_prompt(q, k, v, sinks, bias_table):
    B, T = q.shape[:2]
    blk = WINDOW
    nb = T // blk
    qb = q.reshape(B, nb, blk, N_KV_HEADS, GQA_GROUP, HEAD_DIM).astype(F32)
    kb = k.reshape(B, nb, blk, N_KV_HEADS, HEAD_DIM).astype(F32)
    vb = v.reshape(B, nb, blk, N_KV_HEADS, HEAD_DIM).astype(F32)

    def with_prev(a):
        prev = jnp.concatenate([jnp.zeros_like(a[:, :1]), a[:, :-1]], 1)
        return jnp.concatenate([prev, a], 2)

    kk, vv = with_prev(kb), with_prev(vb)
    s = jnp.einsum('bnqkgd,bnskd->bnkgqs', qb, kk) * HEAD_DIM ** -0.5
    qi = jnp.arange(blk)
    si = jnp.arange(2 * blk)
    rel = blk + qi[:, None] - si[None, :]
    band = (rel >= 0) & (rel < WINDOW)
    blk_ok = (jnp.arange(nb)[:, None] > 0) | (si[None, :] >= blk)
    mask = band[None] & blk_ok[:, None, :]
    s = s + rel_bias(rel, bias_table)
    p = sink_probs(s, sinks, mask[None, :, None, None])
    o = jnp.einsum('bnkgqs,bnskd->bnqkgd', p, vv)
    return o.reshape(B, T, SWA_QW)


def swa_sample(q, k_new, v_new, k_buf, v_buf, sinks, bias_table):
    DB, L = q.shape[:2]
    W = k_buf.shape[1]
    kk = jnp.concatenate([k_buf, k_new.astype(k_buf.dtype)], 1)
    vv = jnp.concatenate([v_buf, v_new.astype(v_buf.dtype)], 1)
    qpos = PAST_LEN + jnp.arange(L)
    kpos = jnp.concatenate([PAST_LEN - W + jnp.arange(W), PAST_LEN + jnp.arange(L)])
    rel = qpos[:, None] - kpos[None, :]
    mask = (rel >= 0) & (rel < WINDOW)
    qg = q.reshape(DB, L, N_KV_HEADS, GQA_GROUP, HEAD_DIM).astype(F32)
    s = jnp.einsum('bqkgd,bskd->bkgqs', qg, kk.astype(F32)) * HEAD_DIM ** -0.5
    s = s + rel_bias(rel, bias_table)
    p = sink_probs(s, sinks, mask)
    o = jnp.einsum('bkgqs,bskd->bqkgd', p, vv.astype(F32))
    return o.reshape(DB, L, SWA_QW), kk[:, -W:], vv[:, -W:]


def token_mixing(h, pos, w_in, w_out, gn_gain, sinks, bias_table, ret_state=None, k_buf=None, v_buf=None):
    B, L, _ = h.shape
    proj = h @ w_in
    rq, rk, rv, rg, sq, sk, sv = jnp.split(proj, PROJ_SPLITS, axis=-1)

    def heads(a, n):
        return a.reshape(B, L, n, HEAD_DIM)

    rq = rotary(heads(rq, N_RET_HEADS), pos)
    rk = rotary(heads(rk, N_RET_HEADS), pos) * HEAD_DIM ** -0.5
    rv = heads(rv, N_RET_HEADS)
    sq = heads(sq, N_SWA_HEADS)
    sk = heads(sk, N_KV_HEADS)
    sv = heads(sv, N_KV_HEADS)
    if ret_state is None:
        ret_o, ret_new = retention_prompt(rq, rk, rv)
        swa_o = swa_prompt(sq, sk, sv, sinks, bias_table)
        buf = min(WINDOW, L)
        k_new, v_new = sk[:, L - buf:], sv[:, L - buf:]
    else:
        ret_new, ret_o = retention_chunk(ret_state.astype(F32), rq, rk, rv)
        swa_o, k_new, v_new = swa_sample(sq, sk, sv, k_buf, v_buf, sinks, bias_table)
    mu = ret_o.mean(-1, keepdims=True)
    var = jnp.mean(jnp.square(ret_o - mu), -1, keepdims=True)
    ret_o = (ret_o - mu) * lax.rsqrt(var + LN_EPS) * gn_gain.astype(F32)
    ret_o = ret_o * jax.nn.silu(heads(rg, N_RET_HEADS).astype(F32))
    mixed = jnp.concatenate([ret_o.reshape(B, L, RET_W).astype(h.dtype), swa_o.astype(h.dtype)], -1)
    return mixed @ w_out, (ret_new, k_new, v_new)


def moe_ffn(h, router_w, router_b, w_gate, w_up, w_down):
    logits = jnp.einsum('bld,de->ble', h.astype(F32), router_w.astype(F32))
    aff = jax.nn.sigmoid(logits)
    sel = aff + router_b.astype(F32)
    grp = sel.reshape(sel.shape[:-1] + (N_GROUPS, EXPERTS_PER_GROUP))
    gscore = lax.top_k(grp, TOP_K)[0].sum(-1)
    gidx = jnp.argmax(gscore, -1)
    in_grp = (jnp.arange(N_EXPERTS) // EXPERTS_PER_GROUP)[None, None, :] == gidx[..., None]
    _, eidx = lax.top_k(jnp.where(in_grp, sel, -jnp.inf), TOP_K)
    w = jnp.take_along_axis(aff, eidx, -1)
    w = w / w.sum(-1, keepdims=True)
    dense_w = (jax.nn.one_hot(eidx, N_EXPERTS, dtype=F32) * w[..., None]).sum(-2)
    y = jnp.zeros(h.shape, F32)
    for e in range(N_EXPERTS):
        a = jax.nn.silu(h @ w_gate[e]) * (h @ w_up[e])
        y = y + dense_w[..., e:e + 1] * (a @ w_down[e]).astype(F32)
    return y.astype(h.dtype)


def decoder_layer(x, c, pos, cache, w_in, w_out, gn_gain, sinks, bias_table, ada_w, ada_b,
                  ln1_g, ln1_b, ln2_g, ln2_b, router_w, router_b, w_gate, w_up, w_down):
    mod = (c @ ada_w + ada_b)[:, None, :]
    sh1, sc1, g1, sh2, sc2, g2 = jnp.split(mod, 6, axis=-1)
    mixed, new_state = token_mixing(x * (1 + sc1) + sh1, pos, w_in, w_out, gn_gain, sinks, bias_table, *cache)
    x = layer_norm(DN_ALPHA * x + (1 + g1) * mixed, ln1_g, ln1_b)
    ffn = moe_ffn(x * (1 + sc2) + sh2, router_w, router_b, w_gate, w_up, w_down)
    x = layer_norm(DN_ALPHA * x + (1 + g2) * ffn, ln2_g, ln2_b)
    return x, new_state


def setup_inputs(seed: int = 0) -> dict:
    key = jax.random.key(seed)
    ks = jax.random.split(key, 24)

    def nrm(k, shape, s):
        return jax.random.normal(k, shape, F32) * s

    swa_buf = min(WINDOW, PAST_LEN)
    col_scale = np.ones((PROJ_W,), np.float32)
    col_scale[2 * RET_W:3 * RET_W] = DN_BETA
    col_scale[PROJ_SPLITS[5]:] = DN_BETA
    w_in = nrm(ks[8], (DEPTH, D_MODEL, PROJ_W), D_MODEL ** -0.5) * jnp.asarray(col_scale)
    return {
        "x_prompt": nrm(ks[0], (BATCH, SEQ, D_MODEL), 1.0),
        "x_sample": nrm(ks[1], (DEC_BATCH, DEC_SEQ, D_MODEL), 1.0),
        "state_ret": nrm(ks[2], (DEPTH, DEC_BATCH, N_RET_HEADS, HEAD_DIM, HEAD_DIM), 1.0),
        "cache_swa_k": nrm(ks[3], (DEPTH, DEC_BATCH, swa_buf, N_KV_HEADS, HEAD_DIM), 1.0),
        "cache_swa_v": nrm(ks[4], (DEPTH, DEC_BATCH, swa_buf, N_KV_HEADS, HEAD_DIM), DN_BETA),
        "c_prompt": nrm(ks[5], (BATCH, D_MODEL), 1.0),
        "c_sample": nrm(ks[6], (DEC_BATCH, D_MODEL), 1.0),
        "w_in": w_in,
        "w_out": nrm(ks[9], (DEPTH, MIX_WIDTH, D_MODEL), MIX_WIDTH ** -0.5 * DN_BETA),
        "ret_gn_gain": 1.0 + nrm(ks[10], (DEPTH, N_RET_HEADS, HEAD_DIM), 0.02),
        "swa_sinks": nrm(ks[11], (DEPTH, N_SWA_HEADS), 0.5),
        "rel_bias_table": nrm(ks[12], (N_BUCKETS, N_SWA_HEADS), 0.5),
        "ada_w": nrm(ks[13], (DEPTH, D_MODEL, 6 * D_MODEL), 0.1 * D_MODEL ** -0.5),
        "ada_b": nrm(ks[14], (DEPTH, 6 * D_MODEL), 0.02),
        "ln1_g": 1.0 + nrm(ks[15], (DEPTH, D_MODEL), 0.02),
        "ln1_b": nrm(ks[16], (DEPTH, D_MODEL), 0.02),
        "ln2_g": 1.0 + nrm(ks[17], (DEPTH, D_MODEL), 0.02),
        "ln2_b": nrm(ks[18], (DEPTH, D_MODEL), 0.02),
        "router_w": nrm(ks[19], (D_MODEL, N_EXPERTS), D_MODEL ** -0.5),
        "router_b": nrm(ks[20], (N_EXPERTS,), 0.01),
        "exp_w_gate": nrm(ks[21], (DEPTH, N_EXPERTS, D_MODEL, D_FF_EXPERT), D_MODEL ** -0.5),
        "exp_w_up": nrm(ks[22], (DEPTH, N_EXPERTS, D_MODEL, D_FF_EXPERT), D_MODEL ** -0.5),
        "exp_w_down": nrm(ks[23], (DEPTH, N_EXPERTS, D_FF_EXPERT, D_MODEL), D_FF_EXPERT ** -0.5 * DN_BETA),
    }


def reference(x_prompt, x_sample, state_ret, cache_swa_k, cache_swa_v, c_prompt, c_sample,
              w_in, w_out, ret_gn_gain, swa_sinks, rel_bias_table, ada_w, ada_b,
              ln1_g, ln1_b, ln2_g, ln2_b, router_w, router_b, exp_w_gate, exp_w_up, exp_w_down):
    pos_p = jnp.arange(x_prompt.shape[1], dtype=jnp.int32)
    pos_s = PAST_LEN + jnp.arange(x_sample.shape[1], dtype=jnp.int32)
    yp, ys = x_prompt, x_sample
    st_p, st_s = [], []
    for l in range(DEPTH):
        w = (w_in[l], w_out[l], ret_gn_gain[l], swa_sinks[l], rel_bias_table, ada_w[l], ada_b[l],
             ln1_g[l], ln1_b[l], ln2_g[l], ln2_b[l], router_w, router_b,
             exp_w_gate[l], exp_w_up[l], exp_w_down[l])
        yp, sp = decoder_layer(yp, c_prompt, pos_p, (), *w)
        ys, ss = decoder_layer(ys, c_sample, pos_s, (state_ret[l], cache_swa_k[l], cache_swa_v[l]), *w)
        st_p.append(sp)
        st_s.append(ss)

    def stack(lst, i):
        return jnp.stack([s[i] for s in lst])

    return (yp, ys, stack(st_p, 0), stack(st_p, 1), stack(st_p, 2), stack(st_s, 0), stack(st_s, 1), stack(st_s, 2))
```

```python
import functools
import math

import numpy as np
import jax
import jax.numpy as jnp
from jax import lax
from jax.experimental import pallas as pl
from jax.experimental.pallas import tpu as pltpu

F32 = jnp.float32
BF16 = jnp.bfloat16
I32 = jnp.int32

D_MODEL = 1024
DEPTH = 2
HEAD_DIM = 64
N_RET_HEADS = 8
N_SWA_HEADS = 8
N_KV_HEADS = 2
GQA_GROUP = N_SWA_HEADS // N_KV_HEADS
RET_W = N_RET_HEADS * HEAD_DIM
SWA_QW = N_SWA_HEADS * HEAD_DIM
SWA_KVW = N_KV_HEADS * HEAD_DIM
PROJ_W = 4 * RET_W + SWA_QW + 2 * SWA_KVW
CHUNK = 128
ROPE_BASE = 10000.0
N_BUCKETS = 32
MAX_DISTANCE = 128
N_EXPERTS = 16
N_GROUPS = 4
EXPERTS_PER_GROUP = 4
D_FF = 512
LN_EPS = 1e-5
DN_ALPHA = (2 * DEPTH) ** 0.25
QK_SCALE = HEAD_DIM ** -0.5
NEG_BIG = -1e30

N_PAIRS = 6
N_BINS = N_GROUPS * N_PAIRS
BIN_ROWS = 32
MOE_TM = 256
INPROJ_TM = 512
VMEM_LIMIT = 56 * 1024 * 1024

_PAIR_LO = (0, 0, 0, 1, 1, 2)
_PAIR_HI = (1, 2, 3, 2, 3, 3)


def _cparams(sem):
    return pltpu.CompilerParams(dimension_semantics=sem, vmem_limit_bytes=VMEM_LIMIT)


def _ln(v, g, b):
    mu = jnp.mean(v, axis=-1, keepdims=True)
    c = v - mu
    var = jnp.mean(c * c, axis=-1, keepdims=True)
    return c * lax.rsqrt(var + LN_EPS) * g + b


def _silu(v):
    return v * (1.0 / (1.0 + jnp.exp(-v)))


def _ada_kernel(c_ref, w_ref, b_ref, o_ref):
    o_ref[0] = jnp.dot(c_ref[...].astype(BF16), w_ref[0].astype(BF16),
                       preferred_element_type=F32) + b_ref[0]


def _ada(c_all, ada_w, ada_b):
    rows = c_all.shape[0]
    nt = 6 * D_MODEL // 1024
    return pl.pallas_call(
        _ada_kernel,
        grid=(DEPTH, nt),
        in_specs=[pl.BlockSpec((rows, D_MODEL), lambda l, j: (0, 0)),
                  pl.BlockSpec((1, D_MODEL, 1024), lambda l, j: (l, 0, j)),
                  pl.BlockSpec((1, 1, 1024), lambda l, j: (l, 0, j))],
        out_specs=pl.BlockSpec((1, rows, 1024), lambda l, j: (l, 0, j)),
        out_shape=jax.ShapeDtypeStruct((DEPTH, rows, 6 * D_MODEL), F32),
        compiler_params=_cparams(("arbitrary", "arbitrary")),
        name="ada",
    )(c_all, ada_w, ada_b.reshape(DEPTH, 1, 6 * D_MODEL))


def _bias_kernel(tab_ref, bkt_ref, full_ref, first_ref, row_ref):
    bkt = bkt_ref[...]
    col = lax.broadcasted_iota(I32, bkt.shape, 1)
    for h in range(N_SWA_HEADS):
        acc = jnp.full(bkt.shape, NEG_BIG, F32)
        for b in range(N_BUCKETS):
            acc = jnp.where(bkt == b, tab_ref[b, h], acc)
        full_ref[h] = acc
        first_ref[h] = jnp.where(col < CHUNK, NEG_BIG, acc)
        row_ref[h:h + 1, :] = acc[CHUNK - 1:CHUNK, CHUNK:]


def _t5_bucket(rel):
    max_exact = N_BUCKETS // 2
    relf = jnp.maximum(rel, 1).astype(F32)
    large = max_exact + (jnp.log(relf / max_exact) / math.log(MAX_DISTANCE / max_exact)
                         * (N_BUCKETS - max_exact)).astype(I32)
    large = jnp.minimum(large, N_BUCKETS - 1)
    return jnp.where(rel < max_exact, rel, large)


def _bias_tables(rel_bias_table):
    qi = jnp.arange(CHUNK)
    si = jnp.arange(2 * CHUNK)
    rel = CHUNK + qi[:, None] - si[None, :]
    band = (rel >= 0) & (rel < CHUNK)
    bkt = jnp.where(band, _t5_bucket(jnp.maximum(rel, 0)), -1).astype(I32)
    return pl.pallas_call(
        _bias_kernel,
        in_specs=[pl.BlockSpec(memory_space=pltpu.SMEM),
                  pl.BlockSpec((CHUNK, 2 * CHUNK), lambda: (0, 0))],
        out_specs=[pl.BlockSpec((N_SWA_HEADS, CHUNK, 2 * CHUNK), lambda: (0, 0, 0)),
                   pl.BlockSpec((N_SWA_HEADS, CHUNK, 2 * CHUNK), lambda: (0, 0, 0)),
                   pl.BlockSpec((N_SWA_HEADS, CHUNK), lambda: (0, 0))],
        out_shape=[jax.ShapeDtypeStruct((N_SWA_HEADS, CHUNK, 2 * CHUNK), F32),
                   jax.ShapeDtypeStruct((N_SWA_HEADS, CHUNK, 2 * CHUNK), F32),
                   jax.ShapeDtypeStruct((N_SWA_HEADS, CHUNK), F32)],
        name="t5_bias",
    )(rel_bias_table, bkt)


def _rotary(v, cos, s_lo, s_hi):
    outs = []
    for j in range(RET_W // 128):
        blk = v[:, j * 128:(j + 1) * 128]
        outs.append(blk * cos + pltpu.roll(blk, 96, 1) * s_lo + pltpu.roll(blk, 32, 1) * s_hi)
    return jnp.concatenate(outs, axis=-1)


def _inproj_kernel(x_ref, sh_ref, sc_ref, w_ref, cos_ref, slo_ref, shi_ref, zeta_ref,
                   q_ref, k_ref, kz_ref, v_ref, g_ref, sq_ref, skv_ref, *, mod_rows):
    sh = sh_ref[0][:mod_rows]
    sc = sc_ref[0][:mod_rows]
    h = (x_ref[...] * (1.0 + sc) + sh).astype(BF16)
    cos, s_lo, s_hi = cos_ref[...], slo_ref[...], shi_ref[...]

    def proj(lo, hi):
        return jnp.dot(h, w_ref[:, lo:hi], preferred_element_type=F32)

    q_ref[...] = _rotary(proj(0, RET_W), cos, s_lo, s_hi).astype(BF16)
    k = _rotary(proj(RET_W, 2 * RET_W), cos, s_lo, s_hi) * QK_SCALE
    k_ref[...] = k.astype(BF16)
    kz_ref[...] = (k * zeta_ref[...]).astype(BF16)
    v_ref[...] = proj(2 * RET_W, 3 * RET_W).astype(BF16)
    g_ref[...] = proj(3 * RET_W, 4 * RET_W)
    sq_ref[...] = (proj(4 * RET_W, 4 * RET_W + SWA_QW) * QK_SCALE).astype(BF16)
    skv_ref[...] = proj(4 * RET_W + SWA_QW, PROJ_W)


def _inproj(x, mod, layer, mod_row0, mod_rows, w_in_bf, rope, zeta_tile, tm):
    rows = x.shape[0]
    mblk = 8 if mod_rows == 1 else mod_rows
    mrow = mod_row0 // mblk
    cos, s_lo, s_hi = rope
    row_spec = lambda w: pl.BlockSpec((tm, w), lambda i: (i, 0))
    mod_spec = lambda c: pl.BlockSpec((1, mblk, D_MODEL), lambda i: (layer, mrow, c))
    return pl.pallas_call(
        functools.partial(_inproj_kernel, mod_rows=mod_rows),
        grid=(rows // tm,),
        in_specs=[row_spec(D_MODEL), mod_spec(0), mod_spec(1),
                  pl.BlockSpec((D_MODEL, PROJ_W), lambda i: (0, 0)),
                  row_spec(128), row_spec(128), row_spec(128),
                  pl.BlockSpec((tm, RET_W), lambda i: (0, 0))],
        out_specs=[row_spec(RET_W), row_spec(RET_W), row_spec(RET_W), row_spec(RET_W), row_spec(RET_W),
                   row_spec(SWA_QW), row_spec(2 * SWA_KVW)],
        out_shape=[jax.ShapeDtypeStruct((rows, RET_W), BF16),
                   jax.ShapeDtypeStruct((rows, RET_W), BF16),
                   jax.ShapeDtypeStruct((rows, RET_W), BF16),
                   jax.ShapeDtypeStruct((rows, RET_W), BF16),
                   jax.ShapeDtypeStruct((rows, RET_W), F32),
                   jax.ShapeDtypeStruct((rows, SWA_QW), BF16),
                   jax.ShapeDtypeStruct((rows, 2 * SWA_KVW), F32)],
        compiler_params=_cparams(("arbitrary",)),
        name="inproj",
    )(x, mod, mod, w_in_bf, cos, s_lo, s_hi, zeta_tile)


def _rope_tables(pos):
    half = HEAD_DIM // 2
    inv = ROPE_BASE ** (-jnp.arange(half, dtype=F32) / half)
    ang = pos.astype(F32)[:, None] * inv[None, :]
    cos, sin = jnp.cos(ang), jnp.sin(ang)
    zero = jnp.zeros_like(sin)
    cos_t = jnp.concatenate([cos, cos, cos, cos], axis=-1)
    s_lo = jnp.concatenate([-sin, zero, -sin, zero], axis=-1)
    s_hi = jnp.concatenate([zero, sin, zero, sin], axis=-1)
    return cos_t, s_lo, s_hi


def _mix_prompt_kernel(gl_ref, sink_ref,
                       q_ref, k_ref, kz_ref, v_ref, g_ref, sq_ref, kvc_ref, kvp_ref, x_ref, gate_ref,
                       wout_ref, decay_ref, xi_ref, bias_ref, bias0_ref,
                       gain_ref, lng_ref, lnb_ref,
                       x1_ref, st_ref, state):
    i = pl.program_id(0)

    @pl.when(i == 0)
    def _():
        state[...] = jnp.zeros_like(state)

    q = q_ref[...]
    k = k_ref[...]
    v = v_ref[...]
    kz = kz_ref[...]

    intra, inter = [], []
    for h in range(N_RET_HEADS):
        sl = slice(h * HEAD_DIM, (h + 1) * HEAD_DIM)
        qh, kh, vh = q[:, sl], k[:, sl], v[:, sl]
        s = lax.dot_general(qh, kh, (((1,), (1,)), ((), ())), preferred_element_type=F32)
        p = (s * decay_ref[h]).astype(BF16)
        intra.append(jnp.dot(p, vh, preferred_element_type=F32))
        s_old = state[h]
        inter.append(jnp.dot(qh, s_old.astype(BF16), preferred_element_type=F32))
        upd = lax.dot_general(kz[:, sl], vh, (((0,), (0,)), ((), ())), preferred_element_type=F32)
        state[h] = gl_ref[h] * s_old + upd

    xi = xi_ref[...]
    g = g_ref[...]
    ret_parts = []
    for h in range(N_RET_HEADS):
        sl = slice(h * HEAD_DIM, (h + 1) * HEAD_DIM)
        o = intra[h] + inter[h] * xi[:, sl]
        mu = jnp.mean(o, axis=-1, keepdims=True)
        c = o - mu
        var = jnp.mean(c * c, axis=-1, keepdims=True)
        ret_parts.append(c * lax.rsqrt(var + LN_EPS))
    ret_o = jnp.concatenate(ret_parts, axis=-1) * gain_ref[...] * _silu(g)

    sq = sq_ref[...]
    kvc = kvc_ref[...]
    kvp = kvp_ref[...]
    swa_parts = []
    for kv in range(N_KV_HEADS):
        ks = slice(kv * HEAD_DIM, (kv + 1) * HEAD_DIM)
        vs = slice(SWA_KVW + kv * HEAD_DIM, SWA_KVW + (kv + 1) * HEAD_DIM)
        kk = jnp.concatenate([kvp[:, ks], kvc[:, ks]], axis=0).astype(BF16)
        vv = jnp.concatenate([kvp[:, vs], kvc[:, vs]], axis=0).astype(BF16)
        for j in range(GQA_GROUP):
            hh = kv * GQA_GROUP + j
            qh = sq[:, hh * HEAD_DIM:(hh + 1) * HEAD_DIM]
            s = lax.dot_general(qh, kk, (((1,), (1,)), ((), ())), preferred_element_type=F32)
            s = s + jnp.where(i == 0, bias0_ref[hh], bias_ref[hh])
            sink = sink_ref[hh]
            m = jnp.maximum(jnp.max(s, axis=-1, keepdims=True), sink)
            p = jnp.exp(s - m)
            den = jnp.sum(p, axis=-1, keepdims=True) + jnp.exp(sink - m)
            swa_parts.append(jnp.dot((p / den).astype(BF16), vv, preferred_element_type=F32))
    swa_o = jnp.concatenate(swa_parts, axis=-1)

    mixed = jnp.concatenate([ret_o, swa_o], axis=-1).astype(BF16)
    y = jnp.dot(mixed, wout_ref[...], preferred_element_type=F32)
    gate = gate_ref[0][:1]
    x1_ref[...] = _ln(DN_ALPHA * x_ref[...] + (1.0 + gate) * y, lng_ref[...], lnb_ref[...])

    @pl.when(i == pl.num_programs(0) - 1)
    def _():
        st_ref[...] = state[...]


def _mix_prompt(proj, x, mod, layer, mod_row0, w_out_bf, ret_consts, bias_tabs, sinks,
                gain, ln_g, ln_b):
    q, k, kz, v, g, sq, skv = proj
    rows = x.shape[0]
    nblk = rows // CHUNK
    gl, decay, xi = ret_consts
    bias_full, bias_first, _ = bias_tabs
    mrow = mod_row0 // 8
    row_spec = lambda w: pl.BlockSpec((CHUNK, w), lambda i: (i, 0))
    const2 = lambda a: pl.BlockSpec(a.shape, lambda i: (0, 0))
    const3 = lambda a: pl.BlockSpec(a.shape, lambda i: (0, 0, 0))
    smem = pl.BlockSpec(memory_space=pltpu.SMEM)
    return pl.pallas_call(
        _mix_prompt_kernel,
        grid=(nblk,),
        in_specs=[smem, smem,
                  row_spec(RET_W), row_spec(RET_W), row_spec(RET_W), row_spec(RET_W), row_spec(RET_W),
                  row_spec(SWA_QW), row_spec(2 * SWA_KVW),
                  pl.BlockSpec((CHUNK, 2 * SWA_KVW), lambda i: (jnp.maximum(i - 1, 0), 0)),
                  row_spec(D_MODEL),
                  pl.BlockSpec((1, 8, D_MODEL), lambda i: (layer, mrow, 2)),
                  const2(w_out_bf), const3(decay), const2(xi),
                  const3(bias_full), const3(bias_first),
                  const2(gain), const2(ln_g), const2(ln_b)],
        out_specs=[row_spec(D_MODEL),
                   pl.BlockSpec((N_RET_HEADS, HEAD_DIM, HEAD_DIM), lambda i: (0, 0, 0))],
        out_shape=[jax.ShapeDtypeStruct((rows, D_MODEL), F32),
                   jax.ShapeDtypeStruct((N_RET_HEADS, HEAD_DIM, HEAD_DIM), F32)],
        scratch_shapes=[pltpu.VMEM((N_RET_HEADS, HEAD_DIM, HEAD_DIM), F32)],
        compiler_params=_cparams(("arbitrary",)),
        name="mix_prompt",
    )(gl, sinks, q, k, kz, v, g, sq, skv, skv, x, mod, w_out_bf, decay, xi,
      bias_full, bias_first, gain, ln_g, ln_b)


def _ret_consts():
    lg = jnp.log(1.0 - 2.0 ** (-5.0 - jnp.arange(N_RET_HEADS, dtype=F32)))
    idx = jnp.arange(CHUNK, dtype=F32)
    diff = idx[:, None] - idx[None, :]
    decay = jnp.where(diff >= 0, jnp.exp(jnp.maximum(diff, 0.0)[None] * lg[:, None, None]), 0.0)
    xi = jnp.exp((idx + 1.0)[:, None] * lg[None, :])
    zeta = jnp.exp((CHUNK - 1.0 - idx)[:, None] * lg[None, :])
    gl = jnp.exp(CHUNK * lg)
    expand = lambda a: jnp.repeat(a, HEAD_DIM, axis=1)
    return gl, decay, expand(xi), expand(zeta), jnp.exp(1.0 * lg)


SAMPLE_BB = 8


def _mix_sample_kernel(g1_ref, sink_ref,
                       qt_ref, kt_ref, v3_ref, sq3_ref, knew_ref, vnew_ref,
                       st_ref, ck_ref, cv_ref, g3_ref, x_ref, gate_ref,
                       wout_ref, brow_ref, gain_ref, lng_ref, lnb_ref,
                       x1_ref, nst_ref, nk_ref, nv_ref, ret_scr, swa_scr):
    i = pl.program_id(0)
    row = lax.broadcasted_iota(I32, (N_SWA_HEADS, 2 * HEAD_DIM), 0)
    lane = lax.broadcasted_iota(I32, (N_SWA_HEADS, 2 * HEAD_DIM), 1)
    own_half = (row // GQA_GROUP) == (lane // HEAD_DIM)
    sink_col = jnp.concatenate(
        [jnp.full((1, 1), sink_ref[hh], F32) for hh in range(N_SWA_HEADS)], axis=0)
    brow = brow_ref[...]
    qt = qt_ref[0]
    kt = kt_ref[0]

    for b in range(SAMPLE_BB):
        o_rows = []
        for h in range(N_RET_HEADS):
            rs = slice(h * HEAD_DIM, (h + 1) * HEAD_DIM)
            s_old = st_ref[b, rs, :]
            kcol = kt[rs, b:b + 1]
            qcol = qt[rs, b:b + 1]
            vrow = v3_ref[b, h:h + 1, :]
            s_new = g1_ref[h] * s_old + kcol * vrow
            nst_ref[b, rs, :] = s_new
            o_rows.append(jnp.sum(qcol * s_new, axis=0, keepdims=True))
        ret_scr[i * SAMPLE_BB + b] = jnp.concatenate(o_rows, axis=0)

        kk = jnp.concatenate([ck_ref[b, 1:, :], knew_ref[b:b + 1, :]], axis=0)
        vv = jnp.concatenate([cv_ref[b, 1:, :], vnew_ref[b:b + 1, :]], axis=0)
        nk_ref[b] = kk
        nv_ref[b] = vv
        q8 = sq3_ref[b]
        qblk = jnp.where(own_half, jnp.concatenate([q8, q8], axis=-1), 0.0).astype(BF16)
        s = lax.dot_general(qblk, kk.astype(BF16), (((1,), (1,)), ((), ())),
                            preferred_element_type=F32) + brow
        m = jnp.maximum(jnp.max(s, axis=-1, keepdims=True), sink_col)
        p = jnp.exp(s - m)
        den = jnp.sum(p, axis=-1, keepdims=True) + jnp.exp(sink_col - m)
        o = jnp.dot((p / den).astype(BF16), vv.astype(BF16), preferred_element_type=F32)
        swa_scr[i * SAMPLE_BB + b] = jnp.where(own_half[:, :HEAD_DIM], o[:, :HEAD_DIM], o[:, HEAD_DIM:])

    @pl.when(i == pl.num_programs(0) - 1)
    def _():
        y = jnp.zeros(x_ref.shape, F32)
        for h in range(N_RET_HEADS):
            o = ret_scr[:, h, :]
            mu = jnp.mean(o, axis=-1, keepdims=True)
            c = o - mu
            var = jnp.mean(c * c, axis=-1, keepdims=True)
            r = c * lax.rsqrt(var + LN_EPS) * gain_ref[h:h + 1, :] * _silu(g3_ref[h])
            y = y + jnp.dot(r.astype(BF16), wout_ref[h * HEAD_DIM:(h + 1) * HEAD_DIM, :],
                            preferred_element_type=F32)
        for hh in range(N_SWA_HEADS):
            o = swa_scr[:, hh, :].astype(BF16)
            lo = RET_W + hh * HEAD_DIM
            y = y + jnp.dot(o, wout_ref[lo:lo + HEAD_DIM, :], preferred_element_type=F32)
        x1_ref[...] = _ln(DN_ALPHA * x_ref[...] + (1.0 + gate_ref[0]) * y, lng_ref[...], lnb_ref[...])


def _mix_sample(proj, x, mod, layer, w_out_bf, gamma1, bias_row, sinks, gain8, ln_g, ln_b,
                st, ck, cv):
    q, k, _, v, g, sq, skv = proj
    nb = x.shape[0]
    steps = nb // SAMPLE_BB
    to_cols = lambda a: a.astype(F32).reshape(steps, SAMPLE_BB, RET_W).transpose(0, 2, 1)
    qt, kt = to_cols(q), to_cols(k)
    v3 = v.astype(F32).reshape(nb, N_RET_HEADS, HEAD_DIM)
    sq3 = sq.astype(F32).reshape(nb, N_SWA_HEADS, HEAD_DIM)
    g3 = g.reshape(nb, N_RET_HEADS, HEAD_DIM).transpose(1, 0, 2)
    knew, vnew = skv[:, :SWA_KVW], skv[:, SWA_KVW:]
    st2 = st.reshape(nb, RET_W, HEAD_DIM)
    ck2 = ck.reshape(nb, CHUNK, SWA_KVW)
    cv2 = cv.reshape(nb, CHUNK, SWA_KVW)
    smem = pl.BlockSpec(memory_space=pltpu.SMEM)
    blk3 = lambda a, b, c: pl.BlockSpec((a, b, c), lambda i: (i, 0, 0))
    const2 = lambda a: pl.BlockSpec(a.shape, lambda i: (0, 0))
    const3 = lambda a: pl.BlockSpec(a.shape, lambda i: (0, 0, 0))
    x1, nst, nk, nv = pl.pallas_call(
        _mix_sample_kernel,
        grid=(steps,),
        in_specs=[smem, smem,
                  blk3(1, RET_W, SAMPLE_BB), blk3(1, RET_W, SAMPLE_BB),
                  blk3(SAMPLE_BB, N_RET_HEADS, HEAD_DIM), blk3(SAMPLE_BB, N_SWA_HEADS, HEAD_DIM),
                  pl.BlockSpec((SAMPLE_BB, SWA_KVW), lambda i: (i, 0)),
                  pl.BlockSpec((SAMPLE_BB, SWA_KVW), lambda i: (i, 0)),
                  blk3(SAMPLE_BB, RET_W, HEAD_DIM), blk3(SAMPLE_BB, CHUNK, SWA_KVW),
                  blk3(SAMPLE_BB, CHUNK, SWA_KVW),
                  const3(g3), const2(x),
                  pl.BlockSpec((1, nb, D_MODEL), lambda i: (layer, 0, 2)),
                  const2(w_out_bf), const2(bias_row), const2(gain8), const2(ln_g), const2(ln_b)],
        out_specs=[const2(x), blk3(SAMPLE_BB, RET_W, HEAD_DIM), blk3(SAMPLE_BB, CHUNK, SWA_KVW),
                   blk3(SAMPLE_BB, CHUNK, SWA_KVW)],
        out_shape=[jax.ShapeDtypeStruct(x.shape, F32), jax.ShapeDtypeStruct(st2.shape, F32),
                   jax.ShapeDtypeStruct(ck2.shape, F32), jax.ShapeDtypeStruct(cv2.shape, F32)],
        scratch_shapes=[pltpu.VMEM((nb, N_RET_HEADS, HEAD_DIM), F32),
                        pltpu.VMEM((nb, N_SWA_HEADS, HEAD_DIM), F32)],
        compiler_params=_cparams(("arbitrary",)),
        name="mix_sample",
    )(gamma1, sinks, qt, kt, v3, sq3, knew, vnew, st2, ck2, cv2, g3, x, mod,
      w_out_bf, bias_row, gain8, ln_g, ln_b)
    return x1, nst.reshape(st.shape), nk.reshape(ck.shape), nv.reshape(cv.shape)


def _router_kernel(x_ref, sh_ref, sc_ref, rwt_ref, rb_ref, tri_ref,
                   oi_ref, of_ref, cnt_ref, run_ref, *, mod_rows):
    i = pl.program_id(0)

    @pl.when(i == 0)
    def _():
        run_ref[...] = jnp.zeros_like(run_ref)

    sh = sh_ref[0][:mod_rows]
    sc = sc_ref[0][:mod_rows]
    h2 = x_ref[...] * (1.0 + sc) + sh
    logits = lax.dot_general(rwt_ref[...].astype(BF16), h2.astype(BF16), (((1,), (1,)), ((), ())),
                             preferred_element_type=F32)
    aff = 1.0 / (1.0 + jnp.exp(-logits))
    sel = aff + rb_ref[...]
    s = [sel[e:e + 1, :] for e in range(N_EXPERTS)]
    a = [aff[e:e + 1, :] for e in range(N_EXPERTS)]

    def top2sum(v0, v1, v2, v3):
        hi01, lo01 = jnp.maximum(v0, v1), jnp.minimum(v0, v1)
        hi23, lo23 = jnp.maximum(v2, v3), jnp.minimum(v2, v3)
        return jnp.maximum(hi01, hi23) + jnp.maximum(jnp.minimum(hi01, hi23),
                                                     jnp.maximum(lo01, lo23))

    def argmax_first(vals):
        best, idx = vals[0], jnp.zeros(vals[0].shape, I32)
        for j in range(1, len(vals)):
            upd = vals[j] > best
            idx = jnp.where(upd, j, idx)
            best = jnp.where(upd, vals[j], best)
        return idx

    def pick(idx, vals):
        out = vals[-1]
        for j in range(len(vals) - 2, -1, -1):
            out = jnp.where(idx == j, vals[j], out)
        return out

    gi = argmax_first([top2sum(*s[4 * g:4 * g + 4]) for g in range(N_GROUPS)])
    sv = [pick(gi, [s[4 * g + j] for g in range(N_GROUPS)]) for j in range(EXPERTS_PER_GROUP)]
    av = [pick(gi, [a[4 * g + j] for g in range(N_GROUPS)]) for j in range(EXPERTS_PER_GROUP)]
    i1 = argmax_first(sv)
    i2 = argmax_first([jnp.where(i1 == j, -jnp.inf, sv[j]) for j in range(EXPERTS_PER_GROUP)])
    w1, w2 = pick(i1, av), pick(i2, av)
    wsum = w1 + w2
    w1, w2 = w1 / wsum, w2 / wsum
    lo, hi = jnp.minimum(i1, i2), jnp.maximum(i1, i2)
    w_lo = jnp.where(i1 < i2, w1, w2)
    w_hi = jnp.where(i1 < i2, w2, w1)
    pair = jnp.where(lo == 0, hi - 1, jnp.where(lo == 1, hi + 1, 5))
    bin_id = gi * N_PAIRS + pair

    tm = bin_id.shape[1]
    onehot = lax.broadcasted_iota(I32, (BIN_ROWS, tm), 0) == bin_id
    oh_f = jnp.where(onehot, 1.0, 0.0)
    before = jnp.dot(oh_f.astype(BF16), tri_ref[...], preferred_element_type=F32)
    run = run_ref[...]
    run_t = jnp.concatenate([run] * (tm // 128), axis=-1)
    rank = jnp.sum(oh_f * (before + run_t), axis=0, keepdims=True)
    run_new = run + jnp.sum(oh_f, axis=1, keepdims=True)
    run_ref[...] = run_new
    cnt_ref[...] = run_new.astype(I32)

    zi = jnp.zeros_like(bin_id)
    oi_ref[0] = jnp.concatenate([gi * 4 + lo, gi * 4 + hi, bin_id, rank.astype(I32), zi, zi, zi, zi], axis=0)
    zf = jnp.zeros_like(w_lo)
    of_ref[0] = jnp.concatenate([w_lo, w_hi, zf, zf, zf, zf, zf, zf], axis=0)


def _router(x1, mod, layer, mod_row0, mod_rows, router_wt, router_b, tm):
    rows = x1.shape[0]
    nt = rows // tm
    mblk = 8 if mod_rows == 1 else mod_rows
    mrow = mod_row0 // mblk
    tri = jnp.asarray(np.triu(np.ones((tm, tm), np.float32), 1), BF16)
    rb = jnp.broadcast_to(router_b.astype(F32)[:, None], (N_EXPERTS, tm))
    mod_spec = lambda c: pl.BlockSpec((1, mblk, D_MODEL), lambda i: (layer, mrow, c))
    oi, of, cnt = pl.pallas_call(
        functools.partial(_router_kernel, mod_rows=mod_rows),
        grid=(nt,),
        in_specs=[pl.BlockSpec((tm, D_MODEL), lambda i: (i, 0)), mod_spec(3), mod_spec(4),
                  pl.BlockSpec((N_EXPERTS, D_MODEL), lambda i: (0, 0)),
                  pl.BlockSpec((N_EXPERTS, tm), lambda i: (0, 0)),
                  pl.BlockSpec((tm, tm), lambda i: (0, 0))],
        out_specs=[pl.BlockSpec((1, 8, tm), lambda i: (i, 0, 0)),
                   pl.BlockSpec((1, 8, tm), lambda i: (i, 0, 0)),
                   pl.BlockSpec((BIN_ROWS, 128), lambda i: (0, 0))],
        out_shape=[jax.ShapeDtypeStruct((nt, 8, tm), I32),
                   jax.ShapeDtypeStruct((nt, 8, tm), F32),
                   jax.ShapeDtypeStruct((BIN_ROWS, 128), I32)],
        scratch_shapes=[pltpu.VMEM((BIN_ROWS, 128), F32)],
        compiler_params=_cparams(("arbitrary",)),
        name="router",
    )(x1, mod, mod, router_wt, rb, tri)
    flat = lambda a, r: a[:, r, :].reshape(rows)
    return (flat(oi, 0), flat(oi, 1), flat(oi, 2), flat(oi, 3), flat(of, 0), flat(of, 1),
            cnt[:N_BINS, 0])


def _moe_sorted_kernel(ea_ref, eb_ref, nvalid_ref, inv_ref,
                       x_hbm, wrow_ref, sh_ref, sc_ref, gate_ref,
                       wga_ref, wua_ref, wda_ref, wgb_ref, wub_ref, wdb_ref,
                       lng_ref, lnb_ref,
                       out_hbm, xbuf, obuf, gsem, ssem):
    i = pl.program_id(0)
    nt = pl.num_programs(0)
    slot = i % 2

    def gather_copy(t, r, dst_slot):
        tok = jnp.maximum(inv_ref[t * MOE_TM + r], 0)
        return pltpu.make_async_copy(x_hbm.at[pl.ds(tok, 1)], xbuf.at[dst_slot, pl.ds(r, 1)],
                                     gsem.at[dst_slot])

    def start_gather(t, dst_slot):
        def body(r, carry):
            gather_copy(t, r, dst_slot).start()
            return carry
        lax.fori_loop(0, MOE_TM, body, 0)

    def wait_gather(t, dst_slot):
        def body(r, carry):
            gather_copy(t, r, dst_slot).wait()
            return carry
        lax.fori_loop(0, MOE_TM, body, 0)

    @pl.when(jnp.logical_and(i == 0, nvalid_ref[0] > 0))
    def _():
        start_gather(0, 0)

    @pl.when(nvalid_ref[i] > 0)
    def _():
        wait_gather(i, slot)

        @pl.when(jnp.logical_and(i + 1 < nt, nvalid_ref[jnp.minimum(i + 1, nt - 1)] > 0))
        def _():
            start_gather(i + 1, 1 - slot)

        x = xbuf[slot]
        h2 = (x * (1.0 + sc_ref[0][:1]) + sh_ref[0][:1]).astype(BF16)

        def expert(wg, wu, wd):
            a = _silu(jnp.dot(h2, wg[0], preferred_element_type=F32)) * \
                jnp.dot(h2, wu[0], preferred_element_type=F32)
            return jnp.dot(a.astype(BF16), wd[0], preferred_element_type=F32)

        wrow = wrow_ref[...]
        w_lo = wrow[:, :1]
        w_hi = wrow[:, HEAD_DIM:HEAD_DIM + 1]
        y = w_lo * expert(wga_ref, wua_ref, wda_ref) + w_hi * expert(wgb_ref, wub_ref, wdb_ref)
        obuf[...] = _ln(DN_ALPHA * x + (1.0 + gate_ref[0][:1]) * y, lng_ref[...], lnb_ref[...])

        def scatter_copy(r):
            tok = inv_ref[i * MOE_TM + r]
            return pltpu.make_async_copy(obuf.at[pl.ds(r, 1)], out_hbm.at[pl.ds(tok, 1)], ssem)

        def s_body(r, carry):
            scatter_copy(r).start()
            return carry
        lax.fori_loop(0, nvalid_ref[i], s_body, 0)

        def w_body(r, carry):
            scatter_copy(r).wait()
            return carry
        lax.fori_loop(0, nvalid_ref[i], w_body, 0)


def _moe_sorted(x1, mod, layer, mod_row0, route, wg_bf, wu_bf, wd_bf, ln_g, ln_b):
    ea, eb, bin_id, rank, w_lo, w_hi, counts = route
    rows = x1.shape[0]
    nt = rows // MOE_TM + N_BINS
    ns = nt * MOE_TM
    tiles = (counts + MOE_TM - 1) // MOE_TM
    tile_end = jnp.cumsum(tiles)
    row_start = (tile_end - tiles) * MOE_TM
    pos = row_start[bin_id] + rank
    inv = jnp.full((ns,), -1, I32).at[pos].set(jnp.arange(rows, dtype=I32))
    total = tile_end[-1]
    t = jnp.arange(nt, dtype=I32)
    tile_bin = jnp.searchsorted(tile_end, jnp.minimum(t, total - 1), side="right").astype(I32)
    tile_bin = jnp.clip(tile_bin, 0, N_BINS - 1)
    bin_lo = jnp.asarray([g * 4 + _PAIR_LO[p] for g in range(N_GROUPS) for p in range(N_PAIRS)], I32)
    bin_hi = jnp.asarray([g * 4 + _PAIR_HI[p] for g in range(N_GROUPS) for p in range(N_PAIRS)], I32)
    tile_ea, tile_eb = bin_lo[tile_bin], bin_hi[tile_bin]
    nvalid = jnp.clip(counts[tile_bin] - (t - (tile_end - tiles)[tile_bin]) * MOE_TM, 0, MOE_TM)
    nvalid = jnp.where(t < total, nvalid, 0).astype(I32)
    safe = jnp.maximum(inv, 0)
    wl = jnp.where(inv >= 0, w_lo[safe], 0.0)
    wh = jnp.where(inv >= 0, w_hi[safe], 0.0)
    wrow = jnp.concatenate([jnp.broadcast_to(wl[:, None], (ns, HEAD_DIM)),
                            jnp.broadcast_to(wh[:, None], (ns, HEAD_DIM))], axis=-1)
    mrow = mod_row0 // 8
    mod_spec = lambda c: pl.BlockSpec((1, 8, D_MODEL), lambda i, *_: (layer, mrow, c))
    w_spec = lambda shape, ref_idx: pl.BlockSpec(
        (1,) + shape, lambda i, ea_r, eb_r, v_r, inv_r: ((ea_r, eb_r)[ref_idx][i], 0, 0))
    any_spec = pl.BlockSpec(memory_space=pl.ANY)
    grid_spec = pltpu.PrefetchScalarGridSpec(
        num_scalar_prefetch=4,
        grid=(nt,),
        in_specs=[any_spec,
                  pl.BlockSpec((MOE_TM, 128), lambda i, *_: (i, 0)),
                  mod_spec(3), mod_spec(4), mod_spec(5),
                  w_spec((D_MODEL, D_FF), 0), w_spec((D_MODEL, D_FF), 0), w_spec((D_FF, D_MODEL), 0),
                  w_spec((D_MODEL, D_FF), 1), w_spec((D_MODEL, D_FF), 1), w_spec((D_FF, D_MODEL), 1),
                  pl.BlockSpec((1, D_MODEL), lambda i, *_: (0, 0)),
                  pl.BlockSpec((1, D_MODEL), lambda i, *_: (0, 0))],
        out_specs=any_spec,
        scratch_shapes=[pltpu.VMEM((2, MOE_TM, D_MODEL), F32),
                        pltpu.VMEM((MOE_TM, D_MODEL), F32),
                        pltpu.SemaphoreType.DMA((2,)),
                        pltpu.SemaphoreType.DMA(())],
    )
    return pl.pallas_call(
        _moe_sorted_kernel,
        grid_spec=grid_spec,
        out_shape=jax.ShapeDtypeStruct((rows, D_MODEL), F32),
        compiler_params=_cparams(("arbitrary",)),
        name="moe_sorted",
    )(tile_ea, tile_eb, nvalid, inv, x1, wrow, mod, mod, mod,
      wg_bf, wu_bf, wd_bf, wg_bf, wu_bf, wd_bf, ln_g, ln_b)


def _moe_dense_kernel(x_ref, dw_ref, sh_ref, sc_ref, gate_ref, wg_ref, wu_ref, wd_ref,
                      lng_ref, lnb_ref, o_ref, acc):
    e = pl.program_id(0)

    @pl.when(e == 0)
    def _():
        acc[...] = jnp.zeros_like(acc)

    x = x_ref[...]
    h2 = (x * (1.0 + sc_ref[0]) + sh_ref[0]).astype(BF16)
    a = _silu(jnp.dot(h2, wg_ref[0], preferred_element_type=F32)) * \
        jnp.dot(h2, wu_ref[0], preferred_element_type=F32)
    y = jnp.dot(a.astype(BF16), wd_ref[0], preferred_element_type=F32)
    acc[...] = acc[...] + dw_ref[0][:, :1] * y

    @pl.when(e == pl.num_programs(0) - 1)
    def _():
        o_ref[...] = _ln(DN_ALPHA * x + (1.0 + gate_ref[0]) * acc[...], lng_ref[...], lnb_ref[...])


def _moe_dense(x1, mod, layer, route, wg_bf, wu_bf, wd_bf, ln_g, ln_b):
    ea, eb, _, _, w_lo, w_hi, _ = route
    nb = x1.shape[0]
    eids = jnp.arange(N_EXPERTS, dtype=I32)[:, None]
    dw = jnp.where(eids == ea[None, :], w_lo[None, :], 0.0) + \
        jnp.where(eids == eb[None, :], w_hi[None, :], 0.0)
    dw = jnp.broadcast_to(dw[:, :, None], (N_EXPERTS, nb, 128))
    mod_spec = lambda c: pl.BlockSpec((1, nb, D_MODEL), lambda e: (layer, 0, c))
    const2 = lambda a: pl.BlockSpec(a.shape, lambda e: (0, 0))
    return pl.pallas_call(
        _moe_dense_kernel,
        grid=(N_EXPERTS,),
        in_specs=[const2(x1), pl.BlockSpec((1, nb, 128), lambda e: (e, 0, 0)),
                  mod_spec(3), mod_spec(4), mod_spec(5),
                  pl.BlockSpec((1, D_MODEL, D_FF), lambda e: (e, 0, 0)),
                  pl.BlockSpec((1, D_MODEL, D_FF), lambda e: (e, 0, 0)),
                  pl.BlockSpec((1, D_FF, D_MODEL), lambda e: (e, 0, 0)),
                  const2(ln_g), const2(ln_b)],
        out_specs=const2(x1),
        out_shape=jax.ShapeDtypeStruct(x1.shape, F32),
        scratch_shapes=[pltpu.VMEM(x1.shape, F32)],
        compiler_params=_cparams(("arbitrary",)),
        name="moe_dense",
    )(x1, dw, mod, mod, mod, wg_bf, wu_bf, wd_bf, ln_g, ln_b)


def kernel(x_prompt, x_sample, state_ret, cache_swa_k, cache_swa_v, c_prompt, c_sample, w_in, w_out, ret_gn_gain, swa_sinks, rel_bias_table, ada_w, ada_b, ln1_g, ln1_b, ln2_g, ln2_b, router_w, router_b, exp_w_gate, exp_w_up, exp_w_down):
    seq = x_prompt.shape[1]
    nb = x_sample.shape[0]
    past_len = 16384
    assert x_prompt.shape[0] == 1 and x_sample.shape[1] == 1

    c_all = jnp.concatenate([c_sample, c_prompt, jnp.zeros((7, D_MODEL), F32)], axis=0)
    mod = _ada(c_all, ada_w, ada_b)
    prompt_row = nb

    bias_tabs = _bias_tables(rel_bias_table.astype(F32))
    gl, decay, xi, zeta, gamma1 = _ret_consts()
    rope_p = _rope_tables(jnp.arange(seq, dtype=I32))
    rope_s = _rope_tables(jnp.full((nb,), past_len, I32))
    router_wt = router_w.astype(F32).T
    zeta_p = jnp.tile(zeta, (INPROJ_TM // CHUNK, 1))
    zeta_s = jnp.ones((nb, RET_W), F32)

    xp = x_prompt.reshape(seq, D_MODEL)
    xs = x_sample.reshape(nb, D_MODEL)
    st_p, k_p, v_p, st_s, k_s, v_s = [], [], [], [], [], []
    for l in range(DEPTH):
        w_in_bf = w_in[l].astype(BF16)
        w_out_bf = w_out[l].astype(BF16)
        wg_bf = exp_w_gate[l].astype(BF16)
        wu_bf = exp_w_up[l].astype(BF16)
        wd_bf = exp_w_down[l].astype(BF16)
        gain = ret_gn_gain[l].astype(F32)
        sinks = swa_sinks[l].astype(F32)
        row = lambda a: a[l].astype(F32).reshape(1, D_MODEL)

        proj = _inproj(xp, mod, l, prompt_row, 1, w_in_bf, rope_p, zeta_p, INPROJ_TM)
        x1, st = _mix_prompt(proj, xp, mod, l, prompt_row, w_out_bf, (gl, decay, xi),
                             bias_tabs, sinks, gain.reshape(1, RET_W), row(ln1_g), row(ln1_b))
        skv = proj[6]
        st_p.append(st.reshape(1, N_RET_HEADS, HEAD_DIM, HEAD_DIM))
        k_p.append(skv[seq - CHUNK:, :SWA_KVW].reshape(1, CHUNK, N_KV_HEADS, HEAD_DIM))
        v_p.append(skv[seq - CHUNK:, SWA_KVW:].reshape(1, CHUNK, N_KV_HEADS, HEAD_DIM))
        route = _router(x1, mod, l, prompt_row, 1, router_wt, router_b, 512)
        xp = _moe_sorted(x1, mod, l, prompt_row, route, wg_bf, wu_bf, wd_bf, row(ln2_g), row(ln2_b))

        proj = _inproj(xs, mod, l, 0, nb, w_in_bf, rope_s, zeta_s, nb)
        x1, nst, nk, nv = _mix_sample(proj, xs, mod, l, w_out_bf, gamma1, bias_tabs[2], sinks, gain,
                                      row(ln1_g), row(ln1_b), state_ret[l].astype(F32),
                                      cache_swa_k[l], cache_swa_v[l])
        st_s.append(nst)
        k_s.append(nk)
        v_s.append(nv)
        route = _router(x1, mod, l, 0, nb, router_wt, router_b, nb)
        xs = _moe_dense(x1, mod, l, route, wg_bf, wu_bf, wd_bf, row(ln2_g), row(ln2_b))

    return (xp.reshape(1, seq, D_MODEL), xs.reshape(nb, 1, D_MODEL),
            jnp.stack(st_p), jnp.stack(k_p), jnp.stack(v_p),
            jnp.stack(st_s), jnp.stack(k_s), jnp.stack(v_s))
```

```python
import functools
import math

import numpy as np
import jax
import jax.numpy as jnp
from jax import lax
from jax.experimental import pallas as pl
from jax.experimental.pallas import tpu as pltpu

F32 = jnp.float32
BF16 = jnp.bfloat16
I32 = jnp.int32

D_MODEL = 1024
DEPTH = 2
HEAD_DIM = 64
N_RET_HEADS = 8
N_SWA_HEADS = 8
N_KV_HEADS = 2
GQA_GROUP = N_SWA_HEADS // N_KV_HEADS
RET_W = N_RET_HEADS * HEAD_DIM
SWA_QW = N_SWA_HEADS * HEAD_DIM
SWA_KVW = N_KV_HEADS * HEAD_DIM
PROJ_W = 4 * RET_W + SWA_QW + 2 * SWA_KVW
CHUNK = 128
ROPE_BASE = 10000.0
N_BUCKETS = 32
MAX_DISTANCE = 128
N_EXPERTS = 16
N_GROUPS = 4
EXPERTS_PER_GROUP = 4
D_FF = 512
LN_EPS = 1e-5
DN_ALPHA = (2 * DEPTH) ** 0.25
QK_SCALE = HEAD_DIM ** -0.5
NEG_BIG = -1e30

N_PAIRS = 6
N_BINS = N_GROUPS * N_PAIRS
BIN_ROWS = 32
MOE_TM = 256
INPROJ_TM = 512
VMEM_LIMIT = 56 * 1024 * 1024

_PAIR_LO = (0, 0, 0, 1, 1, 2)
_PAIR_HI = (1, 2, 3, 2, 3, 3)


def _cparams(sem):
    return pltpu.CompilerParams(dimension_semantics=sem, vmem_limit_bytes=VMEM_LIMIT)


def _ln(v, g, b):
    mu = jnp.mean(v, axis=-1, keepdims=True)
    c = v - mu
    var = jnp.mean(c * c, axis=-1, keepdims=True)
    return c * lax.rsqrt(var + LN_EPS) * g + b


def _silu(v):
    return v * (1.0 / (1.0 + jnp.exp(-v)))


def _ada_kernel(c_ref, w_ref, b_ref, o_ref):
    o_ref[0] = jnp.dot(c_ref[...].astype(BF16), w_ref[0].astype(BF16),
                       preferred_element_type=F32) + b_ref[0]


def _ada(c_all, ada_w, ada_b):
    rows = c_all.shape[0]
    nt = 6 * D_MODEL // 1024
    return pl.pallas_call(
        _ada_kernel,
        grid=(DEPTH, nt),
        in_specs=[pl.BlockSpec((rows, D_MODEL), lambda l, j: (0, 0)),
                  pl.BlockSpec((1, D_MODEL, 1024), lambda l, j: (l, 0, j)),
                  pl.BlockSpec((1, 1, 1024), lambda l, j: (l, 0, j))],
        out_specs=pl.BlockSpec((1, rows, 1024), lambda l, j: (l, 0, j)),
        out_shape=jax.ShapeDtypeStruct((DEPTH, rows, 6 * D_MODEL), F32),
        compiler_params=_cparams(("arbitrary", "arbitrary")),
        name="ada",
    )(c_all, ada_w, ada_b.reshape(DEPTH, 1, 6 * D_MODEL))


def _bias_kernel(tab_ref, bkt_ref, full_ref, first_ref, row_ref):
    bkt = bkt_ref[...]
    col = lax.broadcasted_iota(I32, bkt.shape, 1)
    for h in range(N_SWA_HEADS):
        acc = jnp.full(bkt.shape, NEG_BIG, F32)
        for b in range(N_BUCKETS):
            acc = jnp.where(bkt == b, tab_ref[b, h], acc)
        full_ref[h] = acc
        first_ref[h] = jnp.where(col < CHUNK, NEG_BIG, acc)
        row_ref[h:h + 1, :] = acc[CHUNK - 1:CHUNK, CHUNK:]


def _t5_bucket(rel):
    max_exact = N_BUCKETS // 2
    relf = jnp.maximum(rel, 1).astype(F32)
    large = max_exact + (jnp.log(relf / max_exact) / math.log(MAX_DISTANCE / max_exact)
                         * (N_BUCKETS - max_exact)).astype(I32)
    large = jnp.minimum(large, N_BUCKETS - 1)
    return jnp.where(rel < max_exact, rel, large)


def _bias_tables(rel_bias_table):
    qi = jnp.arange(CHUNK)
    si = jnp.arange(2 * CHUNK)
    rel = CHUNK + qi[:, None] - si[None, :]
    band = (rel >= 0) & (rel < CHUNK)
    bkt = jnp.where(band, _t5_bucket(jnp.maximum(rel, 0)), -1).astype(I32)
    return pl.pallas_call(
        _bias_kernel,
        in_specs=[pl.BlockSpec(memory_space=pltpu.SMEM),
                  pl.BlockSpec((CHUNK, 2 * CHUNK), lambda: (0, 0))],
        out_specs=[pl.BlockSpec((N_SWA_HEADS, CHUNK, 2 * CHUNK), lambda: (0, 0, 0)),
                   pl.BlockSpec((N_SWA_HEADS, CHUNK, 2 * CHUNK), lambda: (0, 0, 0)),
                   pl.BlockSpec((N_SWA_HEADS, CHUNK), lambda: (0, 0))],
        out_shape=[jax.ShapeDtypeStruct((N_SWA_HEADS, CHUNK, 2 * CHUNK), F32),
                   jax.ShapeDtypeStruct((N_SWA_HEADS, CHUNK, 2 * CHUNK), F32),
                   jax.ShapeDtypeStruct((N_SWA_HEADS, CHUNK), F32)],
        name="t5_bias",
    )(rel_bias_table, bkt)


def _rotary(v, cos, s_lo, s_hi):
    outs = []
    for j in range(RET_W // 128):
        blk = v[:, j * 128:(j + 1) * 128]
        outs.append(blk * cos + pltpu.roll(blk, 96, 1) * s_lo + pltpu.roll(blk, 32, 1) * s_hi)
    return jnp.concatenate(outs, axis=-1)


def _inproj_kernel(x_ref, sh_ref, sc_ref, w_ref, cos_ref, slo_ref, shi_ref, zeta_ref,
                   q_ref, k_ref, kz_ref, v_ref, g_ref, sq_ref, skv_ref, *, mod_rows):
    sh = sh_ref[0][:mod_rows]
    sc = sc_ref[0][:mod_rows]
    h = (x_ref[...] * (1.0 + sc) + sh).astype(BF16)
    cos, s_lo, s_hi = cos_ref[...], slo_ref[...], shi_ref[...]

    def proj(lo, hi):
        return jnp.dot(h, w_ref[:, lo:hi], preferred_element_type=F32)

    q_ref[...] = _rotary(proj(0, RET_W), cos, s_lo, s_hi).astype(BF16)
    k = _rotary(proj(RET_W, 2 * RET_W), cos, s_lo, s_hi) * QK_SCALE
    k_ref[...] = k.astype(BF16)
    kz_ref[...] = (k * zeta_ref[...]).astype(BF16)
    v_ref[...] = proj(2 * RET_W, 3 * RET_W).astype(BF16)
    g_ref[...] = proj(3 * RET_W, 4 * RET_W)
    sq_ref[...] = (proj(4 * RET_W, 4 * RET_W + SWA_QW) * QK_SCALE).astype(BF16)
    skv_ref[...] = proj(4 * RET_W + SWA_QW, PROJ_W)


def _inproj(x, mod, layer, mod_row0, mod_rows, w_in_bf, rope, zeta_tile, tm):
    rows = x.shape[0]
    mblk = 8 if mod_rows == 1 else mod_rows
    mrow = mod_row0 // mblk
    cos, s_lo, s_hi = rope
    row_spec = lambda w: pl.BlockSpec((tm, w), lambda i: (i, 0))
    mod_spec = lambda c: pl.BlockSpec((1, mblk, D_MODEL), lambda i: (layer, mrow, c))
    return pl.pallas_call(
        functools.partial(_inproj_kernel, mod_rows=mod_rows),
        grid=(rows // tm,),
        in_specs=[row_spec(D_MODEL), mod_spec(0), mod_spec(1),
                  pl.BlockSpec((D_MODEL, PROJ_W), lambda i: (0, 0)),
                  row_spec(128), row_spec(128), row_spec(128),
                  pl.BlockSpec((tm, RET_W), lambda i: (0, 0))],
        out_specs=[row_spec(RET_W), row_spec(RET_W), row_spec(RET_W), row_spec(RET_W), row_spec(RET_W),
                   row_spec(SWA_QW), row_spec(2 * SWA_KVW)],
        out_shape=[jax.ShapeDtypeStruct((rows, RET_W), BF16),
                   jax.ShapeDtypeStruct((rows, RET_W), BF16),
                   jax.ShapeDtypeStruct((rows, RET_W), BF16),
                   jax.ShapeDtypeStruct((rows, RET_W), BF16),
                   jax.ShapeDtypeStruct((rows, RET_W), F32),
                   jax.ShapeDtypeStruct((rows, SWA_QW), BF16),
                   jax.ShapeDtypeStruct((rows, 2 * SWA_KVW), F32)],
        compiler_params=_cparams(("arbitrary",)),
        name="inproj",
    )(x, mod, mod, w_in_bf, cos, s_lo, s_hi, zeta_tile)


def _rope_tables(pos):
    half = HEAD_DIM // 2
    inv = ROPE_BASE ** (-jnp.arange(half, dtype=F32) / half)
    ang = pos.astype(F32)[:, None] * inv[None, :]
    cos, sin = jnp.cos(ang), jnp.sin(ang)
    zero = jnp.zeros_like(sin)
    cos_t = jnp.concatenate([cos, cos, cos, cos], axis=-1)
    s_lo = jnp.concatenate([-sin, zero, -sin, zero], axis=-1)
    s_hi = jnp.concatenate([zero, sin, zero, sin], axis=-1)
    return cos_t, s_lo, s_hi


def _mix_prompt_kernel(gl_ref, sink_ref,
                       q_ref, k_ref, kz_ref, v_ref, g_ref, sq_ref, kvc_ref, kvp_ref, x_ref, gate_ref,
                       wout_ref, decay_ref, xi_ref, bias_ref, bias0_ref,
                       gain_ref, lng_ref, lnb_ref,
                       x1_ref, st_ref, state):
    i = pl.program_id(0)

    @pl.when(i == 0)
    def _():
        state[...] = jnp.zeros_like(state)

    q = q_ref[...]
    k = k_ref[...]
    v = v_ref[...]
    kz = kz_ref[...]

    intra, inter = [], []
    for h in range(N_RET_HEADS):
        sl = slice(h * HEAD_DIM, (h + 1) * HEAD_DIM)
        qh, kh, vh = q[:, sl], k[:, sl], v[:, sl]
        s = lax.dot_general(qh, kh, (((1,), (1,)), ((), ())), preferred_element_type=F32)
        p = (s * decay_ref[h]).astype(BF16)
        intra.append(jnp.dot(p, vh, preferred_element_type=F32))
        s_old = state[h]
        inter.append(jnp.dot(qh, s_old.astype(BF16), preferred_element_type=F32))
        upd = lax.dot_general(kz[:, sl], vh, (((0,), (0,)), ((), ())), preferred_element_type=F32)
        state[h] = gl_ref[h] * s_old + upd

    xi = xi_ref[...]
    g = g_ref[...]
    ret_parts = []
    for h in range(N_RET_HEADS):
        sl = slice(h * HEAD_DIM, (h + 1) * HEAD_DIM)
        o = intra[h] + inter[h] * xi[:, sl]
        mu = jnp.mean(o, axis=-1, keepdims=True)
        c = o - mu
        var = jnp.mean(c * c, axis=-1, keepdims=True)
        ret_parts.append(c * lax.rsqrt(var + LN_EPS))
    ret_o = jnp.concatenate(ret_parts, axis=-1) * gain_ref[...] * _silu(g)

    sq = sq_ref[...]
    kvc = kvc_ref[...]
    kvp = kvp_ref[...]
    swa_parts = []
    for kv in range(N_KV_HEADS):
        ks = slice(kv * HEAD_DIM, (kv + 1) * HEAD_DIM)
        vs = slice(SWA_KVW + kv * HEAD_DIM, SWA_KVW + (kv + 1) * HEAD_DIM)
        kk = jnp.concatenate([kvp[:, ks], kvc[:, ks]], axis=0).astype(BF16)
        vv = jnp.concatenate([kvp[:, vs], kvc[:, vs]], axis=0).astype(BF16)
        for j in range(GQA_GROUP):
            hh = kv * GQA_GROUP + j
            qh = sq[:, hh * HEAD_DIM:(hh + 1) * HEAD_DIM]
            s = lax.dot_general(qh, kk, (((1,), (1,)), ((), ())), preferred_element_type=F32)
            s = s + jnp.where(i == 0, bias0_ref[hh], bias_ref[hh])
            sink = sink_ref[hh]
            m = jnp.maximum(jnp.max(s, axis=-1, keepdims=True), sink)
            p = jnp.exp(s - m)
            den = jnp.sum(p, axis=-1, keepdims=True) + jnp.exp(sink - m)
            swa_parts.append(jnp.dot((p / den).astype(BF16), vv, preferred_element_type=F32))
    swa_o = jnp.concatenate(swa_parts, axis=-1)

    mixed = jnp.concatenate([ret_o, swa_o], axis=-1).astype(BF16)
    y = jnp.dot(mixed, wout_ref[...], preferred_element_type=F32)
    gate = gate_ref[0][:1]
    x1_ref[...] = _ln(DN_ALPHA * x_ref[...] + (1.0 + gate) * y, lng_ref[...], lnb_ref[...])

    @pl.when(i == pl.num_programs(0) - 1)
    def _():
        st_ref[...] = state[...]


def _mix_prompt(proj, x, mod, layer, mod_row0, w_out_bf, ret_consts, bias_tabs, sinks,
                gain, ln_g, ln_b):
    q, k, kz, v, g, sq, skv = proj
    rows = x.shape[0]
    nblk = rows // CHUNK
    gl, decay, xi = ret_consts
    bias_full, bias_first, _ = bias_tabs
    mrow = mod_row0 // 8
    row_spec = lambda w: pl.BlockSpec((CHUNK, w), lambda i: (i, 0))
    const2 = lambda a: pl.BlockSpec(a.shape, lambda i: (0, 0))
    const3 = lambda a: pl.BlockSpec(a.shape, lambda i: (0, 0, 0))
    smem = pl.BlockSpec(memory_space=pltpu.SMEM)
    return pl.pallas_call(
        _mix_prompt_kernel,
        grid=(nblk,),
        in_specs=[smem, smem,
                  row_spec(RET_W), row_spec(RET_W), row_spec(RET_W), row_spec(RET_W), row_spec(RET_W),
                  row_spec(SWA_QW), row_spec(2 * SWA_KVW),
                  pl.BlockSpec((CHUNK, 2 * SWA_KVW), lambda i: (jnp.maximum(i - 1, 0), 0)),
                  row_spec(D_MODEL),
                  pl.BlockSpec((1, 8, D_MODEL), lambda i: (layer, mrow, 2)),
                  const2(w_out_bf), const3(decay), const2(xi),
                  const3(bias_full), const3(bias_first),
                  const2(gain), const2(ln_g), const2(ln_b)],
        out_specs=[row_spec(D_MODEL),
                   pl.BlockSpec((N_RET_HEADS, HEAD_DIM, HEAD_DIM), lambda i: (0, 0, 0))],
        out_shape=[jax.ShapeDtypeStruct((rows, D_MODEL), F32),
                   jax.ShapeDtypeStruct((N_RET_HEADS, HEAD_DIM, HEAD_DIM), F32)],
        scratch_shapes=[pltpu.VMEM((N_RET_HEADS, HEAD_DIM, HEAD_DIM), F32)],
        compiler_params=_cparams(("arbitrary",)),
        name="mix_prompt",
    )(gl, sinks, q, k, kz, v, g, sq, skv, skv, x, mod, w_out_bf, decay, xi,
      bias_full, bias_first, gain, ln_g, ln_b)


def _ret_consts():
    lg = jnp.log(1.0 - 2.0 ** (-5.0 - jnp.arange(N_RET_HEADS, dtype=F32)))
    idx = jnp.arange(CHUNK, dtype=F32)
    diff = idx[:, None] - idx[None, :]
    decay = jnp.where(diff >= 0, jnp.exp(jnp.maximum(diff, 0.0)[None] * lg[:, None, None]), 0.0)
    xi = jnp.exp((idx + 1.0)[:, None] * lg[None, :])
    zeta = jnp.exp((CHUNK - 1.0 - idx)[:, None] * lg[None, :])
    gl = jnp.exp(CHUNK * lg)
    expand = lambda a: jnp.repeat(a, HEAD_DIM, axis=1)
    return gl, decay, expand(xi), expand(zeta), jnp.exp(1.0 * lg)


SAMPLE_BB = 8


def _mix_sample_kernel(g1_ref, sink_ref,
                       qt_ref, kt_ref, v3_ref, sq3_ref, knew_ref, vnew_ref,
                       st_ref, ck_ref, cv_ref, g3_ref, x_ref, gate_ref,
                       wout_ref, brow_ref, gain_ref, lng_ref, lnb_ref,
                       x1_ref, nst_ref, nk_ref, nv_ref, ret_scr, swa_scr):
    i = pl.program_id(0)
    row = lax.broadcasted_iota(I32, (N_SWA_HEADS, 2 * HEAD_DIM), 0)
    lane = lax.broadcasted_iota(I32, (N_SWA_HEADS, 2 * HEAD_DIM), 1)
    own_half = (row // GQA_GROUP) == (lane // HEAD_DIM)
    sink_col = jnp.concatenate(
        [jnp.full((1, 1), sink_ref[hh], F32) for hh in range(N_SWA_HEADS)], axis=0)
    brow = brow_ref[...]
    qt = qt_ref[0]
    kt = kt_ref[0]

    for b in range(SAMPLE_BB):
        o_rows = []
        for h in range(N_RET_HEADS):
            rs = slice(h * HEAD_DIM, (h + 1) * HEAD_DIM)
            s_old = st_ref[b, rs, :]
            kcol = kt[rs, b:b + 1]
            qcol = qt[rs, b:b + 1]
            vrow = v3_ref[b, h:h + 1, :]
            s_new = g1_ref[h] * s_old + kcol * vrow
            nst_ref[b, rs, :] = s_new
            o_rows.append(jnp.sum(qcol * s_new, axis=0, keepdims=True))
        ret_scr[i * SAMPLE_BB + b] = jnp.concatenate(o_rows, axis=0)

        kk = jnp.concatenate([ck_ref[b, 1:, :], knew_ref[b:b + 1, :]], axis=0)
        vv = jnp.concatenate([cv_ref[b, 1:, :], vnew_ref[b:b + 1, :]], axis=0)
        nk_ref[b] = kk
        nv_ref[b] = vv
        q8 = sq3_ref[b]
        qblk = jnp.where(own_half, jnp.concatenate([q8, q8], axis=-1), 0.0).astype(BF16)
        s = lax.dot_general(qblk, kk.astype(BF16), (((1,), (1,)), ((), ())),
                            preferred_element_type=F32) + brow
        m = jnp.maximum(jnp.max(s, axis=-1, keepdims=True), sink_col)
        p = jnp.exp(s - m)
        den = jnp.sum(p, axis=-1, keepdims=True) + jnp.exp(sink_col - m)
        o = jnp.dot((p / den).astype(BF16), vv.astype(BF16), preferred_element_type=F32)
        swa_scr[i * SAMPLE_BB + b] = jnp.where(own_half[:, :HEAD_DIM], o[:, :HEAD_DIM], o[:, HEAD_DIM:])

    @pl.when(i == pl.num_programs(0) - 1)
    def _():
        y = jnp.zeros(x_ref.shape, F32)
        for h in range(N_RET_HEADS):
            o = ret_scr[:, h, :]
            mu = jnp.mean(o, axis=-1, keepdims=True)
            c = o - mu
            var = jnp.mean(c * c, axis=-1, keepdims=True)
            r = c * lax.rsqrt(var + LN_EPS) * gain_ref[h:h + 1, :] * _silu(g3_ref[h])
            y = y + jnp.dot(r.astype(BF16), wout_ref[h * HEAD_DIM:(h + 1) * HEAD_DIM, :],
                            preferred_element_type=F32)
        for hh in range(N_SWA_HEADS):
            o = swa_scr[:, hh, :].astype(BF16)
            lo = RET_W + hh * HEAD_DIM
            y = y + jnp.dot(o, wout_ref[lo:lo + HEAD_DIM, :], preferred_element_type=F32)
        x1_ref[...] = _ln(DN_ALPHA * x_ref[...] + (1.0 + gate_ref[0]) * y, lng_ref[...], lnb_ref[...])


def _mix_sample(proj, x, mod, layer, w_out_bf, gamma1, bias_row, sinks, gain8, ln_g, ln_b,
                st, ck, cv):
    q, k, _, v, g, sq, skv = proj
    nb = x.shape[0]
    steps = nb // SAMPLE_BB
    to_cols = lambda a: a.astype(F32).reshape(steps, SAMPLE_BB, RET_W).transpose(0, 2, 1)
    qt, kt = to_cols(q), to_cols(k)
    v3 = v.astype(F32).reshape(nb, N_RET_HEADS, HEAD_DIM)
    sq3 = sq.astype(F32).reshape(nb, N_SWA_HEADS, HEAD_DIM)
    g3 = g.reshape(nb, N_RET_HEADS, HEAD_DIM).transpose(1, 0, 2)
    knew, vnew = skv[:, :SWA_KVW], skv[:, SWA_KVW:]
    st2 = st.reshape(nb, RET_W, HEAD_DIM)
    ck2 = ck.reshape(nb, CHUNK, SWA_KVW)
    cv2 = cv.reshape(nb, CHUNK, SWA_KVW)
    smem = pl.BlockSpec(memory_space=pltpu.SMEM)
    blk3 = lambda a, b, c: pl.BlockSpec((a, b, c), lambda i: (i, 0, 0))
    const2 = lambda a: pl.BlockSpec(a.shape, lambda i: (0, 0))
    const3 = lambda a: pl.BlockSpec(a.shape, lambda i: (0, 0, 0))
    x1, nst, nk, nv = pl.pallas_call(
        _mix_sample_kernel,
        grid=(steps,),
        in_specs=[smem, smem,
                  blk3(1, RET_W, SAMPLE_BB), blk3(1, RET_W, SAMPLE_BB),
                  blk3(SAMPLE_BB, N_RET_HEADS, HEAD_DIM), blk3(SAMPLE_BB, N_SWA_HEADS, HEAD_DIM),
                  pl.BlockSpec((SAMPLE_BB, SWA_KVW), lambda i: (i, 0)),
                  pl.BlockSpec((SAMPLE_BB, SWA_KVW), lambda i: (i, 0)),
                  blk3(SAMPLE_BB, RET_W, HEAD_DIM), blk3(SAMPLE_BB, CHUNK, SWA_KVW),
                  blk3(SAMPLE_BB, CHUNK, SWA_KVW),
                  const3(g3), const2(x),
                  pl.BlockSpec((1, nb, D_MODEL), lambda i: (layer, 0, 2)),
                  const2(w_out_bf), const2(bias_row), const2(gain8), const2(ln_g), const2(ln_b)],
        out_specs=[const2(x), blk3(SAMPLE_BB, RET_W, HEAD_DIM), blk3(SAMPLE_BB, CHUNK, SWA_KVW),
                   blk3(SAMPLE_BB, CHUNK, SWA_KVW)],
        out_shape=[jax.ShapeDtypeStruct(x.shape, F32), jax.ShapeDtypeStruct(st2.shape, F32),
                   jax.ShapeDtypeStruct(ck2.shape, F32), jax.ShapeDtypeStruct(cv2.shape, F32)],
        scratch_shapes=[pltpu.VMEM((nb, N_RET_HEADS, HEAD_DIM), F32),
                        pltpu.VMEM((nb, N_SWA_HEADS, HEAD_DIM), F32)],
        compiler_params=_cparams(("arbitrary",)),
        name="mix_sample",
    )(gamma1, sinks, qt, kt, v3, sq3, knew, vnew, st2, ck2, cv2, g3, x, mod,
      w_out_bf, bias_row, gain8, ln_g, ln_b)
    return x1, nst.reshape(st.shape), nk.reshape(ck.shape), nv.reshape(cv.shape)


def _router_kernel(x_ref, sh_ref, sc_ref, rwt_ref, rb_ref, tri_ref,
                   oi_ref, of_ref, cnt_ref, run_ref, *, mod_rows):
    i = pl.program_id(0)

    @pl.when(i == 0)
    def _():
        run_ref[...] = jnp.zeros_like(run_ref)

    sh = sh_ref[0][:mod_rows]
    sc = sc_ref[0][:mod_rows]
    h2 = x_ref[...] * (1.0 + sc) + sh
    logits = lax.dot_general(rwt_ref[...].astype(BF16), h2.astype(BF16), (((1,), (1,)), ((), ())),
                             preferred_element_type=F32)
    aff = 1.0 / (1.0 + jnp.exp(-logits))
    sel = aff + rb_ref[...]
    s = [sel[e:e + 1, :] for e in range(N_EXPERTS)]
    a = [aff[e:e + 1, :] for e in range(N_EXPERTS)]

    def top2sum(v0, v1, v2, v3):
        hi01, lo01 = jnp.maximum(v0, v1), jnp.minimum(v0, v1)
        hi23, lo23 = jnp.maximum(v2, v3), jnp.minimum(v2, v3)
        return jnp.maximum(hi01, hi23) + jnp.maximum(jnp.minimum(hi01, hi23),
                                                     jnp.maximum(lo01, lo23))

    def argmax_first(vals):
        best, idx = vals[0], jnp.zeros(vals[0].shape, I32)
        for j in range(1, len(vals)):
            upd = vals[j] > best
            idx = jnp.where(upd, j, idx)
            best = jnp.where(upd, vals[j], best)
        return idx

    def pick(idx, vals):
        out = vals[-1]
        for j in range(len(vals) - 2, -1, -1):
            out = jnp.where(idx == j, vals[j], out)
        return out

    gi = argmax_first([top2sum(*s[4 * g:4 * g + 4]) for g in range(N_GROUPS)])
    sv = [pick(gi, [s[4 * g + j] for g in range(N_GROUPS)]) for j in range(EXPERTS_PER_GROUP)]
    av = [pick(gi, [a[4 * g + j] for g in range(N_GROUPS)]) for j in range(EXPERTS_PER_GROUP)]
    i1 = argmax_first(sv)
    i2 = argmax_first([jnp.where(i1 == j, -jnp.inf, sv[j]) for j in range(EXPERTS_PER_GROUP)])
    w1, w2 = pick(i1, av), pick(i2, av)
    wsum = w1 + w2
    w1, w2 = w1 / wsum, w2 / wsum
    lo, hi = jnp.minimum(i1, i2), jnp.maximum(i1, i2)
    w_lo = jnp.where(i1 < i2, w1, w2)
    w_hi = jnp.where(i1 < i2, w2, w1)
    pair = jnp.where(lo == 0, hi - 1, jnp.where(lo == 1, hi + 1, 5))
    bin_id = gi * N_PAIRS + pair

    tm = bin_id.shape[1]
    onehot = lax.broadcasted_iota(I32, (BIN_ROWS, tm), 0) == bin_id
    oh_f = jnp.where(onehot, 1.0, 0.0)
    before = jnp.dot(oh_f.astype(BF16), tri_ref[...], preferred_element_type=F32)
    run = run_ref[...]
    run_t = jnp.concatenate([run] * (tm // 128), axis=-1)
    rank = jnp.sum(oh_f * (before + run_t), axis=0, keepdims=True)
    run_new = run + jnp.sum(oh_f, axis=1, keepdims=True)
    run_ref[...] = run_new
    cnt_ref[...] = run_new.astype(I32)

    zi = jnp.zeros_like(bin_id)
    oi_ref[0] = jnp.concatenate([gi * 4 + lo, gi * 4 + hi, bin_id, rank.astype(I32), zi, zi, zi, zi], axis=0)
    zf = jnp.zeros_like(w_lo)
    of_ref[0] = jnp.concatenate([w_lo, w_hi, zf, zf, zf, zf, zf, zf], axis=0)


def _router(x1, mod, layer, mod_row0, mod_rows, router_wt, router_b, tm):
    rows = x1.shape[0]
    nt = rows // tm
    mblk = 8 if mod_rows == 1 else mod_rows
    mrow = mod_row0 // mblk
    tri = jnp.asarray(np.triu(np.ones((tm, tm), np.float32), 1), BF16)
    rb = jnp.broadcast_to(router_b.astype(F32)[:, None], (N_EXPERTS, tm))
    mod_spec = lambda c: pl.BlockSpec((1, mblk, D_MODEL), lambda i: (layer, mrow, c))
    oi, of, cnt = pl.pallas_call(
        functools.partial(_router_kernel, mod_rows=mod_rows),
        grid=(nt,),
        in_specs=[pl.BlockSpec((tm, D_MODEL), lambda i: (i, 0)), mod_spec(3), mod_spec(4),
                  pl.BlockSpec((N_EXPERTS, D_MODEL), lambda i: (0, 0)),
                  pl.BlockSpec((N_EXPERTS, tm), lambda i: (0, 0)),
                  pl.BlockSpec((tm, tm), lambda i: (0, 0))],
        out_specs=[pl.BlockSpec((1, 8, tm), lambda i: (i, 0, 0)),
                   pl.BlockSpec((1, 8, tm), lambda i: (i, 0, 0)),
                   pl.BlockSpec((BIN_ROWS, 128), lambda i: (0, 0))],
        out_shape=[jax.ShapeDtypeStruct((nt, 8, tm), I32),
                   jax.ShapeDtypeStruct((nt, 8, tm), F32),
                   jax.ShapeDtypeStruct((BIN_ROWS, 128), I32)],
        scratch_shapes=[pltpu.VMEM((BIN_ROWS, 128), F32)],
        compiler_params=_cparams(("arbitrary",)),
        name="router",
    )(x1, mod, mod, router_wt, rb, tri)
    flat = lambda a, r: a[:, r, :].reshape(rows)
    return (flat(oi, 0), flat(oi, 1), flat(oi, 2), flat(oi, 3), flat(of, 0), flat(of, 1),
            cnt[:N_BINS, 0])


DMA_UNROLL = 8
ROW_GROUPS = MOE_TM // DMA_UNROLL


def _for_rows(n, fn):
    def body_u(j, carry):
        for u in range(DMA_UNROLL):
            fn(j, u)
        return carry
    lax.fori_loop(0, n // DMA_UNROLL, body_u, 0)

    def body_1(r, carry):
        fn(r // DMA_UNROLL, r % DMA_UNROLL)
        return carry
    lax.fori_loop((n // DMA_UNROLL) * DMA_UNROLL, n, body_1, 0)


def _wait_rows(n, buf, sem):
    p = ROW_GROUPS
    while p >= 1:
        @pl.when((n & (p * DMA_UNROLL)) != 0)
        def _(p=p):
            pltpu.make_async_copy(buf.at[pl.ds(0, p)], buf.at[pl.ds(0, p)], sem).wait()
        p //= 2
    p = DMA_UNROLL // 2
    while p >= 1:
        @pl.when((n & p) != 0)
        def _(p=p):
            pltpu.make_async_copy(buf.at[0, pl.ds(0, p)], buf.at[0, pl.ds(0, p)], sem).wait()
        p //= 2


def _moe_sorted_kernel(ea_ref, eb_ref, nvalid_ref, pos_ref, wlo_ref, whi_ref,
                       x_hbm, sh_ref, sc_ref, gate_ref,
                       wga_ref, wua_ref, wda_ref, wgb_ref, wub_ref, wdb_ref,
                       lng_ref, lnb_ref,
                       out_hbm,
                       inv, xbuf, wcol, obuf, wg_a, wu_a, wd_a, wg_b, wu_b, wd_b, gsem, ssem):
    i = pl.program_id(0)
    nt = pl.num_programs(0)
    slot = i % 2
    n_tok = pos_ref.shape[0]
    low_half = lax.broadcasted_iota(I32, (1, 128), 1) < HEAD_DIM

    def start_gather(t, dst_slot):
        def one(j, u):
            tok = inv[t * MOE_TM + j * DMA_UNROLL + u]
            pltpu.make_async_copy(x_hbm.at[pl.ds(tok, 1)], xbuf.at[dst_slot, j, pl.ds(u, 1)],
                                  gsem.at[dst_slot]).start()
            wcol[dst_slot, j, pl.ds(u, 1), :] = jnp.where(low_half, wlo_ref[tok], whi_ref[tok])
        _for_rows(nvalid_ref[t], one)

    @pl.when(i == 0)
    def _():
        def body(j, carry):
            for u in range(DMA_UNROLL):
                t = j * DMA_UNROLL + u
                inv[pos_ref[t]] = t
            return carry
        lax.fori_loop(0, n_tok // DMA_UNROLL, body, 0)
        xbuf[...] = jnp.zeros_like(xbuf)
        wcol[...] = jnp.zeros_like(wcol)
        start_gather(0, 0)

    @pl.when(i >= 2)
    def _():
        _wait_rows(nvalid_ref[jnp.maximum(i - 2, 0)], obuf.at[slot], ssem.at[slot])

    @pl.when(nvalid_ref[i] > 0)
    def _():
        _wait_rows(nvalid_ref[i], xbuf.at[slot], gsem.at[slot])

        @pl.when(i + 1 < nt)
        def _():
            start_gather(jnp.minimum(i + 1, nt - 1), 1 - slot)

        @pl.when(jnp.logical_or(i == 0, ea_ref[i] != ea_ref[jnp.maximum(i - 1, 0)]))
        def _():
            wg_a[...] = wga_ref[0, 0].astype(BF16)
            wu_a[...] = wua_ref[0, 0].astype(BF16)
            wd_a[...] = wda_ref[0, 0].astype(BF16)

        @pl.when(jnp.logical_or(i == 0, eb_ref[i] != eb_ref[jnp.maximum(i - 1, 0)]))
        def _():
            wg_b[...] = wgb_ref[0, 0].astype(BF16)
            wu_b[...] = wub_ref[0, 0].astype(BF16)
            wd_b[...] = wdb_ref[0, 0].astype(BF16)

        x = xbuf[slot].reshape(MOE_TM, D_MODEL)
        h2 = (x * (1.0 + sc_ref[0][:1]) + sh_ref[0][:1]).astype(BF16)

        def expert(wg, wu, wd):
            a = _silu(jnp.dot(h2, wg[...], preferred_element_type=F32)) * \
                jnp.dot(h2, wu[...], preferred_element_type=F32)
            return jnp.dot(a.astype(BF16), wd[...], preferred_element_type=F32)

        wrow = wcol[slot].reshape(MOE_TM, 128)
        w_lo = wrow[:, :1]
        w_hi = wrow[:, HEAD_DIM:HEAD_DIM + 1]
        y = w_lo * expert(wg_a, wu_a, wd_a) + w_hi * expert(wg_b, wu_b, wd_b)
        out = _ln(DN_ALPHA * x + (1.0 + gate_ref[0][:1]) * y, lng_ref[...], lnb_ref[...])
        obuf[slot] = out.reshape(ROW_GROUPS, DMA_UNROLL, D_MODEL)

        def scatter_one(j, u):
            tok = inv[i * MOE_TM + j * DMA_UNROLL + u]
            pltpu.make_async_copy(obuf.at[slot, j, pl.ds(u, 1)], out_hbm.at[pl.ds(tok, 1)],
                                  ssem.at[slot]).start()
        _for_rows(nvalid_ref[i], scatter_one)

    @pl.when(i == nt - 1)
    def _():
        _wait_rows(nvalid_ref[jnp.maximum(i - 1, 0)], obuf.at[1 - slot], ssem.at[1 - slot])
        _wait_rows(nvalid_ref[i], obuf.at[slot], ssem.at[slot])


def _moe_sorted(x1, mod, layer, mod_row0, route, wg, wu, wd, ln_g, ln_b):
    ea, eb, bin_id, rank, w_lo, w_hi, counts = route
    rows = x1.shape[0]
    nt = rows // MOE_TM + N_BINS
    ns = nt * MOE_TM
    tiles = (counts + MOE_TM - 1) // MOE_TM
    tile_end = jnp.cumsum(tiles)
    row_start = (tile_end - tiles) * MOE_TM
    pos = row_start[bin_id] + rank
    total = tile_end[-1]
    t = jnp.arange(nt, dtype=I32)
    tile_bin = jnp.searchsorted(tile_end, jnp.minimum(t, total - 1), side="right").astype(I32)
    tile_bin = jnp.clip(tile_bin, 0, N_BINS - 1)
    bin_lo = jnp.asarray([g * 4 + _PAIR_LO[p] for g in range(N_GROUPS) for p in range(N_PAIRS)], I32)
    bin_hi = jnp.asarray([g * 4 + _PAIR_HI[p] for g in range(N_GROUPS) for p in range(N_PAIRS)], I32)
    tile_ea, tile_eb = bin_lo[tile_bin], bin_hi[tile_bin]
    nvalid = jnp.clip(counts[tile_bin] - (t - (tile_end - tiles)[tile_bin]) * MOE_TM, 0, MOE_TM)
    nvalid = jnp.where(t < total, nvalid, 0).astype(I32)
    mrow = mod_row0 // 8
    mod_spec = lambda c: pl.BlockSpec((1, 8, D_MODEL), lambda i, *_: (layer, mrow, c))
    w_spec = lambda shape, ref_idx: pl.BlockSpec(
        (1, 1) + shape, lambda i, ea_r, eb_r, *_: (layer, (ea_r, eb_r)[ref_idx][i], 0, 0))
    any_spec = pl.BlockSpec(memory_space=pl.ANY)
    w_bf = lambda shape: pltpu.VMEM(shape, BF16)
    grid_spec = pltpu.PrefetchScalarGridSpec(
        num_scalar_prefetch=6,
        grid=(nt,),
        in_specs=[any_spec,
                  mod_spec(3), mod_spec(4), mod_spec(5),
                  w_spec((D_MODEL, D_FF), 0), w_spec((D_MODEL, D_FF), 0), w_spec((D_FF, D_MODEL), 0),
                  w_spec((D_MODEL, D_FF), 1), w_spec((D_MODEL, D_FF), 1), w_spec((D_FF, D_MODEL), 1),
                  pl.BlockSpec((1, D_MODEL), lambda i, *_: (0, 0)),
                  pl.BlockSpec((1, D_MODEL), lambda i, *_: (0, 0))],
        out_specs=any_spec,
        scratch_shapes=[pltpu.SMEM((ns,), I32),
                        pltpu.VMEM((2, ROW_GROUPS, DMA_UNROLL, D_MODEL), F32),
                        pltpu.VMEM((2, ROW_GROUPS, DMA_UNROLL, 128), F32),
                        pltpu.VMEM((2, ROW_GROUPS, DMA_UNROLL, D_MODEL), F32),
                        w_bf((D_MODEL, D_FF)), w_bf((D_MODEL, D_FF)), w_bf((D_FF, D_MODEL)),
                        w_bf((D_MODEL, D_FF)), w_bf((D_MODEL, D_FF)), w_bf((D_FF, D_MODEL)),
                        pltpu.SemaphoreType.DMA((2,)),
                        pltpu.SemaphoreType.DMA((2,))],
    )
    return pl.pallas_call(
        _moe_sorted_kernel,
        grid_spec=grid_spec,
        out_shape=jax.ShapeDtypeStruct((rows, D_MODEL), F32),
        compiler_params=_cparams(("arbitrary",)),
        name="moe_sorted",
    )(tile_ea, tile_eb, nvalid, pos, w_lo, w_hi, x1, mod, mod, mod,
      wg, wu, wd, wg, wu, wd, ln_g, ln_b)


def _moe_dense_kernel(x_ref, dw_ref, sh_ref, sc_ref, gate_ref, wg_ref, wu_ref, wd_ref,
                      lng_ref, lnb_ref, o_ref, acc):
    e = pl.program_id(0)

    @pl.when(e == 0)
    def _():
        acc[...] = jnp.zeros_like(acc)

    x = x_ref[...]
    h2 = (x * (1.0 + sc_ref[0]) + sh_ref[0]).astype(BF16)
    a = _silu(jnp.dot(h2, wg_ref[0, 0].astype(BF16), preferred_element_type=F32)) * \
        jnp.dot(h2, wu_ref[0, 0].astype(BF16), preferred_element_type=F32)
    y = jnp.dot(a.astype(BF16), wd_ref[0, 0].astype(BF16), preferred_element_type=F32)
    acc[...] = acc[...] + dw_ref[0][:, :1] * y

    @pl.when(e == pl.num_programs(0) - 1)
    def _():
        o_ref[...] = _ln(DN_ALPHA * x + (1.0 + gate_ref[0]) * acc[...], lng_ref[...], lnb_ref[...])


def _moe_dense(x1, mod, layer, route, wg, wu, wd, ln_g, ln_b):
    ea, eb, _, _, w_lo, w_hi, _ = route
    nb = x1.shape[0]
    eids = jnp.arange(N_EXPERTS, dtype=I32)[:, None]
    dw = jnp.where(eids == ea[None, :], w_lo[None, :], 0.0) + \
        jnp.where(eids == eb[None, :], w_hi[None, :], 0.0)
    dw = jnp.broadcast_to(dw[:, :, None], (N_EXPERTS, nb, 128))
    mod_spec = lambda c: pl.BlockSpec((1, nb, D_MODEL), lambda e: (layer, 0, c))
    const2 = lambda a: pl.BlockSpec(a.shape, lambda e: (0, 0))
    return pl.pallas_call(
        _moe_dense_kernel,
        grid=(N_EXPERTS,),
        in_specs=[const2(x1), pl.BlockSpec((1, nb, 128), lambda e: (e, 0, 0)),
                  mod_spec(3), mod_spec(4), mod_spec(5),
                  pl.BlockSpec((1, 1, D_MODEL, D_FF), lambda e: (layer, e, 0, 0)),
                  pl.BlockSpec((1, 1, D_MODEL, D_FF), lambda e: (layer, e, 0, 0)),
                  pl.BlockSpec((1, 1, D_FF, D_MODEL), lambda e: (layer, e, 0, 0)),
                  const2(ln_g), const2(ln_b)],
        out_specs=const2(x1),
        out_shape=jax.ShapeDtypeStruct(x1.shape, F32),
        scratch_shapes=[pltpu.VMEM(x1.shape, F32)],
        compiler_params=_cparams(("arbitrary",)),
        name="moe_dense",
    )(x1, dw, mod, mod, mod, wg, wu, wd, ln_g, ln_b)


def kernel(x_prompt, x_sample, state_ret, cache_swa_k, cache_swa_v, c_prompt, c_sample, w_in, w_out, ret_gn_gain, swa_sinks, rel_bias_table, ada_w, ada_b, ln1_g, ln1_b, ln2_g, ln2_b, router_w, router_b, exp_w_gate, exp_w_up, exp_w_down):
    seq = x_prompt.shape[1]
    nb = x_sample.shape[0]
    past_len = 16384
    assert x_prompt.shape[0] == 1 and x_sample.shape[1] == 1

    c_all = jnp.concatenate([c_sample, c_prompt, jnp.zeros((7, D_MODEL), F32)], axis=0)
    mod = _ada(c_all, ada_w, ada_b)
    prompt_row = nb

    bias_tabs = _bias_tables(rel_bias_table.astype(F32))
    gl, decay, xi, zeta, gamma1 = _ret_consts()
    rope_p = _rope_tables(jnp.arange(seq, dtype=I32))
    rope_s = _rope_tables(jnp.full((nb,), past_len, I32))
    router_wt = router_w.astype(F32).T
    zeta_p = jnp.tile(zeta, (INPROJ_TM // CHUNK, 1))
    zeta_s = jnp.ones((nb, RET_W), F32)

    xp = x_prompt.reshape(seq, D_MODEL)
    xs = x_sample.reshape(nb, D_MODEL)
    st_p, k_p, v_p, st_s, k_s, v_s = [], [], [], [], [], []
    for l in range(DEPTH):
        w_in_bf = w_in[l].astype(BF16)
        w_out_bf = w_out[l].astype(BF16)
        experts = (exp_w_gate.astype(F32), exp_w_up.astype(F32), exp_w_down.astype(F32))
        gain = ret_gn_gain[l].astype(F32)
        sinks = swa_sinks[l].astype(F32)
        row = lambda a: a[l].astype(F32).reshape(1, D_MODEL)

        proj = _inproj(xp, mod, l, prompt_row, 1, w_in_bf, rope_p, zeta_p, INPROJ_TM)
        x1, st = _mix_prompt(proj, xp, mod, l, prompt_row, w_out_bf, (gl, decay, xi),
                             bias_tabs, sinks, gain.reshape(1, RET_W), row(ln1_g), row(ln1_b))
        skv = proj[6]
        st_p.append(st.reshape(1, N_RET_HEADS, HEAD_DIM, HEAD_DIM))
        k_p.append(skv[seq - CHUNK:, :SWA_KVW].reshape(1, CHUNK, N_KV_HEADS, HEAD_DIM))
        v_p.append(skv[seq - CHUNK:, SWA_KVW:].reshape(1, CHUNK, N_KV_HEADS, HEAD_DIM))
        route = _router(x1, mod, l, prompt_row, 1, router_wt, router_b, 512)
        xp = _moe_sorted(x1, mod, l, prompt_row, route, *experts, row(ln2_g), row(ln2_b))

        proj = _inproj(xs, mod, l, 0, nb, w_in_bf, rope_s, zeta_s, nb)
        x1, nst, nk, nv = _mix_sample(proj, xs, mod, l, w_out_bf, gamma1, bias_tabs[2], sinks, gain,
                                      row(ln1_g), row(ln1_b), state_ret[l].astype(F32),
                                      cache_swa_k[l], cache_swa_v[l])
        st_s.append(nst)
        k_s.append(nk)
        v_s.append(nv)
        route = _router(x1, mod, l, 0, nb, router_wt, router_b, nb)
        xs = _moe_dense(x1, mod, l, route, *experts, row(ln2_g), row(ln2_b))

    return (xp.reshape(1, seq, D_MODEL), xs.reshape(nb, 1, D_MODEL),
            jnp.stack(st_p), jnp.stack(k_p), jnp.stack(v_p),
            jnp.stack(st_s), jnp.stack(k_s), jnp.stack(v_s))
```

```python
import functools
import math

import numpy as np
import jax
import jax.numpy as jnp
from jax import lax
from jax.experimental import pallas as pl
from jax.experimental.pallas import tpu as pltpu

F32 = jnp.float32
BF16 = jnp.bfloat16
I32 = jnp.int32

D_MODEL = 1024
DEPTH = 2
HEAD_DIM = 64
N_RET_HEADS = 8
N_SWA_HEADS = 8
N_KV_HEADS = 2
GQA_GROUP = N_SWA_HEADS // N_KV_HEADS
RET_W = N_RET_HEADS * HEAD_DIM
SWA_QW = N_SWA_HEADS * HEAD_DIM
SWA_KVW = N_KV_HEADS * HEAD_DIM
PROJ_W = 4 * RET_W + SWA_QW + 2 * SWA_KVW
CHUNK = 128
ROPE_BASE = 10000.0
N_BUCKETS = 32
MAX_DISTANCE = 128
N_EXPERTS = 16
N_GROUPS = 4
EXPERTS_PER_GROUP = 4
D_FF = 512
LN_EPS = 1e-5
DN_ALPHA = (2 * DEPTH) ** 0.25
QK_SCALE = HEAD_DIM ** -0.5
NEG_BIG = -1e30

N_PAIRS = 6
N_BINS = N_GROUPS * N_PAIRS
BIN_ROWS = 32
MOE_TM = 256
INPROJ_TM = 512
VMEM_LIMIT = 56 * 1024 * 1024

_PAIR_LO = (0, 0, 0, 1, 1, 2)
_PAIR_HI = (1, 2, 3, 2, 3, 3)


def _cparams(sem):
    return pltpu.CompilerParams(dimension_semantics=sem, vmem_limit_bytes=VMEM_LIMIT)


def _ln(v, g, b):
    mu = jnp.mean(v, axis=-1, keepdims=True)
    c = v - mu
    var = jnp.mean(c * c, axis=-1, keepdims=True)
    return c * lax.rsqrt(var + LN_EPS) * g + b


def _silu(v):
    return v * (1.0 / (1.0 + jnp.exp(-v)))


def _ada_kernel(c_ref, w_ref, b_ref, o_ref):
    o_ref[0] = jnp.dot(c_ref[...].astype(BF16), w_ref[0].astype(BF16),
                       preferred_element_type=F32) + b_ref[0]


def _ada(c_all, ada_w, ada_b):
    rows = c_all.shape[0]
    nt = 6 * D_MODEL // 1024
    return pl.pallas_call(
        _ada_kernel,
        grid=(DEPTH, nt),
        in_specs=[pl.BlockSpec((rows, D_MODEL), lambda l, j: (0, 0)),
                  pl.BlockSpec((1, D_MODEL, 1024), lambda l, j: (l, 0, j)),
                  pl.BlockSpec((1, 1, 1024), lambda l, j: (l, 0, j))],
        out_specs=pl.BlockSpec((1, rows, 1024), lambda l, j: (l, 0, j)),
        out_shape=jax.ShapeDtypeStruct((DEPTH, rows, 6 * D_MODEL), F32),
        compiler_params=_cparams(("arbitrary", "arbitrary")),
        name="ada",
    )(c_all, ada_w, ada_b.reshape(DEPTH, 1, 6 * D_MODEL))


def _bias_kernel(tab_ref, bkt_ref, full_ref, first_ref, row_ref):
    bkt = bkt_ref[...]
    col = lax.broadcasted_iota(I32, bkt.shape, 1)
    for h in range(N_SWA_HEADS):
        acc = jnp.full(bkt.shape, NEG_BIG, F32)
        for b in range(N_BUCKETS):
            acc = jnp.where(bkt == b, tab_ref[b, h], acc)
        full_ref[h] = acc
        first_ref[h] = jnp.where(col < CHUNK, NEG_BIG, acc)
        row_ref[h:h + 1, :] = acc[CHUNK - 1:CHUNK, CHUNK:]


def _t5_bucket(rel):
    max_exact = N_BUCKETS // 2
    relf = jnp.maximum(rel, 1).astype(F32)
    large = max_exact + (jnp.log(relf / max_exact) / math.log(MAX_DISTANCE / max_exact)
                         * (N_BUCKETS - max_exact)).astype(I32)
    large = jnp.minimum(large, N_BUCKETS - 1)
    return jnp.where(rel < max_exact, rel, large)


def _bias_tables(rel_bias_table):
    qi = jnp.arange(CHUNK)
    si = jnp.arange(2 * CHUNK)
    rel = CHUNK + qi[:, None] - si[None, :]
    band = (rel >= 0) & (rel < CHUNK)
    bkt = jnp.where(band, _t5_bucket(jnp.maximum(rel, 0)), -1).astype(I32)
    return pl.pallas_call(
        _bias_kernel,
        in_specs=[pl.BlockSpec(memory_space=pltpu.SMEM),
                  pl.BlockSpec((CHUNK, 2 * CHUNK), lambda: (0, 0))],
        out_specs=[pl.BlockSpec((N_SWA_HEADS, CHUNK, 2 * CHUNK), lambda: (0, 0, 0)),
                   pl.BlockSpec((N_SWA_HEADS, CHUNK, 2 * CHUNK), lambda: (0, 0, 0)),
                   pl.BlockSpec((N_SWA_HEADS, CHUNK), lambda: (0, 0))],
        out_shape=[jax.ShapeDtypeStruct((N_SWA_HEADS, CHUNK, 2 * CHUNK), F32),
                   jax.ShapeDtypeStruct((N_SWA_HEADS, CHUNK, 2 * CHUNK), F32),
                   jax.ShapeDtypeStruct((N_SWA_HEADS, CHUNK), F32)],
        name="t5_bias",
    )(rel_bias_table, bkt)


def _rotary(v, cos, s_lo, s_hi):
    outs = []
    for j in range(RET_W // 128):
        blk = v[:, j * 128:(j + 1) * 128]
        outs.append(blk * cos + pltpu.roll(blk, 96, 1) * s_lo + pltpu.roll(blk, 32, 1) * s_hi)
    return jnp.concatenate(outs, axis=-1)


def _inproj_kernel(x_ref, sh_ref, sc_ref, w_ref, cs_ref, zeta_ref,
                   q_ref, k_ref, kz_ref, v_ref, g_ref, sq_ref, skv_ref, *, mod_rows):
    sh = sh_ref[0][:mod_rows]
    sc = sc_ref[0][:mod_rows]
    h = (x_ref[...] * (1.0 + sc) + sh).astype(BF16)
    cs = cs_ref[...]
    first = (lax.broadcasted_iota(I32, cs.shape, 1) % HEAD_DIM) < (HEAD_DIM // 2)
    cos = jnp.where(first, cs, pltpu.roll(cs, 32, 1))
    s_lo = jnp.where(first, -pltpu.roll(cs, 96, 1), 0.0)
    s_hi = jnp.where(first, 0.0, cs)

    def proj(lo, hi):
        return jnp.dot(h, w_ref[:, lo:hi], preferred_element_type=F32)

    q_ref[...] = _rotary(proj(0, RET_W), cos, s_lo, s_hi).astype(BF16)
    k = _rotary(proj(RET_W, 2 * RET_W), cos, s_lo, s_hi) * QK_SCALE
    k_ref[...] = k.astype(BF16)
    kz_ref[...] = (k * zeta_ref[...]).astype(BF16)
    v_ref[...] = proj(2 * RET_W, 3 * RET_W).astype(BF16)
    g_ref[...] = proj(3 * RET_W, 4 * RET_W)
    sq_ref[...] = (proj(4 * RET_W, 4 * RET_W + SWA_QW) * QK_SCALE).astype(BF16)
    skv_ref[...] = proj(4 * RET_W + SWA_QW, PROJ_W)


def _inproj(x, mod, layer, mod_row0, mod_rows, w_in_bf, rope, zeta_tile, tm):
    rows = x.shape[0]
    mblk = 8 if mod_rows == 1 else mod_rows
    mrow = mod_row0 // mblk
    row_spec = lambda w: pl.BlockSpec((tm, w), lambda i: (i, 0))
    mod_spec = lambda c: pl.BlockSpec((1, mblk, D_MODEL), lambda i: (layer, mrow, c))
    return pl.pallas_call(
        functools.partial(_inproj_kernel, mod_rows=mod_rows),
        grid=(rows // tm,),
        in_specs=[row_spec(D_MODEL), mod_spec(0), mod_spec(1),
                  pl.BlockSpec((D_MODEL, PROJ_W), lambda i: (0, 0)),
                  row_spec(128),
                  pl.BlockSpec((tm, RET_W), lambda i: (0, 0))],
        out_specs=[row_spec(RET_W), row_spec(RET_W), row_spec(RET_W), row_spec(RET_W), row_spec(RET_W),
                   row_spec(SWA_QW), row_spec(2 * SWA_KVW)],
        out_shape=[jax.ShapeDtypeStruct((rows, RET_W), BF16),
                   jax.ShapeDtypeStruct((rows, RET_W), BF16),
                   jax.ShapeDtypeStruct((rows, RET_W), BF16),
                   jax.ShapeDtypeStruct((rows, RET_W), BF16),
                   jax.ShapeDtypeStruct((rows, RET_W), F32),
                   jax.ShapeDtypeStruct((rows, SWA_QW), BF16),
                   jax.ShapeDtypeStruct((rows, 2 * SWA_KVW), F32)],
        compiler_params=_cparams(("arbitrary",)),
        name="inproj",
    )(x, mod, mod, w_in_bf, rope, zeta_tile)


def _rope_tables(pos):
    half = HEAD_DIM // 2
    inv = ROPE_BASE ** (-jnp.arange(half, dtype=F32) / half)
    ang = pos.astype(F32)[:, None] * inv[None, :]
    cos, sin = jnp.cos(ang), jnp.sin(ang)
    return jnp.concatenate([cos, sin, cos, sin], axis=-1)


def _mix_prompt_kernel(gl_ref, sink_ref,
                       q_ref, k_ref, kz_ref, v_ref, g_ref, sq_ref, kvc_ref, kvp_ref,
                       decay_ref, xi_ref, bias_ref, bias0_ref, gain_ref,
                       mixed_ref, st_ref, state):
    i = pl.program_id(0)

    @pl.when(i == 0)
    def _():
        state[...] = jnp.zeros_like(state)

    q = q_ref[...]
    k = k_ref[...]
    v = v_ref[...]
    kz = kz_ref[...]

    intra, inter = [], []
    for h in range(N_RET_HEADS):
        sl = slice(h * HEAD_DIM, (h + 1) * HEAD_DIM)
        qh, kh, vh = q[:, sl], k[:, sl], v[:, sl]
        s = lax.dot_general(qh, kh, (((1,), (1,)), ((), ())), preferred_element_type=F32)
        p = (s * decay_ref[h]).astype(BF16)
        intra.append(jnp.dot(p, vh, preferred_element_type=F32))
        s_old = state[h]
        inter.append(jnp.dot(qh, s_old.astype(BF16), preferred_element_type=F32))
        upd = lax.dot_general(kz[:, sl], vh, (((0,), (0,)), ((), ())), preferred_element_type=F32)
        state[h] = gl_ref[h] * s_old + upd

    xi = xi_ref[...]
    g = g_ref[...]
    ret_parts = []
    for h in range(N_RET_HEADS):
        sl = slice(h * HEAD_DIM, (h + 1) * HEAD_DIM)
        o = intra[h] + inter[h] * xi[:, sl]
        mu = jnp.mean(o, axis=-1, keepdims=True)
        c = o - mu
        var = jnp.mean(c * c, axis=-1, keepdims=True)
        ret_parts.append(c * lax.rsqrt(var + LN_EPS))
    ret_o = jnp.concatenate(ret_parts, axis=-1) * gain_ref[...] * _silu(g)

    sq = sq_ref[...]
    kvc = kvc_ref[...]
    kvp = kvp_ref[...]
    swa_parts = []
    for kv in range(N_KV_HEADS):
        ks = slice(kv * HEAD_DIM, (kv + 1) * HEAD_DIM)
        vs = slice(SWA_KVW + kv * HEAD_DIM, SWA_KVW + (kv + 1) * HEAD_DIM)
        kk = jnp.concatenate([kvp[:, ks], kvc[:, ks]], axis=0).astype(BF16)
        vv = jnp.concatenate([kvp[:, vs], kvc[:, vs]], axis=0).astype(BF16)
        for j in range(GQA_GROUP):
            hh = kv * GQA_GROUP + j
            qh = sq[:, hh * HEAD_DIM:(hh + 1) * HEAD_DIM]
            s = lax.dot_general(qh, kk, (((1,), (1,)), ((), ())), preferred_element_type=F32)
            s = s + jnp.where(i == 0, bias0_ref[hh], bias_ref[hh])
            sink = sink_ref[hh]
            m = jnp.maximum(jnp.max(s, axis=-1, keepdims=True), sink)
            p = jnp.exp(s - m)
            den = jnp.sum(p, axis=-1, keepdims=True) + jnp.exp(sink - m)
            swa_parts.append(jnp.dot((p / den).astype(BF16), vv, preferred_element_type=F32))
    swa_o = jnp.concatenate(swa_parts, axis=-1)

    mixed_ref[...] = jnp.concatenate([ret_o, swa_o], axis=-1).astype(BF16)

    @pl.when(i == pl.num_programs(0) - 1)
    def _():
        st_ref[...] = state[...]


def _mix_prompt(proj, ret_consts, bias_tabs, sinks, gain):
    q, k, kz, v, g, sq, skv = proj
    rows = q.shape[0]
    nblk = rows // CHUNK
    gl, decay, xi = ret_consts
    bias_full, bias_first, _ = bias_tabs
    row_spec = lambda w: pl.BlockSpec((CHUNK, w), lambda i: (i, 0))
    const2 = lambda a: pl.BlockSpec(a.shape, lambda i: (0, 0))
    const3 = lambda a: pl.BlockSpec(a.shape, lambda i: (0, 0, 0))
    smem = pl.BlockSpec(memory_space=pltpu.SMEM)
    return pl.pallas_call(
        _mix_prompt_kernel,
        grid=(nblk,),
        in_specs=[smem, smem,
                  row_spec(RET_W), row_spec(RET_W), row_spec(RET_W), row_spec(RET_W), row_spec(RET_W),
                  row_spec(SWA_QW), row_spec(2 * SWA_KVW),
                  pl.BlockSpec((CHUNK, 2 * SWA_KVW), lambda i: (jnp.maximum(i - 1, 0), 0)),
                  const3(decay), const2(xi),
                  const3(bias_full), const3(bias_first), const2(gain)],
        out_specs=[row_spec(D_MODEL),
                   pl.BlockSpec((N_RET_HEADS, HEAD_DIM, HEAD_DIM), lambda i: (0, 0, 0))],
        out_shape=[jax.ShapeDtypeStruct((rows, D_MODEL), BF16),
                   jax.ShapeDtypeStruct((N_RET_HEADS, HEAD_DIM, HEAD_DIM), F32)],
        scratch_shapes=[pltpu.VMEM((N_RET_HEADS, HEAD_DIM, HEAD_DIM), F32)],
        compiler_params=_cparams(("arbitrary",)),
        name="mix_prompt",
    )(gl, sinks, q, k, kz, v, g, sq, skv, skv, decay, xi, bias_full, bias_first, gain)


def _ret_consts():
    lg = jnp.log(1.0 - 2.0 ** (-5.0 - jnp.arange(N_RET_HEADS, dtype=F32)))
    idx = jnp.arange(CHUNK, dtype=F32)
    diff = idx[:, None] - idx[None, :]
    decay = jnp.where(diff >= 0, jnp.exp(jnp.maximum(diff, 0.0)[None] * lg[:, None, None]), 0.0)
    xi = jnp.exp((idx + 1.0)[:, None] * lg[None, :])
    zeta = jnp.exp((CHUNK - 1.0 - idx)[:, None] * lg[None, :])
    gl = jnp.exp(CHUNK * lg)
    expand = lambda a: jnp.repeat(a, HEAD_DIM, axis=1)
    return gl, decay, expand(xi), expand(zeta), jnp.exp(1.0 * lg)


SAMPLE_BB = 8


def _mix_sample_kernel(g1_ref, sink_ref,
                       qt_ref, kt_ref, v3_ref, sq3_ref, knew_ref, vnew_ref,
                       st_ref, ck_ref, cv_ref, g3_ref, x_ref, gate_ref,
                       wout_ref, brow_ref, gain_ref, lng_ref, lnb_ref,
                       x1_ref, nst_ref, nk_ref, nv_ref, ret_scr, swa_scr):
    i = pl.program_id(0)
    row = lax.broadcasted_iota(I32, (N_SWA_HEADS, 2 * HEAD_DIM), 0)
    lane = lax.broadcasted_iota(I32, (N_SWA_HEADS, 2 * HEAD_DIM), 1)
    own_half = (row // GQA_GROUP) == (lane // HEAD_DIM)
    sink_col = jnp.concatenate(
        [jnp.full((1, 1), sink_ref[hh], F32) for hh in range(N_SWA_HEADS)], axis=0)
    brow = brow_ref[...]
    qt = qt_ref[0]
    kt = kt_ref[0]

    for b in range(SAMPLE_BB):
        o_rows = []
        for h in range(N_RET_HEADS):
            rs = slice(h * HEAD_DIM, (h + 1) * HEAD_DIM)
            s_old = st_ref[b, rs, :]
            kcol = kt[rs, b:b + 1]
            qcol = qt[rs, b:b + 1]
            vrow = v3_ref[b, h:h + 1, :]
            s_new = g1_ref[h] * s_old + kcol * vrow
            nst_ref[b, rs, :] = s_new
            o_rows.append(jnp.sum(qcol * s_new, axis=0, keepdims=True))
        ret_scr[i * SAMPLE_BB + b] = jnp.concatenate(o_rows, axis=0)

        kk = jnp.concatenate([ck_ref[b, 1:, :], knew_ref[b:b + 1, :]], axis=0)
        vv = jnp.concatenate([cv_ref[b, 1:, :], vnew_ref[b:b + 1, :]], axis=0)
        nk_ref[b] = kk
        nv_ref[b] = vv
        q8 = sq3_ref[b]
        qblk = jnp.where(own_half, jnp.concatenate([q8, q8], axis=-1), 0.0).astype(BF16)
        s = lax.dot_general(qblk, kk.astype(BF16), (((1,), (1,)), ((), ())),
                            preferred_element_type=F32) + brow
        m = jnp.maximum(jnp.max(s, axis=-1, keepdims=True), sink_col)
        p = jnp.exp(s - m)
        den = jnp.sum(p, axis=-1, keepdims=True) + jnp.exp(sink_col - m)
        o = jnp.dot((p / den).astype(BF16), vv.astype(BF16), preferred_element_type=F32)
        swa_scr[i * SAMPLE_BB + b] = jnp.where(own_half[:, :HEAD_DIM], o[:, :HEAD_DIM], o[:, HEAD_DIM:])

    @pl.when(i == pl.num_programs(0) - 1)
    def _():
        y = jnp.zeros(x_ref.shape, F32)
        for h in range(N_RET_HEADS):
            o = ret_scr[:, h, :]
            mu = jnp.mean(o, axis=-1, keepdims=True)
            c = o - mu
            var = jnp.mean(c * c, axis=-1, keepdims=True)
            r = c * lax.rsqrt(var + LN_EPS) * gain_ref[h:h + 1, :] * _silu(g3_ref[h])
            y = y + jnp.dot(r.astype(BF16), wout_ref[h * HEAD_DIM:(h + 1) * HEAD_DIM, :],
                            preferred_element_type=F32)
        for hh in range(N_SWA_HEADS):
            o = swa_scr[:, hh, :].astype(BF16)
            lo = RET_W + hh * HEAD_DIM
            y = y + jnp.dot(o, wout_ref[lo:lo + HEAD_DIM, :], preferred_element_type=F32)
        x1_ref[...] = _ln(DN_ALPHA * x_ref[...] + (1.0 + gate_ref[0]) * y, lng_ref[...], lnb_ref[...])


def _mix_sample(proj, x, mod, layer, w_out_bf, gamma1, bias_row, sinks, gain8, ln_g, ln_b,
                st, ck, cv):
    q, k, _, v, g, sq, skv = proj
    nb = x.shape[0]
    steps = nb // SAMPLE_BB
    to_cols = lambda a: a.astype(F32).reshape(steps, SAMPLE_BB, RET_W).transpose(0, 2, 1)
    qt, kt = to_cols(q), to_cols(k)
    v3 = v.astype(F32).reshape(nb, N_RET_HEADS, HEAD_DIM)
    sq3 = sq.astype(F32).reshape(nb, N_SWA_HEADS, HEAD_DIM)
    g3 = g.reshape(nb, N_RET_HEADS, HEAD_DIM).transpose(1, 0, 2)
    knew, vnew = skv[:, :SWA_KVW], skv[:, SWA_KVW:]
    st2 = st.reshape(nb, RET_W, HEAD_DIM)
    ck2 = ck.reshape(nb, CHUNK, SWA_KVW)
    cv2 = cv.reshape(nb, CHUNK, SWA_KVW)
    smem = pl.BlockSpec(memory_space=pltpu.SMEM)
    blk3 = lambda a, b, c: pl.BlockSpec((a, b, c), lambda i: (i, 0, 0))
    const2 = lambda a: pl.BlockSpec(a.shape, lambda i: (0, 0))
    const3 = lambda a: pl.BlockSpec(a.shape, lambda i: (0, 0, 0))
    x1, nst, nk, nv = pl.pallas_call(
        _mix_sample_kernel,
        grid=(steps,),
        in_specs=[smem, smem,
                  blk3(1, RET_W, SAMPLE_BB), blk3(1, RET_W, SAMPLE_BB),
                  blk3(SAMPLE_BB, N_RET_HEADS, HEAD_DIM), blk3(SAMPLE_BB, N_SWA_HEADS, HEAD_DIM),
                  pl.BlockSpec((SAMPLE_BB, SWA_KVW), lambda i: (i, 0)),
                  pl.BlockSpec((SAMPLE_BB, SWA_KVW), lambda i: (i, 0)),
                  blk3(SAMPLE_BB, RET_W, HEAD_DIM), blk3(SAMPLE_BB, CHUNK, SWA_KVW),
                  blk3(SAMPLE_BB, CHUNK, SWA_KVW),
                  const3(g3), const2(x),
                  pl.BlockSpec((1, nb, D_MODEL), lambda i: (layer, 0, 2)),
                  const2(w_out_bf), const2(bias_row), const2(gain8), const2(ln_g), const2(ln_b)],
        out_specs=[const2(x), blk3(SAMPLE_BB, RET_W, HEAD_DIM), blk3(SAMPLE_BB, CHUNK, SWA_KVW),
                   blk3(SAMPLE_BB, CHUNK, SWA_KVW)],
        out_shape=[jax.ShapeDtypeStruct(x.shape, F32), jax.ShapeDtypeStruct(st2.shape, F32),
                   jax.ShapeDtypeStruct(ck2.shape, F32), jax.ShapeDtypeStruct(cv2.shape, F32)],
        scratch_shapes=[pltpu.VMEM((nb, N_RET_HEADS, HEAD_DIM), F32),
                        pltpu.VMEM((nb, N_SWA_HEADS, HEAD_DIM), F32)],
        compiler_params=_cparams(("arbitrary",)),
        name="mix_sample",
    )(gamma1, sinks, qt, kt, v3, sq3, knew, vnew, st2, ck2, cv2, g3, x, mod,
      w_out_bf, bias_row, gain8, ln_g, ln_b)
    return x1, nst.reshape(st.shape), nk.reshape(ck.shape), nv.reshape(cv.shape)


def _router_kernel(x_ref, sh_ref, sc_ref, rwt_ref, rb_ref, tri_ref,
                   oi_ref, of_ref, cnt_ref, run_ref, *, mod_rows):
    h2 = x_ref[...] * (1.0 + sc_ref[0][:mod_rows]) + sh_ref[0][:mod_rows]
    _route_rows(h2, rwt_ref, rb_ref, tri_ref, oi_ref, of_ref, cnt_ref, run_ref)


def _post_mix_kernel(mixed_ref, x_ref, gate_ref, wout_ref, lng_ref, lnb_ref,
                     sh_ref, sc_ref, rwt_ref, rb_ref, tri_ref,
                     x1_ref, oi_ref, of_ref, cnt_ref, run_ref):
    y = jnp.dot(mixed_ref[...], wout_ref[...], preferred_element_type=F32)
    x1 = _ln(DN_ALPHA * x_ref[...] + (1.0 + gate_ref[0][:1]) * y, lng_ref[...], lnb_ref[...])
    x1_ref[...] = x1
    h2 = x1 * (1.0 + sc_ref[0][:1]) + sh_ref[0][:1]
    _route_rows(h2, rwt_ref, rb_ref, tri_ref, oi_ref, of_ref, cnt_ref, run_ref)


def _route_rows(h2, rwt_ref, rb_ref, tri_ref, oi_ref, of_ref, cnt_ref, run_ref):
    i = pl.program_id(0)

    @pl.when(i == 0)
    def _():
        run_ref[...] = jnp.zeros_like(run_ref)

    logits = lax.dot_general(rwt_ref[...].astype(BF16), h2.astype(BF16), (((1,), (1,)), ((), ())),
                             preferred_element_type=F32)
    aff = 1.0 / (1.0 + jnp.exp(-logits))
    sel = aff + rb_ref[...]
    s = [sel[e:e + 1, :] for e in range(N_EXPERTS)]
    a = [aff[e:e + 1, :] for e in range(N_EXPERTS)]

    def top2sum(v0, v1, v2, v3):
        hi01, lo01 = jnp.maximum(v0, v1), jnp.minimum(v0, v1)
        hi23, lo23 = jnp.maximum(v2, v3), jnp.minimum(v2, v3)
        return jnp.maximum(hi01, hi23) + jnp.maximum(jnp.minimum(hi01, hi23),
                                                     jnp.maximum(lo01, lo23))

    def argmax_first(vals):
        best, idx = vals[0], jnp.zeros(vals[0].shape, I32)
        for j in range(1, len(vals)):
            upd = vals[j] > best
            idx = jnp.where(upd, j, idx)
            best = jnp.where(upd, vals[j], best)
        return idx

    def pick(idx, vals):
        out = vals[-1]
        for j in range(len(vals) - 2, -1, -1):
            out = jnp.where(idx == j, vals[j], out)
        return out

    gi = argmax_first([top2sum(*s[4 * g:4 * g + 4]) for g in range(N_GROUPS)])
    sv = [pick(gi, [s[4 * g + j] for g in range(N_GROUPS)]) for j in range(EXPERTS_PER_GROUP)]
    av = [pick(gi, [a[4 * g + j] for g in range(N_GROUPS)]) for j in range(EXPERTS_PER_GROUP)]
    i1 = argmax_first(sv)
    i2 = argmax_first([jnp.where(i1 == j, -jnp.inf, sv[j]) for j in range(EXPERTS_PER_GROUP)])
    w1, w2 = pick(i1, av), pick(i2, av)
    wsum = w1 + w2
    w1, w2 = w1 / wsum, w2 / wsum
    lo, hi = jnp.minimum(i1, i2), jnp.maximum(i1, i2)
    w_lo = jnp.where(i1 < i2, w1, w2)
    w_hi = jnp.where(i1 < i2, w2, w1)
    pair = jnp.where(lo == 0, hi - 1, jnp.where(lo == 1, hi + 1, 5))
    bin_id = gi * N_PAIRS + pair

    tm = bin_id.shape[1]
    onehot = lax.broadcasted_iota(I32, (BIN_ROWS, tm), 0) == bin_id
    oh_f = jnp.where(onehot, 1.0, 0.0)
    before = jnp.dot(oh_f.astype(BF16), tri_ref[...], preferred_element_type=F32)
    run = run_ref[...]
    run_t = jnp.concatenate([run] * (tm // 128), axis=-1)
    rank = jnp.sum(oh_f * (before + run_t), axis=0, keepdims=True)
    run_new = run + jnp.sum(oh_f, axis=1, keepdims=True)
    run_ref[...] = run_new
    cnt_ref[...] = run_new.astype(I32)

    zi = jnp.zeros_like(bin_id)
    oi_ref[0] = jnp.concatenate([gi * 4 + lo, gi * 4 + hi, bin_id, rank.astype(I32), zi, zi, zi, zi], axis=0)
    zf = jnp.zeros_like(w_lo)
    of_ref[0] = jnp.concatenate([w_lo, w_hi, zf, zf, zf, zf, zf, zf], axis=0)


def _router(x1, mod, layer, mod_row0, mod_rows, router_wt, router_b, tm):
    rows = x1.shape[0]
    nt = rows // tm
    mblk = 8 if mod_rows == 1 else mod_rows
    mrow = mod_row0 // mblk
    tri = jnp.asarray(np.triu(np.ones((tm, tm), np.float32), 1), BF16)
    rb = jnp.broadcast_to(router_b.astype(F32)[:, None], (N_EXPERTS, tm))
    mod_spec = lambda c: pl.BlockSpec((1, mblk, D_MODEL), lambda i: (layer, mrow, c))
    oi, of, cnt = pl.pallas_call(
        functools.partial(_router_kernel, mod_rows=mod_rows),
        grid=(nt,),
        in_specs=[pl.BlockSpec((tm, D_MODEL), lambda i: (i, 0)), mod_spec(3), mod_spec(4),
                  pl.BlockSpec((N_EXPERTS, D_MODEL), lambda i: (0, 0)),
                  pl.BlockSpec((N_EXPERTS, tm), lambda i: (0, 0)),
                  pl.BlockSpec((tm, tm), lambda i: (0, 0))],
        out_specs=[pl.BlockSpec((1, 8, tm), lambda i: (i, 0, 0)),
                   pl.BlockSpec((1, 8, tm), lambda i: (i, 0, 0)),
                   pl.BlockSpec((BIN_ROWS, 128), lambda i: (0, 0))],
        out_shape=[jax.ShapeDtypeStruct((nt, 8, tm), I32),
                   jax.ShapeDtypeStruct((nt, 8, tm), F32),
                   jax.ShapeDtypeStruct((BIN_ROWS, 128), I32)],
        scratch_shapes=[pltpu.VMEM((BIN_ROWS, 128), F32)],
        compiler_params=_cparams(("arbitrary",)),
        name="router",
    )(x1, mod, mod, router_wt, rb, tri)
    return _unpack_route(oi, of, cnt)


def _unpack_route(oi, of, cnt):
    flat = lambda a, r: a[:, r, :].reshape(-1)
    return (flat(oi, 0), flat(oi, 1), flat(oi, 2), flat(oi, 3), flat(of, 0), flat(of, 1),
            cnt[:N_BINS, 0])


POST_TM = 512


def _post_mix(mixed, x, mod, layer, mod_row0, w_out_bf, ln_g, ln_b, router_wt, router_b):
    rows = x.shape[0]
    tm = POST_TM
    nt = rows // tm
    mrow = mod_row0 // 8
    tri = jnp.asarray(np.triu(np.ones((tm, tm), np.float32), 1), BF16)
    rb = jnp.broadcast_to(router_b.astype(F32)[:, None], (N_EXPERTS, tm))
    mod_spec = lambda c: pl.BlockSpec((1, 8, D_MODEL), lambda i: (layer, mrow, c))
    row_spec = pl.BlockSpec((tm, D_MODEL), lambda i: (i, 0))
    const2 = lambda a: pl.BlockSpec(a.shape, lambda i: (0, 0))
    x1, oi, of, cnt = pl.pallas_call(
        _post_mix_kernel,
        grid=(nt,),
        in_specs=[row_spec, row_spec, mod_spec(2), const2(w_out_bf), const2(ln_g), const2(ln_b),
                  mod_spec(3), mod_spec(4), const2(router_wt), const2(rb), const2(tri)],
        out_specs=[row_spec,
                   pl.BlockSpec((1, 8, tm), lambda i: (i, 0, 0)),
                   pl.BlockSpec((1, 8, tm), lambda i: (i, 0, 0)),
                   pl.BlockSpec((BIN_ROWS, 128), lambda i: (0, 0))],
        out_shape=[jax.ShapeDtypeStruct((rows, D_MODEL), F32),
                   jax.ShapeDtypeStruct((nt, 8, tm), I32),
                   jax.ShapeDtypeStruct((nt, 8, tm), F32),
                   jax.ShapeDtypeStruct((BIN_ROWS, 128), I32)],
        scratch_shapes=[pltpu.VMEM((BIN_ROWS, 128), F32)],
        compiler_params=_cparams(("arbitrary",)),
        name="post_mix",
    )(mixed, x, mod, w_out_bf, ln_g, ln_b, mod, mod, router_wt, rb, tri)
    return x1, _unpack_route(oi, of, cnt)


DMA_UNROLL = 8
ROW_GROUPS = MOE_TM // DMA_UNROLL


def _for_rows(n, fn):
    def body_u(j, carry):
        for u in range(DMA_UNROLL):
            fn(j, u)
        return carry
    lax.fori_loop(0, n // DMA_UNROLL, body_u, 0)

    def body_1(r, carry):
        fn(r // DMA_UNROLL, r % DMA_UNROLL)
        return carry
    lax.fori_loop((n // DMA_UNROLL) * DMA_UNROLL, n, body_1, 0)


def _wait_rows(n, buf, sem):
    p = ROW_GROUPS
    while p >= 1:
        @pl.when((n & (p * DMA_UNROLL)) != 0)
        def _(p=p):
            pltpu.make_async_copy(buf.at[pl.ds(0, p)], buf.at[pl.ds(0, p)], sem).wait()
        p //= 2
    p = DMA_UNROLL // 2
    while p >= 1:
        @pl.when((n & p) != 0)
        def _(p=p):
            pltpu.make_async_copy(buf.at[0, pl.ds(0, p)], buf.at[0, pl.ds(0, p)], sem).wait()
        p //= 2


def _moe_sorted_kernel(ea_ref, eb_ref, nvalid_ref, pos_ref,
                       x_hbm, sh_ref, sc_ref, gate_ref, rw_ref,
                       wga_ref, wua_ref, wda_ref, wgb_ref, wub_ref, wdb_ref,
                       lng_ref, lnb_ref,
                       out_hbm,
                       inv, xbuf, obuf, wg_a, wu_a, wd_a, wg_b, wu_b, wd_b, gsem, ssem):
    i = pl.program_id(0)
    nt = pl.num_programs(0)
    slot = i % 2
    n_tok = pos_ref.shape[0]

    def start_gather(t, dst_slot):
        def one(j, u):
            tok = inv[t * MOE_TM + j * DMA_UNROLL + u]
            pltpu.make_async_copy(x_hbm.at[pl.ds(tok, 1)], xbuf.at[dst_slot, j, pl.ds(u, 1)],
                                  gsem.at[dst_slot]).start()
        _for_rows(nvalid_ref[t], one)

    @pl.when(i == 0)
    def _():
        def body(j, carry):
            for u in range(DMA_UNROLL):
                t = j * DMA_UNROLL + u
                inv[pos_ref[t]] = t
            return carry
        lax.fori_loop(0, n_tok // DMA_UNROLL, body, 0)
        xbuf[...] = jnp.zeros_like(xbuf)
        start_gather(0, 0)

    @pl.when(i >= 2)
    def _():
        _wait_rows(nvalid_ref[jnp.maximum(i - 2, 0)], obuf.at[slot], ssem.at[slot])

    @pl.when(nvalid_ref[i] > 0)
    def _():
        _wait_rows(nvalid_ref[i], xbuf.at[slot], gsem.at[slot])

        @pl.when(i + 1 < nt)
        def _():
            start_gather(jnp.minimum(i + 1, nt - 1), 1 - slot)

        @pl.when(jnp.logical_or(i == 0, ea_ref[i] != ea_ref[jnp.maximum(i - 1, 0)]))
        def _():
            wg_a[...] = wga_ref[0, 0].astype(BF16)
            wu_a[...] = wua_ref[0, 0].astype(BF16)
            wd_a[...] = wda_ref[0, 0].astype(BF16)

        @pl.when(jnp.logical_or(i == 0, eb_ref[i] != eb_ref[jnp.maximum(i - 1, 0)]))
        def _():
            wg_b[...] = wgb_ref[0, 0].astype(BF16)
            wu_b[...] = wub_ref[0, 0].astype(BF16)
            wd_b[...] = wdb_ref[0, 0].astype(BF16)

        x = xbuf[slot].reshape(MOE_TM, D_MODEL)
        h2 = (x * (1.0 + sc_ref[0][:1]) + sh_ref[0][:1]).astype(BF16)

        def expert(wg, wu, wd):
            a = _silu(jnp.dot(h2, wg[...], preferred_element_type=F32)) * \
                jnp.dot(h2, wu[...], preferred_element_type=F32)
            return jnp.dot(a.astype(BF16), wd[...], preferred_element_type=F32)

        aff = 1.0 / (1.0 + jnp.exp(-jnp.dot(h2, rw_ref[...], preferred_element_type=F32)))
        lane = lax.broadcasted_iota(I32, aff.shape, 1)
        a_lo = jnp.sum(jnp.where(lane == ea_ref[i], aff, 0.0), axis=-1, keepdims=True)
        a_hi = jnp.sum(jnp.where(lane == eb_ref[i], aff, 0.0), axis=-1, keepdims=True)
        a_sum = a_lo + a_hi
        y = (a_lo / a_sum) * expert(wg_a, wu_a, wd_a) + (a_hi / a_sum) * expert(wg_b, wu_b, wd_b)
        out = _ln(DN_ALPHA * x + (1.0 + gate_ref[0][:1]) * y, lng_ref[...], lnb_ref[...])
        obuf[slot] = out.reshape(ROW_GROUPS, DMA_UNROLL, D_MODEL)

        def scatter_one(j, u):
            tok = inv[i * MOE_TM + j * DMA_UNROLL + u]
            pltpu.make_async_copy(obuf.at[slot, j, pl.ds(u, 1)], out_hbm.at[pl.ds(tok, 1)],
                                  ssem.at[slot]).start()
        _for_rows(nvalid_ref[i], scatter_one)

    @pl.when(i == nt - 1)
    def _():
        _wait_rows(nvalid_ref[jnp.maximum(i - 1, 0)], obuf.at[1 - slot], ssem.at[1 - slot])
        _wait_rows(nvalid_ref[i], obuf.at[slot], ssem.at[slot])


def _moe_sorted(x1, mod, layer, mod_row0, route, rw_pad, wg, wu, wd, ln_g, ln_b):
    _, _, bin_id, rank, _, _, counts = route
    rows = x1.shape[0]
    nt = rows // MOE_TM + N_BINS
    ns = nt * MOE_TM
    tiles = (counts + MOE_TM - 1) // MOE_TM
    tile_end = jnp.cumsum(tiles)
    row_start = (tile_end - tiles) * MOE_TM
    pos = row_start[bin_id] + rank
    total = tile_end[-1]
    t = jnp.arange(nt, dtype=I32)
    tile_bin = jnp.searchsorted(tile_end, jnp.minimum(t, total - 1), side="right").astype(I32)
    tile_bin = jnp.clip(tile_bin, 0, N_BINS - 1)
    bin_lo = jnp.asarray([g * 4 + _PAIR_LO[p] for g in range(N_GROUPS) for p in range(N_PAIRS)], I32)
    bin_hi = jnp.asarray([g * 4 + _PAIR_HI[p] for g in range(N_GROUPS) for p in range(N_PAIRS)], I32)
    tile_ea, tile_eb = bin_lo[tile_bin], bin_hi[tile_bin]
    nvalid = jnp.clip(counts[tile_bin] - (t - (tile_end - tiles)[tile_bin]) * MOE_TM, 0, MOE_TM)
    nvalid = jnp.where(t < total, nvalid, 0).astype(I32)
    mrow = mod_row0 // 8
    mod_spec = lambda c: pl.BlockSpec((1, 8, D_MODEL), lambda i, *_: (layer, mrow, c))
    w_spec = lambda shape, ref_idx: pl.BlockSpec(
        (1, 1) + shape, lambda i, ea_r, eb_r, *_: (layer, (ea_r, eb_r)[ref_idx][i], 0, 0))
    any_spec = pl.BlockSpec(memory_space=pl.ANY)
    w_bf = lambda shape: pltpu.VMEM(shape, BF16)
    grid_spec = pltpu.PrefetchScalarGridSpec(
        num_scalar_prefetch=4,
        grid=(nt,),
        in_specs=[any_spec,
                  mod_spec(3), mod_spec(4), mod_spec(5),
                  pl.BlockSpec((D_MODEL, 128), lambda i, *_: (0, 0)),
                  w_spec((D_MODEL, D_FF), 0), w_spec((D_MODEL, D_FF), 0), w_spec((D_FF, D_MODEL), 0),
                  w_spec((D_MODEL, D_FF), 1), w_spec((D_MODEL, D_FF), 1), w_spec((D_FF, D_MODEL), 1),
                  pl.BlockSpec((1, D_MODEL), lambda i, *_: (0, 0)),
                  pl.BlockSpec((1, D_MODEL), lambda i, *_: (0, 0))],
        out_specs=any_spec,
        scratch_shapes=[pltpu.SMEM((ns,), I32),
                        pltpu.VMEM((2, ROW_GROUPS, DMA_UNROLL, D_MODEL), F32),
                        pltpu.VMEM((2, ROW_GROUPS, DMA_UNROLL, D_MODEL), F32),
                        w_bf((D_MODEL, D_FF)), w_bf((D_MODEL, D_FF)), w_bf((D_FF, D_MODEL)),
                        w_bf((D_MODEL, D_FF)), w_bf((D_MODEL, D_FF)), w_bf((D_FF, D_MODEL)),
                        pltpu.SemaphoreType.DMA((2,)),
                        pltpu.SemaphoreType.DMA((2,))],
    )
    return pl.pallas_call(
        _moe_sorted_kernel,
        grid_spec=grid_spec,
        out_shape=jax.ShapeDtypeStruct((rows, D_MODEL), F32),
        compiler_params=_cparams(("arbitrary",)),
        name="moe_sorted",
    )(tile_ea, tile_eb, nvalid, pos, x1, mod, mod, mod, rw_pad,
      wg, wu, wd, wg, wu, wd, ln_g, ln_b)


def _moe_dense_kernel(x_ref, dw_ref, sh_ref, sc_ref, gate_ref, wg_ref, wu_ref, wd_ref,
                      lng_ref, lnb_ref, o_ref, acc):
    e = pl.program_id(0)

    @pl.when(e == 0)
    def _():
        acc[...] = jnp.zeros_like(acc)

    x = x_ref[...]
    h2 = (x * (1.0 + sc_ref[0]) + sh_ref[0]).astype(BF16)
    a = _silu(jnp.dot(h2, wg_ref[0, 0].astype(BF16), preferred_element_type=F32)) * \
        jnp.dot(h2, wu_ref[0, 0].astype(BF16), preferred_element_type=F32)
    y = jnp.dot(a.astype(BF16), wd_ref[0, 0].astype(BF16), preferred_element_type=F32)
    acc[...] = acc[...] + dw_ref[0][:, :1] * y

    @pl.when(e == pl.num_programs(0) - 1)
    def _():
        o_ref[...] = _ln(DN_ALPHA * x + (1.0 + gate_ref[0]) * acc[...], lng_ref[...], lnb_ref[...])


def _moe_dense(x1, mod, layer, route, wg, wu, wd, ln_g, ln_b):
    ea, eb, _, _, w_lo, w_hi, _ = route
    nb = x1.shape[0]
    eids = jnp.arange(N_EXPERTS, dtype=I32)[:, None]
    dw = jnp.where(eids == ea[None, :], w_lo[None, :], 0.0) + \
        jnp.where(eids == eb[None, :], w_hi[None, :], 0.0)
    dw = jnp.broadcast_to(dw[:, :, None], (N_EXPERTS, nb, 128))
    mod_spec = lambda c: pl.BlockSpec((1, nb, D_MODEL), lambda e: (layer, 0, c))
    const2 = lambda a: pl.BlockSpec(a.shape, lambda e: (0, 0))
    return pl.pallas_call(
        _moe_dense_kernel,
        grid=(N_EXPERTS,),
        in_specs=[const2(x1), pl.BlockSpec((1, nb, 128), lambda e: (e, 0, 0)),
                  mod_spec(3), mod_spec(4), mod_spec(5),
                  pl.BlockSpec((1, 1, D_MODEL, D_FF), lambda e: (layer, e, 0, 0)),
                  pl.BlockSpec((1, 1, D_MODEL, D_FF), lambda e: (layer, e, 0, 0)),
                  pl.BlockSpec((1, 1, D_FF, D_MODEL), lambda e: (layer, e, 0, 0)),
                  const2(ln_g), const2(ln_b)],
        out_specs=const2(x1),
        out_shape=jax.ShapeDtypeStruct(x1.shape, F32),
        scratch_shapes=[pltpu.VMEM(x1.shape, F32)],
        compiler_params=_cparams(("arbitrary",)),
        name="moe_dense",
    )(x1, dw, mod, mod, mod, wg, wu, wd, ln_g, ln_b)


def kernel(x_prompt, x_sample, state_ret, cache_swa_k, cache_swa_v, c_prompt, c_sample, w_in, w_out, ret_gn_gain, swa_sinks, rel_bias_table, ada_w, ada_b, ln1_g, ln1_b, ln2_g, ln2_b, router_w, router_b, exp_w_gate, exp_w_up, exp_w_down):
    seq = x_prompt.shape[1]
    nb = x_sample.shape[0]
    past_len = 16384
    assert x_prompt.shape[0] == 1 and x_sample.shape[1] == 1

    c_all = jnp.concatenate([c_sample, c_prompt, jnp.zeros((7, D_MODEL), F32)], axis=0)
    mod = _ada(c_all, ada_w, ada_b)
    prompt_row = nb

    bias_tabs = _bias_tables(rel_bias_table.astype(F32))
    gl, decay, xi, zeta, gamma1 = _ret_consts()
    rope_p = _rope_tables(jnp.arange(seq, dtype=I32))
    rope_s = _rope_tables(jnp.full((nb,), past_len, I32))
    router_wt = router_w.astype(F32).T
    rw_pad = jnp.pad(router_w.astype(BF16), ((0, 0), (0, 128 - N_EXPERTS)))
    zeta_p = jnp.tile(zeta, (INPROJ_TM // CHUNK, 1))
    zeta_s = jnp.ones((nb, RET_W), F32)

    xp = x_prompt.reshape(seq, D_MODEL)
    xs = x_sample.reshape(nb, D_MODEL)
    st_p, k_p, v_p, st_s, k_s, v_s = [], [], [], [], [], []
    for l in range(DEPTH):
        w_in_bf = w_in[l].astype(BF16)
        w_out_bf = w_out[l].astype(BF16)
        experts = (exp_w_gate.astype(F32), exp_w_up.astype(F32), exp_w_down.astype(F32))
        gain = ret_gn_gain[l].astype(F32)
        sinks = swa_sinks[l].astype(F32)
        row = lambda a: a[l].astype(F32).reshape(1, D_MODEL)

        proj = _inproj(xp, mod, l, prompt_row, 1, w_in_bf, rope_p, zeta_p, INPROJ_TM)
        mixed, st = _mix_prompt(proj, (gl, decay, xi), bias_tabs, sinks, gain.reshape(1, RET_W))
        skv = proj[6]
        st_p.append(st.reshape(1, N_RET_HEADS, HEAD_DIM, HEAD_DIM))
        k_p.append(skv[seq - CHUNK:, :SWA_KVW].reshape(1, CHUNK, N_KV_HEADS, HEAD_DIM))
        v_p.append(skv[seq - CHUNK:, SWA_KVW:].reshape(1, CHUNK, N_KV_HEADS, HEAD_DIM))
        x1, route = _post_mix(mixed, xp, mod, l, prompt_row, w_out_bf, row(ln1_g), row(ln1_b),
                              router_wt, router_b)
        xp = _moe_sorted(x1, mod, l, prompt_row, route, rw_pad, *experts, row(ln2_g), row(ln2_b))

        proj = _inproj(xs, mod, l, 0, nb, w_in_bf, rope_s, zeta_s, nb)
        x1, nst, nk, nv = _mix_sample(proj, xs, mod, l, w_out_bf, gamma1, bias_tabs[2], sinks, gain,
                                      row(ln1_g), row(ln1_b), state_ret[l].astype(F32),
                                      cache_swa_k[l], cache_swa_v[l])
        st_s.append(nst)
        k_s.append(nk)
        v_s.append(nv)
        route = _router(x1, mod, l, 0, nb, router_wt, router_b, nb)
        xs = _moe_dense(x1, mod, l, route, *experts, row(ln2_g), row(ln2_b))

    return (xp.reshape(1, seq, D_MODEL), xs.reshape(nb, 1, D_MODEL),
            jnp.stack(st_p), jnp.stack(k_p), jnp.stack(v_p),
            jnp.stack(st_s), jnp.stack(k_s), jnp.stack(v_s))
```

```python
import functools
import math

import numpy as np
import jax
import jax.numpy as jnp
from jax import lax
from jax.experimental import pallas as pl
from jax.experimental.pallas import tpu as pltpu

F32 = jnp.float32
BF16 = jnp.bfloat16
I32 = jnp.int32

D_MODEL = 1024
DEPTH = 2
HEAD_DIM = 64
N_RET_HEADS = 8
N_SWA_HEADS = 8
N_KV_HEADS = 2
GQA_GROUP = N_SWA_HEADS // N_KV_HEADS
RET_W = N_RET_HEADS * HEAD_DIM
SWA_QW = N_SWA_HEADS * HEAD_DIM
SWA_KVW = N_KV_HEADS * HEAD_DIM
PROJ_W = 4 * RET_W + SWA_QW + 2 * SWA_KVW
CHUNK = 128
ROPE_BASE = 10000.0
N_BUCKETS = 32
MAX_DISTANCE = 128
N_EXPERTS = 16
N_GROUPS = 4
EXPERTS_PER_GROUP = 4
D_FF = 512
LN_EPS = 1e-5
DN_ALPHA = (2 * DEPTH) ** 0.25
QK_SCALE = HEAD_DIM ** -0.5
NEG_BIG = -1e30

N_PAIRS = 6
N_BINS = N_GROUPS * N_PAIRS
BIN_ROWS = 32
MOE_TM = 256
INPROJ_TM = 512
VMEM_LIMIT = 56 * 1024 * 1024

_PAIR_LO = (0, 0, 0, 1, 1, 2)
_PAIR_HI = (1, 2, 3, 2, 3, 3)


def _cparams(sem):
    return pltpu.CompilerParams(dimension_semantics=sem, vmem_limit_bytes=VMEM_LIMIT)


def _ln(v, g, b):
    mu = jnp.mean(v, axis=-1, keepdims=True)
    c = v - mu
    var = jnp.mean(c * c, axis=-1, keepdims=True)
    return c * lax.rsqrt(var + LN_EPS) * g + b


def _silu(v):
    return v * (1.0 / (1.0 + jnp.exp(-v)))


def _ada_kernel(c_ref, w_ref, b_ref, o_ref):
    o_ref[0] = jnp.dot(c_ref[...].astype(BF16), w_ref[0].astype(BF16),
                       preferred_element_type=F32) + b_ref[0]


def _ada(c_all, ada_w, ada_b):
    rows = c_all.shape[0]
    nt = 6 * D_MODEL // 1024
    return pl.pallas_call(
        _ada_kernel,
        grid=(DEPTH, nt),
        in_specs=[pl.BlockSpec((rows, D_MODEL), lambda l, j: (0, 0)),
                  pl.BlockSpec((1, D_MODEL, 1024), lambda l, j: (l, 0, j)),
                  pl.BlockSpec((1, 1, 1024), lambda l, j: (l, 0, j))],
        out_specs=pl.BlockSpec((1, rows, 1024), lambda l, j: (l, 0, j)),
        out_shape=jax.ShapeDtypeStruct((DEPTH, rows, 6 * D_MODEL), F32),
        compiler_params=_cparams(("arbitrary", "arbitrary")),
        name="ada",
    )(c_all, ada_w, ada_b.reshape(DEPTH, 1, 6 * D_MODEL))


def _bias_kernel(tab_ref, bkt_ref, full_ref, first_ref, row_ref):
    bkt = bkt_ref[...]
    col = lax.broadcasted_iota(I32, bkt.shape, 1)
    for h in range(N_SWA_HEADS):
        acc = jnp.full(bkt.shape, NEG_BIG, F32)
        for b in range(N_BUCKETS):
            acc = jnp.where(bkt == b, tab_ref[b, h], acc)
        full_ref[h] = acc
        first_ref[h] = jnp.where(col < CHUNK, NEG_BIG, acc)
        row_ref[h:h + 1, :] = acc[CHUNK - 1:CHUNK, CHUNK:]


def _t5_bucket(rel):
    max_exact = N_BUCKETS // 2
    relf = jnp.maximum(rel, 1).astype(F32)
    large = max_exact + (jnp.log(relf / max_exact) / math.log(MAX_DISTANCE / max_exact)
                         * (N_BUCKETS - max_exact)).astype(I32)
    large = jnp.minimum(large, N_BUCKETS - 1)
    return jnp.where(rel < max_exact, rel, large)


def _bias_tables(rel_bias_table):
    qi = jnp.arange(CHUNK)
    si = jnp.arange(2 * CHUNK)
    rel = CHUNK + qi[:, None] - si[None, :]
    band = (rel >= 0) & (rel < CHUNK)
    bkt = jnp.where(band, _t5_bucket(jnp.maximum(rel, 0)), -1).astype(I32)
    return pl.pallas_call(
        _bias_kernel,
        in_specs=[pl.BlockSpec(memory_space=pltpu.SMEM),
                  pl.BlockSpec((CHUNK, 2 * CHUNK), lambda: (0, 0))],
        out_specs=[pl.BlockSpec((N_SWA_HEADS, CHUNK, 2 * CHUNK), lambda: (0, 0, 0)),
                   pl.BlockSpec((N_SWA_HEADS, CHUNK, 2 * CHUNK), lambda: (0, 0, 0)),
                   pl.BlockSpec((N_SWA_HEADS, CHUNK), lambda: (0, 0))],
        out_shape=[jax.ShapeDtypeStruct((N_SWA_HEADS, CHUNK, 2 * CHUNK), F32),
                   jax.ShapeDtypeStruct((N_SWA_HEADS, CHUNK, 2 * CHUNK), F32),
                   jax.ShapeDtypeStruct((N_SWA_HEADS, CHUNK), F32)],
        name="t5_bias",
    )(rel_bias_table, bkt)


def _rotary(v, cos, s_lo, s_hi):
    outs = []
    for j in range(RET_W // 128):
        blk = v[:, j * 128:(j + 1) * 128]
        outs.append(blk * cos + pltpu.roll(blk, 96, 1) * s_lo + pltpu.roll(blk, 32, 1) * s_hi)
    return jnp.concatenate(outs, axis=-1)


def _inproj_kernel(x_ref, sh_ref, sc_ref, w_ref, cs_ref, zeta_ref,
                   q_ref, k_ref, kz_ref, v_ref, g_ref, sq_ref, skv_ref, *, mod_rows):
    sh = sh_ref[0][:mod_rows]
    sc = sc_ref[0][:mod_rows]
    h = (x_ref[...] * (1.0 + sc) + sh).astype(BF16)
    cs = cs_ref[...]
    first = (lax.broadcasted_iota(I32, cs.shape, 1) % HEAD_DIM) < (HEAD_DIM // 2)
    cos = jnp.where(first, cs, pltpu.roll(cs, 32, 1))
    s_lo = jnp.where(first, -pltpu.roll(cs, 96, 1), 0.0)
    s_hi = jnp.where(first, 0.0, cs)

    def proj(lo, hi):
        return jnp.dot(h, w_ref[:, lo:hi], preferred_element_type=F32)

    q_ref[...] = _rotary(proj(0, RET_W), cos, s_lo, s_hi).astype(BF16)
    k = _rotary(proj(RET_W, 2 * RET_W), cos, s_lo, s_hi) * QK_SCALE
    k_ref[...] = k.astype(BF16)
    kz_ref[...] = (k * zeta_ref[...]).astype(BF16)
    v_ref[...] = proj(2 * RET_W, 3 * RET_W).astype(BF16)
    g_ref[...] = proj(3 * RET_W, 4 * RET_W)
    sq_ref[...] = (proj(4 * RET_W, 4 * RET_W + SWA_QW) * QK_SCALE).astype(BF16)
    skv_ref[...] = proj(4 * RET_W + SWA_QW, PROJ_W)


def _inproj(x, rows, mod, layer, mod_row0, mod_rows, w_in_bf, rope, zeta_tile, tm):
    mblk = 8 if mod_rows == 1 else mod_rows
    mrow = mod_row0 // mblk
    row_spec = lambda w: pl.BlockSpec((tm, w), lambda i: (i, 0))
    mod_spec = lambda c: pl.BlockSpec((1, mblk, D_MODEL), lambda i: (layer, mrow, c))
    return pl.pallas_call(
        functools.partial(_inproj_kernel, mod_rows=mod_rows),
        grid=(rows // tm,),
        in_specs=[row_spec(D_MODEL), mod_spec(0), mod_spec(1),
                  pl.BlockSpec((D_MODEL, PROJ_W), lambda i: (0, 0)),
                  row_spec(128),
                  pl.BlockSpec((tm, RET_W), lambda i: (0, 0))],
        out_specs=[row_spec(RET_W), row_spec(RET_W), row_spec(RET_W), row_spec(RET_W), row_spec(RET_W),
                   row_spec(SWA_QW), row_spec(2 * SWA_KVW)],
        out_shape=[jax.ShapeDtypeStruct((rows, RET_W), BF16),
                   jax.ShapeDtypeStruct((rows, RET_W), BF16),
                   jax.ShapeDtypeStruct((rows, RET_W), BF16),
                   jax.ShapeDtypeStruct((rows, RET_W), BF16),
                   jax.ShapeDtypeStruct((rows, RET_W), F32),
                   jax.ShapeDtypeStruct((rows, SWA_QW), BF16),
                   jax.ShapeDtypeStruct((rows, 2 * SWA_KVW), F32)],
        compiler_params=_cparams(("arbitrary",)),
        name="inproj",
    )(x, mod, mod, w_in_bf, rope, zeta_tile)


def _rope_tables(pos):
    half = HEAD_DIM // 2
    inv = ROPE_BASE ** (-jnp.arange(half, dtype=F32) / half)
    ang = pos.astype(F32)[:, None] * inv[None, :]
    cos, sin = jnp.cos(ang), jnp.sin(ang)
    return jnp.concatenate([cos, sin, cos, sin], axis=-1)


def _mix_prompt_kernel(gl_ref, sink_ref,
                       q_ref, k_ref, kz_ref, v_ref, g_ref, sq_ref, kvc_ref, kvp_ref,
                       decay_ref, xi_ref, bias_ref, bias0_ref, gain_ref,
                       mixed_ref, st_ref, state):
    i = pl.program_id(0)

    @pl.when(i == 0)
    def _():
        state[...] = jnp.zeros_like(state)

    q = q_ref[...]
    k = k_ref[...]
    v = v_ref[...]
    kz = kz_ref[...]

    intra, inter = [], []
    for h in range(N_RET_HEADS):
        sl = slice(h * HEAD_DIM, (h + 1) * HEAD_DIM)
        qh, kh, vh = q[:, sl], k[:, sl], v[:, sl]
        s = lax.dot_general(qh, kh, (((1,), (1,)), ((), ())), preferred_element_type=F32)
        p = (s * decay_ref[h]).astype(BF16)
        intra.append(jnp.dot(p, vh, preferred_element_type=F32))
        s_old = state[h]
        inter.append(jnp.dot(qh, s_old.astype(BF16), preferred_element_type=F32))
        upd = lax.dot_general(kz[:, sl], vh, (((0,), (0,)), ((), ())), preferred_element_type=F32)
        state[h] = gl_ref[h] * s_old + upd

    xi = xi_ref[...]
    g = g_ref[...]
    ret_parts = []
    for h in range(N_RET_HEADS):
        sl = slice(h * HEAD_DIM, (h + 1) * HEAD_DIM)
        o = intra[h] + inter[h] * xi[:, sl]
        mu = jnp.mean(o, axis=-1, keepdims=True)
        c = o - mu
        var = jnp.mean(c * c, axis=-1, keepdims=True)
        ret_parts.append(c * lax.rsqrt(var + LN_EPS))
    ret_o = jnp.concatenate(ret_parts, axis=-1) * gain_ref[...] * _silu(g)

    sq = sq_ref[...]
    kvc = kvc_ref[...]
    kvp = kvp_ref[...]
    swa_parts = []
    for kv in range(N_KV_HEADS):
        ks = slice(kv * HEAD_DIM, (kv + 1) * HEAD_DIM)
        vs = slice(SWA_KVW + kv * HEAD_DIM, SWA_KVW + (kv + 1) * HEAD_DIM)
        kk = jnp.concatenate([kvp[:, ks], kvc[:, ks]], axis=0).astype(BF16)
        vv = jnp.concatenate([kvp[:, vs], kvc[:, vs]], axis=0).astype(BF16)
        for j in range(GQA_GROUP):
            hh = kv * GQA_GROUP + j
            qh = sq[:, hh * HEAD_DIM:(hh + 1) * HEAD_DIM]
            s = lax.dot_general(qh, kk, (((1,), (1,)), ((), ())), preferred_element_type=F32)
            s = s + jnp.where(i == 0, bias0_ref[hh], bias_ref[hh])
            sink = sink_ref[hh]
            m = jnp.maximum(jnp.max(s, axis=-1, keepdims=True), sink)
            p = jnp.exp(s - m)
            den = jnp.sum(p, axis=-1, keepdims=True) + jnp.exp(sink - m)
            swa_parts.append(jnp.dot((p / den).astype(BF16), vv, preferred_element_type=F32))
    swa_o = jnp.concatenate(swa_parts, axis=-1)

    mixed_ref[...] = jnp.concatenate([ret_o, swa_o], axis=-1).astype(BF16)

    @pl.when(i == pl.num_programs(0) - 1)
    def _():
        st_ref[...] = state[...]


def _mix_prompt(proj, ret_consts, bias_tabs, sinks, gain):
    q, k, kz, v, g, sq, skv = proj
    rows = q.shape[0]
    nblk = rows // CHUNK
    gl, decay, xi = ret_consts
    bias_full, bias_first, _ = bias_tabs
    row_spec = lambda w: pl.BlockSpec((CHUNK, w), lambda i: (i, 0))
    const2 = lambda a: pl.BlockSpec(a.shape, lambda i: (0, 0))
    const3 = lambda a: pl.BlockSpec(a.shape, lambda i: (0, 0, 0))
    smem = pl.BlockSpec(memory_space=pltpu.SMEM)
    return pl.pallas_call(
        _mix_prompt_kernel,
        grid=(nblk,),
        in_specs=[smem, smem,
                  row_spec(RET_W), row_spec(RET_W), row_spec(RET_W), row_spec(RET_W), row_spec(RET_W),
                  row_spec(SWA_QW), row_spec(2 * SWA_KVW),
                  pl.BlockSpec((CHUNK, 2 * SWA_KVW), lambda i: (jnp.maximum(i - 1, 0), 0)),
                  const3(decay), const2(xi),
                  const3(bias_full), const3(bias_first), const2(gain)],
        out_specs=[row_spec(D_MODEL),
                   pl.BlockSpec((N_RET_HEADS, HEAD_DIM, HEAD_DIM), lambda i: (0, 0, 0))],
        out_shape=[jax.ShapeDtypeStruct((rows, D_MODEL), BF16),
                   jax.ShapeDtypeStruct((N_RET_HEADS, HEAD_DIM, HEAD_DIM), F32)],
        scratch_shapes=[pltpu.VMEM((N_RET_HEADS, HEAD_DIM, HEAD_DIM), F32)],
        compiler_params=_cparams(("arbitrary",)),
        name="mix_prompt",
    )(gl, sinks, q, k, kz, v, g, sq, skv, skv, decay, xi, bias_full, bias_first, gain)


def _ret_consts():
    lg = jnp.log(1.0 - 2.0 ** (-5.0 - jnp.arange(N_RET_HEADS, dtype=F32)))
    idx = jnp.arange(CHUNK, dtype=F32)
    diff = idx[:, None] - idx[None, :]
    decay = jnp.where(diff >= 0, jnp.exp(jnp.maximum(diff, 0.0)[None] * lg[:, None, None]), 0.0)
    xi = jnp.exp((idx + 1.0)[:, None] * lg[None, :])
    zeta = jnp.exp((CHUNK - 1.0 - idx)[:, None] * lg[None, :])
    gl = jnp.exp(CHUNK * lg)
    expand = lambda a: jnp.repeat(a, HEAD_DIM, axis=1)
    return gl, decay, expand(xi), expand(zeta), jnp.exp(1.0 * lg)


SAMPLE_BB = 8


def _mix_sample_kernel(g1_ref, sink_ref,
                       qt_ref, kt_ref, v3_ref, sq3_ref, knew_ref, vnew_ref,
                       st_ref, ck_ref, cv_ref, g3_ref, x_ref, gate_ref,
                       wout_ref, brow_ref, gain_ref, lng_ref, lnb_ref,
                       x1_ref, nst_ref, nk_ref, nv_ref, ret_scr, swa_scr):
    i = pl.program_id(0)
    row = lax.broadcasted_iota(I32, (N_SWA_HEADS, 2 * HEAD_DIM), 0)
    lane = lax.broadcasted_iota(I32, (N_SWA_HEADS, 2 * HEAD_DIM), 1)
    own_half = (row // GQA_GROUP) == (lane // HEAD_DIM)
    sink_col = jnp.concatenate(
        [jnp.full((1, 1), sink_ref[hh], F32) for hh in range(N_SWA_HEADS)], axis=0)
    brow = brow_ref[...]
    qt = qt_ref[0]
    kt = kt_ref[0]

    for b in range(SAMPLE_BB):
        o_rows = []
        for h in range(N_RET_HEADS):
            rs = slice(h * HEAD_DIM, (h + 1) * HEAD_DIM)
            s_old = st_ref[b, rs, :]
            kcol = kt[rs, b:b + 1]
            qcol = qt[rs, b:b + 1]
            vrow = v3_ref[b, h:h + 1, :]
            s_new = g1_ref[h] * s_old + kcol * vrow
            nst_ref[b, rs, :] = s_new
            o_rows.append(jnp.sum(qcol * s_new, axis=0, keepdims=True))
        ret_scr[i * SAMPLE_BB + b] = jnp.concatenate(o_rows, axis=0)

        kk = jnp.concatenate([ck_ref[b, 1:, :], knew_ref[b:b + 1, :]], axis=0)
        vv = jnp.concatenate([cv_ref[b, 1:, :], vnew_ref[b:b + 1, :]], axis=0)
        nk_ref[b] = kk
        nv_ref[b] = vv
        q8 = sq3_ref[b]
        qblk = jnp.where(own_half, jnp.concatenate([q8, q8], axis=-1), 0.0).astype(BF16)
        s = lax.dot_general(qblk, kk.astype(BF16), (((1,), (1,)), ((), ())),
                            preferred_element_type=F32) + brow
        m = jnp.maximum(jnp.max(s, axis=-1, keepdims=True), sink_col)
        p = jnp.exp(s - m)
        den = jnp.sum(p, axis=-1, keepdims=True) + jnp.exp(sink_col - m)
        o = jnp.dot((p / den).astype(BF16), vv.astype(BF16), preferred_element_type=F32)
        swa_scr[i * SAMPLE_BB + b] = jnp.where(own_half[:, :HEAD_DIM], o[:, :HEAD_DIM], o[:, HEAD_DIM:])

    @pl.when(i == pl.num_programs(0) - 1)
    def _():
        y = jnp.zeros(x_ref.shape, F32)
        for h in range(N_RET_HEADS):
            o = ret_scr[:, h, :]
            mu = jnp.mean(o, axis=-1, keepdims=True)
            c = o - mu
            var = jnp.mean(c * c, axis=-1, keepdims=True)
            r = c * lax.rsqrt(var + LN_EPS) * gain_ref[h:h + 1, :] * _silu(g3_ref[h])
            y = y + jnp.dot(r.astype(BF16), wout_ref[h * HEAD_DIM:(h + 1) * HEAD_DIM, :],
                            preferred_element_type=F32)
        for hh in range(N_SWA_HEADS):
            o = swa_scr[:, hh, :].astype(BF16)
            lo = RET_W + hh * HEAD_DIM
            y = y + jnp.dot(o, wout_ref[lo:lo + HEAD_DIM, :], preferred_element_type=F32)
        x1_ref[...] = _ln(DN_ALPHA * x_ref[...] + (1.0 + gate_ref[0]) * y, lng_ref[...], lnb_ref[...])


def _mix_sample(proj, x, mod, layer, w_out_bf, gamma1, bias_row, sinks, gain8, ln_g, ln_b,
                st, ck, cv):
    q, k, _, v, g, sq, skv = proj
    nb = x.shape[0]
    steps = nb // SAMPLE_BB
    to_cols = lambda a: a.astype(F32).reshape(steps, SAMPLE_BB, RET_W).transpose(0, 2, 1)
    qt, kt = to_cols(q), to_cols(k)
    v3 = v.astype(F32).reshape(nb, N_RET_HEADS, HEAD_DIM)
    sq3 = sq.astype(F32).reshape(nb, N_SWA_HEADS, HEAD_DIM)
    g3 = g.reshape(nb, N_RET_HEADS, HEAD_DIM).transpose(1, 0, 2)
    knew, vnew = skv[:, :SWA_KVW], skv[:, SWA_KVW:]
    st2 = st.reshape(nb, RET_W, HEAD_DIM)
    ck2 = ck.reshape(nb, CHUNK, SWA_KVW)
    cv2 = cv.reshape(nb, CHUNK, SWA_KVW)
    smem = pl.BlockSpec(memory_space=pltpu.SMEM)
    blk3 = lambda a, b, c: pl.BlockSpec((a, b, c), lambda i: (i, 0, 0))
    const2 = lambda a: pl.BlockSpec(a.shape, lambda i: (0, 0))
    const3 = lambda a: pl.BlockSpec(a.shape, lambda i: (0, 0, 0))
    x1, nst, nk, nv = pl.pallas_call(
        _mix_sample_kernel,
        grid=(steps,),
        in_specs=[smem, smem,
                  blk3(1, RET_W, SAMPLE_BB), blk3(1, RET_W, SAMPLE_BB),
                  blk3(SAMPLE_BB, N_RET_HEADS, HEAD_DIM), blk3(SAMPLE_BB, N_SWA_HEADS, HEAD_DIM),
                  pl.BlockSpec((SAMPLE_BB, SWA_KVW), lambda i: (i, 0)),
                  pl.BlockSpec((SAMPLE_BB, SWA_KVW), lambda i: (i, 0)),
                  blk3(SAMPLE_BB, RET_W, HEAD_DIM), blk3(SAMPLE_BB, CHUNK, SWA_KVW),
                  blk3(SAMPLE_BB, CHUNK, SWA_KVW),
                  const3(g3), const2(x),
                  pl.BlockSpec((1, nb, D_MODEL), lambda i: (layer, 0, 2)),
                  const2(w_out_bf), const2(bias_row), const2(gain8), const2(ln_g), const2(ln_b)],
        out_specs=[const2(x), blk3(SAMPLE_BB, RET_W, HEAD_DIM), blk3(SAMPLE_BB, CHUNK, SWA_KVW),
                   blk3(SAMPLE_BB, CHUNK, SWA_KVW)],
        out_shape=[jax.ShapeDtypeStruct(x.shape, F32), jax.ShapeDtypeStruct(st2.shape, F32),
                   jax.ShapeDtypeStruct(ck2.shape, F32), jax.ShapeDtypeStruct(cv2.shape, F32)],
        scratch_shapes=[pltpu.VMEM((nb, N_RET_HEADS, HEAD_DIM), F32),
                        pltpu.VMEM((nb, N_SWA_HEADS, HEAD_DIM), F32)],
        compiler_params=_cparams(("arbitrary",)),
        name="mix_sample",
    )(gamma1, sinks, qt, kt, v3, sq3, knew, vnew, st2, ck2, cv2, g3, x, mod,
      w_out_bf, bias_row, gain8, ln_g, ln_b)
    return x1, nst.reshape(st.shape), nk.reshape(ck.shape), nv.reshape(cv.shape)


def _router_kernel(x_ref, sh_ref, sc_ref, rwt_ref, rb_ref, tri_ref,
                   oi_ref, of_ref, cnt_ref, run_ref, *, mod_rows):
    h2 = x_ref[...] * (1.0 + sc_ref[0][:mod_rows]) + sh_ref[0][:mod_rows]
    _route_rows(h2, rwt_ref, rb_ref, tri_ref, oi_ref, of_ref, cnt_ref, run_ref)


def _post_mix_kernel(mixed_ref, x_ref, gate_ref, wout_ref, lng_ref, lnb_ref,
                     sh_ref, sc_ref, rwt_ref, rb_ref, tri_ref,
                     x1_ref, oi_ref, of_ref, cnt_ref, run_ref):
    y = jnp.dot(mixed_ref[...], wout_ref[...], preferred_element_type=F32)
    x1 = _ln(DN_ALPHA * x_ref[...] + (1.0 + gate_ref[0][:1]) * y, lng_ref[...], lnb_ref[...])
    x1_ref[...] = x1
    h2 = x1 * (1.0 + sc_ref[0][:1]) + sh_ref[0][:1]
    _route_rows(h2, rwt_ref, rb_ref, tri_ref, oi_ref, of_ref, cnt_ref, run_ref)


def _route_rows(h2, rwt_ref, rb_ref, tri_ref, oi_ref, of_ref, cnt_ref, run_ref):
    i = pl.program_id(0)

    @pl.when(i == 0)
    def _():
        run_ref[...] = jnp.zeros_like(run_ref)

    logits = lax.dot_general(rwt_ref[...].astype(BF16), h2.astype(BF16), (((1,), (1,)), ((), ())),
                             preferred_element_type=F32)
    aff = 1.0 / (1.0 + jnp.exp(-logits))
    sel = aff + rb_ref[...]
    s = [sel[e:e + 1, :] for e in range(N_EXPERTS)]
    a = [aff[e:e + 1, :] for e in range(N_EXPERTS)]

    def top2sum(v0, v1, v2, v3):
        hi01, lo01 = jnp.maximum(v0, v1), jnp.minimum(v0, v1)
        hi23, lo23 = jnp.maximum(v2, v3), jnp.minimum(v2, v3)
        return jnp.maximum(hi01, hi23) + jnp.maximum(jnp.minimum(hi01, hi23),
                                                     jnp.maximum(lo01, lo23))

    def argmax_first(vals):
        best, idx = vals[0], jnp.zeros(vals[0].shape, I32)
        for j in range(1, len(vals)):
            upd = vals[j] > best
            idx = jnp.where(upd, j, idx)
            best = jnp.where(upd, vals[j], best)
        return idx

    def pick(idx, vals):
        out = vals[-1]
        for j in range(len(vals) - 2, -1, -1):
            out = jnp.where(idx == j, vals[j], out)
        return out

    gi = argmax_first([top2sum(*s[4 * g:4 * g + 4]) for g in range(N_GROUPS)])
    sv = [pick(gi, [s[4 * g + j] for g in range(N_GROUPS)]) for j in range(EXPERTS_PER_GROUP)]
    av = [pick(gi, [a[4 * g + j] for g in range(N_GROUPS)]) for j in range(EXPERTS_PER_GROUP)]
    i1 = argmax_first(sv)
    i2 = argmax_first([jnp.where(i1 == j, -jnp.inf, sv[j]) for j in range(EXPERTS_PER_GROUP)])
    w1, w2 = pick(i1, av), pick(i2, av)
    wsum = w1 + w2
    w1, w2 = w1 / wsum, w2 / wsum
    lo, hi = jnp.minimum(i1, i2), jnp.maximum(i1, i2)
    w_lo = jnp.where(i1 < i2, w1, w2)
    w_hi = jnp.where(i1 < i2, w2, w1)
    pair = jnp.where(lo == 0, hi - 1, jnp.where(lo == 1, hi + 1, 5))
    bin_id = gi * N_PAIRS + pair

    tm = bin_id.shape[1]
    onehot = lax.broadcasted_iota(I32, (BIN_ROWS, tm), 0) == bin_id
    oh_f = jnp.where(onehot, 1.0, 0.0)
    before = jnp.dot(oh_f.astype(BF16), tri_ref[...], preferred_element_type=F32)
    run = run_ref[...]
    run_t = jnp.concatenate([run] * (tm // 128), axis=-1)
    rank = jnp.sum(oh_f * (before + run_t), axis=0, keepdims=True)
    run_new = run + jnp.sum(oh_f, axis=1, keepdims=True)
    run_ref[...] = run_new
    cnt_ref[...] = run_new.astype(I32)

    zi = jnp.zeros_like(bin_id)
    oi_ref[0] = jnp.concatenate([gi * 4 + lo, gi * 4 + hi, bin_id, rank.astype(I32), zi, zi, zi, zi], axis=0)
    zf = jnp.zeros_like(w_lo)
    of_ref[0] = jnp.concatenate([w_lo, w_hi, zf, zf, zf, zf, zf, zf], axis=0)


def _router(x1, mod, layer, mod_row0, mod_rows, router_wt, router_b, tm):
    rows = x1.shape[0]
    nt = rows // tm
    mblk = 8 if mod_rows == 1 else mod_rows
    mrow = mod_row0 // mblk
    tri = jnp.asarray(np.triu(np.ones((tm, tm), np.float32), 1), BF16)
    rb = jnp.broadcast_to(router_b.astype(F32)[:, None], (N_EXPERTS, tm))
    mod_spec = lambda c: pl.BlockSpec((1, mblk, D_MODEL), lambda i: (layer, mrow, c))
    oi, of, cnt = pl.pallas_call(
        functools.partial(_router_kernel, mod_rows=mod_rows),
        grid=(nt,),
        in_specs=[pl.BlockSpec((tm, D_MODEL), lambda i: (i, 0)), mod_spec(3), mod_spec(4),
                  pl.BlockSpec((N_EXPERTS, D_MODEL), lambda i: (0, 0)),
                  pl.BlockSpec((N_EXPERTS, tm), lambda i: (0, 0)),
                  pl.BlockSpec((tm, tm), lambda i: (0, 0))],
        out_specs=[pl.BlockSpec((1, 8, tm), lambda i: (i, 0, 0)),
                   pl.BlockSpec((1, 8, tm), lambda i: (i, 0, 0)),
                   pl.BlockSpec((BIN_ROWS, 128), lambda i: (0, 0))],
        out_shape=[jax.ShapeDtypeStruct((nt, 8, tm), I32),
                   jax.ShapeDtypeStruct((nt, 8, tm), F32),
                   jax.ShapeDtypeStruct((BIN_ROWS, 128), I32)],
        scratch_shapes=[pltpu.VMEM((BIN_ROWS, 128), F32)],
        compiler_params=_cparams(("arbitrary",)),
        name="router",
    )(x1, mod, mod, router_wt, rb, tri)
    return _unpack_route(oi, of, cnt)


def _unpack_route(oi, of, cnt):
    flat = lambda a, r: a[:, r, :].reshape(-1)
    return (flat(oi, 0), flat(oi, 1), flat(oi, 2), flat(oi, 3), flat(of, 0), flat(of, 1),
            cnt[:N_BINS, 0])


POST_TM = 512


def _post_mix(mixed, x, mod, layer, mod_row0, w_out_bf, ln_g, ln_b, router_wt, router_b):
    rows = mixed.shape[0]
    tm = POST_TM
    nt = rows // tm
    mrow = mod_row0 // 8
    tri = jnp.asarray(np.triu(np.ones((tm, tm), np.float32), 1), BF16)
    rb = jnp.broadcast_to(router_b.astype(F32)[:, None], (N_EXPERTS, tm))
    mod_spec = lambda c: pl.BlockSpec((1, 8, D_MODEL), lambda i: (layer, mrow, c))
    row_spec = pl.BlockSpec((tm, D_MODEL), lambda i: (i, 0))
    const2 = lambda a: pl.BlockSpec(a.shape, lambda i: (0, 0))
    x1, oi, of, cnt = pl.pallas_call(
        _post_mix_kernel,
        grid=(nt,),
        in_specs=[row_spec, row_spec, mod_spec(2), const2(w_out_bf), const2(ln_g), const2(ln_b),
                  mod_spec(3), mod_spec(4), const2(router_wt), const2(rb), const2(tri)],
        out_specs=[row_spec,
                   pl.BlockSpec((1, 8, tm), lambda i: (i, 0, 0)),
                   pl.BlockSpec((1, 8, tm), lambda i: (i, 0, 0)),
                   pl.BlockSpec((BIN_ROWS, 128), lambda i: (0, 0))],
        out_shape=[jax.ShapeDtypeStruct((rows, D_MODEL), F32),
                   jax.ShapeDtypeStruct((nt, 8, tm), I32),
                   jax.ShapeDtypeStruct((nt, 8, tm), F32),
                   jax.ShapeDtypeStruct((BIN_ROWS, 128), I32)],
        scratch_shapes=[pltpu.VMEM((BIN_ROWS, 128), F32)],
        compiler_params=_cparams(("arbitrary",)),
        name="post_mix",
    )(mixed, x, mod, w_out_bf, ln_g, ln_b, mod, mod, router_wt, rb, tri)
    return x1, _unpack_route(oi, of, cnt)


DMA_UNROLL = 8
ROW_GROUPS = MOE_TM // DMA_UNROLL


def _for_rows(n, fn):
    def body_u(j, carry):
        for u in range(DMA_UNROLL):
            fn(j, u)
        return carry
    lax.fori_loop(0, n // DMA_UNROLL, body_u, 0)

    def body_1(r, carry):
        fn(r // DMA_UNROLL, r % DMA_UNROLL)
        return carry
    lax.fori_loop((n // DMA_UNROLL) * DMA_UNROLL, n, body_1, 0)


def _moe_sorted_kernel(ea_ref, eb_ref, nvalid_ref, pos_ref,
                       x_hbm, sh_ref, sc_ref, gate_ref, rw_ref,
                       wga_ref, wua_ref, wda_ref, wgb_ref, wub_ref, wdb_ref,
                       lng_ref, lnb_ref,
                       out_hbm,
                       inv, xbuf, obuf, wg_a, wu_a, wd_a, wg_b, wu_b, wd_b, gsem, ssem):
    i = pl.program_id(0)
    slot = i % 2
    n_tok = pos_ref.shape[0]
    valid = nvalid_ref[i] > 0
    prev_valid = jnp.logical_and(i >= 1, nvalid_ref[jnp.maximum(i - 1, 0)] > 0)

    def gather_copy(t, j, u, dst_slot):
        tok = jnp.minimum(inv[t * MOE_TM + j * DMA_UNROLL + u], n_tok - 1)
        return pltpu.make_async_copy(x_hbm.at[pl.ds(tok, 1)], xbuf.at[dst_slot, j, pl.ds(u, 1)], gsem)

    def scatter_copy(row, j, u, src_slot):
        return pltpu.make_async_copy(obuf.at[src_slot, j, pl.ds(u, 1)], out_hbm.at[pl.ds(row, 1)], ssem)

    def wait_tile(buf, sem):
        pltpu.make_async_copy(buf, buf, sem).wait()

    @pl.when(i == 0)
    def _():
        def fill(j, carry):
            for u in range(DMA_UNROLL):
                inv[j * DMA_UNROLL + u] = n_tok + (j % ROW_GROUPS) * DMA_UNROLL + u
            return carry
        lax.fori_loop(0, inv.shape[0] // DMA_UNROLL, fill, 0)

        def body(j, carry):
            for u in range(DMA_UNROLL):
                t = j * DMA_UNROLL + u
                inv[pos_ref[t]] = t
            return carry
        lax.fori_loop(0, n_tok // DMA_UNROLL, body, 0)
        obuf[...] = jnp.zeros_like(obuf)
        _for_rows(MOE_TM, lambda j, u: gather_copy(0, j, u, 0).start())

    @pl.when(valid)
    def _():
        wait_tile(xbuf.at[slot], gsem)

        @pl.when(i >= 1)
        def _():
            wait_tile(obuf.at[slot], ssem)

        @pl.when(jnp.logical_or(i == 0, ea_ref[i] != ea_ref[jnp.maximum(i - 1, 0)]))
        def _():
            wg_a[...] = wga_ref[0, 0].astype(BF16)
            wu_a[...] = wua_ref[0, 0].astype(BF16)
            wd_a[...] = wda_ref[0, 0].astype(BF16)

        @pl.when(jnp.logical_or(i == 0, eb_ref[i] != eb_ref[jnp.maximum(i - 1, 0)]))
        def _():
            wg_b[...] = wgb_ref[0, 0].astype(BF16)
            wu_b[...] = wub_ref[0, 0].astype(BF16)
            wd_b[...] = wdb_ref[0, 0].astype(BF16)

        prev_base = jnp.maximum(i - 1, 0) * MOE_TM
        for r in range(MOE_TM):
            row = jnp.where(i == 0, n_tok + r, inv[prev_base + r])
            scatter_copy(row, r // DMA_UNROLL, r % DMA_UNROLL, 1 - slot).start()
        for r in range(MOE_TM):
            gather_copy(i + 1, r // DMA_UNROLL, r % DMA_UNROLL, 1 - slot).start()

        x = xbuf[slot].reshape(MOE_TM, D_MODEL)
        h2 = (x * (1.0 + sc_ref[0][:1]) + sh_ref[0][:1]).astype(BF16)

        def expert(wg, wu, wd):
            a = _silu(jnp.dot(h2, wg[...], preferred_element_type=F32)) * \
                jnp.dot(h2, wu[...], preferred_element_type=F32)
            return jnp.dot(a.astype(BF16), wd[...], preferred_element_type=F32)

        aff = 1.0 / (1.0 + jnp.exp(-jnp.dot(h2, rw_ref[...], preferred_element_type=F32)))
        lane = lax.broadcasted_iota(I32, aff.shape, 1)
        a_lo = jnp.sum(jnp.where(lane == ea_ref[i], aff, 0.0), axis=-1, keepdims=True)
        a_hi = jnp.sum(jnp.where(lane == eb_ref[i], aff, 0.0), axis=-1, keepdims=True)
        a_sum = a_lo + a_hi
        y = (a_lo / a_sum) * expert(wg_a, wu_a, wd_a) + (a_hi / a_sum) * expert(wg_b, wu_b, wd_b)
        out = _ln(DN_ALPHA * x + (1.0 + gate_ref[0][:1]) * y, lng_ref[...], lnb_ref[...])
        obuf[slot] = out.reshape(ROW_GROUPS, DMA_UNROLL, D_MODEL)

    @pl.when(jnp.logical_and(jnp.logical_not(valid), prev_valid))
    def _():
        wait_tile(xbuf.at[slot], gsem)
        wait_tile(obuf.at[slot], ssem)
        base = (i - 1) * MOE_TM
        _for_rows(MOE_TM, lambda j, u: scatter_copy(inv[base + j * DMA_UNROLL + u], j, u, 1 - slot).start())
        wait_tile(obuf.at[1 - slot], ssem)


def _moe_sorted(x1, mod, layer, mod_row0, route, rw_pad, wg, wu, wd, ln_g, ln_b):
    _, _, bin_id, rank, _, _, counts = route
    rows = x1.shape[0]
    nt = rows // MOE_TM + N_BINS
    ns = nt * MOE_TM
    tiles = (counts + MOE_TM - 1) // MOE_TM
    tile_end = jnp.cumsum(tiles)
    row_start = (tile_end - tiles) * MOE_TM
    pos = row_start[bin_id] + rank
    total = tile_end[-1]
    t = jnp.arange(nt, dtype=I32)
    tile_bin = jnp.searchsorted(tile_end, jnp.minimum(t, total - 1), side="right").astype(I32)
    tile_bin = jnp.clip(tile_bin, 0, N_BINS - 1)
    bin_lo = jnp.asarray([g * 4 + _PAIR_LO[p] for g in range(N_GROUPS) for p in range(N_PAIRS)], I32)
    bin_hi = jnp.asarray([g * 4 + _PAIR_HI[p] for g in range(N_GROUPS) for p in range(N_PAIRS)], I32)
    tile_ea, tile_eb = bin_lo[tile_bin], bin_hi[tile_bin]
    nvalid = jnp.clip(counts[tile_bin] - (t - (tile_end - tiles)[tile_bin]) * MOE_TM, 0, MOE_TM)
    nvalid = jnp.where(t < total, nvalid, 0).astype(I32)
    mrow = mod_row0 // 8
    mod_spec = lambda c: pl.BlockSpec((1, 8, D_MODEL), lambda i, *_: (layer, mrow, c))
    w_spec = lambda shape, ref_idx: pl.BlockSpec(
        (1, 1) + shape, lambda i, ea_r, eb_r, *_: (layer, (ea_r, eb_r)[ref_idx][i], 0, 0))
    any_spec = pl.BlockSpec(memory_space=pl.ANY)
    w_bf = lambda shape: pltpu.VMEM(shape, BF16)
    grid_spec = pltpu.PrefetchScalarGridSpec(
        num_scalar_prefetch=4,
        grid=(nt,),
        in_specs=[any_spec,
                  mod_spec(3), mod_spec(4), mod_spec(5),
                  pl.BlockSpec((D_MODEL, 128), lambda i, *_: (0, 0)),
                  w_spec((D_MODEL, D_FF), 0), w_spec((D_MODEL, D_FF), 0), w_spec((D_FF, D_MODEL), 0),
                  w_spec((D_MODEL, D_FF), 1), w_spec((D_MODEL, D_FF), 1), w_spec((D_FF, D_MODEL), 1),
                  pl.BlockSpec((1, D_MODEL), lambda i, *_: (0, 0)),
                  pl.BlockSpec((1, D_MODEL), lambda i, *_: (0, 0))],
        out_specs=any_spec,
        scratch_shapes=[pltpu.SMEM((ns,), I32),
                        pltpu.VMEM((2, ROW_GROUPS, DMA_UNROLL, D_MODEL), F32),
                        pltpu.VMEM((2, ROW_GROUPS, DMA_UNROLL, D_MODEL), F32),
                        w_bf((D_MODEL, D_FF)), w_bf((D_MODEL, D_FF)), w_bf((D_FF, D_MODEL)),
                        w_bf((D_MODEL, D_FF)), w_bf((D_MODEL, D_FF)), w_bf((D_FF, D_MODEL)),
                        pltpu.SemaphoreType.DMA(()),
                        pltpu.SemaphoreType.DMA(())],
    )
    return pl.pallas_call(
        _moe_sorted_kernel,
        grid_spec=grid_spec,
        out_shape=jax.ShapeDtypeStruct((rows + MOE_TM, D_MODEL), F32),
        compiler_params=_cparams(("arbitrary",)),
        name="moe_sorted",
    )(tile_ea, tile_eb, nvalid, pos, x1, mod, mod, mod, rw_pad,
      wg, wu, wd, wg, wu, wd, ln_g, ln_b)


def _moe_dense_kernel(x_ref, dw_ref, sh_ref, sc_ref, gate_ref, wg_ref, wu_ref, wd_ref,
                      lng_ref, lnb_ref, o_ref, acc):
    e = pl.program_id(0)

    @pl.when(e == 0)
    def _():
        acc[...] = jnp.zeros_like(acc)

    x = x_ref[...]
    h2 = (x * (1.0 + sc_ref[0]) + sh_ref[0]).astype(BF16)
    a = _silu(jnp.dot(h2, wg_ref[0, 0].astype(BF16), preferred_element_type=F32)) * \
        jnp.dot(h2, wu_ref[0, 0].astype(BF16), preferred_element_type=F32)
    y = jnp.dot(a.astype(BF16), wd_ref[0, 0].astype(BF16), preferred_element_type=F32)
    acc[...] = acc[...] + dw_ref[0][:, :1] * y

    @pl.when(e == pl.num_programs(0) - 1)
    def _():
        o_ref[...] = _ln(DN_ALPHA * x + (1.0 + gate_ref[0]) * acc[...], lng_ref[...], lnb_ref[...])


def _moe_dense(x1, mod, layer, route, wg, wu, wd, ln_g, ln_b):
    ea, eb, _, _, w_lo, w_hi, _ = route
    nb = x1.shape[0]
    eids = jnp.arange(N_EXPERTS, dtype=I32)[:, None]
    dw = jnp.where(eids == ea[None, :], w_lo[None, :], 0.0) + \
        jnp.where(eids == eb[None, :], w_hi[None, :], 0.0)
    dw = jnp.broadcast_to(dw[:, :, None], (N_EXPERTS, nb, 128))
    mod_spec = lambda c: pl.BlockSpec((1, nb, D_MODEL), lambda e: (layer, 0, c))
    const2 = lambda a: pl.BlockSpec(a.shape, lambda e: (0, 0))
    return pl.pallas_call(
        _moe_dense_kernel,
        grid=(N_EXPERTS,),
        in_specs=[const2(x1), pl.BlockSpec((1, nb, 128), lambda e: (e, 0, 0)),
                  mod_spec(3), mod_spec(4), mod_spec(5),
                  pl.BlockSpec((1, 1, D_MODEL, D_FF), lambda e: (layer, e, 0, 0)),
                  pl.BlockSpec((1, 1, D_MODEL, D_FF), lambda e: (layer, e, 0, 0)),
                  pl.BlockSpec((1, 1, D_FF, D_MODEL), lambda e: (layer, e, 0, 0)),
                  const2(ln_g), const2(ln_b)],
        out_specs=const2(x1),
        out_shape=jax.ShapeDtypeStruct(x1.shape, F32),
        scratch_shapes=[pltpu.VMEM(x1.shape, F32)],
        compiler_params=_cparams(("arbitrary",)),
        name="moe_dense",
    )(x1, dw, mod, mod, mod, wg, wu, wd, ln_g, ln_b)


def kernel(x_prompt, x_sample, state_ret, cache_swa_k, cache_swa_v, c_prompt, c_sample, w_in, w_out, ret_gn_gain, swa_sinks, rel_bias_table, ada_w, ada_b, ln1_g, ln1_b, ln2_g, ln2_b, router_w, router_b, exp_w_gate, exp_w_up, exp_w_down):
    seq = x_prompt.shape[1]
    nb = x_sample.shape[0]
    past_len = 16384
    assert x_prompt.shape[0] == 1 and x_sample.shape[1] == 1

    c_all = jnp.concatenate([c_sample, c_prompt, jnp.zeros((7, D_MODEL), F32)], axis=0)
    mod = _ada(c_all, ada_w, ada_b)
    prompt_row = nb

    bias_tabs = _bias_tables(rel_bias_table.astype(F32))
    gl, decay, xi, zeta, gamma1 = _ret_consts()
    rope_p = _rope_tables(jnp.arange(seq, dtype=I32))
    rope_s = _rope_tables(jnp.full((nb,), past_len, I32))
    router_wt = router_w.astype(F32).T
    rw_pad = jnp.pad(router_w.astype(BF16), ((0, 0), (0, 128 - N_EXPERTS)))
    zeta_p = jnp.tile(zeta, (INPROJ_TM // CHUNK, 1))
    zeta_s = jnp.ones((nb, RET_W), F32)

    xp = x_prompt.reshape(seq, D_MODEL)
    xs = x_sample.reshape(nb, D_MODEL)
    st_p, k_p, v_p, st_s, k_s, v_s = [], [], [], [], [], []
    for l in range(DEPTH):
        w_in_bf = w_in[l].astype(BF16)
        w_out_bf = w_out[l].astype(BF16)
        experts = (exp_w_gate.astype(F32), exp_w_up.astype(F32), exp_w_down.astype(F32))
        gain = ret_gn_gain[l].astype(F32)
        sinks = swa_sinks[l].astype(F32)
        row = lambda a: a[l].astype(F32).reshape(1, D_MODEL)

        proj = _inproj(xp, seq, mod, l, prompt_row, 1, w_in_bf, rope_p, zeta_p, INPROJ_TM)
        mixed, st = _mix_prompt(proj, (gl, decay, xi), bias_tabs, sinks, gain.reshape(1, RET_W))
        skv = proj[6]
        st_p.append(st.reshape(1, N_RET_HEADS, HEAD_DIM, HEAD_DIM))
        k_p.append(skv[seq - CHUNK:, :SWA_KVW].reshape(1, CHUNK, N_KV_HEADS, HEAD_DIM))
        v_p.append(skv[seq - CHUNK:, SWA_KVW:].reshape(1, CHUNK, N_KV_HEADS, HEAD_DIM))
        x1, route = _post_mix(mixed, xp, mod, l, prompt_row, w_out_bf, row(ln1_g), row(ln1_b),
                              router_wt, router_b)
        xp = _moe_sorted(x1, mod, l, prompt_row, route, rw_pad, *experts, row(ln2_g), row(ln2_b))

        proj = _inproj(xs, nb, mod, l, 0, nb, w_in_bf, rope_s, zeta_s, nb)
        x1, nst, nk, nv = _mix_sample(proj, xs, mod, l, w_out_bf, gamma1, bias_tabs[2], sinks, gain,
                                      row(ln1_g), row(ln1_b), state_ret[l].astype(F32),
                                      cache_swa_k[l], cache_swa_v[l])
        st_s.append(nst)
        k_s.append(nk)
        v_s.append(nv)
        route = _router(x1, mod, l, 0, nb, router_wt, router_b, nb)
        xs = _moe_dense(x1, mod, l, route, *experts, row(ln2_g), row(ln2_b))

    return (xp[:seq].reshape(1, seq, D_MODEL), xs.reshape(nb, 1, D_MODEL),
            jnp.stack(st_p), jnp.stack(k_p), jnp.stack(v_p),
            jnp.stack(st_s), jnp.stack(k_s), jnp.stack(v_s))
```

```python
import functools
import math

import numpy as np
import jax
import jax.numpy as jnp
from jax import lax
from jax.experimental import pallas as pl
from jax.experimental.pallas import tpu as pltpu

F32 = jnp.float32
BF16 = jnp.bfloat16
I32 = jnp.int32

D_MODEL = 1024
DEPTH = 2
HEAD_DIM = 64
N_RET_HEADS = 8
N_SWA_HEADS = 8
N_KV_HEADS = 2
GQA_GROUP = N_SWA_HEADS // N_KV_HEADS
RET_W = N_RET_HEADS * HEAD_DIM
SWA_QW = N_SWA_HEADS * HEAD_DIM
SWA_KVW = N_KV_HEADS * HEAD_DIM
PROJ_W = 4 * RET_W + SWA_QW + 2 * SWA_KVW
CHUNK = 128
ROPE_BASE = 10000.0
N_BUCKETS = 32
MAX_DISTANCE = 128
N_EXPERTS = 16
N_GROUPS = 4
EXPERTS_PER_GROUP = 4
D_FF = 512
LN_EPS = 1e-5
DN_ALPHA = (2 * DEPTH) ** 0.25
QK_SCALE = HEAD_DIM ** -0.5
NEG_BIG = -1e30

N_PAIRS = 6
N_BINS = N_GROUPS * N_PAIRS
BIN_ROWS = 32
MOE_TM = 256
INPROJ_TM = 512
VMEM_LIMIT = 56 * 1024 * 1024

_PAIR_LO = (0, 0, 0, 1, 1, 2)
_PAIR_HI = (1, 2, 3, 2, 3, 3)


def _cparams(sem):
    return pltpu.CompilerParams(dimension_semantics=sem, vmem_limit_bytes=VMEM_LIMIT)


def _ln(v, g, b):
    mu = jnp.mean(v, axis=-1, keepdims=True)
    c = v - mu
    var = jnp.mean(c * c, axis=-1, keepdims=True)
    return c * lax.rsqrt(var + LN_EPS) * g + b


def _silu(v):
    return v * (1.0 / (1.0 + jnp.exp(-v)))


def _ada_kernel(c_ref, w_ref, b_ref, o_ref):
    o_ref[0] = jnp.dot(c_ref[...].astype(BF16), w_ref[0].astype(BF16),
                       preferred_element_type=F32) + b_ref[0]


def _ada(c_all, ada_w, ada_b):
    rows = c_all.shape[0]
    nt = 6 * D_MODEL // 1024
    return pl.pallas_call(
        _ada_kernel,
        grid=(DEPTH, nt),
        in_specs=[pl.BlockSpec((rows, D_MODEL), lambda l, j: (0, 0)),
                  pl.BlockSpec((1, D_MODEL, 1024), lambda l, j: (l, 0, j)),
                  pl.BlockSpec((1, 1, 1024), lambda l, j: (l, 0, j))],
        out_specs=pl.BlockSpec((1, rows, 1024), lambda l, j: (l, 0, j)),
        out_shape=jax.ShapeDtypeStruct((DEPTH, rows, 6 * D_MODEL), F32),
        compiler_params=_cparams(("arbitrary", "arbitrary")),
        name="ada",
    )(c_all, ada_w, ada_b.reshape(DEPTH, 1, 6 * D_MODEL))


def _bias_kernel(tab_ref, bkt_ref, fold_ref, row_ref):
    bkt = bkt_ref[...]
    rows = lax.broadcasted_iota(I32, (CHUNK, CHUNK), 0)
    cols = lax.broadcasted_iota(I32, (CHUNK, CHUNK), 1)
    from_prev = cols > rows
    for h in range(N_SWA_HEADS):
        acc = jnp.zeros(bkt.shape, F32)
        for b in range(N_BUCKETS):
            acc = jnp.where(bkt == b, tab_ref[b, h], acc)
        own = acc[:, CHUNK:]
        fold_ref[0, h] = jnp.where(from_prev, NEG_BIG, own)
        fold_ref[1, h] = jnp.where(from_prev, acc[:, :CHUNK], own)
        row_ref[h:h + 1, :] = own[CHUNK - 1:CHUNK, :]


def _t5_bucket(rel):
    max_exact = N_BUCKETS // 2
    relf = jnp.maximum(rel, 1).astype(F32)
    large = max_exact + (jnp.log(relf / max_exact) / math.log(MAX_DISTANCE / max_exact)
                         * (N_BUCKETS - max_exact)).astype(I32)
    large = jnp.minimum(large, N_BUCKETS - 1)
    return jnp.where(rel < max_exact, rel, large)


def _bias_tables(rel_bias_table):
    qi = jnp.arange(CHUNK)
    si = jnp.arange(2 * CHUNK)
    rel = CHUNK + qi[:, None] - si[None, :]
    bkt = _t5_bucket(jnp.maximum(rel, 0)).astype(I32)
    return pl.pallas_call(
        _bias_kernel,
        in_specs=[pl.BlockSpec(memory_space=pltpu.SMEM),
                  pl.BlockSpec((CHUNK, 2 * CHUNK), lambda: (0, 0))],
        out_specs=[pl.BlockSpec((2, N_SWA_HEADS, CHUNK, CHUNK), lambda: (0, 0, 0, 0)),
                   pl.BlockSpec((N_SWA_HEADS, CHUNK), lambda: (0, 0))],
        out_shape=[jax.ShapeDtypeStruct((2, N_SWA_HEADS, CHUNK, CHUNK), F32),
                   jax.ShapeDtypeStruct((N_SWA_HEADS, CHUNK), F32)],
        name="t5_bias",
    )(rel_bias_table, bkt)


def _rotary(v, cos, s_lo, s_hi):
    outs = []
    for j in range(RET_W // 128):
        blk = v[:, j * 128:(j + 1) * 128]
        outs.append(blk * cos + pltpu.roll(blk, 96, 1) * s_lo + pltpu.roll(blk, 32, 1) * s_hi)
    return jnp.concatenate(outs, axis=-1)


def _inproj_kernel(x_ref, sh_ref, sc_ref, w_ref, cs_ref, zeta_ref,
                   q_ref, k_ref, kz_ref, v_ref, g_ref, sq_ref, skv_ref, *, mod_rows):
    sh = sh_ref[0][:mod_rows]
    sc = sc_ref[0][:mod_rows]
    h = (x_ref[...] * (1.0 + sc) + sh).astype(BF16)
    cs = cs_ref[...]
    first = (lax.broadcasted_iota(I32, cs.shape, 1) % HEAD_DIM) < (HEAD_DIM // 2)
    cos = jnp.where(first, cs, pltpu.roll(cs, 32, 1))
    s_lo = jnp.where(first, -pltpu.roll(cs, 96, 1), 0.0)
    s_hi = jnp.where(first, 0.0, cs)

    def proj(lo, hi):
        return jnp.dot(h, w_ref[:, lo:hi], preferred_element_type=F32)

    q_ref[...] = _rotary(proj(0, RET_W), cos, s_lo, s_hi).astype(BF16)
    k = _rotary(proj(RET_W, 2 * RET_W), cos, s_lo, s_hi) * QK_SCALE
    k_ref[...] = k.astype(BF16)
    kz_ref[...] = (k * zeta_ref[...]).astype(BF16)
    v_ref[...] = proj(2 * RET_W, 3 * RET_W).astype(BF16)
    g_ref[...] = proj(3 * RET_W, 4 * RET_W)
    sq_ref[...] = (proj(4 * RET_W, 4 * RET_W + SWA_QW) * QK_SCALE).astype(BF16)
    skv_ref[...] = proj(4 * RET_W + SWA_QW, PROJ_W)


def _inproj(x, mod, layer, mod_row0, mod_rows, w_in_bf, rope, zeta_tile, tm):
    rows = x.shape[0]
    mblk = 8 if mod_rows == 1 else mod_rows
    mrow = mod_row0 // mblk
    row_spec = lambda w: pl.BlockSpec((tm, w), lambda i: (i, 0))
    mod_spec = lambda c: pl.BlockSpec((1, mblk, D_MODEL), lambda i: (layer, mrow, c))
    return pl.pallas_call(
        functools.partial(_inproj_kernel, mod_rows=mod_rows),
        grid=(rows // tm,),
        in_specs=[row_spec(D_MODEL), mod_spec(0), mod_spec(1),
                  pl.BlockSpec((D_MODEL, PROJ_W), lambda i: (0, 0)),
                  row_spec(128),
                  pl.BlockSpec((tm, RET_W), lambda i: (0, 0))],
        out_specs=[row_spec(RET_W), row_spec(RET_W), row_spec(RET_W), row_spec(RET_W), row_spec(RET_W),
                   row_spec(SWA_QW), row_spec(2 * SWA_KVW)],
        out_shape=[jax.ShapeDtypeStruct((rows, RET_W), BF16),
                   jax.ShapeDtypeStruct((rows, RET_W), BF16),
                   jax.ShapeDtypeStruct((rows, RET_W), BF16),
                   jax.ShapeDtypeStruct((rows, RET_W), BF16),
                   jax.ShapeDtypeStruct((rows, RET_W), F32),
                   jax.ShapeDtypeStruct((rows, SWA_QW), BF16),
                   jax.ShapeDtypeStruct((rows, 2 * SWA_KVW), F32)],
        compiler_params=_cparams(("arbitrary",)),
        name="inproj",
    )(x, mod, mod, w_in_bf, rope, zeta_tile)


def _rope_tables(pos):
    half = HEAD_DIM // 2
    inv = ROPE_BASE ** (-jnp.arange(half, dtype=F32) / half)
    ang = pos.astype(F32)[:, None] * inv[None, :]
    cos, sin = jnp.cos(ang), jnp.sin(ang)
    return jnp.concatenate([cos, sin, cos, sin], axis=-1)


def _mix_prompt_kernel(gl_ref, sink_ref,
                       q_ref, k_ref, kz_ref, v_ref, g_ref, sq_ref, kvc_ref, kvp_ref,
                       decay_ref, xi_ref, bias_ref, gain_ref,
                       mixed_ref, st_ref, state):
    i = pl.program_id(0)

    @pl.when(i == 0)
    def _():
        state[...] = jnp.zeros_like(state)

    q = q_ref[...]
    k = k_ref[...]
    v = v_ref[...]
    kz = kz_ref[...]

    intra, inter = [], []
    for h in range(N_RET_HEADS):
        sl = slice(h * HEAD_DIM, (h + 1) * HEAD_DIM)
        qh, kh, vh = q[:, sl], k[:, sl], v[:, sl]
        s = lax.dot_general(qh, kh, (((1,), (1,)), ((), ())), preferred_element_type=F32)
        p = (s * decay_ref[h]).astype(BF16)
        intra.append(jnp.dot(p, vh, preferred_element_type=F32))
        s_old = state[h]
        inter.append(jnp.dot(qh, s_old.astype(BF16), preferred_element_type=F32))
        upd = lax.dot_general(kz[:, sl], vh, (((0,), (0,)), ((), ())), preferred_element_type=F32)
        state[h] = gl_ref[h] * s_old + upd

    xi = xi_ref[...]
    g = g_ref[...]
    ret_parts = []
    for h in range(N_RET_HEADS):
        sl = slice(h * HEAD_DIM, (h + 1) * HEAD_DIM)
        o = intra[h] + inter[h] * xi[:, sl]
        mu = jnp.mean(o, axis=-1, keepdims=True)
        c = o - mu
        var = jnp.mean(c * c, axis=-1, keepdims=True)
        ret_parts.append(c * lax.rsqrt(var + LN_EPS))
    ret_o = jnp.concatenate(ret_parts, axis=-1) * gain_ref[...] * _silu(g)

    sq = sq_ref[...]
    kvc = kvc_ref[...]
    kvp = kvp_ref[...]
    swa_parts = []
    from_prev = (lax.broadcasted_iota(I32, (CHUNK, CHUNK), 1) >
                 lax.broadcasted_iota(I32, (CHUNK, CHUNK), 0))
    for kv in range(N_KV_HEADS):
        ks = slice(kv * HEAD_DIM, (kv + 1) * HEAD_DIM)
        vs = slice(SWA_KVW + kv * HEAD_DIM, SWA_KVW + (kv + 1) * HEAD_DIM)
        kk = jnp.concatenate([kvp[:, ks], kvc[:, ks]], axis=0).astype(BF16)
        vv = jnp.concatenate([kvp[:, vs], kvc[:, vs]], axis=0).astype(BF16)
        for j in range(GQA_GROUP):
            hh = kv * GQA_GROUP + j
            qh = sq[:, hh * HEAD_DIM:(hh + 1) * HEAD_DIM]
            s2 = lax.dot_general(qh, kk, (((1,), (1,)), ((), ())), preferred_element_type=F32)
            s = jnp.where(from_prev, s2[:, :CHUNK], s2[:, CHUNK:]) + bias_ref[0, hh]
            sink = sink_ref[hh]
            m = jnp.maximum(jnp.max(s, axis=-1, keepdims=True), sink)
            p = jnp.exp(s - m)
            den = jnp.sum(p, axis=-1, keepdims=True) + jnp.exp(sink - m)
            pn = p / den
            p2 = jnp.concatenate([jnp.where(from_prev, pn, 0.0), jnp.where(from_prev, 0.0, pn)], axis=-1)
            swa_parts.append(jnp.dot(p2.astype(BF16), vv, preferred_element_type=F32))
    swa_o = jnp.concatenate(swa_parts, axis=-1)

    mixed_ref[...] = jnp.concatenate([ret_o, swa_o], axis=-1).astype(BF16)

    @pl.when(i == pl.num_programs(0) - 1)
    def _():
        st_ref[...] = state[...]


def _mix_prompt(proj, ret_consts, bias_tabs, sinks, gain):
    q, k, kz, v, g, sq, skv = proj
    rows = q.shape[0]
    nblk = rows // CHUNK
    gl, decay, xi = ret_consts
    bias_fold = bias_tabs[0]
    row_spec = lambda w: pl.BlockSpec((CHUNK, w), lambda i: (i, 0))
    const2 = lambda a: pl.BlockSpec(a.shape, lambda i: (0, 0))
    const3 = lambda a: pl.BlockSpec(a.shape, lambda i: (0, 0, 0))
    smem = pl.BlockSpec(memory_space=pltpu.SMEM)
    return pl.pallas_call(
        _mix_prompt_kernel,
        grid=(nblk,),
        in_specs=[smem, smem,
                  row_spec(RET_W), row_spec(RET_W), row_spec(RET_W), row_spec(RET_W), row_spec(RET_W),
                  row_spec(SWA_QW), row_spec(2 * SWA_KVW),
                  pl.BlockSpec((CHUNK, 2 * SWA_KVW), lambda i: (jnp.maximum(i - 1, 0), 0)),
                  const3(decay), const2(xi),
                  pl.BlockSpec((1, N_SWA_HEADS, CHUNK, CHUNK), lambda i: (jnp.minimum(i, 1), 0, 0, 0)),
                  const2(gain)],
        out_specs=[row_spec(D_MODEL),
                   pl.BlockSpec((N_RET_HEADS, HEAD_DIM, HEAD_DIM), lambda i: (0, 0, 0))],
        out_shape=[jax.ShapeDtypeStruct((rows, D_MODEL), BF16),
                   jax.ShapeDtypeStruct((N_RET_HEADS, HEAD_DIM, HEAD_DIM), F32)],
        scratch_shapes=[pltpu.VMEM((N_RET_HEADS, HEAD_DIM, HEAD_DIM), F32)],
        compiler_params=_cparams(("arbitrary",)),
        name="mix_prompt",
    )(gl, sinks, q, k, kz, v, g, sq, skv, skv, decay, xi, bias_fold, gain)


def _ret_consts():
    lg = jnp.log(1.0 - 2.0 ** (-5.0 - jnp.arange(N_RET_HEADS, dtype=F32)))
    idx = jnp.arange(CHUNK, dtype=F32)
    diff = idx[:, None] - idx[None, :]
    decay = jnp.where(diff >= 0, jnp.exp(jnp.maximum(diff, 0.0)[None] * lg[:, None, None]), 0.0)
    xi = jnp.exp((idx + 1.0)[:, None] * lg[None, :])
    zeta = jnp.exp((CHUNK - 1.0 - idx)[:, None] * lg[None, :])
    gl = jnp.exp(CHUNK * lg)
    expand = lambda a: jnp.repeat(a, HEAD_DIM, axis=1)
    return gl, decay, expand(xi), expand(zeta), jnp.exp(1.0 * lg)


SAMPLE_BB = 8


def _mix_sample_kernel(g1_ref, sink_ref,
                       qt_ref, kt_ref, v3_ref, sq3_ref, knew_ref, vnew_ref,
                       st_ref, ck_ref, cv_ref, g3_ref, x_ref, gate_ref,
                       wout_ref, brow_ref, gain_ref, lng_ref, lnb_ref,
                       x1_ref, nst_ref, nk_ref, nv_ref, ret_scr, swa_scr):
    i = pl.program_id(0)
    row = lax.broadcasted_iota(I32, (N_SWA_HEADS, 2 * HEAD_DIM), 0)
    lane = lax.broadcasted_iota(I32, (N_SWA_HEADS, 2 * HEAD_DIM), 1)
    own_half = (row // GQA_GROUP) == (lane // HEAD_DIM)
    sink_col = jnp.concatenate(
        [jnp.full((1, 1), sink_ref[hh], F32) for hh in range(N_SWA_HEADS)], axis=0)
    brow = brow_ref[...]
    qt = qt_ref[0]
    kt = kt_ref[0]

    for b in range(SAMPLE_BB):
        o_rows = []
        for h in range(N_RET_HEADS):
            rs = slice(h * HEAD_DIM, (h + 1) * HEAD_DIM)
            s_old = st_ref[b, rs, :]
            kcol = kt[rs, b:b + 1]
            qcol = qt[rs, b:b + 1]
            vrow = v3_ref[b, h:h + 1, :]
            s_new = g1_ref[h] * s_old + kcol * vrow
            nst_ref[b, rs, :] = s_new
            o_rows.append(jnp.sum(qcol * s_new, axis=0, keepdims=True))
        ret_scr[i * SAMPLE_BB + b] = jnp.concatenate(o_rows, axis=0)

        kk = jnp.concatenate([ck_ref[b, 1:, :], knew_ref[b:b + 1, :]], axis=0)
        vv = jnp.concatenate([cv_ref[b, 1:, :], vnew_ref[b:b + 1, :]], axis=0)
        nk_ref[b] = kk
        nv_ref[b] = vv
        q8 = sq3_ref[b]
        qblk = jnp.where(own_half, jnp.concatenate([q8, q8], axis=-1), 0.0).astype(BF16)
        s = lax.dot_general(qblk, kk.astype(BF16), (((1,), (1,)), ((), ())),
                            preferred_element_type=F32) + brow
        m = jnp.maximum(jnp.max(s, axis=-1, keepdims=True), sink_col)
        p = jnp.exp(s - m)
        den = jnp.sum(p, axis=-1, keepdims=True) + jnp.exp(sink_col - m)
        o = jnp.dot((p / den).astype(BF16), vv.astype(BF16), preferred_element_type=F32)
        swa_scr[i * SAMPLE_BB + b] = jnp.where(own_half[:, :HEAD_DIM], o[:, :HEAD_DIM], o[:, HEAD_DIM:])

    @pl.when(i == pl.num_programs(0) - 1)
    def _():
        y = jnp.zeros(x_ref.shape, F32)
        for h in range(N_RET_HEADS):
            o = ret_scr[:, h, :]
            mu = jnp.mean(o, axis=-1, keepdims=True)
            c = o - mu
            var = jnp.mean(c * c, axis=-1, keepdims=True)
            r = c * lax.rsqrt(var + LN_EPS) * gain_ref[h:h + 1, :] * _silu(g3_ref[h])
            y = y + jnp.dot(r.astype(BF16), wout_ref[h * HEAD_DIM:(h + 1) * HEAD_DIM, :],
                            preferred_element_type=F32)
        for hh in range(N_SWA_HEADS):
            o = swa_scr[:, hh, :].astype(BF16)
            lo = RET_W + hh * HEAD_DIM
            y = y + jnp.dot(o, wout_ref[lo:lo + HEAD_DIM, :], preferred_element_type=F32)
        x1_ref[...] = _ln(DN_ALPHA * x_ref[...] + (1.0 + gate_ref[0]) * y, lng_ref[...], lnb_ref[...])


def _mix_sample(proj, x, mod, layer, w_out_bf, gamma1, bias_row, sinks, gain8, ln_g, ln_b,
                st, ck, cv):
    q, k, _, v, g, sq, skv = proj
    nb = x.shape[0]
    steps = nb // SAMPLE_BB
    to_cols = lambda a: a.astype(F32).reshape(steps, SAMPLE_BB, RET_W).transpose(0, 2, 1)
    qt, kt = to_cols(q), to_cols(k)
    v3 = v.astype(F32).reshape(nb, N_RET_HEADS, HEAD_DIM)
    sq3 = sq.astype(F32).reshape(nb, N_SWA_HEADS, HEAD_DIM)
    g3 = g.reshape(nb, N_RET_HEADS, HEAD_DIM).transpose(1, 0, 2)
    knew, vnew = skv[:, :SWA_KVW], skv[:, SWA_KVW:]
    st2 = st.reshape(nb, RET_W, HEAD_DIM)
    ck2 = ck.reshape(nb, CHUNK, SWA_KVW)
    cv2 = cv.reshape(nb, CHUNK, SWA_KVW)
    smem = pl.BlockSpec(memory_space=pltpu.SMEM)
    blk3 = lambda a, b, c: pl.BlockSpec((a, b, c), lambda i: (i, 0, 0))
    const2 = lambda a: pl.BlockSpec(a.shape, lambda i: (0, 0))
    const3 = lambda a: pl.BlockSpec(a.shape, lambda i: (0, 0, 0))
    x1, nst, nk, nv = pl.pallas_call(
        _mix_sample_kernel,
        grid=(steps,),
        in_specs=[smem, smem,
                  blk3(1, RET_W, SAMPLE_BB), blk3(1, RET_W, SAMPLE_BB),
                  blk3(SAMPLE_BB, N_RET_HEADS, HEAD_DIM), blk3(SAMPLE_BB, N_SWA_HEADS, HEAD_DIM),
                  pl.BlockSpec((SAMPLE_BB, SWA_KVW), lambda i: (i, 0)),
                  pl.BlockSpec((SAMPLE_BB, SWA_KVW), lambda i: (i, 0)),
                  blk3(SAMPLE_BB, RET_W, HEAD_DIM), blk3(SAMPLE_BB, CHUNK, SWA_KVW),
                  blk3(SAMPLE_BB, CHUNK, SWA_KVW),
                  const3(g3), const2(x),
                  pl.BlockSpec((1, nb, D_MODEL), lambda i: (layer, 0, 2)),
                  const2(w_out_bf), const2(bias_row), const2(gain8), const2(ln_g), const2(ln_b)],
        out_specs=[const2(x), blk3(SAMPLE_BB, RET_W, HEAD_DIM), blk3(SAMPLE_BB, CHUNK, SWA_KVW),
                   blk3(SAMPLE_BB, CHUNK, SWA_KVW)],
        out_shape=[jax.ShapeDtypeStruct(x.shape, F32), jax.ShapeDtypeStruct(st2.shape, F32),
                   jax.ShapeDtypeStruct(ck2.shape, F32), jax.ShapeDtypeStruct(cv2.shape, F32)],
        scratch_shapes=[pltpu.VMEM((nb, N_RET_HEADS, HEAD_DIM), F32),
                        pltpu.VMEM((nb, N_SWA_HEADS, HEAD_DIM), F32)],
        compiler_params=_cparams(("arbitrary",)),
        name="mix_sample",
    )(gamma1, sinks, qt, kt, v3, sq3, knew, vnew, st2, ck2, cv2, g3, x, mod,
      w_out_bf, bias_row, gain8, ln_g, ln_b)
    return x1, nst.reshape(st.shape), nk.reshape(ck.shape), nv.reshape(cv.shape)


def _router_kernel(x_ref, sh_ref, sc_ref, rwt_ref, rb_ref, tri_ref,
                   oi_ref, of_ref, cnt_ref, run_ref, *, mod_rows):
    h2 = x_ref[...] * (1.0 + sc_ref[0][:mod_rows]) + sh_ref[0][:mod_rows]
    _route_rows(h2, rwt_ref, rb_ref, tri_ref, oi_ref, of_ref, cnt_ref, run_ref)


def _post_mix_kernel(mixed_ref, x_ref, gate_ref, wout_ref, lng_ref, lnb_ref,
                     sh_ref, sc_ref, rwt_ref, rb_ref, tri_ref,
                     x1_ref, oi_ref, of_ref, cnt_ref, run_ref):
    y = jnp.dot(mixed_ref[...], wout_ref[...], preferred_element_type=F32)
    x1 = _ln(DN_ALPHA * x_ref[...] + (1.0 + gate_ref[0][:1]) * y, lng_ref[...], lnb_ref[...])
    x1_ref[...] = x1
    h2 = x1 * (1.0 + sc_ref[0][:1]) + sh_ref[0][:1]
    _route_rows(h2, rwt_ref, rb_ref, tri_ref, oi_ref, of_ref, cnt_ref, run_ref)


def _route_rows(h2, rwt_ref, rb_ref, tri_ref, oi_ref, of_ref, cnt_ref, run_ref):
    i = pl.program_id(0)

    @pl.when(i == 0)
    def _():
        run_ref[...] = jnp.zeros_like(run_ref)

    logits = lax.dot_general(rwt_ref[...].astype(BF16), h2.astype(BF16), (((1,), (1,)), ((), ())),
                             preferred_element_type=F32)
    aff = 1.0 / (1.0 + jnp.exp(-logits))
    sel = aff + rb_ref[...]
    s = [sel[e:e + 1, :] for e in range(N_EXPERTS)]
    a = [aff[e:e + 1, :] for e in range(N_EXPERTS)]

    def top2sum(v0, v1, v2, v3):
        hi01, lo01 = jnp.maximum(v0, v1), jnp.minimum(v0, v1)
        hi23, lo23 = jnp.maximum(v2, v3), jnp.minimum(v2, v3)
        return jnp.maximum(hi01, hi23) + jnp.maximum(jnp.minimum(hi01, hi23),
                                                     jnp.maximum(lo01, lo23))

    def argmax_first(vals):
        best, idx = vals[0], jnp.zeros(vals[0].shape, I32)
        for j in range(1, len(vals)):
            upd = vals[j] > best
            idx = jnp.where(upd, j, idx)
            best = jnp.where(upd, vals[j], best)
        return idx

    def pick(idx, vals):
        out = vals[-1]
        for j in range(len(vals) - 2, -1, -1):
            out = jnp.where(idx == j, vals[j], out)
        return out

    gi = argmax_first([top2sum(*s[4 * g:4 * g + 4]) for g in range(N_GROUPS)])
    sv = [pick(gi, [s[4 * g + j] for g in range(N_GROUPS)]) for j in range(EXPERTS_PER_GROUP)]
    av = [pick(gi, [a[4 * g + j] for g in range(N_GROUPS)]) for j in range(EXPERTS_PER_GROUP)]
    i1 = argmax_first(sv)
    i2 = argmax_first([jnp.where(i1 == j, -jnp.inf, sv[j]) for j in range(EXPERTS_PER_GROUP)])
    w1, w2 = pick(i1, av), pick(i2, av)
    wsum = w1 + w2
    w1, w2 = w1 / wsum, w2 / wsum
    lo, hi = jnp.minimum(i1, i2), jnp.maximum(i1, i2)
    w_lo = jnp.where(i1 < i2, w1, w2)
    w_hi = jnp.where(i1 < i2, w2, w1)
    pair = jnp.where(lo == 0, hi - 1, jnp.where(lo == 1, hi + 1, 5))
    bin_id = gi * N_PAIRS + pair

    tm = bin_id.shape[1]
    onehot = lax.broadcasted_iota(I32, (BIN_ROWS, tm), 0) == bin_id
    oh_f = jnp.where(onehot, 1.0, 0.0)
    before = jnp.dot(oh_f.astype(BF16), tri_ref[...], preferred_element_type=F32)
    run = run_ref[...]
    run_t = jnp.concatenate([run] * (tm // 128), axis=-1)
    rank = jnp.sum(oh_f * (before + run_t), axis=0, keepdims=True)
    run_new = run + jnp.sum(oh_f, axis=1, keepdims=True)
    run_ref[...] = run_new
    cnt_ref[...] = run_new.astype(I32)

    zi = jnp.zeros_like(bin_id)
    oi_ref[0] = jnp.concatenate([gi * 4 + lo, gi * 4 + hi, bin_id, rank.astype(I32), zi, zi, zi, zi], axis=0)
    zf = jnp.zeros_like(w_lo)
    of_ref[0] = jnp.concatenate([w_lo, w_hi, zf, zf, zf, zf, zf, zf], axis=0)


def _router(x1, mod, layer, mod_row0, mod_rows, router_wt, router_b, tm):
    rows = x1.shape[0]
    nt = rows // tm
    mblk = 8 if mod_rows == 1 else mod_rows
    mrow = mod_row0 // mblk
    tri = jnp.asarray(np.triu(np.ones((tm, tm), np.float32), 1), BF16)
    rb = jnp.broadcast_to(router_b.astype(F32)[:, None], (N_EXPERTS, tm))
    mod_spec = lambda c: pl.BlockSpec((1, mblk, D_MODEL), lambda i: (layer, mrow, c))
    oi, of, cnt = pl.pallas_call(
        functools.partial(_router_kernel, mod_rows=mod_rows),
        grid=(nt,),
        in_specs=[pl.BlockSpec((tm, D_MODEL), lambda i: (i, 0)), mod_spec(3), mod_spec(4),
                  pl.BlockSpec((N_EXPERTS, D_MODEL), lambda i: (0, 0)),
                  pl.BlockSpec((N_EXPERTS, tm), lambda i: (0, 0)),
                  pl.BlockSpec((tm, tm), lambda i: (0, 0))],
        out_specs=[pl.BlockSpec((1, 8, tm), lambda i: (i, 0, 0)),
                   pl.BlockSpec((1, 8, tm), lambda i: (i, 0, 0)),
                   pl.BlockSpec((BIN_ROWS, 128), lambda i: (0, 0))],
        out_shape=[jax.ShapeDtypeStruct((nt, 8, tm), I32),
                   jax.ShapeDtypeStruct((nt, 8, tm), F32),
                   jax.ShapeDtypeStruct((BIN_ROWS, 128), I32)],
        scratch_shapes=[pltpu.VMEM((BIN_ROWS, 128), F32)],
        compiler_params=_cparams(("arbitrary",)),
        name="router",
    )(x1, mod, mod, router_wt, rb, tri)
    return _unpack_route(oi, of, cnt)


def _unpack_route(oi, of, cnt):
    flat = lambda a, r: a[:, r, :].reshape(-1)
    return (flat(oi, 0), flat(oi, 1), flat(oi, 2), flat(oi, 3), flat(of, 0), flat(of, 1),
            cnt[:N_BINS, 0])


POST_TM = 512


def _post_mix(mixed, x, mod, layer, mod_row0, w_out_bf, ln_g, ln_b, router_wt, router_b):
    rows = x.shape[0]
    tm = POST_TM
    nt = rows // tm
    mrow = mod_row0 // 8
    tri = jnp.asarray(np.triu(np.ones((tm, tm), np.float32), 1), BF16)
    rb = jnp.broadcast_to(router_b.astype(F32)[:, None], (N_EXPERTS, tm))
    mod_spec = lambda c: pl.BlockSpec((1, 8, D_MODEL), lambda i: (layer, mrow, c))
    row_spec = pl.BlockSpec((tm, D_MODEL), lambda i: (i, 0))
    const2 = lambda a: pl.BlockSpec(a.shape, lambda i: (0, 0))
    x1, oi, of, cnt = pl.pallas_call(
        _post_mix_kernel,
        grid=(nt,),
        in_specs=[row_spec, row_spec, mod_spec(2), const2(w_out_bf), const2(ln_g), const2(ln_b),
                  mod_spec(3), mod_spec(4), const2(router_wt), const2(rb), const2(tri)],
        out_specs=[row_spec,
                   pl.BlockSpec((1, 8, tm), lambda i: (i, 0, 0)),
                   pl.BlockSpec((1, 8, tm), lambda i: (i, 0, 0)),
                   pl.BlockSpec((BIN_ROWS, 128), lambda i: (0, 0))],
        out_shape=[jax.ShapeDtypeStruct((rows, D_MODEL), F32),
                   jax.ShapeDtypeStruct((nt, 8, tm), I32),
                   jax.ShapeDtypeStruct((nt, 8, tm), F32),
                   jax.ShapeDtypeStruct((BIN_ROWS, 128), I32)],
        scratch_shapes=[pltpu.VMEM((BIN_ROWS, 128), F32)],
        compiler_params=_cparams(("arbitrary",)),
        name="post_mix",
    )(mixed, x, mod, w_out_bf, ln_g, ln_b, mod, mod, router_wt, rb, tri)
    return x1, _unpack_route(oi, of, cnt)


DMA_UNROLL = 8
ROW_GROUPS = MOE_TM // DMA_UNROLL


def _for_rows(n, fn):
    def body_u(j, carry):
        for u in range(DMA_UNROLL):
            fn(j, u)
        return carry
    lax.fori_loop(0, n // DMA_UNROLL, body_u, 0)

    def body_1(r, carry):
        fn(r // DMA_UNROLL, r % DMA_UNROLL)
        return carry
    lax.fori_loop((n // DMA_UNROLL) * DMA_UNROLL, n, body_1, 0)


def _wait_rows(n, buf, sem):
    p = ROW_GROUPS
    while p >= 1:
        @pl.when((n & (p * DMA_UNROLL)) != 0)
        def _(p=p):
            pltpu.make_async_copy(buf.at[pl.ds(0, p)], buf.at[pl.ds(0, p)], sem).wait()
        p //= 2
    p = DMA_UNROLL // 2
    while p >= 1:
        @pl.when((n & p) != 0)
        def _(p=p):
            pltpu.make_async_copy(buf.at[0, pl.ds(0, p)], buf.at[0, pl.ds(0, p)], sem).wait()
        p //= 2


def _moe_sorted_kernel(ea_ref, eb_ref, nvalid_ref, pos_ref,
                       x_hbm, sh_ref, sc_ref, gate_ref, rw_ref,
                       wga_ref, wua_ref, wda_ref, wgb_ref, wub_ref, wdb_ref,
                       lng_ref, lnb_ref,
                       out_hbm,
                       inv, xbuf, obuf, wg_a, wu_a, wd_a, wg_b, wu_b, wd_b, gsem, ssem):
    i = pl.program_id(0)
    nt = pl.num_programs(0)
    slot = i % 2
    n_tok = pos_ref.shape[0]

    def start_gather(t, dst_slot):
        def one(j, u):
            tok = inv[t * MOE_TM + j * DMA_UNROLL + u]
            pltpu.make_async_copy(x_hbm.at[pl.ds(tok, 1)], xbuf.at[dst_slot, j, pl.ds(u, 1)],
                                  gsem.at[dst_slot]).start()
        _for_rows(nvalid_ref[t], one)

    @pl.when(i == 0)
    def _():
        def body(j, carry):
            for u in range(DMA_UNROLL):
                t = j * DMA_UNROLL + u
                inv[pos_ref[t]] = t
            return carry
        lax.fori_loop(0, n_tok // DMA_UNROLL, body, 0)
        xbuf[...] = jnp.zeros_like(xbuf)
        start_gather(0, 0)

    @pl.when(i >= 2)
    def _():
        _wait_rows(nvalid_ref[jnp.maximum(i - 2, 0)], obuf.at[slot], ssem.at[slot])

    @pl.when(nvalid_ref[i] > 0)
    def _():
        _wait_rows(nvalid_ref[i], xbuf.at[slot], gsem.at[slot])

        @pl.when(i + 1 < nt)
        def _():
            start_gather(jnp.minimum(i + 1, nt - 1), 1 - slot)

        @pl.when(jnp.logical_or(i == 0, ea_ref[i] != ea_ref[jnp.maximum(i - 1, 0)]))
        def _():
            wg_a[...] = wga_ref[0, 0].astype(BF16)
            wu_a[...] = wua_ref[0, 0].astype(BF16)
            wd_a[...] = wda_ref[0, 0].astype(BF16)

        @pl.when(jnp.logical_or(i == 0, eb_ref[i] != eb_ref[jnp.maximum(i - 1, 0)]))
        def _():
            wg_b[...] = wgb_ref[0, 0].astype(BF16)
            wu_b[...] = wub_ref[0, 0].astype(BF16)
            wd_b[...] = wdb_ref[0, 0].astype(BF16)

        x = xbuf[slot].reshape(MOE_TM, D_MODEL)
        h2 = (x * (1.0 + sc_ref[0][:1]) + sh_ref[0][:1]).astype(BF16)

        def expert(wg, wu, wd):
            a = _silu(jnp.dot(h2, wg[...], preferred_element_type=F32)) * \
                jnp.dot(h2, wu[...], preferred_element_type=F32)
            return jnp.dot(a.astype(BF16), wd[...], preferred_element_type=F32)

        aff = 1.0 / (1.0 + jnp.exp(-jnp.dot(h2, rw_ref[...], preferred_element_type=F32)))
        lane = lax.broadcasted_iota(I32, aff.shape, 1)
        a_lo = jnp.sum(jnp.where(lane == ea_ref[i], aff, 0.0), axis=-1, keepdims=True)
        a_hi = jnp.sum(jnp.where(lane == eb_ref[i], aff, 0.0), axis=-1, keepdims=True)
        a_sum = a_lo + a_hi
        y = (a_lo / a_sum) * expert(wg_a, wu_a, wd_a) + (a_hi / a_sum) * expert(wg_b, wu_b, wd_b)
        out = _ln(DN_ALPHA * x + (1.0 + gate_ref[0][:1]) * y, lng_ref[...], lnb_ref[...])
        obuf[slot] = out.reshape(ROW_GROUPS, DMA_UNROLL, D_MODEL)

        def scatter_one(j, u):
            tok = inv[i * MOE_TM + j * DMA_UNROLL + u]
            pltpu.make_async_copy(obuf.at[slot, j, pl.ds(u, 1)], out_hbm.at[pl.ds(tok, 1)],
                                  ssem.at[slot]).start()
        _for_rows(nvalid_ref[i], scatter_one)

    @pl.when(i == nt - 1)
    def _():
        _wait_rows(nvalid_ref[jnp.maximum(i - 1, 0)], obuf.at[1 - slot], ssem.at[1 - slot])
        _wait_rows(nvalid_ref[i], obuf.at[slot], ssem.at[slot])


def _moe_sorted(x1, mod, layer, mod_row0, route, rw_pad, wg, wu, wd, ln_g, ln_b):
    _, _, bin_id, rank, _, _, counts = route
    rows = x1.shape[0]
    nt = rows // MOE_TM + N_BINS
    ns = nt * MOE_TM
    tiles = (counts + MOE_TM - 1) // MOE_TM
    tile_end = jnp.cumsum(tiles)
    row_start = (tile_end - tiles) * MOE_TM
    pos = row_start[bin_id] + rank
    total = tile_end[-1]
    t = jnp.arange(nt, dtype=I32)
    tile_bin = jnp.searchsorted(tile_end, jnp.minimum(t, total - 1), side="right").astype(I32)
    tile_bin = jnp.clip(tile_bin, 0, N_BINS - 1)
    bin_lo = jnp.asarray([g * 4 + _PAIR_LO[p] for g in range(N_GROUPS) for p in range(N_PAIRS)], I32)
    bin_hi = jnp.asarray([g * 4 + _PAIR_HI[p] for g in range(N_GROUPS) for p in range(N_PAIRS)], I32)
    tile_ea, tile_eb = bin_lo[tile_bin], bin_hi[tile_bin]
    nvalid = jnp.clip(counts[tile_bin] - (t - (tile_end - tiles)[tile_bin]) * MOE_TM, 0, MOE_TM)
    nvalid = jnp.where(t < total, nvalid, 0).astype(I32)
    mrow = mod_row0 // 8
    mod_spec = lambda c: pl.BlockSpec((1, 8, D_MODEL), lambda i, *_: (layer, mrow, c))
    w_spec = lambda shape, ref_idx: pl.BlockSpec(
        (1, 1) + shape, lambda i, ea_r, eb_r, *_: (layer, (ea_r, eb_r)[ref_idx][i], 0, 0))
    any_spec = pl.BlockSpec(memory_space=pl.ANY)
    w_bf = lambda shape: pltpu.VMEM(shape, BF16)
    grid_spec = pltpu.PrefetchScalarGridSpec(
        num_scalar_prefetch=4,
        grid=(nt,),
        in_specs=[any_spec,
                  mod_spec(3), mod_spec(4), mod_spec(5),
                  pl.BlockSpec((D_MODEL, 128), lambda i, *_: (0, 0)),
                  w_spec((D_MODEL, D_FF), 0), w_spec((D_MODEL, D_FF), 0), w_spec((D_FF, D_MODEL), 0),
                  w_spec((D_MODEL, D_FF), 1), w_spec((D_MODEL, D_FF), 1), w_spec((D_FF, D_MODEL), 1),
                  pl.BlockSpec((1, D_MODEL), lambda i, *_: (0, 0)),
                  pl.BlockSpec((1, D_MODEL), lambda i, *_: (0, 0))],
        out_specs=any_spec,
        scratch_shapes=[pltpu.SMEM((ns,), I32),
                        pltpu.VMEM((2, ROW_GROUPS, DMA_UNROLL, D_MODEL), F32),
                        pltpu.VMEM((2, ROW_GROUPS, DMA_UNROLL, D_MODEL), F32),
                        w_bf((D_MODEL, D_FF)), w_bf((D_MODEL, D_FF)), w_bf((D_FF, D_MODEL)),
                        w_bf((D_MODEL, D_FF)), w_bf((D_MODEL, D_FF)), w_bf((D_FF, D_MODEL)),
                        pltpu.SemaphoreType.DMA((2,)),
                        pltpu.SemaphoreType.DMA((2,))],
    )
    return pl.pallas_call(
        _moe_sorted_kernel,
        grid_spec=grid_spec,
        out_shape=jax.ShapeDtypeStruct((rows, D_MODEL), F32),
        compiler_params=_cparams(("arbitrary",)),
        name="moe_sorted",
    )(tile_ea, tile_eb, nvalid, pos, x1, mod, mod, mod, rw_pad,
      wg, wu, wd, wg, wu, wd, ln_g, ln_b)


def _moe_dense_kernel(x_ref, dw_ref, sh_ref, sc_ref, gate_ref, wg_ref, wu_ref, wd_ref,
                      lng_ref, lnb_ref, o_ref, acc):
    e = pl.program_id(0)

    @pl.when(e == 0)
    def _():
        acc[...] = jnp.zeros_like(acc)

    x = x_ref[...]
    h2 = (x * (1.0 + sc_ref[0]) + sh_ref[0]).astype(BF16)
    a = _silu(jnp.dot(h2, wg_ref[0, 0].astype(BF16), preferred_element_type=F32)) * \
        jnp.dot(h2, wu_ref[0, 0].astype(BF16), preferred_element_type=F32)
    y = jnp.dot(a.astype(BF16), wd_ref[0, 0].astype(BF16), preferred_element_type=F32)
    acc[...] = acc[...] + dw_ref[0][:, :1] * y

    @pl.when(e == pl.num_programs(0) - 1)
    def _():
        o_ref[...] = _ln(DN_ALPHA * x + (1.0 + gate_ref[0]) * acc[...], lng_ref[...], lnb_ref[...])


def _moe_dense(x1, mod, layer, route, wg, wu, wd, ln_g, ln_b):
    ea, eb, _, _, w_lo, w_hi, _ = route
    nb = x1.shape[0]
    eids = jnp.arange(N_EXPERTS, dtype=I32)[:, None]
    dw = jnp.where(eids == ea[None, :], w_lo[None, :], 0.0) + \
        jnp.where(eids == eb[None, :], w_hi[None, :], 0.0)
    dw = jnp.broadcast_to(dw[:, :, None], (N_EXPERTS, nb, 128))
    mod_spec = lambda c: pl.BlockSpec((1, nb, D_MODEL), lambda e: (layer, 0, c))
    const2 = lambda a: pl.BlockSpec(a.shape, lambda e: (0, 0))
    return pl.pallas_call(
        _moe_dense_kernel,
        grid=(N_EXPERTS,),
        in_specs=[const2(x1), pl.BlockSpec((1, nb, 128), lambda e: (e, 0, 0)),
                  mod_spec(3), mod_spec(4), mod_spec(5),
                  pl.BlockSpec((1, 1, D_MODEL, D_FF), lambda e: (layer, e, 0, 0)),
                  pl.BlockSpec((1, 1, D_MODEL, D_FF), lambda e: (layer, e, 0, 0)),
                  pl.BlockSpec((1, 1, D_FF, D_MODEL), lambda e: (layer, e, 0, 0)),
                  const2(ln_g), const2(ln_b)],
        out_specs=const2(x1),
        out_shape=jax.ShapeDtypeStruct(x1.shape, F32),
        scratch_shapes=[pltpu.VMEM(x1.shape, F32)],
        compiler_params=_cparams(("arbitrary",)),
        name="moe_dense",
    )(x1, dw, mod, mod, mod, wg, wu, wd, ln_g, ln_b)


def kernel(x_prompt, x_sample, state_ret, cache_swa_k, cache_swa_v, c_prompt, c_sample, w_in, w_out, ret_gn_gain, swa_sinks, rel_bias_table, ada_w, ada_b, ln1_g, ln1_b, ln2_g, ln2_b, router_w, router_b, exp_w_gate, exp_w_up, exp_w_down):
    seq = x_prompt.shape[1]
    nb = x_sample.shape[0]
    past_len = 16384
    assert x_prompt.shape[0] == 1 and x_sample.shape[1] == 1

    c_all = jnp.concatenate([c_sample, c_prompt, jnp.zeros((7, D_MODEL), F32)], axis=0)
    mod = _ada(c_all, ada_w, ada_b)
    prompt_row = nb

    bias_tabs = _bias_tables(rel_bias_table.astype(F32))
    gl, decay, xi, zeta, gamma1 = _ret_consts()
    rope_p = _rope_tables(jnp.arange(seq, dtype=I32))
    rope_s = _rope_tables(jnp.full((nb,), past_len, I32))
    router_wt = router_w.astype(F32).T
    rw_pad = jnp.pad(router_w.astype(BF16), ((0, 0), (0, 128 - N_EXPERTS)))
    zeta_p = jnp.tile(zeta, (INPROJ_TM // CHUNK, 1))
    zeta_s = jnp.ones((nb, RET_W), F32)

    xp = x_prompt.reshape(seq, D_MODEL)
    xs = x_sample.reshape(nb, D_MODEL)
    st_p, k_p, v_p, st_s, k_s, v_s = [], [], [], [], [], []
    for l in range(DEPTH):
        w_in_bf = w_in[l].astype(BF16)
        w_out_bf = w_out[l].astype(BF16)
        experts = (exp_w_gate.astype(F32), exp_w_up.astype(F32), exp_w_down.astype(F32))
        gain = ret_gn_gain[l].astype(F32)
        sinks = swa_sinks[l].astype(F32)
        row = lambda a: a[l].astype(F32).reshape(1, D_MODEL)

        proj = _inproj(xp, mod, l, prompt_row, 1, w_in_bf, rope_p, zeta_p, INPROJ_TM)
        mixed, st = _mix_prompt(proj, (gl, decay, xi), bias_tabs, sinks, gain.reshape(1, RET_W))
        skv = proj[6]
        st_p.append(st.reshape(1, N_RET_HEADS, HEAD_DIM, HEAD_DIM))
        k_p.append(skv[seq - CHUNK:, :SWA_KVW].reshape(1, CHUNK, N_KV_HEADS, HEAD_DIM))
        v_p.append(skv[seq - CHUNK:, SWA_KVW:].reshape(1, CHUNK, N_KV_HEADS, HEAD_DIM))
        x1, route = _post_mix(mixed, xp, mod, l, prompt_row, w_out_bf, row(ln1_g), row(ln1_b),
                              router_wt, router_b)
        xp = _moe_sorted(x1, mod, l, prompt_row, route, rw_pad, *experts, row(ln2_g), row(ln2_b))

        proj = _inproj(xs, mod, l, 0, nb, w_in_bf, rope_s, zeta_s, nb)
        x1, nst, nk, nv = _mix_sample(proj, xs, mod, l, w_out_bf, gamma1, bias_tabs[1], sinks, gain,
                                      row(ln1_g), row(ln1_b), state_ret[l].astype(F32),
                                      cache_swa_k[l], cache_swa_v[l])
        st_s.append(nst)
        k_s.append(nk)
        v_s.append(nv)
        route = _router(x1, mod, l, 0, nb, router_wt, router_b, nb)
        xs = _moe_dense(x1, mod, l, route, *experts, row(ln2_g), row(ln2_b))

    return (xp.reshape(1, seq, D_MODEL), xs.reshape(nb, 1, D_MODEL),
            jnp.stack(st_p), jnp.stack(k_p), jnp.stack(v_p),
            jnp.stack(st_s), jnp.stack(k_s), jnp.stack(v_s))
```

```python
import functools
import math

import numpy as np
import jax
import jax.numpy as jnp
from jax import lax
from jax.experimental import pallas as pl
from jax.experimental.pallas import tpu as pltpu

F32 = jnp.float32
BF16 = jnp.bfloat16
I32 = jnp.int32

D_MODEL = 1024
DEPTH = 2
HEAD_DIM = 64
N_RET_HEADS = 8
N_SWA_HEADS = 8
N_KV_HEADS = 2
GQA_GROUP = N_SWA_HEADS // N_KV_HEADS
RET_W = N_RET_HEADS * HEAD_DIM
SWA_QW = N_SWA_HEADS * HEAD_DIM
SWA_KVW = N_KV_HEADS * HEAD_DIM
PROJ_W = 4 * RET_W + SWA_QW + 2 * SWA_KVW
CHUNK = 128
ROPE_BASE = 10000.0
N_BUCKETS = 32
MAX_DISTANCE = 128
N_EXPERTS = 16
N_GROUPS = 4
EXPERTS_PER_GROUP = 4
D_FF = 512
LN_EPS = 1e-5
DN_ALPHA = (2 * DEPTH) ** 0.25
QK_SCALE = HEAD_DIM ** -0.5
NEG_BIG = -1e30

N_PAIRS = 6
N_BINS = N_GROUPS * N_PAIRS
BIN_ROWS = 32
MOE_TM = 256
INPROJ_TM = 512
VMEM_LIMIT = 56 * 1024 * 1024

_PAIR_LO = (0, 0, 0, 1, 1, 2)
_PAIR_HI = (1, 2, 3, 2, 3, 3)


def _cparams(sem):
    return pltpu.CompilerParams(dimension_semantics=sem, vmem_limit_bytes=VMEM_LIMIT)


def _ln(v, g, b):
    mu = jnp.mean(v, axis=-1, keepdims=True)
    c = v - mu
    var = jnp.mean(c * c, axis=-1, keepdims=True)
    return c * lax.rsqrt(var + LN_EPS) * g + b


def _silu(v):
    return v * (1.0 / (1.0 + jnp.exp(-v)))


def _ada_kernel(c_ref, w_ref, b_ref, o_ref):
    o_ref[0] = jnp.dot(c_ref[...].astype(BF16), w_ref[0].astype(BF16),
                       preferred_element_type=F32) + b_ref[0]


def _ada(c_all, ada_w, ada_b):
    rows = c_all.shape[0]
    nt = 6 * D_MODEL // 1024
    return pl.pallas_call(
        _ada_kernel,
        grid=(DEPTH, nt),
        in_specs=[pl.BlockSpec((rows, D_MODEL), lambda l, j: (0, 0)),
                  pl.BlockSpec((1, D_MODEL, 1024), lambda l, j: (l, 0, j)),
                  pl.BlockSpec((1, 1, 1024), lambda l, j: (l, 0, j))],
        out_specs=pl.BlockSpec((1, rows, 1024), lambda l, j: (l, 0, j)),
        out_shape=jax.ShapeDtypeStruct((DEPTH, rows, 6 * D_MODEL), F32),
        compiler_params=_cparams(("arbitrary", "arbitrary")),
        name="ada",
    )(c_all, ada_w, ada_b.reshape(DEPTH, 1, 6 * D_MODEL))


def _bias_kernel(tab_ref, bkt_ref, fold_ref, row_ref):
    bkt = bkt_ref[...]
    rows = lax.broadcasted_iota(I32, (CHUNK, CHUNK), 0)
    cols = lax.broadcasted_iota(I32, (CHUNK, CHUNK), 1)
    from_prev = cols > rows
    for h in range(N_SWA_HEADS):
        acc = jnp.zeros(bkt.shape, F32)
        for b in range(N_BUCKETS):
            acc = jnp.where(bkt == b, tab_ref[b, h], acc)
        own = acc[:, CHUNK:]
        fold_ref[0, h] = jnp.where(from_prev, NEG_BIG, own)
        fold_ref[1, h] = jnp.where(from_prev, acc[:, :CHUNK], own)
        row_ref[h:h + 1, :] = own[CHUNK - 1:CHUNK, :]


def _t5_bucket(rel):
    max_exact = N_BUCKETS // 2
    relf = jnp.maximum(rel, 1).astype(F32)
    large = max_exact + (jnp.log(relf / max_exact) / math.log(MAX_DISTANCE / max_exact)
                         * (N_BUCKETS - max_exact)).astype(I32)
    large = jnp.minimum(large, N_BUCKETS - 1)
    return jnp.where(rel < max_exact, rel, large)


def _bias_tables(rel_bias_table):
    qi = jnp.arange(CHUNK)
    si = jnp.arange(2 * CHUNK)
    rel = CHUNK + qi[:, None] - si[None, :]
    bkt = _t5_bucket(jnp.maximum(rel, 0)).astype(I32)
    return pl.pallas_call(
        _bias_kernel,
        in_specs=[pl.BlockSpec(memory_space=pltpu.SMEM),
                  pl.BlockSpec((CHUNK, 2 * CHUNK), lambda: (0, 0))],
        out_specs=[pl.BlockSpec((2, N_SWA_HEADS, CHUNK, CHUNK), lambda: (0, 0, 0, 0)),
                   pl.BlockSpec((N_SWA_HEADS, CHUNK), lambda: (0, 0))],
        out_shape=[jax.ShapeDtypeStruct((2, N_SWA_HEADS, CHUNK, CHUNK), F32),
                   jax.ShapeDtypeStruct((N_SWA_HEADS, CHUNK), F32)],
        name="t5_bias",
    )(rel_bias_table, bkt)


def _rotary(v, cos, s_lo, s_hi):
    outs = []
    for j in range(RET_W // 128):
        blk = v[:, j * 128:(j + 1) * 128]
        outs.append(blk * cos + pltpu.roll(blk, 96, 1) * s_lo + pltpu.roll(blk, 32, 1) * s_hi)
    return jnp.concatenate(outs, axis=-1)


def _inproj_kernel(x_ref, sh_ref, sc_ref, w_ref, cs_ref, zeta_ref,
                   q_ref, k_ref, kz_ref, v_ref, g_ref, sq_ref, skv_ref, *, mod_rows):
    sh = sh_ref[0][:mod_rows]
    sc = sc_ref[0][:mod_rows]
    h = (x_ref[...] * (1.0 + sc) + sh).astype(BF16)
    cs = cs_ref[...]
    first = (lax.broadcasted_iota(I32, cs.shape, 1) % HEAD_DIM) < (HEAD_DIM // 2)
    cos = jnp.where(first, cs, pltpu.roll(cs, 32, 1))
    s_lo = jnp.where(first, -pltpu.roll(cs, 96, 1), 0.0)
    s_hi = jnp.where(first, 0.0, cs)

    def proj(lo, hi):
        return jnp.dot(h, w_ref[:, lo:hi], preferred_element_type=F32)

    q_ref[...] = _rotary(proj(0, RET_W), cos, s_lo, s_hi).astype(BF16)
    k = _rotary(proj(RET_W, 2 * RET_W), cos, s_lo, s_hi) * QK_SCALE
    k_ref[...] = k.astype(BF16)
    kz_ref[...] = (k * zeta_ref[...]).astype(BF16)
    v_ref[...] = proj(2 * RET_W, 3 * RET_W).astype(BF16)
    g_ref[...] = proj(3 * RET_W, 4 * RET_W)
    sq_ref[...] = (proj(4 * RET_W, 4 * RET_W + SWA_QW) * QK_SCALE).astype(BF16)
    skv_ref[...] = proj(4 * RET_W + SWA_QW, PROJ_W)


def _inproj(x, rows, mod, layer, mod_row0, mod_rows, w_in_bf, rope, zeta_tile, tm):
    mblk = 8 if mod_rows == 1 else mod_rows
    mrow = mod_row0 // mblk
    row_spec = lambda w: pl.BlockSpec((tm, w), lambda i: (i, 0))
    mod_spec = lambda c: pl.BlockSpec((1, mblk, D_MODEL), lambda i: (layer, mrow, c))
    return pl.pallas_call(
        functools.partial(_inproj_kernel, mod_rows=mod_rows),
        grid=(rows // tm,),
        in_specs=[row_spec(D_MODEL), mod_spec(0), mod_spec(1),
                  pl.BlockSpec((D_MODEL, PROJ_W), lambda i: (0, 0)),
                  row_spec(128),
                  pl.BlockSpec((tm, RET_W), lambda i: (0, 0))],
        out_specs=[row_spec(RET_W), row_spec(RET_W), row_spec(RET_W), row_spec(RET_W), row_spec(RET_W),
                   row_spec(SWA_QW), row_spec(2 * SWA_KVW)],
        out_shape=[jax.ShapeDtypeStruct((rows, RET_W), BF16),
                   jax.ShapeDtypeStruct((rows, RET_W), BF16),
                   jax.ShapeDtypeStruct((rows, RET_W), BF16),
                   jax.ShapeDtypeStruct((rows, RET_W), BF16),
                   jax.ShapeDtypeStruct((rows, RET_W), F32),
                   jax.ShapeDtypeStruct((rows, SWA_QW), BF16),
                   jax.ShapeDtypeStruct((rows, 2 * SWA_KVW), F32)],
        compiler_params=_cparams(("arbitrary",)),
        name="inproj",
    )(x, mod, mod, w_in_bf, rope, zeta_tile)


def _rope_tables(pos):
    half = HEAD_DIM // 2
    inv = ROPE_BASE ** (-jnp.arange(half, dtype=F32) / half)
    ang = pos.astype(F32)[:, None] * inv[None, :]
    cos, sin = jnp.cos(ang), jnp.sin(ang)
    return jnp.concatenate([cos, sin, cos, sin], axis=-1)


def _mix_prompt_kernel(gl_ref, sink_ref,
                       q_ref, k_ref, kz_ref, v_ref, g_ref, sq_ref, kvc_ref, kvp_ref,
                       decay_ref, xi_ref, bias_ref, gain_ref,
                       mixed_ref, st_ref, state):
    i = pl.program_id(0)

    @pl.when(i == 0)
    def _():
        state[...] = jnp.zeros_like(state)

    q = q_ref[...]
    k = k_ref[...]
    v = v_ref[...]
    kz = kz_ref[...]

    intra, inter = [], []
    for h in range(N_RET_HEADS):
        sl = slice(h * HEAD_DIM, (h + 1) * HEAD_DIM)
        qh, kh, vh = q[:, sl], k[:, sl], v[:, sl]
        s = lax.dot_general(qh, kh, (((1,), (1,)), ((), ())), preferred_element_type=F32)
        p = (s * decay_ref[h]).astype(BF16)
        intra.append(jnp.dot(p, vh, preferred_element_type=F32))
        s_old = state[h]
        inter.append(jnp.dot(qh, s_old.astype(BF16), preferred_element_type=F32))
        upd = lax.dot_general(kz[:, sl], vh, (((0,), (0,)), ((), ())), preferred_element_type=F32)
        state[h] = gl_ref[h] * s_old + upd

    xi = xi_ref[...]
    g = g_ref[...]
    ret_parts = []
    for h in range(N_RET_HEADS):
        sl = slice(h * HEAD_DIM, (h + 1) * HEAD_DIM)
        o = intra[h] + inter[h] * xi[:, sl]
        mu = jnp.mean(o, axis=-1, keepdims=True)
        c = o - mu
        var = jnp.mean(c * c, axis=-1, keepdims=True)
        ret_parts.append(c * lax.rsqrt(var + LN_EPS))
    ret_o = jnp.concatenate(ret_parts, axis=-1) * gain_ref[...] * _silu(g)

    sq = sq_ref[...]
    kvc = kvc_ref[...]
    kvp = kvp_ref[...]
    swa_parts = []
    from_prev = (lax.broadcasted_iota(I32, (CHUNK, CHUNK), 1) >
                 lax.broadcasted_iota(I32, (CHUNK, CHUNK), 0))
    for kv in range(N_KV_HEADS):
        ks = slice(kv * HEAD_DIM, (kv + 1) * HEAD_DIM)
        vs = slice(SWA_KVW + kv * HEAD_DIM, SWA_KVW + (kv + 1) * HEAD_DIM)
        kk = jnp.concatenate([kvp[:, ks], kvc[:, ks]], axis=0).astype(BF16)
        vv = jnp.concatenate([kvp[:, vs], kvc[:, vs]], axis=0).astype(BF16)
        for j in range(GQA_GROUP):
            hh = kv * GQA_GROUP + j
            qh = sq[:, hh * HEAD_DIM:(hh + 1) * HEAD_DIM]
            s2 = lax.dot_general(qh, kk, (((1,), (1,)), ((), ())), preferred_element_type=F32)
            s = jnp.where(from_prev, s2[:, :CHUNK], s2[:, CHUNK:]) + bias_ref[0, hh]
            sink = sink_ref[hh]
            m = jnp.maximum(jnp.max(s, axis=-1, keepdims=True), sink)
            p = jnp.exp(s - m)
            den = jnp.sum(p, axis=-1, keepdims=True) + jnp.exp(sink - m)
            pn = p / den
            p2 = jnp.concatenate([jnp.where(from_prev, pn, 0.0), jnp.where(from_prev, 0.0, pn)], axis=-1)
            swa_parts.append(jnp.dot(p2.astype(BF16), vv, preferred_element_type=F32))
    swa_o = jnp.concatenate(swa_parts, axis=-1)

    mixed_ref[...] = jnp.concatenate([ret_o, swa_o], axis=-1).astype(BF16)

    @pl.when(i == pl.num_programs(0) - 1)
    def _():
        st_ref[...] = state[...]


def _mix_prompt(proj, ret_consts, bias_tabs, sinks, gain):
    q, k, kz, v, g, sq, skv = proj
    rows = q.shape[0]
    nblk = rows // CHUNK
    gl, decay, xi = ret_consts
    bias_fold = bias_tabs[0]
    row_spec = lambda w: pl.BlockSpec((CHUNK, w), lambda i: (i, 0))
    const2 = lambda a: pl.BlockSpec(a.shape, lambda i: (0, 0))
    const3 = lambda a: pl.BlockSpec(a.shape, lambda i: (0, 0, 0))
    smem = pl.BlockSpec(memory_space=pltpu.SMEM)
    return pl.pallas_call(
        _mix_prompt_kernel,
        grid=(nblk,),
        in_specs=[smem, smem,
                  row_spec(RET_W), row_spec(RET_W), row_spec(RET_W), row_spec(RET_W), row_spec(RET_W),
                  row_spec(SWA_QW), row_spec(2 * SWA_KVW),
                  pl.BlockSpec((CHUNK, 2 * SWA_KVW), lambda i: (jnp.maximum(i - 1, 0), 0)),
                  const3(decay), const2(xi),
                  pl.BlockSpec((1, N_SWA_HEADS, CHUNK, CHUNK), lambda i: (jnp.minimum(i, 1), 0, 0, 0)),
                  const2(gain)],
        out_specs=[row_spec(D_MODEL),
                   pl.BlockSpec((N_RET_HEADS, HEAD_DIM, HEAD_DIM), lambda i: (0, 0, 0))],
        out_shape=[jax.ShapeDtypeStruct((rows, D_MODEL), BF16),
                   jax.ShapeDtypeStruct((N_RET_HEADS, HEAD_DIM, HEAD_DIM), F32)],
        scratch_shapes=[pltpu.VMEM((N_RET_HEADS, HEAD_DIM, HEAD_DIM), F32)],
        compiler_params=_cparams(("arbitrary",)),
        name="mix_prompt",
    )(gl, sinks, q, k, kz, v, g, sq, skv, skv, decay, xi, bias_fold, gain)


def _ret_consts():
    lg = jnp.log(1.0 - 2.0 ** (-5.0 - jnp.arange(N_RET_HEADS, dtype=F32)))
    idx = jnp.arange(CHUNK, dtype=F32)
    diff = idx[:, None] - idx[None, :]
    decay = jnp.where(diff >= 0, jnp.exp(jnp.maximum(diff, 0.0)[None] * lg[:, None, None]), 0.0)
    xi = jnp.exp((idx + 1.0)[:, None] * lg[None, :])
    zeta = jnp.exp((CHUNK - 1.0 - idx)[:, None] * lg[None, :])
    gl = jnp.exp(CHUNK * lg)
    expand = lambda a: jnp.repeat(a, HEAD_DIM, axis=1)
    return gl, decay, expand(xi), expand(zeta), jnp.exp(1.0 * lg)


SAMPLE_BB = 8


def _mix_sample_kernel(g1_ref, sink_ref,
                       qt_ref, kt_ref, v3_ref, sq3_ref, knew_ref, vnew_ref,
                       st_ref, ck_ref, cv_ref, g3_ref, x_ref, gate_ref,
                       wout_ref, brow_ref, gain_ref, lng_ref, lnb_ref,
                       x1_ref, nst_ref, nk_ref, nv_ref, ret_scr, swa_scr):
    i = pl.program_id(0)
    row = lax.broadcasted_iota(I32, (N_SWA_HEADS, 2 * HEAD_DIM), 0)
    lane = lax.broadcasted_iota(I32, (N_SWA_HEADS, 2 * HEAD_DIM), 1)
    own_half = (row // GQA_GROUP) == (lane // HEAD_DIM)
    sink_col = jnp.concatenate(
        [jnp.full((1, 1), sink_ref[hh], F32) for hh in range(N_SWA_HEADS)], axis=0)
    brow = brow_ref[...]
    qt = qt_ref[0]
    kt = kt_ref[0]

    for b in range(SAMPLE_BB):
        o_rows = []
        for h in range(N_RET_HEADS):
            rs = slice(h * HEAD_DIM, (h + 1) * HEAD_DIM)
            s_old = st_ref[b, rs, :]
            kcol = kt[rs, b:b + 1]
            qcol = qt[rs, b:b + 1]
            vrow = v3_ref[b, h:h + 1, :]
            s_new = g1_ref[h] * s_old + kcol * vrow
            nst_ref[b, rs, :] = s_new
            o_rows.append(jnp.sum(qcol * s_new, axis=0, keepdims=True))
        ret_scr[i * SAMPLE_BB + b] = jnp.concatenate(o_rows, axis=0)

        kk = jnp.concatenate([ck_ref[b, 1:, :], knew_ref[b:b + 1, :]], axis=0)
        vv = jnp.concatenate([cv_ref[b, 1:, :], vnew_ref[b:b + 1, :]], axis=0)
        nk_ref[b] = kk
        nv_ref[b] = vv
        q8 = sq3_ref[b]
        qblk = jnp.where(own_half, jnp.concatenate([q8, q8], axis=-1), 0.0).astype(BF16)
        s = lax.dot_general(qblk, kk.astype(BF16), (((1,), (1,)), ((), ())),
                            preferred_element_type=F32) + brow
        m = jnp.maximum(jnp.max(s, axis=-1, keepdims=True), sink_col)
        p = jnp.exp(s - m)
        den = jnp.sum(p, axis=-1, keepdims=True) + jnp.exp(sink_col - m)
        o = jnp.dot((p / den).astype(BF16), vv.astype(BF16), preferred_element_type=F32)
        swa_scr[i * SAMPLE_BB + b] = jnp.where(own_half[:, :HEAD_DIM], o[:, :HEAD_DIM], o[:, HEAD_DIM:])

    @pl.when(i == pl.num_programs(0) - 1)
    def _():
        y = jnp.zeros(x_ref.shape, F32)
        for h in range(N_RET_HEADS):
            o = ret_scr[:, h, :]
            mu = jnp.mean(o, axis=-1, keepdims=True)
            c = o - mu
            var = jnp.mean(c * c, axis=-1, keepdims=True)
            r = c * lax.rsqrt(var + LN_EPS) * gain_ref[h:h + 1, :] * _silu(g3_ref[h])
            y = y + jnp.dot(r.astype(BF16), wout_ref[h * HEAD_DIM:(h + 1) * HEAD_DIM, :],
                            preferred_element_type=F32)
        for hh in range(N_SWA_HEADS):
            o = swa_scr[:, hh, :].astype(BF16)
            lo = RET_W + hh * HEAD_DIM
            y = y + jnp.dot(o, wout_ref[lo:lo + HEAD_DIM, :], preferred_element_type=F32)
        x1_ref[...] = _ln(DN_ALPHA * x_ref[...] + (1.0 + gate_ref[0]) * y, lng_ref[...], lnb_ref[...])


def _mix_sample(proj, x, mod, layer, w_out_bf, gamma1, bias_row, sinks, gain8, ln_g, ln_b,
                st, ck, cv):
    q, k, _, v, g, sq, skv = proj
    nb = x.shape[0]
    steps = nb // SAMPLE_BB
    to_cols = lambda a: a.astype(F32).reshape(steps, SAMPLE_BB, RET_W).transpose(0, 2, 1)
    qt, kt = to_cols(q), to_cols(k)
    v3 = v.astype(F32).reshape(nb, N_RET_HEADS, HEAD_DIM)
    sq3 = sq.astype(F32).reshape(nb, N_SWA_HEADS, HEAD_DIM)
    g3 = g.reshape(nb, N_RET_HEADS, HEAD_DIM).transpose(1, 0, 2)
    knew, vnew = skv[:, :SWA_KVW], skv[:, SWA_KVW:]
    st2 = st.reshape(nb, RET_W, HEAD_DIM)
    ck2 = ck.reshape(nb, CHUNK, SWA_KVW)
    cv2 = cv.reshape(nb, CHUNK, SWA_KVW)
    smem = pl.BlockSpec(memory_space=pltpu.SMEM)
    blk3 = lambda a, b, c: pl.BlockSpec((a, b, c), lambda i: (i, 0, 0))
    const2 = lambda a: pl.BlockSpec(a.shape, lambda i: (0, 0))
    const3 = lambda a: pl.BlockSpec(a.shape, lambda i: (0, 0, 0))
    x1, nst, nk, nv = pl.pallas_call(
        _mix_sample_kernel,
        grid=(steps,),
        in_specs=[smem, smem,
                  blk3(1, RET_W, SAMPLE_BB), blk3(1, RET_W, SAMPLE_BB),
                  blk3(SAMPLE_BB, N_RET_HEADS, HEAD_DIM), blk3(SAMPLE_BB, N_SWA_HEADS, HEAD_DIM),
                  pl.BlockSpec((SAMPLE_BB, SWA_KVW), lambda i: (i, 0)),
                  pl.BlockSpec((SAMPLE_BB, SWA_KVW), lambda i: (i, 0)),
                  blk3(SAMPLE_BB, RET_W, HEAD_DIM), blk3(SAMPLE_BB, CHUNK, SWA_KVW),
                  blk3(SAMPLE_BB, CHUNK, SWA_KVW),
                  const3(g3), const2(x),
                  pl.BlockSpec((1, nb, D_MODEL), lambda i: (layer, 0, 2)),
                  const2(w_out_bf), const2(bias_row), const2(gain8), const2(ln_g), const2(ln_b)],
        out_specs=[const2(x), blk3(SAMPLE_BB, RET_W, HEAD_DIM), blk3(SAMPLE_BB, CHUNK, SWA_KVW),
                   blk3(SAMPLE_BB, CHUNK, SWA_KVW)],
        out_shape=[jax.ShapeDtypeStruct(x.shape, F32), jax.ShapeDtypeStruct(st2.shape, F32),
                   jax.ShapeDtypeStruct(ck2.shape, F32), jax.ShapeDtypeStruct(cv2.shape, F32)],
        scratch_shapes=[pltpu.VMEM((nb, N_RET_HEADS, HEAD_DIM), F32),
                        pltpu.VMEM((nb, N_SWA_HEADS, HEAD_DIM), F32)],
        compiler_params=_cparams(("arbitrary",)),
        name="mix_sample",
    )(gamma1, sinks, qt, kt, v3, sq3, knew, vnew, st2, ck2, cv2, g3, x, mod,
      w_out_bf, bias_row, gain8, ln_g, ln_b)
    return x1, nst.reshape(st.shape), nk.reshape(ck.shape), nv.reshape(cv.shape)


def _router_kernel(x_ref, sh_ref, sc_ref, rwt_ref, rb_ref, tri_ref,
                   oi_ref, of_ref, cnt_ref, run_ref, *, mod_rows):
    h2 = x_ref[...] * (1.0 + sc_ref[0][:mod_rows]) + sh_ref[0][:mod_rows]
    _route_rows(h2, rwt_ref, rb_ref, tri_ref, oi_ref, of_ref, cnt_ref, run_ref)


def _post_mix_kernel(mixed_ref, x_ref, gate_ref, wout_ref, lng_ref, lnb_ref,
                     sh_ref, sc_ref, rwt_ref, rb_ref, tri_ref,
                     x1_ref, oi_ref, of_ref, cnt_ref, run_ref):
    y = jnp.dot(mixed_ref[...], wout_ref[...], preferred_element_type=F32)
    x1 = _ln(DN_ALPHA * x_ref[...] + (1.0 + gate_ref[0][:1]) * y, lng_ref[...], lnb_ref[...])
    x1_ref[...] = x1
    h2 = x1 * (1.0 + sc_ref[0][:1]) + sh_ref[0][:1]
    _route_rows(h2, rwt_ref, rb_ref, tri_ref, oi_ref, of_ref, cnt_ref, run_ref)


def _route_rows(h2, rwt_ref, rb_ref, tri_ref, oi_ref, of_ref, cnt_ref, run_ref):
    i = pl.program_id(0)

    @pl.when(i == 0)
    def _():
        run_ref[...] = jnp.zeros_like(run_ref)

    logits = lax.dot_general(rwt_ref[...].astype(BF16), h2.astype(BF16), (((1,), (1,)), ((), ())),
                             preferred_element_type=F32)
    aff = 1.0 / (1.0 + jnp.exp(-logits))
    sel = aff + rb_ref[...]
    s = [sel[e:e + 1, :] for e in range(N_EXPERTS)]
    a = [aff[e:e + 1, :] for e in range(N_EXPERTS)]

    def top2sum(v0, v1, v2, v3):
        hi01, lo01 = jnp.maximum(v0, v1), jnp.minimum(v0, v1)
        hi23, lo23 = jnp.maximum(v2, v3), jnp.minimum(v2, v3)
        return jnp.maximum(hi01, hi23) + jnp.maximum(jnp.minimum(hi01, hi23),
                                                     jnp.maximum(lo01, lo23))

    def argmax_first(vals):
        best, idx = vals[0], jnp.zeros(vals[0].shape, I32)
        for j in range(1, len(vals)):
            upd = vals[j] > best
            idx = jnp.where(upd, j, idx)
            best = jnp.where(upd, vals[j], best)
        return idx

    def pick(idx, vals):
        out = vals[-1]
        for j in range(len(vals) - 2, -1, -1):
            out = jnp.where(idx == j, vals[j], out)
        return out

    gi = argmax_first([top2sum(*s[4 * g:4 * g + 4]) for g in range(N_GROUPS)])
    sv = [pick(gi, [s[4 * g + j] for g in range(N_GROUPS)]) for j in range(EXPERTS_PER_GROUP)]
    av = [pick(gi, [a[4 * g + j] for g in range(N_GROUPS)]) for j in range(EXPERTS_PER_GROUP)]
    i1 = argmax_first(sv)
    i2 = argmax_first([jnp.where(i1 == j, -jnp.inf, sv[j]) for j in range(EXPERTS_PER_GROUP)])
    w1, w2 = pick(i1, av), pick(i2, av)
    wsum = w1 + w2
    w1, w2 = w1 / wsum, w2 / wsum
    lo, hi = jnp.minimum(i1, i2), jnp.maximum(i1, i2)
    w_lo = jnp.where(i1 < i2, w1, w2)
    w_hi = jnp.where(i1 < i2, w2, w1)
    pair = jnp.where(lo == 0, hi - 1, jnp.where(lo == 1, hi + 1, 5))
    bin_id = gi * N_PAIRS + pair

    tm = bin_id.shape[1]
    onehot = lax.broadcasted_iota(I32, (BIN_ROWS, tm), 0) == bin_id
    oh_f = jnp.where(onehot, 1.0, 0.0)
    before = jnp.dot(oh_f.astype(BF16), tri_ref[...], preferred_element_type=F32)
    run = run_ref[...]
    run_t = jnp.concatenate([run] * (tm // 128), axis=-1)
    rank = jnp.sum(oh_f * (before + run_t), axis=0, keepdims=True)
    run_new = run + jnp.sum(oh_f, axis=1, keepdims=True)
    run_ref[...] = run_new
    cnt_ref[...] = run_new.astype(I32)

    zi = jnp.zeros_like(bin_id)
    oi_ref[0] = jnp.concatenate([gi * 4 + lo, gi * 4 + hi, bin_id, rank.astype(I32), zi, zi, zi, zi], axis=0)
    zf = jnp.zeros_like(w_lo)
    of_ref[0] = jnp.concatenate([w_lo, w_hi, zf, zf, zf, zf, zf, zf], axis=0)


def _router(x1, mod, layer, mod_row0, mod_rows, router_wt, router_b, tm):
    rows = x1.shape[0]
    nt = rows // tm
    mblk = 8 if mod_rows == 1 else mod_rows
    mrow = mod_row0 // mblk
    tri = jnp.asarray(np.triu(np.ones((tm, tm), np.float32), 1), BF16)
    rb = jnp.broadcast_to(router_b.astype(F32)[:, None], (N_EXPERTS, tm))
    mod_spec = lambda c: pl.BlockSpec((1, mblk, D_MODEL), lambda i: (layer, mrow, c))
    oi, of, cnt = pl.pallas_call(
        functools.partial(_router_kernel, mod_rows=mod_rows),
        grid=(nt,),
        in_specs=[pl.BlockSpec((tm, D_MODEL), lambda i: (i, 0)), mod_spec(3), mod_spec(4),
                  pl.BlockSpec((N_EXPERTS, D_MODEL), lambda i: (0, 0)),
                  pl.BlockSpec((N_EXPERTS, tm), lambda i: (0, 0)),
                  pl.BlockSpec((tm, tm), lambda i: (0, 0))],
        out_specs=[pl.BlockSpec((1, 8, tm), lambda i: (i, 0, 0)),
                   pl.BlockSpec((1, 8, tm), lambda i: (i, 0, 0)),
                   pl.BlockSpec((BIN_ROWS, 128), lambda i: (0, 0))],
        out_shape=[jax.ShapeDtypeStruct((nt, 8, tm), I32),
                   jax.ShapeDtypeStruct((nt, 8, tm), F32),
                   jax.ShapeDtypeStruct((BIN_ROWS, 128), I32)],
        scratch_shapes=[pltpu.VMEM((BIN_ROWS, 128), F32)],
        compiler_params=_cparams(("arbitrary",)),
        name="router",
    )(x1, mod, mod, router_wt, rb, tri)
    return _unpack_route(oi, of, cnt)


def _unpack_route(oi, of, cnt):
    flat = lambda a, r: a[:, r, :].reshape(-1)
    return (flat(oi, 0), flat(oi, 1), flat(oi, 2), flat(oi, 3), flat(of, 0), flat(of, 1),
            cnt[:N_BINS, 0])


POST_TM = 512


def _post_mix(mixed, x, mod, layer, mod_row0, w_out_bf, ln_g, ln_b, router_wt, router_b):
    rows = mixed.shape[0]
    tm = POST_TM
    nt = rows // tm
    mrow = mod_row0 // 8
    tri = jnp.asarray(np.triu(np.ones((tm, tm), np.float32), 1), BF16)
    rb = jnp.broadcast_to(router_b.astype(F32)[:, None], (N_EXPERTS, tm))
    mod_spec = lambda c: pl.BlockSpec((1, 8, D_MODEL), lambda i: (layer, mrow, c))
    row_spec = pl.BlockSpec((tm, D_MODEL), lambda i: (i, 0))
    const2 = lambda a: pl.BlockSpec(a.shape, lambda i: (0, 0))
    x1, oi, of, cnt = pl.pallas_call(
        _post_mix_kernel,
        grid=(nt,),
        in_specs=[row_spec, row_spec, mod_spec(2), const2(w_out_bf), const2(ln_g), const2(ln_b),
                  mod_spec(3), mod_spec(4), const2(router_wt), const2(rb), const2(tri)],
        out_specs=[row_spec,
                   pl.BlockSpec((1, 8, tm), lambda i: (i, 0, 0)),
                   pl.BlockSpec((1, 8, tm), lambda i: (i, 0, 0)),
                   pl.BlockSpec((BIN_ROWS, 128), lambda i: (0, 0))],
        out_shape=[jax.ShapeDtypeStruct((rows, D_MODEL), F32),
                   jax.ShapeDtypeStruct((nt, 8, tm), I32),
                   jax.ShapeDtypeStruct((nt, 8, tm), F32),
                   jax.ShapeDtypeStruct((BIN_ROWS, 128), I32)],
        scratch_shapes=[pltpu.VMEM((BIN_ROWS, 128), F32)],
        compiler_params=_cparams(("arbitrary",)),
        name="post_mix",
    )(mixed, x, mod, w_out_bf, ln_g, ln_b, mod, mod, router_wt, rb, tri)
    return x1, _unpack_route(oi, of, cnt)


DMA_UNROLL = 8
ROW_GROUPS = MOE_TM // DMA_UNROLL


def _for_rows(n, fn):
    def body_u(j, carry):
        for u in range(DMA_UNROLL):
            fn(j, u)
        return carry
    lax.fori_loop(0, n // DMA_UNROLL, body_u, 0)

    def body_1(r, carry):
        fn(r // DMA_UNROLL, r % DMA_UNROLL)
        return carry
    lax.fori_loop((n // DMA_UNROLL) * DMA_UNROLL, n, body_1, 0)


FF_CHUNK = 256


def _moe_sorted_kernel(ea_ref, eb_ref, nvalid_ref, pos_ref,
                       x_hbm, sh_ref, sc_ref, gate_ref, rw_ref,
                       wga_ref, wua_ref, wda_ref, wgb_ref, wub_ref, wdb_ref,
                       lng_ref, lnb_ref,
                       out_hbm,
                       inv, xbuf, obuf, wg_a, wu_a, wd_a, wg_b, wu_b, wd_b, gsem, ssem, fence):
    i = pl.program_id(0)
    slot = i % 2
    n_tok = pos_ref.shape[0]
    valid = nvalid_ref[i] > 0
    prev_valid = jnp.logical_and(i >= 1, nvalid_ref[jnp.maximum(i - 1, 0)] > 0)

    def gather_row(t, j, u, dst_slot):
        tok = jnp.minimum(inv[(t + 1) * MOE_TM + j * DMA_UNROLL + u], n_tok - 1)
        pltpu.make_async_copy(x_hbm.at[pl.ds(tok, 1)], xbuf.at[dst_slot, j, pl.ds(u, 1)], gsem).start()

    def scatter_row(t, j, u, src_slot):
        row = inv[(t + 1) * MOE_TM + j * DMA_UNROLL + u]
        pltpu.make_async_copy(obuf.at[src_slot, j, pl.ds(u, 1)], out_hbm.at[pl.ds(row, 1)], ssem).start()

    def wait_tile(buf, sem):
        pltpu.make_async_copy(buf, buf, sem).wait()

    @pl.when(i == 0)
    def _():
        def fill(j, carry):
            for u in range(DMA_UNROLL):
                inv[j * DMA_UNROLL + u] = n_tok + (j % ROW_GROUPS) * DMA_UNROLL + u
            return carry
        lax.fori_loop(0, inv.shape[0] // DMA_UNROLL, fill, 0)

        def body(j, carry):
            for u in range(DMA_UNROLL):
                t = j * DMA_UNROLL + u
                inv[MOE_TM + pos_ref[t]] = t
            return carry
        lax.fori_loop(0, n_tok // DMA_UNROLL, body, 0)
        obuf[...] = jnp.zeros_like(obuf)
        _for_rows(MOE_TM, lambda j, u: gather_row(0, j, u, 0))

    @pl.when(valid)
    def _():
        wait_tile(xbuf.at[slot], gsem)

        @pl.when(i >= 1)
        def _():
            wait_tile(obuf.at[slot], ssem)

        @pl.when(jnp.logical_or(i == 0, ea_ref[i] != ea_ref[jnp.maximum(i - 1, 0)]))
        def _():
            wg_a[...] = wga_ref[0, 0].astype(BF16)
            wu_a[...] = wua_ref[0, 0].astype(BF16)
            wd_a[...] = wda_ref[0, 0].astype(BF16)

        @pl.when(jnp.logical_or(i == 0, eb_ref[i] != eb_ref[jnp.maximum(i - 1, 0)]))
        def _():
            wg_b[...] = wgb_ref[0, 0].astype(BF16)
            wu_b[...] = wub_ref[0, 0].astype(BF16)
            wd_b[...] = wdb_ref[0, 0].astype(BF16)

        x = xbuf[slot].reshape(MOE_TM, D_MODEL)
        h2 = (x * (1.0 + sc_ref[0][:1]) + sh_ref[0][:1]).astype(BF16)

        aff = 1.0 / (1.0 + jnp.exp(-jnp.dot(h2, rw_ref[...], preferred_element_type=F32)))
        lane = lax.broadcasted_iota(I32, aff.shape, 1)
        a_lo = jnp.sum(jnp.where(lane == ea_ref[i], aff, 0.0), axis=-1, keepdims=True)
        a_hi = jnp.sum(jnp.where(lane == eb_ref[i], aff, 0.0), axis=-1, keepdims=True)
        a_sum = a_lo + a_hi

        n_chunks = D_FF // FF_CHUNK
        batch = MOE_TM // n_chunks
        ys = []
        for e, (wg, wu, wd) in enumerate(((wg_a, wu_a, wd_a), (wg_b, wu_b, wd_b))):
            y_e = None
            for c in range(n_chunks):
                cs = slice(c * FF_CHUNK, (c + 1) * FF_CHUNK)
                a = _silu(jnp.dot(h2, wg[:, cs], preferred_element_type=F32)) * \
                    jnp.dot(h2, wu[:, cs], preferred_element_type=F32)
                part = jnp.dot(a.astype(BF16), wd[cs, :], preferred_element_type=F32)
                y_e = part if y_e is None else y_e + part
                for r in range(c * batch, (c + 1) * batch):
                    if e == 0:
                        gather_row(i + 1, r // DMA_UNROLL, r % DMA_UNROLL, 1 - slot)
                    else:
                        scatter_row(i - 1, r // DMA_UNROLL, r % DMA_UNROLL, 1 - slot)
                pl.semaphore_signal(fence, 1)
                pl.semaphore_wait(fence, 1)
            ys.append(y_e)
        y = (a_lo / a_sum) * ys[0] + (a_hi / a_sum) * ys[1]
        out = _ln(DN_ALPHA * x + (1.0 + gate_ref[0][:1]) * y, lng_ref[...], lnb_ref[...])
        obuf[slot] = out.reshape(ROW_GROUPS, DMA_UNROLL, D_MODEL)

    @pl.when(jnp.logical_and(jnp.logical_not(valid), prev_valid))
    def _():
        wait_tile(xbuf.at[slot], gsem)
        wait_tile(obuf.at[slot], ssem)
        _for_rows(MOE_TM, lambda j, u: scatter_row(i - 1, j, u, 1 - slot))
        wait_tile(obuf.at[1 - slot], ssem)


def _moe_sorted(x1, mod, layer, mod_row0, route, rw_pad, wg, wu, wd, ln_g, ln_b):
    _, _, bin_id, rank, _, _, counts = route
    rows = x1.shape[0]
    nt = rows // MOE_TM + N_BINS
    ns = nt * MOE_TM
    tiles = (counts + MOE_TM - 1) // MOE_TM
    tile_end = jnp.cumsum(tiles)
    row_start = (tile_end - tiles) * MOE_TM
    pos = row_start[bin_id] + rank
    total = tile_end[-1]
    t = jnp.arange(nt, dtype=I32)
    tile_bin = jnp.searchsorted(tile_end, jnp.minimum(t, total - 1), side="right").astype(I32)
    tile_bin = jnp.clip(tile_bin, 0, N_BINS - 1)
    bin_lo = jnp.asarray([g * 4 + _PAIR_LO[p] for g in range(N_GROUPS) for p in range(N_PAIRS)], I32)
    bin_hi = jnp.asarray([g * 4 + _PAIR_HI[p] for g in range(N_GROUPS) for p in range(N_PAIRS)], I32)
    tile_ea, tile_eb = bin_lo[tile_bin], bin_hi[tile_bin]
    nvalid = jnp.clip(counts[tile_bin] - (t - (tile_end - tiles)[tile_bin]) * MOE_TM, 0, MOE_TM)
    nvalid = jnp.where(t < total, nvalid, 0).astype(I32)
    mrow = mod_row0 // 8
    mod_spec = lambda c: pl.BlockSpec((1, 8, D_MODEL), lambda i, *_: (layer, mrow, c))
    w_spec = lambda shape, ref_idx: pl.BlockSpec(
        (1, 1) + shape, lambda i, ea_r, eb_r, *_: (layer, (ea_r, eb_r)[ref_idx][i], 0, 0))
    any_spec = pl.BlockSpec(memory_space=pl.ANY)
    w_bf = lambda shape: pltpu.VMEM(shape, BF16)
    grid_spec = pltpu.PrefetchScalarGridSpec(
        num_scalar_prefetch=4,
        grid=(nt,),
        in_specs=[any_spec,
                  mod_spec(3), mod_spec(4), mod_spec(5),
                  pl.BlockSpec((D_MODEL, 128), lambda i, *_: (0, 0)),
                  w_spec((D_MODEL, D_FF), 0), w_spec((D_MODEL, D_FF), 0), w_spec((D_FF, D_MODEL), 0),
                  w_spec((D_MODEL, D_FF), 1), w_spec((D_MODEL, D_FF), 1), w_spec((D_FF, D_MODEL), 1),
                  pl.BlockSpec((1, D_MODEL), lambda i, *_: (0, 0)),
                  pl.BlockSpec((1, D_MODEL), lambda i, *_: (0, 0))],
        out_specs=any_spec,
        scratch_shapes=[pltpu.SMEM((MOE_TM + ns,), I32),
                        pltpu.VMEM((2, ROW_GROUPS, DMA_UNROLL, D_MODEL), F32),
                        pltpu.VMEM((2, ROW_GROUPS, DMA_UNROLL, D_MODEL), F32),
                        w_bf((D_MODEL, D_FF)), w_bf((D_MODEL, D_FF)), w_bf((D_FF, D_MODEL)),
                        w_bf((D_MODEL, D_FF)), w_bf((D_MODEL, D_FF)), w_bf((D_FF, D_MODEL)),
                        pltpu.SemaphoreType.DMA(()),
                        pltpu.SemaphoreType.DMA(()),
                        pltpu.SemaphoreType.REGULAR(())],
    )
    assert (rows + N_BINS * (MOE_TM - 1)) // MOE_TM < nt
    return pl.pallas_call(
        _moe_sorted_kernel,
        grid_spec=grid_spec,
        out_shape=jax.ShapeDtypeStruct((rows + MOE_TM, D_MODEL), F32),
        compiler_params=_cparams(("arbitrary",)),
        name="moe_sorted",
    )(tile_ea, tile_eb, nvalid, pos, x1, mod, mod, mod, rw_pad,
      wg, wu, wd, wg, wu, wd, ln_g, ln_b)


def _moe_dense_kernel(x_ref, dw_ref, sh_ref, sc_ref, gate_ref, wg_ref, wu_ref, wd_ref,
                      lng_ref, lnb_ref, o_ref, acc):
    e = pl.program_id(0)

    @pl.when(e == 0)
    def _():
        acc[...] = jnp.zeros_like(acc)

    x = x_ref[...]
    h2 = (x * (1.0 + sc_ref[0]) + sh_ref[0]).astype(BF16)
    a = _silu(jnp.dot(h2, wg_ref[0, 0].astype(BF16), preferred_element_type=F32)) * \
        jnp.dot(h2, wu_ref[0, 0].astype(BF16), preferred_element_type=F32)
    y = jnp.dot(a.astype(BF16), wd_ref[0, 0].astype(BF16), preferred_element_type=F32)
    acc[...] = acc[...] + dw_ref[0][:, :1] * y

    @pl.when(e == pl.num_programs(0) - 1)
    def _():
        o_ref[...] = _ln(DN_ALPHA * x + (1.0 + gate_ref[0]) * acc[...], lng_ref[...], lnb_ref[...])


def _moe_dense(x1, mod, layer, route, wg, wu, wd, ln_g, ln_b):
    ea, eb, _, _, w_lo, w_hi, _ = route
    nb = x1.shape[0]
    eids = jnp.arange(N_EXPERTS, dtype=I32)[:, None]
    dw = jnp.where(eids == ea[None, :], w_lo[None, :], 0.0) + \
        jnp.where(eids == eb[None, :], w_hi[None, :], 0.0)
    dw = jnp.broadcast_to(dw[:, :, None], (N_EXPERTS, nb, 128))
    mod_spec = lambda c: pl.BlockSpec((1, nb, D_MODEL), lambda e: (layer, 0, c))
    const2 = lambda a: pl.BlockSpec(a.shape, lambda e: (0, 0))
    return pl.pallas_call(
        _moe_dense_kernel,
        grid=(N_EXPERTS,),
        in_specs=[const2(x1), pl.BlockSpec((1, nb, 128), lambda e: (e, 0, 0)),
                  mod_spec(3), mod_spec(4), mod_spec(5),
                  pl.BlockSpec((1, 1, D_MODEL, D_FF), lambda e: (layer, e, 0, 0)),
                  pl.BlockSpec((1, 1, D_MODEL, D_FF), lambda e: (layer, e, 0, 0)),
                  pl.BlockSpec((1, 1, D_FF, D_MODEL), lambda e: (layer, e, 0, 0)),
                  const2(ln_g), const2(ln_b)],
        out_specs=const2(x1),
        out_shape=jax.ShapeDtypeStruct(x1.shape, F32),
        scratch_shapes=[pltpu.VMEM(x1.shape, F32)],
        compiler_params=_cparams(("arbitrary",)),
        name="moe_dense",
    )(x1, dw, mod, mod, mod, wg, wu, wd, ln_g, ln_b)


def kernel(x_prompt, x_sample, state_ret, cache_swa_k, cache_swa_v, c_prompt, c_sample, w_in, w_out, ret_gn_gain, swa_sinks, rel_bias_table, ada_w, ada_b, ln1_g, ln1_b, ln2_g, ln2_b, router_w, router_b, exp_w_gate, exp_w_up, exp_w_down):
    seq = x_prompt.shape[1]
    nb = x_sample.shape[0]
    past_len = 16384
    assert x_prompt.shape[0] == 1 and x_sample.shape[1] == 1

    c_all = jnp.concatenate([c_sample, c_prompt, jnp.zeros((7, D_MODEL), F32)], axis=0)
    mod = _ada(c_all, ada_w, ada_b)
    prompt_row = nb

    bias_tabs = _bias_tables(rel_bias_table.astype(F32))
    gl, decay, xi, zeta, gamma1 = _ret_consts()
    rope_p = _rope_tables(jnp.arange(seq, dtype=I32))
    rope_s = _rope_tables(jnp.full((nb,), past_len, I32))
    router_wt = router_w.astype(F32).T
    rw_pad = jnp.pad(router_w.astype(BF16), ((0, 0), (0, 128 - N_EXPERTS)))
    zeta_p = jnp.tile(zeta, (INPROJ_TM // CHUNK, 1))
    zeta_s = jnp.ones((nb, RET_W), F32)

    xp = x_prompt.reshape(seq, D_MODEL)
    xs = x_sample.reshape(nb, D_MODEL)
    st_p, k_p, v_p, st_s, k_s, v_s = [], [], [], [], [], []
    for l in range(DEPTH):
        w_in_bf = w_in[l].astype(BF16)
        w_out_bf = w_out[l].astype(BF16)
        experts = (exp_w_gate.astype(F32), exp_w_up.astype(F32), exp_w_down.astype(F32))
        gain = ret_gn_gain[l].astype(F32)
        sinks = swa_sinks[l].astype(F32)
        row = lambda a: a[l].astype(F32).reshape(1, D_MODEL)

        proj = _inproj(xp, seq, mod, l, prompt_row, 1, w_in_bf, rope_p, zeta_p, INPROJ_TM)
        mixed, st = _mix_prompt(proj, (gl, decay, xi), bias_tabs, sinks, gain.reshape(1, RET_W))
        skv = proj[6]
        st_p.append(st.reshape(1, N_RET_HEADS, HEAD_DIM, HEAD_DIM))
        k_p.append(skv[seq - CHUNK:, :SWA_KVW].reshape(1, CHUNK, N_KV_HEADS, HEAD_DIM))
        v_p.append(skv[seq - CHUNK:, SWA_KVW:].reshape(1, CHUNK, N_KV_HEADS, HEAD_DIM))
        x1, route = _post_mix(mixed, xp, mod, l, prompt_row, w_out_bf, row(ln1_g), row(ln1_b),
                              router_wt, router_b)
        xp = _moe_sorted(x1, mod, l, prompt_row, route, rw_pad, *experts, row(ln2_g), row(ln2_b))

        proj = _inproj(xs, nb, mod, l, 0, nb, w_in_bf, rope_s, zeta_s, nb)
        x1, nst, nk, nv = _mix_sample(proj, xs, mod, l, w_out_bf, gamma1, bias_tabs[1], sinks, gain,
                                      row(ln1_g), row(ln1_b), state_ret[l].astype(F32),
                                      cache_swa_k[l], cache_swa_v[l])
        st_s.append(nst)
        k_s.append(nk)
        v_s.append(nv)
        route = _router(x1, mod, l, 0, nb, router_wt, router_b, nb)
        xs = _moe_dense(x1, mod, l, route, *experts, row(ln2_g), row(ln2_b))

    return (xp[:seq].reshape(1, seq, D_MODEL), xs.reshape(nb, 1, D_MODEL),
            jnp.stack(st_p), jnp.stack(k_p), jnp.stack(v_p),
            jnp.stack(st_s), jnp.stack(k_s), jnp.stack(v_s))
```

```python
import functools
import math

import numpy as np
import jax
import jax.numpy as jnp
from jax import lax
from jax.experimental import pallas as pl
from jax.experimental.pallas import tpu as pltpu

F32 = jnp.float32
BF16 = jnp.bfloat16
I32 = jnp.int32

D_MODEL = 1024
DEPTH = 2
HEAD_DIM = 64
N_RET_HEADS = 8
N_SWA_HEADS = 8
N_KV_HEADS = 2
GQA_GROUP = N_SWA_HEADS // N_KV_HEADS
RET_W = N_RET_HEADS * HEAD_DIM
SWA_QW = N_SWA_HEADS * HEAD_DIM
SWA_KVW = N_KV_HEADS * HEAD_DIM
PROJ_W = 4 * RET_W + SWA_QW + 2 * SWA_KVW
CHUNK = 128
ROPE_BASE = 10000.0
N_BUCKETS = 32
MAX_DISTANCE = 128
N_EXPERTS = 16
N_GROUPS = 4
EXPERTS_PER_GROUP = 4
D_FF = 512
LN_EPS = 1e-5
DN_ALPHA = (2 * DEPTH) ** 0.25
QK_SCALE = HEAD_DIM ** -0.5
NEG_BIG = -1e30

N_PAIRS = 6
N_BINS = N_GROUPS * N_PAIRS
BIN_ROWS = 32
MOE_TM = 256
INPROJ_TM = 512
MIX_SUB = 2
VMEM_LIMIT = 56 * 1024 * 1024

_PAIR_LO = (0, 0, 0, 1, 1, 2)
_PAIR_HI = (1, 2, 3, 2, 3, 3)


def _cparams(sem):
    return pltpu.CompilerParams(dimension_semantics=sem, vmem_limit_bytes=VMEM_LIMIT)


def _ln(v, g, b):
    mu = jnp.mean(v, axis=-1, keepdims=True)
    c = v - mu
    var = jnp.mean(c * c, axis=-1, keepdims=True)
    return c * lax.rsqrt(var + LN_EPS) * g + b


def _silu(v):
    return v * (1.0 / (1.0 + jnp.exp(-v)))


def _ada_kernel(c_ref, w_ref, b_ref, o_ref):
    o_ref[0] = jnp.dot(c_ref[...].astype(BF16), w_ref[0].astype(BF16),
                       preferred_element_type=F32) + b_ref[0]


def _ada(c_all, ada_w, ada_b):
    rows = c_all.shape[0]
    nt = 6 * D_MODEL // 1024
    return pl.pallas_call(
        _ada_kernel,
        grid=(DEPTH, nt),
        in_specs=[pl.BlockSpec((rows, D_MODEL), lambda l, j: (0, 0)),
                  pl.BlockSpec((1, D_MODEL, 1024), lambda l, j: (l, 0, j)),
                  pl.BlockSpec((1, 1, 1024), lambda l, j: (l, 0, j))],
        out_specs=pl.BlockSpec((1, rows, 1024), lambda l, j: (l, 0, j)),
        out_shape=jax.ShapeDtypeStruct((DEPTH, rows, 6 * D_MODEL), F32),
        compiler_params=_cparams(("arbitrary", "arbitrary")),
        name="ada",
    )(c_all, ada_w, ada_b.reshape(DEPTH, 1, 6 * D_MODEL))


def _bias_kernel(tab_ref, bkt_ref, fold_ref, row_ref):
    bkt = bkt_ref[...]
    rows = lax.broadcasted_iota(I32, (CHUNK, CHUNK), 0)
    cols = lax.broadcasted_iota(I32, (CHUNK, CHUNK), 1)
    from_prev = cols > rows
    for h in range(N_SWA_HEADS):
        acc = jnp.zeros(bkt.shape, F32)
        for b in range(N_BUCKETS):
            acc = jnp.where(bkt == b, tab_ref[b, h], acc)
        own = acc[:, CHUNK:]
        fold_ref[0, h] = jnp.where(from_prev, NEG_BIG, own)
        fold_ref[1, h] = jnp.where(from_prev, acc[:, :CHUNK], own)
        row_ref[h:h + 1, :] = own[CHUNK - 1:CHUNK, :]


def _t5_bucket(rel):
    max_exact = N_BUCKETS // 2
    relf = jnp.maximum(rel, 1).astype(F32)
    large = max_exact + (jnp.log(relf / max_exact) / math.log(MAX_DISTANCE / max_exact)
                         * (N_BUCKETS - max_exact)).astype(I32)
    large = jnp.minimum(large, N_BUCKETS - 1)
    return jnp.where(rel < max_exact, rel, large)


def _bias_tables(rel_bias_table):
    qi = jnp.arange(CHUNK)
    si = jnp.arange(2 * CHUNK)
    rel = CHUNK + qi[:, None] - si[None, :]
    bkt = _t5_bucket(jnp.maximum(rel, 0)).astype(I32)
    return pl.pallas_call(
        _bias_kernel,
        in_specs=[pl.BlockSpec(memory_space=pltpu.SMEM),
                  pl.BlockSpec((CHUNK, 2 * CHUNK), lambda: (0, 0))],
        out_specs=[pl.BlockSpec((2, N_SWA_HEADS, CHUNK, CHUNK), lambda: (0, 0, 0, 0)),
                   pl.BlockSpec((N_SWA_HEADS, CHUNK), lambda: (0, 0))],
        out_shape=[jax.ShapeDtypeStruct((2, N_SWA_HEADS, CHUNK, CHUNK), F32),
                   jax.ShapeDtypeStruct((N_SWA_HEADS, CHUNK), F32)],
        name="t5_bias",
    )(rel_bias_table, bkt)


def _rotary(v, cos, s_lo, s_hi):
    outs = []
    for j in range(RET_W // 128):
        blk = v[:, j * 128:(j + 1) * 128]
        outs.append(blk * cos + pltpu.roll(blk, 96, 1) * s_lo + pltpu.roll(blk, 32, 1) * s_hi)
    return jnp.concatenate(outs, axis=-1)


def _inproj_kernel(x_ref, sh_ref, sc_ref, w_ref, cs_ref, zeta_ref,
                   q_ref, k_ref, kz_ref, v_ref, g_ref, sq_ref, skv_ref, *, mod_rows):
    sh = sh_ref[0][:mod_rows]
    sc = sc_ref[0][:mod_rows]
    h = (x_ref[...] * (1.0 + sc) + sh).astype(BF16)
    cs = cs_ref[...]
    first = (lax.broadcasted_iota(I32, cs.shape, 1) % HEAD_DIM) < (HEAD_DIM // 2)
    cos = jnp.where(first, cs, pltpu.roll(cs, 32, 1))
    s_lo = jnp.where(first, -pltpu.roll(cs, 96, 1), 0.0)
    s_hi = jnp.where(first, 0.0, cs)

    def proj(lo, hi):
        return jnp.dot(h, w_ref[:, lo:hi], preferred_element_type=F32)

    q_ref[...] = _rotary(proj(0, RET_W), cos, s_lo, s_hi).astype(BF16)
    k = _rotary(proj(RET_W, 2 * RET_W), cos, s_lo, s_hi) * QK_SCALE
    k_ref[...] = k.astype(BF16)
    kz_ref[...] = (k * zeta_ref[...]).astype(BF16)
    v_ref[...] = proj(2 * RET_W, 3 * RET_W).astype(BF16)
    g_ref[...] = proj(3 * RET_W, 4 * RET_W)
    sq_ref[...] = (proj(4 * RET_W, 4 * RET_W + SWA_QW) * QK_SCALE).astype(BF16)
    skv_ref[...] = proj(4 * RET_W + SWA_QW, PROJ_W)


def _inproj(x, mod, layer, mod_row0, mod_rows, w_in_bf, rope, zeta_tile, tm):
    rows = x.shape[0]
    mblk = 8 if mod_rows == 1 else mod_rows
    mrow = mod_row0 // mblk
    row_spec = lambda w: pl.BlockSpec((tm, w), lambda i: (i, 0))
    mod_spec = lambda c: pl.BlockSpec((1, mblk, D_MODEL), lambda i: (layer, mrow, c))
    return pl.pallas_call(
        functools.partial(_inproj_kernel, mod_rows=mod_rows),
        grid=(rows // tm,),
        in_specs=[row_spec(D_MODEL), mod_spec(0), mod_spec(1),
                  pl.BlockSpec((D_MODEL, PROJ_W), lambda i: (0, 0)),
                  row_spec(128),
                  pl.BlockSpec((tm, RET_W), lambda i: (0, 0))],
        out_specs=[row_spec(RET_W), row_spec(RET_W), row_spec(RET_W), row_spec(RET_W), row_spec(RET_W),
                   row_spec(SWA_QW), row_spec(2 * SWA_KVW)],
        out_shape=[jax.ShapeDtypeStruct((rows, RET_W), BF16),
                   jax.ShapeDtypeStruct((rows, RET_W), BF16),
                   jax.ShapeDtypeStruct((rows, RET_W), BF16),
                   jax.ShapeDtypeStruct((rows, RET_W), BF16),
                   jax.ShapeDtypeStruct((rows, RET_W), F32),
                   jax.ShapeDtypeStruct((rows, SWA_QW), BF16),
                   jax.ShapeDtypeStruct((rows, 2 * SWA_KVW), F32)],
        compiler_params=_cparams(("arbitrary",)),
        name="inproj",
    )(x, mod, mod, w_in_bf, rope, zeta_tile)


def _rope_tables(pos):
    half = HEAD_DIM // 2
    inv = ROPE_BASE ** (-jnp.arange(half, dtype=F32) / half)
    ang = pos.astype(F32)[:, None] * inv[None, :]
    cos, sin = jnp.cos(ang), jnp.sin(ang)
    return jnp.concatenate([cos, sin, cos, sin], axis=-1)


def _mix_prompt_kernel(gl_ref, sink_ref,
                       q_ref, k_ref, kz_ref, v_ref, g_ref, sq_ref, kvc_ref, kvp_ref,
                       decay_ref, xi_ref, bias_ref, gain_ref,
                       mixed_ref, st_ref, state):
    i = pl.program_id(0)

    @pl.when(i == 0)
    def _():
        state[...] = jnp.zeros_like(state)

    xi = xi_ref[...]
    gain = gain_ref[...]
    from_prev = (lax.broadcasted_iota(I32, (CHUNK, CHUNK), 1) >
                 lax.broadcasted_iota(I32, (CHUNK, CHUNK), 0))

    for sub in range(MIX_SUB):
        rows = slice(sub * CHUNK, (sub + 1) * CHUNK)
        q = q_ref[rows, :]
        k = k_ref[rows, :]
        v = v_ref[rows, :]
        kz = kz_ref[rows, :]

        intra, inter = [], []
        for h in range(N_RET_HEADS):
            sl = slice(h * HEAD_DIM, (h + 1) * HEAD_DIM)
            qh, kh, vh = q[:, sl], k[:, sl], v[:, sl]
            s = lax.dot_general(qh, kh, (((1,), (1,)), ((), ())), preferred_element_type=F32)
            p = (s * decay_ref[h]).astype(BF16)
            intra.append(jnp.dot(p, vh, preferred_element_type=F32))
            s_old = state[h]
            inter.append(jnp.dot(qh, s_old.astype(BF16), preferred_element_type=F32))
            upd = lax.dot_general(kz[:, sl], vh, (((0,), (0,)), ((), ())), preferred_element_type=F32)
            state[h] = gl_ref[h] * s_old + upd

        ret_parts = []
        for h in range(N_RET_HEADS):
            sl = slice(h * HEAD_DIM, (h + 1) * HEAD_DIM)
            o = intra[h] + inter[h] * xi[:, sl]
            mu = jnp.mean(o, axis=-1, keepdims=True)
            c = o - mu
            var = jnp.mean(c * c, axis=-1, keepdims=True)
            ret_parts.append(c * lax.rsqrt(var + LN_EPS))
        ret_o = jnp.concatenate(ret_parts, axis=-1) * gain * _silu(g_ref[rows, :])

        sq = sq_ref[rows, :]
        kvc = kvc_ref[rows, :]
        kvp = kvp_ref[...] if sub == 0 else kvc_ref[(sub - 1) * CHUNK:sub * CHUNK, :]
        table = jnp.minimum(i, 1) if sub == 0 else 1
        swa_parts = []
        for kv in range(N_KV_HEADS):
            ks = slice(kv * HEAD_DIM, (kv + 1) * HEAD_DIM)
            vs = slice(SWA_KVW + kv * HEAD_DIM, SWA_KVW + (kv + 1) * HEAD_DIM)
            kk = jnp.concatenate([kvp[:, ks], kvc[:, ks]], axis=0).astype(BF16)
            vv = jnp.concatenate([kvp[:, vs], kvc[:, vs]], axis=0).astype(BF16)
            for j in range(GQA_GROUP):
                hh = kv * GQA_GROUP + j
                qh = sq[:, hh * HEAD_DIM:(hh + 1) * HEAD_DIM]
                s2 = lax.dot_general(qh, kk, (((1,), (1,)), ((), ())), preferred_element_type=F32)
                s = jnp.where(from_prev, s2[:, :CHUNK], s2[:, CHUNK:]) + bias_ref[table, hh]
                sink = sink_ref[hh]
                m = jnp.maximum(jnp.max(s, axis=-1, keepdims=True), sink)
                p = jnp.exp(s - m)
                den = jnp.sum(p, axis=-1, keepdims=True) + jnp.exp(sink - m)
                pn = p / den
                p2 = jnp.concatenate([jnp.where(from_prev, pn, 0.0), jnp.where(from_prev, 0.0, pn)], axis=-1)
                swa_parts.append(jnp.dot(p2.astype(BF16), vv, preferred_element_type=F32))
        swa_o = jnp.concatenate(swa_parts, axis=-1)

        mixed_ref[rows, :] = jnp.concatenate([ret_o, swa_o], axis=-1).astype(BF16)

    @pl.when(i == pl.num_programs(0) - 1)
    def _():
        st_ref[...] = state[...]


def _mix_prompt(proj, ret_consts, bias_tabs, sinks, gain):
    q, k, kz, v, g, sq, skv = proj
    rows = q.shape[0]
    tm = MIX_SUB * CHUNK
    gl, decay, xi = ret_consts
    bias_fold = bias_tabs[0]
    row_spec = lambda w: pl.BlockSpec((tm, w), lambda i: (i, 0))
    const2 = lambda a: pl.BlockSpec(a.shape, lambda i: (0, 0))
    const3 = lambda a: pl.BlockSpec(a.shape, lambda i: (0, 0, 0))
    smem = pl.BlockSpec(memory_space=pltpu.SMEM)
    return pl.pallas_call(
        _mix_prompt_kernel,
        grid=(rows // tm,),
        in_specs=[smem, smem,
                  row_spec(RET_W), row_spec(RET_W), row_spec(RET_W), row_spec(RET_W), row_spec(RET_W),
                  row_spec(SWA_QW), row_spec(2 * SWA_KVW),
                  pl.BlockSpec((CHUNK, 2 * SWA_KVW), lambda i: (jnp.maximum(MIX_SUB * i - 1, 0), 0)),
                  const3(decay), const2(xi),
                  pl.BlockSpec(bias_fold.shape, lambda i: (0, 0, 0, 0)),
                  const2(gain)],
        out_specs=[row_spec(D_MODEL),
                   pl.BlockSpec((N_RET_HEADS, HEAD_DIM, HEAD_DIM), lambda i: (0, 0, 0))],
        out_shape=[jax.ShapeDtypeStruct((rows, D_MODEL), BF16),
                   jax.ShapeDtypeStruct((N_RET_HEADS, HEAD_DIM, HEAD_DIM), F32)],
        scratch_shapes=[pltpu.VMEM((N_RET_HEADS, HEAD_DIM, HEAD_DIM), F32)],
        compiler_params=_cparams(("arbitrary",)),
        name="mix_prompt",
    )(gl, sinks, q, k, kz, v, g, sq, skv, skv, decay, xi, bias_fold, gain)


def _ret_consts():
    lg = jnp.log(1.0 - 2.0 ** (-5.0 - jnp.arange(N_RET_HEADS, dtype=F32)))
    idx = jnp.arange(CHUNK, dtype=F32)
    diff = idx[:, None] - idx[None, :]
    decay = jnp.where(diff >= 0, jnp.exp(jnp.maximum(diff, 0.0)[None] * lg[:, None, None]), 0.0)
    xi = jnp.exp((idx + 1.0)[:, None] * lg[None, :])
    zeta = jnp.exp((CHUNK - 1.0 - idx)[:, None] * lg[None, :])
    gl = jnp.exp(CHUNK * lg)
    expand = lambda a: jnp.repeat(a, HEAD_DIM, axis=1)
    return gl, decay, expand(xi), expand(zeta), jnp.exp(1.0 * lg)


SAMPLE_BB = 8


def _mix_sample_kernel(g1_ref, sink_ref,
                       qt_ref, kt_ref, v3_ref, sq3_ref, knew_ref, vnew_ref,
                       st_ref, ck_ref, cv_ref, g3_ref, x_ref, gate_ref,
                       wout_ref, brow_ref, gain_ref, lng_ref, lnb_ref,
                       x1_ref, nst_ref, nk_ref, nv_ref, ret_scr, swa_scr):
    i = pl.program_id(0)
    row = lax.broadcasted_iota(I32, (N_SWA_HEADS, 2 * HEAD_DIM), 0)
    lane = lax.broadcasted_iota(I32, (N_SWA_HEADS, 2 * HEAD_DIM), 1)
    own_half = (row // GQA_GROUP) == (lane // HEAD_DIM)
    sink_col = jnp.concatenate(
        [jnp.full((1, 1), sink_ref[hh], F32) for hh in range(N_SWA_HEADS)], axis=0)
    brow = brow_ref[...]
    qt = qt_ref[0]
    kt = kt_ref[0]

    for b in range(SAMPLE_BB):
        o_rows = []
        for h in range(N_RET_HEADS):
            rs = slice(h * HEAD_DIM, (h + 1) * HEAD_DIM)
            s_old = st_ref[b, rs, :]
            kcol = kt[rs, b:b + 1]
            qcol = qt[rs, b:b + 1]
            vrow = v3_ref[b, h:h + 1, :]
            s_new = g1_ref[h] * s_old + kcol * vrow
            nst_ref[b, rs, :] = s_new
            o_rows.append(jnp.sum(qcol * s_new, axis=0, keepdims=True))
        ret_scr[i * SAMPLE_BB + b] = jnp.concatenate(o_rows, axis=0)

        kk = jnp.concatenate([ck_ref[b, 1:, :], knew_ref[b:b + 1, :]], axis=0)
        vv = jnp.concatenate([cv_ref[b, 1:, :], vnew_ref[b:b + 1, :]], axis=0)
        nk_ref[b] = kk
        nv_ref[b] = vv
        q8 = sq3_ref[b]
        qblk = jnp.where(own_half, jnp.concatenate([q8, q8], axis=-1), 0.0).astype(BF16)
        s = lax.dot_general(qblk, kk.astype(BF16), (((1,), (1,)), ((), ())),
                            preferred_element_type=F32) + brow
        m = jnp.maximum(jnp.max(s, axis=-1, keepdims=True), sink_col)
        p = jnp.exp(s - m)
        den = jnp.sum(p, axis=-1, keepdims=True) + jnp.exp(sink_col - m)
        o = jnp.dot((p / den).astype(BF16), vv.astype(BF16), preferred_element_type=F32)
        swa_scr[i * SAMPLE_BB + b] = jnp.where(own_half[:, :HEAD_DIM], o[:, :HEAD_DIM], o[:, HEAD_DIM:])

    @pl.when(i == pl.num_programs(0) - 1)
    def _():
        y = jnp.zeros(x_ref.shape, F32)
        for h in range(N_RET_HEADS):
            o = ret_scr[:, h, :]
            mu = jnp.mean(o, axis=-1, keepdims=True)
            c = o - mu
            var = jnp.mean(c * c, axis=-1, keepdims=True)
            r = c * lax.rsqrt(var + LN_EPS) * gain_ref[h:h + 1, :] * _silu(g3_ref[h])
            y = y + jnp.dot(r.astype(BF16), wout_ref[h * HEAD_DIM:(h + 1) * HEAD_DIM, :],
                            preferred_element_type=F32)
        for hh in range(N_SWA_HEADS):
            o = swa_scr[:, hh, :].astype(BF16)
            lo = RET_W + hh * HEAD_DIM
            y = y + jnp.dot(o, wout_ref[lo:lo + HEAD_DIM, :], preferred_element_type=F32)
        x1_ref[...] = _ln(DN_ALPHA * x_ref[...] + (1.0 + gate_ref[0]) * y, lng_ref[...], lnb_ref[...])


def _mix_sample(proj, x, mod, layer, w_out_bf, gamma1, bias_row, sinks, gain8, ln_g, ln_b,
                st, ck, cv):
    q, k, _, v, g, sq, skv = proj
    nb = x.shape[0]
    steps = nb // SAMPLE_BB
    to_cols = lambda a: a.astype(F32).reshape(steps, SAMPLE_BB, RET_W).transpose(0, 2, 1)
    qt, kt = to_cols(q), to_cols(k)
    v3 = v.astype(F32).reshape(nb, N_RET_HEADS, HEAD_DIM)
    sq3 = sq.astype(F32).reshape(nb, N_SWA_HEADS, HEAD_DIM)
    g3 = g.reshape(nb, N_RET_HEADS, HEAD_DIM).transpose(1, 0, 2)
    knew, vnew = skv[:, :SWA_KVW], skv[:, SWA_KVW:]
    st2 = st.reshape(nb, RET_W, HEAD_DIM)
    ck2 = ck.reshape(nb, CHUNK, SWA_KVW)
    cv2 = cv.reshape(nb, CHUNK, SWA_KVW)
    smem = pl.BlockSpec(memory_space=pltpu.SMEM)
    blk3 = lambda a, b, c: pl.BlockSpec((a, b, c), lambda i: (i, 0, 0))
    const2 = lambda a: pl.BlockSpec(a.shape, lambda i: (0, 0))
    const3 = lambda a: pl.BlockSpec(a.shape, lambda i: (0, 0, 0))
    x1, nst, nk, nv = pl.pallas_call(
        _mix_sample_kernel,
        grid=(steps,),
        in_specs=[smem, smem,
                  blk3(1, RET_W, SAMPLE_BB), blk3(1, RET_W, SAMPLE_BB),
                  blk3(SAMPLE_BB, N_RET_HEADS, HEAD_DIM), blk3(SAMPLE_BB, N_SWA_HEADS, HEAD_DIM),
                  pl.BlockSpec((SAMPLE_BB, SWA_KVW), lambda i: (i, 0)),
                  pl.BlockSpec((SAMPLE_BB, SWA_KVW), lambda i: (i, 0)),
                  blk3(SAMPLE_BB, RET_W, HEAD_DIM), blk3(SAMPLE_BB, CHUNK, SWA_KVW),
                  blk3(SAMPLE_BB, CHUNK, SWA_KVW),
                  const3(g3), const2(x),
                  pl.BlockSpec((1, nb, D_MODEL), lambda i: (layer, 0, 2)),
                  const2(w_out_bf), const2(bias_row), const2(gain8), const2(ln_g), const2(ln_b)],
        out_specs=[const2(x), blk3(SAMPLE_BB, RET_W, HEAD_DIM), blk3(SAMPLE_BB, CHUNK, SWA_KVW),
                   blk3(SAMPLE_BB, CHUNK, SWA_KVW)],
        out_shape=[jax.ShapeDtypeStruct(x.shape, F32), jax.ShapeDtypeStruct(st2.shape, F32),
                   jax.ShapeDtypeStruct(ck2.shape, F32), jax.ShapeDtypeStruct(cv2.shape, F32)],
        scratch_shapes=[pltpu.VMEM((nb, N_RET_HEADS, HEAD_DIM), F32),
                        pltpu.VMEM((nb, N_SWA_HEADS, HEAD_DIM), F32)],
        compiler_params=_cparams(("arbitrary",)),
        name="mix_sample",
    )(gamma1, sinks, qt, kt, v3, sq3, knew, vnew, st2, ck2, cv2, g3, x, mod,
      w_out_bf, bias_row, gain8, ln_g, ln_b)
    return x1, nst.reshape(st.shape), nk.reshape(ck.shape), nv.reshape(cv.shape)


def _router_kernel(x_ref, sh_ref, sc_ref, rwt_ref, rb_ref, tri_ref,
                   oi_ref, of_ref, cnt_ref, run_ref, *, mod_rows):
    h2 = x_ref[...] * (1.0 + sc_ref[0][:mod_rows]) + sh_ref[0][:mod_rows]
    _route_rows(h2, rwt_ref, rb_ref, tri_ref, oi_ref, of_ref, cnt_ref, run_ref)


def _post_mix_kernel(mixed_ref, x_ref, gate_ref, wout_ref, lng_ref, lnb_ref,
                     sh_ref, sc_ref, rwt_ref, rb_ref, tri_ref,
                     x1_ref, oi_ref, of_ref, cnt_ref, run_ref):
    y = jnp.dot(mixed_ref[...], wout_ref[...], preferred_element_type=F32)
    x1 = _ln(DN_ALPHA * x_ref[...] + (1.0 + gate_ref[0][:1]) * y, lng_ref[...], lnb_ref[...])
    x1_ref[...] = x1
    h2 = x1 * (1.0 + sc_ref[0][:1]) + sh_ref[0][:1]
    _route_rows(h2, rwt_ref, rb_ref, tri_ref, oi_ref, of_ref, cnt_ref, run_ref)


def _route_rows(h2, rwt_ref, rb_ref, tri_ref, oi_ref, of_ref, cnt_ref, run_ref):
    i = pl.program_id(0)

    @pl.when(i == 0)
    def _():
        run_ref[...] = jnp.zeros_like(run_ref)

    logits = lax.dot_general(rwt_ref[...].astype(BF16), h2.astype(BF16), (((1,), (1,)), ((), ())),
                             preferred_element_type=F32)
    aff = 1.0 / (1.0 + jnp.exp(-logits))
    sel = aff + rb_ref[...]
    s = [sel[e:e + 1, :] for e in range(N_EXPERTS)]
    a = [aff[e:e + 1, :] for e in range(N_EXPERTS)]

    def top2sum(v0, v1, v2, v3):
        hi01, lo01 = jnp.maximum(v0, v1), jnp.minimum(v0, v1)
        hi23, lo23 = jnp.maximum(v2, v3), jnp.minimum(v2, v3)
        return jnp.maximum(hi01, hi23) + jnp.maximum(jnp.minimum(hi01, hi23),
                                                     jnp.maximum(lo01, lo23))

    def argmax_first(vals):
        best, idx = vals[0], jnp.zeros(vals[0].shape, I32)
        for j in range(1, len(vals)):
            upd = vals[j] > best
            idx = jnp.where(upd, j, idx)
            best = jnp.where(upd, vals[j], best)
        return idx

    def pick(idx, vals):
        out = vals[-1]
        for j in range(len(vals) - 2, -1, -1):
            out = jnp.where(idx == j, vals[j], out)
        return out

    gi = argmax_first([top2sum(*s[4 * g:4 * g + 4]) for g in range(N_GROUPS)])
    sv = [pick(gi, [s[4 * g + j] for g in range(N_GROUPS)]) for j in range(EXPERTS_PER_GROUP)]
    av = [pick(gi, [a[4 * g + j] for g in range(N_GROUPS)]) for j in range(EXPERTS_PER_GROUP)]
    i1 = argmax_first(sv)
    i2 = argmax_first([jnp.where(i1 == j, -jnp.inf, sv[j]) for j in range(EXPERTS_PER_GROUP)])
    w1, w2 = pick(i1, av), pick(i2, av)
    wsum = w1 + w2
    w1, w2 = w1 / wsum, w2 / wsum
    lo, hi = jnp.minimum(i1, i2), jnp.maximum(i1, i2)
    w_lo = jnp.where(i1 < i2, w1, w2)
    w_hi = jnp.where(i1 < i2, w2, w1)
    pair = jnp.where(lo == 0, hi - 1, jnp.where(lo == 1, hi + 1, 5))
    bin_id = gi * N_PAIRS + pair

    tm = bin_id.shape[1]
    onehot = lax.broadcasted_iota(I32, (BIN_ROWS, tm), 0) == bin_id
    oh_f = jnp.where(onehot, 1.0, 0.0)
    before = jnp.dot(oh_f.astype(BF16), tri_ref[...], preferred_element_type=F32)
    run = run_ref[...]
    run_t = jnp.concatenate([run] * (tm // 128), axis=-1)
    rank = jnp.sum(oh_f * (before + run_t), axis=0, keepdims=True)
    run_new = run + jnp.sum(oh_f, axis=1, keepdims=True)
    run_ref[...] = run_new
    cnt_ref[...] = run_new.astype(I32)

    zi = jnp.zeros_like(bin_id)
    oi_ref[0] = jnp.concatenate([gi * 4 + lo, gi * 4 + hi, bin_id, rank.astype(I32), zi, zi, zi, zi], axis=0)
    zf = jnp.zeros_like(w_lo)
    of_ref[0] = jnp.concatenate([w_lo, w_hi, zf, zf, zf, zf, zf, zf], axis=0)


def _router(x1, mod, layer, mod_row0, mod_rows, router_wt, router_b, tm):
    rows = x1.shape[0]
    nt = rows // tm
    mblk = 8 if mod_rows == 1 else mod_rows
    mrow = mod_row0 // mblk
    tri = jnp.asarray(np.triu(np.ones((tm, tm), np.float32), 1), BF16)
    rb = jnp.broadcast_to(router_b.astype(F32)[:, None], (N_EXPERTS, tm))
    mod_spec = lambda c: pl.BlockSpec((1, mblk, D_MODEL), lambda i: (layer, mrow, c))
    oi, of, cnt = pl.pallas_call(
        functools.partial(_router_kernel, mod_rows=mod_rows),
        grid=(nt,),
        in_specs=[pl.BlockSpec((tm, D_MODEL), lambda i: (i, 0)), mod_spec(3), mod_spec(4),
                  pl.BlockSpec((N_EXPERTS, D_MODEL), lambda i: (0, 0)),
                  pl.BlockSpec((N_EXPERTS, tm), lambda i: (0, 0)),
                  pl.BlockSpec((tm, tm), lambda i: (0, 0))],
        out_specs=[pl.BlockSpec((1, 8, tm), lambda i: (i, 0, 0)),
                   pl.BlockSpec((1, 8, tm), lambda i: (i, 0, 0)),
                   pl.BlockSpec((BIN_ROWS, 128), lambda i: (0, 0))],
        out_shape=[jax.ShapeDtypeStruct((nt, 8, tm), I32),
                   jax.ShapeDtypeStruct((nt, 8, tm), F32),
                   jax.ShapeDtypeStruct((BIN_ROWS, 128), I32)],
        scratch_shapes=[pltpu.VMEM((BIN_ROWS, 128), F32)],
        compiler_params=_cparams(("arbitrary",)),
        name="router",
    )(x1, mod, mod, router_wt, rb, tri)
    return _unpack_route(oi, of, cnt)


def _unpack_route(oi, of, cnt):
    flat = lambda a, r: a[:, r, :].reshape(-1)
    return (flat(oi, 0), flat(oi, 1), flat(oi, 2), flat(oi, 3), flat(of, 0), flat(of, 1),
            cnt[:N_BINS, 0])


POST_TM = 512


def _post_mix(mixed, x, mod, layer, mod_row0, w_out_bf, ln_g, ln_b, router_wt, router_b):
    rows = x.shape[0]
    tm = POST_TM
    nt = rows // tm
    mrow = mod_row0 // 8
    tri = jnp.asarray(np.triu(np.ones((tm, tm), np.float32), 1), BF16)
    rb = jnp.broadcast_to(router_b.astype(F32)[:, None], (N_EXPERTS, tm))
    mod_spec = lambda c: pl.BlockSpec((1, 8, D_MODEL), lambda i: (layer, mrow, c))
    row_spec = pl.BlockSpec((tm, D_MODEL), lambda i: (i, 0))
    const2 = lambda a: pl.BlockSpec(a.shape, lambda i: (0, 0))
    x1, oi, of, cnt = pl.pallas_call(
        _post_mix_kernel,
        grid=(nt,),
        in_specs=[row_spec, row_spec, mod_spec(2), const2(w_out_bf), const2(ln_g), const2(ln_b),
                  mod_spec(3), mod_spec(4), const2(router_wt), const2(rb), const2(tri)],
        out_specs=[row_spec,
                   pl.BlockSpec((1, 8, tm), lambda i: (i, 0, 0)),
                   pl.BlockSpec((1, 8, tm), lambda i: (i, 0, 0)),
                   pl.BlockSpec((BIN_ROWS, 128), lambda i: (0, 0))],
        out_shape=[jax.ShapeDtypeStruct((rows, D_MODEL), F32),
                   jax.ShapeDtypeStruct((nt, 8, tm), I32),
                   jax.ShapeDtypeStruct((nt, 8, tm), F32),
                   jax.ShapeDtypeStruct((BIN_ROWS, 128), I32)],
        scratch_shapes=[pltpu.VMEM((BIN_ROWS, 128), F32)],
        compiler_params=_cparams(("arbitrary",)),
        name="post_mix",
    )(mixed, x, mod, w_out_bf, ln_g, ln_b, mod, mod, router_wt, rb, tri)
    return x1, _unpack_route(oi, of, cnt)


DMA_UNROLL = 8
ROW_GROUPS = MOE_TM // DMA_UNROLL


def _for_rows(n, fn):
    def body_u(j, carry):
        for u in range(DMA_UNROLL):
            fn(j, u)
        return carry
    lax.fori_loop(0, n // DMA_UNROLL, body_u, 0)

    def body_1(r, carry):
        fn(r // DMA_UNROLL, r % DMA_UNROLL)
        return carry
    lax.fori_loop((n // DMA_UNROLL) * DMA_UNROLL, n, body_1, 0)


def _wait_rows(n, buf, sem):
    p = ROW_GROUPS
    while p >= 1:
        @pl.when((n & (p * DMA_UNROLL)) != 0)
        def _(p=p):
            pltpu.make_async_copy(buf.at[pl.ds(0, p)], buf.at[pl.ds(0, p)], sem).wait()
        p //= 2
    p = DMA_UNROLL // 2
    while p >= 1:
        @pl.when((n & p) != 0)
        def _(p=p):
            pltpu.make_async_copy(buf.at[0, pl.ds(0, p)], buf.at[0, pl.ds(0, p)], sem).wait()
        p //= 2


def _moe_sorted_kernel(ea_ref, eb_ref, nvalid_ref, pos_ref,
                       x_hbm, sh_ref, sc_ref, gate_ref, rw_ref,
                       wga_ref, wua_ref, wda_ref, wgb_ref, wub_ref, wdb_ref,
                       lng_ref, lnb_ref,
                       out_hbm,
                       inv, xbuf, obuf, wg_a, wu_a, wd_a, wg_b, wu_b, wd_b, gsem, ssem):
    i = pl.program_id(0)
    nt = pl.num_programs(0)
    slot = i % 2
    n_tok = pos_ref.shape[0]

    def start_gather(t, dst_slot):
        def one(j, u):
            tok = inv[t * MOE_TM + j * DMA_UNROLL + u]
            pltpu.make_async_copy(x_hbm.at[pl.ds(tok, 1)], xbuf.at[dst_slot, j, pl.ds(u, 1)],
                                  gsem.at[dst_slot]).start()
        _for_rows(nvalid_ref[t], one)

    @pl.when(i == 0)
    def _():
        def body(j, carry):
            for u in range(DMA_UNROLL):
                t = j * DMA_UNROLL + u
                inv[pos_ref[t]] = t
            return carry
        lax.fori_loop(0, n_tok // DMA_UNROLL, body, 0)
        xbuf[...] = jnp.zeros_like(xbuf)
        start_gather(0, 0)

    @pl.when(i >= 2)
    def _():
        _wait_rows(nvalid_ref[jnp.maximum(i - 2, 0)], obuf.at[slot], ssem.at[slot])

    @pl.when(nvalid_ref[i] > 0)
    def _():
        _wait_rows(nvalid_ref[i], xbuf.at[slot], gsem.at[slot])

        @pl.when(i + 1 < nt)
        def _():
            start_gather(jnp.minimum(i + 1, nt - 1), 1 - slot)

        @pl.when(jnp.logical_or(i == 0, ea_ref[i] != ea_ref[jnp.maximum(i - 1, 0)]))
        def _():
            wg_a[...] = wga_ref[0, 0].astype(BF16)
            wu_a[...] = wua_ref[0, 0].astype(BF16)
            wd_a[...] = wda_ref[0, 0].astype(BF16)

        @pl.when(jnp.logical_or(i == 0, eb_ref[i] != eb_ref[jnp.maximum(i - 1, 0)]))
        def _():
            wg_b[...] = wgb_ref[0, 0].astype(BF16)
            wu_b[...] = wub_ref[0, 0].astype(BF16)
            wd_b[...] = wdb_ref[0, 0].astype(BF16)

        x = xbuf[slot].reshape(MOE_TM, D_MODEL)
        h2 = (x * (1.0 + sc_ref[0][:1]) + sh_ref[0][:1]).astype(BF16)

        def expert(wg, wu, wd):
            a = _silu(jnp.dot(h2, wg[...], preferred_element_type=F32)) * \
                jnp.dot(h2, wu[...], preferred_element_type=F32)
            return jnp.dot(a.astype(BF16), wd[...], preferred_element_type=F32)

        aff = 1.0 / (1.0 + jnp.exp(-jnp.dot(h2, rw_ref[...], preferred_element_type=F32)))
        lane = lax.broadcasted_iota(I32, aff.shape, 1)
        a_lo = jnp.sum(jnp.where(lane == ea_ref[i], aff, 0.0), axis=-1, keepdims=True)
        a_hi = jnp.sum(jnp.where(lane == eb_ref[i], aff, 0.0), axis=-1, keepdims=True)
        a_sum = a_lo + a_hi
        y = (a_lo / a_sum) * expert(wg_a, wu_a, wd_a) + (a_hi / a_sum) * expert(wg_b, wu_b, wd_b)
        out = _ln(DN_ALPHA * x + (1.0 + gate_ref[0][:1]) * y, lng_ref[...], lnb_ref[...])
        obuf[slot] = out.reshape(ROW_GROUPS, DMA_UNROLL, D_MODEL)

        def scatter_one(j, u):
            tok = inv[i * MOE_TM + j * DMA_UNROLL + u]
            pltpu.make_async_copy(obuf.at[slot, j, pl.ds(u, 1)], out_hbm.at[pl.ds(tok, 1)],
                                  ssem.at[slot]).start()
        _for_rows(nvalid_ref[i], scatter_one)

    @pl.when(i == nt - 1)
    def _():
        _wait_rows(nvalid_ref[jnp.maximum(i - 1, 0)], obuf.at[1 - slot], ssem.at[1 - slot])
        _wait_rows(nvalid_ref[i], obuf.at[slot], ssem.at[slot])


def _moe_sorted(x1, mod, layer, mod_row0, route, rw_pad, wg, wu, wd, ln_g, ln_b):
    _, _, bin_id, rank, _, _, counts = route
    rows = x1.shape[0]
    nt = rows // MOE_TM + N_BINS
    ns = nt * MOE_TM
    tiles = (counts + MOE_TM - 1) // MOE_TM
    tile_end = jnp.cumsum(tiles)
    row_start = (tile_end - tiles) * MOE_TM
    pos = row_start[bin_id] + rank
    total = tile_end[-1]
    t = jnp.arange(nt, dtype=I32)
    tile_bin = jnp.searchsorted(tile_end, jnp.minimum(t, total - 1), side="right").astype(I32)
    tile_bin = jnp.clip(tile_bin, 0, N_BINS - 1)
    bin_lo = jnp.asarray([g * 4 + _PAIR_LO[p] for g in range(N_GROUPS) for p in range(N_PAIRS)], I32)
    bin_hi = jnp.asarray([g * 4 + _PAIR_HI[p] for g in range(N_GROUPS) for p in range(N_PAIRS)], I32)
    tile_ea, tile_eb = bin_lo[tile_bin], bin_hi[tile_bin]
    nvalid = jnp.clip(counts[tile_bin] - (t - (tile_end - tiles)[tile_bin]) * MOE_TM, 0, MOE_TM)
    nvalid = jnp.where(t < total, nvalid, 0).astype(I32)
    mrow = mod_row0 // 8
    mod_spec = lambda c: pl.BlockSpec((1, 8, D_MODEL), lambda i, *_: (layer, mrow, c))
    w_spec = lambda shape, ref_idx: pl.BlockSpec(
        (1, 1) + shape, lambda i, ea_r, eb_r, *_: (layer, (ea_r, eb_r)[ref_idx][i], 0, 0))
    any_spec = pl.BlockSpec(memory_space=pl.ANY)
    w_bf = lambda shape: pltpu.VMEM(shape, BF16)
    grid_spec = pltpu.PrefetchScalarGridSpec(
        num_scalar_prefetch=4,
        grid=(nt,),
        in_specs=[any_spec,
                  mod_spec(3), mod_spec(4), mod_spec(5),
                  pl.BlockSpec((D_MODEL, 128), lambda i, *_: (0, 0)),
                  w_spec((D_MODEL, D_FF), 0), w_spec((D_MODEL, D_FF), 0), w_spec((D_FF, D_MODEL), 0),
                  w_spec((D_MODEL, D_FF), 1), w_spec((D_MODEL, D_FF), 1), w_spec((D_FF, D_MODEL), 1),
                  pl.BlockSpec((1, D_MODEL), lambda i, *_: (0, 0)),
                  pl.BlockSpec((1, D_MODEL), lambda i, *_: (0, 0))],
        out_specs=any_spec,
        scratch_shapes=[pltpu.SMEM((ns,), I32),
                        pltpu.VMEM((2, ROW_GROUPS, DMA_UNROLL, D_MODEL), F32),
                        pltpu.VMEM((2, ROW_GROUPS, DMA_UNROLL, D_MODEL), F32),
                        w_bf((D_MODEL, D_FF)), w_bf((D_MODEL, D_FF)), w_bf((D_FF, D_MODEL)),
                        w_bf((D_MODEL, D_FF)), w_bf((D_MODEL, D_FF)), w_bf((D_FF, D_MODEL)),
                        pltpu.SemaphoreType.DMA((2,)),
                        pltpu.SemaphoreType.DMA((2,))],
    )
    return pl.pallas_call(
        _moe_sorted_kernel,
        grid_spec=grid_spec,
        out_shape=jax.ShapeDtypeStruct((rows, D_MODEL), F32),
        compiler_params=_cparams(("arbitrary",)),
        name="moe_sorted",
    )(tile_ea, tile_eb, nvalid, pos, x1, mod, mod, mod, rw_pad,
      wg, wu, wd, wg, wu, wd, ln_g, ln_b)


def _moe_dense_kernel(x_ref, dw_ref, sh_ref, sc_ref, gate_ref, wg_ref, wu_ref, wd_ref,
                      lng_ref, lnb_ref, o_ref, acc):
    e = pl.program_id(0)

    @pl.when(e == 0)
    def _():
        acc[...] = jnp.zeros_like(acc)

    x = x_ref[...]
    h2 = (x * (1.0 + sc_ref[0]) + sh_ref[0]).astype(BF16)
    a = _silu(jnp.dot(h2, wg_ref[0, 0].astype(BF16), preferred_element_type=F32)) * \
        jnp.dot(h2, wu_ref[0, 0].astype(BF16), preferred_element_type=F32)
    y = jnp.dot(a.astype(BF16), wd_ref[0, 0].astype(BF16), preferred_element_type=F32)
    acc[...] = acc[...] + dw_ref[0][:, :1] * y

    @pl.when(e == pl.num_programs(0) - 1)
    def _():
        o_ref[...] = _ln(DN_ALPHA * x + (1.0 + gate_ref[0]) * acc[...], lng_ref[...], lnb_ref[...])


def _moe_dense(x1, mod, layer, route, wg, wu, wd, ln_g, ln_b):
    ea, eb, _, _, w_lo, w_hi, _ = route
    nb = x1.shape[0]
    eids = jnp.arange(N_EXPERTS, dtype=I32)[:, None]
    dw = jnp.where(eids == ea[None, :], w_lo[None, :], 0.0) + \
        jnp.where(eids == eb[None, :], w_hi[None, :], 0.0)
    dw = jnp.broadcast_to(dw[:, :, None], (N_EXPERTS, nb, 128))
    mod_spec = lambda c: pl.BlockSpec((1, nb, D_MODEL), lambda e: (layer, 0, c))
    const2 = lambda a: pl.BlockSpec(a.shape, lambda e: (0, 0))
    return pl.pallas_call(
        _moe_dense_kernel,
        grid=(N_EXPERTS,),
        in_specs=[const2(x1), pl.BlockSpec((1, nb, 128), lambda e: (e, 0, 0)),
                  mod_spec(3), mod_spec(4), mod_spec(5),
                  pl.BlockSpec((1, 1, D_MODEL, D_FF), lambda e: (layer, e, 0, 0)),
                  pl.BlockSpec((1, 1, D_MODEL, D_FF), lambda e: (layer, e, 0, 0)),
                  pl.BlockSpec((1, 1, D_FF, D_MODEL), lambda e: (layer, e, 0, 0)),
                  const2(ln_g), const2(ln_b)],
        out_specs=const2(x1),
        out_shape=jax.ShapeDtypeStruct(x1.shape, F32),
        scratch_shapes=[pltpu.VMEM(x1.shape, F32)],
        compiler_params=_cparams(("arbitrary",)),
        name="moe_dense",
    )(x1, dw, mod, mod, mod, wg, wu, wd, ln_g, ln_b)


def kernel(x_prompt, x_sample, state_ret, cache_swa_k, cache_swa_v, c_prompt, c_sample, w_in, w_out, ret_gn_gain, swa_sinks, rel_bias_table, ada_w, ada_b, ln1_g, ln1_b, ln2_g, ln2_b, router_w, router_b, exp_w_gate, exp_w_up, exp_w_down):
    seq = x_prompt.shape[1]
    nb = x_sample.shape[0]
    past_len = 16384
    assert x_prompt.shape[0] == 1 and x_sample.shape[1] == 1

    c_all = jnp.concatenate([c_sample, c_prompt, jnp.zeros((7, D_MODEL), F32)], axis=0)
    mod = _ada(c_all, ada_w, ada_b)
    prompt_row = nb

    bias_tabs = _bias_tables(rel_bias_table.astype(F32))
    gl, decay, xi, zeta, gamma1 = _ret_consts()
    rope_p = _rope_tables(jnp.arange(seq, dtype=I32))
    rope_s = _rope_tables(jnp.full((nb,), past_len, I32))
    router_wt = router_w.astype(F32).T
    rw_pad = jnp.pad(router_w.astype(BF16), ((0, 0), (0, 128 - N_EXPERTS)))
    zeta_p = jnp.tile(zeta, (INPROJ_TM // CHUNK, 1))
    zeta_s = jnp.ones((nb, RET_W), F32)

    xp = x_prompt.reshape(seq, D_MODEL)
    xs = x_sample.reshape(nb, D_MODEL)
    st_p, k_p, v_p, st_s, k_s, v_s = [], [], [], [], [], []
    for l in range(DEPTH):
        w_in_bf = w_in[l].astype(BF16)
        w_out_bf = w_out[l].astype(BF16)
        experts = (exp_w_gate.astype(F32), exp_w_up.astype(F32), exp_w_down.astype(F32))
        gain = ret_gn_gain[l].astype(F32)
        sinks = swa_sinks[l].astype(F32)
        row = lambda a: a[l].astype(F32).reshape(1, D_MODEL)

        proj = _inproj(xp, mod, l, prompt_row, 1, w_in_bf, rope_p, zeta_p, INPROJ_TM)
        mixed, st = _mix_prompt(proj, (gl, decay, xi), bias_tabs, sinks, gain.reshape(1, RET_W))
        skv = proj[6]
        st_p.append(st.reshape(1, N_RET_HEADS, HEAD_DIM, HEAD_DIM))
        k_p.append(skv[seq - CHUNK:, :SWA_KVW].reshape(1, CHUNK, N_KV_HEADS, HEAD_DIM))
        v_p.append(skv[seq - CHUNK:, SWA_KVW:].reshape(1, CHUNK, N_KV_HEADS, HEAD_DIM))
        x1, route = _post_mix(mixed, xp, mod, l, prompt_row, w_out_bf, row(ln1_g), row(ln1_b),
                              router_wt, router_b)
        xp = _moe_sorted(x1, mod, l, prompt_row, route, rw_pad, *experts, row(ln2_g), row(ln2_b))

        proj = _inproj(xs, mod, l, 0, nb, w_in_bf, rope_s, zeta_s, nb)
        x1, nst, nk, nv = _mix_sample(proj, xs, mod, l, w_out_bf, gamma1, bias_tabs[1], sinks, gain,
                                      row(ln1_g), row(ln1_b), state_ret[l].astype(F32),
                                      cache_swa_k[l], cache_swa_v[l])
        st_s.append(nst)
        k_s.append(nk)
        v_s.append(nv)
        route = _router(x1, mod, l, 0, nb, router_wt, router_b, nb)
        xs = _moe_dense(x1, mod, l, route, *experts, row(ln2_g), row(ln2_b))

    return (xp.reshape(1, seq, D_MODEL), xs.reshape(nb, 1, D_MODEL),
            jnp.stack(st_p), jnp.stack(k_p), jnp.stack(v_p),
            jnp.stack(st_s), jnp.stack(k_s), jnp.stack(v_s))
```

```python
import functools
import math

import numpy as np
import jax
import jax.numpy as jnp
from jax import lax
from jax.experimental import pallas as pl
from jax.experimental.pallas import tpu as pltpu

F32 = jnp.float32
BF16 = jnp.bfloat16
I32 = jnp.int32

D_MODEL = 1024
DEPTH = 2
HEAD_DIM = 64
N_RET_HEADS = 8
N_SWA_HEADS = 8
N_KV_HEADS = 2
GQA_GROUP = N_SWA_HEADS // N_KV_HEADS
RET_W = N_RET_HEADS * HEAD_DIM
SWA_QW = N_SWA_HEADS * HEAD_DIM
SWA_KVW = N_KV_HEADS * HEAD_DIM
PROJ_W = 4 * RET_W + SWA_QW + 2 * SWA_KVW
CHUNK = 128
ROPE_BASE = 10000.0
N_BUCKETS = 32
MAX_DISTANCE = 128
N_EXPERTS = 16
N_GROUPS = 4
EXPERTS_PER_GROUP = 4
D_FF = 512
LN_EPS = 1e-5
DN_ALPHA = (2 * DEPTH) ** 0.25
QK_SCALE = HEAD_DIM ** -0.5
NEG_BIG = -1e30

N_PAIRS = 6
N_BINS = N_GROUPS * N_PAIRS
BIN_ROWS = 32
MOE_TM = 256
INPROJ_TM = 512
MIX_SUB = 2
SPLIT_W = 128 * N_RET_HEADS
_SWA_HEAD_ORDER = (0, 4, 1, 5, 2, 6, 3, 7)
VMEM_LIMIT = 56 * 1024 * 1024

_PAIR_LO = (0, 0, 0, 1, 1, 2)
_PAIR_HI = (1, 2, 3, 2, 3, 3)


def _cparams(sem):
    return pltpu.CompilerParams(dimension_semantics=sem, vmem_limit_bytes=VMEM_LIMIT)


def _ln(v, g, b):
    mu = jnp.mean(v, axis=-1, keepdims=True)
    c = v - mu
    var = jnp.mean(c * c, axis=-1, keepdims=True)
    return c * lax.rsqrt(var + LN_EPS) * g + b


def _silu(v):
    return v * (1.0 / (1.0 + jnp.exp(-v)))


def _ada_kernel(c_ref, w_ref, b_ref, o_ref):
    o_ref[0] = jnp.dot(c_ref[...].astype(BF16), w_ref[0].astype(BF16),
                       preferred_element_type=F32) + b_ref[0]


def _ada(c_all, ada_w, ada_b):
    rows = c_all.shape[0]
    nt = 6 * D_MODEL // 1024
    return pl.pallas_call(
        _ada_kernel,
        grid=(DEPTH, nt),
        in_specs=[pl.BlockSpec((rows, D_MODEL), lambda l, j: (0, 0)),
                  pl.BlockSpec((1, D_MODEL, 1024), lambda l, j: (l, 0, j)),
                  pl.BlockSpec((1, 1, 1024), lambda l, j: (l, 0, j))],
        out_specs=pl.BlockSpec((1, rows, 1024), lambda l, j: (l, 0, j)),
        out_shape=jax.ShapeDtypeStruct((DEPTH, rows, 6 * D_MODEL), F32),
        compiler_params=_cparams(("arbitrary", "arbitrary")),
        name="ada",
    )(c_all, ada_w, ada_b.reshape(DEPTH, 1, 6 * D_MODEL))


def _bias_kernel(tab_ref, bkt_ref, fold_ref, row_ref):
    bkt = bkt_ref[...]
    rows = lax.broadcasted_iota(I32, (CHUNK, CHUNK), 0)
    cols = lax.broadcasted_iota(I32, (CHUNK, CHUNK), 1)
    from_prev = cols > rows
    for h in range(N_SWA_HEADS):
        acc = jnp.zeros(bkt.shape, F32)
        for b in range(N_BUCKETS):
            acc = jnp.where(bkt == b, tab_ref[b, h], acc)
        own = acc[:, CHUNK:]
        fold_ref[0, h] = jnp.where(from_prev, NEG_BIG, own)
        fold_ref[1, h] = jnp.where(from_prev, acc[:, :CHUNK], own)
        row_ref[h:h + 1, :] = own[CHUNK - 1:CHUNK, :]


def _t5_bucket(rel):
    max_exact = N_BUCKETS // 2
    relf = jnp.maximum(rel, 1).astype(F32)
    large = max_exact + (jnp.log(relf / max_exact) / math.log(MAX_DISTANCE / max_exact)
                         * (N_BUCKETS - max_exact)).astype(I32)
    large = jnp.minimum(large, N_BUCKETS - 1)
    return jnp.where(rel < max_exact, rel, large)


def _bias_tables(rel_bias_table):
    qi = jnp.arange(CHUNK)
    si = jnp.arange(2 * CHUNK)
    rel = CHUNK + qi[:, None] - si[None, :]
    bkt = _t5_bucket(jnp.maximum(rel, 0)).astype(I32)
    return pl.pallas_call(
        _bias_kernel,
        in_specs=[pl.BlockSpec(memory_space=pltpu.SMEM),
                  pl.BlockSpec((CHUNK, 2 * CHUNK), lambda: (0, 0))],
        out_specs=[pl.BlockSpec((2, N_SWA_HEADS, CHUNK, CHUNK), lambda: (0, 0, 0, 0)),
                   pl.BlockSpec((N_SWA_HEADS, CHUNK), lambda: (0, 0))],
        out_shape=[jax.ShapeDtypeStruct((2, N_SWA_HEADS, CHUNK, CHUNK), F32),
                   jax.ShapeDtypeStruct((N_SWA_HEADS, CHUNK), F32)],
        name="t5_bias",
    )(rel_bias_table, bkt)


def _rotary(v, cos, s_lo, s_hi):
    outs = []
    for j in range(RET_W // 128):
        blk = v[:, j * 128:(j + 1) * 128]
        outs.append(blk * cos + pltpu.roll(blk, 96, 1) * s_lo + pltpu.roll(blk, 32, 1) * s_hi)
    return jnp.concatenate(outs, axis=-1)


def _inproj_kernel(x_ref, sh_ref, sc_ref, w_ref, cs_ref, zeta_ref,
                   q_ref, k_ref, kz_ref, v_ref, g_ref, sq_ref, skv_ref, *, mod_rows):
    sh = sh_ref[0][:mod_rows]
    sc = sc_ref[0][:mod_rows]
    h = (x_ref[...] * (1.0 + sc) + sh).astype(BF16)
    cs = cs_ref[...]
    first = (lax.broadcasted_iota(I32, cs.shape, 1) % HEAD_DIM) < (HEAD_DIM // 2)
    cos = jnp.where(first, cs, pltpu.roll(cs, 32, 1))
    s_lo = jnp.where(first, -pltpu.roll(cs, 96, 1), 0.0)
    s_hi = jnp.where(first, 0.0, cs)

    def proj(lo, hi):
        return jnp.dot(h, w_ref[:, lo:hi], preferred_element_type=F32)

    left = lax.broadcasted_iota(I32, cs.shape, 1) < HEAD_DIM

    def store_split(ref, val):
        for b in range(RET_W // 128):
            blk = val[:, b * 128:(b + 1) * 128]
            ref[:, (2 * b) * 128:(2 * b + 1) * 128] = jnp.where(left, blk, 0.0).astype(BF16)
            ref[:, (2 * b + 1) * 128:(2 * b + 2) * 128] = jnp.where(left, 0.0, blk).astype(BF16)

    store_split(q_ref, _rotary(proj(0, RET_W), cos, s_lo, s_hi))
    k = _rotary(proj(RET_W, 2 * RET_W), cos, s_lo, s_hi) * QK_SCALE
    store_split(k_ref, k)
    store_split(kz_ref, k * zeta_ref[...])
    store_split(v_ref, proj(2 * RET_W, 3 * RET_W))
    g_ref[...] = proj(3 * RET_W, 4 * RET_W)
    store_split(sq_ref, proj(4 * RET_W, 4 * RET_W + SWA_QW) * QK_SCALE)
    skv_ref[...] = proj(4 * RET_W + SWA_QW, PROJ_W)


def _inproj(x, mod, layer, mod_row0, mod_rows, w_in_bf, rope, zeta_tile, tm):
    rows = x.shape[0]
    mblk = 8 if mod_rows == 1 else mod_rows
    mrow = mod_row0 // mblk
    row_spec = lambda w: pl.BlockSpec((tm, w), lambda i: (i, 0))
    mod_spec = lambda c: pl.BlockSpec((1, mblk, D_MODEL), lambda i: (layer, mrow, c))
    return pl.pallas_call(
        functools.partial(_inproj_kernel, mod_rows=mod_rows),
        grid=(rows // tm,),
        in_specs=[row_spec(D_MODEL), mod_spec(0), mod_spec(1),
                  pl.BlockSpec((D_MODEL, PROJ_W), lambda i: (0, 0)),
                  row_spec(128),
                  pl.BlockSpec((tm, RET_W), lambda i: (0, 0))],
        out_specs=[row_spec(SPLIT_W), row_spec(SPLIT_W), row_spec(SPLIT_W), row_spec(SPLIT_W),
                   row_spec(RET_W), row_spec(SPLIT_W), row_spec(2 * SWA_KVW)],
        out_shape=[jax.ShapeDtypeStruct((rows, SPLIT_W), BF16),
                   jax.ShapeDtypeStruct((rows, SPLIT_W), BF16),
                   jax.ShapeDtypeStruct((rows, SPLIT_W), BF16),
                   jax.ShapeDtypeStruct((rows, SPLIT_W), BF16),
                   jax.ShapeDtypeStruct((rows, RET_W), F32),
                   jax.ShapeDtypeStruct((rows, SPLIT_W), BF16),
                   jax.ShapeDtypeStruct((rows, 2 * SWA_KVW), F32)],
        compiler_params=_cparams(("arbitrary",)),
        name="inproj",
    )(x, mod, mod, w_in_bf, rope, zeta_tile)


def _rope_tables(pos):
    half = HEAD_DIM // 2
    inv = ROPE_BASE ** (-jnp.arange(half, dtype=F32) / half)
    ang = pos.astype(F32)[:, None] * inv[None, :]
    cos, sin = jnp.cos(ang), jnp.sin(ang)
    return jnp.concatenate([cos, sin, cos, sin], axis=-1)


def _mix_prompt_kernel(gl_ref, sink_ref,
                       q_ref, k_ref, kz_ref, v_ref, g_ref, sq_ref, kvc_ref, kvp_ref,
                       decay_ref, xi_ref, bias_ref, gain_ref,
                       mixed_ref, st_ref, state):
    i = pl.program_id(0)

    @pl.when(i == 0)
    def _():
        state[...] = jnp.zeros_like(state)

    gain = gain_ref[...]
    left = lax.broadcasted_iota(I32, (CHUNK, 128), 1) < HEAD_DIM
    from_prev = (lax.broadcasted_iota(I32, (CHUNK, CHUNK), 1) >
                 lax.broadcasted_iota(I32, (CHUNK, CHUNK), 0))
    inv_dim = 1.0 / HEAD_DIM

    for sub in range(MIX_SUB):
        rows = slice(sub * CHUNK, (sub + 1) * CHUNK)

        heads = range(N_RET_HEADS)
        hsl = [slice(h * 128, (h + 1) * 128) for h in heads]
        scores_r = [lax.dot_general(q_ref[rows, hsl[h]], k_ref[rows, hsl[h]], (((1,), (1,)), ((), ())),
                                    preferred_element_type=F32) for h in heads]
        s_olds = [state[h] for h in heads]
        inter = [jnp.dot(q_ref[rows, hsl[h]], s_olds[h].astype(BF16), preferred_element_type=F32)
                 for h in heads]
        upds = [lax.dot_general(kz_ref[rows, hsl[h]], v_ref[rows, hsl[h]], (((0,), (0,)), ((), ())),
                                preferred_element_type=F32) for h in heads]
        for h in heads:
            state[h] = gl_ref[h] * s_olds[h] + upds[h]
        probs_r = [(scores_r[h] * decay_ref[h]).astype(BF16) for h in heads]
        intra = [jnp.dot(probs_r[h], v_ref[rows, hsl[h]], preferred_element_type=F32) for h in heads]

        owns = [left if h % 2 == 0 else jnp.logical_not(left) for h in heads]
        outs = [intra[h] + inter[h] * xi_ref[:, hsl[h]] for h in heads]
        mus = [jnp.sum(outs[h], axis=-1, keepdims=True) * inv_dim for h in heads]
        cen = [jnp.where(owns[h], outs[h] - mus[h], 0.0) for h in heads]
        var = [jnp.sum(cen[h] * cen[h], axis=-1, keepdims=True) * inv_dim for h in heads]
        nrm = [cen[h] * lax.rsqrt(var[h] + LN_EPS) for h in heads]
        ret_blocks = [nrm[2 * b] + nrm[2 * b + 1] for b in range(N_RET_HEADS // 2)]
        ret_o = jnp.concatenate(ret_blocks, axis=-1) * gain * _silu(g_ref[rows, :])

        kvc = kvc_ref[rows, :]
        kvp = kvp_ref[...] if sub == 0 else kvc_ref[(sub - 1) * CHUNK:sub * CHUNK, :]
        kcat = jnp.concatenate([kvp[:, :SWA_KVW], kvc[:, :SWA_KVW]], axis=0)
        vcat = jnp.concatenate([kvp[:, SWA_KVW:], kvc[:, SWA_KVW:]], axis=0)
        left2 = jnp.concatenate([left, left], axis=0)
        table = jnp.minimum(i, 1) if sub == 0 else 1
        swa_blocks = [None] * GQA_GROUP
        kks, vvs = [], []
        for kv in range(N_KV_HEADS):
            own2 = left2 if kv == 0 else jnp.logical_not(left2)
            kks.append(jnp.where(own2, kcat, 0.0).astype(BF16))
            vvs.append(jnp.where(own2, vcat, 0.0).astype(BF16))
        scores = []
        for hh in range(N_SWA_HEADS):
            kv, j = divmod(hh, GQA_GROUP)
            blk = 2 * j + kv
            qh = sq_ref[rows, blk * 128:(blk + 1) * 128]
            scores.append(lax.dot_general(qh, kks[kv], (((1,), (1,)), ((), ())),
                                          preferred_element_type=F32))
        probs = []
        for hh in range(N_SWA_HEADS):
            s2 = scores[hh]
            s = jnp.where(from_prev, s2[:, :CHUNK], s2[:, CHUNK:]) + bias_ref[table, hh]
            sink = sink_ref[hh]
            m = jnp.maximum(jnp.max(s, axis=-1, keepdims=True), sink)
            p = jnp.exp(s - m)
            den = jnp.sum(p, axis=-1, keepdims=True) + jnp.exp(sink - m)
            pn = p / den
            p2 = jnp.concatenate([jnp.where(from_prev, pn, 0.0), jnp.where(from_prev, 0.0, pn)], axis=-1)
            probs.append(p2.astype(BF16))
        for hh in range(N_SWA_HEADS):
            kv, j = divmod(hh, GQA_GROUP)
            o = jnp.dot(probs[hh], vvs[kv], preferred_element_type=F32)
            swa_blocks[j] = o if kv == 0 else swa_blocks[j] + o
        swa_o = jnp.concatenate(swa_blocks, axis=-1)

        mixed_ref[rows, :] = jnp.concatenate([ret_o, swa_o], axis=-1).astype(BF16)

    @pl.when(i == pl.num_programs(0) - 1)
    def _():
        st_ref[...] = state[...]


def _mix_prompt(proj, ret_consts, bias_tabs, sinks, gain):
    q, k, kz, v, g, sq, skv = proj
    rows = q.shape[0]
    tm = MIX_SUB * CHUNK
    gl, decay, xi = ret_consts
    bias_fold = bias_tabs[0]
    row_spec = lambda w: pl.BlockSpec((tm, w), lambda i: (i, 0))
    const2 = lambda a: pl.BlockSpec(a.shape, lambda i: (0, 0))
    const3 = lambda a: pl.BlockSpec(a.shape, lambda i: (0, 0, 0))
    smem = pl.BlockSpec(memory_space=pltpu.SMEM)
    mixed, st_full = pl.pallas_call(
        _mix_prompt_kernel,
        grid=(rows // tm,),
        in_specs=[smem, smem,
                  row_spec(SPLIT_W), row_spec(SPLIT_W), row_spec(SPLIT_W), row_spec(SPLIT_W), row_spec(RET_W),
                  row_spec(SPLIT_W), row_spec(2 * SWA_KVW),
                  pl.BlockSpec((CHUNK, 2 * SWA_KVW), lambda i: (jnp.maximum(MIX_SUB * i - 1, 0), 0)),
                  const3(decay), const2(xi),
                  pl.BlockSpec(bias_fold.shape, lambda i: (0, 0, 0, 0)),
                  const2(gain)],
        out_specs=[row_spec(D_MODEL),
                   pl.BlockSpec((N_RET_HEADS, 128, 128), lambda i: (0, 0, 0))],
        out_shape=[jax.ShapeDtypeStruct((rows, D_MODEL), BF16),
                   jax.ShapeDtypeStruct((N_RET_HEADS, 128, 128), F32)],
        scratch_shapes=[pltpu.VMEM((N_RET_HEADS, 128, 128), F32)],
        compiler_params=_cparams(("arbitrary",)),
        name="mix_prompt",
    )(gl, sinks, q, k, kz, v, g, sq, skv, skv, decay, xi, bias_fold, gain)
    lo, hi = slice(0, HEAD_DIM), slice(HEAD_DIM, 128)
    st = jnp.stack([st_full[h, lo, lo] if h % 2 == 0 else st_full[h, hi, hi] for h in range(N_RET_HEADS)])
    return mixed, st


def _ret_consts():
    lg = jnp.log(1.0 - 2.0 ** (-5.0 - jnp.arange(N_RET_HEADS, dtype=F32)))
    idx = jnp.arange(CHUNK, dtype=F32)
    diff = idx[:, None] - idx[None, :]
    decay = jnp.where(diff >= 0, jnp.exp(jnp.maximum(diff, 0.0)[None] * lg[:, None, None]), 0.0)
    xi = jnp.exp((idx + 1.0)[:, None] * lg[None, :])
    zeta = jnp.exp((CHUNK - 1.0 - idx)[:, None] * lg[None, :])
    gl = jnp.exp(CHUNK * lg)
    return gl, decay, jnp.repeat(xi, 128, axis=1), jnp.repeat(zeta, HEAD_DIM, axis=1), jnp.exp(1.0 * lg)


SAMPLE_BB = 8


def _mix_sample_kernel(g1_ref, sink_ref,
                       qt_ref, kt_ref, v3_ref, sq3_ref, knew_ref, vnew_ref,
                       st_ref, ck_ref, cv_ref, g3_ref, x_ref, gate_ref,
                       wout_ref, brow_ref, gain_ref, lng_ref, lnb_ref,
                       x1_ref, nst_ref, nk_ref, nv_ref, ret_scr, swa_scr):
    i = pl.program_id(0)
    row = lax.broadcasted_iota(I32, (N_SWA_HEADS, 2 * HEAD_DIM), 0)
    lane = lax.broadcasted_iota(I32, (N_SWA_HEADS, 2 * HEAD_DIM), 1)
    own_half = (row // GQA_GROUP) == (lane // HEAD_DIM)
    sink_col = jnp.concatenate(
        [jnp.full((1, 1), sink_ref[hh], F32) for hh in range(N_SWA_HEADS)], axis=0)
    brow = brow_ref[...]
    qt = qt_ref[0]
    kt = kt_ref[0]

    for b in range(SAMPLE_BB):
        o_rows = []
        for h in range(N_RET_HEADS):
            rs = slice(h * HEAD_DIM, (h + 1) * HEAD_DIM)
            s_old = st_ref[b, rs, :]
            kcol = kt[rs, b:b + 1]
            qcol = qt[rs, b:b + 1]
            vrow = v3_ref[b, h:h + 1, :]
            s_new = g1_ref[h] * s_old + kcol * vrow
            nst_ref[b, rs, :] = s_new
            o_rows.append(jnp.sum(qcol * s_new, axis=0, keepdims=True))
        ret_scr[i * SAMPLE_BB + b] = jnp.concatenate(o_rows, axis=0)

        kk = jnp.concatenate([ck_ref[b, 1:, :], knew_ref[b:b + 1, :]], axis=0)
        vv = jnp.concatenate([cv_ref[b, 1:, :], vnew_ref[b:b + 1, :]], axis=0)
        nk_ref[b] = kk
        nv_ref[b] = vv
        q8 = sq3_ref[b]
        qblk = jnp.where(own_half, jnp.concatenate([q8, q8], axis=-1), 0.0).astype(BF16)
        s = lax.dot_general(qblk, kk.astype(BF16), (((1,), (1,)), ((), ())),
                            preferred_element_type=F32) + brow
        m = jnp.maximum(jnp.max(s, axis=-1, keepdims=True), sink_col)
        p = jnp.exp(s - m)
        den = jnp.sum(p, axis=-1, keepdims=True) + jnp.exp(sink_col - m)
        o = jnp.dot((p / den).astype(BF16), vv.astype(BF16), preferred_element_type=F32)
        swa_scr[i * SAMPLE_BB + b] = jnp.where(own_half[:, :HEAD_DIM], o[:, :HEAD_DIM], o[:, HEAD_DIM:])

    @pl.when(i == pl.num_programs(0) - 1)
    def _():
        y = jnp.zeros(x_ref.shape, F32)
        for h in range(N_RET_HEADS):
            o = ret_scr[:, h, :]
            mu = jnp.mean(o, axis=-1, keepdims=True)
            c = o - mu
            var = jnp.mean(c * c, axis=-1, keepdims=True)
            r = c * lax.rsqrt(var + LN_EPS) * gain_ref[h:h + 1, :] * _silu(g3_ref[h])
            y = y + jnp.dot(r.astype(BF16), wout_ref[h * HEAD_DIM:(h + 1) * HEAD_DIM, :],
                            preferred_element_type=F32)
        for hh in range(N_SWA_HEADS):
            o = swa_scr[:, hh, :].astype(BF16)
            lo = RET_W + hh * HEAD_DIM
            y = y + jnp.dot(o, wout_ref[lo:lo + HEAD_DIM, :], preferred_element_type=F32)
        x1_ref[...] = _ln(DN_ALPHA * x_ref[...] + (1.0 + gate_ref[0]) * y, lng_ref[...], lnb_ref[...])


def _mix_sample(proj, x, mod, layer, w_out_bf, gamma1, bias_row, sinks, gain8, ln_g, ln_b,
                st, ck, cv):
    q, k, _, v, g, sq, skv = proj
    nb = x.shape[0]
    steps = nb // SAMPLE_BB

    def joined(a):
        a = a.astype(F32).reshape(nb, RET_W // 128, 2, 128)
        return (a[:, :, 0, :] + a[:, :, 1, :]).reshape(nb, RET_W)

    to_cols = lambda a: joined(a).reshape(steps, SAMPLE_BB, RET_W).transpose(0, 2, 1)
    qt, kt = to_cols(q), to_cols(k)
    v3 = joined(v).reshape(nb, N_RET_HEADS, HEAD_DIM)
    sq3 = joined(sq).reshape(nb, GQA_GROUP, N_KV_HEADS, HEAD_DIM).transpose(0, 2, 1, 3)
    sq3 = sq3.reshape(nb, N_SWA_HEADS, HEAD_DIM)
    g3 = g.reshape(nb, N_RET_HEADS, HEAD_DIM).transpose(1, 0, 2)
    knew, vnew = skv[:, :SWA_KVW], skv[:, SWA_KVW:]
    st2 = st.reshape(nb, RET_W, HEAD_DIM)
    ck2 = ck.reshape(nb, CHUNK, SWA_KVW)
    cv2 = cv.reshape(nb, CHUNK, SWA_KVW)
    smem = pl.BlockSpec(memory_space=pltpu.SMEM)
    blk3 = lambda a, b, c: pl.BlockSpec((a, b, c), lambda i: (i, 0, 0))
    const2 = lambda a: pl.BlockSpec(a.shape, lambda i: (0, 0))
    const3 = lambda a: pl.BlockSpec(a.shape, lambda i: (0, 0, 0))
    x1, nst, nk, nv = pl.pallas_call(
        _mix_sample_kernel,
        grid=(steps,),
        in_specs=[smem, smem,
                  blk3(1, RET_W, SAMPLE_BB), blk3(1, RET_W, SAMPLE_BB),
                  blk3(SAMPLE_BB, N_RET_HEADS, HEAD_DIM), blk3(SAMPLE_BB, N_SWA_HEADS, HEAD_DIM),
                  pl.BlockSpec((SAMPLE_BB, SWA_KVW), lambda i: (i, 0)),
                  pl.BlockSpec((SAMPLE_BB, SWA_KVW), lambda i: (i, 0)),
                  blk3(SAMPLE_BB, RET_W, HEAD_DIM), blk3(SAMPLE_BB, CHUNK, SWA_KVW),
                  blk3(SAMPLE_BB, CHUNK, SWA_KVW),
                  const3(g3), const2(x),
                  pl.BlockSpec((1, nb, D_MODEL), lambda i: (layer, 0, 2)),
                  const2(w_out_bf), const2(bias_row), const2(gain8), const2(ln_g), const2(ln_b)],
        out_specs=[const2(x), blk3(SAMPLE_BB, RET_W, HEAD_DIM), blk3(SAMPLE_BB, CHUNK, SWA_KVW),
                   blk3(SAMPLE_BB, CHUNK, SWA_KVW)],
        out_shape=[jax.ShapeDtypeStruct(x.shape, F32), jax.ShapeDtypeStruct(st2.shape, F32),
                   jax.ShapeDtypeStruct(ck2.shape, F32), jax.ShapeDtypeStruct(cv2.shape, F32)],
        scratch_shapes=[pltpu.VMEM((nb, N_RET_HEADS, HEAD_DIM), F32),
                        pltpu.VMEM((nb, N_SWA_HEADS, HEAD_DIM), F32)],
        compiler_params=_cparams(("arbitrary",)),
        name="mix_sample",
    )(gamma1, sinks, qt, kt, v3, sq3, knew, vnew, st2, ck2, cv2, g3, x, mod,
      w_out_bf, bias_row, gain8, ln_g, ln_b)
    return x1, nst.reshape(st.shape), nk.reshape(ck.shape), nv.reshape(cv.shape)


def _router_kernel(x_ref, sh_ref, sc_ref, rwt_ref, rb_ref, tri_ref,
                   oi_ref, of_ref, cnt_ref, run_ref, *, mod_rows):
    h2 = x_ref[...] * (1.0 + sc_ref[0][:mod_rows]) + sh_ref[0][:mod_rows]
    _route_rows(h2, rwt_ref, rb_ref, tri_ref, oi_ref, of_ref, cnt_ref, run_ref)


def _post_mix_kernel(mixed_ref, x_ref, gate_ref, wout_ref, lng_ref, lnb_ref,
                     sh_ref, sc_ref, rwt_ref, rb_ref, tri_ref,
                     x1_ref, oi_ref, of_ref, cnt_ref, run_ref):
    y = jnp.dot(mixed_ref[...], wout_ref[...], preferred_element_type=F32)
    x1 = _ln(DN_ALPHA * x_ref[...] + (1.0 + gate_ref[0][:1]) * y, lng_ref[...], lnb_ref[...])
    x1_ref[...] = x1
    h2 = x1 * (1.0 + sc_ref[0][:1]) + sh_ref[0][:1]
    _route_rows(h2, rwt_ref, rb_ref, tri_ref, oi_ref, of_ref, cnt_ref, run_ref)


def _route_rows(h2, rwt_ref, rb_ref, tri_ref, oi_ref, of_ref, cnt_ref, run_ref):
    i = pl.program_id(0)

    @pl.when(i == 0)
    def _():
        run_ref[...] = jnp.zeros_like(run_ref)

    logits = lax.dot_general(rwt_ref[...].astype(BF16), h2.astype(BF16), (((1,), (1,)), ((), ())),
                             preferred_element_type=F32)
    aff = 1.0 / (1.0 + jnp.exp(-logits))
    sel = aff + rb_ref[...]
    s = [sel[e:e + 1, :] for e in range(N_EXPERTS)]
    a = [aff[e:e + 1, :] for e in range(N_EXPERTS)]

    def top2sum(v0, v1, v2, v3):
        hi01, lo01 = jnp.maximum(v0, v1), jnp.minimum(v0, v1)
        hi23, lo23 = jnp.maximum(v2, v3), jnp.minimum(v2, v3)
        return jnp.maximum(hi01, hi23) + jnp.maximum(jnp.minimum(hi01, hi23),
                                                     jnp.maximum(lo01, lo23))

    def argmax_first(vals):
        best, idx = vals[0], jnp.zeros(vals[0].shape, I32)
        for j in range(1, len(vals)):
            upd = vals[j] > best
            idx = jnp.where(upd, j, idx)
            best = jnp.where(upd, vals[j], best)
        return idx

    def pick(idx, vals):
        out = vals[-1]
        for j in range(len(vals) - 2, -1, -1):
            out = jnp.where(idx == j, vals[j], out)
        return out

    gi = argmax_first([top2sum(*s[4 * g:4 * g + 4]) for g in range(N_GROUPS)])
    sv = [pick(gi, [s[4 * g + j] for g in range(N_GROUPS)]) for j in range(EXPERTS_PER_GROUP)]
    av = [pick(gi, [a[4 * g + j] for g in range(N_GROUPS)]) for j in range(EXPERTS_PER_GROUP)]
    i1 = argmax_first(sv)
    i2 = argmax_first([jnp.where(i1 == j, -jnp.inf, sv[j]) for j in range(EXPERTS_PER_GROUP)])
    w1, w2 = pick(i1, av), pick(i2, av)
    wsum = w1 + w2
    w1, w2 = w1 / wsum, w2 / wsum
    lo, hi = jnp.minimum(i1, i2), jnp.maximum(i1, i2)
    w_lo = jnp.where(i1 < i2, w1, w2)
    w_hi = jnp.where(i1 < i2, w2, w1)
    pair = jnp.where(lo == 0, hi - 1, jnp.where(lo == 1, hi + 1, 5))
    bin_id = gi * N_PAIRS + pair

    tm = bin_id.shape[1]
    onehot = lax.broadcasted_iota(I32, (BIN_ROWS, tm), 0) == bin_id
    oh_f = jnp.where(onehot, 1.0, 0.0)
    before = jnp.dot(oh_f.astype(BF16), tri_ref[...], preferred_element_type=F32)
    run = run_ref[...]
    run_t = jnp.concatenate([run] * (tm // 128), axis=-1)
    rank = jnp.sum(oh_f * (before + run_t), axis=0, keepdims=True)
    run_new = run + jnp.sum(oh_f, axis=1, keepdims=True)
    run_ref[...] = run_new
    cnt_ref[...] = run_new.astype(I32)

    zi = jnp.zeros_like(bin_id)
    oi_ref[0] = jnp.concatenate([gi * 4 + lo, gi * 4 + hi, bin_id, rank.astype(I32), zi, zi, zi, zi], axis=0)
    zf = jnp.zeros_like(w_lo)
    of_ref[0] = jnp.concatenate([w_lo, w_hi, zf, zf, zf, zf, zf, zf], axis=0)


def _router(x1, mod, layer, mod_row0, mod_rows, router_wt, router_b, tm):
    rows = x1.shape[0]
    nt = rows // tm
    mblk = 8 if mod_rows == 1 else mod_rows
    mrow = mod_row0 // mblk
    tri = jnp.asarray(np.triu(np.ones((tm, tm), np.float32), 1), BF16)
    rb = jnp.broadcast_to(router_b.astype(F32)[:, None], (N_EXPERTS, tm))
    mod_spec = lambda c: pl.BlockSpec((1, mblk, D_MODEL), lambda i: (layer, mrow, c))
    oi, of, cnt = pl.pallas_call(
        functools.partial(_router_kernel, mod_rows=mod_rows),
        grid=(nt,),
        in_specs=[pl.BlockSpec((tm, D_MODEL), lambda i: (i, 0)), mod_spec(3), mod_spec(4),
                  pl.BlockSpec((N_EXPERTS, D_MODEL), lambda i: (0, 0)),
                  pl.BlockSpec((N_EXPERTS, tm), lambda i: (0, 0)),
                  pl.BlockSpec((tm, tm), lambda i: (0, 0))],
        out_specs=[pl.BlockSpec((1, 8, tm), lambda i: (i, 0, 0)),
                   pl.BlockSpec((1, 8, tm), lambda i: (i, 0, 0)),
                   pl.BlockSpec((BIN_ROWS, 128), lambda i: (0, 0))],
        out_shape=[jax.ShapeDtypeStruct((nt, 8, tm), I32),
                   jax.ShapeDtypeStruct((nt, 8, tm), F32),
                   jax.ShapeDtypeStruct((BIN_ROWS, 128), I32)],
        scratch_shapes=[pltpu.VMEM((BIN_ROWS, 128), F32)],
        compiler_params=_cparams(("arbitrary",)),
        name="router",
    )(x1, mod, mod, router_wt, rb, tri)
    return _unpack_route(oi, of, cnt)


def _unpack_route(oi, of, cnt):
    flat = lambda a, r: a[:, r, :].reshape(-1)
    return (flat(oi, 0), flat(oi, 1), flat(oi, 2), flat(oi, 3), flat(of, 0), flat(of, 1),
            cnt[:N_BINS, 0])


POST_TM = 512


def _post_mix(mixed, x, mod, layer, mod_row0, w_out_bf, ln_g, ln_b, router_wt, router_b):
    rows = x.shape[0]
    tm = POST_TM
    nt = rows // tm
    mrow = mod_row0 // 8
    tri = jnp.asarray(np.triu(np.ones((tm, tm), np.float32), 1), BF16)
    rb = jnp.broadcast_to(router_b.astype(F32)[:, None], (N_EXPERTS, tm))
    mod_spec = lambda c: pl.BlockSpec((1, 8, D_MODEL), lambda i: (layer, mrow, c))
    row_spec = pl.BlockSpec((tm, D_MODEL), lambda i: (i, 0))
    const2 = lambda a: pl.BlockSpec(a.shape, lambda i: (0, 0))
    x1, oi, of, cnt = pl.pallas_call(
        _post_mix_kernel,
        grid=(nt,),
        in_specs=[row_spec, row_spec, mod_spec(2), const2(w_out_bf), const2(ln_g), const2(ln_b),
                  mod_spec(3), mod_spec(4), const2(router_wt), const2(rb), const2(tri)],
        out_specs=[row_spec,
                   pl.BlockSpec((1, 8, tm), lambda i: (i, 0, 0)),
                   pl.BlockSpec((1, 8, tm), lambda i: (i, 0, 0)),
                   pl.BlockSpec((BIN_ROWS, 128), lambda i: (0, 0))],
        out_shape=[jax.ShapeDtypeStruct((rows, D_MODEL), F32),
                   jax.ShapeDtypeStruct((nt, 8, tm), I32),
                   jax.ShapeDtypeStruct((nt, 8, tm), F32),
                   jax.ShapeDtypeStruct((BIN_ROWS, 128), I32)],
        scratch_shapes=[pltpu.VMEM((BIN_ROWS, 128), F32)],
        compiler_params=_cparams(("arbitrary",)),
        name="post_mix",
    )(mixed, x, mod, w_out_bf, ln_g, ln_b, mod, mod, router_wt, rb, tri)
    return x1, _unpack_route(oi, of, cnt)


DMA_UNROLL = 8
ROW_GROUPS = MOE_TM // DMA_UNROLL


def _for_rows(n, fn):
    def body_u(j, carry):
        for u in range(DMA_UNROLL):
            fn(j, u)
        return carry
    lax.fori_loop(0, n // DMA_UNROLL, body_u, 0)

    def body_1(r, carry):
        fn(r // DMA_UNROLL, r % DMA_UNROLL)
        return carry
    lax.fori_loop((n // DMA_UNROLL) * DMA_UNROLL, n, body_1, 0)


def _wait_rows(n, buf, sem):
    p = ROW_GROUPS
    while p >= 1:
        @pl.when((n & (p * DMA_UNROLL)) != 0)
        def _(p=p):
            pltpu.make_async_copy(buf.at[pl.ds(0, p)], buf.at[pl.ds(0, p)], sem).wait()
        p //= 2
    p = DMA_UNROLL // 2
    while p >= 1:
        @pl.when((n & p) != 0)
        def _(p=p):
            pltpu.make_async_copy(buf.at[0, pl.ds(0, p)], buf.at[0, pl.ds(0, p)], sem).wait()
        p //= 2


def _moe_sorted_kernel(ea_ref, eb_ref, nvalid_ref, pos_ref,
                       x_hbm, sh_ref, sc_ref, gate_ref, rw_ref,
                       wga_ref, wua_ref, wda_ref, wgb_ref, wub_ref, wdb_ref,
                       lng_ref, lnb_ref,
                       out_hbm,
                       inv, xbuf, obuf, wg_a, wu_a, wd_a, wg_b, wu_b, wd_b, gsem, ssem):
    i = pl.program_id(0)
    nt = pl.num_programs(0)
    slot = i % 2
    n_tok = pos_ref.shape[0]

    def start_gather(t, dst_slot):
        def one(j, u):
            tok = inv[t * MOE_TM + j * DMA_UNROLL + u]
            pltpu.make_async_copy(x_hbm.at[pl.ds(tok, 1)], xbuf.at[dst_slot, j, pl.ds(u, 1)],
                                  gsem.at[dst_slot]).start()
        _for_rows(nvalid_ref[t], one)

    @pl.when(i == 0)
    def _():
        def body(j, carry):
            for u in range(DMA_UNROLL):
                t = j * DMA_UNROLL + u
                inv[pos_ref[t]] = t
            return carry
        lax.fori_loop(0, n_tok // DMA_UNROLL, body, 0)
        xbuf[...] = jnp.zeros_like(xbuf)
        start_gather(0, 0)

    @pl.when(i >= 2)
    def _():
        _wait_rows(nvalid_ref[jnp.maximum(i - 2, 0)], obuf.at[slot], ssem.at[slot])

    @pl.when(nvalid_ref[i] > 0)
    def _():
        _wait_rows(nvalid_ref[i], xbuf.at[slot], gsem.at[slot])

        @pl.when(i + 1 < nt)
        def _():
            start_gather(jnp.minimum(i + 1, nt - 1), 1 - slot)

        @pl.when(jnp.logical_or(i == 0, ea_ref[i] != ea_ref[jnp.maximum(i - 1, 0)]))
        def _():
            wg_a[...] = wga_ref[0, 0].astype(BF16)
            wu_a[...] = wua_ref[0, 0].astype(BF16)
            wd_a[...] = wda_ref[0, 0].astype(BF16)

        @pl.when(jnp.logical_or(i == 0, eb_ref[i] != eb_ref[jnp.maximum(i - 1, 0)]))
        def _():
            wg_b[...] = wgb_ref[0, 0].astype(BF16)
            wu_b[...] = wub_ref[0, 0].astype(BF16)
            wd_b[...] = wdb_ref[0, 0].astype(BF16)

        x = xbuf[slot].reshape(MOE_TM, D_MODEL)
        h2 = (x * (1.0 + sc_ref[0][:1]) + sh_ref[0][:1]).astype(BF16)

        def expert(wg, wu, wd):
            a = _silu(jnp.dot(h2, wg[...], preferred_element_type=F32)) * \
                jnp.dot(h2, wu[...], preferred_element_type=F32)
            return jnp.dot(a.astype(BF16), wd[...], preferred_element_type=F32)

        aff = 1.0 / (1.0 + jnp.exp(-jnp.dot(h2, rw_ref[...], preferred_element_type=F32)))
        lane = lax.broadcasted_iota(I32, aff.shape, 1)
        a_lo = jnp.sum(jnp.where(lane == ea_ref[i], aff, 0.0), axis=-1, keepdims=True)
        a_hi = jnp.sum(jnp.where(lane == eb_ref[i], aff, 0.0), axis=-1, keepdims=True)
        a_sum = a_lo + a_hi
        y = (a_lo / a_sum) * expert(wg_a, wu_a, wd_a) + (a_hi / a_sum) * expert(wg_b, wu_b, wd_b)
        out = _ln(DN_ALPHA * x + (1.0 + gate_ref[0][:1]) * y, lng_ref[...], lnb_ref[...])
        obuf[slot] = out.reshape(ROW_GROUPS, DMA_UNROLL, D_MODEL)

        def scatter_one(j, u):
            tok = inv[i * MOE_TM + j * DMA_UNROLL + u]
            pltpu.make_async_copy(obuf.at[slot, j, pl.ds(u, 1)], out_hbm.at[pl.ds(tok, 1)],
                                  ssem.at[slot]).start()
        _for_rows(nvalid_ref[i], scatter_one)

    @pl.when(i == nt - 1)
    def _():
        _wait_rows(nvalid_ref[jnp.maximum(i - 1, 0)], obuf.at[1 - slot], ssem.at[1 - slot])
        _wait_rows(nvalid_ref[i], obuf.at[slot], ssem.at[slot])


def _moe_sorted(x1, mod, layer, mod_row0, route, rw_pad, wg, wu, wd, ln_g, ln_b):
    _, _, bin_id, rank, _, _, counts = route
    rows = x1.shape[0]
    nt = rows // MOE_TM + N_BINS
    ns = nt * MOE_TM
    tiles = (counts + MOE_TM - 1) // MOE_TM
    tile_end = jnp.cumsum(tiles)
    row_start = (tile_end - tiles) * MOE_TM
    pos = row_start[bin_id] + rank
    total = tile_end[-1]
    t = jnp.arange(nt, dtype=I32)
    tile_bin = jnp.searchsorted(tile_end, jnp.minimum(t, total - 1), side="right").astype(I32)
    tile_bin = jnp.clip(tile_bin, 0, N_BINS - 1)
    bin_lo = jnp.asarray([g * 4 + _PAIR_LO[p] for g in range(N_GROUPS) for p in range(N_PAIRS)], I32)
    bin_hi = jnp.asarray([g * 4 + _PAIR_HI[p] for g in range(N_GROUPS) for p in range(N_PAIRS)], I32)
    tile_ea, tile_eb = bin_lo[tile_bin], bin_hi[tile_bin]
    nvalid = jnp.clip(counts[tile_bin] - (t - (tile_end - tiles)[tile_bin]) * MOE_TM, 0, MOE_TM)
    nvalid = jnp.where(t < total, nvalid, 0).astype(I32)
    mrow = mod_row0 // 8
    mod_spec = lambda c: pl.BlockSpec((1, 8, D_MODEL), lambda i, *_: (layer, mrow, c))
    w_spec = lambda shape, ref_idx: pl.BlockSpec(
        (1, 1) + shape, lambda i, ea_r, eb_r, *_: (layer, (ea_r, eb_r)[ref_idx][i], 0, 0))
    any_spec = pl.BlockSpec(memory_space=pl.ANY)
    w_bf = lambda shape: pltpu.VMEM(shape, BF16)
    grid_spec = pltpu.PrefetchScalarGridSpec(
        num_scalar_prefetch=4,
        grid=(nt,),
        in_specs=[any_spec,
                  mod_spec(3), mod_spec(4), mod_spec(5),
                  pl.BlockSpec((D_MODEL, 128), lambda i, *_: (0, 0)),
                  w_spec((D_MODEL, D_FF), 0), w_spec((D_MODEL, D_FF), 0), w_spec((D_FF, D_MODEL), 0),
                  w_spec((D_MODEL, D_FF), 1), w_spec((D_MODEL, D_FF), 1), w_spec((D_FF, D_MODEL), 1),
                  pl.BlockSpec((1, D_MODEL), lambda i, *_: (0, 0)),
                  pl.BlockSpec((1, D_MODEL), lambda i, *_: (0, 0))],
        out_specs=any_spec,
        scratch_shapes=[pltpu.SMEM((ns,), I32),
                        pltpu.VMEM((2, ROW_GROUPS, DMA_UNROLL, D_MODEL), F32),
                        pltpu.VMEM((2, ROW_GROUPS, DMA_UNROLL, D_MODEL), F32),
                        w_bf((D_MODEL, D_FF)), w_bf((D_MODEL, D_FF)), w_bf((D_FF, D_MODEL)),
                        w_bf((D_MODEL, D_FF)), w_bf((D_MODEL, D_FF)), w_bf((D_FF, D_MODEL)),
                        pltpu.SemaphoreType.DMA((2,)),
                        pltpu.SemaphoreType.DMA((2,))],
    )
    return pl.pallas_call(
        _moe_sorted_kernel,
        grid_spec=grid_spec,
        out_shape=jax.ShapeDtypeStruct((rows, D_MODEL), F32),
        compiler_params=_cparams(("arbitrary",)),
        name="moe_sorted",
    )(tile_ea, tile_eb, nvalid, pos, x1, mod, mod, mod, rw_pad,
      wg, wu, wd, wg, wu, wd, ln_g, ln_b)


def _moe_dense_kernel(x_ref, dw_ref, sh_ref, sc_ref, gate_ref, wg_ref, wu_ref, wd_ref,
                      lng_ref, lnb_ref, o_ref, acc):
    e = pl.program_id(0)

    @pl.when(e == 0)
    def _():
        acc[...] = jnp.zeros_like(acc)

    x = x_ref[...]
    h2 = (x * (1.0 + sc_ref[0]) + sh_ref[0]).astype(BF16)
    a = _silu(jnp.dot(h2, wg_ref[0, 0].astype(BF16), preferred_element_type=F32)) * \
        jnp.dot(h2, wu_ref[0, 0].astype(BF16), preferred_element_type=F32)
    y = jnp.dot(a.astype(BF16), wd_ref[0, 0].astype(BF16), preferred_element_type=F32)
    acc[...] = acc[...] + dw_ref[0][:, :1] * y

    @pl.when(e == pl.num_programs(0) - 1)
    def _():
        o_ref[...] = _ln(DN_ALPHA * x + (1.0 + gate_ref[0]) * acc[...], lng_ref[...], lnb_ref[...])


def _moe_dense(x1, mod, layer, route, wg, wu, wd, ln_g, ln_b):
    ea, eb, _, _, w_lo, w_hi, _ = route
    nb = x1.shape[0]
    eids = jnp.arange(N_EXPERTS, dtype=I32)[:, None]
    dw = jnp.where(eids == ea[None, :], w_lo[None, :], 0.0) + \
        jnp.where(eids == eb[None, :], w_hi[None, :], 0.0)
    dw = jnp.broadcast_to(dw[:, :, None], (N_EXPERTS, nb, 128))
    mod_spec = lambda c: pl.BlockSpec((1, nb, D_MODEL), lambda e: (layer, 0, c))
    const2 = lambda a: pl.BlockSpec(a.shape, lambda e: (0, 0))
    return pl.pallas_call(
        _moe_dense_kernel,
        grid=(N_EXPERTS,),
        in_specs=[const2(x1), pl.BlockSpec((1, nb, 128), lambda e: (e, 0, 0)),
                  mod_spec(3), mod_spec(4), mod_spec(5),
                  pl.BlockSpec((1, 1, D_MODEL, D_FF), lambda e: (layer, e, 0, 0)),
                  pl.BlockSpec((1, 1, D_MODEL, D_FF), lambda e: (layer, e, 0, 0)),
                  pl.BlockSpec((1, 1, D_FF, D_MODEL), lambda e: (layer, e, 0, 0)),
                  const2(ln_g), const2(ln_b)],
        out_specs=const2(x1),
        out_shape=jax.ShapeDtypeStruct(x1.shape, F32),
        scratch_shapes=[pltpu.VMEM(x1.shape, F32)],
        compiler_params=_cparams(("arbitrary",)),
        name="moe_dense",
    )(x1, dw, mod, mod, mod, wg, wu, wd, ln_g, ln_b)


def kernel(x_prompt, x_sample, state_ret, cache_swa_k, cache_swa_v, c_prompt, c_sample, w_in, w_out, ret_gn_gain, swa_sinks, rel_bias_table, ada_w, ada_b, ln1_g, ln1_b, ln2_g, ln2_b, router_w, router_b, exp_w_gate, exp_w_up, exp_w_down):
    seq = x_prompt.shape[1]
    nb = x_sample.shape[0]
    past_len = 16384
    assert x_prompt.shape[0] == 1 and x_sample.shape[1] == 1

    c_all = jnp.concatenate([c_sample, c_prompt, jnp.zeros((7, D_MODEL), F32)], axis=0)
    mod = _ada(c_all, ada_w, ada_b)
    prompt_row = nb

    bias_tabs = _bias_tables(rel_bias_table.astype(F32))
    gl, decay, xi, zeta, gamma1 = _ret_consts()
    rope_p = _rope_tables(jnp.arange(seq, dtype=I32))
    rope_s = _rope_tables(jnp.full((nb,), past_len, I32))
    router_wt = router_w.astype(F32).T
    rw_pad = jnp.pad(router_w.astype(BF16), ((0, 0), (0, 128 - N_EXPERTS)))
    zeta_p = jnp.tile(zeta, (INPROJ_TM // CHUNK, 1))
    zeta_s = jnp.ones((nb, RET_W), F32)

    xp = x_prompt.reshape(seq, D_MODEL)
    xs = x_sample.reshape(nb, D_MODEL)
    st_p, k_p, v_p, st_s, k_s, v_s = [], [], [], [], [], []
    for l in range(DEPTH):
        order = jnp.asarray(_SWA_HEAD_ORDER)
        sq_lo, sq_hi = 4 * RET_W, 4 * RET_W + SWA_QW
        sq_cols = w_in[l][:, sq_lo:sq_hi].reshape(D_MODEL, N_SWA_HEADS, HEAD_DIM)[:, order, :]
        w_in_bf = jnp.concatenate([w_in[l][:, :sq_lo], sq_cols.reshape(D_MODEL, SWA_QW),
                                   w_in[l][:, sq_hi:]], axis=1).astype(BF16)
        w_out_bf = w_out[l].astype(BF16)
        swa_rows = w_out[l][RET_W:].reshape(N_SWA_HEADS, HEAD_DIM, D_MODEL)[order]
        w_out_prompt = jnp.concatenate([w_out[l][:RET_W], swa_rows.reshape(SWA_QW, D_MODEL)],
                                       axis=0).astype(BF16)
        experts = (exp_w_gate.astype(F32), exp_w_up.astype(F32), exp_w_down.astype(F32))
        gain = ret_gn_gain[l].astype(F32)
        sinks = swa_sinks[l].astype(F32)
        row = lambda a: a[l].astype(F32).reshape(1, D_MODEL)

        proj = _inproj(xp, mod, l, prompt_row, 1, w_in_bf, rope_p, zeta_p, INPROJ_TM)
        mixed, st = _mix_prompt(proj, (gl, decay, xi), bias_tabs, sinks, gain.reshape(1, RET_W))
        skv = proj[6]
        st_p.append(st.reshape(1, N_RET_HEADS, HEAD_DIM, HEAD_DIM))
        k_p.append(skv[seq - CHUNK:, :SWA_KVW].reshape(1, CHUNK, N_KV_HEADS, HEAD_DIM))
        v_p.append(skv[seq - CHUNK:, SWA_KVW:].reshape(1, CHUNK, N_KV_HEADS, HEAD_DIM))
        x1, route = _post_mix(mixed, xp, mod, l, prompt_row, w_out_prompt, row(ln1_g), row(ln1_b),
                              router_wt, router_b)
        xp = _moe_sorted(x1, mod, l, prompt_row, route, rw_pad, *experts, row(ln2_g), row(ln2_b))

        proj = _inproj(xs, mod, l, 0, nb, w_in_bf, rope_s, zeta_s, nb)
        x1, nst, nk, nv = _mix_sample(proj, xs, mod, l, w_out_bf, gamma1, bias_tabs[1], sinks, gain,
                                      row(ln1_g), row(ln1_b), state_ret[l].astype(F32),
                                      cache_swa_k[l], cache_swa_v[l])
        st_s.append(nst)
        k_s.append(nk)
        v_s.append(nv)
        route = _router(x1, mod, l, 0, nb, router_wt, router_b, nb)
        xs = _moe_dense(x1, mod, l, route, *experts, row(ln2_g), row(ln2_b))

    return (xp.reshape(1, seq, D_MODEL), xs.reshape(nb, 1, D_MODEL),
            jnp.stack(st_p), jnp.stack(k_p), jnp.stack(v_p),
            jnp.stack(st_s), jnp.stack(k_s), jnp.stack(v_s))
```

```python
import functools
import math

import numpy as np
import jax
import jax.numpy as jnp
from jax import lax
from jax.experimental import pallas as pl
from jax.experimental.pallas import tpu as pltpu

F32 = jnp.float32
BF16 = jnp.bfloat16
I32 = jnp.int32

D_MODEL = 1024
DEPTH = 2
HEAD_DIM = 64
N_RET_HEADS = 8
N_SWA_HEADS = 8
N_KV_HEADS = 2
GQA_GROUP = N_SWA_HEADS // N_KV_HEADS
RET_W = N_RET_HEADS * HEAD_DIM
SWA_QW = N_SWA_HEADS * HEAD_DIM
SWA_KVW = N_KV_HEADS * HEAD_DIM
PROJ_W = 4 * RET_W + SWA_QW + 2 * SWA_KVW
CHUNK = 128
ROPE_BASE = 10000.0
N_BUCKETS = 32
MAX_DISTANCE = 128
N_EXPERTS = 16
N_GROUPS = 4
EXPERTS_PER_GROUP = 4
D_FF = 512
LN_EPS = 1e-5
DN_ALPHA = (2 * DEPTH) ** 0.25
QK_SCALE = HEAD_DIM ** -0.5
NEG_BIG = -1e30

N_PAIRS = 6
N_BINS = N_GROUPS * N_PAIRS
BIN_ROWS = 32
MOE_TM = 256
INPROJ_TM = 512
MIX_SUB = 2
SPLIT_W = 128 * N_RET_HEADS
_SWA_HEAD_ORDER = (0, 4, 1, 5, 2, 6, 3, 7)
VMEM_LIMIT = 56 * 1024 * 1024

_PAIR_LO = (0, 0, 0, 1, 1, 2)
_PAIR_HI = (1, 2, 3, 2, 3, 3)


def _cparams(sem):
    return pltpu.CompilerParams(dimension_semantics=sem, vmem_limit_bytes=VMEM_LIMIT)


def _ln(v, g, b):
    mu = jnp.mean(v, axis=-1, keepdims=True)
    c = v - mu
    var = jnp.mean(c * c, axis=-1, keepdims=True)
    return c * lax.rsqrt(var + LN_EPS) * g + b


def _silu(v):
    return v * (1.0 / (1.0 + jnp.exp(-v)))


def _ada_kernel(c_ref, w_ref, b_ref, o_ref):
    o_ref[0] = jnp.dot(c_ref[...].astype(BF16), w_ref[0].astype(BF16),
                       preferred_element_type=F32) + b_ref[0]


def _ada(c_all, ada_w, ada_b):
    rows = c_all.shape[0]
    nt = 6 * D_MODEL // 1024
    return pl.pallas_call(
        _ada_kernel,
        grid=(DEPTH, nt),
        in_specs=[pl.BlockSpec((rows, D_MODEL), lambda l, j: (0, 0)),
                  pl.BlockSpec((1, D_MODEL, 1024), lambda l, j: (l, 0, j)),
                  pl.BlockSpec((1, 1, 1024), lambda l, j: (l, 0, j))],
        out_specs=pl.BlockSpec((1, rows, 1024), lambda l, j: (l, 0, j)),
        out_shape=jax.ShapeDtypeStruct((DEPTH, rows, 6 * D_MODEL), F32),
        compiler_params=_cparams(("arbitrary", "arbitrary")),
        name="ada",
    )(c_all, ada_w, ada_b.reshape(DEPTH, 1, 6 * D_MODEL))


def _bias_kernel(tab_ref, bkt_ref, fold_ref, row_ref):
    bkt = bkt_ref[...]
    rows = lax.broadcasted_iota(I32, (CHUNK, CHUNK), 0)
    cols = lax.broadcasted_iota(I32, (CHUNK, CHUNK), 1)
    from_prev = cols > rows
    for h in range(N_SWA_HEADS):
        acc = jnp.zeros(bkt.shape, F32)
        for b in range(N_BUCKETS):
            acc = jnp.where(bkt == b, tab_ref[b, h], acc)
        own = acc[:, CHUNK:]
        fold_ref[0, h] = jnp.where(from_prev, NEG_BIG, own)
        fold_ref[1, h] = jnp.where(from_prev, acc[:, :CHUNK], own)
        row_ref[h:h + 1, :] = own[CHUNK - 1:CHUNK, :]


def _t5_bucket(rel):
    max_exact = N_BUCKETS // 2
    relf = jnp.maximum(rel, 1).astype(F32)
    large = max_exact + (jnp.log(relf / max_exact) / math.log(MAX_DISTANCE / max_exact)
                         * (N_BUCKETS - max_exact)).astype(I32)
    large = jnp.minimum(large, N_BUCKETS - 1)
    return jnp.where(rel < max_exact, rel, large)


def _bias_tables(rel_bias_table):
    qi = jnp.arange(CHUNK)
    si = jnp.arange(2 * CHUNK)
    rel = CHUNK + qi[:, None] - si[None, :]
    bkt = _t5_bucket(jnp.maximum(rel, 0)).astype(I32)
    return pl.pallas_call(
        _bias_kernel,
        in_specs=[pl.BlockSpec(memory_space=pltpu.SMEM),
                  pl.BlockSpec((CHUNK, 2 * CHUNK), lambda: (0, 0))],
        out_specs=[pl.BlockSpec((2, N_SWA_HEADS, CHUNK, CHUNK), lambda: (0, 0, 0, 0)),
                   pl.BlockSpec((N_SWA_HEADS, CHUNK), lambda: (0, 0))],
        out_shape=[jax.ShapeDtypeStruct((2, N_SWA_HEADS, CHUNK, CHUNK), F32),
                   jax.ShapeDtypeStruct((N_SWA_HEADS, CHUNK), F32)],
        name="t5_bias",
    )(rel_bias_table, bkt)


def _rotary(v, cos, s_lo, s_hi):
    outs = []
    for j in range(RET_W // 128):
        blk = v[:, j * 128:(j + 1) * 128]
        outs.append(blk * cos + pltpu.roll(blk, 96, 1) * s_lo + pltpu.roll(blk, 32, 1) * s_hi)
    return jnp.concatenate(outs, axis=-1)


def _inproj_kernel(x_ref, sh_ref, sc_ref, w_ref, cs_ref, zeta_ref,
                   q_ref, k_ref, kz_ref, v_ref, g_ref, sq_ref, skv_ref, *, mod_rows):
    sh = sh_ref[0][:mod_rows]
    sc = sc_ref[0][:mod_rows]
    h = (x_ref[...] * (1.0 + sc) + sh).astype(BF16)
    cs = cs_ref[...]
    first = (lax.broadcasted_iota(I32, cs.shape, 1) % HEAD_DIM) < (HEAD_DIM // 2)
    cos = jnp.where(first, cs, pltpu.roll(cs, 32, 1))
    s_lo = jnp.where(first, -pltpu.roll(cs, 96, 1), 0.0)
    s_hi = jnp.where(first, 0.0, cs)

    def proj(lo, hi):
        return jnp.dot(h, w_ref[:, lo:hi], preferred_element_type=F32)

    left = lax.broadcasted_iota(I32, cs.shape, 1) < HEAD_DIM

    def store_split(ref, val):
        for b in range(RET_W // 128):
            blk = val[:, b * 128:(b + 1) * 128]
            ref[:, (2 * b) * 128:(2 * b + 1) * 128] = jnp.where(left, blk, 0.0).astype(BF16)
            ref[:, (2 * b + 1) * 128:(2 * b + 2) * 128] = jnp.where(left, 0.0, blk).astype(BF16)

    store_split(q_ref, _rotary(proj(0, RET_W), cos, s_lo, s_hi))
    k = _rotary(proj(RET_W, 2 * RET_W), cos, s_lo, s_hi) * QK_SCALE
    store_split(k_ref, k)
    store_split(kz_ref, k * zeta_ref[...])
    store_split(v_ref, proj(2 * RET_W, 3 * RET_W))
    g_ref[...] = proj(3 * RET_W, 4 * RET_W)
    store_split(sq_ref, proj(4 * RET_W, 4 * RET_W + SWA_QW) * QK_SCALE)
    skv_ref[...] = proj(4 * RET_W + SWA_QW, PROJ_W)


def _inproj(x, mod, layer, mod_row0, mod_rows, w_in_bf, rope, zeta_tile, tm):
    rows = x.shape[0]
    mblk = 8 if mod_rows == 1 else mod_rows
    mrow = mod_row0 // mblk
    row_spec = lambda w: pl.BlockSpec((tm, w), lambda i: (i, 0))
    mod_spec = lambda c: pl.BlockSpec((1, mblk, D_MODEL), lambda i: (layer, mrow, c))
    return pl.pallas_call(
        functools.partial(_inproj_kernel, mod_rows=mod_rows),
        grid=(rows // tm,),
        in_specs=[row_spec(D_MODEL), mod_spec(0), mod_spec(1),
                  pl.BlockSpec((D_MODEL, PROJ_W), lambda i: (0, 0)),
                  row_spec(128),
                  pl.BlockSpec((tm, RET_W), lambda i: (0, 0))],
        out_specs=[row_spec(SPLIT_W), row_spec(SPLIT_W), row_spec(SPLIT_W), row_spec(SPLIT_W),
                   row_spec(RET_W), row_spec(SPLIT_W), row_spec(2 * SWA_KVW)],
        out_shape=[jax.ShapeDtypeStruct((rows, SPLIT_W), BF16),
                   jax.ShapeDtypeStruct((rows, SPLIT_W), BF16),
                   jax.ShapeDtypeStruct((rows, SPLIT_W), BF16),
                   jax.ShapeDtypeStruct((rows, SPLIT_W), BF16),
                   jax.ShapeDtypeStruct((rows, RET_W), F32),
                   jax.ShapeDtypeStruct((rows, SPLIT_W), BF16),
                   jax.ShapeDtypeStruct((rows, 2 * SWA_KVW), F32)],
        compiler_params=_cparams(("arbitrary",)),
        name="inproj",
    )(x, mod, mod, w_in_bf, rope, zeta_tile)


def _rope_tables(pos):
    half = HEAD_DIM // 2
    inv = ROPE_BASE ** (-jnp.arange(half, dtype=F32) / half)
    ang = pos.astype(F32)[:, None] * inv[None, :]
    cos, sin = jnp.cos(ang), jnp.sin(ang)
    return jnp.concatenate([cos, sin, cos, sin], axis=-1)


def _mix_prompt_kernel(gl_ref, sink_ref,
                       q_ref, k_ref, kz_ref, v_ref, g_ref, sq_ref, kvc_ref, kvp_ref,
                       decay_ref, xi_ref, bias_ref, gain_ref,
                       mixed_ref, st_ref, state):
    i = pl.program_id(0)

    @pl.when(i == 0)
    def _():
        state[...] = jnp.zeros_like(state)

    gain = gain_ref[...]
    left = lax.broadcasted_iota(I32, (CHUNK, 128), 1) < HEAD_DIM
    from_prev = (lax.broadcasted_iota(I32, (CHUNK, CHUNK), 1) >
                 lax.broadcasted_iota(I32, (CHUNK, CHUNK), 0))
    inv_dim = 1.0 / HEAD_DIM

    for sub in range(MIX_SUB):
        rows = slice(sub * CHUNK, (sub + 1) * CHUNK)

        heads = range(N_RET_HEADS)
        hsl = [slice(h * 128, (h + 1) * 128) for h in heads]
        scores_r = [lax.dot_general(q_ref[rows, hsl[h]], k_ref[rows, hsl[h]], (((1,), (1,)), ((), ())),
                                    preferred_element_type=F32) for h in heads]
        s_olds = [state[h] for h in heads]
        inter = [jnp.dot(q_ref[rows, hsl[h]], s_olds[h].astype(BF16), preferred_element_type=F32)
                 for h in heads]
        upds = [lax.dot_general(kz_ref[rows, hsl[h]], v_ref[rows, hsl[h]], (((0,), (0,)), ((), ())),
                                preferred_element_type=F32) for h in heads]
        for h in heads:
            state[h] = gl_ref[h] * s_olds[h] + upds[h]
        probs_r = [(scores_r[h] * decay_ref[h]).astype(BF16) for h in heads]
        intra = [jnp.dot(probs_r[h], v_ref[rows, hsl[h]], preferred_element_type=F32) for h in heads]

        owns = [left if h % 2 == 0 else jnp.logical_not(left) for h in heads]
        outs = [intra[h] + inter[h] * xi_ref[:, hsl[h]] for h in heads]
        mus = [jnp.sum(outs[h], axis=-1, keepdims=True) * inv_dim for h in heads]
        cen = [jnp.where(owns[h], outs[h] - mus[h], 0.0) for h in heads]
        var = [jnp.sum(cen[h] * cen[h], axis=-1, keepdims=True) * inv_dim for h in heads]
        nrm = [cen[h] * lax.rsqrt(var[h] + LN_EPS) for h in heads]
        ret_blocks = [nrm[2 * b] + nrm[2 * b + 1] for b in range(N_RET_HEADS // 2)]
        ret_o = jnp.concatenate(ret_blocks, axis=-1) * gain * _silu(g_ref[rows, :])

        kvc = kvc_ref[rows, :]
        kvp = kvp_ref[...] if sub == 0 else kvc_ref[(sub - 1) * CHUNK:sub * CHUNK, :]
        kcat = jnp.concatenate([kvp[:, :SWA_KVW], kvc[:, :SWA_KVW]], axis=0)
        vcat = jnp.concatenate([kvp[:, SWA_KVW:], kvc[:, SWA_KVW:]], axis=0)
        left2 = jnp.concatenate([left, left], axis=0)
        table = jnp.minimum(i, 1) if sub == 0 else 1
        swa_blocks = [None] * GQA_GROUP
        kks, vvs = [], []
        for kv in range(N_KV_HEADS):
            own2 = left2 if kv == 0 else jnp.logical_not(left2)
            kks.append(jnp.where(own2, kcat, 0.0).astype(BF16))
            vvs.append(jnp.where(own2, vcat, 0.0).astype(BF16))
        scores = []
        for hh in range(N_SWA_HEADS):
            kv, j = divmod(hh, GQA_GROUP)
            blk = 2 * j + kv
            qh = sq_ref[rows, blk * 128:(blk + 1) * 128]
            scores.append(lax.dot_general(qh, kks[kv], (((1,), (1,)), ((), ())),
                                          preferred_element_type=F32))
        probs = []
        for hh in range(N_SWA_HEADS):
            s2 = scores[hh]
            s = jnp.where(from_prev, s2[:, :CHUNK], s2[:, CHUNK:]) + bias_ref[table, hh]
            sink = sink_ref[hh]
            m = jnp.maximum(jnp.max(s, axis=-1, keepdims=True), sink)
            p = jnp.exp(s - m)
            den = jnp.sum(p, axis=-1, keepdims=True) + jnp.exp(sink - m)
            pn = p / den
            p2 = jnp.concatenate([jnp.where(from_prev, pn, 0.0), jnp.where(from_prev, 0.0, pn)], axis=-1)
            probs.append(p2.astype(BF16))
        for hh in range(N_SWA_HEADS):
            kv, j = divmod(hh, GQA_GROUP)
            o = jnp.dot(probs[hh], vvs[kv], preferred_element_type=F32)
            swa_blocks[j] = o if kv == 0 else swa_blocks[j] + o
        swa_o = jnp.concatenate(swa_blocks, axis=-1)

        mixed_ref[rows, :] = jnp.concatenate([ret_o, swa_o], axis=-1).astype(BF16)

    @pl.when(i == pl.num_programs(0) - 1)
    def _():
        st_ref[...] = state[...]


def _mix_prompt(proj, ret_consts, bias_tabs, sinks, gain):
    q, k, kz, v, g, sq, skv = proj
    rows = q.shape[0]
    tm = MIX_SUB * CHUNK
    gl, decay, xi = ret_consts
    bias_fold = bias_tabs[0]
    row_spec = lambda w: pl.BlockSpec((tm, w), lambda i: (i, 0))
    const2 = lambda a: pl.BlockSpec(a.shape, lambda i: (0, 0))
    const3 = lambda a: pl.BlockSpec(a.shape, lambda i: (0, 0, 0))
    smem = pl.BlockSpec(memory_space=pltpu.SMEM)
    mixed, st_full = pl.pallas_call(
        _mix_prompt_kernel,
        grid=(rows // tm,),
        in_specs=[smem, smem,
                  row_spec(SPLIT_W), row_spec(SPLIT_W), row_spec(SPLIT_W), row_spec(SPLIT_W), row_spec(RET_W),
                  row_spec(SPLIT_W), row_spec(2 * SWA_KVW),
                  pl.BlockSpec((CHUNK, 2 * SWA_KVW), lambda i: (jnp.maximum(MIX_SUB * i - 1, 0), 0)),
                  const3(decay), const2(xi),
                  pl.BlockSpec(bias_fold.shape, lambda i: (0, 0, 0, 0)),
                  const2(gain)],
        out_specs=[row_spec(D_MODEL),
                   pl.BlockSpec((N_RET_HEADS, 128, 128), lambda i: (0, 0, 0))],
        out_shape=[jax.ShapeDtypeStruct((rows, D_MODEL), BF16),
                   jax.ShapeDtypeStruct((N_RET_HEADS, 128, 128), F32)],
        scratch_shapes=[pltpu.VMEM((N_RET_HEADS, 128, 128), F32)],
        compiler_params=_cparams(("arbitrary",)),
        name="mix_prompt",
    )(gl, sinks, q, k, kz, v, g, sq, skv, skv, decay, xi, bias_fold, gain)
    lo, hi = slice(0, HEAD_DIM), slice(HEAD_DIM, 128)
    st = jnp.stack([st_full[h, lo, lo] if h % 2 == 0 else st_full[h, hi, hi] for h in range(N_RET_HEADS)])
    return mixed, st


def _ret_consts():
    lg = jnp.log(1.0 - 2.0 ** (-5.0 - jnp.arange(N_RET_HEADS, dtype=F32)))
    idx = jnp.arange(CHUNK, dtype=F32)
    diff = idx[:, None] - idx[None, :]
    decay = jnp.where(diff >= 0, jnp.exp(jnp.maximum(diff, 0.0)[None] * lg[:, None, None]), 0.0)
    xi = jnp.exp((idx + 1.0)[:, None] * lg[None, :])
    zeta = jnp.exp((CHUNK - 1.0 - idx)[:, None] * lg[None, :])
    gl = jnp.exp(CHUNK * lg)
    return gl, decay, jnp.repeat(xi, 128, axis=1), jnp.repeat(zeta, HEAD_DIM, axis=1), jnp.exp(1.0 * lg)


SAMPLE_BB = 8


def _mix_sample_kernel(sink_ref, gam_ref,
                       qt_ref, kt_ref, v3_ref, sq3_ref, knew_ref, vnew_ref,
                       st_ref, ck_ref, cv_ref, g3_ref, x_ref, gate_ref,
                       wout_ref, brow_ref, gain_ref, lng_ref, lnb_ref,
                       x1_ref, nst_ref, nk_ref, nv_ref, ret_scr, swa_scr):
    i = pl.program_id(0)
    row = lax.broadcasted_iota(I32, (N_SWA_HEADS, 2 * HEAD_DIM), 0)
    lane = lax.broadcasted_iota(I32, (N_SWA_HEADS, 2 * HEAD_DIM), 1)
    own_half = (row // GQA_GROUP) == (lane // HEAD_DIM)
    sink_col = jnp.concatenate(
        [jnp.full((1, 1), sink_ref[hh], F32) for hh in range(N_SWA_HEADS)], axis=0)
    brow = brow_ref[...]
    qt = qt_ref[0]
    kt = kt_ref[0]
    gam = gam_ref[...]
    seqs = range(SAMPLE_BB)
    state_shape = (N_RET_HEADS, HEAD_DIM, HEAD_DIM)

    s_old = [st_ref[b] for b in seqs]
    v_full = [jnp.concatenate([jnp.broadcast_to(v3_ref[b, h:h + 1, :], (HEAD_DIM, HEAD_DIM))
                               for h in range(N_RET_HEADS)], axis=0) for b in seqs]
    s_new = [gam * s_old[b] + kt[:, b:b + 1] * v_full[b] for b in seqs]
    for b in seqs:
        nst_ref[b] = s_new[b]
    outs = [jnp.sum((qt[:, b:b + 1] * s_new[b]).reshape(state_shape), axis=1) for b in seqs]
    for b in seqs:
        ret_scr[i * SAMPLE_BB + b] = outs[b]

    kk = [jnp.concatenate([ck_ref[b, 1:, :], knew_ref[b:b + 1, :]], axis=0) for b in seqs]
    vv = [jnp.concatenate([cv_ref[b, 1:, :], vnew_ref[b:b + 1, :]], axis=0) for b in seqs]
    for b in seqs:
        nk_ref[b] = kk[b]
        nv_ref[b] = vv[b]
    qblk = [jnp.where(own_half, jnp.concatenate([sq3_ref[b], sq3_ref[b]], axis=-1), 0.0).astype(BF16)
            for b in seqs]
    s = [lax.dot_general(qblk[b], kk[b].astype(BF16), (((1,), (1,)), ((), ())),
                         preferred_element_type=F32) + brow for b in seqs]
    m = [jnp.maximum(jnp.max(s[b], axis=-1, keepdims=True), sink_col) for b in seqs]
    p = [jnp.exp(s[b] - m[b]) for b in seqs]
    den = [jnp.sum(p[b], axis=-1, keepdims=True) + jnp.exp(sink_col - m[b]) for b in seqs]
    pn = [(p[b] / den[b]).astype(BF16) for b in seqs]
    o = [jnp.dot(pn[b], vv[b].astype(BF16), preferred_element_type=F32) for b in seqs]
    for b in seqs:
        swa_scr[i * SAMPLE_BB + b] = jnp.where(own_half[:, :HEAD_DIM], o[b][:, :HEAD_DIM], o[b][:, HEAD_DIM:])

    @pl.when(i == pl.num_programs(0) - 1)
    def _():
        y = jnp.zeros(x_ref.shape, F32)
        for h in range(N_RET_HEADS):
            o = ret_scr[:, h, :]
            mu = jnp.mean(o, axis=-1, keepdims=True)
            c = o - mu
            var = jnp.mean(c * c, axis=-1, keepdims=True)
            r = c * lax.rsqrt(var + LN_EPS) * gain_ref[h:h + 1, :] * _silu(g3_ref[h])
            y = y + jnp.dot(r.astype(BF16), wout_ref[h * HEAD_DIM:(h + 1) * HEAD_DIM, :],
                            preferred_element_type=F32)
        for hh in range(N_SWA_HEADS):
            o = swa_scr[:, hh, :].astype(BF16)
            lo = RET_W + hh * HEAD_DIM
            y = y + jnp.dot(o, wout_ref[lo:lo + HEAD_DIM, :], preferred_element_type=F32)
        x1_ref[...] = _ln(DN_ALPHA * x_ref[...] + (1.0 + gate_ref[0]) * y, lng_ref[...], lnb_ref[...])


def _mix_sample(proj, x, mod, layer, w_out_bf, gamma1, bias_row, sinks, gain8, ln_g, ln_b,
                st, ck, cv):
    q, k, _, v, g, sq, skv = proj
    nb = x.shape[0]
    steps = nb // SAMPLE_BB

    def joined(a):
        a = a.astype(F32).reshape(nb, RET_W // 128, 2, 128)
        return (a[:, :, 0, :] + a[:, :, 1, :]).reshape(nb, RET_W)

    to_cols = lambda a: joined(a).reshape(steps, SAMPLE_BB, RET_W).transpose(0, 2, 1)
    qt, kt = to_cols(q), to_cols(k)
    v3 = joined(v).reshape(nb, N_RET_HEADS, HEAD_DIM)
    sq3 = joined(sq).reshape(nb, GQA_GROUP, N_KV_HEADS, HEAD_DIM).transpose(0, 2, 1, 3)
    sq3 = sq3.reshape(nb, N_SWA_HEADS, HEAD_DIM)
    g3 = g.reshape(nb, N_RET_HEADS, HEAD_DIM).transpose(1, 0, 2)
    knew, vnew = skv[:, :SWA_KVW], skv[:, SWA_KVW:]
    st2 = st.reshape(nb, RET_W, HEAD_DIM)
    gam = jnp.broadcast_to(jnp.repeat(gamma1, HEAD_DIM)[:, None], (RET_W, HEAD_DIM))
    ck2 = ck.reshape(nb, CHUNK, SWA_KVW)
    cv2 = cv.reshape(nb, CHUNK, SWA_KVW)
    smem = pl.BlockSpec(memory_space=pltpu.SMEM)
    blk3 = lambda a, b, c: pl.BlockSpec((a, b, c), lambda i: (i, 0, 0))
    const2 = lambda a: pl.BlockSpec(a.shape, lambda i: (0, 0))
    const3 = lambda a: pl.BlockSpec(a.shape, lambda i: (0, 0, 0))
    x1, nst, nk, nv = pl.pallas_call(
        _mix_sample_kernel,
        grid=(steps,),
        in_specs=[smem, const2(gam),
                  blk3(1, RET_W, SAMPLE_BB), blk3(1, RET_W, SAMPLE_BB),
                  blk3(SAMPLE_BB, N_RET_HEADS, HEAD_DIM), blk3(SAMPLE_BB, N_SWA_HEADS, HEAD_DIM),
                  pl.BlockSpec((SAMPLE_BB, SWA_KVW), lambda i: (i, 0)),
                  pl.BlockSpec((SAMPLE_BB, SWA_KVW), lambda i: (i, 0)),
                  blk3(SAMPLE_BB, RET_W, HEAD_DIM), blk3(SAMPLE_BB, CHUNK, SWA_KVW),
                  blk3(SAMPLE_BB, CHUNK, SWA_KVW),
                  const3(g3), const2(x),
                  pl.BlockSpec((1, nb, D_MODEL), lambda i: (layer, 0, 2)),
                  const2(w_out_bf), const2(bias_row), const2(gain8), const2(ln_g), const2(ln_b)],
        out_specs=[const2(x), blk3(SAMPLE_BB, RET_W, HEAD_DIM), blk3(SAMPLE_BB, CHUNK, SWA_KVW),
                   blk3(SAMPLE_BB, CHUNK, SWA_KVW)],
        out_shape=[jax.ShapeDtypeStruct(x.shape, F32), jax.ShapeDtypeStruct(st2.shape, F32),
                   jax.ShapeDtypeStruct(ck2.shape, F32), jax.ShapeDtypeStruct(cv2.shape, F32)],
        scratch_shapes=[pltpu.VMEM((nb, N_RET_HEADS, HEAD_DIM), F32),
                        pltpu.VMEM((nb, N_SWA_HEADS, HEAD_DIM), F32)],
        compiler_params=_cparams(("arbitrary",)),
        name="mix_sample",
    )(sinks, gam, qt, kt, v3, sq3, knew, vnew, st2, ck2, cv2, g3, x, mod,
      w_out_bf, bias_row, gain8, ln_g, ln_b)
    return x1, nst.reshape(st.shape), nk.reshape(ck.shape), nv.reshape(cv.shape)


def _router_kernel(x_ref, sh_ref, sc_ref, rwt_ref, rb_ref, tri_ref,
                   oi_ref, of_ref, cnt_ref, run_ref, *, mod_rows):
    h2 = x_ref[...] * (1.0 + sc_ref[0][:mod_rows]) + sh_ref[0][:mod_rows]
    _route_rows(h2, rwt_ref, rb_ref, tri_ref, oi_ref, of_ref, cnt_ref, run_ref)


def _post_mix_kernel(mixed_ref, x_ref, gate_ref, wout_ref, lng_ref, lnb_ref,
                     sh_ref, sc_ref, rwt_ref, rb_ref, tri_ref,
                     x1_ref, oi_ref, of_ref, cnt_ref, run_ref):
    y = jnp.dot(mixed_ref[...], wout_ref[...], preferred_element_type=F32)
    x1 = _ln(DN_ALPHA * x_ref[...] + (1.0 + gate_ref[0][:1]) * y, lng_ref[...], lnb_ref[...])
    x1_ref[...] = x1
    h2 = x1 * (1.0 + sc_ref[0][:1]) + sh_ref[0][:1]
    _route_rows(h2, rwt_ref, rb_ref, tri_ref, oi_ref, of_ref, cnt_ref, run_ref)


def _route_rows(h2, rwt_ref, rb_ref, tri_ref, oi_ref, of_ref, cnt_ref, run_ref):
    i = pl.program_id(0)

    @pl.when(i == 0)
    def _():
        run_ref[...] = jnp.zeros_like(run_ref)

    logits = lax.dot_general(rwt_ref[...].astype(BF16), h2.astype(BF16), (((1,), (1,)), ((), ())),
                             preferred_element_type=F32)
    aff = 1.0 / (1.0 + jnp.exp(-logits))
    sel = aff + rb_ref[...]
    s = [sel[e:e + 1, :] for e in range(N_EXPERTS)]
    a = [aff[e:e + 1, :] for e in range(N_EXPERTS)]

    def top2sum(v0, v1, v2, v3):
        hi01, lo01 = jnp.maximum(v0, v1), jnp.minimum(v0, v1)
        hi23, lo23 = jnp.maximum(v2, v3), jnp.minimum(v2, v3)
        return jnp.maximum(hi01, hi23) + jnp.maximum(jnp.minimum(hi01, hi23),
                                                     jnp.maximum(lo01, lo23))

    def argmax_first(vals):
        best, idx = vals[0], jnp.zeros(vals[0].shape, I32)
        for j in range(1, len(vals)):
            upd = vals[j] > best
            idx = jnp.where(upd, j, idx)
            best = jnp.where(upd, vals[j], best)
        return idx

    def pick(idx, vals):
        out = vals[-1]
        for j in range(len(vals) - 2, -1, -1):
            out = jnp.where(idx == j, vals[j], out)
        return out

    gi = argmax_first([top2sum(*s[4 * g:4 * g + 4]) for g in range(N_GROUPS)])
    sv = [pick(gi, [s[4 * g + j] for g in range(N_GROUPS)]) for j in range(EXPERTS_PER_GROUP)]
    av = [pick(gi, [a[4 * g + j] for g in range(N_GROUPS)]) for j in range(EXPERTS_PER_GROUP)]
    i1 = argmax_first(sv)
    i2 = argmax_first([jnp.where(i1 == j, -jnp.inf, sv[j]) for j in range(EXPERTS_PER_GROUP)])
    w1, w2 = pick(i1, av), pick(i2, av)
    wsum = w1 + w2
    w1, w2 = w1 / wsum, w2 / wsum
    lo, hi = jnp.minimum(i1, i2), jnp.maximum(i1, i2)
    w_lo = jnp.where(i1 < i2, w1, w2)
    w_hi = jnp.where(i1 < i2, w2, w1)
    pair = jnp.where(lo == 0, hi - 1, jnp.where(lo == 1, hi + 1, 5))
    bin_id = gi * N_PAIRS + pair

    tm = bin_id.shape[1]
    onehot = lax.broadcasted_iota(I32, (BIN_ROWS, tm), 0) == bin_id
    oh_f = jnp.where(onehot, 1.0, 0.0)
    before = jnp.dot(oh_f.astype(BF16), tri_ref[...], preferred_element_type=F32)
    run = run_ref[...]
    run_t = jnp.concatenate([run] * (tm // 128), axis=-1)
    rank = jnp.sum(oh_f * (before + run_t), axis=0, keepdims=True)
    run_new = run + jnp.sum(oh_f, axis=1, keepdims=True)
    run_ref[...] = run_new
    cnt_ref[...] = run_new.astype(I32)

    zi = jnp.zeros_like(bin_id)
    oi_ref[0] = jnp.concatenate([gi * 4 + lo, gi * 4 + hi, bin_id, rank.astype(I32), zi, zi, zi, zi], axis=0)
    zf = jnp.zeros_like(w_lo)
    of_ref[0] = jnp.concatenate([w_lo, w_hi, zf, zf, zf, zf, zf, zf], axis=0)


def _router(x1, mod, layer, mod_row0, mod_rows, router_wt, router_b, tm):
    rows = x1.shape[0]
    nt = rows // tm
    mblk = 8 if mod_rows == 1 else mod_rows
    mrow = mod_row0 // mblk
    tri = jnp.asarray(np.triu(np.ones((tm, tm), np.float32), 1), BF16)
    rb = jnp.broadcast_to(router_b.astype(F32)[:, None], (N_EXPERTS, tm))
    mod_spec = lambda c: pl.BlockSpec((1, mblk, D_MODEL), lambda i: (layer, mrow, c))
    oi, of, cnt = pl.pallas_call(
        functools.partial(_router_kernel, mod_rows=mod_rows),
        grid=(nt,),
        in_specs=[pl.BlockSpec((tm, D_MODEL), lambda i: (i, 0)), mod_spec(3), mod_spec(4),
                  pl.BlockSpec((N_EXPERTS, D_MODEL), lambda i: (0, 0)),
                  pl.BlockSpec((N_EXPERTS, tm), lambda i: (0, 0)),
                  pl.BlockSpec((tm, tm), lambda i: (0, 0))],
        out_specs=[pl.BlockSpec((1, 8, tm), lambda i: (i, 0, 0)),
                   pl.BlockSpec((1, 8, tm), lambda i: (i, 0, 0)),
                   pl.BlockSpec((BIN_ROWS, 128), lambda i: (0, 0))],
        out_shape=[jax.ShapeDtypeStruct((nt, 8, tm), I32),
                   jax.ShapeDtypeStruct((nt, 8, tm), F32),
                   jax.ShapeDtypeStruct((BIN_ROWS, 128), I32)],
        scratch_shapes=[pltpu.VMEM((BIN_ROWS, 128), F32)],
        compiler_params=_cparams(("arbitrary",)),
        name="router",
    )(x1, mod, mod, router_wt, rb, tri)
    return _unpack_route(oi, of, cnt)


def _unpack_route(oi, of, cnt):
    flat = lambda a, r: a[:, r, :].reshape(-1)
    return (flat(oi, 0), flat(oi, 1), flat(oi, 2), flat(oi, 3), flat(of, 0), flat(of, 1),
            cnt[:N_BINS, 0])


POST_TM = 512


def _post_mix(mixed, x, mod, layer, mod_row0, w_out_bf, ln_g, ln_b, router_wt, router_b):
    rows = x.shape[0]
    tm = POST_TM
    nt = rows // tm
    mrow = mod_row0 // 8
    tri = jnp.asarray(np.triu(np.ones((tm, tm), np.float32), 1), BF16)
    rb = jnp.broadcast_to(router_b.astype(F32)[:, None], (N_EXPERTS, tm))
    mod_spec = lambda c: pl.BlockSpec((1, 8, D_MODEL), lambda i: (layer, mrow, c))
    row_spec = pl.BlockSpec((tm, D_MODEL), lambda i: (i, 0))
    const2 = lambda a: pl.BlockSpec(a.shape, lambda i: (0, 0))
    x1, oi, of, cnt = pl.pallas_call(
        _post_mix_kernel,
        grid=(nt,),
        in_specs=[row_spec, row_spec, mod_spec(2), const2(w_out_bf), const2(ln_g), const2(ln_b),
                  mod_spec(3), mod_spec(4), const2(router_wt), const2(rb), const2(tri)],
        out_specs=[row_spec,
                   pl.BlockSpec((1, 8, tm), lambda i: (i, 0, 0)),
                   pl.BlockSpec((1, 8, tm), lambda i: (i, 0, 0)),
                   pl.BlockSpec((BIN_ROWS, 128), lambda i: (0, 0))],
        out_shape=[jax.ShapeDtypeStruct((rows, D_MODEL), F32),
                   jax.ShapeDtypeStruct((nt, 8, tm), I32),
                   jax.ShapeDtypeStruct((nt, 8, tm), F32),
                   jax.ShapeDtypeStruct((BIN_ROWS, 128), I32)],
        scratch_shapes=[pltpu.VMEM((BIN_ROWS, 128), F32)],
        compiler_params=_cparams(("arbitrary",)),
        name="post_mix",
    )(mixed, x, mod, w_out_bf, ln_g, ln_b, mod, mod, router_wt, rb, tri)
    return x1, _unpack_route(oi, of, cnt)


DMA_UNROLL = 8
ROW_GROUPS = MOE_TM // DMA_UNROLL


def _for_rows(n, fn):
    def body_u(j, carry):
        for u in range(DMA_UNROLL):
            fn(j, u)
        return carry
    lax.fori_loop(0, n // DMA_UNROLL, body_u, 0)

    def body_1(r, carry):
        fn(r // DMA_UNROLL, r % DMA_UNROLL)
        return carry
    lax.fori_loop((n // DMA_UNROLL) * DMA_UNROLL, n, body_1, 0)


def _wait_rows(n, buf, sem):
    p = ROW_GROUPS
    while p >= 1:
        @pl.when((n & (p * DMA_UNROLL)) != 0)
        def _(p=p):
            pltpu.make_async_copy(buf.at[pl.ds(0, p)], buf.at[pl.ds(0, p)], sem).wait()
        p //= 2
    p = DMA_UNROLL // 2
    while p >= 1:
        @pl.when((n & p) != 0)
        def _(p=p):
            pltpu.make_async_copy(buf.at[0, pl.ds(0, p)], buf.at[0, pl.ds(0, p)], sem).wait()
        p //= 2


def _moe_sorted_kernel(ea_ref, eb_ref, nvalid_ref, pos_ref,
                       x_hbm, sh_ref, sc_ref, gate_ref, rw_ref,
                       wga_ref, wua_ref, wda_ref, wgb_ref, wub_ref, wdb_ref,
                       lng_ref, lnb_ref,
                       out_hbm,
                       inv, xbuf, obuf, wg_a, wu_a, wd_a, wg_b, wu_b, wd_b, gsem, ssem):
    i = pl.program_id(0)
    nt = pl.num_programs(0)
    slot = i % 2
    n_tok = pos_ref.shape[0]

    def start_gather(t, dst_slot):
        def one(j, u):
            tok = inv[t * MOE_TM + j * DMA_UNROLL + u]
            pltpu.make_async_copy(x_hbm.at[pl.ds(tok, 1)], xbuf.at[dst_slot, j, pl.ds(u, 1)],
                                  gsem.at[dst_slot]).start()
        _for_rows(nvalid_ref[t], one)

    @pl.when(i == 0)
    def _():
        def body(j, carry):
            for u in range(DMA_UNROLL):
                t = j * DMA_UNROLL + u
                inv[pos_ref[t]] = t
            return carry
        lax.fori_loop(0, n_tok // DMA_UNROLL, body, 0)
        xbuf[...] = jnp.zeros_like(xbuf)
        start_gather(0, 0)

    @pl.when(i >= 2)
    def _():
        _wait_rows(nvalid_ref[jnp.maximum(i - 2, 0)], obuf.at[slot], ssem.at[slot])

    @pl.when(nvalid_ref[i] > 0)
    def _():
        _wait_rows(nvalid_ref[i], xbuf.at[slot], gsem.at[slot])

        @pl.when(i + 1 < nt)
        def _():
            start_gather(jnp.minimum(i + 1, nt - 1), 1 - slot)

        @pl.when(jnp.logical_or(i == 0, ea_ref[i] != ea_ref[jnp.maximum(i - 1, 0)]))
        def _():
            wg_a[...] = wga_ref[0, 0].astype(BF16)
            wu_a[...] = wua_ref[0, 0].astype(BF16)
            wd_a[...] = wda_ref[0, 0].astype(BF16)

        @pl.when(jnp.logical_or(i == 0, eb_ref[i] != eb_ref[jnp.maximum(i - 1, 0)]))
        def _():
            wg_b[...] = wgb_ref[0, 0].astype(BF16)
            wu_b[...] = wub_ref[0, 0].astype(BF16)
            wd_b[...] = wdb_ref[0, 0].astype(BF16)

        x = xbuf[slot].reshape(MOE_TM, D_MODEL)
        h2 = (x * (1.0 + sc_ref[0][:1]) + sh_ref[0][:1]).astype(BF16)

        dot = functools.partial(jnp.dot, preferred_element_type=F32)
        logits = dot(h2, rw_ref[...])
        gate_a, gate_b = dot(h2, wg_a[...]), dot(h2, wg_b[...])
        up_a, up_b = dot(h2, wu_a[...]), dot(h2, wu_b[...])
        act_a = (_silu(gate_a) * up_a).astype(BF16)
        act_b = (_silu(gate_b) * up_b).astype(BF16)
        y_a, y_b = dot(act_a, wd_a[...]), dot(act_b, wd_b[...])

        aff = 1.0 / (1.0 + jnp.exp(-logits))
        lane = lax.broadcasted_iota(I32, aff.shape, 1)
        a_lo = jnp.sum(jnp.where(lane == ea_ref[i], aff, 0.0), axis=-1, keepdims=True)
        a_hi = jnp.sum(jnp.where(lane == eb_ref[i], aff, 0.0), axis=-1, keepdims=True)
        a_sum = a_lo + a_hi
        y = (a_lo / a_sum) * y_a + (a_hi / a_sum) * y_b
        out = _ln(DN_ALPHA * x + (1.0 + gate_ref[0][:1]) * y, lng_ref[...], lnb_ref[...])
        obuf[slot] = out.reshape(ROW_GROUPS, DMA_UNROLL, D_MODEL)

        def scatter_one(j, u):
            tok = inv[i * MOE_TM + j * DMA_UNROLL + u]
            pltpu.make_async_copy(obuf.at[slot, j, pl.ds(u, 1)], out_hbm.at[pl.ds(tok, 1)],
                                  ssem.at[slot]).start()
        _for_rows(nvalid_ref[i], scatter_one)

    @pl.when(i == nt - 1)
    def _():
        _wait_rows(nvalid_ref[jnp.maximum(i - 1, 0)], obuf.at[1 - slot], ssem.at[1 - slot])
        _wait_rows(nvalid_ref[i], obuf.at[slot], ssem.at[slot])


def _moe_sorted(x1, mod, layer, mod_row0, route, rw_pad, wg, wu, wd, ln_g, ln_b):
    _, _, bin_id, rank, _, _, counts = route
    rows = x1.shape[0]
    nt = rows // MOE_TM + N_BINS
    ns = nt * MOE_TM
    tiles = (counts + MOE_TM - 1) // MOE_TM
    tile_end = jnp.cumsum(tiles)
    row_start = (tile_end - tiles) * MOE_TM
    pos = row_start[bin_id] + rank
    total = tile_end[-1]
    t = jnp.arange(nt, dtype=I32)
    tile_bin = jnp.searchsorted(tile_end, jnp.minimum(t, total - 1), side="right").astype(I32)
    tile_bin = jnp.clip(tile_bin, 0, N_BINS - 1)
    bin_lo = jnp.asarray([g * 4 + _PAIR_LO[p] for g in range(N_GROUPS) for p in range(N_PAIRS)], I32)
    bin_hi = jnp.asarray([g * 4 + _PAIR_HI[p] for g in range(N_GROUPS) for p in range(N_PAIRS)], I32)
    tile_ea, tile_eb = bin_lo[tile_bin], bin_hi[tile_bin]
    nvalid = jnp.clip(counts[tile_bin] - (t - (tile_end - tiles)[tile_bin]) * MOE_TM, 0, MOE_TM)
    nvalid = jnp.where(t < total, nvalid, 0).astype(I32)
    mrow = mod_row0 // 8
    mod_spec = lambda c: pl.BlockSpec((1, 8, D_MODEL), lambda i, *_: (layer, mrow, c))
    w_spec = lambda shape, ref_idx: pl.BlockSpec(
        (1, 1) + shape, lambda i, ea_r, eb_r, *_: (layer, (ea_r, eb_r)[ref_idx][i], 0, 0))
    any_spec = pl.BlockSpec(memory_space=pl.ANY)
    w_bf = lambda shape: pltpu.VMEM(shape, BF16)
    grid_spec = pltpu.PrefetchScalarGridSpec(
        num_scalar_prefetch=4,
        grid=(nt,),
        in_specs=[any_spec,
                  mod_spec(3), mod_spec(4), mod_spec(5),
                  pl.BlockSpec((D_MODEL, 128), lambda i, *_: (0, 0)),
                  w_spec((D_MODEL, D_FF), 0), w_spec((D_MODEL, D_FF), 0), w_spec((D_FF, D_MODEL), 0),
                  w_spec((D_MODEL, D_FF), 1), w_spec((D_MODEL, D_FF), 1), w_spec((D_FF, D_MODEL), 1),
                  pl.BlockSpec((1, D_MODEL), lambda i, *_: (0, 0)),
                  pl.BlockSpec((1, D_MODEL), lambda i, *_: (0, 0))],
        out_specs=any_spec,
        scratch_shapes=[pltpu.SMEM((ns,), I32),
                        pltpu.VMEM((2, ROW_GROUPS, DMA_UNROLL, D_MODEL), F32),
                        pltpu.VMEM((2, ROW_GROUPS, DMA_UNROLL, D_MODEL), F32),
                        w_bf((D_MODEL, D_FF)), w_bf((D_MODEL, D_FF)), w_bf((D_FF, D_MODEL)),
                        w_bf((D_MODEL, D_FF)), w_bf((D_MODEL, D_FF)), w_bf((D_FF, D_MODEL)),
                        pltpu.SemaphoreType.DMA((2,)),
                        pltpu.SemaphoreType.DMA((2,))],
    )
    return pl.pallas_call(
        _moe_sorted_kernel,
        grid_spec=grid_spec,
        out_shape=jax.ShapeDtypeStruct((rows, D_MODEL), F32),
        compiler_params=_cparams(("arbitrary",)),
        name="moe_sorted",
    )(tile_ea, tile_eb, nvalid, pos, x1, mod, mod, mod, rw_pad,
      wg, wu, wd, wg, wu, wd, ln_g, ln_b)


def _moe_dense_kernel(x_ref, dw_ref, sh_ref, sc_ref, gate_ref, wg_ref, wu_ref, wd_ref,
                      lng_ref, lnb_ref, o_ref, acc):
    e = pl.program_id(0)

    @pl.when(e == 0)
    def _():
        acc[...] = jnp.zeros_like(acc)

    x = x_ref[...]
    h2 = (x * (1.0 + sc_ref[0]) + sh_ref[0]).astype(BF16)
    a = _silu(jnp.dot(h2, wg_ref[0, 0].astype(BF16), preferred_element_type=F32)) * \
        jnp.dot(h2, wu_ref[0, 0].astype(BF16), preferred_element_type=F32)
    y = jnp.dot(a.astype(BF16), wd_ref[0, 0].astype(BF16), preferred_element_type=F32)
    acc[...] = acc[...] + dw_ref[0][:, :1] * y

    @pl.when(e == pl.num_programs(0) - 1)
    def _():
        o_ref[...] = _ln(DN_ALPHA * x + (1.0 + gate_ref[0]) * acc[...], lng_ref[...], lnb_ref[...])


def _moe_dense(x1, mod, layer, route, wg, wu, wd, ln_g, ln_b):
    ea, eb, _, _, w_lo, w_hi, _ = route
    nb = x1.shape[0]
    eids = jnp.arange(N_EXPERTS, dtype=I32)[:, None]
    dw = jnp.where(eids == ea[None, :], w_lo[None, :], 0.0) + \
        jnp.where(eids == eb[None, :], w_hi[None, :], 0.0)
    dw = jnp.broadcast_to(dw[:, :, None], (N_EXPERTS, nb, 128))
    mod_spec = lambda c: pl.BlockSpec((1, nb, D_MODEL), lambda e: (layer, 0, c))
    const2 = lambda a: pl.BlockSpec(a.shape, lambda e: (0, 0))
    return pl.pallas_call(
        _moe_dense_kernel,
        grid=(N_EXPERTS,),
        in_specs=[const2(x1), pl.BlockSpec((1, nb, 128), lambda e: (e, 0, 0)),
                  mod_spec(3), mod_spec(4), mod_spec(5),
                  pl.BlockSpec((1, 1, D_MODEL, D_FF), lambda e: (layer, e, 0, 0)),
                  pl.BlockSpec((1, 1, D_MODEL, D_FF), lambda e: (layer, e, 0, 0)),
                  pl.BlockSpec((1, 1, D_FF, D_MODEL), lambda e: (layer, e, 0, 0)),
                  const2(ln_g), const2(ln_b)],
        out_specs=const2(x1),
        out_shape=jax.ShapeDtypeStruct(x1.shape, F32),
        scratch_shapes=[pltpu.VMEM(x1.shape, F32)],
        compiler_params=_cparams(("arbitrary",)),
        name="moe_dense",
    )(x1, dw, mod, mod, mod, wg, wu, wd, ln_g, ln_b)


def kernel(x_prompt, x_sample, state_ret, cache_swa_k, cache_swa_v, c_prompt, c_sample, w_in, w_out, ret_gn_gain, swa_sinks, rel_bias_table, ada_w, ada_b, ln1_g, ln1_b, ln2_g, ln2_b, router_w, router_b, exp_w_gate, exp_w_up, exp_w_down):
    seq = x_prompt.shape[1]
    nb = x_sample.shape[0]
    past_len = 16384
    assert x_prompt.shape[0] == 1 and x_sample.shape[1] == 1

    c_all = jnp.concatenate([c_sample, c_prompt, jnp.zeros((7, D_MODEL), F32)], axis=0)
    mod = _ada(c_all, ada_w, ada_b)
    prompt_row = nb

    bias_tabs = _bias_tables(rel_bias_table.astype(F32))
    gl, decay, xi, zeta, gamma1 = _ret_consts()
    rope_p = _rope_tables(jnp.arange(seq, dtype=I32))
    rope_s = _rope_tables(jnp.full((nb,), past_len, I32))
    router_wt = router_w.astype(F32).T
    rw_pad = jnp.pad(router_w.astype(BF16), ((0, 0), (0, 128 - N_EXPERTS)))
    zeta_p = jnp.tile(zeta, (INPROJ_TM // CHUNK, 1))
    zeta_s = jnp.ones((nb, RET_W), F32)

    xp = x_prompt.reshape(seq, D_MODEL)
    xs = x_sample.reshape(nb, D_MODEL)
    st_p, k_p, v_p, st_s, k_s, v_s = [], [], [], [], [], []
    for l in range(DEPTH):
        order = jnp.asarray(_SWA_HEAD_ORDER)
        sq_lo, sq_hi = 4 * RET_W, 4 * RET_W + SWA_QW
        sq_cols = w_in[l][:, sq_lo:sq_hi].reshape(D_MODEL, N_SWA_HEADS, HEAD_DIM)[:, order, :]
        w_in_bf = jnp.concatenate([w_in[l][:, :sq_lo], sq_cols.reshape(D_MODEL, SWA_QW),
                                   w_in[l][:, sq_hi:]], axis=1).astype(BF16)
        w_out_bf = w_out[l].astype(BF16)
        swa_rows = w_out[l][RET_W:].reshape(N_SWA_HEADS, HEAD_DIM, D_MODEL)[order]
        w_out_prompt = jnp.concatenate([w_out[l][:RET_W], swa_rows.reshape(SWA_QW, D_MODEL)],
                                       axis=0).astype(BF16)
        experts = (exp_w_gate.astype(F32), exp_w_up.astype(F32), exp_w_down.astype(F32))
        gain = ret_gn_gain[l].astype(F32)
        sinks = swa_sinks[l].astype(F32)
        row = lambda a: a[l].astype(F32).reshape(1, D_MODEL)

        proj = _inproj(xp, mod, l, prompt_row, 1, w_in_bf, rope_p, zeta_p, INPROJ_TM)
        mixed, st = _mix_prompt(proj, (gl, decay, xi), bias_tabs, sinks, gain.reshape(1, RET_W))
        skv = proj[6]
        st_p.append(st.reshape(1, N_RET_HEADS, HEAD_DIM, HEAD_DIM))
        k_p.append(skv[seq - CHUNK:, :SWA_KVW].reshape(1, CHUNK, N_KV_HEADS, HEAD_DIM))
        v_p.append(skv[seq - CHUNK:, SWA_KVW:].reshape(1, CHUNK, N_KV_HEADS, HEAD_DIM))
        x1, route = _post_mix(mixed, xp, mod, l, prompt_row, w_out_prompt, row(ln1_g), row(ln1_b),
                              router_wt, router_b)
        xp = _moe_sorted(x1, mod, l, prompt_row, route, rw_pad, *experts, row(ln2_g), row(ln2_b))

        proj = _inproj(xs, mod, l, 0, nb, w_in_bf, rope_s, zeta_s, nb)
        x1, nst, nk, nv = _mix_sample(proj, xs, mod, l, w_out_bf, gamma1, bias_tabs[1], sinks, gain,
                                      row(ln1_g), row(ln1_b), state_ret[l].astype(F32),
                                      cache_swa_k[l], cache_swa_v[l])
        st_s.append(nst)
        k_s.append(nk)
        v_s.append(nv)
        route = _router(x1, mod, l, 0, nb, router_wt, router_b, nb)
        xs = _moe_dense(x1, mod, l, route, *experts, row(ln2_g), row(ln2_b))

    return (xp.reshape(1, seq, D_MODEL), xs.reshape(nb, 1, D_MODEL),
            jnp.stack(st_p), jnp.stack(k_p), jnp.stack(v_p),
            jnp.stack(st_s), jnp.stack(k_s), jnp.stack(v_s))
```

```python
import functools
import math

import numpy as np
import jax
import jax.numpy as jnp
from jax import lax
from jax.experimental import pallas as pl
from jax.experimental.pallas import tpu as pltpu

F32 = jnp.float32
BF16 = jnp.bfloat16
I32 = jnp.int32

D_MODEL = 1024
DEPTH = 2
HEAD_DIM = 64
N_RET_HEADS = 8
N_SWA_HEADS = 8
N_KV_HEADS = 2
GQA_GROUP = N_SWA_HEADS // N_KV_HEADS
RET_W = N_RET_HEADS * HEAD_DIM
SWA_QW = N_SWA_HEADS * HEAD_DIM
SWA_KVW = N_KV_HEADS * HEAD_DIM
PROJ_W = 4 * RET_W + SWA_QW + 2 * SWA_KVW
CHUNK = 128
ROPE_BASE = 10000.0
N_BUCKETS = 32
MAX_DISTANCE = 128
N_EXPERTS = 16
N_GROUPS = 4
EXPERTS_PER_GROUP = 4
D_FF = 512
LN_EPS = 1e-5
DN_ALPHA = (2 * DEPTH) ** 0.25
QK_SCALE = HEAD_DIM ** -0.5
NEG_BIG = -1e30

N_PAIRS = 6
N_BINS = N_GROUPS * N_PAIRS
BIN_ROWS = 32
MOE_TM = 256
INPROJ_TM = 512
MIX_SUB = 2
SPLIT_W = 128 * N_RET_HEADS
_SWA_HEAD_ORDER = (0, 4, 1, 5, 2, 6, 3, 7)
VMEM_LIMIT = 56 * 1024 * 1024

_PAIR_LO = (0, 0, 0, 1, 1, 2)
_PAIR_HI = (1, 2, 3, 2, 3, 3)


def _cparams(sem):
    return pltpu.CompilerParams(dimension_semantics=sem, vmem_limit_bytes=VMEM_LIMIT)


def _ln(v, g, b):
    mu = jnp.mean(v, axis=-1, keepdims=True)
    c = v - mu
    var = jnp.mean(c * c, axis=-1, keepdims=True)
    return c * lax.rsqrt(var + LN_EPS) * g + b


def _silu(v):
    return v * (1.0 / (1.0 + jnp.exp(-v)))


def _ada_kernel(c_ref, w_ref, b_ref, o_ref):
    o_ref[0] = jnp.dot(c_ref[...].astype(BF16), w_ref[0].astype(BF16),
                       preferred_element_type=F32) + b_ref[0]


def _ada(c_all, ada_w, ada_b):
    rows = c_all.shape[0]
    nt = 6 * D_MODEL // 1024
    return pl.pallas_call(
        _ada_kernel,
        grid=(DEPTH, nt),
        in_specs=[pl.BlockSpec((rows, D_MODEL), lambda l, j: (0, 0)),
                  pl.BlockSpec((1, D_MODEL, 1024), lambda l, j: (l, 0, j)),
                  pl.BlockSpec((1, 1, 1024), lambda l, j: (l, 0, j))],
        out_specs=pl.BlockSpec((1, rows, 1024), lambda l, j: (l, 0, j)),
        out_shape=jax.ShapeDtypeStruct((DEPTH, rows, 6 * D_MODEL), F32),
        compiler_params=_cparams(("arbitrary", "arbitrary")),
        name="ada",
    )(c_all, ada_w, ada_b.reshape(DEPTH, 1, 6 * D_MODEL))


def _bias_kernel(tab_ref, bkt_ref, fold_ref, row_ref):
    bkt = bkt_ref[...]
    rows = lax.broadcasted_iota(I32, (CHUNK, CHUNK), 0)
    cols = lax.broadcasted_iota(I32, (CHUNK, CHUNK), 1)
    from_prev = cols > rows
    for h in range(N_SWA_HEADS):
        acc = jnp.zeros(bkt.shape, F32)
        for b in range(N_BUCKETS):
            acc = jnp.where(bkt == b, tab_ref[b, h], acc)
        own = acc[:, CHUNK:]
        fold_ref[0, h] = jnp.where(from_prev, NEG_BIG, own)
        fold_ref[1, h] = jnp.where(from_prev, acc[:, :CHUNK], own)
        row_ref[h:h + 1, :] = own[CHUNK - 1:CHUNK, :]


def _t5_bucket(rel):
    max_exact = N_BUCKETS // 2
    relf = jnp.maximum(rel, 1).astype(F32)
    large = max_exact + (jnp.log(relf / max_exact) / math.log(MAX_DISTANCE / max_exact)
                         * (N_BUCKETS - max_exact)).astype(I32)
    large = jnp.minimum(large, N_BUCKETS - 1)
    return jnp.where(rel < max_exact, rel, large)


def _bias_tables(rel_bias_table):
    qi = jnp.arange(CHUNK)
    si = jnp.arange(2 * CHUNK)
    rel = CHUNK + qi[:, None] - si[None, :]
    bkt = _t5_bucket(jnp.maximum(rel, 0)).astype(I32)
    return pl.pallas_call(
        _bias_kernel,
        in_specs=[pl.BlockSpec(memory_space=pltpu.SMEM),
                  pl.BlockSpec((CHUNK, 2 * CHUNK), lambda: (0, 0))],
        out_specs=[pl.BlockSpec((2, N_SWA_HEADS, CHUNK, CHUNK), lambda: (0, 0, 0, 0)),
                   pl.BlockSpec((N_SWA_HEADS, CHUNK), lambda: (0, 0))],
        out_shape=[jax.ShapeDtypeStruct((2, N_SWA_HEADS, CHUNK, CHUNK), F32),
                   jax.ShapeDtypeStruct((N_SWA_HEADS, CHUNK), F32)],
        name="t5_bias",
    )(rel_bias_table, bkt)


def _rotary(v, cos, s_lo, s_hi):
    outs = []
    for j in range(RET_W // 128):
        blk = v[:, j * 128:(j + 1) * 128]
        outs.append(blk * cos + pltpu.roll(blk, 96, 1) * s_lo + pltpu.roll(blk, 32, 1) * s_hi)
    return jnp.concatenate(outs, axis=-1)


def _inproj_kernel(x_ref, sh_ref, sc_ref, w_ref, cs_ref, zeta_ref,
                   q_ref, k_ref, kz_ref, v_ref, g_ref, sq_ref, skv_ref, *, mod_rows):
    sh = sh_ref[0][:mod_rows]
    sc = sc_ref[0][:mod_rows]
    h = (x_ref[...] * (1.0 + sc) + sh).astype(BF16)
    cs = cs_ref[...]
    first = (lax.broadcasted_iota(I32, cs.shape, 1) % HEAD_DIM) < (HEAD_DIM // 2)
    cos = jnp.where(first, cs, pltpu.roll(cs, 32, 1))
    s_lo = jnp.where(first, -pltpu.roll(cs, 96, 1), 0.0)
    s_hi = jnp.where(first, 0.0, cs)

    def proj(lo, hi):
        return jnp.dot(h, w_ref[:, lo:hi], preferred_element_type=F32)

    left = lax.broadcasted_iota(I32, cs.shape, 1) < HEAD_DIM

    def store_split(ref, val):
        for b in range(RET_W // 128):
            blk = val[:, b * 128:(b + 1) * 128]
            ref[:, (2 * b) * 128:(2 * b + 1) * 128] = jnp.where(left, blk, 0.0).astype(BF16)
            ref[:, (2 * b + 1) * 128:(2 * b + 2) * 128] = jnp.where(left, 0.0, blk).astype(BF16)

    store_split(q_ref, _rotary(proj(0, RET_W), cos, s_lo, s_hi))
    k = _rotary(proj(RET_W, 2 * RET_W), cos, s_lo, s_hi) * QK_SCALE
    store_split(k_ref, k)
    store_split(kz_ref, k * zeta_ref[...])
    store_split(v_ref, proj(2 * RET_W, 3 * RET_W))
    g_ref[...] = proj(3 * RET_W, 4 * RET_W)
    sq = proj(4 * RET_W, 4 * RET_W + SWA_QW) * QK_SCALE
    for hh in range(N_SWA_HEADS):
        kv, j = divmod(hh, GQA_GROUP)
        blk = sq[:, (hh // 2) * 128:(hh // 2 + 1) * 128]
        if hh % 2 != kv:
            blk = pltpu.roll(blk, HEAD_DIM, 1)
        keep = left if kv == 0 else jnp.logical_not(left)
        sq_ref[:, (2 * j + kv) * 128:(2 * j + kv + 1) * 128] = jnp.where(keep, blk, 0.0).astype(BF16)
    skv_ref[...] = proj(4 * RET_W + SWA_QW, PROJ_W)


def _inproj(x, mod, layer, mod_row0, mod_rows, w_in_bf, rope, zeta_tile, tm):
    rows = x.shape[0]
    mblk = 8 if mod_rows == 1 else mod_rows
    mrow = mod_row0 // mblk
    row_spec = lambda w: pl.BlockSpec((tm, w), lambda i: (i, 0))
    mod_spec = lambda c: pl.BlockSpec((1, mblk, D_MODEL), lambda i: (layer, mrow, c))
    return pl.pallas_call(
        functools.partial(_inproj_kernel, mod_rows=mod_rows),
        grid=(rows // tm,),
        in_specs=[row_spec(D_MODEL), mod_spec(0), mod_spec(1),
                  pl.BlockSpec((D_MODEL, PROJ_W), lambda i: (0, 0)),
                  row_spec(128),
                  pl.BlockSpec((tm, RET_W), lambda i: (0, 0))],
        out_specs=[row_spec(SPLIT_W), row_spec(SPLIT_W), row_spec(SPLIT_W), row_spec(SPLIT_W),
                   row_spec(RET_W), row_spec(SPLIT_W), row_spec(2 * SWA_KVW)],
        out_shape=[jax.ShapeDtypeStruct((rows, SPLIT_W), BF16),
                   jax.ShapeDtypeStruct((rows, SPLIT_W), BF16),
                   jax.ShapeDtypeStruct((rows, SPLIT_W), BF16),
                   jax.ShapeDtypeStruct((rows, SPLIT_W), BF16),
                   jax.ShapeDtypeStruct((rows, RET_W), F32),
                   jax.ShapeDtypeStruct((rows, SPLIT_W), BF16),
                   jax.ShapeDtypeStruct((rows, 2 * SWA_KVW), F32)],
        compiler_params=_cparams(("arbitrary",)),
        name="inproj",
    )(x, mod, mod, w_in_bf, rope, zeta_tile)


def _rope_tables(pos):
    half = HEAD_DIM // 2
    inv = ROPE_BASE ** (-jnp.arange(half, dtype=F32) / half)
    ang = pos.astype(F32)[:, None] * inv[None, :]
    cos, sin = jnp.cos(ang), jnp.sin(ang)
    return jnp.concatenate([cos, sin, cos, sin], axis=-1)


def _mix_prompt_kernel(gl_ref, sink_ref,
                       q_ref, k_ref, kz_ref, v_ref, g_ref, sq_ref, kvc_ref, kvp_ref,
                       decay_ref, xi_ref, bias_ref, gain_ref,
                       mixed_ref, st_ref, state):
    i = pl.program_id(0)

    @pl.when(i == 0)
    def _():
        state[...] = jnp.zeros_like(state)

    gain = gain_ref[...]
    left = lax.broadcasted_iota(I32, (CHUNK, 128), 1) < HEAD_DIM
    from_prev = (lax.broadcasted_iota(I32, (CHUNK, CHUNK), 1) >
                 lax.broadcasted_iota(I32, (CHUNK, CHUNK), 0))
    inv_dim = 1.0 / HEAD_DIM

    for sub in range(MIX_SUB):
        rows = slice(sub * CHUNK, (sub + 1) * CHUNK)

        heads = range(N_RET_HEADS)
        hsl = [slice(h * 128, (h + 1) * 128) for h in heads]
        scores_r = [lax.dot_general(q_ref[rows, hsl[h]], k_ref[rows, hsl[h]], (((1,), (1,)), ((), ())),
                                    preferred_element_type=F32) for h in heads]
        s_olds = [state[h] for h in heads]
        inter = [jnp.dot(q_ref[rows, hsl[h]], s_olds[h].astype(BF16), preferred_element_type=F32)
                 for h in heads]
        upds = [lax.dot_general(kz_ref[rows, hsl[h]], v_ref[rows, hsl[h]], (((0,), (0,)), ((), ())),
                                preferred_element_type=F32) for h in heads]
        for h in heads:
            state[h] = gl_ref[h] * s_olds[h] + upds[h]
        probs_r = [(scores_r[h] * decay_ref[h]).astype(BF16) for h in heads]
        intra = [jnp.dot(probs_r[h], v_ref[rows, hsl[h]], preferred_element_type=F32) for h in heads]

        owns = [left if h % 2 == 0 else jnp.logical_not(left) for h in heads]
        outs = [intra[h] + inter[h] * xi_ref[:, hsl[h]] for h in heads]
        mus = [jnp.sum(outs[h], axis=-1, keepdims=True) * inv_dim for h in heads]
        cen = [jnp.where(owns[h], outs[h] - mus[h], 0.0) for h in heads]
        var = [jnp.sum(cen[h] * cen[h], axis=-1, keepdims=True) * inv_dim for h in heads]
        nrm = [cen[h] * lax.rsqrt(var[h] + LN_EPS) for h in heads]
        ret_blocks = [nrm[2 * b] + nrm[2 * b + 1] for b in range(N_RET_HEADS // 2)]
        ret_o = jnp.concatenate(ret_blocks, axis=-1) * gain * _silu(g_ref[rows, :])

        kvc = kvc_ref[rows, :]
        kvp = kvp_ref[...] if sub == 0 else kvc_ref[(sub - 1) * CHUNK:sub * CHUNK, :]
        kcat = jnp.concatenate([kvp[:, :SWA_KVW], kvc[:, :SWA_KVW]], axis=0)
        vcat = jnp.concatenate([kvp[:, SWA_KVW:], kvc[:, SWA_KVW:]], axis=0)
        left2 = jnp.concatenate([left, left], axis=0)
        table = jnp.minimum(i, 1) if sub == 0 else 1
        swa_blocks = [None] * GQA_GROUP
        kks, vvs = [], []
        for kv in range(N_KV_HEADS):
            own2 = left2 if kv == 0 else jnp.logical_not(left2)
            kks.append(jnp.where(own2, kcat, 0.0).astype(BF16))
            vvs.append(jnp.where(own2, vcat, 0.0).astype(BF16))
        scores = []
        for hh in range(N_SWA_HEADS):
            kv, j = divmod(hh, GQA_GROUP)
            blk = 2 * j + kv
            qh = sq_ref[rows, blk * 128:(blk + 1) * 128]
            scores.append(lax.dot_general(qh, kks[kv], (((1,), (1,)), ((), ())),
                                          preferred_element_type=F32))
        probs = []
        for hh in range(N_SWA_HEADS):
            s2 = scores[hh]
            s = jnp.where(from_prev, s2[:, :CHUNK], s2[:, CHUNK:]) + bias_ref[table, hh]
            sink = sink_ref[hh]
            m = jnp.maximum(jnp.max(s, axis=-1, keepdims=True), sink)
            p = jnp.exp(s - m)
            den = jnp.sum(p, axis=-1, keepdims=True) + jnp.exp(sink - m)
            pn = p / den
            p2 = jnp.concatenate([jnp.where(from_prev, pn, 0.0), jnp.where(from_prev, 0.0, pn)], axis=-1)
            probs.append(p2.astype(BF16))
        for hh in range(N_SWA_HEADS):
            kv, j = divmod(hh, GQA_GROUP)
            o = jnp.dot(probs[hh], vvs[kv], preferred_element_type=F32)
            swa_blocks[j] = o if kv == 0 else swa_blocks[j] + o
        swa_o = jnp.concatenate(swa_blocks, axis=-1)

        mixed_ref[rows, :] = jnp.concatenate([ret_o, swa_o], axis=-1).astype(BF16)

    @pl.when(i == pl.num_programs(0) - 1)
    def _():
        st_ref[...] = state[...]


def _mix_prompt(proj, ret_consts, bias_tabs, sinks, gain):
    q, k, kz, v, g, sq, skv = proj
    rows = q.shape[0]
    tm = MIX_SUB * CHUNK
    gl, decay, xi = ret_consts
    bias_fold = bias_tabs[0]
    row_spec = lambda w: pl.BlockSpec((tm, w), lambda i: (i, 0))
    const2 = lambda a: pl.BlockSpec(a.shape, lambda i: (0, 0))
    const3 = lambda a: pl.BlockSpec(a.shape, lambda i: (0, 0, 0))
    smem = pl.BlockSpec(memory_space=pltpu.SMEM)
    mixed, st_full = pl.pallas_call(
        _mix_prompt_kernel,
        grid=(rows // tm,),
        in_specs=[smem, smem,
                  row_spec(SPLIT_W), row_spec(SPLIT_W), row_spec(SPLIT_W), row_spec(SPLIT_W), row_spec(RET_W),
                  row_spec(SPLIT_W), row_spec(2 * SWA_KVW),
                  pl.BlockSpec((CHUNK, 2 * SWA_KVW), lambda i: (jnp.maximum(MIX_SUB * i - 1, 0), 0)),
                  const3(decay), const2(xi),
                  pl.BlockSpec(bias_fold.shape, lambda i: (0, 0, 0, 0)),
                  const2(gain)],
        out_specs=[row_spec(D_MODEL),
                   pl.BlockSpec((N_RET_HEADS, 128, 128), lambda i: (0, 0, 0))],
        out_shape=[jax.ShapeDtypeStruct((rows, D_MODEL), BF16),
                   jax.ShapeDtypeStruct((N_RET_HEADS, 128, 128), F32)],
        scratch_shapes=[pltpu.VMEM((N_RET_HEADS, 128, 128), F32)],
        compiler_params=_cparams(("arbitrary",)),
        name="mix_prompt",
    )(gl, sinks, q, k, kz, v, g, sq, skv, skv, decay, xi, bias_fold, gain)
    lo, hi = slice(0, HEAD_DIM), slice(HEAD_DIM, 128)
    st = jnp.stack([st_full[h, lo, lo] if h % 2 == 0 else st_full[h, hi, hi] for h in range(N_RET_HEADS)])
    return mixed, st


def _ret_consts():
    lg = jnp.log(1.0 - 2.0 ** (-5.0 - jnp.arange(N_RET_HEADS, dtype=F32)))
    idx = jnp.arange(CHUNK, dtype=F32)
    diff = idx[:, None] - idx[None, :]
    decay = jnp.where(diff >= 0, jnp.exp(jnp.maximum(diff, 0.0)[None] * lg[:, None, None]), 0.0)
    xi = jnp.exp((idx + 1.0)[:, None] * lg[None, :])
    zeta = jnp.exp((CHUNK - 1.0 - idx)[:, None] * lg[None, :])
    gl = jnp.exp(CHUNK * lg)
    return gl, decay, jnp.repeat(xi, 128, axis=1), jnp.repeat(zeta, HEAD_DIM, axis=1), jnp.exp(1.0 * lg)


SAMPLE_BB = 8


def _mix_sample_kernel(sink_ref, gam_ref,
                       qt_ref, kt_ref, v3_ref, sq3_ref, knew_ref, vnew_ref,
                       st_ref, ck_ref, cv_ref, g3_ref, x_ref, gate_ref,
                       wout_ref, brow_ref, gain_ref, lng_ref, lnb_ref,
                       x1_ref, nst_ref, nk_ref, nv_ref, ret_scr, swa_scr):
    i = pl.program_id(0)
    row = lax.broadcasted_iota(I32, (N_SWA_HEADS, 2 * HEAD_DIM), 0)
    lane = lax.broadcasted_iota(I32, (N_SWA_HEADS, 2 * HEAD_DIM), 1)
    own_half = (row // GQA_GROUP) == (lane // HEAD_DIM)
    sink_col = jnp.concatenate(
        [jnp.full((1, 1), sink_ref[hh], F32) for hh in range(N_SWA_HEADS)], axis=0)
    brow = brow_ref[...]
    qt = qt_ref[0]
    kt = kt_ref[0]
    gam = gam_ref[...]
    seqs = range(SAMPLE_BB)
    state_shape = (N_RET_HEADS, HEAD_DIM, HEAD_DIM)

    s_old = [st_ref[b] for b in seqs]
    v_full = [jnp.concatenate([jnp.broadcast_to(v3_ref[b, h:h + 1, :], (HEAD_DIM, HEAD_DIM))
                               for h in range(N_RET_HEADS)], axis=0) for b in seqs]
    s_new = [gam * s_old[b] + kt[:, b:b + 1] * v_full[b] for b in seqs]
    for b in seqs:
        nst_ref[b] = s_new[b]
    outs = [jnp.sum((qt[:, b:b + 1] * s_new[b]).reshape(state_shape), axis=1) for b in seqs]
    for b in seqs:
        ret_scr[i * SAMPLE_BB + b] = outs[b]

    kk = [jnp.concatenate([ck_ref[b, 1:, :], knew_ref[b:b + 1, :]], axis=0) for b in seqs]
    vv = [jnp.concatenate([cv_ref[b, 1:, :], vnew_ref[b:b + 1, :]], axis=0) for b in seqs]
    for b in seqs:
        nk_ref[b] = kk[b]
        nv_ref[b] = vv[b]
    qblk = [jnp.where(own_half, jnp.concatenate([sq3_ref[b], sq3_ref[b]], axis=-1), 0.0).astype(BF16)
            for b in seqs]
    s = [lax.dot_general(qblk[b], kk[b].astype(BF16), (((1,), (1,)), ((), ())),
                         preferred_element_type=F32) + brow for b in seqs]
    m = [jnp.maximum(jnp.max(s[b], axis=-1, keepdims=True), sink_col) for b in seqs]
    p = [jnp.exp(s[b] - m[b]) for b in seqs]
    den = [jnp.sum(p[b], axis=-1, keepdims=True) + jnp.exp(sink_col - m[b]) for b in seqs]
    pn = [(p[b] / den[b]).astype(BF16) for b in seqs]
    o = [jnp.dot(pn[b], vv[b].astype(BF16), preferred_element_type=F32) for b in seqs]
    for b in seqs:
        swa_scr[i * SAMPLE_BB + b] = jnp.where(own_half[:, :HEAD_DIM], o[b][:, :HEAD_DIM], o[b][:, HEAD_DIM:])

    @pl.when(i == pl.num_programs(0) - 1)
    def _():
        y = jnp.zeros(x_ref.shape, F32)
        for h in range(N_RET_HEADS):
            o = ret_scr[:, h, :]
            mu = jnp.mean(o, axis=-1, keepdims=True)
            c = o - mu
            var = jnp.mean(c * c, axis=-1, keepdims=True)
            r = c * lax.rsqrt(var + LN_EPS) * gain_ref[h:h + 1, :] * _silu(g3_ref[h])
            y = y + jnp.dot(r.astype(BF16), wout_ref[h * HEAD_DIM:(h + 1) * HEAD_DIM, :],
                            preferred_element_type=F32)
        for hh in range(N_SWA_HEADS):
            o = swa_scr[:, hh, :].astype(BF16)
            lo = RET_W + hh * HEAD_DIM
            y = y + jnp.dot(o, wout_ref[lo:lo + HEAD_DIM, :], preferred_element_type=F32)
        x1_ref[...] = _ln(DN_ALPHA * x_ref[...] + (1.0 + gate_ref[0]) * y, lng_ref[...], lnb_ref[...])


def _mix_sample(proj, x, mod, layer, w_out_bf, gamma1, bias_row, sinks, gain8, ln_g, ln_b,
                st, ck, cv):
    q, k, _, v, g, sq, skv = proj
    nb = x.shape[0]
    steps = nb // SAMPLE_BB

    def joined(a):
        a = a.astype(F32).reshape(nb, RET_W // 128, 2, 128)
        return (a[:, :, 0, :] + a[:, :, 1, :]).reshape(nb, RET_W)

    to_cols = lambda a: joined(a).reshape(steps, SAMPLE_BB, RET_W).transpose(0, 2, 1)
    qt, kt = to_cols(q), to_cols(k)
    v3 = joined(v).reshape(nb, N_RET_HEADS, HEAD_DIM)
    sq3 = joined(sq).reshape(nb, GQA_GROUP, N_KV_HEADS, HEAD_DIM).transpose(0, 2, 1, 3)
    sq3 = sq3.reshape(nb, N_SWA_HEADS, HEAD_DIM)
    g3 = g.reshape(nb, N_RET_HEADS, HEAD_DIM).transpose(1, 0, 2)
    knew, vnew = skv[:, :SWA_KVW], skv[:, SWA_KVW:]
    st2 = st.reshape(nb, RET_W, HEAD_DIM)
    gam = jnp.broadcast_to(jnp.repeat(gamma1, HEAD_DIM)[:, None], (RET_W, HEAD_DIM))
    ck2 = ck.reshape(nb, CHUNK, SWA_KVW)
    cv2 = cv.reshape(nb, CHUNK, SWA_KVW)
    smem = pl.BlockSpec(memory_space=pltpu.SMEM)
    blk3 = lambda a, b, c: pl.BlockSpec((a, b, c), lambda i: (i, 0, 0))
    const2 = lambda a: pl.BlockSpec(a.shape, lambda i: (0, 0))
    const3 = lambda a: pl.BlockSpec(a.shape, lambda i: (0, 0, 0))
    x1, nst, nk, nv = pl.pallas_call(
        _mix_sample_kernel,
        grid=(steps,),
        in_specs=[smem, const2(gam),
                  blk3(1, RET_W, SAMPLE_BB), blk3(1, RET_W, SAMPLE_BB),
                  blk3(SAMPLE_BB, N_RET_HEADS, HEAD_DIM), blk3(SAMPLE_BB, N_SWA_HEADS, HEAD_DIM),
                  pl.BlockSpec((SAMPLE_BB, SWA_KVW), lambda i: (i, 0)),
                  pl.BlockSpec((SAMPLE_BB, SWA_KVW), lambda i: (i, 0)),
                  blk3(SAMPLE_BB, RET_W, HEAD_DIM), blk3(SAMPLE_BB, CHUNK, SWA_KVW),
                  blk3(SAMPLE_BB, CHUNK, SWA_KVW),
                  const3(g3), const2(x),
                  pl.BlockSpec((1, nb, D_MODEL), lambda i: (layer, 0, 2)),
                  const2(w_out_bf), const2(bias_row), const2(gain8), const2(ln_g), const2(ln_b)],
        out_specs=[const2(x), blk3(SAMPLE_BB, RET_W, HEAD_DIM), blk3(SAMPLE_BB, CHUNK, SWA_KVW),
                   blk3(SAMPLE_BB, CHUNK, SWA_KVW)],
        out_shape=[jax.ShapeDtypeStruct(x.shape, F32), jax.ShapeDtypeStruct(st2.shape, F32),
                   jax.ShapeDtypeStruct(ck2.shape, F32), jax.ShapeDtypeStruct(cv2.shape, F32)],
        scratch_shapes=[pltpu.VMEM((nb, N_RET_HEADS, HEAD_DIM), F32),
                        pltpu.VMEM((nb, N_SWA_HEADS, HEAD_DIM), F32)],
        compiler_params=_cparams(("arbitrary",)),
        name="mix_sample",
    )(sinks, gam, qt, kt, v3, sq3, knew, vnew, st2, ck2, cv2, g3, x, mod,
      w_out_bf, bias_row, gain8, ln_g, ln_b)
    return x1, nst.reshape(st.shape), nk.reshape(ck.shape), nv.reshape(cv.shape)


def _router_kernel(x_ref, sh_ref, sc_ref, rwt_ref, rb_ref, tri_ref,
                   oi_ref, of_ref, cnt_ref, run_ref, *, mod_rows):
    h2 = x_ref[...] * (1.0 + sc_ref[0][:mod_rows]) + sh_ref[0][:mod_rows]
    _route_rows(h2, rwt_ref, rb_ref, tri_ref, oi_ref, of_ref, cnt_ref, run_ref)


def _post_mix_kernel(mixed_ref, x_ref, gate_ref, wout_ref, lng_ref, lnb_ref,
                     sh_ref, sc_ref, rwt_ref, rb_ref, tri_ref,
                     x1_ref, oi_ref, of_ref, cnt_ref, run_ref):
    y = jnp.dot(mixed_ref[...], wout_ref[...], preferred_element_type=F32)
    x1 = _ln(DN_ALPHA * x_ref[...] + (1.0 + gate_ref[0][:1]) * y, lng_ref[...], lnb_ref[...])
    x1_ref[...] = x1
    h2 = x1 * (1.0 + sc_ref[0][:1]) + sh_ref[0][:1]
    _route_rows(h2, rwt_ref, rb_ref, tri_ref, oi_ref, of_ref, cnt_ref, run_ref)


def _route_rows(h2, rwt_ref, rb_ref, tri_ref, oi_ref, of_ref, cnt_ref, run_ref):
    i = pl.program_id(0)

    @pl.when(i == 0)
    def _():
        run_ref[...] = jnp.zeros_like(run_ref)

    logits = lax.dot_general(rwt_ref[...].astype(BF16), h2.astype(BF16), (((1,), (1,)), ((), ())),
                             preferred_element_type=F32)
    aff = 1.0 / (1.0 + jnp.exp(-logits))
    sel = aff + rb_ref[...]
    s = [sel[e:e + 1, :] for e in range(N_EXPERTS)]
    a = [aff[e:e + 1, :] for e in range(N_EXPERTS)]

    def top2sum(v0, v1, v2, v3):
        hi01, lo01 = jnp.maximum(v0, v1), jnp.minimum(v0, v1)
        hi23, lo23 = jnp.maximum(v2, v3), jnp.minimum(v2, v3)
        return jnp.maximum(hi01, hi23) + jnp.maximum(jnp.minimum(hi01, hi23),
                                                     jnp.maximum(lo01, lo23))

    def argmax_first(vals):
        best, idx = vals[0], jnp.zeros(vals[0].shape, I32)
        for j in range(1, len(vals)):
            upd = vals[j] > best
            idx = jnp.where(upd, j, idx)
            best = jnp.where(upd, vals[j], best)
        return idx

    def pick(idx, vals):
        out = vals[-1]
        for j in range(len(vals) - 2, -1, -1):
            out = jnp.where(idx == j, vals[j], out)
        return out

    gi = argmax_first([top2sum(*s[4 * g:4 * g + 4]) for g in range(N_GROUPS)])
    sv = [pick(gi, [s[4 * g + j] for g in range(N_GROUPS)]) for j in range(EXPERTS_PER_GROUP)]
    av = [pick(gi, [a[4 * g + j] for g in range(N_GROUPS)]) for j in range(EXPERTS_PER_GROUP)]
    i1 = argmax_first(sv)
    i2 = argmax_first([jnp.where(i1 == j, -jnp.inf, sv[j]) for j in range(EXPERTS_PER_GROUP)])
    w1, w2 = pick(i1, av), pick(i2, av)
    wsum = w1 + w2
    w1, w2 = w1 / wsum, w2 / wsum
    lo, hi = jnp.minimum(i1, i2), jnp.maximum(i1, i2)
    w_lo = jnp.where(i1 < i2, w1, w2)
    w_hi = jnp.where(i1 < i2, w2, w1)
    pair = jnp.where(lo == 0, hi - 1, jnp.where(lo == 1, hi + 1, 5))
    bin_id = gi * N_PAIRS + pair

    tm = bin_id.shape[1]
    onehot = lax.broadcasted_iota(I32, (BIN_ROWS, tm), 0) == bin_id
    oh_f = jnp.where(onehot, 1.0, 0.0)
    before = jnp.dot(oh_f.astype(BF16), tri_ref[...], preferred_element_type=F32)
    run = run_ref[...]
    run_t = jnp.concatenate([run] * (tm // 128), axis=-1)
    rank = jnp.sum(oh_f * (before + run_t), axis=0, keepdims=True)
    run_new = run + jnp.sum(oh_f, axis=1, keepdims=True)
    run_ref[...] = run_new
    cnt_ref[...] = run_new.astype(I32)

    zi = jnp.zeros_like(bin_id)
    oi_ref[0] = jnp.concatenate([gi * 4 + lo, gi * 4 + hi, bin_id, rank.astype(I32), zi, zi, zi, zi], axis=0)
    zf = jnp.zeros_like(w_lo)
    of_ref[0] = jnp.concatenate([w_lo, w_hi, zf, zf, zf, zf, zf, zf], axis=0)


def _router(x1, mod, layer, mod_row0, mod_rows, router_wt, router_b, tm):
    rows = x1.shape[0]
    nt = rows // tm
    mblk = 8 if mod_rows == 1 else mod_rows
    mrow = mod_row0 // mblk
    tri = jnp.asarray(np.triu(np.ones((tm, tm), np.float32), 1), BF16)
    rb = jnp.broadcast_to(router_b.astype(F32)[:, None], (N_EXPERTS, tm))
    mod_spec = lambda c: pl.BlockSpec((1, mblk, D_MODEL), lambda i: (layer, mrow, c))
    oi, of, cnt = pl.pallas_call(
        functools.partial(_router_kernel, mod_rows=mod_rows),
        grid=(nt,),
        in_specs=[pl.BlockSpec((tm, D_MODEL), lambda i: (i, 0)), mod_spec(3), mod_spec(4),
                  pl.BlockSpec((N_EXPERTS, D_MODEL), lambda i: (0, 0)),
                  pl.BlockSpec((N_EXPERTS, tm), lambda i: (0, 0)),
                  pl.BlockSpec((tm, tm), lambda i: (0, 0))],
        out_specs=[pl.BlockSpec((1, 8, tm), lambda i: (i, 0, 0)),
                   pl.BlockSpec((1, 8, tm), lambda i: (i, 0, 0)),
                   pl.BlockSpec((BIN_ROWS, 128), lambda i: (0, 0))],
        out_shape=[jax.ShapeDtypeStruct((nt, 8, tm), I32),
                   jax.ShapeDtypeStruct((nt, 8, tm), F32),
                   jax.ShapeDtypeStruct((BIN_ROWS, 128), I32)],
        scratch_shapes=[pltpu.VMEM((BIN_ROWS, 128), F32)],
        compiler_params=_cparams(("arbitrary",)),
        name="router",
    )(x1, mod, mod, router_wt, rb, tri)
    return _unpack_route(oi, of, cnt)


def _unpack_route(oi, of, cnt):
    flat = lambda a, r: a[:, r, :].reshape(-1)
    return (flat(oi, 0), flat(oi, 1), flat(oi, 2), flat(oi, 3), flat(of, 0), flat(of, 1),
            cnt[:N_BINS, 0])


POST_TM = 512


def _post_mix(mixed, x, mod, layer, mod_row0, w_out_bf, ln_g, ln_b, router_wt, router_b):
    rows = x.shape[0]
    tm = POST_TM
    nt = rows // tm
    mrow = mod_row0 // 8
    tri = jnp.asarray(np.triu(np.ones((tm, tm), np.float32), 1), BF16)
    rb = jnp.broadcast_to(router_b.astype(F32)[:, None], (N_EXPERTS, tm))
    mod_spec = lambda c: pl.BlockSpec((1, 8, D_MODEL), lambda i: (layer, mrow, c))
    row_spec = pl.BlockSpec((tm, D_MODEL), lambda i: (i, 0))
    const2 = lambda a: pl.BlockSpec(a.shape, lambda i: (0, 0))
    x1, oi, of, cnt = pl.pallas_call(
        _post_mix_kernel,
        grid=(nt,),
        in_specs=[row_spec, row_spec, mod_spec(2), const2(w_out_bf), const2(ln_g), const2(ln_b),
                  mod_spec(3), mod_spec(4), const2(router_wt), const2(rb), const2(tri)],
        out_specs=[row_spec,
                   pl.BlockSpec((1, 8, tm), lambda i: (i, 0, 0)),
                   pl.BlockSpec((1, 8, tm), lambda i: (i, 0, 0)),
                   pl.BlockSpec((BIN_ROWS, 128), lambda i: (0, 0))],
        out_shape=[jax.ShapeDtypeStruct((rows, D_MODEL), F32),
                   jax.ShapeDtypeStruct((nt, 8, tm), I32),
                   jax.ShapeDtypeStruct((nt, 8, tm), F32),
                   jax.ShapeDtypeStruct((BIN_ROWS, 128), I32)],
        scratch_shapes=[pltpu.VMEM((BIN_ROWS, 128), F32)],
        compiler_params=_cparams(("arbitrary",)),
        name="post_mix",
    )(mixed, x, mod, w_out_bf, ln_g, ln_b, mod, mod, router_wt, rb, tri)
    return x1, _unpack_route(oi, of, cnt)


DMA_UNROLL = 8
ROW_GROUPS = MOE_TM // DMA_UNROLL


def _for_rows(n, fn):
    def body_u(j, carry):
        for u in range(DMA_UNROLL):
            fn(j, u)
        return carry
    lax.fori_loop(0, n // DMA_UNROLL, body_u, 0)

    def body_1(r, carry):
        fn(r // DMA_UNROLL, r % DMA_UNROLL)
        return carry
    lax.fori_loop((n // DMA_UNROLL) * DMA_UNROLL, n, body_1, 0)


def _wait_rows(n, buf, sem):
    p = ROW_GROUPS
    while p >= 1:
        @pl.when((n & (p * DMA_UNROLL)) != 0)
        def _(p=p):
            pltpu.make_async_copy(buf.at[pl.ds(0, p)], buf.at[pl.ds(0, p)], sem).wait()
        p //= 2
    p = DMA_UNROLL // 2
    while p >= 1:
        @pl.when((n & p) != 0)
        def _(p=p):
            pltpu.make_async_copy(buf.at[0, pl.ds(0, p)], buf.at[0, pl.ds(0, p)], sem).wait()
        p //= 2


def _moe_sorted_kernel(ea_ref, eb_ref, nvalid_ref, start_ref, bin_ref, rank_ref,
                       x_hbm, sh_ref, sc_ref, gate_ref, rw_ref,
                       wga_ref, wua_ref, wda_ref, wgb_ref, wub_ref, wdb_ref,
                       lng_ref, lnb_ref,
                       out_hbm,
                       inv, xbuf, obuf, wg_a, wu_a, wd_a, wg_b, wu_b, wd_b, gsem, ssem):
    i = pl.program_id(0)
    nt = pl.num_programs(0)
    slot = i % 2
    n_tok = bin_ref.shape[0]

    def start_gather(t, dst_slot):
        def one(j, u):
            tok = inv[t * MOE_TM + j * DMA_UNROLL + u]
            pltpu.make_async_copy(x_hbm.at[pl.ds(tok, 1)], xbuf.at[dst_slot, j, pl.ds(u, 1)],
                                  gsem.at[dst_slot]).start()
        _for_rows(nvalid_ref[t], one)

    @pl.when(i == 0)
    def _():
        def body(j, carry):
            for u in range(DMA_UNROLL):
                t = j * DMA_UNROLL + u
                inv[start_ref[bin_ref[t]] + rank_ref[t]] = t
            return carry
        lax.fori_loop(0, n_tok // DMA_UNROLL, body, 0)
        xbuf[...] = jnp.zeros_like(xbuf)
        start_gather(0, 0)

    @pl.when(i >= 2)
    def _():
        _wait_rows(nvalid_ref[jnp.maximum(i - 2, 0)], obuf.at[slot], ssem.at[slot])

    @pl.when(nvalid_ref[i] > 0)
    def _():
        _wait_rows(nvalid_ref[i], xbuf.at[slot], gsem.at[slot])

        @pl.when(i + 1 < nt)
        def _():
            start_gather(jnp.minimum(i + 1, nt - 1), 1 - slot)

        @pl.when(jnp.logical_or(i == 0, ea_ref[i] != ea_ref[jnp.maximum(i - 1, 0)]))
        def _():
            wg_a[...] = wga_ref[0, 0].astype(BF16)
            wu_a[...] = wua_ref[0, 0].astype(BF16)
            wd_a[...] = wda_ref[0, 0].astype(BF16)

        @pl.when(jnp.logical_or(i == 0, eb_ref[i] != eb_ref[jnp.maximum(i - 1, 0)]))
        def _():
            wg_b[...] = wgb_ref[0, 0].astype(BF16)
            wu_b[...] = wub_ref[0, 0].astype(BF16)
            wd_b[...] = wdb_ref[0, 0].astype(BF16)

        x = xbuf[slot].reshape(MOE_TM, D_MODEL)
        h2 = (x * (1.0 + sc_ref[0][:1]) + sh_ref[0][:1]).astype(BF16)

        dot = functools.partial(jnp.dot, preferred_element_type=F32)
        logits = dot(h2, rw_ref[...])
        gate_a, gate_b = dot(h2, wg_a[...]), dot(h2, wg_b[...])
        up_a, up_b = dot(h2, wu_a[...]), dot(h2, wu_b[...])
        act_a = (_silu(gate_a) * up_a).astype(BF16)
        act_b = (_silu(gate_b) * up_b).astype(BF16)
        y_a, y_b = dot(act_a, wd_a[...]), dot(act_b, wd_b[...])

        aff = 1.0 / (1.0 + jnp.exp(-logits))
        lane = lax.broadcasted_iota(I32, aff.shape, 1)
        a_lo = jnp.sum(jnp.where(lane == ea_ref[i], aff, 0.0), axis=-1, keepdims=True)
        a_hi = jnp.sum(jnp.where(lane == eb_ref[i], aff, 0.0), axis=-1, keepdims=True)
        a_sum = a_lo + a_hi
        y = (a_lo / a_sum) * y_a + (a_hi / a_sum) * y_b
        out = _ln(DN_ALPHA * x + (1.0 + gate_ref[0][:1]) * y, lng_ref[...], lnb_ref[...])
        obuf[slot] = out.reshape(ROW_GROUPS, DMA_UNROLL, D_MODEL)

        def scatter_one(j, u):
            tok = inv[i * MOE_TM + j * DMA_UNROLL + u]
            pltpu.make_async_copy(obuf.at[slot, j, pl.ds(u, 1)], out_hbm.at[pl.ds(tok, 1)],
                                  ssem.at[slot]).start()
        _for_rows(nvalid_ref[i], scatter_one)

    @pl.when(i == nt - 1)
    def _():
        _wait_rows(nvalid_ref[jnp.maximum(i - 1, 0)], obuf.at[1 - slot], ssem.at[1 - slot])
        _wait_rows(nvalid_ref[i], obuf.at[slot], ssem.at[slot])


def _moe_sorted(x1, mod, layer, mod_row0, route, rw_pad, wg, wu, wd, ln_g, ln_b):
    _, _, bin_id, rank, _, _, counts = route
    rows = x1.shape[0]
    nt = rows // MOE_TM + N_BINS
    ns = nt * MOE_TM
    tiles = (counts + MOE_TM - 1) // MOE_TM
    tile_end = jnp.cumsum(tiles)
    row_start = ((tile_end - tiles) * MOE_TM).astype(I32)
    total = tile_end[-1]
    t = jnp.arange(nt, dtype=I32)
    tile_bin = jnp.sum(jnp.minimum(t, total - 1)[:, None] >= tile_end[None, :], axis=1).astype(I32)
    tile_bin = jnp.clip(tile_bin, 0, N_BINS - 1)
    bin_lo = jnp.asarray([g * 4 + _PAIR_LO[p] for g in range(N_GROUPS) for p in range(N_PAIRS)], I32)
    bin_hi = jnp.asarray([g * 4 + _PAIR_HI[p] for g in range(N_GROUPS) for p in range(N_PAIRS)], I32)
    tile_ea, tile_eb = bin_lo[tile_bin], bin_hi[tile_bin]
    nvalid = jnp.clip(counts[tile_bin] - (t - (tile_end - tiles)[tile_bin]) * MOE_TM, 0, MOE_TM)
    nvalid = jnp.where(t < total, nvalid, 0).astype(I32)
    mrow = mod_row0 // 8
    mod_spec = lambda c: pl.BlockSpec((1, 8, D_MODEL), lambda i, *_: (layer, mrow, c))
    w_spec = lambda shape, ref_idx: pl.BlockSpec(
        (1, 1) + shape, lambda i, ea_r, eb_r, *_: (layer, (ea_r, eb_r)[ref_idx][i], 0, 0))
    any_spec = pl.BlockSpec(memory_space=pl.ANY)
    w_bf = lambda shape: pltpu.VMEM(shape, BF16)
    grid_spec = pltpu.PrefetchScalarGridSpec(
        num_scalar_prefetch=6,
        grid=(nt,),
        in_specs=[any_spec,
                  mod_spec(3), mod_spec(4), mod_spec(5),
                  pl.BlockSpec((D_MODEL, 128), lambda i, *_: (0, 0)),
                  w_spec((D_MODEL, D_FF), 0), w_spec((D_MODEL, D_FF), 0), w_spec((D_FF, D_MODEL), 0),
                  w_spec((D_MODEL, D_FF), 1), w_spec((D_MODEL, D_FF), 1), w_spec((D_FF, D_MODEL), 1),
                  pl.BlockSpec((1, D_MODEL), lambda i, *_: (0, 0)),
                  pl.BlockSpec((1, D_MODEL), lambda i, *_: (0, 0))],
        out_specs=any_spec,
        scratch_shapes=[pltpu.SMEM((ns,), I32),
                        pltpu.VMEM((2, ROW_GROUPS, DMA_UNROLL, D_MODEL), F32),
                        pltpu.VMEM((2, ROW_GROUPS, DMA_UNROLL, D_MODEL), F32),
                        w_bf((D_MODEL, D_FF)), w_bf((D_MODEL, D_FF)), w_bf((D_FF, D_MODEL)),
                        w_bf((D_MODEL, D_FF)), w_bf((D_MODEL, D_FF)), w_bf((D_FF, D_MODEL)),
                        pltpu.SemaphoreType.DMA((2,)),
                        pltpu.SemaphoreType.DMA((2,))],
    )
    return pl.pallas_call(
        _moe_sorted_kernel,
        grid_spec=grid_spec,
        out_shape=jax.ShapeDtypeStruct((rows, D_MODEL), F32),
        compiler_params=_cparams(("arbitrary",)),
        name="moe_sorted",
    )(tile_ea, tile_eb, nvalid, row_start, bin_id, rank, x1, mod, mod, mod, rw_pad,
      wg, wu, wd, wg, wu, wd, ln_g, ln_b)


def _moe_dense_kernel(x_ref, dw_ref, sh_ref, sc_ref, gate_ref, wg_ref, wu_ref, wd_ref,
                      lng_ref, lnb_ref, o_ref, acc):
    e = pl.program_id(0)

    @pl.when(e == 0)
    def _():
        acc[...] = jnp.zeros_like(acc)

    x = x_ref[...]
    h2 = (x * (1.0 + sc_ref[0]) + sh_ref[0]).astype(BF16)
    a = _silu(jnp.dot(h2, wg_ref[0, 0].astype(BF16), preferred_element_type=F32)) * \
        jnp.dot(h2, wu_ref[0, 0].astype(BF16), preferred_element_type=F32)
    y = jnp.dot(a.astype(BF16), wd_ref[0, 0].astype(BF16), preferred_element_type=F32)
    acc[...] = acc[...] + dw_ref[0][:, :1] * y

    @pl.when(e == pl.num_programs(0) - 1)
    def _():
        o_ref[...] = _ln(DN_ALPHA * x + (1.0 + gate_ref[0]) * acc[...], lng_ref[...], lnb_ref[...])


def _moe_dense(x1, mod, layer, route, wg, wu, wd, ln_g, ln_b):
    ea, eb, _, _, w_lo, w_hi, _ = route
    nb = x1.shape[0]
    eids = jnp.arange(N_EXPERTS, dtype=I32)[:, None]
    dw = jnp.where(eids == ea[None, :], w_lo[None, :], 0.0) + \
        jnp.where(eids == eb[None, :], w_hi[None, :], 0.0)
    dw = jnp.broadcast_to(dw[:, :, None], (N_EXPERTS, nb, 128))
    mod_spec = lambda c: pl.BlockSpec((1, nb, D_MODEL), lambda e: (layer, 0, c))
    const2 = lambda a: pl.BlockSpec(a.shape, lambda e: (0, 0))
    return pl.pallas_call(
        _moe_dense_kernel,
        grid=(N_EXPERTS,),
        in_specs=[const2(x1), pl.BlockSpec((1, nb, 128), lambda e: (e, 0, 0)),
                  mod_spec(3), mod_spec(4), mod_spec(5),
                  pl.BlockSpec((1, 1, D_MODEL, D_FF), lambda e: (layer, e, 0, 0)),
                  pl.BlockSpec((1, 1, D_MODEL, D_FF), lambda e: (layer, e, 0, 0)),
                  pl.BlockSpec((1, 1, D_FF, D_MODEL), lambda e: (layer, e, 0, 0)),
                  const2(ln_g), const2(ln_b)],
        out_specs=const2(x1),
        out_shape=jax.ShapeDtypeStruct(x1.shape, F32),
        scratch_shapes=[pltpu.VMEM(x1.shape, F32)],
        compiler_params=_cparams(("arbitrary",)),
        name="moe_dense",
    )(x1, dw, mod, mod, mod, wg, wu, wd, ln_g, ln_b)


def kernel(x_prompt, x_sample, state_ret, cache_swa_k, cache_swa_v, c_prompt, c_sample, w_in, w_out, ret_gn_gain, swa_sinks, rel_bias_table, ada_w, ada_b, ln1_g, ln1_b, ln2_g, ln2_b, router_w, router_b, exp_w_gate, exp_w_up, exp_w_down):
    seq = x_prompt.shape[1]
    nb = x_sample.shape[0]
    past_len = 16384
    assert x_prompt.shape[0] == 1 and x_sample.shape[1] == 1

    c_all = jnp.concatenate([c_sample, c_prompt, jnp.zeros((7, D_MODEL), F32)], axis=0)
    mod = _ada(c_all, ada_w, ada_b)
    prompt_row = nb

    bias_tabs = _bias_tables(rel_bias_table.astype(F32))
    gl, decay, xi, zeta, gamma1 = _ret_consts()
    rope_p = _rope_tables(jnp.arange(seq, dtype=I32))
    rope_s = _rope_tables(jnp.full((nb,), past_len, I32))
    router_wt = router_w.astype(F32).T
    rw_pad = jnp.pad(router_w.astype(BF16), ((0, 0), (0, 128 - N_EXPERTS)))
    zeta_p = jnp.tile(zeta, (INPROJ_TM // CHUNK, 1))
    zeta_s = jnp.ones((nb, RET_W), F32)

    xp = x_prompt.reshape(seq, D_MODEL)
    xs = x_sample.reshape(nb, D_MODEL)
    st_p, k_p, v_p, st_s, k_s, v_s = [], [], [], [], [], []
    for l in range(DEPTH):
        w_in_bf = w_in[l].astype(BF16)
        w_out_bf = w_out[l].astype(BF16)
        order = jnp.asarray(_SWA_HEAD_ORDER)
        swa_rows = w_out[l][RET_W:].reshape(N_SWA_HEADS, HEAD_DIM, D_MODEL)[order]
        w_out_prompt = jnp.concatenate([w_out[l][:RET_W], swa_rows.reshape(SWA_QW, D_MODEL)],
                                       axis=0).astype(BF16)
        experts = (exp_w_gate.astype(F32), exp_w_up.astype(F32), exp_w_down.astype(F32))
        gain = ret_gn_gain[l].astype(F32)
        sinks = swa_sinks[l].astype(F32)
        row = lambda a: a[l].astype(F32).reshape(1, D_MODEL)

        proj = _inproj(xp, mod, l, prompt_row, 1, w_in_bf, rope_p, zeta_p, INPROJ_TM)
        mixed, st = _mix_prompt(proj, (gl, decay, xi), bias_tabs, sinks, gain.reshape(1, RET_W))
        skv = proj[6]
        st_p.append(st.reshape(1, N_RET_HEADS, HEAD_DIM, HEAD_DIM))
        k_p.append(skv[seq - CHUNK:, :SWA_KVW].reshape(1, CHUNK, N_KV_HEADS, HEAD_DIM))
        v_p.append(skv[seq - CHUNK:, SWA_KVW:].reshape(1, CHUNK, N_KV_HEADS, HEAD_DIM))
        x1, route = _post_mix(mixed, xp, mod, l, prompt_row, w_out_prompt, row(ln1_g), row(ln1_b),
                              router_wt, router_b)
        xp = _moe_sorted(x1, mod, l, prompt_row, route, rw_pad, *experts, row(ln2_g), row(ln2_b))

        proj = _inproj(xs, mod, l, 0, nb, w_in_bf, rope_s, zeta_s, nb)
        x1, nst, nk, nv = _mix_sample(proj, xs, mod, l, w_out_bf, gamma1, bias_tabs[1], sinks, gain,
                                      row(ln1_g), row(ln1_b), state_ret[l].astype(F32),
                                      cache_swa_k[l], cache_swa_v[l])
        st_s.append(nst)
        k_s.append(nk)
        v_s.append(nv)
        route = _router(x1, mod, l, 0, nb, router_wt, router_b, nb)
        xs = _moe_dense(x1, mod, l, route, *experts, row(ln2_g), row(ln2_b))

    return (xp.reshape(1, seq, D_MODEL), xs.reshape(nb, 1, D_MODEL),
            jnp.stack(st_p), jnp.stack(k_p), jnp.stack(v_p),
            jnp.stack(st_s), jnp.stack(k_s), jnp.stack(v_s))
```

```python
import functools
import math

import numpy as np
import jax
import jax.numpy as jnp
from jax import lax
from jax.experimental import pallas as pl
from jax.experimental.pallas import tpu as pltpu

F32 = jnp.float32
BF16 = jnp.bfloat16
I32 = jnp.int32

D_MODEL = 1024
DEPTH = 2
HEAD_DIM = 64
N_RET_HEADS = 8
N_SWA_HEADS = 8
N_KV_HEADS = 2
GQA_GROUP = N_SWA_HEADS // N_KV_HEADS
RET_W = N_RET_HEADS * HEAD_DIM
SWA_QW = N_SWA_HEADS * HEAD_DIM
SWA_KVW = N_KV_HEADS * HEAD_DIM
PROJ_W = 4 * RET_W + SWA_QW + 2 * SWA_KVW
CHUNK = 128
ROPE_BASE = 10000.0
N_BUCKETS = 32
MAX_DISTANCE = 128
N_EXPERTS = 16
N_GROUPS = 4
EXPERTS_PER_GROUP = 4
D_FF = 512
LN_EPS = 1e-5
DN_ALPHA = (2 * DEPTH) ** 0.25
QK_SCALE = HEAD_DIM ** -0.5
NEG_BIG = -1e30

N_PAIRS = 6
N_BINS = N_GROUPS * N_PAIRS
BIN_ROWS = 32
MOE_TM = 256
INPROJ_TM = 512
MIX_SUB = 2
SPLIT_W = 128 * N_RET_HEADS
_SWA_HEAD_ORDER = (0, 4, 1, 5, 2, 6, 3, 7)
VMEM_LIMIT = 56 * 1024 * 1024

_PAIR_LO = (0, 0, 0, 1, 1, 2)
_PAIR_HI = (1, 2, 3, 2, 3, 3)


def _cparams(sem):
    return pltpu.CompilerParams(dimension_semantics=sem, vmem_limit_bytes=VMEM_LIMIT)


def _ln(v, g, b):
    mu = jnp.mean(v, axis=-1, keepdims=True)
    c = v - mu
    var = jnp.mean(c * c, axis=-1, keepdims=True)
    return c * lax.rsqrt(var + LN_EPS) * g + b


def _silu(v):
    return v * (1.0 / (1.0 + jnp.exp(-v)))


def _ada_kernel(c_ref, w_ref, b_ref, o_ref):
    o_ref[0] = jnp.dot(c_ref[...].astype(BF16), w_ref[0].astype(BF16),
                       preferred_element_type=F32) + b_ref[0]


def _ada(c_all, ada_w, ada_b):
    rows = c_all.shape[0]
    nt = 6 * D_MODEL // 1024
    return pl.pallas_call(
        _ada_kernel,
        grid=(DEPTH, nt),
        in_specs=[pl.BlockSpec((rows, D_MODEL), lambda l, j: (0, 0)),
                  pl.BlockSpec((1, D_MODEL, 1024), lambda l, j: (l, 0, j)),
                  pl.BlockSpec((1, 1, 1024), lambda l, j: (l, 0, j))],
        out_specs=pl.BlockSpec((1, rows, 1024), lambda l, j: (l, 0, j)),
        out_shape=jax.ShapeDtypeStruct((DEPTH, rows, 6 * D_MODEL), F32),
        compiler_params=_cparams(("arbitrary", "arbitrary")),
        name="ada",
    )(c_all, ada_w, ada_b.reshape(DEPTH, 1, 6 * D_MODEL))


def _bias_kernel(tab_ref, bkt_ref, fold_ref, row_ref):
    bkt = bkt_ref[...]
    rows = lax.broadcasted_iota(I32, (CHUNK, CHUNK), 0)
    cols = lax.broadcasted_iota(I32, (CHUNK, CHUNK), 1)
    from_prev = cols > rows
    for h in range(N_SWA_HEADS):
        acc = jnp.zeros(bkt.shape, F32)
        for b in range(N_BUCKETS):
            acc = jnp.where(bkt == b, tab_ref[b, h], acc)
        own = acc[:, CHUNK:]
        fold_ref[0, h] = jnp.where(from_prev, NEG_BIG, own)
        fold_ref[1, h] = jnp.where(from_prev, acc[:, :CHUNK], own)
        row_ref[h:h + 1, :] = own[CHUNK - 1:CHUNK, :]


def _t5_bucket(rel):
    max_exact = N_BUCKETS // 2
    relf = jnp.maximum(rel, 1).astype(F32)
    large = max_exact + (jnp.log(relf / max_exact) / math.log(MAX_DISTANCE / max_exact)
                         * (N_BUCKETS - max_exact)).astype(I32)
    large = jnp.minimum(large, N_BUCKETS - 1)
    return jnp.where(rel < max_exact, rel, large)


def _bias_tables(rel_bias_table):
    qi = jnp.arange(CHUNK)
    si = jnp.arange(2 * CHUNK)
    rel = CHUNK + qi[:, None] - si[None, :]
    bkt = _t5_bucket(jnp.maximum(rel, 0)).astype(I32)
    return pl.pallas_call(
        _bias_kernel,
        in_specs=[pl.BlockSpec(memory_space=pltpu.SMEM),
                  pl.BlockSpec((CHUNK, 2 * CHUNK), lambda: (0, 0))],
        out_specs=[pl.BlockSpec((2, N_SWA_HEADS, CHUNK, CHUNK), lambda: (0, 0, 0, 0)),
                   pl.BlockSpec((N_SWA_HEADS, CHUNK), lambda: (0, 0))],
        out_shape=[jax.ShapeDtypeStruct((2, N_SWA_HEADS, CHUNK, CHUNK), F32),
                   jax.ShapeDtypeStruct((N_SWA_HEADS, CHUNK), F32)],
        name="t5_bias",
    )(rel_bias_table, bkt)


def _rotary(v, cos, s_lo, s_hi):
    outs = []
    for j in range(RET_W // 128):
        blk = v[:, j * 128:(j + 1) * 128]
        outs.append(blk * cos + pltpu.roll(blk, 96, 1) * s_lo + pltpu.roll(blk, 32, 1) * s_hi)
    return jnp.concatenate(outs, axis=-1)


def _inproj_kernel(x_ref, sh_ref, sc_ref, w_ref, cs_ref, zeta_ref,
                   q_ref, k_ref, kz_ref, v_ref, g_ref, sq_ref, skv_ref, *, mod_rows):
    sh = sh_ref[0][:mod_rows]
    sc = sc_ref[0][:mod_rows]
    h = (x_ref[...] * (1.0 + sc) + sh).astype(BF16)
    cs = jnp.concatenate([cs_ref[...], cs_ref[...]], axis=-1)
    first = (lax.broadcasted_iota(I32, cs.shape, 1) % HEAD_DIM) < (HEAD_DIM // 2)
    cos = jnp.where(first, cs, pltpu.roll(cs, 32, 1))
    s_lo = jnp.where(first, -pltpu.roll(cs, 96, 1), 0.0)
    s_hi = jnp.where(first, 0.0, cs)

    def proj(lo, hi):
        return jnp.dot(h, w_ref[:, lo:hi], preferred_element_type=F32)

    left = lax.broadcasted_iota(I32, cs.shape, 1) < HEAD_DIM

    def store_split(ref, val):
        for b in range(RET_W // 128):
            blk = val[:, b * 128:(b + 1) * 128]
            ref[:, (2 * b) * 128:(2 * b + 1) * 128] = jnp.where(left, blk, 0.0).astype(BF16)
            ref[:, (2 * b + 1) * 128:(2 * b + 2) * 128] = jnp.where(left, 0.0, blk).astype(BF16)

    store_split(q_ref, _rotary(proj(0, RET_W), cos, s_lo, s_hi))
    k = _rotary(proj(RET_W, 2 * RET_W), cos, s_lo, s_hi) * QK_SCALE
    store_split(k_ref, k)
    store_split(kz_ref, k * zeta_ref[...])
    store_split(v_ref, proj(2 * RET_W, 3 * RET_W))
    g_ref[...] = proj(3 * RET_W, 4 * RET_W)
    sq = proj(4 * RET_W, 4 * RET_W + SWA_QW) * QK_SCALE
    for hh in range(N_SWA_HEADS):
        kv, j = divmod(hh, GQA_GROUP)
        blk = sq[:, (hh // 2) * 128:(hh // 2 + 1) * 128]
        if hh % 2 != kv:
            blk = pltpu.roll(blk, HEAD_DIM, 1)
        keep = left if kv == 0 else jnp.logical_not(left)
        sq_ref[:, (2 * j + kv) * 128:(2 * j + kv + 1) * 128] = jnp.where(keep, blk, 0.0).astype(BF16)
    skv_ref[...] = proj(4 * RET_W + SWA_QW, PROJ_W)


def _inproj(x, mod, layer, mod_row0, mod_rows, w_in_bf, rope, zeta_tile, tm):
    rows = x.shape[0]
    mblk = 8 if mod_rows == 1 else mod_rows
    mrow = mod_row0 // mblk
    row_spec = lambda w: pl.BlockSpec((tm, w), lambda i: (i, 0))
    mod_spec = lambda c: pl.BlockSpec((1, mblk, D_MODEL), lambda i: (layer, mrow, c))
    return pl.pallas_call(
        functools.partial(_inproj_kernel, mod_rows=mod_rows),
        grid=(rows // tm,),
        in_specs=[row_spec(D_MODEL), mod_spec(0), mod_spec(1),
                  pl.BlockSpec((D_MODEL, PROJ_W), lambda i: (0, 0)),
                  row_spec(HEAD_DIM),
                  pl.BlockSpec((tm, RET_W), lambda i: (0, 0))],
        out_specs=[row_spec(SPLIT_W), row_spec(SPLIT_W), row_spec(SPLIT_W), row_spec(SPLIT_W),
                   row_spec(RET_W), row_spec(SPLIT_W), row_spec(2 * SWA_KVW)],
        out_shape=[jax.ShapeDtypeStruct((rows, SPLIT_W), BF16),
                   jax.ShapeDtypeStruct((rows, SPLIT_W), BF16),
                   jax.ShapeDtypeStruct((rows, SPLIT_W), BF16),
                   jax.ShapeDtypeStruct((rows, SPLIT_W), BF16),
                   jax.ShapeDtypeStruct((rows, RET_W), F32),
                   jax.ShapeDtypeStruct((rows, SPLIT_W), BF16),
                   jax.ShapeDtypeStruct((rows, 2 * SWA_KVW), F32)],
        compiler_params=_cparams(("arbitrary",)),
        name="inproj",
    )(x, mod, mod, w_in_bf, rope, zeta_tile)


def _rope_tables(pos):
    half = HEAD_DIM // 2
    inv = ROPE_BASE ** (-jnp.arange(half, dtype=F32) / half)
    ang = pos.astype(F32)[:, None] * inv[None, :]
    cos, sin = jnp.cos(ang), jnp.sin(ang)
    return jnp.concatenate([cos, sin], axis=-1)


def _mix_prompt_kernel(gl_ref, sink_ref,
                       q_ref, k_ref, kz_ref, v_ref, g_ref, sq_ref, kvc_ref, kvp_ref,
                       decay_ref, xi_ref, bias_ref, gain_ref,
                       mixed_ref, st_ref, state):
    i = pl.program_id(0)

    @pl.when(i == 0)
    def _():
        state[...] = jnp.zeros_like(state)

    gain = gain_ref[...]
    left = lax.broadcasted_iota(I32, (CHUNK, 128), 1) < HEAD_DIM
    from_prev = (lax.broadcasted_iota(I32, (CHUNK, CHUNK), 1) >
                 lax.broadcasted_iota(I32, (CHUNK, CHUNK), 0))
    inv_dim = 1.0 / HEAD_DIM

    for sub in range(MIX_SUB):
        rows = slice(sub * CHUNK, (sub + 1) * CHUNK)

        heads = range(N_RET_HEADS)
        hsl = [slice(h * 128, (h + 1) * 128) for h in heads]
        scores_r = [lax.dot_general(q_ref[rows, hsl[h]], k_ref[rows, hsl[h]], (((1,), (1,)), ((), ())),
                                    preferred_element_type=F32) for h in heads]
        s_olds = [state[h] for h in heads]
        inter = [jnp.dot(q_ref[rows, hsl[h]], s_olds[h].astype(BF16), preferred_element_type=F32)
                 for h in heads]
        upds = [lax.dot_general(kz_ref[rows, hsl[h]], v_ref[rows, hsl[h]], (((0,), (0,)), ((), ())),
                                preferred_element_type=F32) for h in heads]
        for h in heads:
            state[h] = gl_ref[h] * s_olds[h] + upds[h]
        probs_r = [(scores_r[h] * decay_ref[h]).astype(BF16) for h in heads]
        intra = [jnp.dot(probs_r[h], v_ref[rows, hsl[h]], preferred_element_type=F32) for h in heads]

        owns = [left if h % 2 == 0 else jnp.logical_not(left) for h in heads]
        outs = [intra[h] + inter[h] * xi_ref[:, hsl[h]] for h in heads]
        mus = [jnp.sum(outs[h], axis=-1, keepdims=True) * inv_dim for h in heads]
        cen = [jnp.where(owns[h], outs[h] - mus[h], 0.0) for h in heads]
        var = [jnp.sum(cen[h] * cen[h], axis=-1, keepdims=True) * inv_dim for h in heads]
        nrm = [cen[h] * lax.rsqrt(var[h] + LN_EPS) for h in heads]
        ret_blocks = [nrm[2 * b] + nrm[2 * b + 1] for b in range(N_RET_HEADS // 2)]
        ret_o = jnp.concatenate(ret_blocks, axis=-1) * gain * _silu(g_ref[rows, :])

        kvc = kvc_ref[rows, :]
        kvp = kvp_ref[...] if sub == 0 else kvc_ref[(sub - 1) * CHUNK:sub * CHUNK, :]
        kcat = jnp.concatenate([kvp[:, :SWA_KVW], kvc[:, :SWA_KVW]], axis=0)
        vcat = jnp.concatenate([kvp[:, SWA_KVW:], kvc[:, SWA_KVW:]], axis=0)
        left2 = jnp.concatenate([left, left], axis=0)
        table = jnp.minimum(i, 1) if sub == 0 else 1
        swa_blocks = [None] * GQA_GROUP
        kks, vvs = [], []
        for kv in range(N_KV_HEADS):
            own2 = left2 if kv == 0 else jnp.logical_not(left2)
            kks.append(jnp.where(own2, kcat, 0.0).astype(BF16))
            vvs.append(jnp.where(own2, vcat, 0.0).astype(BF16))
        scores = []
        for hh in range(N_SWA_HEADS):
            kv, j = divmod(hh, GQA_GROUP)
            blk = 2 * j + kv
            qh = sq_ref[rows, blk * 128:(blk + 1) * 128]
            scores.append(lax.dot_general(qh, kks[kv], (((1,), (1,)), ((), ())),
                                          preferred_element_type=F32))
        probs = []
        for hh in range(N_SWA_HEADS):
            s2 = scores[hh]
            s = jnp.where(from_prev, s2[:, :CHUNK], s2[:, CHUNK:]) + bias_ref[table, hh]
            sink = sink_ref[hh]
            m = jnp.maximum(jnp.max(s, axis=-1, keepdims=True), sink)
            p = jnp.exp(s - m)
            den = jnp.sum(p, axis=-1, keepdims=True) + jnp.exp(sink - m)
            pn = p / den
            p2 = jnp.concatenate([jnp.where(from_prev, pn, 0.0), jnp.where(from_prev, 0.0, pn)], axis=-1)
            probs.append(p2.astype(BF16))
        for hh in range(N_SWA_HEADS):
            kv, j = divmod(hh, GQA_GROUP)
            o = jnp.dot(probs[hh], vvs[kv], preferred_element_type=F32)
            swa_blocks[j] = o if kv == 0 else swa_blocks[j] + o
        swa_o = jnp.concatenate(swa_blocks, axis=-1)

        mixed_ref[rows, :] = jnp.concatenate([ret_o, swa_o], axis=-1).astype(BF16)

    @pl.when(i == pl.num_programs(0) - 1)
    def _():
        st_ref[...] = state[...]


def _mix_prompt(proj, ret_consts, bias_tabs, sinks, gain):
    q, k, kz, v, g, sq, skv = proj
    rows = q.shape[0]
    tm = MIX_SUB * CHUNK
    gl, decay, xi = ret_consts
    bias_fold = bias_tabs[0]
    row_spec = lambda w: pl.BlockSpec((tm, w), lambda i: (i, 0))
    const2 = lambda a: pl.BlockSpec(a.shape, lambda i: (0, 0))
    const3 = lambda a: pl.BlockSpec(a.shape, lambda i: (0, 0, 0))
    smem = pl.BlockSpec(memory_space=pltpu.SMEM)
    mixed, st_full = pl.pallas_call(
        _mix_prompt_kernel,
        grid=(rows // tm,),
        in_specs=[smem, smem,
                  row_spec(SPLIT_W), row_spec(SPLIT_W), row_spec(SPLIT_W), row_spec(SPLIT_W), row_spec(RET_W),
                  row_spec(SPLIT_W), row_spec(2 * SWA_KVW),
                  pl.BlockSpec((CHUNK, 2 * SWA_KVW), lambda i: (jnp.maximum(MIX_SUB * i - 1, 0), 0)),
                  const3(decay), const2(xi),
                  pl.BlockSpec(bias_fold.shape, lambda i: (0, 0, 0, 0)),
                  const2(gain)],
        out_specs=[row_spec(D_MODEL),
                   pl.BlockSpec((N_RET_HEADS, 128, 128), lambda i: (0, 0, 0))],
        out_shape=[jax.ShapeDtypeStruct((rows, D_MODEL), BF16),
                   jax.ShapeDtypeStruct((N_RET_HEADS, 128, 128), F32)],
        scratch_shapes=[pltpu.VMEM((N_RET_HEADS, 128, 128), F32)],
        compiler_params=_cparams(("arbitrary",)),
        name="mix_prompt",
    )(gl, sinks, q, k, kz, v, g, sq, skv, skv, decay, xi, bias_fold, gain)
    lo, hi = slice(0, HEAD_DIM), slice(HEAD_DIM, 128)
    st = jnp.stack([st_full[h, lo, lo] if h % 2 == 0 else st_full[h, hi, hi] for h in range(N_RET_HEADS)])
    return mixed, st


def _ret_consts():
    lg = jnp.log(1.0 - 2.0 ** (-5.0 - jnp.arange(N_RET_HEADS, dtype=F32)))
    idx = jnp.arange(CHUNK, dtype=F32)
    diff = idx[:, None] - idx[None, :]
    decay = jnp.where(diff >= 0, jnp.exp(jnp.maximum(diff, 0.0)[None] * lg[:, None, None]), 0.0)
    xi = jnp.exp((idx + 1.0)[:, None] * lg[None, :])
    zeta = jnp.exp((CHUNK - 1.0 - idx)[:, None] * lg[None, :])
    gl = jnp.exp(CHUNK * lg)
    return gl, decay, jnp.repeat(xi, 128, axis=1), jnp.repeat(zeta, HEAD_DIM, axis=1), jnp.exp(1.0 * lg)


SAMPLE_BB = 8


def _mix_sample_kernel(sink_ref, gam_ref,
                       qt_ref, kt_ref, v3_ref, sq3_ref, knew_ref, vnew_ref,
                       st_ref, ck_ref, cv_ref, g3_ref, x_ref, gate_ref,
                       wout_ref, brow_ref, gain_ref, lng_ref, lnb_ref,
                       x1_ref, nst_ref, nk_ref, nv_ref, ret_scr, swa_scr):
    i = pl.program_id(0)
    row = lax.broadcasted_iota(I32, (N_SWA_HEADS, 2 * HEAD_DIM), 0)
    lane = lax.broadcasted_iota(I32, (N_SWA_HEADS, 2 * HEAD_DIM), 1)
    own_half = (row // GQA_GROUP) == (lane // HEAD_DIM)
    sink_col = jnp.concatenate(
        [jnp.full((1, 1), sink_ref[hh], F32) for hh in range(N_SWA_HEADS)], axis=0)
    brow = brow_ref[...]
    qt = qt_ref[0]
    kt = kt_ref[0]
    gam = gam_ref[...]
    seqs = range(SAMPLE_BB)
    state_shape = (N_RET_HEADS, HEAD_DIM, HEAD_DIM)

    s_old = [st_ref[0, b] for b in seqs]
    v_full = [jnp.concatenate([jnp.broadcast_to(v3_ref[b, h:h + 1, :], (HEAD_DIM, HEAD_DIM))
                               for h in range(N_RET_HEADS)], axis=0) for b in seqs]
    s_new = [gam * s_old[b] + kt[:, b:b + 1] * v_full[b] for b in seqs]
    for b in seqs:
        nst_ref[b] = s_new[b]
    outs = [jnp.sum((qt[:, b:b + 1] * s_new[b]).reshape(state_shape), axis=1) for b in seqs]
    for b in seqs:
        ret_scr[i * SAMPLE_BB + b] = outs[b]

    kk = [jnp.concatenate([ck_ref[0, b, 1:, :], knew_ref[b:b + 1, :]], axis=0) for b in seqs]
    vv = [jnp.concatenate([cv_ref[0, b, 1:, :], vnew_ref[b:b + 1, :]], axis=0) for b in seqs]
    for b in seqs:
        nk_ref[b] = kk[b]
        nv_ref[b] = vv[b]
    qblk = [jnp.where(own_half, jnp.concatenate([sq3_ref[b], sq3_ref[b]], axis=-1), 0.0).astype(BF16)
            for b in seqs]
    s = [lax.dot_general(qblk[b], kk[b].astype(BF16), (((1,), (1,)), ((), ())),
                         preferred_element_type=F32) + brow for b in seqs]
    m = [jnp.maximum(jnp.max(s[b], axis=-1, keepdims=True), sink_col) for b in seqs]
    p = [jnp.exp(s[b] - m[b]) for b in seqs]
    den = [jnp.sum(p[b], axis=-1, keepdims=True) + jnp.exp(sink_col - m[b]) for b in seqs]
    pn = [(p[b] / den[b]).astype(BF16) for b in seqs]
    o = [jnp.dot(pn[b], vv[b].astype(BF16), preferred_element_type=F32) for b in seqs]
    for b in seqs:
        swa_scr[i * SAMPLE_BB + b] = jnp.where(own_half[:, :HEAD_DIM], o[b][:, :HEAD_DIM], o[b][:, HEAD_DIM:])

    @pl.when(i == pl.num_programs(0) - 1)
    def _():
        y = jnp.zeros(x_ref.shape, F32)
        for h in range(N_RET_HEADS):
            o = ret_scr[:, h, :]
            mu = jnp.mean(o, axis=-1, keepdims=True)
            c = o - mu
            var = jnp.mean(c * c, axis=-1, keepdims=True)
            r = c * lax.rsqrt(var + LN_EPS) * gain_ref[h:h + 1, :] * _silu(g3_ref[h])
            y = y + jnp.dot(r.astype(BF16), wout_ref[h * HEAD_DIM:(h + 1) * HEAD_DIM, :],
                            preferred_element_type=F32)
        for hh in range(N_SWA_HEADS):
            o = swa_scr[:, hh, :].astype(BF16)
            lo = RET_W + hh * HEAD_DIM
            y = y + jnp.dot(o, wout_ref[lo:lo + HEAD_DIM, :], preferred_element_type=F32)
        x1_ref[...] = _ln(DN_ALPHA * x_ref[...] + (1.0 + gate_ref[0]) * y, lng_ref[...], lnb_ref[...])


def _mix_sample(proj, x, mod, layer, w_out_bf, gamma1, bias_row, sinks, gain8, ln_g, ln_b,
                st, ck, cv):
    q, k, _, v, g, sq, skv = proj
    nb = x.shape[0]
    steps = nb // SAMPLE_BB

    def joined(a):
        a = a.astype(F32).reshape(nb, RET_W // 128, 2, 128)
        return (a[:, :, 0, :] + a[:, :, 1, :]).reshape(nb, RET_W)

    to_cols = lambda a: joined(a).reshape(steps, SAMPLE_BB, RET_W).transpose(0, 2, 1)
    qt, kt = to_cols(q), to_cols(k)
    v3 = joined(v).reshape(nb, N_RET_HEADS, HEAD_DIM)
    sq3 = joined(sq).reshape(nb, GQA_GROUP, N_KV_HEADS, HEAD_DIM).transpose(0, 2, 1, 3)
    sq3 = sq3.reshape(nb, N_SWA_HEADS, HEAD_DIM)
    g3 = g.reshape(nb, N_RET_HEADS, HEAD_DIM).transpose(1, 0, 2)
    knew, vnew = skv[:, :SWA_KVW], skv[:, SWA_KVW:]
    gam = jnp.broadcast_to(jnp.repeat(gamma1, HEAD_DIM)[:, None], (RET_W, HEAD_DIM))
    smem = pl.BlockSpec(memory_space=pltpu.SMEM)
    blk3 = lambda a, b, c: pl.BlockSpec((a, b, c), lambda i: (i, 0, 0))
    lay4 = lambda b, c: pl.BlockSpec((1, SAMPLE_BB, b, c), lambda i: (layer, i, 0, 0))
    const2 = lambda a: pl.BlockSpec(a.shape, lambda i: (0, 0))
    const3 = lambda a: pl.BlockSpec(a.shape, lambda i: (0, 0, 0))
    x1, nst, nk, nv = pl.pallas_call(
        _mix_sample_kernel,
        grid=(steps,),
        in_specs=[smem, const2(gam),
                  blk3(1, RET_W, SAMPLE_BB), blk3(1, RET_W, SAMPLE_BB),
                  blk3(SAMPLE_BB, N_RET_HEADS, HEAD_DIM), blk3(SAMPLE_BB, N_SWA_HEADS, HEAD_DIM),
                  pl.BlockSpec((SAMPLE_BB, SWA_KVW), lambda i: (i, 0)),
                  pl.BlockSpec((SAMPLE_BB, SWA_KVW), lambda i: (i, 0)),
                  lay4(RET_W, HEAD_DIM), lay4(CHUNK, SWA_KVW), lay4(CHUNK, SWA_KVW),
                  const3(g3), const2(x),
                  pl.BlockSpec((1, nb, D_MODEL), lambda i: (layer, 0, 2)),
                  const2(w_out_bf), const2(bias_row), const2(gain8), const2(ln_g), const2(ln_b)],
        out_specs=[const2(x), blk3(SAMPLE_BB, RET_W, HEAD_DIM), blk3(SAMPLE_BB, CHUNK, SWA_KVW),
                   blk3(SAMPLE_BB, CHUNK, SWA_KVW)],
        out_shape=[jax.ShapeDtypeStruct(x.shape, F32), jax.ShapeDtypeStruct(st.shape[1:], F32),
                   jax.ShapeDtypeStruct(ck.shape[1:], F32), jax.ShapeDtypeStruct(cv.shape[1:], F32)],
        scratch_shapes=[pltpu.VMEM((nb, N_RET_HEADS, HEAD_DIM), F32),
                        pltpu.VMEM((nb, N_SWA_HEADS, HEAD_DIM), F32)],
        compiler_params=_cparams(("arbitrary",)),
        name="mix_sample",
    )(sinks, gam, qt, kt, v3, sq3, knew, vnew, st, ck, cv, g3, x, mod,
      w_out_bf, bias_row, gain8, ln_g, ln_b)
    cache_shape = (nb, CHUNK, N_KV_HEADS, HEAD_DIM)
    return (x1, nst.reshape(nb, N_RET_HEADS, HEAD_DIM, HEAD_DIM), nk.reshape(cache_shape),
            nv.reshape(cache_shape))


def _router_kernel(x_ref, sh_ref, sc_ref, rwt_ref, rb_ref, tri_ref,
                   oi_ref, of_ref, cnt_ref, run_ref, *, mod_rows):
    h2 = x_ref[...] * (1.0 + sc_ref[0][:mod_rows]) + sh_ref[0][:mod_rows]
    _route_rows(h2.astype(BF16), rwt_ref, rb_ref, tri_ref, oi_ref, of_ref, cnt_ref, run_ref)


def _post_mix_kernel(mixed_ref, x_ref, gate_ref, wout_ref, lng_ref, lnb_ref,
                     sh_ref, sc_ref, rwt_ref, rb_ref, tri_ref,
                     x1_ref, oi_ref, of_ref, cnt_ref, run_ref):
    y = jnp.dot(mixed_ref[...], wout_ref[...], preferred_element_type=F32)
    x1 = _ln(DN_ALPHA * x_ref[...] + (1.0 + gate_ref[0][:1]) * y, lng_ref[...], lnb_ref[...])
    x1_ref[...] = x1
    h2 = x1 * (1.0 + sc_ref[0][:1]) + sh_ref[0][:1]
    _route_rows(h2.astype(BF16), rwt_ref, rb_ref, tri_ref, oi_ref, of_ref, cnt_ref, run_ref)


def _route_rows(h2b, rwt_ref, rb_ref, tri_ref, oi_ref, of_ref, cnt_ref, run_ref):
    i = pl.program_id(0)

    @pl.when(i == 0)
    def _():
        run_ref[...] = jnp.zeros_like(run_ref)

    logits = lax.dot_general(rwt_ref[...].astype(BF16), h2b, (((1,), (1,)), ((), ())),
                             preferred_element_type=F32)
    aff = 1.0 / (1.0 + jnp.exp(-logits))
    sel = aff + rb_ref[...]
    s = [sel[e:e + 1, :] for e in range(N_EXPERTS)]
    a = [aff[e:e + 1, :] for e in range(N_EXPERTS)]

    def top2sum(v0, v1, v2, v3):
        hi01, lo01 = jnp.maximum(v0, v1), jnp.minimum(v0, v1)
        hi23, lo23 = jnp.maximum(v2, v3), jnp.minimum(v2, v3)
        return jnp.maximum(hi01, hi23) + jnp.maximum(jnp.minimum(hi01, hi23),
                                                     jnp.maximum(lo01, lo23))

    def argmax_first(vals):
        best, idx = vals[0], jnp.zeros(vals[0].shape, I32)
        for j in range(1, len(vals)):
            upd = vals[j] > best
            idx = jnp.where(upd, j, idx)
            best = jnp.where(upd, vals[j], best)
        return idx

    def pick(idx, vals):
        out = vals[-1]
        for j in range(len(vals) - 2, -1, -1):
            out = jnp.where(idx == j, vals[j], out)
        return out

    gi = argmax_first([top2sum(*s[4 * g:4 * g + 4]) for g in range(N_GROUPS)])
    sv = [pick(gi, [s[4 * g + j] for g in range(N_GROUPS)]) for j in range(EXPERTS_PER_GROUP)]
    av = [pick(gi, [a[4 * g + j] for g in range(N_GROUPS)]) for j in range(EXPERTS_PER_GROUP)]
    i1 = argmax_first(sv)
    i2 = argmax_first([jnp.where(i1 == j, -jnp.inf, sv[j]) for j in range(EXPERTS_PER_GROUP)])
    w1, w2 = pick(i1, av), pick(i2, av)
    wsum = w1 + w2
    w1, w2 = w1 / wsum, w2 / wsum
    lo, hi = jnp.minimum(i1, i2), jnp.maximum(i1, i2)
    w_lo = jnp.where(i1 < i2, w1, w2)
    w_hi = jnp.where(i1 < i2, w2, w1)
    pair = jnp.where(lo == 0, hi - 1, jnp.where(lo == 1, hi + 1, 5))
    bin_id = gi * N_PAIRS + pair

    tm = bin_id.shape[1]
    onehot = lax.broadcasted_iota(I32, (BIN_ROWS, tm), 0) == bin_id
    oh_f = jnp.where(onehot, 1.0, 0.0)
    before = jnp.dot(oh_f.astype(BF16), tri_ref[...], preferred_element_type=F32)
    run = run_ref[...]
    run_t = jnp.concatenate([run] * (tm // 128), axis=-1)
    rank = jnp.sum(oh_f * (before + run_t), axis=0, keepdims=True)
    run_new = run + jnp.sum(oh_f, axis=1, keepdims=True)
    run_ref[...] = run_new
    cnt_ref[...] = run_new.astype(I32)

    zi = jnp.zeros_like(bin_id)
    oi_ref[0] = jnp.concatenate([gi * 4 + lo, gi * 4 + hi, bin_id, rank.astype(I32), zi, zi, zi, zi], axis=0)
    zf = jnp.zeros_like(w_lo)
    of_ref[0] = jnp.concatenate([w_lo, w_hi, zf, zf, zf, zf, zf, zf], axis=0)


def _router(x1, mod, layer, mod_row0, mod_rows, router_wt, router_b, tm):
    rows = x1.shape[0]
    nt = rows // tm
    mblk = 8 if mod_rows == 1 else mod_rows
    mrow = mod_row0 // mblk
    tri = jnp.asarray(np.triu(np.ones((tm, tm), np.float32), 1), BF16)
    rb = jnp.broadcast_to(router_b.astype(F32)[:, None], (N_EXPERTS, tm))
    mod_spec = lambda c: pl.BlockSpec((1, mblk, D_MODEL), lambda i: (layer, mrow, c))
    oi, of, cnt = pl.pallas_call(
        functools.partial(_router_kernel, mod_rows=mod_rows),
        grid=(nt,),
        in_specs=[pl.BlockSpec((tm, D_MODEL), lambda i: (i, 0)), mod_spec(3), mod_spec(4),
                  pl.BlockSpec((N_EXPERTS, D_MODEL), lambda i: (0, 0)),
                  pl.BlockSpec((N_EXPERTS, tm), lambda i: (0, 0)),
                  pl.BlockSpec((tm, tm), lambda i: (0, 0))],
        out_specs=[pl.BlockSpec((1, 8, tm), lambda i: (i, 0, 0)),
                   pl.BlockSpec((1, 8, tm), lambda i: (i, 0, 0)),
                   pl.BlockSpec((BIN_ROWS, 128), lambda i: (0, 0))],
        out_shape=[jax.ShapeDtypeStruct((nt, 8, tm), I32),
                   jax.ShapeDtypeStruct((nt, 8, tm), F32),
                   jax.ShapeDtypeStruct((BIN_ROWS, 128), I32)],
        scratch_shapes=[pltpu.VMEM((BIN_ROWS, 128), F32)],
        compiler_params=_cparams(("arbitrary",)),
        name="router",
    )(x1, mod, mod, router_wt, rb, tri)
    return _unpack_route(oi, of, cnt)


def _unpack_route(oi, of, cnt):
    flat = lambda a, r: a[:, r, :].reshape(-1)
    return (flat(oi, 0), flat(oi, 1), flat(oi, 2), flat(oi, 3), flat(of, 0), flat(of, 1),
            cnt[:N_BINS, 0])


POST_TM = 512


def _post_mix(mixed, x, mod, layer, mod_row0, w_out_bf, ln_g, ln_b, router_wt, router_b):
    rows = x.shape[0]
    tm = POST_TM
    nt = rows // tm
    mrow = mod_row0 // 8
    tri = jnp.asarray(np.triu(np.ones((tm, tm), np.float32), 1), BF16)
    rb = jnp.broadcast_to(router_b.astype(F32)[:, None], (N_EXPERTS, tm))
    mod_spec = lambda c: pl.BlockSpec((1, 8, D_MODEL), lambda i: (layer, mrow, c))
    row_spec = pl.BlockSpec((tm, D_MODEL), lambda i: (i, 0))
    route_spec = pl.BlockSpec((1, 8, tm), lambda i: (i, 0, 0))
    const2 = lambda a: pl.BlockSpec(a.shape, lambda i: (0, 0))
    x1, oi, of, cnt = pl.pallas_call(
        _post_mix_kernel,
        grid=(nt,),
        in_specs=[row_spec, row_spec, mod_spec(2), const2(w_out_bf), const2(ln_g), const2(ln_b),
                  mod_spec(3), mod_spec(4), const2(router_wt), const2(rb), const2(tri)],
        out_specs=[row_spec, route_spec, route_spec,
                   pl.BlockSpec((BIN_ROWS, 128), lambda i: (0, 0))],
        out_shape=[jax.ShapeDtypeStruct((rows, D_MODEL), F32),
                   jax.ShapeDtypeStruct((nt, 8, tm), I32),
                   jax.ShapeDtypeStruct((nt, 8, tm), F32),
                   jax.ShapeDtypeStruct((BIN_ROWS, 128), I32)],
        scratch_shapes=[pltpu.VMEM((BIN_ROWS, 128), F32)],
        compiler_params=_cparams(("arbitrary",)),
        name="post_mix",
    )(mixed, x, mod, w_out_bf, ln_g, ln_b, mod, mod, router_wt, rb, tri)
    return x1, _unpack_route(oi, of, cnt)


DMA_UNROLL = 8
ROW_GROUPS = MOE_TM // DMA_UNROLL


def _for_rows(n, fn):
    def body_u(j, carry):
        for u in range(DMA_UNROLL):
            fn(j, u)
        return carry
    lax.fori_loop(0, n // DMA_UNROLL, body_u, 0)

    def body_1(r, carry):
        fn(r // DMA_UNROLL, r % DMA_UNROLL)
        return carry
    lax.fori_loop((n // DMA_UNROLL) * DMA_UNROLL, n, body_1, 0)


def _wait_rows(n, buf, sem):
    p = ROW_GROUPS
    while p >= 1:
        @pl.when((n & (p * DMA_UNROLL)) != 0)
        def _(p=p):
            pltpu.make_async_copy(buf.at[pl.ds(0, p)], buf.at[pl.ds(0, p)], sem).wait()
        p //= 2
    p = DMA_UNROLL // 2
    while p >= 1:
        @pl.when((n & p) != 0)
        def _(p=p):
            pltpu.make_async_copy(buf.at[0, pl.ds(0, p)], buf.at[0, pl.ds(0, p)], sem).wait()
        p //= 2


def _moe_sorted_kernel(ea_ref, eb_ref, nvalid_ref, pos_ref,
                       x_hbm, sh_ref, sc_ref, gate_ref, rw_ref,
                       wga_ref, wua_ref, wda_ref, wgb_ref, wub_ref, wdb_ref,
                       lng_ref, lnb_ref,
                       out_hbm,
                       inv, xbuf, obuf, wg_a, wu_a, wd_a, wg_b, wu_b, wd_b, gsem, ssem):
    i = pl.program_id(0)
    nt = pl.num_programs(0)
    slot = i % 2
    n_tok = pos_ref.shape[0]

    def start_gather(t, dst_slot):
        def one(j, u):
            tok = inv[t * MOE_TM + j * DMA_UNROLL + u]
            pltpu.make_async_copy(x_hbm.at[pl.ds(tok, 1)], xbuf.at[dst_slot, j, pl.ds(u, 1)],
                                  gsem.at[dst_slot]).start()
        _for_rows(nvalid_ref[t], one)

    @pl.when(i == 0)
    def _():
        def body(j, carry):
            for u in range(DMA_UNROLL):
                t = j * DMA_UNROLL + u
                inv[pos_ref[t]] = t
            return carry
        lax.fori_loop(0, n_tok // DMA_UNROLL, body, 0)
        xbuf[...] = jnp.zeros_like(xbuf)
        start_gather(0, 0)

    @pl.when(i >= 2)
    def _():
        _wait_rows(nvalid_ref[jnp.maximum(i - 2, 0)], obuf.at[slot], ssem.at[slot])

    @pl.when(nvalid_ref[i] > 0)
    def _():
        _wait_rows(nvalid_ref[i], xbuf.at[slot], gsem.at[slot])

        @pl.when(i + 1 < nt)
        def _():
            start_gather(jnp.minimum(i + 1, nt - 1), 1 - slot)

        @pl.when(jnp.logical_or(i == 0, ea_ref[i] != ea_ref[jnp.maximum(i - 1, 0)]))
        def _():
            wg_a[...] = wga_ref[0, 0].astype(BF16)
            wu_a[...] = wua_ref[0, 0].astype(BF16)
            wd_a[...] = wda_ref[0, 0].astype(BF16)

        @pl.when(jnp.logical_or(i == 0, eb_ref[i] != eb_ref[jnp.maximum(i - 1, 0)]))
        def _():
            wg_b[...] = wgb_ref[0, 0].astype(BF16)
            wu_b[...] = wub_ref[0, 0].astype(BF16)
            wd_b[...] = wdb_ref[0, 0].astype(BF16)

        x = xbuf[slot].reshape(MOE_TM, D_MODEL)
        h2 = (x * (1.0 + sc_ref[0][:1]) + sh_ref[0][:1]).astype(BF16)

        dot = functools.partial(jnp.dot, preferred_element_type=F32)
        logits = dot(h2, rw_ref[...])
        gate_a, gate_b = dot(h2, wg_a[...]), dot(h2, wg_b[...])
        up_a, up_b = dot(h2, wu_a[...]), dot(h2, wu_b[...])
        act_a = (_silu(gate_a) * up_a).astype(BF16)
        act_b = (_silu(gate_b) * up_b).astype(BF16)
        y_a, y_b = dot(act_a, wd_a[...]), dot(act_b, wd_b[...])

        aff = 1.0 / (1.0 + jnp.exp(-logits))
        lane = lax.broadcasted_iota(I32, aff.shape, 1)
        a_lo = jnp.sum(jnp.where(lane == ea_ref[i], aff, 0.0), axis=-1, keepdims=True)
        a_hi = jnp.sum(jnp.where(lane == eb_ref[i], aff, 0.0), axis=-1, keepdims=True)
        a_sum = a_lo + a_hi
        y = (a_lo / a_sum) * y_a + (a_hi / a_sum) * y_b
        out = _ln(DN_ALPHA * x + (1.0 + gate_ref[0][:1]) * y, lng_ref[...], lnb_ref[...])
        obuf[slot] = out.reshape(ROW_GROUPS, DMA_UNROLL, D_MODEL)

        def scatter_one(j, u):
            tok = inv[i * MOE_TM + j * DMA_UNROLL + u]
            pltpu.make_async_copy(obuf.at[slot, j, pl.ds(u, 1)], out_hbm.at[pl.ds(tok, 1)],
                                  ssem.at[slot]).start()
        _for_rows(nvalid_ref[i], scatter_one)

    @pl.when(i == nt - 1)
    def _():
        _wait_rows(nvalid_ref[jnp.maximum(i - 1, 0)], obuf.at[1 - slot], ssem.at[1 - slot])
        _wait_rows(nvalid_ref[i], obuf.at[slot], ssem.at[slot])


def _moe_sorted(x1, mod, layer, mod_row0, route, rw_pad, wg, wu, wd, ln_g, ln_b):
    _, _, bin_id, rank, _, _, counts = route
    rows = x1.shape[0]
    nt = rows // MOE_TM + N_BINS
    ns = nt * MOE_TM
    tiles = (counts + MOE_TM - 1) // MOE_TM
    tile_end = jnp.cumsum(tiles)
    row_start = ((tile_end - tiles) * MOE_TM).astype(I32)
    pos = jnp.sum(jnp.where(bin_id[:, None] == jnp.arange(N_BINS, dtype=I32)[None, :],
                            row_start[None, :], 0), axis=1) + rank
    total = tile_end[-1]
    t = jnp.arange(nt, dtype=I32)
    tile_bin = jnp.sum(jnp.minimum(t, total - 1)[:, None] >= tile_end[None, :], axis=1).astype(I32)
    tile_bin = jnp.clip(tile_bin, 0, N_BINS - 1)
    bin_lo = jnp.asarray([g * 4 + _PAIR_LO[p] for g in range(N_GROUPS) for p in range(N_PAIRS)], I32)
    bin_hi = jnp.asarray([g * 4 + _PAIR_HI[p] for g in range(N_GROUPS) for p in range(N_PAIRS)], I32)
    tile_ea, tile_eb = bin_lo[tile_bin], bin_hi[tile_bin]
    nvalid = jnp.clip(counts[tile_bin] - (t - (tile_end - tiles)[tile_bin]) * MOE_TM, 0, MOE_TM)
    nvalid = jnp.where(t < total, nvalid, 0).astype(I32)
    mrow = mod_row0 // 8
    mod_spec = lambda c: pl.BlockSpec((1, 8, D_MODEL), lambda i, *_: (layer, mrow, c))
    w_spec = lambda shape, ref_idx: pl.BlockSpec(
        (1, 1) + shape, lambda i, ea_r, eb_r, *_: (layer, (ea_r, eb_r)[ref_idx][i], 0, 0))
    any_spec = pl.BlockSpec(memory_space=pl.ANY)
    w_bf = lambda shape: pltpu.VMEM(shape, BF16)
    grid_spec = pltpu.PrefetchScalarGridSpec(
        num_scalar_prefetch=4,
        grid=(nt,),
        in_specs=[any_spec,
                  mod_spec(3), mod_spec(4), mod_spec(5),
                  pl.BlockSpec((D_MODEL, 128), lambda i, *_: (0, 0)),
                  w_spec((D_MODEL, D_FF), 0), w_spec((D_MODEL, D_FF), 0), w_spec((D_FF, D_MODEL), 0),
                  w_spec((D_MODEL, D_FF), 1), w_spec((D_MODEL, D_FF), 1), w_spec((D_FF, D_MODEL), 1),
                  pl.BlockSpec((1, D_MODEL), lambda i, *_: (0, 0)),
                  pl.BlockSpec((1, D_MODEL), lambda i, *_: (0, 0))],
        out_specs=any_spec,
        scratch_shapes=[pltpu.SMEM((ns,), I32),
                        pltpu.VMEM((2, ROW_GROUPS, DMA_UNROLL, D_MODEL), F32),
                        pltpu.VMEM((2, ROW_GROUPS, DMA_UNROLL, D_MODEL), F32),
                        w_bf((D_MODEL, D_FF)), w_bf((D_MODEL, D_FF)), w_bf((D_FF, D_MODEL)),
                        w_bf((D_MODEL, D_FF)), w_bf((D_MODEL, D_FF)), w_bf((D_FF, D_MODEL)),
                        pltpu.SemaphoreType.DMA((2,)),
                        pltpu.SemaphoreType.DMA((2,))],
    )
    return pl.pallas_call(
        _moe_sorted_kernel,
        grid_spec=grid_spec,
        out_shape=jax.ShapeDtypeStruct((rows, D_MODEL), F32),
        compiler_params=_cparams(("arbitrary",)),
        name="moe_sorted",
    )(tile_ea, tile_eb, nvalid, pos, x1, mod, mod, mod, rw_pad,
      wg, wu, wd, wg, wu, wd, ln_g, ln_b)


def _moe_dense_kernel(x_ref, dw_ref, sh_ref, sc_ref, gate_ref, wg_ref, wu_ref, wd_ref,
                      lng_ref, lnb_ref, o_ref, acc):
    e = pl.program_id(0)

    @pl.when(e == 0)
    def _():
        acc[...] = jnp.zeros_like(acc)

    x = x_ref[...]
    h2 = (x * (1.0 + sc_ref[0]) + sh_ref[0]).astype(BF16)
    a = _silu(jnp.dot(h2, wg_ref[0, 0].astype(BF16), preferred_element_type=F32)) * \
        jnp.dot(h2, wu_ref[0, 0].astype(BF16), preferred_element_type=F32)
    y = jnp.dot(a.astype(BF16), wd_ref[0, 0].astype(BF16), preferred_element_type=F32)
    acc[...] = acc[...] + dw_ref[0][:, :1] * y

    @pl.when(e == pl.num_programs(0) - 1)
    def _():
        o_ref[...] = _ln(DN_ALPHA * x + (1.0 + gate_ref[0]) * acc[...], lng_ref[...], lnb_ref[...])


def _moe_dense(x1, mod, layer, route, wg, wu, wd, ln_g, ln_b):
    ea, eb, _, _, w_lo, w_hi, _ = route
    nb = x1.shape[0]
    eids = jnp.arange(N_EXPERTS, dtype=I32)[:, None]
    dw = jnp.where(eids == ea[None, :], w_lo[None, :], 0.0) + \
        jnp.where(eids == eb[None, :], w_hi[None, :], 0.0)
    dw = jnp.broadcast_to(dw[:, :, None], (N_EXPERTS, nb, 128))
    mod_spec = lambda c: pl.BlockSpec((1, nb, D_MODEL), lambda e: (layer, 0, c))
    const2 = lambda a: pl.BlockSpec(a.shape, lambda e: (0, 0))
    return pl.pallas_call(
        _moe_dense_kernel,
        grid=(N_EXPERTS,),
        in_specs=[const2(x1), pl.BlockSpec((1, nb, 128), lambda e: (e, 0, 0)),
                  mod_spec(3), mod_spec(4), mod_spec(5),
                  pl.BlockSpec((1, 1, D_MODEL, D_FF), lambda e: (layer, e, 0, 0)),
                  pl.BlockSpec((1, 1, D_MODEL, D_FF), lambda e: (layer, e, 0, 0)),
                  pl.BlockSpec((1, 1, D_FF, D_MODEL), lambda e: (layer, e, 0, 0)),
                  const2(ln_g), const2(ln_b)],
        out_specs=const2(x1),
        out_shape=jax.ShapeDtypeStruct(x1.shape, F32),
        scratch_shapes=[pltpu.VMEM(x1.shape, F32)],
        compiler_params=_cparams(("arbitrary",)),
        name="moe_dense",
    )(x1, dw, mod, mod, mod, wg, wu, wd, ln_g, ln_b)


def kernel(x_prompt, x_sample, state_ret, cache_swa_k, cache_swa_v, c_prompt, c_sample, w_in, w_out, ret_gn_gain, swa_sinks, rel_bias_table, ada_w, ada_b, ln1_g, ln1_b, ln2_g, ln2_b, router_w, router_b, exp_w_gate, exp_w_up, exp_w_down):
    seq = x_prompt.shape[1]
    nb = x_sample.shape[0]
    past_len = 16384
    assert x_prompt.shape[0] == 1 and x_sample.shape[1] == 1

    c_all = jnp.concatenate([c_sample, c_prompt, jnp.zeros((7, D_MODEL), F32)], axis=0)
    mod = _ada(c_all, ada_w, ada_b)
    prompt_row = nb

    bias_tabs = _bias_tables(rel_bias_table.astype(F32))
    gl, decay, xi, zeta, gamma1 = _ret_consts()
    rope_p = _rope_tables(jnp.arange(seq, dtype=I32))
    rope_s = _rope_tables(jnp.full((nb,), past_len, I32))
    router_wt = router_w.astype(F32).T
    rw_pad = jnp.pad(router_w.astype(BF16), ((0, 0), (0, 128 - N_EXPERTS)))
    zeta_p = jnp.tile(zeta, (INPROJ_TM // CHUNK, 1))
    zeta_s = jnp.ones((nb, RET_W), F32)

    st_all = state_ret.astype(F32).reshape(DEPTH, nb, RET_W, HEAD_DIM)
    ck_all = cache_swa_k.astype(F32).reshape(DEPTH, nb, CHUNK, SWA_KVW)
    cv_all = cache_swa_v.astype(F32).reshape(DEPTH, nb, CHUNK, SWA_KVW)

    xp = x_prompt.reshape(seq, D_MODEL)
    xs = x_sample.reshape(nb, D_MODEL)
    st_p, k_p, v_p, st_s, k_s, v_s = [], [], [], [], [], []
    for l in range(DEPTH):
        w_in_bf = w_in[l].astype(BF16)
        w_out_bf = w_out[l].astype(BF16)
        order = jnp.asarray(_SWA_HEAD_ORDER)
        swa_rows = w_out[l][RET_W:].reshape(N_SWA_HEADS, HEAD_DIM, D_MODEL)[order]
        w_out_prompt = jnp.concatenate([w_out[l][:RET_W], swa_rows.reshape(SWA_QW, D_MODEL)],
                                       axis=0).astype(BF16)
        experts = (exp_w_gate.astype(F32), exp_w_up.astype(F32), exp_w_down.astype(F32))
        gain = ret_gn_gain[l].astype(F32)
        sinks = swa_sinks[l].astype(F32)
        row = lambda a: a[l].astype(F32).reshape(1, D_MODEL)

        proj = _inproj(xp, mod, l, prompt_row, 1, w_in_bf, rope_p, zeta_p, INPROJ_TM)
        mixed, st = _mix_prompt(proj, (gl, decay, xi), bias_tabs, sinks, gain.reshape(1, RET_W))
        skv = proj[6]
        st_p.append(st.reshape(1, N_RET_HEADS, HEAD_DIM, HEAD_DIM))
        k_p.append(skv[seq - CHUNK:, :SWA_KVW].reshape(1, CHUNK, N_KV_HEADS, HEAD_DIM))
        v_p.append(skv[seq - CHUNK:, SWA_KVW:].reshape(1, CHUNK, N_KV_HEADS, HEAD_DIM))
        x1, route = _post_mix(mixed, xp, mod, l, prompt_row, w_out_prompt, row(ln1_g), row(ln1_b),
                              router_wt, router_b)
        xp = _moe_sorted(x1, mod, l, prompt_row, route, rw_pad, *experts, row(ln2_g), row(ln2_b))

        proj = _inproj(xs, mod, l, 0, nb, w_in_bf, rope_s, zeta_s, nb)
        x1, nst, nk, nv = _mix_sample(proj, xs, mod, l, w_out_bf, gamma1, bias_tabs[1], sinks, gain,
                                      row(ln1_g), row(ln1_b), st_all, ck_all, cv_all)
        st_s.append(nst)
        k_s.append(nk)
        v_s.append(nv)
        route = _router(x1, mod, l, 0, nb, router_wt, router_b, nb)
        xs = _moe_dense(x1, mod, l, route, *experts, row(ln2_g), row(ln2_b))

    return (xp.reshape(1, seq, D_MODEL), xs.reshape(nb, 1, D_MODEL),
            jnp.stack(st_p), jnp.stack(k_p), jnp.stack(v_p),
            jnp.stack(st_s), jnp.stack(k_s), jnp.stack(v_s))
```

```python
import functools
import math

import numpy as np
import jax
import jax.numpy as jnp
from jax import lax
from jax.experimental import pallas as pl
from jax.experimental.pallas import tpu as pltpu

F32 = jnp.float32
BF16 = jnp.bfloat16
I32 = jnp.int32

D_MODEL = 1024
DEPTH = 2
HEAD_DIM = 64
N_RET_HEADS = 8
N_SWA_HEADS = 8
N_KV_HEADS = 2
GQA_GROUP = N_SWA_HEADS // N_KV_HEADS
RET_W = N_RET_HEADS * HEAD_DIM
SWA_QW = N_SWA_HEADS * HEAD_DIM
SWA_KVW = N_KV_HEADS * HEAD_DIM
PROJ_W = 4 * RET_W + SWA_QW + 2 * SWA_KVW
CHUNK = 128
ROPE_BASE = 10000.0
N_BUCKETS = 32
MAX_DISTANCE = 128
N_EXPERTS = 16
N_GROUPS = 4
EXPERTS_PER_GROUP = 4
D_FF = 512
LN_EPS = 1e-5
DN_ALPHA = (2 * DEPTH) ** 0.25
QK_SCALE = HEAD_DIM ** -0.5
NEG_BIG = -1e30

N_PAIRS = 6
N_BINS = N_GROUPS * N_PAIRS
BIN_ROWS = 32
MOE_TM = 256
INPROJ_TM = 512
MIX_SUB = 2
SPLIT_W = 128 * N_RET_HEADS
_SWA_HEAD_ORDER = (0, 4, 1, 5, 2, 6, 3, 7)
VMEM_LIMIT = 56 * 1024 * 1024

_PAIR_LO = (0, 0, 0, 1, 1, 2)
_PAIR_HI = (1, 2, 3, 2, 3, 3)


def _cparams(sem):
    return pltpu.CompilerParams(dimension_semantics=sem, vmem_limit_bytes=VMEM_LIMIT)


def _ln(v, g, b):
    mu = jnp.mean(v, axis=-1, keepdims=True)
    c = v - mu
    var = jnp.mean(c * c, axis=-1, keepdims=True)
    return c * lax.rsqrt(var + LN_EPS) * g + b


def _silu(v):
    return v * (1.0 / (1.0 + jnp.exp(-v)))


def _ada_kernel(c_ref, w_ref, b_ref, o_ref):
    o_ref[0] = jnp.dot(c_ref[...].astype(BF16), w_ref[0].astype(BF16),
                       preferred_element_type=F32) + b_ref[0]


def _ada(c_all, ada_w, ada_b):
    rows = c_all.shape[0]
    nt = 6 * D_MODEL // 1024
    return pl.pallas_call(
        _ada_kernel,
        grid=(DEPTH, nt),
        in_specs=[pl.BlockSpec((rows, D_MODEL), lambda l, j: (0, 0)),
                  pl.BlockSpec((1, D_MODEL, 1024), lambda l, j: (l, 0, j)),
                  pl.BlockSpec((1, 1, 1024), lambda l, j: (l, 0, j))],
        out_specs=pl.BlockSpec((1, rows, 1024), lambda l, j: (l, 0, j)),
        out_shape=jax.ShapeDtypeStruct((DEPTH, rows, 6 * D_MODEL), F32),
        compiler_params=_cparams(("arbitrary", "arbitrary")),
        name="ada",
    )(c_all, ada_w, ada_b.reshape(DEPTH, 1, 6 * D_MODEL))


def _bias_kernel(tab_ref, bkt_ref, fold_ref, row_ref):
    bkt = bkt_ref[...]
    rows = lax.broadcasted_iota(I32, (CHUNK, CHUNK), 0)
    cols = lax.broadcasted_iota(I32, (CHUNK, CHUNK), 1)
    from_prev = cols > rows
    for h in range(N_SWA_HEADS):
        acc = jnp.zeros(bkt.shape, F32)
        for b in range(N_BUCKETS):
            acc = jnp.where(bkt == b, tab_ref[b, h], acc)
        own = acc[:, CHUNK:]
        fold_ref[0, h] = jnp.where(from_prev, NEG_BIG, own)
        fold_ref[1, h] = jnp.where(from_prev, acc[:, :CHUNK], own)
        row_ref[h:h + 1, :] = own[CHUNK - 1:CHUNK, :]


def _t5_bucket(rel):
    max_exact = N_BUCKETS // 2
    relf = jnp.maximum(rel, 1).astype(F32)
    large = max_exact + (jnp.log(relf / max_exact) / math.log(MAX_DISTANCE / max_exact)
                         * (N_BUCKETS - max_exact)).astype(I32)
    large = jnp.minimum(large, N_BUCKETS - 1)
    return jnp.where(rel < max_exact, rel, large)


def _bias_tables(rel_bias_table):
    qi = jnp.arange(CHUNK)
    si = jnp.arange(2 * CHUNK)
    rel = CHUNK + qi[:, None] - si[None, :]
    bkt = _t5_bucket(jnp.maximum(rel, 0)).astype(I32)
    return pl.pallas_call(
        _bias_kernel,
        in_specs=[pl.BlockSpec(memory_space=pltpu.SMEM),
                  pl.BlockSpec((CHUNK, 2 * CHUNK), lambda: (0, 0))],
        out_specs=[pl.BlockSpec((2, N_SWA_HEADS, CHUNK, CHUNK), lambda: (0, 0, 0, 0)),
                   pl.BlockSpec((N_SWA_HEADS, CHUNK), lambda: (0, 0))],
        out_shape=[jax.ShapeDtypeStruct((2, N_SWA_HEADS, CHUNK, CHUNK), F32),
                   jax.ShapeDtypeStruct((N_SWA_HEADS, CHUNK), F32)],
        name="t5_bias",
    )(rel_bias_table, bkt)


def _rotary(v, cos, s_lo, s_hi):
    outs = []
    for j in range(RET_W // 128):
        blk = v[:, j * 128:(j + 1) * 128]
        outs.append(blk * cos + pltpu.roll(blk, 96, 1) * s_lo + pltpu.roll(blk, 32, 1) * s_hi)
    return jnp.concatenate(outs, axis=-1)


def _inproj_kernel(x_ref, sh_ref, sc_ref, w_ref, cs_ref, zeta_ref,
                   q_ref, k_ref, kz_ref, v_ref, g_ref, sq_ref, skv_ref, *, mod_rows):
    sh = sh_ref[0][:mod_rows]
    sc = sc_ref[0][:mod_rows]
    h = (x_ref[...] * (1.0 + sc) + sh).astype(BF16)
    cs = jnp.concatenate([cs_ref[...], cs_ref[...]], axis=-1)
    first = (lax.broadcasted_iota(I32, cs.shape, 1) % HEAD_DIM) < (HEAD_DIM // 2)
    cos = jnp.where(first, cs, pltpu.roll(cs, 32, 1))
    s_lo = jnp.where(first, -pltpu.roll(cs, 96, 1), 0.0)
    s_hi = jnp.where(first, 0.0, cs)

    def proj(lo, hi):
        return jnp.dot(h, w_ref[:, lo:hi], preferred_element_type=F32)

    left = lax.broadcasted_iota(I32, cs.shape, 1) < HEAD_DIM

    def store_split(ref, val):
        for b in range(RET_W // 128):
            blk = val[:, b * 128:(b + 1) * 128]
            ref[:, (2 * b) * 128:(2 * b + 1) * 128] = jnp.where(left, blk, 0.0).astype(BF16)
            ref[:, (2 * b + 1) * 128:(2 * b + 2) * 128] = jnp.where(left, 0.0, blk).astype(BF16)

    store_split(q_ref, _rotary(proj(0, RET_W), cos, s_lo, s_hi))
    k = _rotary(proj(RET_W, 2 * RET_W), cos, s_lo, s_hi) * QK_SCALE
    store_split(k_ref, k)
    store_split(kz_ref, k * zeta_ref[...])
    store_split(v_ref, proj(2 * RET_W, 3 * RET_W))
    g_ref[...] = proj(3 * RET_W, 4 * RET_W)
    sq = proj(4 * RET_W, 4 * RET_W + SWA_QW) * QK_SCALE
    for hh in range(N_SWA_HEADS):
        kv, j = divmod(hh, GQA_GROUP)
        blk = sq[:, (hh // 2) * 128:(hh // 2 + 1) * 128]
        if hh % 2 != kv:
            blk = pltpu.roll(blk, HEAD_DIM, 1)
        keep = left if kv == 0 else jnp.logical_not(left)
        sq_ref[:, (2 * j + kv) * 128:(2 * j + kv + 1) * 128] = jnp.where(keep, blk, 0.0).astype(BF16)
    skv_ref[...] = proj(4 * RET_W + SWA_QW, PROJ_W)


def _inproj(x, mod, layer, mod_row0, mod_rows, w_in_bf, rope, zeta_tile, tm):
    rows = x.shape[0]
    mblk = 8 if mod_rows == 1 else mod_rows
    mrow = mod_row0 // mblk
    row_spec = lambda w: pl.BlockSpec((tm, w), lambda i: (i, 0))
    mod_spec = lambda c: pl.BlockSpec((1, mblk, D_MODEL), lambda i: (layer, mrow, c))
    return pl.pallas_call(
        functools.partial(_inproj_kernel, mod_rows=mod_rows),
        grid=(rows // tm,),
        in_specs=[row_spec(D_MODEL), mod_spec(0), mod_spec(1),
                  pl.BlockSpec((D_MODEL, PROJ_W), lambda i: (0, 0)),
                  row_spec(HEAD_DIM),
                  pl.BlockSpec((tm, RET_W), lambda i: (0, 0))],
        out_specs=[row_spec(SPLIT_W), row_spec(SPLIT_W), row_spec(SPLIT_W), row_spec(SPLIT_W),
                   row_spec(RET_W), row_spec(SPLIT_W), row_spec(2 * SWA_KVW)],
        out_shape=[jax.ShapeDtypeStruct((rows, SPLIT_W), BF16),
                   jax.ShapeDtypeStruct((rows, SPLIT_W), BF16),
                   jax.ShapeDtypeStruct((rows, SPLIT_W), BF16),
                   jax.ShapeDtypeStruct((rows, SPLIT_W), BF16),
                   jax.ShapeDtypeStruct((rows, RET_W), F32),
                   jax.ShapeDtypeStruct((rows, SPLIT_W), BF16),
                   jax.ShapeDtypeStruct((rows, 2 * SWA_KVW), F32)],
        compiler_params=_cparams(("arbitrary",)),
        name="inproj",
    )(x, mod, mod, w_in_bf, rope, zeta_tile)


def _rope_tables(pos):
    half = HEAD_DIM // 2
    inv = ROPE_BASE ** (-jnp.arange(half, dtype=F32) / half)
    ang = pos.astype(F32)[:, None] * inv[None, :]
    cos, sin = jnp.cos(ang), jnp.sin(ang)
    return jnp.concatenate([cos, sin], axis=-1)


def _mix_prompt_kernel(gl_ref, sink_ref,
                       q_ref, k_ref, kz_ref, v_ref, g_ref, sq_ref, kvc_ref, kvp_ref,
                       decay_ref, xi_ref, bias_ref, gain_ref,
                       mixed_ref, st_ref, state):
    i = pl.program_id(0)

    @pl.when(i == 0)
    def _():
        state[...] = jnp.zeros_like(state)

    gain = gain_ref[...]
    left = lax.broadcasted_iota(I32, (CHUNK, 128), 1) < HEAD_DIM
    from_prev = (lax.broadcasted_iota(I32, (CHUNK, CHUNK), 1) >
                 lax.broadcasted_iota(I32, (CHUNK, CHUNK), 0))
    inv_dim = 1.0 / HEAD_DIM

    for sub in range(MIX_SUB):
        rows = slice(sub * CHUNK, (sub + 1) * CHUNK)

        heads = range(N_RET_HEADS)
        hsl = [slice(h * 128, (h + 1) * 128) for h in heads]
        scores_r = [lax.dot_general(q_ref[rows, hsl[h]], k_ref[rows, hsl[h]], (((1,), (1,)), ((), ())),
                                    preferred_element_type=F32) for h in heads]
        s_olds = [state[h] for h in heads]
        inter = [jnp.dot(q_ref[rows, hsl[h]], s_olds[h].astype(BF16), preferred_element_type=F32)
                 for h in heads]
        upds = [lax.dot_general(kz_ref[rows, hsl[h]], v_ref[rows, hsl[h]], (((0,), (0,)), ((), ())),
                                preferred_element_type=F32) for h in heads]
        for h in heads:
            state[h] = gl_ref[h] * s_olds[h] + upds[h]
        probs_r = [(scores_r[h] * decay_ref[h]).astype(BF16) for h in heads]
        intra = [jnp.dot(probs_r[h], v_ref[rows, hsl[h]], preferred_element_type=F32) for h in heads]

        owns = [left if h % 2 == 0 else jnp.logical_not(left) for h in heads]
        outs = [intra[h] + inter[h] * xi_ref[:, hsl[h]] for h in heads]
        mus = [jnp.sum(outs[h], axis=-1, keepdims=True) * inv_dim for h in heads]
        cen = [jnp.where(owns[h], outs[h] - mus[h], 0.0) for h in heads]
        var = [jnp.sum(cen[h] * cen[h], axis=-1, keepdims=True) * inv_dim for h in heads]
        nrm = [cen[h] * lax.rsqrt(var[h] + LN_EPS) for h in heads]
        ret_blocks = [nrm[2 * b] + nrm[2 * b + 1] for b in range(N_RET_HEADS // 2)]
        ret_o = jnp.concatenate(ret_blocks, axis=-1) * gain * _silu(g_ref[rows, :])

        kvc = kvc_ref[rows, :]
        kvp = kvp_ref[...] if sub == 0 else kvc_ref[(sub - 1) * CHUNK:sub * CHUNK, :]
        kcat = jnp.concatenate([kvp[:, :SWA_KVW], kvc[:, :SWA_KVW]], axis=0)
        vcat = jnp.concatenate([kvp[:, SWA_KVW:], kvc[:, SWA_KVW:]], axis=0)
        left2 = jnp.concatenate([left, left], axis=0)
        table = jnp.minimum(i, 1) if sub == 0 else 1
        swa_blocks = [None] * GQA_GROUP
        kks, vvs = [], []
        for kv in range(N_KV_HEADS):
            own2 = left2 if kv == 0 else jnp.logical_not(left2)
            kks.append(jnp.where(own2, kcat, 0.0).astype(BF16))
            vvs.append(jnp.where(own2, vcat, 0.0).astype(BF16))
        scores = []
        for hh in range(N_SWA_HEADS):
            kv, j = divmod(hh, GQA_GROUP)
            blk = 2 * j + kv
            qh = sq_ref[rows, blk * 128:(blk + 1) * 128]
            scores.append(lax.dot_general(qh, kks[kv], (((1,), (1,)), ((), ())),
                                          preferred_element_type=F32))
        probs = []
        for hh in range(N_SWA_HEADS):
            s2 = scores[hh]
            s = jnp.where(from_prev, s2[:, :CHUNK], s2[:, CHUNK:]) + bias_ref[table, hh]
            sink = sink_ref[hh]
            m = jnp.maximum(jnp.max(s, axis=-1, keepdims=True), sink)
            p = jnp.exp(s - m)
            den = jnp.sum(p, axis=-1, keepdims=True) + jnp.exp(sink - m)
            pn = p / den
            p2 = jnp.concatenate([jnp.where(from_prev, pn, 0.0), jnp.where(from_prev, 0.0, pn)], axis=-1)
            probs.append(p2.astype(BF16))
        for hh in range(N_SWA_HEADS):
            kv, j = divmod(hh, GQA_GROUP)
            o = jnp.dot(probs[hh], vvs[kv], preferred_element_type=F32)
            swa_blocks[j] = o if kv == 0 else swa_blocks[j] + o
        swa_o = jnp.concatenate(swa_blocks, axis=-1)

        mixed_ref[rows, :] = jnp.concatenate([ret_o, swa_o], axis=-1).astype(BF16)

    @pl.when(i == pl.num_programs(0) - 1)
    def _():
        st_ref[...] = state[...]


def _mix_prompt(proj, ret_consts, bias_tabs, sinks, gain):
    q, k, kz, v, g, sq, skv = proj
    rows = q.shape[0]
    tm = MIX_SUB * CHUNK
    gl, decay, xi = ret_consts
    bias_fold = bias_tabs[0]
    row_spec = lambda w: pl.BlockSpec((tm, w), lambda i: (i, 0))
    const2 = lambda a: pl.BlockSpec(a.shape, lambda i: (0, 0))
    const3 = lambda a: pl.BlockSpec(a.shape, lambda i: (0, 0, 0))
    smem = pl.BlockSpec(memory_space=pltpu.SMEM)
    mixed, st_full = pl.pallas_call(
        _mix_prompt_kernel,
        grid=(rows // tm,),
        in_specs=[smem, smem,
                  row_spec(SPLIT_W), row_spec(SPLIT_W), row_spec(SPLIT_W), row_spec(SPLIT_W), row_spec(RET_W),
                  row_spec(SPLIT_W), row_spec(2 * SWA_KVW),
                  pl.BlockSpec((CHUNK, 2 * SWA_KVW), lambda i: (jnp.maximum(MIX_SUB * i - 1, 0), 0)),
                  const3(decay), const2(xi),
                  pl.BlockSpec(bias_fold.shape, lambda i: (0, 0, 0, 0)),
                  const2(gain)],
        out_specs=[row_spec(D_MODEL),
                   pl.BlockSpec((N_RET_HEADS, 128, 128), lambda i: (0, 0, 0))],
        out_shape=[jax.ShapeDtypeStruct((rows, D_MODEL), BF16),
                   jax.ShapeDtypeStruct((N_RET_HEADS, 128, 128), F32)],
        scratch_shapes=[pltpu.VMEM((N_RET_HEADS, 128, 128), F32)],
        compiler_params=_cparams(("arbitrary",)),
        name="mix_prompt",
    )(gl, sinks, q, k, kz, v, g, sq, skv, skv, decay, xi, bias_fold, gain)
    lo, hi = slice(0, HEAD_DIM), slice(HEAD_DIM, 128)
    st = jnp.stack([st_full[h, lo, lo] if h % 2 == 0 else st_full[h, hi, hi] for h in range(N_RET_HEADS)])
    return mixed, st


def _ret_consts():
    lg = jnp.log(1.0 - 2.0 ** (-5.0 - jnp.arange(N_RET_HEADS, dtype=F32)))
    idx = jnp.arange(CHUNK, dtype=F32)
    diff = idx[:, None] - idx[None, :]
    decay = jnp.where(diff >= 0, jnp.exp(jnp.maximum(diff, 0.0)[None] * lg[:, None, None]), 0.0)
    xi = jnp.exp((idx + 1.0)[:, None] * lg[None, :])
    zeta = jnp.exp((CHUNK - 1.0 - idx)[:, None] * lg[None, :])
    gl = jnp.exp(CHUNK * lg)
    return gl, decay, jnp.repeat(xi, 128, axis=1), jnp.repeat(zeta, HEAD_DIM, axis=1), jnp.exp(1.0 * lg)


SAMPLE_BB = 8


def _mix_sample_kernel(sink_ref, gam_ref,
                       qt_ref, kt_ref, v3_ref, sq3_ref, knew_ref, vnew_ref,
                       st_ref, ck_ref, cv_ref, g3_ref, x_ref, gate_ref,
                       wout_ref, brow_ref, gain_ref, lng_ref, lnb_ref,
                       x1_ref, nst_ref, nk_ref, nv_ref, ret_scr, swa_scr):
    i = pl.program_id(0)
    row = lax.broadcasted_iota(I32, (N_SWA_HEADS, 2 * HEAD_DIM), 0)
    lane = lax.broadcasted_iota(I32, (N_SWA_HEADS, 2 * HEAD_DIM), 1)
    own_half = (row // GQA_GROUP) == (lane // HEAD_DIM)
    sink_col = jnp.concatenate(
        [jnp.full((1, 1), sink_ref[hh], F32) for hh in range(N_SWA_HEADS)], axis=0)
    brow = brow_ref[...]
    qt = qt_ref[0]
    kt = kt_ref[0]
    gam = gam_ref[...]
    seqs = range(SAMPLE_BB)
    state_shape = (N_RET_HEADS, HEAD_DIM, HEAD_DIM)

    s_old = [st_ref[0, b].reshape(RET_W, HEAD_DIM) for b in seqs]
    v_full = [jnp.concatenate([jnp.broadcast_to(v3_ref[b, h:h + 1, :], (HEAD_DIM, HEAD_DIM))
                               for h in range(N_RET_HEADS)], axis=0) for b in seqs]
    s_new = [gam * s_old[b] + kt[:, b:b + 1] * v_full[b] for b in seqs]
    for b in seqs:
        nst_ref[b] = s_new[b].reshape(state_shape)
    outs = [jnp.sum((qt[:, b:b + 1] * s_new[b]).reshape(state_shape), axis=1) for b in seqs]
    for b in seqs:
        ret_scr[i * SAMPLE_BB + b] = outs[b]

    kk = [jnp.concatenate([ck_ref[0, b, 1:, :], knew_ref[b:b + 1, :]], axis=0) for b in seqs]
    vv = [jnp.concatenate([cv_ref[0, b, 1:, :], vnew_ref[b:b + 1, :]], axis=0) for b in seqs]
    for b in seqs:
        nk_ref[b] = kk[b]
        nv_ref[b] = vv[b]
    qblk = [jnp.where(own_half, jnp.concatenate([sq3_ref[b], sq3_ref[b]], axis=-1), 0.0).astype(BF16)
            for b in seqs]
    s = [lax.dot_general(qblk[b], kk[b].astype(BF16), (((1,), (1,)), ((), ())),
                         preferred_element_type=F32) + brow for b in seqs]
    m = [jnp.maximum(jnp.max(s[b], axis=-1, keepdims=True), sink_col) for b in seqs]
    p = [jnp.exp(s[b] - m[b]) for b in seqs]
    den = [jnp.sum(p[b], axis=-1, keepdims=True) + jnp.exp(sink_col - m[b]) for b in seqs]
    pn = [(p[b] / den[b]).astype(BF16) for b in seqs]
    o = [jnp.dot(pn[b], vv[b].astype(BF16), preferred_element_type=F32) for b in seqs]
    for b in seqs:
        swa_scr[i * SAMPLE_BB + b] = jnp.where(own_half[:, :HEAD_DIM], o[b][:, :HEAD_DIM], o[b][:, HEAD_DIM:])

    @pl.when(i == pl.num_programs(0) - 1)
    def _():
        y = jnp.zeros(x_ref.shape, F32)
        for h in range(N_RET_HEADS):
            o = ret_scr[:, h, :]
            mu = jnp.mean(o, axis=-1, keepdims=True)
            c = o - mu
            var = jnp.mean(c * c, axis=-1, keepdims=True)
            r = c * lax.rsqrt(var + LN_EPS) * gain_ref[h:h + 1, :] * _silu(g3_ref[h])
            y = y + jnp.dot(r.astype(BF16), wout_ref[h * HEAD_DIM:(h + 1) * HEAD_DIM, :],
                            preferred_element_type=F32)
        for hh in range(N_SWA_HEADS):
            o = swa_scr[:, hh, :].astype(BF16)
            lo = RET_W + hh * HEAD_DIM
            y = y + jnp.dot(o, wout_ref[lo:lo + HEAD_DIM, :], preferred_element_type=F32)
        x1_ref[...] = _ln(DN_ALPHA * x_ref[...] + (1.0 + gate_ref[0]) * y, lng_ref[...], lnb_ref[...])


def _mix_sample(proj, x, mod, layer, w_out_bf, gamma1, bias_row, sinks, gain8, ln_g, ln_b,
                st, ck, cv):
    q, k, _, v, g, sq, skv = proj
    nb = x.shape[0]
    steps = nb // SAMPLE_BB

    def joined(a):
        a = a.astype(F32).reshape(nb, RET_W // 128, 2, 128)
        return (a[:, :, 0, :] + a[:, :, 1, :]).reshape(nb, RET_W)

    to_cols = lambda a: joined(a).reshape(steps, SAMPLE_BB, RET_W).transpose(0, 2, 1)
    qt, kt = to_cols(q), to_cols(k)
    v3 = joined(v).reshape(nb, N_RET_HEADS, HEAD_DIM)
    sq3 = joined(sq).reshape(nb, GQA_GROUP, N_KV_HEADS, HEAD_DIM).transpose(0, 2, 1, 3)
    sq3 = sq3.reshape(nb, N_SWA_HEADS, HEAD_DIM)
    g3 = g.reshape(nb, N_RET_HEADS, HEAD_DIM).transpose(1, 0, 2)
    knew, vnew = skv[:, :SWA_KVW], skv[:, SWA_KVW:]
    state_shape = (N_RET_HEADS, HEAD_DIM, HEAD_DIM)
    gam =jnp.broadcast_to(jnp.repeat(gamma1, HEAD_DIM)[:, None], (RET_W, HEAD_DIM))
    smem = pl.BlockSpec(memory_space=pltpu.SMEM)
    blk3 = lambda a, b, c: pl.BlockSpec((a, b, c), lambda i: (i, 0, 0))
    lay4 = lambda b, c: pl.BlockSpec((1, SAMPLE_BB, b, c), lambda i: (layer, i, 0, 0))
    const2 = lambda a: pl.BlockSpec(a.shape, lambda i: (0, 0))
    const3 = lambda a: pl.BlockSpec(a.shape, lambda i: (0, 0, 0))
    x1, nst, nk, nv = pl.pallas_call(
        _mix_sample_kernel,
        grid=(steps,),
        in_specs=[smem, const2(gam),
                  blk3(1, RET_W, SAMPLE_BB), blk3(1, RET_W, SAMPLE_BB),
                  blk3(SAMPLE_BB, N_RET_HEADS, HEAD_DIM), blk3(SAMPLE_BB, N_SWA_HEADS, HEAD_DIM),
                  pl.BlockSpec((SAMPLE_BB, SWA_KVW), lambda i: (i, 0)),
                  pl.BlockSpec((SAMPLE_BB, SWA_KVW), lambda i: (i, 0)),
                  pl.BlockSpec((1, SAMPLE_BB) + state_shape, lambda i: (layer, i, 0, 0, 0)),
                  lay4(CHUNK, SWA_KVW), lay4(CHUNK, SWA_KVW),
                  const3(g3), const2(x),
                  pl.BlockSpec((1, nb, D_MODEL), lambda i: (layer, 0, 2)),
                  const2(w_out_bf), const2(bias_row), const2(gain8), const2(ln_g), const2(ln_b)],
        out_specs=[const2(x), pl.BlockSpec((SAMPLE_BB,) + state_shape, lambda i: (i, 0, 0, 0)),
                   blk3(SAMPLE_BB, CHUNK, SWA_KVW), blk3(SAMPLE_BB, CHUNK, SWA_KVW)],
        out_shape=[jax.ShapeDtypeStruct(x.shape, F32), jax.ShapeDtypeStruct(st.shape[1:], F32),
                   jax.ShapeDtypeStruct(ck.shape[1:], F32), jax.ShapeDtypeStruct(cv.shape[1:], F32)],
        scratch_shapes=[pltpu.VMEM((nb, N_RET_HEADS, HEAD_DIM), F32),
                        pltpu.VMEM((nb, N_SWA_HEADS, HEAD_DIM), F32)],
        compiler_params=_cparams(("arbitrary",)),
        name="mix_sample",
    )(sinks, gam, qt, kt, v3, sq3, knew, vnew, st, ck, cv, g3, x, mod,
      w_out_bf, bias_row, gain8, ln_g, ln_b)
    cache_shape = (nb, CHUNK, N_KV_HEADS, HEAD_DIM)
    return x1, nst, nk.reshape(cache_shape), nv.reshape(cache_shape)


def _router_kernel(x_ref, sh_ref, sc_ref, rwt_ref, rb_ref, tri_ref,
                   oi_ref, of_ref, cnt_ref, run_ref, *, mod_rows):
    h2 = x_ref[...] * (1.0 + sc_ref[0][:mod_rows]) + sh_ref[0][:mod_rows]
    _route_rows(h2.astype(BF16), rwt_ref, rb_ref, tri_ref, oi_ref, of_ref, cnt_ref, run_ref)


def _post_mix_kernel(mixed_ref, x_ref, gate_ref, wout_ref, lng_ref, lnb_ref,
                     sh_ref, sc_ref, rwt_ref, rb_ref, tri_ref,
                     x1_ref, oi_ref, of_ref, cnt_ref, run_ref):
    y = jnp.dot(mixed_ref[...], wout_ref[...], preferred_element_type=F32)
    x1 = _ln(DN_ALPHA * x_ref[...] + (1.0 + gate_ref[0][:1]) * y, lng_ref[...], lnb_ref[...])
    x1_ref[...] = x1
    h2 = x1 * (1.0 + sc_ref[0][:1]) + sh_ref[0][:1]
    _route_rows(h2.astype(BF16), rwt_ref, rb_ref, tri_ref, oi_ref, of_ref, cnt_ref, run_ref)


def _route_rows(h2b, rwt_ref, rb_ref, tri_ref, oi_ref, of_ref, cnt_ref, run_ref):
    i = pl.program_id(0)

    @pl.when(i == 0)
    def _():
        run_ref[...] = jnp.zeros_like(run_ref)

    logits = lax.dot_general(rwt_ref[...].astype(BF16), h2b, (((1,), (1,)), ((), ())),
                             preferred_element_type=F32)
    aff = 1.0 / (1.0 + jnp.exp(-logits))
    sel = aff + rb_ref[...]
    s = [sel[e:e + 1, :] for e in range(N_EXPERTS)]
    a = [aff[e:e + 1, :] for e in range(N_EXPERTS)]

    def top2sum(v0, v1, v2, v3):
        hi01, lo01 = jnp.maximum(v0, v1), jnp.minimum(v0, v1)
        hi23, lo23 = jnp.maximum(v2, v3), jnp.minimum(v2, v3)
        return jnp.maximum(hi01, hi23) + jnp.maximum(jnp.minimum(hi01, hi23),
                                                     jnp.maximum(lo01, lo23))

    def argmax_first(vals):
        best, idx = vals[0], jnp.zeros(vals[0].shape, I32)
        for j in range(1, len(vals)):
            upd = vals[j] > best
            idx = jnp.where(upd, j, idx)
            best = jnp.where(upd, vals[j], best)
        return idx

    def pick(idx, vals):
        out = vals[-1]
        for j in range(len(vals) - 2, -1, -1):
            out = jnp.where(idx == j, vals[j], out)
        return out

    gi = argmax_first([top2sum(*s[4 * g:4 * g + 4]) for g in range(N_GROUPS)])
    sv = [pick(gi, [s[4 * g + j] for g in range(N_GROUPS)]) for j in range(EXPERTS_PER_GROUP)]
    av = [pick(gi, [a[4 * g + j] for g in range(N_GROUPS)]) for j in range(EXPERTS_PER_GROUP)]
    i1 = argmax_first(sv)
    i2 = argmax_first([jnp.where(i1 == j, -jnp.inf, sv[j]) for j in range(EXPERTS_PER_GROUP)])
    w1, w2 = pick(i1, av), pick(i2, av)
    wsum = w1 + w2
    w1, w2 = w1 / wsum, w2 / wsum
    lo, hi = jnp.minimum(i1, i2), jnp.maximum(i1, i2)
    w_lo = jnp.where(i1 < i2, w1, w2)
    w_hi = jnp.where(i1 < i2, w2, w1)
    pair = jnp.where(lo == 0, hi - 1, jnp.where(lo == 1, hi + 1, 5))
    bin_id = gi * N_PAIRS + pair

    tm = bin_id.shape[1]
    onehot = lax.broadcasted_iota(I32, (BIN_ROWS, tm), 0) == bin_id
    oh_f = jnp.where(onehot, 1.0, 0.0)
    before = jnp.dot(oh_f.astype(BF16), tri_ref[...], preferred_element_type=F32)
    run = run_ref[...]
    run_t = jnp.concatenate([run] * (tm // 128), axis=-1)
    rank = jnp.sum(oh_f * (before + run_t), axis=0, keepdims=True)
    run_new = run + jnp.sum(oh_f, axis=1, keepdims=True)
    run_ref[...] = run_new
    cnt_ref[...] = run_new.astype(I32)

    zi = jnp.zeros_like(bin_id)
    oi_ref[0] = jnp.concatenate([gi * 4 + lo, gi * 4 + hi, bin_id, rank.astype(I32), zi, zi, zi, zi], axis=0)
    zf = jnp.zeros_like(w_lo)
    of_ref[0] = jnp.concatenate([w_lo, w_hi, zf, zf, zf, zf, zf, zf], axis=0)


def _router(x1, mod, layer, mod_row0, mod_rows, router_wt, router_b, tm):
    rows = x1.shape[0]
    nt = rows // tm
    mblk = 8 if mod_rows == 1 else mod_rows
    mrow = mod_row0 // mblk
    tri = jnp.asarray(np.triu(np.ones((tm, tm), np.float32), 1), BF16)
    rb = jnp.broadcast_to(router_b.astype(F32)[:, None], (N_EXPERTS, tm))
    mod_spec = lambda c: pl.BlockSpec((1, mblk, D_MODEL), lambda i: (layer, mrow, c))
    oi, of, cnt = pl.pallas_call(
        functools.partial(_router_kernel, mod_rows=mod_rows),
        grid=(nt,),
        in_specs=[pl.BlockSpec((tm, D_MODEL), lambda i: (i, 0)), mod_spec(3), mod_spec(4),
                  pl.BlockSpec((N_EXPERTS, D_MODEL), lambda i: (0, 0)),
                  pl.BlockSpec((N_EXPERTS, tm), lambda i: (0, 0)),
                  pl.BlockSpec((tm, tm), lambda i: (0, 0))],
        out_specs=[pl.BlockSpec((1, 8, tm), lambda i: (i, 0, 0)),
                   pl.BlockSpec((1, 8, tm), lambda i: (i, 0, 0)),
                   pl.BlockSpec((BIN_ROWS, 128), lambda i: (0, 0))],
        out_shape=[jax.ShapeDtypeStruct((nt, 8, tm), I32),
                   jax.ShapeDtypeStruct((nt, 8, tm), F32),
                   jax.ShapeDtypeStruct((BIN_ROWS, 128), I32)],
        scratch_shapes=[pltpu.VMEM((BIN_ROWS, 128), F32)],
        compiler_params=_cparams(("arbitrary",)),
        name="router",
    )(x1, mod, mod, router_wt, rb, tri)
    return _unpack_route(oi, of, cnt)


def _unpack_route(oi, of, cnt):
    flat = lambda a, r: a[:, r, :].reshape(-1)
    return (flat(oi, 0), flat(oi, 1), flat(oi, 2), flat(oi, 3), flat(of, 0), flat(of, 1),
            cnt[:N_BINS, 0])


POST_TM = 512


def _post_mix(mixed, x, mod, layer, mod_row0, w_out_bf, ln_g, ln_b, router_wt, router_b):
    rows = x.shape[0]
    tm = POST_TM
    nt = rows // tm
    mrow = mod_row0 // 8
    tri = jnp.asarray(np.triu(np.ones((tm, tm), np.float32), 1), BF16)
    rb = jnp.broadcast_to(router_b.astype(F32)[:, None], (N_EXPERTS, tm))
    mod_spec = lambda c: pl.BlockSpec((1, 8, D_MODEL), lambda i: (layer, mrow, c))
    row_spec = pl.BlockSpec((tm, D_MODEL), lambda i: (i, 0))
    route_spec = pl.BlockSpec((1, 8, tm), lambda i: (i, 0, 0))
    const2 = lambda a: pl.BlockSpec(a.shape, lambda i: (0, 0))
    x1, oi, of, cnt = pl.pallas_call(
        _post_mix_kernel,
        grid=(nt,),
        in_specs=[row_spec, row_spec, mod_spec(2), const2(w_out_bf), const2(ln_g), const2(ln_b),
                  mod_spec(3), mod_spec(4), const2(router_wt), const2(rb), const2(tri)],
        out_specs=[row_spec, route_spec, route_spec,
                   pl.BlockSpec((BIN_ROWS, 128), lambda i: (0, 0))],
        out_shape=[jax.ShapeDtypeStruct((rows, D_MODEL), F32),
                   jax.ShapeDtypeStruct((nt, 8, tm), I32),
                   jax.ShapeDtypeStruct((nt, 8, tm), F32),
                   jax.ShapeDtypeStruct((BIN_ROWS, 128), I32)],
        scratch_shapes=[pltpu.VMEM((BIN_ROWS, 128), F32)],
        compiler_params=_cparams(("arbitrary",)),
        name="post_mix",
    )(mixed, x, mod, w_out_bf, ln_g, ln_b, mod, mod, router_wt, rb, tri)
    return x1, _unpack_route(oi, of, cnt)


DMA_UNROLL = 8
ROW_GROUPS = MOE_TM // DMA_UNROLL


def _for_rows(n, fn):
    def body_u(j, carry):
        for u in range(DMA_UNROLL):
            fn(j, u)
        return carry
    lax.fori_loop(0, n // DMA_UNROLL, body_u, 0)

    def body_1(r, carry):
        fn(r // DMA_UNROLL, r % DMA_UNROLL)
        return carry
    lax.fori_loop((n // DMA_UNROLL) * DMA_UNROLL, n, body_1, 0)


def _wait_rows(n, buf, sem):
    p = ROW_GROUPS
    while p >= 1:
        @pl.when((n & (p * DMA_UNROLL)) != 0)
        def _(p=p):
            pltpu.make_async_copy(buf.at[pl.ds(0, p)], buf.at[pl.ds(0, p)], sem).wait()
        p //= 2
    p = DMA_UNROLL // 2
    while p >= 1:
        @pl.when((n & p) != 0)
        def _(p=p):
            pltpu.make_async_copy(buf.at[0, pl.ds(0, p)], buf.at[0, pl.ds(0, p)], sem).wait()
        p //= 2


def _moe_sorted_kernel(ea_ref, eb_ref, nvalid_ref, pos_ref,
                       x_hbm, sh_ref, sc_ref, gate_ref, rw_ref,
                       wga_ref, wua_ref, wda_ref, wgb_ref, wub_ref, wdb_ref,
                       lng_ref, lnb_ref,
                       out_hbm,
                       inv, xbuf, obuf, wg_a, wu_a, wd_a, wg_b, wu_b, wd_b, gsem, ssem):
    i = pl.program_id(0)
    nt = pl.num_programs(0)
    slot = i % 2
    n_tok = pos_ref.shape[0]

    def start_gather(t, dst_slot):
        def one(j, u):
            tok = inv[t * MOE_TM + j * DMA_UNROLL + u]
            pltpu.make_async_copy(x_hbm.at[pl.ds(tok, 1)], xbuf.at[dst_slot, j, pl.ds(u, 1)],
                                  gsem.at[dst_slot]).start()
        _for_rows(nvalid_ref[t], one)

    @pl.when(i == 0)
    def _():
        def body(j, carry):
            for u in range(DMA_UNROLL):
                t = j * DMA_UNROLL + u
                inv[pos_ref[t]] = t
            return carry
        lax.fori_loop(0, n_tok // DMA_UNROLL, body, 0)
        xbuf[...] = jnp.zeros_like(xbuf)
        start_gather(0, 0)

    @pl.when(i >= 2)
    def _():
        _wait_rows(nvalid_ref[jnp.maximum(i - 2, 0)], obuf.at[slot], ssem.at[slot])

    @pl.when(nvalid_ref[i] > 0)
    def _():
        _wait_rows(nvalid_ref[i], xbuf.at[slot], gsem.at[slot])

        @pl.when(i + 1 < nt)
        def _():
            start_gather(jnp.minimum(i + 1, nt - 1), 1 - slot)

        @pl.when(jnp.logical_or(i == 0, ea_ref[i] != ea_ref[jnp.maximum(i - 1, 0)]))
        def _():
            wg_a[...] = wga_ref[0, 0].astype(BF16)
            wu_a[...] = wua_ref[0, 0].astype(BF16)
            wd_a[...] = wda_ref[0, 0].astype(BF16)

        @pl.when(jnp.logical_or(i == 0, eb_ref[i] != eb_ref[jnp.maximum(i - 1, 0)]))
        def _():
            wg_b[...] = wgb_ref[0, 0].astype(BF16)
            wu_b[...] = wub_ref[0, 0].astype(BF16)
            wd_b[...] = wdb_ref[0, 0].astype(BF16)

        x = xbuf[slot].reshape(MOE_TM, D_MODEL)
        h2 = (x * (1.0 + sc_ref[0][:1]) + sh_ref[0][:1]).astype(BF16)

        dot = functools.partial(jnp.dot, preferred_element_type=F32)
        logits = dot(h2, rw_ref[...])
        gate_a, gate_b = dot(h2, wg_a[...]), dot(h2, wg_b[...])
        up_a, up_b = dot(h2, wu_a[...]), dot(h2, wu_b[...])
        act_a = (_silu(gate_a) * up_a).astype(BF16)
        act_b = (_silu(gate_b) * up_b).astype(BF16)
        y_a, y_b = dot(act_a, wd_a[...]), dot(act_b, wd_b[...])

        aff = 1.0 / (1.0 + jnp.exp(-logits))
        lane = lax.broadcasted_iota(I32, aff.shape, 1)
        a_lo = jnp.sum(jnp.where(lane == ea_ref[i], aff, 0.0), axis=-1, keepdims=True)
        a_hi = jnp.sum(jnp.where(lane == eb_ref[i], aff, 0.0), axis=-1, keepdims=True)
        a_sum = a_lo + a_hi
        y = (a_lo / a_sum) * y_a + (a_hi / a_sum) * y_b
        out = _ln(DN_ALPHA * x + (1.0 + gate_ref[0][:1]) * y, lng_ref[...], lnb_ref[...])
        obuf[slot] = out.reshape(ROW_GROUPS, DMA_UNROLL, D_MODEL)

        def scatter_one(j, u):
            tok = inv[i * MOE_TM + j * DMA_UNROLL + u]
            pltpu.make_async_copy(obuf.at[slot, j, pl.ds(u, 1)], out_hbm.at[pl.ds(tok, 1)],
                                  ssem.at[slot]).start()
        _for_rows(nvalid_ref[i], scatter_one)

    @pl.when(i == nt - 1)
    def _():
        _wait_rows(nvalid_ref[jnp.maximum(i - 1, 0)], obuf.at[1 - slot], ssem.at[1 - slot])
        _wait_rows(nvalid_ref[i], obuf.at[slot], ssem.at[slot])


def _moe_sorted(x1, mod, layer, mod_row0, route, rw_pad, wg, wu, wd, ln_g, ln_b):
    _, _, bin_id, rank, _, _, counts = route
    rows = x1.shape[0]
    nt = rows // MOE_TM + N_BINS
    ns = nt * MOE_TM
    tiles = (counts + MOE_TM - 1) // MOE_TM
    tile_end = jnp.cumsum(tiles)
    row_start = ((tile_end - tiles) * MOE_TM).astype(I32)
    pos = jnp.sum(jnp.where(bin_id[:, None] == jnp.arange(N_BINS, dtype=I32)[None, :],
                            row_start[None, :], 0), axis=1) + rank
    total = tile_end[-1]
    t = jnp.arange(nt, dtype=I32)
    tile_bin = jnp.sum(jnp.minimum(t, total - 1)[:, None] >= tile_end[None, :], axis=1).astype(I32)
    tile_bin = jnp.clip(tile_bin, 0, N_BINS - 1)
    bin_lo = jnp.asarray([g * 4 + _PAIR_LO[p] for g in range(N_GROUPS) for p in range(N_PAIRS)], I32)
    bin_hi = jnp.asarray([g * 4 + _PAIR_HI[p] for g in range(N_GROUPS) for p in range(N_PAIRS)], I32)
    tile_ea, tile_eb = bin_lo[tile_bin], bin_hi[tile_bin]
    nvalid = jnp.clip(counts[tile_bin] - (t - (tile_end - tiles)[tile_bin]) * MOE_TM, 0, MOE_TM)
    nvalid = jnp.where(t < total, nvalid, 0).astype(I32)
    mrow = mod_row0 // 8
    mod_spec = lambda c: pl.BlockSpec((1, 8, D_MODEL), lambda i, *_: (layer, mrow, c))
    w_spec = lambda shape, ref_idx: pl.BlockSpec(
        (1, 1) + shape, lambda i, ea_r, eb_r, *_: (layer, (ea_r, eb_r)[ref_idx][i], 0, 0))
    any_spec = pl.BlockSpec(memory_space=pl.ANY)
    w_bf = lambda shape: pltpu.VMEM(shape, BF16)
    grid_spec = pltpu.PrefetchScalarGridSpec(
        num_scalar_prefetch=4,
        grid=(nt,),
        in_specs=[any_spec,
                  mod_spec(3), mod_spec(4), mod_spec(5),
                  pl.BlockSpec((D_MODEL, 128), lambda i, *_: (0, 0)),
                  w_spec((D_MODEL, D_FF), 0), w_spec((D_MODEL, D_FF), 0), w_spec((D_FF, D_MODEL), 0),
                  w_spec((D_MODEL, D_FF), 1), w_spec((D_MODEL, D_FF), 1), w_spec((D_FF, D_MODEL), 1),
                  pl.BlockSpec((1, D_MODEL), lambda i, *_: (0, 0)),
                  pl.BlockSpec((1, D_MODEL), lambda i, *_: (0, 0))],
        out_specs=any_spec,
        scratch_shapes=[pltpu.SMEM((ns,), I32),
                        pltpu.VMEM((2, ROW_GROUPS, DMA_UNROLL, D_MODEL), F32),
                        pltpu.VMEM((2, ROW_GROUPS, DMA_UNROLL, D_MODEL), F32),
                        w_bf((D_MODEL, D_FF)), w_bf((D_MODEL, D_FF)), w_bf((D_FF, D_MODEL)),
                        w_bf((D_MODEL, D_FF)), w_bf((D_MODEL, D_FF)), w_bf((D_FF, D_MODEL)),
                        pltpu.SemaphoreType.DMA((2,)),
                        pltpu.SemaphoreType.DMA((2,))],
    )
    return pl.pallas_call(
        _moe_sorted_kernel,
        grid_spec=grid_spec,
        out_shape=jax.ShapeDtypeStruct((rows, D_MODEL), F32),
        compiler_params=_cparams(("arbitrary",)),
        name="moe_sorted",
    )(tile_ea, tile_eb, nvalid, pos, x1, mod, mod, mod, rw_pad,
      wg, wu, wd, wg, wu, wd, ln_g, ln_b)


def _moe_dense_kernel(x_ref, dw_ref, sh_ref, sc_ref, gate_ref, wg_ref, wu_ref, wd_ref,
                      lng_ref, lnb_ref, o_ref, acc):
    e = pl.program_id(0)

    @pl.when(e == 0)
    def _():
        acc[...] = jnp.zeros_like(acc)

    x = x_ref[...]
    h2 = (x * (1.0 + sc_ref[0]) + sh_ref[0]).astype(BF16)
    a = _silu(jnp.dot(h2, wg_ref[0, 0].astype(BF16), preferred_element_type=F32)) * \
        jnp.dot(h2, wu_ref[0, 0].astype(BF16), preferred_element_type=F32)
    y = jnp.dot(a.astype(BF16), wd_ref[0, 0].astype(BF16), preferred_element_type=F32)
    acc[...] = acc[...] + dw_ref[0][:, :1] * y

    @pl.when(e == pl.num_programs(0) - 1)
    def _():
        o_ref[...] = _ln(DN_ALPHA * x + (1.0 + gate_ref[0]) * acc[...], lng_ref[...], lnb_ref[...])


def _moe_dense(x1, mod, layer, route, wg, wu, wd, ln_g, ln_b):
    ea, eb, _, _, w_lo, w_hi, _ = route
    nb = x1.shape[0]
    eids = jnp.arange(N_EXPERTS, dtype=I32)[:, None]
    dw = jnp.where(eids == ea[None, :], w_lo[None, :], 0.0) + \
        jnp.where(eids == eb[None, :], w_hi[None, :], 0.0)
    dw = jnp.broadcast_to(dw[:, :, None], (N_EXPERTS, nb, 128))
    mod_spec = lambda c: pl.BlockSpec((1, nb, D_MODEL), lambda e: (layer, 0, c))
    const2 = lambda a: pl.BlockSpec(a.shape, lambda e: (0, 0))
    return pl.pallas_call(
        _moe_dense_kernel,
        grid=(N_EXPERTS,),
        in_specs=[const2(x1), pl.BlockSpec((1, nb, 128), lambda e: (e, 0, 0)),
                  mod_spec(3), mod_spec(4), mod_spec(5),
                  pl.BlockSpec((1, 1, D_MODEL, D_FF), lambda e: (layer, e, 0, 0)),
                  pl.BlockSpec((1, 1, D_MODEL, D_FF), lambda e: (layer, e, 0, 0)),
                  pl.BlockSpec((1, 1, D_FF, D_MODEL), lambda e: (layer, e, 0, 0)),
                  const2(ln_g), const2(ln_b)],
        out_specs=const2(x1),
        out_shape=jax.ShapeDtypeStruct(x1.shape, F32),
        scratch_shapes=[pltpu.VMEM(x1.shape, F32)],
        compiler_params=_cparams(("arbitrary",)),
        name="moe_dense",
    )(x1, dw, mod, mod, mod, wg, wu, wd, ln_g, ln_b)


def kernel(x_prompt, x_sample, state_ret, cache_swa_k, cache_swa_v, c_prompt, c_sample, w_in, w_out, ret_gn_gain, swa_sinks, rel_bias_table, ada_w, ada_b, ln1_g, ln1_b, ln2_g, ln2_b, router_w, router_b, exp_w_gate, exp_w_up, exp_w_down):
    seq = x_prompt.shape[1]
    nb = x_sample.shape[0]
    past_len = 16384
    assert x_prompt.shape[0] == 1 and x_sample.shape[1] == 1

    c_all = jnp.concatenate([c_sample, c_prompt, jnp.zeros((7, D_MODEL), F32)], axis=0)
    mod = _ada(c_all, ada_w, ada_b)
    prompt_row = nb

    bias_tabs = _bias_tables(rel_bias_table.astype(F32))
    gl, decay, xi, zeta, gamma1 = _ret_consts()
    rope_p = _rope_tables(jnp.arange(seq, dtype=I32))
    rope_s = _rope_tables(jnp.full((nb,), past_len, I32))
    router_wt = router_w.astype(F32).T
    rw_pad = jnp.pad(router_w.astype(BF16), ((0, 0), (0, 128 - N_EXPERTS)))
    zeta_p = jnp.tile(zeta, (INPROJ_TM // CHUNK, 1))
    zeta_s = jnp.ones((nb, RET_W), F32)

    st_all = state_ret.astype(F32)
    ck_all = cache_swa_k.astype(F32).reshape(DEPTH, nb, CHUNK, SWA_KVW)
    cv_all = cache_swa_v.astype(F32).reshape(DEPTH, nb, CHUNK, SWA_KVW)

    xp = x_prompt.reshape(seq, D_MODEL)
    xs = x_sample.reshape(nb, D_MODEL)
    st_p, k_p, v_p, st_s, k_s, v_s = [], [], [], [], [], []
    for l in range(DEPTH):
        w_in_bf = w_in[l].astype(BF16)
        w_out_bf = w_out[l].astype(BF16)
        order = jnp.asarray(_SWA_HEAD_ORDER)
        swa_rows = w_out[l][RET_W:].reshape(N_SWA_HEADS, HEAD_DIM, D_MODEL)[order]
        w_out_prompt = jnp.concatenate([w_out[l][:RET_W], swa_rows.reshape(SWA_QW, D_MODEL)],
                                       axis=0).astype(BF16)
        experts = (exp_w_gate.astype(F32), exp_w_up.astype(F32), exp_w_down.astype(F32))
        gain = ret_gn_gain[l].astype(F32)
        sinks = swa_sinks[l].astype(F32)
        row = lambda a: a[l].astype(F32).reshape(1, D_MODEL)

        proj = _inproj(xp, mod, l, prompt_row, 1, w_in_bf, rope_p, zeta_p, INPROJ_TM)
        mixed, st = _mix_prompt(proj, (gl, decay, xi), bias_tabs, sinks, gain.reshape(1, RET_W))
        skv = proj[6]
        st_p.append(st.reshape(1, N_RET_HEADS, HEAD_DIM, HEAD_DIM))
        k_p.append(skv[seq - CHUNK:, :SWA_KVW].reshape(1, CHUNK, N_KV_HEADS, HEAD_DIM))
        v_p.append(skv[seq - CHUNK:, SWA_KVW:].reshape(1, CHUNK, N_KV_HEADS, HEAD_DIM))
        x1, route = _post_mix(mixed, xp, mod, l, prompt_row, w_out_prompt, row(ln1_g), row(ln1_b),
                              router_wt, router_b)
        xp = _moe_sorted(x1, mod, l, prompt_row, route, rw_pad, *experts, row(ln2_g), row(ln2_b))

        proj = _inproj(xs, mod, l, 0, nb, w_in_bf, rope_s, zeta_s, nb)
        x1, nst, nk, nv = _mix_sample(proj, xs, mod, l, w_out_bf, gamma1, bias_tabs[1], sinks, gain,
                                      row(ln1_g), row(ln1_b), st_all, ck_all, cv_all)
        st_s.append(nst)
        k_s.append(nk)
        v_s.append(nv)
        route = _router(x1, mod, l, 0, nb, router_wt, router_b, nb)
        xs = _moe_dense(x1, mod, l, route, *experts, row(ln2_g), row(ln2_b))

    return (xp.reshape(1, seq, D_MODEL), xs.reshape(nb, 1, D_MODEL),
            jnp.stack(st_p), jnp.stack(k_p), jnp.stack(v_p),
            jnp.stack(st_s), jnp.stack(k_s), jnp.stack(v_s))
```

```python
import functools
import math

import numpy as np
import jax
import jax.numpy as jnp
from jax import lax
from jax.experimental import pallas as pl
from jax.experimental.pallas import tpu as pltpu

F32 = jnp.float32
BF16 = jnp.bfloat16
I32 = jnp.int32

D_MODEL = 1024
DEPTH = 2
HEAD_DIM = 64
N_RET_HEADS = 8
N_SWA_HEADS = 8
N_KV_HEADS = 2
GQA_GROUP = N_SWA_HEADS // N_KV_HEADS
RET_W = N_RET_HEADS * HEAD_DIM
SWA_QW = N_SWA_HEADS * HEAD_DIM
SWA_KVW = N_KV_HEADS * HEAD_DIM
PROJ_W = 4 * RET_W + SWA_QW + 2 * SWA_KVW
CHUNK = 128
ROPE_BASE = 10000.0
N_BUCKETS = 32
MAX_DISTANCE = 128
N_EXPERTS = 16
N_GROUPS = 4
EXPERTS_PER_GROUP = 4
D_FF = 512
LN_EPS = 1e-5
DN_ALPHA = (2 * DEPTH) ** 0.25
QK_SCALE = HEAD_DIM ** -0.5
NEG_BIG = -1e30

N_PAIRS = 6
N_BINS = N_GROUPS * N_PAIRS
BIN_ROWS = 32
MOE_TM = 256
INPROJ_TM = 512
MIX_SUB = 2
SPLIT_W = 128 * N_RET_HEADS
_SWA_HEAD_ORDER = (0, 4, 1, 5, 2, 6, 3, 7)
VMEM_LIMIT = 56 * 1024 * 1024

_PAIR_LO = (0, 0, 0, 1, 1, 2)
_PAIR_HI = (1, 2, 3, 2, 3, 3)


def _cparams(sem):
    return pltpu.CompilerParams(dimension_semantics=sem, vmem_limit_bytes=VMEM_LIMIT)


def _ln(v, g, b):
    mu = jnp.mean(v, axis=-1, keepdims=True)
    c = v - mu
    var = jnp.mean(c * c, axis=-1, keepdims=True)
    return c * lax.rsqrt(var + LN_EPS) * g + b


def _silu(v):
    return v * (1.0 / (1.0 + jnp.exp(-v)))


def _ada_kernel(c_ref, w_ref, b_ref, o_ref):
    o_ref[0] = jnp.dot(c_ref[...].astype(BF16), w_ref[0].astype(BF16),
                       preferred_element_type=F32) + b_ref[0]


def _ada(c_all, ada_w, ada_b):
    rows = c_all.shape[0]
    nt = 6 * D_MODEL // 1024
    return pl.pallas_call(
        _ada_kernel,
        grid=(DEPTH, nt),
        in_specs=[pl.BlockSpec((rows, D_MODEL), lambda l, j: (0, 0)),
                  pl.BlockSpec((1, D_MODEL, 1024), lambda l, j: (l, 0, j)),
                  pl.BlockSpec((1, 1, 1024), lambda l, j: (l, 0, j))],
        out_specs=pl.BlockSpec((1, rows, 1024), lambda l, j: (l, 0, j)),
        out_shape=jax.ShapeDtypeStruct((DEPTH, rows, 6 * D_MODEL), F32),
        compiler_params=_cparams(("arbitrary", "arbitrary")),
        name="ada",
    )(c_all, ada_w, ada_b.reshape(DEPTH, 1, 6 * D_MODEL))


def _bias_kernel(tab_ref, bkt_ref, fold_ref, row_ref):
    bkt = bkt_ref[...]
    rows = lax.broadcasted_iota(I32, (CHUNK, CHUNK), 0)
    cols = lax.broadcasted_iota(I32, (CHUNK, CHUNK), 1)
    from_prev = cols > rows
    for h in range(N_SWA_HEADS):
        acc = jnp.zeros(bkt.shape, F32)
        for b in range(N_BUCKETS):
            acc = jnp.where(bkt == b, tab_ref[b, h], acc)
        own = acc[:, CHUNK:]
        fold_ref[0, h] = jnp.where(from_prev, NEG_BIG, own)
        fold_ref[1, h] = jnp.where(from_prev, acc[:, :CHUNK], own)
        row_ref[h:h + 1, :] = own[CHUNK - 1:CHUNK, :]


def _t5_bucket(rel):
    max_exact = N_BUCKETS // 2
    relf = jnp.maximum(rel, 1).astype(F32)
    large = max_exact + (jnp.log(relf / max_exact) / math.log(MAX_DISTANCE / max_exact)
                         * (N_BUCKETS - max_exact)).astype(I32)
    large = jnp.minimum(large, N_BUCKETS - 1)
    return jnp.where(rel < max_exact, rel, large)


def _bias_tables(rel_bias_table):
    qi = jnp.arange(CHUNK)
    si = jnp.arange(2 * CHUNK)
    rel = CHUNK + qi[:, None] - si[None, :]
    bkt = _t5_bucket(jnp.maximum(rel, 0)).astype(I32)
    return pl.pallas_call(
        _bias_kernel,
        in_specs=[pl.BlockSpec(memory_space=pltpu.SMEM),
                  pl.BlockSpec((CHUNK, 2 * CHUNK), lambda: (0, 0))],
        out_specs=[pl.BlockSpec((2, N_SWA_HEADS, CHUNK, CHUNK), lambda: (0, 0, 0, 0)),
                   pl.BlockSpec((N_SWA_HEADS, CHUNK), lambda: (0, 0))],
        out_shape=[jax.ShapeDtypeStruct((2, N_SWA_HEADS, CHUNK, CHUNK), F32),
                   jax.ShapeDtypeStruct((N_SWA_HEADS, CHUNK), F32)],
        name="t5_bias",
    )(rel_bias_table, bkt)


def _rotary(v, cos, s_lo, s_hi):
    outs = []
    for j in range(RET_W // 128):
        blk = v[:, j * 128:(j + 1) * 128]
        outs.append(blk * cos + pltpu.roll(blk, 96, 1) * s_lo + pltpu.roll(blk, 32, 1) * s_hi)
    return jnp.concatenate(outs, axis=-1)


def _inproj_kernel(x_ref, sh_ref, sc_ref, w_ref, cs_ref, zeta_ref,
                   q_ref, k_ref, kz_ref, v_ref, g_ref, sq_ref, skv_ref, *, mod_rows):
    sh = sh_ref[0][:mod_rows]
    sc = sc_ref[0][:mod_rows]
    h = (x_ref[...] * (1.0 + sc) + sh).astype(BF16)
    cs = jnp.concatenate([cs_ref[...], cs_ref[...]], axis=-1)
    first = (lax.broadcasted_iota(I32, cs.shape, 1) % HEAD_DIM) < (HEAD_DIM // 2)
    cos = jnp.where(first, cs, pltpu.roll(cs, 32, 1))
    s_lo = jnp.where(first, -pltpu.roll(cs, 96, 1), 0.0)
    s_hi = jnp.where(first, 0.0, cs)

    def proj(lo, hi):
        return jnp.dot(h, w_ref[:, lo:hi], preferred_element_type=F32)

    left = lax.broadcasted_iota(I32, cs.shape, 1) < HEAD_DIM

    def store_split(ref, val):
        for b in range(RET_W // 128):
            blk = val[:, b * 128:(b + 1) * 128]
            ref[:, (2 * b) * 128:(2 * b + 1) * 128] = jnp.where(left, blk, 0.0).astype(BF16)
            ref[:, (2 * b + 1) * 128:(2 * b + 2) * 128] = jnp.where(left, 0.0, blk).astype(BF16)

    store_split(q_ref, _rotary(proj(0, RET_W), cos, s_lo, s_hi))
    k = _rotary(proj(RET_W, 2 * RET_W), cos, s_lo, s_hi) * QK_SCALE
    store_split(k_ref, k)
    store_split(kz_ref, k * zeta_ref[...])
    store_split(v_ref, proj(2 * RET_W, 3 * RET_W))
    g_ref[...] = proj(3 * RET_W, 4 * RET_W)
    sq = proj(4 * RET_W, 4 * RET_W + SWA_QW) * QK_SCALE
    for hh in range(N_SWA_HEADS):
        kv, j = divmod(hh, GQA_GROUP)
        blk = sq[:, (hh // 2) * 128:(hh // 2 + 1) * 128]
        if hh % 2 != kv:
            blk = pltpu.roll(blk, HEAD_DIM, 1)
        keep = left if kv == 0 else jnp.logical_not(left)
        sq_ref[:, (2 * j + kv) * 128:(2 * j + kv + 1) * 128] = jnp.where(keep, blk, 0.0).astype(BF16)
    skv_ref[...] = proj(4 * RET_W + SWA_QW, PROJ_W)


def _inproj(x, mod, layer, mod_row0, mod_rows, w_in_bf, rope, zeta_tile, tm):
    rows = x.shape[0]
    mblk = 8 if mod_rows == 1 else mod_rows
    mrow = mod_row0 // mblk
    row_spec = lambda w: pl.BlockSpec((tm, w), lambda i: (i, 0))
    mod_spec = lambda c: pl.BlockSpec((1, mblk, D_MODEL), lambda i: (layer, mrow, c))
    return pl.pallas_call(
        functools.partial(_inproj_kernel, mod_rows=mod_rows),
        grid=(rows // tm,),
        in_specs=[row_spec(D_MODEL), mod_spec(0), mod_spec(1),
                  pl.BlockSpec((D_MODEL, PROJ_W), lambda i: (0, 0)),
                  row_spec(HEAD_DIM),
                  pl.BlockSpec((tm, RET_W), lambda i: (0, 0))],
        out_specs=[row_spec(SPLIT_W), row_spec(SPLIT_W), row_spec(SPLIT_W), row_spec(SPLIT_W),
                   row_spec(RET_W), row_spec(SPLIT_W), row_spec(2 * SWA_KVW)],
        out_shape=[jax.ShapeDtypeStruct((rows, SPLIT_W), BF16),
                   jax.ShapeDtypeStruct((rows, SPLIT_W), BF16),
                   jax.ShapeDtypeStruct((rows, SPLIT_W), BF16),
                   jax.ShapeDtypeStruct((rows, SPLIT_W), BF16),
                   jax.ShapeDtypeStruct((rows, RET_W), F32),
                   jax.ShapeDtypeStruct((rows, SPLIT_W), BF16),
                   jax.ShapeDtypeStruct((rows, 2 * SWA_KVW), F32)],
        compiler_params=_cparams(("arbitrary",)),
        name="inproj",
    )(x, mod, mod, w_in_bf, rope, zeta_tile)


def _rope_tables(pos):
    half = HEAD_DIM // 2
    inv = ROPE_BASE ** (-jnp.arange(half, dtype=F32) / half)
    ang = pos.astype(F32)[:, None] * inv[None, :]
    cos, sin = jnp.cos(ang), jnp.sin(ang)
    return jnp.concatenate([cos, sin], axis=-1)


def _mix_prompt_kernel(gl_ref, sink_ref,
                       q_ref, k_ref, kz_ref, v_ref, g_ref, sq_ref, kvc_ref, kvp_ref,
                       decay_ref, xi_ref, bias_ref, gain_ref,
                       mixed_ref, st_ref, state):
    i = pl.program_id(0)

    @pl.when(i == 0)
    def _():
        state[...] = jnp.zeros_like(state)

    gain = gain_ref[...]
    left = lax.broadcasted_iota(I32, (CHUNK, 128), 1) < HEAD_DIM
    from_prev = (lax.broadcasted_iota(I32, (CHUNK, CHUNK), 1) >
                 lax.broadcasted_iota(I32, (CHUNK, CHUNK), 0))
    inv_dim = 1.0 / HEAD_DIM

    for sub in range(MIX_SUB):
        rows = slice(sub * CHUNK, (sub + 1) * CHUNK)

        heads = range(N_RET_HEADS)
        hsl = [slice(h * 128, (h + 1) * 128) for h in heads]
        scores_r = [lax.dot_general(q_ref[rows, hsl[h]], k_ref[rows, hsl[h]], (((1,), (1,)), ((), ())),
                                    preferred_element_type=F32) for h in heads]
        s_olds = [state[h] for h in heads]
        inter = [jnp.dot(q_ref[rows, hsl[h]], s_olds[h].astype(BF16), preferred_element_type=F32)
                 for h in heads]
        upds = [lax.dot_general(kz_ref[rows, hsl[h]], v_ref[rows, hsl[h]], (((0,), (0,)), ((), ())),
                                preferred_element_type=F32) for h in heads]
        for h in heads:
            state[h] = gl_ref[h] * s_olds[h] + upds[h]
        probs_r = [(scores_r[h] * decay_ref[h]).astype(BF16) for h in heads]
        intra = [jnp.dot(probs_r[h], v_ref[rows, hsl[h]], preferred_element_type=F32) for h in heads]

        owns = [left if h % 2 == 0 else jnp.logical_not(left) for h in heads]
        outs = [intra[h] + inter[h] * xi_ref[:, hsl[h]] for h in heads]
        mus = [jnp.sum(outs[h], axis=-1, keepdims=True) * inv_dim for h in heads]
        cen = [jnp.where(owns[h], outs[h] - mus[h], 0.0) for h in heads]
        var = [jnp.sum(cen[h] * cen[h], axis=-1, keepdims=True) * inv_dim for h in heads]
        nrm = [cen[h] * lax.rsqrt(var[h] + LN_EPS) for h in heads]
        ret_blocks = [nrm[2 * b] + nrm[2 * b + 1] for b in range(N_RET_HEADS // 2)]
        ret_o = jnp.concatenate(ret_blocks, axis=-1) * gain * _silu(g_ref[rows, :])

        kvc = kvc_ref[rows, :]
        kvp = kvp_ref[...] if sub == 0 else kvc_ref[(sub - 1) * CHUNK:sub * CHUNK, :]
        kcat = jnp.concatenate([kvp[:, :SWA_KVW], kvc[:, :SWA_KVW]], axis=0)
        vcat = jnp.concatenate([kvp[:, SWA_KVW:], kvc[:, SWA_KVW:]], axis=0)
        left2 = jnp.concatenate([left, left], axis=0)
        table = jnp.minimum(i, 1) if sub == 0 else 1
        swa_blocks = [None] * GQA_GROUP
        kks, vvs = [], []
        for kv in range(N_KV_HEADS):
            own2 = left2 if kv == 0 else jnp.logical_not(left2)
            kks.append(jnp.where(own2, kcat, 0.0).astype(BF16))
            vvs.append(jnp.where(own2, vcat, 0.0).astype(BF16))
        scores = []
        for hh in range(N_SWA_HEADS):
            kv, j = divmod(hh, GQA_GROUP)
            blk = 2 * j + kv
            qh = sq_ref[rows, blk * 128:(blk + 1) * 128]
            scores.append(lax.dot_general(qh, kks[kv], (((1,), (1,)), ((), ())),
                                          preferred_element_type=F32))
        probs = []
        for hh in range(N_SWA_HEADS):
            s2 = scores[hh]
            s = jnp.where(from_prev, s2[:, :CHUNK], s2[:, CHUNK:]) + bias_ref[table, hh]
            sink = sink_ref[hh]
            m = jnp.maximum(jnp.max(s, axis=-1, keepdims=True), sink)
            p = jnp.exp(s - m)
            den = jnp.sum(p, axis=-1, keepdims=True) + jnp.exp(sink - m)
            pn = p / den
            p2 = jnp.concatenate([jnp.where(from_prev, pn, 0.0), jnp.where(from_prev, 0.0, pn)], axis=-1)
            probs.append(p2.astype(BF16))
        for hh in range(N_SWA_HEADS):
            kv, j = divmod(hh, GQA_GROUP)
            o = jnp.dot(probs[hh], vvs[kv], preferred_element_type=F32)
            swa_blocks[j] = o if kv == 0 else swa_blocks[j] + o
        swa_o = jnp.concatenate(swa_blocks, axis=-1)

        mixed_ref[rows, :] = jnp.concatenate([ret_o, swa_o], axis=-1).astype(BF16)

    @pl.when(i == pl.num_programs(0) - 1)
    def _():
        st_ref[...] = state[...]


def _mix_prompt(proj, ret_consts, bias_tabs, sinks, gain):
    q, k, kz, v, g, sq, skv = proj
    rows = q.shape[0]
    tm = MIX_SUB * CHUNK
    gl, decay, xi = ret_consts
    bias_fold = bias_tabs[0]
    row_spec = lambda w: pl.BlockSpec((tm, w), lambda i: (i, 0))
    const2 = lambda a: pl.BlockSpec(a.shape, lambda i: (0, 0))
    const3 = lambda a: pl.BlockSpec(a.shape, lambda i: (0, 0, 0))
    smem = pl.BlockSpec(memory_space=pltpu.SMEM)
    mixed, st_full = pl.pallas_call(
        _mix_prompt_kernel,
        grid=(rows // tm,),
        in_specs=[smem, smem,
                  row_spec(SPLIT_W), row_spec(SPLIT_W), row_spec(SPLIT_W), row_spec(SPLIT_W), row_spec(RET_W),
                  row_spec(SPLIT_W), row_spec(2 * SWA_KVW),
                  pl.BlockSpec((CHUNK, 2 * SWA_KVW), lambda i: (jnp.maximum(MIX_SUB * i - 1, 0), 0)),
                  const3(decay), const2(xi),
                  pl.BlockSpec(bias_fold.shape, lambda i: (0, 0, 0, 0)),
                  const2(gain)],
        out_specs=[row_spec(D_MODEL),
                   pl.BlockSpec((N_RET_HEADS, 128, 128), lambda i: (0, 0, 0))],
        out_shape=[jax.ShapeDtypeStruct((rows, D_MODEL), BF16),
                   jax.ShapeDtypeStruct((N_RET_HEADS, 128, 128), F32)],
        scratch_shapes=[pltpu.VMEM((N_RET_HEADS, 128, 128), F32)],
        compiler_params=_cparams(("arbitrary",)),
        name="mix_prompt",
    )(gl, sinks, q, k, kz, v, g, sq, skv, skv, decay, xi, bias_fold, gain)
    lo, hi = slice(0, HEAD_DIM), slice(HEAD_DIM, 128)
    st = jnp.stack([st_full[h, lo, lo] if h % 2 == 0 else st_full[h, hi, hi] for h in range(N_RET_HEADS)])
    return mixed, st


def _ret_consts():
    lg = jnp.log(1.0 - 2.0 ** (-5.0 - jnp.arange(N_RET_HEADS, dtype=F32)))
    idx = jnp.arange(CHUNK, dtype=F32)
    diff = idx[:, None] - idx[None, :]
    decay = jnp.where(diff >= 0, jnp.exp(jnp.maximum(diff, 0.0)[None] * lg[:, None, None]), 0.0)
    xi = jnp.exp((idx + 1.0)[:, None] * lg[None, :])
    zeta = jnp.exp((CHUNK - 1.0 - idx)[:, None] * lg[None, :])
    gl = jnp.exp(CHUNK * lg)
    return gl, decay, jnp.repeat(xi, 128, axis=1), jnp.repeat(zeta, HEAD_DIM, axis=1), jnp.exp(1.0 * lg)


SAMPLE_BB = 8


def _mix_sample_kernel(sink_ref, gam_ref,
                       qt_ref, kt_ref, v3_ref, sq3_ref, knew_ref, vnew_ref,
                       st_ref, ck_ref, cv_ref, g3_ref, x_ref, gate_ref,
                       wout_ref, brow_ref, gain_ref, lng_ref, lnb_ref,
                       x1_ref, nst_ref, nk_ref, nv_ref, ret_scr, swa_scr):
    i = pl.program_id(0)
    row = lax.broadcasted_iota(I32, (N_SWA_HEADS, 2 * HEAD_DIM), 0)
    lane = lax.broadcasted_iota(I32, (N_SWA_HEADS, 2 * HEAD_DIM), 1)
    own_half = (row // GQA_GROUP) == (lane // HEAD_DIM)
    sink_col = jnp.concatenate(
        [jnp.full((1, 1), sink_ref[hh], F32) for hh in range(N_SWA_HEADS)], axis=0)
    brow = brow_ref[...]
    qt = qt_ref[0]
    kt = kt_ref[0]
    gam = gam_ref[...]
    seqs = range(SAMPLE_BB)
    state_shape = (N_RET_HEADS, HEAD_DIM, HEAD_DIM)

    s_old = [st_ref[b] for b in seqs]
    v_full = [jnp.concatenate([jnp.broadcast_to(v3_ref[b, h:h + 1, :], (HEAD_DIM, HEAD_DIM))
                               for h in range(N_RET_HEADS)], axis=0) for b in seqs]
    s_new = [gam * s_old[b] + kt[:, b:b + 1] * v_full[b] for b in seqs]
    for b in seqs:
        nst_ref[b] = s_new[b]
    outs = [jnp.sum((qt[:, b:b + 1] * s_new[b]).reshape(state_shape), axis=1) for b in seqs]
    for b in seqs:
        ret_scr[i * SAMPLE_BB + b] = outs[b]

    kk = [jnp.concatenate([ck_ref[0, b, 1:, :], knew_ref[b:b + 1, :]], axis=0) for b in seqs]
    vv = [jnp.concatenate([cv_ref[0, b, 1:, :], vnew_ref[b:b + 1, :]], axis=0) for b in seqs]
    for b in seqs:
        nk_ref[b] = kk[b]
        nv_ref[b] = vv[b]
    qblk = [jnp.where(own_half, jnp.concatenate([sq3_ref[b], sq3_ref[b]], axis=-1), 0.0).astype(BF16)
            for b in seqs]
    s = [lax.dot_general(qblk[b], kk[b].astype(BF16), (((1,), (1,)), ((), ())),
                         preferred_element_type=F32) + brow for b in seqs]
    m = [jnp.maximum(jnp.max(s[b], axis=-1, keepdims=True), sink_col) for b in seqs]
    p = [jnp.exp(s[b] - m[b]) for b in seqs]
    den = [jnp.sum(p[b], axis=-1, keepdims=True) + jnp.exp(sink_col - m[b]) for b in seqs]
    pn = [(p[b] / den[b]).astype(BF16) for b in seqs]
    o = [jnp.dot(pn[b], vv[b].astype(BF16), preferred_element_type=F32) for b in seqs]
    for b in seqs:
        swa_scr[i * SAMPLE_BB + b] = jnp.where(own_half[:, :HEAD_DIM], o[b][:, :HEAD_DIM], o[b][:, HEAD_DIM:])

    @pl.when(i == pl.num_programs(0) - 1)
    def _():
        y = jnp.zeros(x_ref.shape, F32)
        for h in range(N_RET_HEADS):
            o = ret_scr[:, h, :]
            mu = jnp.mean(o, axis=-1, keepdims=True)
            c = o - mu
            var = jnp.mean(c * c, axis=-1, keepdims=True)
            r = c * lax.rsqrt(var + LN_EPS) * gain_ref[h:h + 1, :] * _silu(g3_ref[h])
            y = y + jnp.dot(r.astype(BF16), wout_ref[h * HEAD_DIM:(h + 1) * HEAD_DIM, :],
                            preferred_element_type=F32)
        for hh in range(N_SWA_HEADS):
            o = swa_scr[:, hh, :].astype(BF16)
            lo = RET_W + hh * HEAD_DIM
            y = y + jnp.dot(o, wout_ref[lo:lo + HEAD_DIM, :], preferred_element_type=F32)
        x1_ref[...] = _ln(DN_ALPHA * x_ref[...] + (1.0 + gate_ref[0]) * y, lng_ref[...], lnb_ref[...])


def _mix_sample(proj, x, mod, layer, w_out_bf, gamma1, bias_row, sinks, gain8, ln_g, ln_b,
                st, ck, cv):
    q, k, _, v, g, sq, skv = proj
    nb = x.shape[0]
    steps = nb // SAMPLE_BB

    def joined(a):
        a = a.astype(F32).reshape(nb, RET_W // 128, 2, 128)
        return (a[:, :, 0, :] + a[:, :, 1, :]).reshape(nb, RET_W)

    to_cols = lambda a: joined(a).reshape(steps, SAMPLE_BB, RET_W).transpose(0, 2, 1)
    qt, kt = to_cols(q), to_cols(k)
    v3 = joined(v).reshape(nb, N_RET_HEADS, HEAD_DIM)
    sq3 = joined(sq).reshape(nb, GQA_GROUP, N_KV_HEADS, HEAD_DIM).transpose(0, 2, 1, 3)
    sq3 = sq3.reshape(nb, N_SWA_HEADS, HEAD_DIM)
    g3 = g.reshape(nb, N_RET_HEADS, HEAD_DIM).transpose(1, 0, 2)
    knew, vnew = skv[:, :SWA_KVW], skv[:, SWA_KVW:]
    gam =jnp.broadcast_to(jnp.repeat(gamma1, HEAD_DIM)[:, None], (RET_W, HEAD_DIM))
    smem = pl.BlockSpec(memory_space=pltpu.SMEM)
    blk3 = lambda a, b, c: pl.BlockSpec((a, b, c), lambda i: (i, 0, 0))
    lay4 = lambda b, c: pl.BlockSpec((1, SAMPLE_BB, b, c), lambda i: (layer, i, 0, 0))
    const2 = lambda a: pl.BlockSpec(a.shape, lambda i: (0, 0))
    const3 = lambda a: pl.BlockSpec(a.shape, lambda i: (0, 0, 0))
    x1, nst, nk, nv = pl.pallas_call(
        _mix_sample_kernel,
        grid=(steps,),
        in_specs=[smem, const2(gam),
                  blk3(1, RET_W, SAMPLE_BB), blk3(1, RET_W, SAMPLE_BB),
                  blk3(SAMPLE_BB, N_RET_HEADS, HEAD_DIM), blk3(SAMPLE_BB, N_SWA_HEADS, HEAD_DIM),
                  pl.BlockSpec((SAMPLE_BB, SWA_KVW), lambda i: (i, 0)),
                  pl.BlockSpec((SAMPLE_BB, SWA_KVW), lambda i: (i, 0)),
                  blk3(SAMPLE_BB, RET_W, HEAD_DIM), lay4(CHUNK, SWA_KVW), lay4(CHUNK, SWA_KVW),
                  const3(g3), const2(x),
                  pl.BlockSpec((1, nb, D_MODEL), lambda i: (layer, 0, 2)),
                  const2(w_out_bf), const2(bias_row), const2(gain8), const2(ln_g), const2(ln_b)],
        out_specs=[const2(x), blk3(SAMPLE_BB, RET_W, HEAD_DIM),
                   blk3(SAMPLE_BB, CHUNK, SWA_KVW), blk3(SAMPLE_BB, CHUNK, SWA_KVW)],
        out_shape=[jax.ShapeDtypeStruct(x.shape, F32), jax.ShapeDtypeStruct(st.shape, F32),
                   jax.ShapeDtypeStruct(ck.shape[1:], F32), jax.ShapeDtypeStruct(cv.shape[1:], F32)],
        scratch_shapes=[pltpu.VMEM((nb, N_RET_HEADS, HEAD_DIM), F32),
                        pltpu.VMEM((nb, N_SWA_HEADS, HEAD_DIM), F32)],
        compiler_params=_cparams(("arbitrary",)),
        name="mix_sample",
    )(sinks, gam, qt, kt, v3, sq3, knew, vnew, st, ck, cv, g3, x, mod,
      w_out_bf, bias_row, gain8, ln_g, ln_b)
    cache_shape = (nb, CHUNK, N_KV_HEADS, HEAD_DIM)
    return (x1, nst.reshape(nb, N_RET_HEADS, HEAD_DIM, HEAD_DIM), nk.reshape(cache_shape),
            nv.reshape(cache_shape))


def _router_kernel(x_ref, sh_ref, sc_ref, rwt_ref, rb_ref, tri_ref,
                   oi_ref, of_ref, cnt_ref, run_ref, *, mod_rows):
    h2 = x_ref[...] * (1.0 + sc_ref[0][:mod_rows]) + sh_ref[0][:mod_rows]
    _route_rows(h2.astype(BF16), rwt_ref, rb_ref, tri_ref, oi_ref, of_ref, cnt_ref, run_ref)


def _post_mix_kernel(mixed_ref, x_ref, gate_ref, wout_ref, lng_ref, lnb_ref,
                     sh_ref, sc_ref, rwt_ref, rb_ref, tri_ref,
                     x1_ref, oi_ref, of_ref, cnt_ref, run_ref):
    y = jnp.dot(mixed_ref[...], wout_ref[...], preferred_element_type=F32)
    x1 = _ln(DN_ALPHA * x_ref[...] + (1.0 + gate_ref[0][:1]) * y, lng_ref[...], lnb_ref[...])
    x1_ref[...] = x1
    h2 = x1 * (1.0 + sc_ref[0][:1]) + sh_ref[0][:1]
    _route_rows(h2.astype(BF16), rwt_ref, rb_ref, tri_ref, oi_ref, of_ref, cnt_ref, run_ref)


def _route_rows(h2b, rwt_ref, rb_ref, tri_ref, oi_ref, of_ref, cnt_ref, run_ref):
    i = pl.program_id(0)

    @pl.when(i == 0)
    def _():
        run_ref[...] = jnp.zeros_like(run_ref)

    logits = lax.dot_general(rwt_ref[...].astype(BF16), h2b, (((1,), (1,)), ((), ())),
                             preferred_element_type=F32)
    aff = 1.0 / (1.0 + jnp.exp(-logits))
    sel = aff + rb_ref[...]
    s = [sel[e:e + 1, :] for e in range(N_EXPERTS)]
    a = [aff[e:e + 1, :] for e in range(N_EXPERTS)]

    def top2sum(v0, v1, v2, v3):
        hi01, lo01 = jnp.maximum(v0, v1), jnp.minimum(v0, v1)
        hi23, lo23 = jnp.maximum(v2, v3), jnp.minimum(v2, v3)
        return jnp.maximum(hi01, hi23) + jnp.maximum(jnp.minimum(hi01, hi23),
                                                     jnp.maximum(lo01, lo23))

    def argmax_first(vals):
        best, idx = vals[0], jnp.zeros(vals[0].shape, I32)
        for j in range(1, len(vals)):
            upd = vals[j] > best
            idx = jnp.where(upd, j, idx)
            best = jnp.where(upd, vals[j], best)
        return idx

    def pick(idx, vals):
        out = vals[-1]
        for j in range(len(vals) - 2, -1, -1):
            out = jnp.where(idx == j, vals[j], out)
        return out

    gi = argmax_first([top2sum(*s[4 * g:4 * g + 4]) for g in range(N_GROUPS)])
    sv = [pick(gi, [s[4 * g + j] for g in range(N_GROUPS)]) for j in range(EXPERTS_PER_GROUP)]
    av = [pick(gi, [a[4 * g + j] for g in range(N_GROUPS)]) for j in range(EXPERTS_PER_GROUP)]
    i1 = argmax_first(sv)
    i2 = argmax_first([jnp.where(i1 == j, -jnp.inf, sv[j]) for j in range(EXPERTS_PER_GROUP)])
    w1, w2 = pick(i1, av), pick(i2, av)
    wsum = w1 + w2
    w1, w2 = w1 / wsum, w2 / wsum
    lo, hi = jnp.minimum(i1, i2), jnp.maximum(i1, i2)
    w_lo = jnp.where(i1 < i2, w1, w2)
    w_hi = jnp.where(i1 < i2, w2, w1)
    pair = jnp.where(lo == 0, hi - 1, jnp.where(lo == 1, hi + 1, 5))
    bin_id = gi * N_PAIRS + pair

    tm = bin_id.shape[1]
    onehot = lax.broadcasted_iota(I32, (BIN_ROWS, tm), 0) == bin_id
    oh_f = jnp.where(onehot, 1.0, 0.0)
    before = jnp.dot(oh_f.astype(BF16), tri_ref[...], preferred_element_type=F32)
    run = run_ref[...]
    run_t = jnp.concatenate([run] * (tm // 128), axis=-1)
    rank = jnp.sum(oh_f * (before + run_t), axis=0, keepdims=True)
    run_new = run + jnp.sum(oh_f, axis=1, keepdims=True)
    run_ref[...] = run_new
    cnt_ref[...] = run_new.astype(I32)

    zi = jnp.zeros_like(bin_id)
    oi_ref[0] = jnp.concatenate([gi * 4 + lo, gi * 4 + hi, bin_id, rank.astype(I32), zi, zi, zi, zi], axis=0)
    zf = jnp.zeros_like(w_lo)
    of_ref[0] = jnp.concatenate([w_lo, w_hi, zf, zf, zf, zf, zf, zf], axis=0)


def _router(x1, mod, layer, mod_row0, mod_rows, router_wt, router_b, tm):
    rows = x1.shape[0]
    nt = rows // tm
    mblk = 8 if mod_rows == 1 else mod_rows
    mrow = mod_row0 // mblk
    tri = jnp.asarray(np.triu(np.ones((tm, tm), np.float32), 1), BF16)
    rb = jnp.broadcast_to(router_b.astype(F32)[:, None], (N_EXPERTS, tm))
    mod_spec = lambda c: pl.BlockSpec((1, mblk, D_MODEL), lambda i: (layer, mrow, c))
    oi, of, cnt = pl.pallas_call(
        functools.partial(_router_kernel, mod_rows=mod_rows),
        grid=(nt,),
        in_specs=[pl.BlockSpec((tm, D_MODEL), lambda i: (i, 0)), mod_spec(3), mod_spec(4),
                  pl.BlockSpec((N_EXPERTS, D_MODEL), lambda i: (0, 0)),
                  pl.BlockSpec((N_EXPERTS, tm), lambda i: (0, 0)),
                  pl.BlockSpec((tm, tm), lambda i: (0, 0))],
        out_specs=[pl.BlockSpec((1, 8, tm), lambda i: (i, 0, 0)),
                   pl.BlockSpec((1, 8, tm), lambda i: (i, 0, 0)),
                   pl.BlockSpec((BIN_ROWS, 128), lambda i: (0, 0))],
        out_shape=[jax.ShapeDtypeStruct((nt, 8, tm), I32),
                   jax.ShapeDtypeStruct((nt, 8, tm), F32),
                   jax.ShapeDtypeStruct((BIN_ROWS, 128), I32)],
        scratch_shapes=[pltpu.VMEM((BIN_ROWS, 128), F32)],
        compiler_params=_cparams(("arbitrary",)),
        name="router",
    )(x1, mod, mod, router_wt, rb, tri)
    return _unpack_route(oi, of, cnt)


def _unpack_route(oi, of, cnt):
    flat = lambda a, r: a[:, r, :].reshape(-1)
    return (flat(oi, 0), flat(oi, 1), flat(oi, 2), flat(oi, 3), flat(of, 0), flat(of, 1),
            cnt[:N_BINS, 0])


POST_TM = 512


def _post_mix(mixed, x, mod, layer, mod_row0, w_out_bf, ln_g, ln_b, router_wt, router_b):
    rows = x.shape[0]
    tm = POST_TM
    nt = rows // tm
    mrow = mod_row0 // 8
    tri = jnp.asarray(np.triu(np.ones((tm, tm), np.float32), 1), BF16)
    rb = jnp.broadcast_to(router_b.astype(F32)[:, None], (N_EXPERTS, tm))
    mod_spec = lambda c: pl.BlockSpec((1, 8, D_MODEL), lambda i: (layer, mrow, c))
    row_spec = pl.BlockSpec((tm, D_MODEL), lambda i: (i, 0))
    route_spec = pl.BlockSpec((1, 8, tm), lambda i: (i, 0, 0))
    const2 = lambda a: pl.BlockSpec(a.shape, lambda i: (0, 0))
    x1, oi, of, cnt = pl.pallas_call(
        _post_mix_kernel,
        grid=(nt,),
        in_specs=[row_spec, row_spec, mod_spec(2), const2(w_out_bf), const2(ln_g), const2(ln_b),
                  mod_spec(3), mod_spec(4), const2(router_wt), const2(rb), const2(tri)],
        out_specs=[row_spec, route_spec, route_spec,
                   pl.BlockSpec((BIN_ROWS, 128), lambda i: (0, 0))],
        out_shape=[jax.ShapeDtypeStruct((rows, D_MODEL), F32),
                   jax.ShapeDtypeStruct((nt, 8, tm), I32),
                   jax.ShapeDtypeStruct((nt, 8, tm), F32),
                   jax.ShapeDtypeStruct((BIN_ROWS, 128), I32)],
        scratch_shapes=[pltpu.VMEM((BIN_ROWS, 128), F32)],
        compiler_params=_cparams(("arbitrary",)),
        name="post_mix",
    )(mixed, x, mod, w_out_bf, ln_g, ln_b, mod, mod, router_wt, rb, tri)
    return x1, _unpack_route(oi, of, cnt)


DMA_UNROLL = 8
ROW_GROUPS = MOE_TM // DMA_UNROLL


def _for_rows(n, fn):
    def body_u(j, carry):
        for u in range(DMA_UNROLL):
            fn(j, u)
        return carry
    lax.fori_loop(0, n // DMA_UNROLL, body_u, 0)

    def body_1(r, carry):
        fn(r // DMA_UNROLL, r % DMA_UNROLL)
        return carry
    lax.fori_loop((n // DMA_UNROLL) * DMA_UNROLL, n, body_1, 0)


def _wait_rows(n, buf, sem):
    p = ROW_GROUPS
    while p >= 1:
        @pl.when((n & (p * DMA_UNROLL)) != 0)
        def _(p=p):
            pltpu.make_async_copy(buf.at[pl.ds(0, p)], buf.at[pl.ds(0, p)], sem).wait()
        p //= 2
    p = DMA_UNROLL // 2
    while p >= 1:
        @pl.when((n & p) != 0)
        def _(p=p):
            pltpu.make_async_copy(buf.at[0, pl.ds(0, p)], buf.at[0, pl.ds(0, p)], sem).wait()
        p //= 2


def _moe_sorted_kernel(ea_ref, eb_ref, nvalid_ref, pos_ref,
                       x_hbm, sh_ref, sc_ref, gate_ref, rw_ref,
                       wga_ref, wua_ref, wda_ref, wgb_ref, wub_ref, wdb_ref,
                       lng_ref, lnb_ref,
                       out_hbm,
                       inv, xbuf, obuf, wg_a, wu_a, wd_a, wg_b, wu_b, wd_b, gsem, ssem):
    i = pl.program_id(0)
    nt = pl.num_programs(0)
    slot = i % 2
    n_tok = pos_ref.shape[0]

    def start_gather(t, dst_slot):
        def one(j, u):
            tok = inv[t * MOE_TM + j * DMA_UNROLL + u]
            pltpu.make_async_copy(x_hbm.at[pl.ds(tok, 1)], xbuf.at[dst_slot, j, pl.ds(u, 1)],
                                  gsem.at[dst_slot]).start()
        _for_rows(nvalid_ref[t], one)

    @pl.when(i == 0)
    def _():
        def body(j, carry):
            for u in range(DMA_UNROLL):
                t = j * DMA_UNROLL + u
                inv[pos_ref[t]] = t
            return carry
        lax.fori_loop(0, n_tok // DMA_UNROLL, body, 0)
        xbuf[...] = jnp.zeros_like(xbuf)
        start_gather(0, 0)

    @pl.when(i >= 2)
    def _():
        _wait_rows(nvalid_ref[jnp.maximum(i - 2, 0)], obuf.at[slot], ssem.at[slot])

    @pl.when(nvalid_ref[i] > 0)
    def _():
        _wait_rows(nvalid_ref[i], xbuf.at[slot], gsem.at[slot])

        @pl.when(i + 1 < nt)
        def _():
            start_gather(jnp.minimum(i + 1, nt - 1), 1 - slot)

        @pl.when(jnp.logical_or(i == 0, ea_ref[i] != ea_ref[jnp.maximum(i - 1, 0)]))
        def _():
            wg_a[...] = wga_ref[0, 0].astype(BF16)
            wu_a[...] = wua_ref[0, 0].astype(BF16)
            wd_a[...] = wda_ref[0, 0].astype(BF16)

        @pl.when(jnp.logical_or(i == 0, eb_ref[i] != eb_ref[jnp.maximum(i - 1, 0)]))
        def _():
            wg_b[...] = wgb_ref[0, 0].astype(BF16)
            wu_b[...] = wub_ref[0, 0].astype(BF16)
            wd_b[...] = wdb_ref[0, 0].astype(BF16)

        x = xbuf[slot].reshape(MOE_TM, D_MODEL)
        h2 = (x * (1.0 + sc_ref[0][:1]) + sh_ref[0][:1]).astype(BF16)

        dot = functools.partial(jnp.dot, preferred_element_type=F32)
        logits = dot(h2, rw_ref[...])
        gate_a, gate_b = dot(h2, wg_a[...]), dot(h2, wg_b[...])
        up_a, up_b = dot(h2, wu_a[...]), dot(h2, wu_b[...])
        act_a = (_silu(gate_a) * up_a).astype(BF16)
        act_b = (_silu(gate_b) * up_b).astype(BF16)
        y_a, y_b = dot(act_a, wd_a[...]), dot(act_b, wd_b[...])

        aff = 1.0 / (1.0 + jnp.exp(-logits))
        lane = lax.broadcasted_iota(I32, aff.shape, 1)
        a_lo = jnp.sum(jnp.where(lane == ea_ref[i], aff, 0.0), axis=-1, keepdims=True)
        a_hi = jnp.sum(jnp.where(lane == eb_ref[i], aff, 0.0), axis=-1, keepdims=True)
        a_sum = a_lo + a_hi
        y = (a_lo / a_sum) * y_a + (a_hi / a_sum) * y_b
        out = _ln(DN_ALPHA * x + (1.0 + gate_ref[0][:1]) * y, lng_ref[...], lnb_ref[...])
        obuf[slot] = out.reshape(ROW_GROUPS, DMA_UNROLL, D_MODEL)

        def scatter_one(j, u):
            tok = inv[i * MOE_TM + j * DMA_UNROLL + u]
            pltpu.make_async_copy(obuf.at[slot, j, pl.ds(u, 1)], out_hbm.at[pl.ds(tok, 1)],
                                  ssem.at[slot]).start()
        _for_rows(nvalid_ref[i], scatter_one)

    @pl.when(i == nt - 1)
    def _():
        _wait_rows(nvalid_ref[jnp.maximum(i - 1, 0)], obuf.at[1 - slot], ssem.at[1 - slot])
        _wait_rows(nvalid_ref[i], obuf.at[slot], ssem.at[slot])


def _moe_sorted(x1, mod, layer, mod_row0, route, rw_pad, wg, wu, wd, ln_g, ln_b):
    _, _, bin_id, rank, _, _, counts = route
    rows = x1.shape[0]
    nt = rows // MOE_TM + N_BINS
    ns = nt * MOE_TM
    tiles = (counts + MOE_TM - 1) // MOE_TM
    tile_end = jnp.cumsum(tiles)
    row_start = ((tile_end - tiles) * MOE_TM).astype(I32)
    pos = jnp.sum(jnp.where(bin_id[:, None] == jnp.arange(N_BINS, dtype=I32)[None, :],
                            row_start[None, :], 0), axis=1) + rank
    total = tile_end[-1]
    t = jnp.arange(nt, dtype=I32)
    tile_bin = jnp.sum(jnp.minimum(t, total - 1)[:, None] >= tile_end[None, :], axis=1).astype(I32)
    tile_bin = jnp.clip(tile_bin, 0, N_BINS - 1)
    bin_lo = jnp.asarray([g * 4 + _PAIR_LO[p] for g in range(N_GROUPS) for p in range(N_PAIRS)], I32)
    bin_hi = jnp.asarray([g * 4 + _PAIR_HI[p] for g in range(N_GROUPS) for p in range(N_PAIRS)], I32)
    tile_ea, tile_eb = bin_lo[tile_bin], bin_hi[tile_bin]
    nvalid = jnp.clip(counts[tile_bin] - (t - (tile_end - tiles)[tile_bin]) * MOE_TM, 0, MOE_TM)
    nvalid = jnp.where(t < total, nvalid, 0).astype(I32)
    mrow = mod_row0 // 8
    mod_spec = lambda c: pl.BlockSpec((1, 8, D_MODEL), lambda i, *_: (layer, mrow, c))
    w_spec = lambda shape, ref_idx: pl.BlockSpec(
        (1, 1) + shape, lambda i, ea_r, eb_r, *_: (layer, (ea_r, eb_r)[ref_idx][i], 0, 0))
    any_spec = pl.BlockSpec(memory_space=pl.ANY)
    w_bf = lambda shape: pltpu.VMEM(shape, BF16)
    grid_spec = pltpu.PrefetchScalarGridSpec(
        num_scalar_prefetch=4,
        grid=(nt,),
        in_specs=[any_spec,
                  mod_spec(3), mod_spec(4), mod_spec(5),
                  pl.BlockSpec((D_MODEL, 128), lambda i, *_: (0, 0)),
                  w_spec((D_MODEL, D_FF), 0), w_spec((D_MODEL, D_FF), 0), w_spec((D_FF, D_MODEL), 0),
                  w_spec((D_MODEL, D_FF), 1), w_spec((D_MODEL, D_FF), 1), w_spec((D_FF, D_MODEL), 1),
                  pl.BlockSpec((1, D_MODEL), lambda i, *_: (0, 0)),
                  pl.BlockSpec((1, D_MODEL), lambda i, *_: (0, 0))],
        out_specs=any_spec,
        scratch_shapes=[pltpu.SMEM((ns,), I32),
                        pltpu.VMEM((2, ROW_GROUPS, DMA_UNROLL, D_MODEL), F32),
                        pltpu.VMEM((2, ROW_GROUPS, DMA_UNROLL, D_MODEL), F32),
                        w_bf((D_MODEL, D_FF)), w_bf((D_MODEL, D_FF)), w_bf((D_FF, D_MODEL)),
                        w_bf((D_MODEL, D_FF)), w_bf((D_MODEL, D_FF)), w_bf((D_FF, D_MODEL)),
                        pltpu.SemaphoreType.DMA((2,)),
                        pltpu.SemaphoreType.DMA((2,))],
    )
    return pl.pallas_call(
        _moe_sorted_kernel,
        grid_spec=grid_spec,
        out_shape=jax.ShapeDtypeStruct((rows, D_MODEL), F32),
        compiler_params=_cparams(("arbitrary",)),
        name="moe_sorted",
    )(tile_ea, tile_eb, nvalid, pos, x1, mod, mod, mod, rw_pad,
      wg, wu, wd, wg, wu, wd, ln_g, ln_b)


def _moe_dense_kernel(x_ref, dw_ref, sh_ref, sc_ref, gate_ref, wg_ref, wu_ref, wd_ref,
                      lng_ref, lnb_ref, o_ref, acc):
    e = pl.program_id(0)

    @pl.when(e == 0)
    def _():
        acc[...] = jnp.zeros_like(acc)

    x = x_ref[...]
    h2 = (x * (1.0 + sc_ref[0]) + sh_ref[0]).astype(BF16)
    a = _silu(jnp.dot(h2, wg_ref[0, 0].astype(BF16), preferred_element_type=F32)) * \
        jnp.dot(h2, wu_ref[0, 0].astype(BF16), preferred_element_type=F32)
    y = jnp.dot(a.astype(BF16), wd_ref[0, 0].astype(BF16), preferred_element_type=F32)
    acc[...] = acc[...] + dw_ref[0][:, :1] * y

    @pl.when(e == pl.num_programs(0) - 1)
    def _():
        o_ref[...] = _ln(DN_ALPHA * x + (1.0 + gate_ref[0]) * acc[...], lng_ref[...], lnb_ref[...])


def _moe_dense(x1, mod, layer, route, wg, wu, wd, ln_g, ln_b):
    ea, eb, _, _, w_lo, w_hi, _ = route
    nb = x1.shape[0]
    eids = jnp.arange(N_EXPERTS, dtype=I32)[:, None]
    dw = jnp.where(eids == ea[None, :], w_lo[None, :], 0.0) + \
        jnp.where(eids == eb[None, :], w_hi[None, :], 0.0)
    dw = jnp.broadcast_to(dw[:, :, None], (N_EXPERTS, nb, 128))
    mod_spec = lambda c: pl.BlockSpec((1, nb, D_MODEL), lambda e: (layer, 0, c))
    const2 = lambda a: pl.BlockSpec(a.shape, lambda e: (0, 0))
    return pl.pallas_call(
        _moe_dense_kernel,
        grid=(N_EXPERTS,),
        in_specs=[const2(x1), pl.BlockSpec((1, nb, 128), lambda e: (e, 0, 0)),
                  mod_spec(3), mod_spec(4), mod_spec(5),
                  pl.BlockSpec((1, 1, D_MODEL, D_FF), lambda e: (layer, e, 0, 0)),
                  pl.BlockSpec((1, 1, D_MODEL, D_FF), lambda e: (layer, e, 0, 0)),
                  pl.BlockSpec((1, 1, D_FF, D_MODEL), lambda e: (layer, e, 0, 0)),
                  const2(ln_g), const2(ln_b)],
        out_specs=const2(x1),
        out_shape=jax.ShapeDtypeStruct(x1.shape, F32),
        scratch_shapes=[pltpu.VMEM(x1.shape, F32)],
        compiler_params=_cparams(("arbitrary",)),
        name="moe_dense",
    )(x1, dw, mod, mod, mod, wg, wu, wd, ln_g, ln_b)


def kernel(x_prompt, x_sample, state_ret, cache_swa_k, cache_swa_v, c_prompt, c_sample, w_in, w_out, ret_gn_gain, swa_sinks, rel_bias_table, ada_w, ada_b, ln1_g, ln1_b, ln2_g, ln2_b, router_w, router_b, exp_w_gate, exp_w_up, exp_w_down):
    seq = x_prompt.shape[1]
    nb = x_sample.shape[0]
    past_len = 16384
    assert x_prompt.shape[0] == 1 and x_sample.shape[1] == 1

    c_all = jnp.concatenate([c_sample, c_prompt, jnp.zeros((7, D_MODEL), F32)], axis=0)
    mod = _ada(c_all, ada_w, ada_b)
    prompt_row = nb

    bias_tabs = _bias_tables(rel_bias_table.astype(F32))
    gl, decay, xi, zeta, gamma1 = _ret_consts()
    rope_p = _rope_tables(jnp.arange(seq, dtype=I32))
    rope_s = _rope_tables(jnp.full((nb,), past_len, I32))
    router_wt = router_w.astype(F32).T
    rw_pad = jnp.pad(router_w.astype(BF16), ((0, 0), (0, 128 - N_EXPERTS)))
    zeta_p = jnp.tile(zeta, (INPROJ_TM // CHUNK, 1))
    zeta_s = jnp.ones((nb, RET_W), F32)

    ck_all = cache_swa_k.astype(F32).reshape(DEPTH, nb, CHUNK, SWA_KVW)
    cv_all = cache_swa_v.astype(F32).reshape(DEPTH, nb, CHUNK, SWA_KVW)

    xp = x_prompt.reshape(seq, D_MODEL)
    xs = x_sample.reshape(nb, D_MODEL)
    st_p, k_p, v_p, st_s, k_s, v_s = [], [], [], [], [], []
    for l in range(DEPTH):
        w_in_bf = w_in[l].astype(BF16)
        w_out_bf = w_out[l].astype(BF16)
        order = jnp.asarray(_SWA_HEAD_ORDER)
        swa_rows = w_out[l][RET_W:].reshape(N_SWA_HEADS, HEAD_DIM, D_MODEL)[order]
        w_out_prompt = jnp.concatenate([w_out[l][:RET_W], swa_rows.reshape(SWA_QW, D_MODEL)],
                                       axis=0).astype(BF16)
        experts = (exp_w_gate.astype(F32), exp_w_up.astype(F32), exp_w_down.astype(F32))
        gain = ret_gn_gain[l].astype(F32)
        sinks = swa_sinks[l].astype(F32)
        row = lambda a: a[l].astype(F32).reshape(1, D_MODEL)

        proj = _inproj(xp, mod, l, prompt_row, 1, w_in_bf, rope_p, zeta_p, INPROJ_TM)
        mixed, st = _mix_prompt(proj, (gl, decay, xi), bias_tabs, sinks, gain.reshape(1, RET_W))
        skv = proj[6]
        st_p.append(st.reshape(1, N_RET_HEADS, HEAD_DIM, HEAD_DIM))
        k_p.append(skv[seq - CHUNK:, :SWA_KVW].reshape(1, CHUNK, N_KV_HEADS, HEAD_DIM))
        v_p.append(skv[seq - CHUNK:, SWA_KVW:].reshape(1, CHUNK, N_KV_HEADS, HEAD_DIM))
        x1, route = _post_mix(mixed, xp, mod, l, prompt_row, w_out_prompt, row(ln1_g), row(ln1_b),
                              router_wt, router_b)
        xp = _moe_sorted(x1, mod, l, prompt_row, route, rw_pad, *experts, row(ln2_g), row(ln2_b))

        proj = _inproj(xs, mod, l, 0, nb, w_in_bf, rope_s, zeta_s, nb)
        x1, nst, nk, nv = _mix_sample(proj, xs, mod, l, w_out_bf, gamma1, bias_tabs[1], sinks, gain,
                                      row(ln1_g), row(ln1_b),
                                      state_ret[l].astype(F32).reshape(nb, RET_W, HEAD_DIM), ck_all, cv_all)
        st_s.append(nst)
        k_s.append(nk)
        v_s.append(nv)
        route = _router(x1, mod, l, 0, nb, router_wt, router_b, nb)
        xs = _moe_dense(x1, mod, l, route, *experts, row(ln2_g), row(ln2_b))

    return (xp.reshape(1, seq, D_MODEL), xs.reshape(nb, 1, D_MODEL),
            jnp.stack(st_p), jnp.stack(k_p), jnp.stack(v_p),
            jnp.stack(st_s), jnp.stack(k_s), jnp.stack(v_s))
```

```python
import functools
import math

import numpy as np
import jax
import jax.numpy as jnp
from jax import lax
from jax.experimental import pallas as pl
from jax.experimental.pallas import tpu as pltpu

F32 = jnp.float32
BF16 = jnp.bfloat16
I32 = jnp.int32

D_MODEL = 1024
DEPTH = 2
HEAD_DIM = 64
N_RET_HEADS = 8
N_SWA_HEADS = 8
N_KV_HEADS = 2
GQA_GROUP = N_SWA_HEADS // N_KV_HEADS
RET_W = N_RET_HEADS * HEAD_DIM
SWA_QW = N_SWA_HEADS * HEAD_DIM
SWA_KVW = N_KV_HEADS * HEAD_DIM
PROJ_W = 4 * RET_W + SWA_QW + 2 * SWA_KVW
CHUNK = 128
ROPE_BASE = 10000.0
N_BUCKETS = 32
MAX_DISTANCE = 128
N_EXPERTS = 16
N_GROUPS = 4
EXPERTS_PER_GROUP = 4
D_FF = 512
LN_EPS = 1e-5
DN_ALPHA = (2 * DEPTH) ** 0.25
QK_SCALE = HEAD_DIM ** -0.5
NEG_BIG = -1e30

N_PAIRS = 6
N_BINS = N_GROUPS * N_PAIRS
BIN_ROWS = 32
MOE_TM = 256
INPROJ_TM = 512
MIX_SUB = 4
SPLIT_W = 128 * N_RET_HEADS
_SWA_HEAD_ORDER = (0, 4, 1, 5, 2, 6, 3, 7)
VMEM_LIMIT = 56 * 1024 * 1024

_PAIR_LO = (0, 0, 0, 1, 1, 2)
_PAIR_HI = (1, 2, 3, 2, 3, 3)


def _cparams(sem):
    return pltpu.CompilerParams(dimension_semantics=sem, vmem_limit_bytes=VMEM_LIMIT)


def _ln(v, g, b):
    mu = jnp.mean(v, axis=-1, keepdims=True)
    c = v - mu
    var = jnp.mean(c * c, axis=-1, keepdims=True)
    return c * lax.rsqrt(var + LN_EPS) * g + b


def _silu(v):
    return v * (1.0 / (1.0 + jnp.exp(-v)))


def _ada_kernel(c_ref, w_ref, b_ref, o_ref):
    o_ref[0] = jnp.dot(c_ref[...].astype(BF16), w_ref[0].astype(BF16),
                       preferred_element_type=F32) + b_ref[0]


def _ada(c_all, ada_w, ada_b):
    rows = c_all.shape[0]
    nt = 6 * D_MODEL // 1024
    return pl.pallas_call(
        _ada_kernel,
        grid=(DEPTH, nt),
        in_specs=[pl.BlockSpec((rows, D_MODEL), lambda l, j: (0, 0)),
                  pl.BlockSpec((1, D_MODEL, 1024), lambda l, j: (l, 0, j)),
                  pl.BlockSpec((1, 1, 1024), lambda l, j: (l, 0, j))],
        out_specs=pl.BlockSpec((1, rows, 1024), lambda l, j: (l, 0, j)),
        out_shape=jax.ShapeDtypeStruct((DEPTH, rows, 6 * D_MODEL), F32),
        compiler_params=_cparams(("arbitrary", "arbitrary")),
        name="ada",
    )(c_all, ada_w, ada_b.reshape(DEPTH, 1, 6 * D_MODEL))


def _bias_kernel(tab_ref, bkt_ref, fold_ref, row_ref):
    bkt = bkt_ref[...]
    rows = lax.broadcasted_iota(I32, (CHUNK, CHUNK), 0)
    cols = lax.broadcasted_iota(I32, (CHUNK, CHUNK), 1)
    from_prev = cols > rows
    for h in range(N_SWA_HEADS):
        acc = jnp.zeros(bkt.shape, F32)
        for b in range(N_BUCKETS):
            acc = jnp.where(bkt == b, tab_ref[b, h], acc)
        own = acc[:, CHUNK:]
        fold_ref[0, h] = jnp.where(from_prev, NEG_BIG, own)
        fold_ref[1, h] = jnp.where(from_prev, acc[:, :CHUNK], own)
        row_ref[h:h + 1, :] = own[CHUNK - 1:CHUNK, :]


def _t5_bucket(rel):
    max_exact = N_BUCKETS // 2
    relf = jnp.maximum(rel, 1).astype(F32)
    large = max_exact + (jnp.log(relf / max_exact) / math.log(MAX_DISTANCE / max_exact)
                         * (N_BUCKETS - max_exact)).astype(I32)
    large = jnp.minimum(large, N_BUCKETS - 1)
    return jnp.where(rel < max_exact, rel, large)


def _bias_tables(rel_bias_table):
    qi = jnp.arange(CHUNK)
    si = jnp.arange(2 * CHUNK)
    rel = CHUNK + qi[:, None] - si[None, :]
    bkt = _t5_bucket(jnp.maximum(rel, 0)).astype(I32)
    return pl.pallas_call(
        _bias_kernel,
        in_specs=[pl.BlockSpec(memory_space=pltpu.SMEM),
                  pl.BlockSpec((CHUNK, 2 * CHUNK), lambda: (0, 0))],
        out_specs=[pl.BlockSpec((2, N_SWA_HEADS, CHUNK, CHUNK), lambda: (0, 0, 0, 0)),
                   pl.BlockSpec((N_SWA_HEADS, CHUNK), lambda: (0, 0))],
        out_shape=[jax.ShapeDtypeStruct((2, N_SWA_HEADS, CHUNK, CHUNK), F32),
                   jax.ShapeDtypeStruct((N_SWA_HEADS, CHUNK), F32)],
        name="t5_bias",
    )(rel_bias_table, bkt)


def _rotary(v, cos, s_lo, s_hi):
    outs = []
    for j in range(RET_W // 128):
        blk = v[:, j * 128:(j + 1) * 128]
        outs.append(blk * cos + pltpu.roll(blk, 96, 1) * s_lo + pltpu.roll(blk, 32, 1) * s_hi)
    return jnp.concatenate(outs, axis=-1)


def _inproj_kernel(x_ref, sh_ref, sc_ref, w_ref, cs_ref, zeta_ref,
                   q_ref, k_ref, kz_ref, v_ref, g_ref, sq_ref, skv_ref, *, mod_rows):
    sh = sh_ref[0][:mod_rows]
    sc = sc_ref[0][:mod_rows]
    h = (x_ref[...] * (1.0 + sc) + sh).astype(BF16)
    cs = jnp.concatenate([cs_ref[...], cs_ref[...]], axis=-1)
    first = (lax.broadcasted_iota(I32, cs.shape, 1) % HEAD_DIM) < (HEAD_DIM // 2)
    cos = jnp.where(first, cs, pltpu.roll(cs, 32, 1))
    s_lo = jnp.where(first, -pltpu.roll(cs, 96, 1), 0.0)
    s_hi = jnp.where(first, 0.0, cs)

    def proj(lo, hi):
        return jnp.dot(h, w_ref[:, lo:hi], preferred_element_type=F32)

    left = lax.broadcasted_iota(I32, cs.shape, 1) < HEAD_DIM

    def store_split(ref, val):
        for b in range(RET_W // 128):
            blk = val[:, b * 128:(b + 1) * 128]
            ref[:, (2 * b) * 128:(2 * b + 1) * 128] = jnp.where(left, blk, 0.0).astype(BF16)
            ref[:, (2 * b + 1) * 128:(2 * b + 2) * 128] = jnp.where(left, 0.0, blk).astype(BF16)

    store_split(q_ref, _rotary(proj(0, RET_W), cos, s_lo, s_hi))
    k = _rotary(proj(RET_W, 2 * RET_W), cos, s_lo, s_hi) * QK_SCALE
    store_split(k_ref, k)
    store_split(kz_ref, k * zeta_ref[...])
    store_split(v_ref, proj(2 * RET_W, 3 * RET_W))
    g_ref[...] = proj(3 * RET_W, 4 * RET_W)
    sq = proj(4 * RET_W, 4 * RET_W + SWA_QW) * QK_SCALE
    for hh in range(N_SWA_HEADS):
        kv, j = divmod(hh, GQA_GROUP)
        blk = sq[:, (hh // 2) * 128:(hh // 2 + 1) * 128]
        if hh % 2 != kv:
            blk = pltpu.roll(blk, HEAD_DIM, 1)
        keep = left if kv == 0 else jnp.logical_not(left)
        sq_ref[:, (2 * j + kv) * 128:(2 * j + kv + 1) * 128] = jnp.where(keep, blk, 0.0).astype(BF16)
    skv_ref[...] = proj(4 * RET_W + SWA_QW, PROJ_W)


def _inproj(x, mod, layer, mod_row0, mod_rows, w_in_bf, rope, zeta_tile, tm):
    rows = x.shape[0]
    mblk = 8 if mod_rows == 1 else mod_rows
    mrow = mod_row0 // mblk
    row_spec = lambda w: pl.BlockSpec((tm, w), lambda i: (i, 0))
    mod_spec = lambda c: pl.BlockSpec((1, mblk, D_MODEL), lambda i: (layer, mrow, c))
    return pl.pallas_call(
        functools.partial(_inproj_kernel, mod_rows=mod_rows),
        grid=(rows // tm,),
        in_specs=[row_spec(D_MODEL), mod_spec(0), mod_spec(1),
                  pl.BlockSpec((D_MODEL, PROJ_W), lambda i: (0, 0)),
                  row_spec(HEAD_DIM),
                  pl.BlockSpec((tm, RET_W), lambda i: (0, 0))],
        out_specs=[row_spec(SPLIT_W), row_spec(SPLIT_W), row_spec(SPLIT_W), row_spec(SPLIT_W),
                   row_spec(RET_W), row_spec(SPLIT_W), row_spec(2 * SWA_KVW)],
        out_shape=[jax.ShapeDtypeStruct((rows, SPLIT_W), BF16),
                   jax.ShapeDtypeStruct((rows, SPLIT_W), BF16),
                   jax.ShapeDtypeStruct((rows, SPLIT_W), BF16),
                   jax.ShapeDtypeStruct((rows, SPLIT_W), BF16),
                   jax.ShapeDtypeStruct((rows, RET_W), F32),
                   jax.ShapeDtypeStruct((rows, SPLIT_W), BF16),
                   jax.ShapeDtypeStruct((rows, 2 * SWA_KVW), F32)],
        compiler_params=_cparams(("arbitrary",)),
        name="inproj",
    )(x, mod, mod, w_in_bf, rope, zeta_tile)


def _rope_tables(pos):
    half = HEAD_DIM // 2
    inv = ROPE_BASE ** (-jnp.arange(half, dtype=F32) / half)
    ang = pos.astype(F32)[:, None] * inv[None, :]
    cos, sin = jnp.cos(ang), jnp.sin(ang)
    return jnp.concatenate([cos, sin], axis=-1)


def _mix_prompt_kernel(gl_ref, sink_ref,
                       q_ref, k_ref, kz_ref, v_ref, g_ref, sq_ref, kvc_ref, kvp_ref,
                       decay_ref, xi_ref, bias_ref, gain_ref,
                       mixed_ref, st_ref, state):
    i = pl.program_id(0)

    @pl.when(i == 0)
    def _():
        state[...] = jnp.zeros_like(state)

    gain = gain_ref[...]
    left = lax.broadcasted_iota(I32, (CHUNK, 128), 1) < HEAD_DIM
    from_prev = (lax.broadcasted_iota(I32, (CHUNK, CHUNK), 1) >
                 lax.broadcasted_iota(I32, (CHUNK, CHUNK), 0))
    inv_dim = 1.0 / HEAD_DIM

    for sub in range(MIX_SUB):
        rows = slice(sub * CHUNK, (sub + 1) * CHUNK)

        heads = range(N_RET_HEADS)
        hsl = [slice(h * 128, (h + 1) * 128) for h in heads]
        scores_r = [lax.dot_general(q_ref[rows, hsl[h]], k_ref[rows, hsl[h]], (((1,), (1,)), ((), ())),
                                    preferred_element_type=F32) for h in heads]
        s_olds = [state[h] for h in heads]
        inter = [jnp.dot(q_ref[rows, hsl[h]], s_olds[h].astype(BF16), preferred_element_type=F32)
                 for h in heads]
        upds = [lax.dot_general(kz_ref[rows, hsl[h]], v_ref[rows, hsl[h]], (((0,), (0,)), ((), ())),
                                preferred_element_type=F32) for h in heads]
        for h in heads:
            state[h] = gl_ref[h] * s_olds[h] + upds[h]
        probs_r = [(scores_r[h] * decay_ref[h]).astype(BF16) for h in heads]
        intra = [jnp.dot(probs_r[h], v_ref[rows, hsl[h]], preferred_element_type=F32) for h in heads]

        owns = [left if h % 2 == 0 else jnp.logical_not(left) for h in heads]
        outs = [intra[h] + inter[h] * xi_ref[:, hsl[h]] for h in heads]
        mus = [jnp.sum(outs[h], axis=-1, keepdims=True) * inv_dim for h in heads]
        cen = [jnp.where(owns[h], outs[h] - mus[h], 0.0) for h in heads]
        var = [jnp.sum(cen[h] * cen[h], axis=-1, keepdims=True) * inv_dim for h in heads]
        nrm = [cen[h] * lax.rsqrt(var[h] + LN_EPS) for h in heads]
        ret_blocks = [nrm[2 * b] + nrm[2 * b + 1] for b in range(N_RET_HEADS // 2)]
        ret_o = jnp.concatenate(ret_blocks, axis=-1) * gain * _silu(g_ref[rows, :])

        kvc = kvc_ref[rows, :]
        kvp = kvp_ref[...] if sub == 0 else kvc_ref[(sub - 1) * CHUNK:sub * CHUNK, :]
        kcat = jnp.concatenate([kvp[:, :SWA_KVW], kvc[:, :SWA_KVW]], axis=0)
        vcat = jnp.concatenate([kvp[:, SWA_KVW:], kvc[:, SWA_KVW:]], axis=0)
        left2 = jnp.concatenate([left, left], axis=0)
        table = jnp.minimum(i, 1) if sub == 0 else 1
        swa_blocks = [None] * GQA_GROUP
        kks, vvs = [], []
        for kv in range(N_KV_HEADS):
            own2 = left2 if kv == 0 else jnp.logical_not(left2)
            kks.append(jnp.where(own2, kcat, 0.0).astype(BF16))
            vvs.append(jnp.where(own2, vcat, 0.0).astype(BF16))
        scores = []
        for hh in range(N_SWA_HEADS):
            kv, j = divmod(hh, GQA_GROUP)
            blk = 2 * j + kv
            qh = sq_ref[rows, blk * 128:(blk + 1) * 128]
            scores.append(lax.dot_general(qh, kks[kv], (((1,), (1,)), ((), ())),
                                          preferred_element_type=F32))
        probs = []
        for hh in range(N_SWA_HEADS):
            s2 = scores[hh]
            s = jnp.where(from_prev, s2[:, :CHUNK], s2[:, CHUNK:]) + bias_ref[table, hh]
            sink = sink_ref[hh]
            m = jnp.maximum(jnp.max(s, axis=-1, keepdims=True), sink)
            p = jnp.exp(s - m)
            den = jnp.sum(p, axis=-1, keepdims=True) + jnp.exp(sink - m)
            pn = p / den
            p2 = jnp.concatenate([jnp.where(from_prev, pn, 0.0), jnp.where(from_prev, 0.0, pn)], axis=-1)
            probs.append(p2.astype(BF16))
        for hh in range(N_SWA_HEADS):
            kv, j = divmod(hh, GQA_GROUP)
            o = jnp.dot(probs[hh], vvs[kv], preferred_element_type=F32)
            swa_blocks[j] = o if kv == 0 else swa_blocks[j] + o
        swa_o = jnp.concatenate(swa_blocks, axis=-1)

        mixed_ref[rows, :] = jnp.concatenate([ret_o, swa_o], axis=-1).astype(BF16)

    @pl.when(i == pl.num_programs(0) - 1)
    def _():
        st_ref[...] = state[...]


def _mix_prompt(proj, ret_consts, bias_tabs, sinks, gain):
    q, k, kz, v, g, sq, skv = proj
    rows = q.shape[0]
    tm = MIX_SUB * CHUNK
    gl, decay, xi = ret_consts
    bias_fold = bias_tabs[0]
    row_spec = lambda w: pl.BlockSpec((tm, w), lambda i: (i, 0))
    const2 = lambda a: pl.BlockSpec(a.shape, lambda i: (0, 0))
    const3 = lambda a: pl.BlockSpec(a.shape, lambda i: (0, 0, 0))
    smem = pl.BlockSpec(memory_space=pltpu.SMEM)
    mixed, st_full = pl.pallas_call(
        _mix_prompt_kernel,
        grid=(rows // tm,),
        in_specs=[smem, smem,
                  row_spec(SPLIT_W), row_spec(SPLIT_W), row_spec(SPLIT_W), row_spec(SPLIT_W), row_spec(RET_W),
                  row_spec(SPLIT_W), row_spec(2 * SWA_KVW),
                  pl.BlockSpec((CHUNK, 2 * SWA_KVW), lambda i: (jnp.maximum(MIX_SUB * i - 1, 0), 0)),
                  const3(decay), const2(xi),
                  pl.BlockSpec(bias_fold.shape, lambda i: (0, 0, 0, 0)),
                  const2(gain)],
        out_specs=[row_spec(D_MODEL),
                   pl.BlockSpec((N_RET_HEADS, 128, 128), lambda i: (0, 0, 0))],
        out_shape=[jax.ShapeDtypeStruct((rows, D_MODEL), BF16),
                   jax.ShapeDtypeStruct((N_RET_HEADS, 128, 128), F32)],
        scratch_shapes=[pltpu.VMEM((N_RET_HEADS, 128, 128), F32)],
        compiler_params=_cparams(("arbitrary",)),
        name="mix_prompt",
    )(gl, sinks, q, k, kz, v, g, sq, skv, skv, decay, xi, bias_fold, gain)
    lo, hi = slice(0, HEAD_DIM), slice(HEAD_DIM, 128)
    st = jnp.stack([st_full[h, lo, lo] if h % 2 == 0 else st_full[h, hi, hi] for h in range(N_RET_HEADS)])
    return mixed, st


def _ret_consts():
    lg = jnp.log(1.0 - 2.0 ** (-5.0 - jnp.arange(N_RET_HEADS, dtype=F32)))
    idx = jnp.arange(CHUNK, dtype=F32)
    diff = idx[:, None] - idx[None, :]
    decay = jnp.where(diff >= 0, jnp.exp(jnp.maximum(diff, 0.0)[None] * lg[:, None, None]), 0.0)
    xi = jnp.exp((idx + 1.0)[:, None] * lg[None, :])
    zeta = jnp.exp((CHUNK - 1.0 - idx)[:, None] * lg[None, :])
    gl = jnp.exp(CHUNK * lg)
    return gl, decay, jnp.repeat(xi, 128, axis=1), jnp.repeat(zeta, HEAD_DIM, axis=1), jnp.exp(1.0 * lg)


SAMPLE_BB = 8


def _mix_sample_kernel(sink_ref, gam_ref,
                       qt_ref, kt_ref, v3_ref, sq3_ref, knew_ref, vnew_ref,
                       st_ref, ck_ref, cv_ref, g3_ref, x_ref, gate_ref,
                       wout_ref, brow_ref, gain_ref, lng_ref, lnb_ref,
                       x1_ref, nst_ref, nk_ref, nv_ref, ret_scr, swa_scr):
    i = pl.program_id(0)
    row = lax.broadcasted_iota(I32, (N_SWA_HEADS, 2 * HEAD_DIM), 0)
    lane = lax.broadcasted_iota(I32, (N_SWA_HEADS, 2 * HEAD_DIM), 1)
    own_half = (row // GQA_GROUP) == (lane // HEAD_DIM)
    sink_col = jnp.concatenate(
        [jnp.full((1, 1), sink_ref[hh], F32) for hh in range(N_SWA_HEADS)], axis=0)
    brow = brow_ref[...]
    qt = qt_ref[0]
    kt = kt_ref[0]
    gam = gam_ref[...]
    seqs = range(SAMPLE_BB)
    state_shape = (N_RET_HEADS, HEAD_DIM, HEAD_DIM)

    s_old = [st_ref[b] for b in seqs]
    v_full = [jnp.concatenate([jnp.broadcast_to(v3_ref[b, h:h + 1, :], (HEAD_DIM, HEAD_DIM))
                               for h in range(N_RET_HEADS)], axis=0) for b in seqs]
    s_new = [gam * s_old[b] + kt[:, b:b + 1] * v_full[b] for b in seqs]
    for b in seqs:
        nst_ref[b] = s_new[b]
    outs = [jnp.sum((qt[:, b:b + 1] * s_new[b]).reshape(state_shape), axis=1) for b in seqs]
    for b in seqs:
        ret_scr[i * SAMPLE_BB + b] = outs[b]

    kk = [jnp.concatenate([ck_ref[0, b, 1:, :], knew_ref[b:b + 1, :]], axis=0) for b in seqs]
    vv = [jnp.concatenate([cv_ref[0, b, 1:, :], vnew_ref[b:b + 1, :]], axis=0) for b in seqs]
    for b in seqs:
        nk_ref[b] = kk[b]
        nv_ref[b] = vv[b]
    qblk = [jnp.where(own_half, jnp.concatenate([sq3_ref[b], sq3_ref[b]], axis=-1), 0.0).astype(BF16)
            for b in seqs]
    s = [lax.dot_general(qblk[b], kk[b].astype(BF16), (((1,), (1,)), ((), ())),
                         preferred_element_type=F32) + brow for b in seqs]
    m = [jnp.maximum(jnp.max(s[b], axis=-1, keepdims=True), sink_col) for b in seqs]
    p = [jnp.exp(s[b] - m[b]) for b in seqs]
    den = [jnp.sum(p[b], axis=-1, keepdims=True) + jnp.exp(sink_col - m[b]) for b in seqs]
    pn = [(p[b] / den[b]).astype(BF16) for b in seqs]
    o = [jnp.dot(pn[b], vv[b].astype(BF16), preferred_element_type=F32) for b in seqs]
    for b in seqs:
        swa_scr[i * SAMPLE_BB + b] = jnp.where(own_half[:, :HEAD_DIM], o[b][:, :HEAD_DIM], o[b][:, HEAD_DIM:])

    @pl.when(i == pl.num_programs(0) - 1)
    def _():
        y = jnp.zeros(x_ref.shape, F32)
        for h in range(N_RET_HEADS):
            o = ret_scr[:, h, :]
            mu = jnp.mean(o, axis=-1, keepdims=True)
            c = o - mu
            var = jnp.mean(c * c, axis=-1, keepdims=True)
            r = c * lax.rsqrt(var + LN_EPS) * gain_ref[h:h + 1, :] * _silu(g3_ref[h])
            y = y + jnp.dot(r.astype(BF16), wout_ref[h * HEAD_DIM:(h + 1) * HEAD_DIM, :],
                            preferred_element_type=F32)
        for hh in range(N_SWA_HEADS):
            o = swa_scr[:, hh, :].astype(BF16)
            lo = RET_W + hh * HEAD_DIM
            y = y + jnp.dot(o, wout_ref[lo:lo + HEAD_DIM, :], preferred_element_type=F32)
        x1_ref[...] = _ln(DN_ALPHA * x_ref[...] + (1.0 + gate_ref[0]) * y, lng_ref[...], lnb_ref[...])


def _mix_sample(proj, x, mod, layer, w_out_bf, gamma1, bias_row, sinks, gain8, ln_g, ln_b,
                st, ck, cv):
    q, k, _, v, g, sq, skv = proj
    nb = x.shape[0]
    steps = nb // SAMPLE_BB

    def joined(a):
        a = a.astype(F32).reshape(nb, RET_W // 128, 2, 128)
        return (a[:, :, 0, :] + a[:, :, 1, :]).reshape(nb, RET_W)

    to_cols = lambda a: joined(a).reshape(steps, SAMPLE_BB, RET_W).transpose(0, 2, 1)
    qt, kt = to_cols(q), to_cols(k)
    v3 = joined(v).reshape(nb, N_RET_HEADS, HEAD_DIM)
    sq3 = joined(sq).reshape(nb, GQA_GROUP, N_KV_HEADS, HEAD_DIM).transpose(0, 2, 1, 3)
    sq3 = sq3.reshape(nb, N_SWA_HEADS, HEAD_DIM)
    g3 = g.reshape(nb, N_RET_HEADS, HEAD_DIM).transpose(1, 0, 2)
    knew, vnew = skv[:, :SWA_KVW], skv[:, SWA_KVW:]
    gam =jnp.broadcast_to(jnp.repeat(gamma1, HEAD_DIM)[:, None], (RET_W, HEAD_DIM))
    smem = pl.BlockSpec(memory_space=pltpu.SMEM)
    blk3 = lambda a, b, c: pl.BlockSpec((a, b, c), lambda i: (i, 0, 0))
    lay4 = lambda b, c: pl.BlockSpec((1, SAMPLE_BB, b, c), lambda i: (layer, i, 0, 0))
    const2 = lambda a: pl.BlockSpec(a.shape, lambda i: (0, 0))
    const3 = lambda a: pl.BlockSpec(a.shape, lambda i: (0, 0, 0))
    x1, nst, nk, nv = pl.pallas_call(
        _mix_sample_kernel,
        grid=(steps,),
        in_specs=[smem, const2(gam),
                  blk3(1, RET_W, SAMPLE_BB), blk3(1, RET_W, SAMPLE_BB),
                  blk3(SAMPLE_BB, N_RET_HEADS, HEAD_DIM), blk3(SAMPLE_BB, N_SWA_HEADS, HEAD_DIM),
                  pl.BlockSpec((SAMPLE_BB, SWA_KVW), lambda i: (i, 0)),
                  pl.BlockSpec((SAMPLE_BB, SWA_KVW), lambda i: (i, 0)),
                  blk3(SAMPLE_BB, RET_W, HEAD_DIM), lay4(CHUNK, SWA_KVW), lay4(CHUNK, SWA_KVW),
                  const3(g3), const2(x),
                  pl.BlockSpec((1, nb, D_MODEL), lambda i: (layer, 0, 2)),
                  const2(w_out_bf), const2(bias_row), const2(gain8), const2(ln_g), const2(ln_b)],
        out_specs=[const2(x), blk3(SAMPLE_BB, RET_W, HEAD_DIM),
                   blk3(SAMPLE_BB, CHUNK, SWA_KVW), blk3(SAMPLE_BB, CHUNK, SWA_KVW)],
        out_shape=[jax.ShapeDtypeStruct(x.shape, F32), jax.ShapeDtypeStruct(st.shape, F32),
                   jax.ShapeDtypeStruct(ck.shape[1:], F32), jax.ShapeDtypeStruct(cv.shape[1:], F32)],
        scratch_shapes=[pltpu.VMEM((nb, N_RET_HEADS, HEAD_DIM), F32),
                        pltpu.VMEM((nb, N_SWA_HEADS, HEAD_DIM), F32)],
        compiler_params=_cparams(("arbitrary",)),
        name="mix_sample",
    )(sinks, gam, qt, kt, v3, sq3, knew, vnew, st, ck, cv, g3, x, mod,
      w_out_bf, bias_row, gain8, ln_g, ln_b)
    cache_shape = (nb, CHUNK, N_KV_HEADS, HEAD_DIM)
    return (x1, nst.reshape(nb, N_RET_HEADS, HEAD_DIM, HEAD_DIM), nk.reshape(cache_shape),
            nv.reshape(cache_shape))


def _router_kernel(x_ref, sh_ref, sc_ref, rwt_ref, rb_ref, tri_ref,
                   oi_ref, of_ref, cnt_ref, run_ref, *, mod_rows):
    h2 = x_ref[...] * (1.0 + sc_ref[0][:mod_rows]) + sh_ref[0][:mod_rows]
    _route_rows(h2.astype(BF16), rwt_ref, rb_ref, tri_ref, oi_ref, of_ref, cnt_ref, run_ref)


def _post_mix_kernel(mixed_ref, x_ref, gate_ref, wout_ref, lng_ref, lnb_ref,
                     sh_ref, sc_ref, rwt_ref, rb_ref, tri_ref,
                     x1_ref, oi_ref, of_ref, cnt_ref, run_ref):
    y = jnp.dot(mixed_ref[...], wout_ref[...], preferred_element_type=F32)
    x1 = _ln(DN_ALPHA * x_ref[...] + (1.0 + gate_ref[0][:1]) * y, lng_ref[...], lnb_ref[...])
    x1_ref[...] = x1
    h2 = x1 * (1.0 + sc_ref[0][:1]) + sh_ref[0][:1]
    _route_rows(h2.astype(BF16), rwt_ref, rb_ref, tri_ref, oi_ref, of_ref, cnt_ref, run_ref)


def _route_rows(h2b, rwt_ref, rb_ref, tri_ref, oi_ref, of_ref, cnt_ref, run_ref):
    i = pl.program_id(0)

    @pl.when(i == 0)
    def _():
        run_ref[...] = jnp.zeros_like(run_ref)

    logits = lax.dot_general(rwt_ref[...].astype(BF16), h2b, (((1,), (1,)), ((), ())),
                             preferred_element_type=F32)
    aff = 1.0 / (1.0 + jnp.exp(-logits))
    sel = aff + rb_ref[...]
    s = [sel[e:e + 1, :] for e in range(N_EXPERTS)]
    a = [aff[e:e + 1, :] for e in range(N_EXPERTS)]

    def top2sum(v0, v1, v2, v3):
        hi01, lo01 = jnp.maximum(v0, v1), jnp.minimum(v0, v1)
        hi23, lo23 = jnp.maximum(v2, v3), jnp.minimum(v2, v3)
        return jnp.maximum(hi01, hi23) + jnp.maximum(jnp.minimum(hi01, hi23),
                                                     jnp.maximum(lo01, lo23))

    def argmax_first(vals):
        best, idx = vals[0], jnp.zeros(vals[0].shape, I32)
        for j in range(1, len(vals)):
            upd = vals[j] > best
            idx = jnp.where(upd, j, idx)
            best = jnp.where(upd, vals[j], best)
        return idx

    def pick(idx, vals):
        out = vals[-1]
        for j in range(len(vals) - 2, -1, -1):
            out = jnp.where(idx == j, vals[j], out)
        return out

    gi = argmax_first([top2sum(*s[4 * g:4 * g + 4]) for g in range(N_GROUPS)])
    sv = [pick(gi, [s[4 * g + j] for g in range(N_GROUPS)]) for j in range(EXPERTS_PER_GROUP)]
    av = [pick(gi, [a[4 * g + j] for g in range(N_GROUPS)]) for j in range(EXPERTS_PER_GROUP)]
    i1 = argmax_first(sv)
    i2 = argmax_first([jnp.where(i1 == j, -jnp.inf, sv[j]) for j in range(EXPERTS_PER_GROUP)])
    w1, w2 = pick(i1, av), pick(i2, av)
    wsum = w1 + w2
    w1, w2 = w1 / wsum, w2 / wsum
    lo, hi = jnp.minimum(i1, i2), jnp.maximum(i1, i2)
    w_lo = jnp.where(i1 < i2, w1, w2)
    w_hi = jnp.where(i1 < i2, w2, w1)
    pair = jnp.where(lo == 0, hi - 1, jnp.where(lo == 1, hi + 1, 5))
    bin_id = gi * N_PAIRS + pair

    tm = bin_id.shape[1]
    onehot = lax.broadcasted_iota(I32, (BIN_ROWS, tm), 0) == bin_id
    oh_f = jnp.where(onehot, 1.0, 0.0)
    before = jnp.dot(oh_f.astype(BF16), tri_ref[...], preferred_element_type=F32)
    run = run_ref[...]
    run_t = jnp.concatenate([run] * (tm // 128), axis=-1)
    rank = jnp.sum(oh_f * (before + run_t), axis=0, keepdims=True)
    run_new = run + jnp.sum(oh_f, axis=1, keepdims=True)
    run_ref[...] = run_new
    cnt_ref[...] = run_new.astype(I32)

    zi = jnp.zeros_like(bin_id)
    oi_ref[0] = jnp.concatenate([gi * 4 + lo, gi * 4 + hi, bin_id, rank.astype(I32), zi, zi, zi, zi], axis=0)
    zf = jnp.zeros_like(w_lo)
    of_ref[0] = jnp.concatenate([w_lo, w_hi, zf, zf, zf, zf, zf, zf], axis=0)


def _router(x1, mod, layer, mod_row0, mod_rows, router_wt, router_b, tm):
    rows = x1.shape[0]
    nt = rows // tm
    mblk = 8 if mod_rows == 1 else mod_rows
    mrow = mod_row0 // mblk
    tri = jnp.asarray(np.triu(np.ones((tm, tm), np.float32), 1), BF16)
    rb = jnp.broadcast_to(router_b.astype(F32)[:, None], (N_EXPERTS, tm))
    mod_spec = lambda c: pl.BlockSpec((1, mblk, D_MODEL), lambda i: (layer, mrow, c))
    oi, of, cnt = pl.pallas_call(
        functools.partial(_router_kernel, mod_rows=mod_rows),
        grid=(nt,),
        in_specs=[pl.BlockSpec((tm, D_MODEL), lambda i: (i, 0)), mod_spec(3), mod_spec(4),
                  pl.BlockSpec((N_EXPERTS, D_MODEL), lambda i: (0, 0)),
                  pl.BlockSpec((N_EXPERTS, tm), lambda i: (0, 0)),
                  pl.BlockSpec((tm, tm), lambda i: (0, 0))],
        out_specs=[pl.BlockSpec((1, 8, tm), lambda i: (i, 0, 0)),
                   pl.BlockSpec((1, 8, tm), lambda i: (i, 0, 0)),
                   pl.BlockSpec((BIN_ROWS, 128), lambda i: (0, 0))],
        out_shape=[jax.ShapeDtypeStruct((nt, 8, tm), I32),
                   jax.ShapeDtypeStruct((nt, 8, tm), F32),
                   jax.ShapeDtypeStruct((BIN_ROWS, 128), I32)],
        scratch_shapes=[pltpu.VMEM((BIN_ROWS, 128), F32)],
        compiler_params=_cparams(("arbitrary",)),
        name="router",
    )(x1, mod, mod, router_wt, rb, tri)
    return _unpack_route(oi, of, cnt)


def _unpack_route(oi, of, cnt):
    flat = lambda a, r: a[:, r, :].reshape(-1)
    return (flat(oi, 0), flat(oi, 1), flat(oi, 2), flat(oi, 3), flat(of, 0), flat(of, 1),
            cnt[:N_BINS, 0])


POST_TM = 512


def _post_mix(mixed, x, mod, layer, mod_row0, w_out_bf, ln_g, ln_b, router_wt, router_b):
    rows = x.shape[0]
    tm = POST_TM
    nt = rows // tm
    mrow = mod_row0 // 8
    tri = jnp.asarray(np.triu(np.ones((tm, tm), np.float32), 1), BF16)
    rb = jnp.broadcast_to(router_b.astype(F32)[:, None], (N_EXPERTS, tm))
    mod_spec = lambda c: pl.BlockSpec((1, 8, D_MODEL), lambda i: (layer, mrow, c))
    row_spec = pl.BlockSpec((tm, D_MODEL), lambda i: (i, 0))
    route_spec = pl.BlockSpec((1, 8, tm), lambda i: (i, 0, 0))
    const2 = lambda a: pl.BlockSpec(a.shape, lambda i: (0, 0))
    x1, oi, of, cnt = pl.pallas_call(
        _post_mix_kernel,
        grid=(nt,),
        in_specs=[row_spec, row_spec, mod_spec(2), const2(w_out_bf), const2(ln_g), const2(ln_b),
                  mod_spec(3), mod_spec(4), const2(router_wt), const2(rb), const2(tri)],
        out_specs=[row_spec, route_spec, route_spec,
                   pl.BlockSpec((BIN_ROWS, 128), lambda i: (0, 0))],
        out_shape=[jax.ShapeDtypeStruct((rows, D_MODEL), F32),
                   jax.ShapeDtypeStruct((nt, 8, tm), I32),
                   jax.ShapeDtypeStruct((nt, 8, tm), F32),
                   jax.ShapeDtypeStruct((BIN_ROWS, 128), I32)],
        scratch_shapes=[pltpu.VMEM((BIN_ROWS, 128), F32)],
        compiler_params=_cparams(("arbitrary",)),
        name="post_mix",
    )(mixed, x, mod, w_out_bf, ln_g, ln_b, mod, mod, router_wt, rb, tri)
    return x1, _unpack_route(oi, of, cnt)


DMA_UNROLL = 8
ROW_GROUPS = MOE_TM // DMA_UNROLL


GROUPS_PER_ITER = 2


def _for_rows(n, fn):
    per_iter = GROUPS_PER_ITER * DMA_UNROLL

    def body_u(jj, carry):
        for g in range(GROUPS_PER_ITER):
            for u in range(DMA_UNROLL):
                fn(jj * GROUPS_PER_ITER + g, u)
        return carry
    lax.fori_loop(0, n // per_iter, body_u, 0)

    def body_1(r, carry):
        fn(r // DMA_UNROLL, r % DMA_UNROLL)
        return carry
    lax.fori_loop((n // per_iter) * per_iter, n, body_1, 0)


def _wait_rows(n, buf, sem):
    p = ROW_GROUPS
    while p >= 1:
        @pl.when((n & (p * DMA_UNROLL)) != 0)
        def _(p=p):
            pltpu.make_async_copy(buf.at[pl.ds(0, p)], buf.at[pl.ds(0, p)], sem).wait()
        p //= 2
    p = DMA_UNROLL // 2
    while p >= 1:
        @pl.when((n & p) != 0)
        def _(p=p):
            pltpu.make_async_copy(buf.at[0, pl.ds(0, p)], buf.at[0, pl.ds(0, p)], sem).wait()
        p //= 2


def _moe_sorted_kernel(ea_ref, eb_ref, nvalid_ref, pos_ref,
                       x_hbm, sh_ref, sc_ref, gate_ref, rw_ref,
                       wga_ref, wua_ref, wda_ref, wgb_ref, wub_ref, wdb_ref,
                       lng_ref, lnb_ref,
                       out_hbm,
                       inv, xbuf, obuf, wg_a, wu_a, wd_a, wg_b, wu_b, wd_b, gsem, ssem):
    i = pl.program_id(0)
    nt = pl.num_programs(0)
    slot = i % 2
    n_tok = pos_ref.shape[0]

    def start_gather(t, dst_slot):
        def one(j, u):
            tok = inv[t * MOE_TM + j * DMA_UNROLL + u]
            pltpu.make_async_copy(x_hbm.at[pl.ds(tok, 1)], xbuf.at[dst_slot, j, pl.ds(u, 1)],
                                  gsem.at[dst_slot]).start()
        _for_rows(nvalid_ref[t], one)

    @pl.when(i == 0)
    def _():
        def body(j, carry):
            for u in range(DMA_UNROLL):
                t = j * DMA_UNROLL + u
                inv[pos_ref[t]] = t
            return carry
        lax.fori_loop(0, n_tok // DMA_UNROLL, body, 0)
        xbuf[...] = jnp.zeros_like(xbuf)
        start_gather(0, 0)

    @pl.when(i >= 2)
    def _():
        _wait_rows(nvalid_ref[jnp.maximum(i - 2, 0)], obuf.at[slot], ssem.at[slot])

    @pl.when(nvalid_ref[i] > 0)
    def _():
        _wait_rows(nvalid_ref[i], xbuf.at[slot], gsem.at[slot])

        @pl.when(i + 1 < nt)
        def _():
            start_gather(jnp.minimum(i + 1, nt - 1), 1 - slot)

        @pl.when(jnp.logical_or(i == 0, ea_ref[i] != ea_ref[jnp.maximum(i - 1, 0)]))
        def _():
            wg_a[...] = wga_ref[0, 0].astype(BF16)
            wu_a[...] = wua_ref[0, 0].astype(BF16)
            wd_a[...] = wda_ref[0, 0].astype(BF16)

        @pl.when(jnp.logical_or(i == 0, eb_ref[i] != eb_ref[jnp.maximum(i - 1, 0)]))
        def _():
            wg_b[...] = wgb_ref[0, 0].astype(BF16)
            wu_b[...] = wub_ref[0, 0].astype(BF16)
            wd_b[...] = wdb_ref[0, 0].astype(BF16)

        x = xbuf[slot].reshape(MOE_TM, D_MODEL)
        h2 = (x * (1.0 + sc_ref[0][:1]) + sh_ref[0][:1]).astype(BF16)

        dot = functools.partial(jnp.dot, preferred_element_type=F32)
        logits = dot(h2, rw_ref[...])
        gate_a, gate_b = dot(h2, wg_a[...]), dot(h2, wg_b[...])
        up_a, up_b = dot(h2, wu_a[...]), dot(h2, wu_b[...])
        act_a = (_silu(gate_a) * up_a).astype(BF16)
        act_b = (_silu(gate_b) * up_b).astype(BF16)
        y_a, y_b = dot(act_a, wd_a[...]), dot(act_b, wd_b[...])

        aff = 1.0 / (1.0 + jnp.exp(-logits))
        lane = lax.broadcasted_iota(I32, aff.shape, 1)
        a_lo = jnp.sum(jnp.where(lane == ea_ref[i], aff, 0.0), axis=-1, keepdims=True)
        a_hi = jnp.sum(jnp.where(lane == eb_ref[i], aff, 0.0), axis=-1, keepdims=True)
        a_sum = a_lo + a_hi
        y = (a_lo / a_sum) * y_a + (a_hi / a_sum) * y_b
        out = _ln(DN_ALPHA * x + (1.0 + gate_ref[0][:1]) * y, lng_ref[...], lnb_ref[...])
        obuf[slot] = out.reshape(ROW_GROUPS, DMA_UNROLL, D_MODEL)

        def scatter_one(j, u):
            tok = inv[i * MOE_TM + j * DMA_UNROLL + u]
            pltpu.make_async_copy(obuf.at[slot, j, pl.ds(u, 1)], out_hbm.at[pl.ds(tok, 1)],
                                  ssem.at[slot]).start()
        _for_rows(nvalid_ref[i], scatter_one)

    @pl.when(i == nt - 1)
    def _():
        _wait_rows(nvalid_ref[jnp.maximum(i - 1, 0)], obuf.at[1 - slot], ssem.at[1 - slot])
        _wait_rows(nvalid_ref[i], obuf.at[slot], ssem.at[slot])


def _moe_sorted(x1, mod, layer, mod_row0, route, rw_pad, wg, wu, wd, ln_g, ln_b):
    _, _, bin_id, rank, _, _, counts = route
    rows = x1.shape[0]
    nt = rows // MOE_TM + N_BINS
    ns = nt * MOE_TM
    tiles = (counts + MOE_TM - 1) // MOE_TM
    tile_end = jnp.cumsum(tiles)
    row_start = ((tile_end - tiles) * MOE_TM).astype(I32)
    pos = jnp.sum(jnp.where(bin_id[:, None] == jnp.arange(N_BINS, dtype=I32)[None, :],
                            row_start[None, :], 0), axis=1) + rank
    total = tile_end[-1]
    t = jnp.arange(nt, dtype=I32)
    tile_bin = jnp.sum(jnp.minimum(t, total - 1)[:, None] >= tile_end[None, :], axis=1).astype(I32)
    tile_bin = jnp.clip(tile_bin, 0, N_BINS - 1)
    bin_lo = jnp.asarray([g * 4 + _PAIR_LO[p] for g in range(N_GROUPS) for p in range(N_PAIRS)], I32)
    bin_hi = jnp.asarray([g * 4 + _PAIR_HI[p] for g in range(N_GROUPS) for p in range(N_PAIRS)], I32)
    tile_ea, tile_eb = bin_lo[tile_bin], bin_hi[tile_bin]
    nvalid = jnp.clip(counts[tile_bin] - (t - (tile_end - tiles)[tile_bin]) * MOE_TM, 0, MOE_TM)
    nvalid = jnp.where(t < total, nvalid, 0).astype(I32)
    mrow = mod_row0 // 8
    mod_spec = lambda c: pl.BlockSpec((1, 8, D_MODEL), lambda i, *_: (layer, mrow, c))
    w_spec = lambda shape, ref_idx: pl.BlockSpec(
        (1, 1) + shape, lambda i, ea_r, eb_r, *_: (layer, (ea_r, eb_r)[ref_idx][i], 0, 0))
    any_spec = pl.BlockSpec(memory_space=pl.ANY)
    w_bf = lambda shape: pltpu.VMEM(shape, BF16)
    grid_spec = pltpu.PrefetchScalarGridSpec(
        num_scalar_prefetch=4,
        grid=(nt,),
        in_specs=[any_spec,
                  mod_spec(3), mod_spec(4), mod_spec(5),
                  pl.BlockSpec((D_MODEL, 128), lambda i, *_: (0, 0)),
                  w_spec((D_MODEL, D_FF), 0), w_spec((D_MODEL, D_FF), 0), w_spec((D_FF, D_MODEL), 0),
                  w_spec((D_MODEL, D_FF), 1), w_spec((D_MODEL, D_FF), 1), w_spec((D_FF, D_MODEL), 1),
                  pl.BlockSpec((1, D_MODEL), lambda i, *_: (0, 0)),
                  pl.BlockSpec((1, D_MODEL), lambda i, *_: (0, 0))],
        out_specs=any_spec,
        scratch_shapes=[pltpu.SMEM((ns,), I32),
                        pltpu.VMEM((2, ROW_GROUPS, DMA_UNROLL, D_MODEL), F32),
                        pltpu.VMEM((2, ROW_GROUPS, DMA_UNROLL, D_MODEL), F32),
                        w_bf((D_MODEL, D_FF)), w_bf((D_MODEL, D_FF)), w_bf((D_FF, D_MODEL)),
                        w_bf((D_MODEL, D_FF)), w_bf((D_MODEL, D_FF)), w_bf((D_FF, D_MODEL)),
                        pltpu.SemaphoreType.DMA((2,)),
                        pltpu.SemaphoreType.DMA((2,))],
    )
    return pl.pallas_call(
        _moe_sorted_kernel,
        grid_spec=grid_spec,
        out_shape=jax.ShapeDtypeStruct((rows, D_MODEL), F32),
        compiler_params=_cparams(("arbitrary",)),
        name="moe_sorted",
    )(tile_ea, tile_eb, nvalid, pos, x1, mod, mod, mod, rw_pad,
      wg, wu, wd, wg, wu, wd, ln_g, ln_b)


def _moe_dense_kernel(x_ref, dw_ref, sh_ref, sc_ref, gate_ref, wg_ref, wu_ref, wd_ref,
                      lng_ref, lnb_ref, o_ref, acc):
    e = pl.program_id(0)

    @pl.when(e == 0)
    def _():
        acc[...] = jnp.zeros_like(acc)

    x = x_ref[...]
    h2 = (x * (1.0 + sc_ref[0]) + sh_ref[0]).astype(BF16)
    a = _silu(jnp.dot(h2, wg_ref[0, 0].astype(BF16), preferred_element_type=F32)) * \
        jnp.dot(h2, wu_ref[0, 0].astype(BF16), preferred_element_type=F32)
    y = jnp.dot(a.astype(BF16), wd_ref[0, 0].astype(BF16), preferred_element_type=F32)
    acc[...] = acc[...] + dw_ref[0][:, :1] * y

    @pl.when(e == pl.num_programs(0) - 1)
    def _():
        o_ref[...] = _ln(DN_ALPHA * x + (1.0 + gate_ref[0]) * acc[...], lng_ref[...], lnb_ref[...])


def _moe_dense(x1, mod, layer, route, wg, wu, wd, ln_g, ln_b):
    ea, eb, _, _, w_lo, w_hi, _ = route
    nb = x1.shape[0]
    eids = jnp.arange(N_EXPERTS, dtype=I32)[:, None]
    dw = jnp.where(eids == ea[None, :], w_lo[None, :], 0.0) + \
        jnp.where(eids == eb[None, :], w_hi[None, :], 0.0)
    dw = jnp.broadcast_to(dw[:, :, None], (N_EXPERTS, nb, 128))
    mod_spec = lambda c: pl.BlockSpec((1, nb, D_MODEL), lambda e: (layer, 0, c))
    const2 = lambda a: pl.BlockSpec(a.shape, lambda e: (0, 0))
    return pl.pallas_call(
        _moe_dense_kernel,
        grid=(N_EXPERTS,),
        in_specs=[const2(x1), pl.BlockSpec((1, nb, 128), lambda e: (e, 0, 0)),
                  mod_spec(3), mod_spec(4), mod_spec(5),
                  pl.BlockSpec((1, 1, D_MODEL, D_FF), lambda e: (layer, e, 0, 0)),
                  pl.BlockSpec((1, 1, D_MODEL, D_FF), lambda e: (layer, e, 0, 0)),
                  pl.BlockSpec((1, 1, D_FF, D_MODEL), lambda e: (layer, e, 0, 0)),
                  const2(ln_g), const2(ln_b)],
        out_specs=const2(x1),
        out_shape=jax.ShapeDtypeStruct(x1.shape, F32),
        scratch_shapes=[pltpu.VMEM(x1.shape, F32)],
        compiler_params=_cparams(("arbitrary",)),
        name="moe_dense",
    )(x1, dw, mod, mod, mod, wg, wu, wd, ln_g, ln_b)


def kernel(x_prompt, x_sample, state_ret, cache_swa_k, cache_swa_v, c_prompt, c_sample, w_in, w_out, ret_gn_gain, swa_sinks, rel_bias_table, ada_w, ada_b, ln1_g, ln1_b, ln2_g, ln2_b, router_w, router_b, exp_w_gate, exp_w_up, exp_w_down):
    seq = x_prompt.shape[1]
    nb = x_sample.shape[0]
    past_len = 16384
    assert x_prompt.shape[0] == 1 and x_sample.shape[1] == 1

    c_all = jnp.concatenate([c_sample, c_prompt, jnp.zeros((7, D_MODEL), F32)], axis=0)
    mod = _ada(c_all, ada_w, ada_b)
    prompt_row = nb

    bias_tabs = _bias_tables(rel_bias_table.astype(F32))
    gl, decay, xi, zeta, gamma1 = _ret_consts()
    rope_p = _rope_tables(jnp.arange(seq, dtype=I32))
    rope_s = _rope_tables(jnp.full((nb,), past_len, I32))
    router_wt = router_w.astype(F32).T
    rw_pad = jnp.pad(router_w.astype(BF16), ((0, 0), (0, 128 - N_EXPERTS)))
    zeta_p = jnp.tile(zeta, (INPROJ_TM // CHUNK, 1))
    zeta_s = jnp.ones((nb, RET_W), F32)

    ck_all = cache_swa_k.astype(F32).reshape(DEPTH, nb, CHUNK, SWA_KVW)
    cv_all = cache_swa_v.astype(F32).reshape(DEPTH, nb, CHUNK, SWA_KVW)

    xp = x_prompt.reshape(seq, D_MODEL)
    xs = x_sample.reshape(nb, D_MODEL)
    st_p, k_p, v_p, st_s, k_s, v_s = [], [], [], [], [], []
    for l in range(DEPTH):
        w_in_bf = w_in[l].astype(BF16)
        w_out_bf = w_out[l].astype(BF16)
        order = jnp.asarray(_SWA_HEAD_ORDER)
        swa_rows = w_out[l][RET_W:].reshape(N_SWA_HEADS, HEAD_DIM, D_MODEL)[order]
        w_out_prompt = jnp.concatenate([w_out[l][:RET_W], swa_rows.reshape(SWA_QW, D_MODEL)],
                                       axis=0).astype(BF16)
        experts = (exp_w_gate.astype(F32), exp_w_up.astype(F32), exp_w_down.astype(F32))
        gain = ret_gn_gain[l].astype(F32)
        sinks = swa_sinks[l].astype(F32)
        row = lambda a: a[l].astype(F32).reshape(1, D_MODEL)

        proj = _inproj(xp, mod, l, prompt_row, 1, w_in_bf, rope_p, zeta_p, INPROJ_TM)
        mixed, st = _mix_prompt(proj, (gl, decay, xi), bias_tabs, sinks, gain.reshape(1, RET_W))
        skv = proj[6]
        st_p.append(st.reshape(1, N_RET_HEADS, HEAD_DIM, HEAD_DIM))
        k_p.append(skv[seq - CHUNK:, :SWA_KVW].reshape(1, CHUNK, N_KV_HEADS, HEAD_DIM))
        v_p.append(skv[seq - CHUNK:, SWA_KVW:].reshape(1, CHUNK, N_KV_HEADS, HEAD_DIM))
        x1, route = _post_mix(mixed, xp, mod, l, prompt_row, w_out_prompt, row(ln1_g), row(ln1_b),
                              router_wt, router_b)
        xp = _moe_sorted(x1, mod, l, prompt_row, route, rw_pad, *experts, row(ln2_g), row(ln2_b))

        proj = _inproj(xs, mod, l, 0, nb, w_in_bf, rope_s, zeta_s, nb)
        x1, nst, nk, nv = _mix_sample(proj, xs, mod, l, w_out_bf, gamma1, bias_tabs[1], sinks, gain,
                                      row(ln1_g), row(ln1_b),
                                      state_ret[l].astype(F32).reshape(nb, RET_W, HEAD_DIM), ck_all, cv_all)
        st_s.append(nst)
        k_s.append(nk)
        v_s.append(nv)
        route = _router(x1, mod, l, 0, nb, router_wt, router_b, nb)
        xs = _moe_dense(x1, mod, l, route, *experts, row(ln2_g), row(ln2_b))

    return (xp.reshape(1, seq, D_MODEL), xs.reshape(nb, 1, D_MODEL),
            jnp.stack(st_p), jnp.stack(k_p), jnp.stack(v_p),
            jnp.stack(st_s), jnp.stack(k_s), jnp.stack(v_s))
```

```python
import functools
import math

import numpy as np
import jax
import jax.numpy as jnp
from jax import lax
from jax.experimental import pallas as pl
from jax.experimental.pallas import tpu as pltpu

F32 = jnp.float32
BF16 = jnp.bfloat16
I32 = jnp.int32

D_MODEL = 1024
DEPTH = 2
HEAD_DIM = 64
N_RET_HEADS = 8
N_SWA_HEADS = 8
N_KV_HEADS = 2
GQA_GROUP = N_SWA_HEADS // N_KV_HEADS
RET_W = N_RET_HEADS * HEAD_DIM
SWA_QW = N_SWA_HEADS * HEAD_DIM
SWA_KVW = N_KV_HEADS * HEAD_DIM
PROJ_W = 4 * RET_W + SWA_QW + 2 * SWA_KVW
CHUNK = 128
ROPE_BASE = 10000.0
N_BUCKETS = 32
MAX_DISTANCE = 128
N_EXPERTS = 16
N_GROUPS = 4
EXPERTS_PER_GROUP = 4
D_FF = 512
LN_EPS = 1e-5
DN_ALPHA = (2 * DEPTH) ** 0.25
QK_SCALE = HEAD_DIM ** -0.5
NEG_BIG = -1e30

N_PAIRS = 6
N_BINS = N_GROUPS * N_PAIRS
BIN_ROWS = 32
MOE_TM = 256
INPROJ_TM = 512
MIX_SUB = 4
SPLIT_W = 128 * N_RET_HEADS
_SWA_HEAD_ORDER = (0, 4, 1, 5, 2, 6, 3, 7)
VMEM_LIMIT = 56 * 1024 * 1024

_PAIR_LO = (0, 0, 0, 1, 1, 3)
_PAIR_HI = (1, 2, 3, 3, 2, 2)


def _cparams(sem):
    return pltpu.CompilerParams(dimension_semantics=sem, vmem_limit_bytes=VMEM_LIMIT)


def _ln(v, g, b):
    mu = jnp.mean(v, axis=-1, keepdims=True)
    c = v - mu
    var = jnp.mean(c * c, axis=-1, keepdims=True)
    return c * lax.rsqrt(var + LN_EPS) * g + b


def _silu(v):
    return v * (1.0 / (1.0 + jnp.exp(-v)))


def _ada_kernel(c_ref, w_ref, b_ref, o_ref):
    o_ref[0] = jnp.dot(c_ref[...].astype(BF16), w_ref[0].astype(BF16),
                       preferred_element_type=F32) + b_ref[0]


def _ada(c_all, ada_w, ada_b):
    rows = c_all.shape[0]
    nt = 6 * D_MODEL // 1024
    return pl.pallas_call(
        _ada_kernel,
        grid=(DEPTH, nt),
        in_specs=[pl.BlockSpec((rows, D_MODEL), lambda l, j: (0, 0)),
                  pl.BlockSpec((1, D_MODEL, 1024), lambda l, j: (l, 0, j)),
                  pl.BlockSpec((1, 1, 1024), lambda l, j: (l, 0, j))],
        out_specs=pl.BlockSpec((1, rows, 1024), lambda l, j: (l, 0, j)),
        out_shape=jax.ShapeDtypeStruct((DEPTH, rows, 6 * D_MODEL), F32),
        compiler_params=_cparams(("arbitrary", "arbitrary")),
        name="ada",
    )(c_all, ada_w, ada_b.reshape(DEPTH, 1, 6 * D_MODEL))


def _bias_kernel(tab_ref, bkt_ref, fold_ref, row_ref):
    bkt = bkt_ref[...]
    rows = lax.broadcasted_iota(I32, (CHUNK, CHUNK), 0)
    cols = lax.broadcasted_iota(I32, (CHUNK, CHUNK), 1)
    from_prev = cols > rows
    for h in range(N_SWA_HEADS):
        acc = jnp.zeros(bkt.shape, F32)
        for b in range(N_BUCKETS):
            acc = jnp.where(bkt == b, tab_ref[b, h], acc)
        own = acc[:, CHUNK:]
        fold_ref[0, h] = jnp.where(from_prev, NEG_BIG, own)
        fold_ref[1, h] = jnp.where(from_prev, acc[:, :CHUNK], own)
        row_ref[h:h + 1, :] = own[CHUNK - 1:CHUNK, :]


def _t5_bucket(rel):
    max_exact = N_BUCKETS // 2
    relf = jnp.maximum(rel, 1).astype(F32)
    large = max_exact + (jnp.log(relf / max_exact) / math.log(MAX_DISTANCE / max_exact)
                         * (N_BUCKETS - max_exact)).astype(I32)
    large = jnp.minimum(large, N_BUCKETS - 1)
    return jnp.where(rel < max_exact, rel, large)


def _bias_tables(rel_bias_table):
    qi = jnp.arange(CHUNK)
    si = jnp.arange(2 * CHUNK)
    rel = CHUNK + qi[:, None] - si[None, :]
    bkt = _t5_bucket(jnp.maximum(rel, 0)).astype(I32)
    return pl.pallas_call(
        _bias_kernel,
        in_specs=[pl.BlockSpec(memory_space=pltpu.SMEM),
                  pl.BlockSpec((CHUNK, 2 * CHUNK), lambda: (0, 0))],
        out_specs=[pl.BlockSpec((2, N_SWA_HEADS, CHUNK, CHUNK), lambda: (0, 0, 0, 0)),
                   pl.BlockSpec((N_SWA_HEADS, CHUNK), lambda: (0, 0))],
        out_shape=[jax.ShapeDtypeStruct((2, N_SWA_HEADS, CHUNK, CHUNK), F32),
                   jax.ShapeDtypeStruct((N_SWA_HEADS, CHUNK), F32)],
        name="t5_bias",
    )(rel_bias_table, bkt)


def _rotary(v, cos, s_lo, s_hi):
    outs = []
    for j in range(RET_W // 128):
        blk = v[:, j * 128:(j + 1) * 128]
        outs.append(blk * cos + pltpu.roll(blk, 96, 1) * s_lo + pltpu.roll(blk, 32, 1) * s_hi)
    return jnp.concatenate(outs, axis=-1)


def _inproj_kernel(x_ref, sh_ref, sc_ref, w_ref, cs_ref, zeta_ref,
                   q_ref, k_ref, kz_ref, v_ref, g_ref, sq_ref, skv_ref, *, mod_rows):
    sh = sh_ref[0][:mod_rows]
    sc = sc_ref[0][:mod_rows]
    h = (x_ref[...] * (1.0 + sc) + sh).astype(BF16)
    cs = jnp.concatenate([cs_ref[...], cs_ref[...]], axis=-1)
    first = (lax.broadcasted_iota(I32, cs.shape, 1) % HEAD_DIM) < (HEAD_DIM // 2)
    cos = jnp.where(first, cs, pltpu.roll(cs, 32, 1))
    s_lo = jnp.where(first, -pltpu.roll(cs, 96, 1), 0.0)
    s_hi = jnp.where(first, 0.0, cs)

    def proj(lo, hi):
        return jnp.dot(h, w_ref[:, lo:hi], preferred_element_type=F32)

    left = lax.broadcasted_iota(I32, cs.shape, 1) < HEAD_DIM

    def store_split(ref, val):
        for b in range(RET_W // 128):
            blk = val[:, b * 128:(b + 1) * 128]
            ref[:, (2 * b) * 128:(2 * b + 1) * 128] = jnp.where(left, blk, 0.0).astype(BF16)
            ref[:, (2 * b + 1) * 128:(2 * b + 2) * 128] = jnp.where(left, 0.0, blk).astype(BF16)

    store_split(q_ref, _rotary(proj(0, RET_W), cos, s_lo, s_hi))
    k = _rotary(proj(RET_W, 2 * RET_W), cos, s_lo, s_hi) * QK_SCALE
    store_split(k_ref, k)
    store_split(kz_ref, k * zeta_ref[...])
    store_split(v_ref, proj(2 * RET_W, 3 * RET_W))
    g_ref[...] = proj(3 * RET_W, 4 * RET_W)
    sq = proj(4 * RET_W, 4 * RET_W + SWA_QW) * QK_SCALE
    for hh in range(N_SWA_HEADS):
        kv, j = divmod(hh, GQA_GROUP)
        blk = sq[:, (hh // 2) * 128:(hh // 2 + 1) * 128]
        if hh % 2 != kv:
            blk = pltpu.roll(blk, HEAD_DIM, 1)
        keep = left if kv == 0 else jnp.logical_not(left)
        sq_ref[:, (2 * j + kv) * 128:(2 * j + kv + 1) * 128] = jnp.where(keep, blk, 0.0).astype(BF16)
    skv_ref[...] = proj(4 * RET_W + SWA_QW, PROJ_W)


def _inproj(x, mod, layer, mod_row0, mod_rows, w_in_bf, rope, zeta_tile, tm):
    rows = x.shape[0]
    mblk = 8 if mod_rows == 1 else mod_rows
    mrow = mod_row0 // mblk
    row_spec = lambda w: pl.BlockSpec((tm, w), lambda i: (i, 0))
    mod_spec = lambda c: pl.BlockSpec((1, mblk, D_MODEL), lambda i: (layer, mrow, c))
    return pl.pallas_call(
        functools.partial(_inproj_kernel, mod_rows=mod_rows),
        grid=(rows // tm,),
        in_specs=[row_spec(D_MODEL), mod_spec(0), mod_spec(1),
                  pl.BlockSpec((D_MODEL, PROJ_W), lambda i: (0, 0)),
                  row_spec(HEAD_DIM),
                  pl.BlockSpec((tm, RET_W), lambda i: (0, 0))],
        out_specs=[row_spec(SPLIT_W), row_spec(SPLIT_W), row_spec(SPLIT_W), row_spec(SPLIT_W),
                   row_spec(RET_W), row_spec(SPLIT_W), row_spec(2 * SWA_KVW)],
        out_shape=[jax.ShapeDtypeStruct((rows, SPLIT_W), BF16),
                   jax.ShapeDtypeStruct((rows, SPLIT_W), BF16),
                   jax.ShapeDtypeStruct((rows, SPLIT_W), BF16),
                   jax.ShapeDtypeStruct((rows, SPLIT_W), BF16),
                   jax.ShapeDtypeStruct((rows, RET_W), F32),
                   jax.ShapeDtypeStruct((rows, SPLIT_W), BF16),
                   jax.ShapeDtypeStruct((rows, 2 * SWA_KVW), F32)],
        compiler_params=_cparams(("arbitrary",)),
        name="inproj",
    )(x, mod, mod, w_in_bf, rope, zeta_tile)


def _rope_tables(pos):
    half = HEAD_DIM // 2
    inv = ROPE_BASE ** (-jnp.arange(half, dtype=F32) / half)
    rows = pos.shape[0]
    ang = pos.astype(F32).reshape(rows // 4, 4, 1) * inv[None, None, :]
    ang = ang.reshape(rows // 4, 4 * half)
    cos, sin = jnp.cos(ang).reshape(rows, half), jnp.sin(ang).reshape(rows, half)
    return jnp.concatenate([cos, sin], axis=-1)


def _mix_prompt_kernel(gl_ref, sink_ref,
                       q_ref, k_ref, kz_ref, v_ref, g_ref, sq_ref, kvc_ref, kvp_ref,
                       decay_ref, xi_ref, bias_ref, gain_ref,
                       mixed_ref, st_ref, state):
    i = pl.program_id(0)

    @pl.when(i == 0)
    def _():
        state[...] = jnp.zeros_like(state)

    gain = gain_ref[...]
    left = lax.broadcasted_iota(I32, (CHUNK, 128), 1) < HEAD_DIM
    from_prev = (lax.broadcasted_iota(I32, (CHUNK, CHUNK), 1) >
                 lax.broadcasted_iota(I32, (CHUNK, CHUNK), 0))
    inv_dim = 1.0 / HEAD_DIM

    for sub in range(MIX_SUB):
        rows = slice(sub * CHUNK, (sub + 1) * CHUNK)

        heads = range(N_RET_HEADS)
        hsl = [slice(h * 128, (h + 1) * 128) for h in heads]
        scores_r = [lax.dot_general(q_ref[rows, hsl[h]], k_ref[rows, hsl[h]], (((1,), (1,)), ((), ())),
                                    preferred_element_type=F32) for h in heads]
        s_olds = [state[h] for h in heads]
        inter = [jnp.dot(q_ref[rows, hsl[h]], s_olds[h].astype(BF16), preferred_element_type=F32)
                 for h in heads]
        upds = [lax.dot_general(kz_ref[rows, hsl[h]], v_ref[rows, hsl[h]], (((0,), (0,)), ((), ())),
                                preferred_element_type=F32) for h in heads]
        for h in heads:
            state[h] = gl_ref[h] * s_olds[h] + upds[h]
        probs_r = [(scores_r[h] * decay_ref[h]).astype(BF16) for h in heads]
        intra = [jnp.dot(probs_r[h], v_ref[rows, hsl[h]], preferred_element_type=F32) for h in heads]

        owns = [left if h % 2 == 0 else jnp.logical_not(left) for h in heads]
        outs = [intra[h] + inter[h] * xi_ref[:, hsl[h]] for h in heads]
        mus = [jnp.sum(outs[h], axis=-1, keepdims=True) * inv_dim for h in heads]
        cen = [jnp.where(owns[h], outs[h] - mus[h], 0.0) for h in heads]
        var = [jnp.sum(cen[h] * cen[h], axis=-1, keepdims=True) * inv_dim for h in heads]
        nrm = [cen[h] * lax.rsqrt(var[h] + LN_EPS) for h in heads]
        ret_blocks = [nrm[2 * b] + nrm[2 * b + 1] for b in range(N_RET_HEADS // 2)]
        ret_o = jnp.concatenate(ret_blocks, axis=-1) * gain * _silu(g_ref[rows, :])

        kvc = kvc_ref[rows, :]
        kvp = kvp_ref[...] if sub == 0 else kvc_ref[(sub - 1) * CHUNK:sub * CHUNK, :]
        kcat = jnp.concatenate([kvp[:, :SWA_KVW], kvc[:, :SWA_KVW]], axis=0)
        vcat = jnp.concatenate([kvp[:, SWA_KVW:], kvc[:, SWA_KVW:]], axis=0)
        left2 = jnp.concatenate([left, left], axis=0)
        table = jnp.minimum(i, 1) if sub == 0 else 1
        swa_blocks = [None] * GQA_GROUP
        kks, vvs = [], []
        for kv in range(N_KV_HEADS):
            own2 = left2 if kv == 0 else jnp.logical_not(left2)
            kks.append(jnp.where(own2, kcat, 0.0).astype(BF16))
            vvs.append(jnp.where(own2, vcat, 0.0).astype(BF16))
        scores = []
        for hh in range(N_SWA_HEADS):
            kv, j = divmod(hh, GQA_GROUP)
            blk = 2 * j + kv
            qh = sq_ref[rows, blk * 128:(blk + 1) * 128]
            scores.append(lax.dot_general(qh, kks[kv], (((1,), (1,)), ((), ())),
                                          preferred_element_type=F32))
        probs = []
        for hh in range(N_SWA_HEADS):
            s2 = scores[hh]
            s = jnp.where(from_prev, s2[:, :CHUNK], s2[:, CHUNK:]) + bias_ref[table, hh]
            sink = sink_ref[hh]
            m = jnp.maximum(jnp.max(s, axis=-1, keepdims=True), sink)
            p = jnp.exp(s - m)
            den = jnp.sum(p, axis=-1, keepdims=True) + jnp.exp(sink - m)
            pn = p / den
            p2 = jnp.concatenate([jnp.where(from_prev, pn, 0.0), jnp.where(from_prev, 0.0, pn)], axis=-1)
            probs.append(p2.astype(BF16))
        for hh in range(N_SWA_HEADS):
            kv, j = divmod(hh, GQA_GROUP)
            o = jnp.dot(probs[hh], vvs[kv], preferred_element_type=F32)
            swa_blocks[j] = o if kv == 0 else swa_blocks[j] + o
        swa_o = jnp.concatenate(swa_blocks, axis=-1)

        mixed_ref[rows, :] = jnp.concatenate([ret_o, swa_o], axis=-1).astype(BF16)

    @pl.when(i == pl.num_programs(0) - 1)
    def _():
        st_ref[...] = state[...]


def _mix_prompt(proj, ret_consts, bias_tabs, sinks, gain):
    q, k, kz, v, g, sq, skv = proj
    rows = q.shape[0]
    tm = MIX_SUB * CHUNK
    gl, decay, xi = ret_consts
    bias_fold = bias_tabs[0]
    row_spec = lambda w: pl.BlockSpec((tm, w), lambda i: (i, 0))
    const2 = lambda a: pl.BlockSpec(a.shape, lambda i: (0, 0))
    const3 = lambda a: pl.BlockSpec(a.shape, lambda i: (0, 0, 0))
    smem = pl.BlockSpec(memory_space=pltpu.SMEM)
    mixed, st_full = pl.pallas_call(
        _mix_prompt_kernel,
        grid=(rows // tm,),
        in_specs=[smem, smem,
                  row_spec(SPLIT_W), row_spec(SPLIT_W), row_spec(SPLIT_W), row_spec(SPLIT_W), row_spec(RET_W),
                  row_spec(SPLIT_W), row_spec(2 * SWA_KVW),
                  pl.BlockSpec((CHUNK, 2 * SWA_KVW), lambda i: (jnp.maximum(MIX_SUB * i - 1, 0), 0)),
                  const3(decay), const2(xi),
                  pl.BlockSpec(bias_fold.shape, lambda i: (0, 0, 0, 0)),
                  const2(gain)],
        out_specs=[row_spec(D_MODEL),
                   pl.BlockSpec((N_RET_HEADS, 128, 128), lambda i: (0, 0, 0))],
        out_shape=[jax.ShapeDtypeStruct((rows, D_MODEL), BF16),
                   jax.ShapeDtypeStruct((N_RET_HEADS, 128, 128), F32)],
        scratch_shapes=[pltpu.VMEM((N_RET_HEADS, 128, 128), F32)],
        compiler_params=_cparams(("arbitrary",)),
        name="mix_prompt",
    )(gl, sinks, q, k, kz, v, g, sq, skv, skv, decay, xi, bias_fold, gain)
    lo, hi = slice(0, HEAD_DIM), slice(HEAD_DIM, 128)
    st = jnp.stack([st_full[h, lo, lo] if h % 2 == 0 else st_full[h, hi, hi] for h in range(N_RET_HEADS)])
    return mixed, st


def _ret_consts():
    lg = jnp.log(1.0 - 2.0 ** (-5.0 - jnp.arange(N_RET_HEADS, dtype=F32)))
    idx = jnp.arange(CHUNK, dtype=F32)
    diff = idx[:, None] - idx[None, :]
    decay = jnp.where(diff >= 0, jnp.exp(jnp.maximum(diff, 0.0)[None] * lg[:, None, None]), 0.0)
    xi = jnp.exp((idx + 1.0)[:, None] * lg[None, :])
    zeta = jnp.exp((CHUNK - 1.0 - idx)[:, None] * lg[None, :])
    gl = jnp.exp(CHUNK * lg)
    return gl, decay, jnp.repeat(xi, 128, axis=1), jnp.repeat(zeta, HEAD_DIM, axis=1), jnp.exp(1.0 * lg)


SAMPLE_BB = 8


def _mix_sample_kernel(sink_ref, gam_ref,
                       qt_ref, kt_ref, v3_ref, sq3_ref, knew_ref, vnew_ref,
                       st_ref, ck_ref, cv_ref, g3_ref, x_ref, gate_ref,
                       wout_ref, brow_ref, gain_ref, lng_ref, lnb_ref,
                       x1_ref, nst_ref, nk_ref, nv_ref, ret_scr, swa_scr):
    i = pl.program_id(0)
    row = lax.broadcasted_iota(I32, (N_SWA_HEADS, 2 * HEAD_DIM), 0)
    lane = lax.broadcasted_iota(I32, (N_SWA_HEADS, 2 * HEAD_DIM), 1)
    own_half = (row // GQA_GROUP) == (lane // HEAD_DIM)
    sink_col = jnp.concatenate(
        [jnp.full((1, 1), sink_ref[hh], F32) for hh in range(N_SWA_HEADS)], axis=0)
    brow = brow_ref[...]
    qt = qt_ref[0]
    kt = kt_ref[0]
    gam = gam_ref[...]
    seqs = range(SAMPLE_BB)
    state_shape = (N_RET_HEADS, HEAD_DIM, HEAD_DIM)

    s_old = [st_ref[b] for b in seqs]
    v_full = [jnp.concatenate([jnp.broadcast_to(v3_ref[b, h:h + 1, :], (HEAD_DIM, HEAD_DIM))
                               for h in range(N_RET_HEADS)], axis=0) for b in seqs]
    s_new = [gam * s_old[b] + kt[:, b:b + 1] * v_full[b] for b in seqs]
    for b in seqs:
        nst_ref[b] = s_new[b]
    outs = [jnp.sum((qt[:, b:b + 1] * s_new[b]).reshape(state_shape), axis=1) for b in seqs]
    for b in seqs:
        ret_scr[i * SAMPLE_BB + b] = outs[b]

    kk = [jnp.concatenate([ck_ref[0, b, 1:, :], knew_ref[b:b + 1, :]], axis=0) for b in seqs]
    vv = [jnp.concatenate([cv_ref[0, b, 1:, :], vnew_ref[b:b + 1, :]], axis=0) for b in seqs]
    for b in seqs:
        nk_ref[b] = kk[b]
        nv_ref[b] = vv[b]
    qblk = [jnp.where(own_half, jnp.concatenate([sq3_ref[b], sq3_ref[b]], axis=-1), 0.0).astype(BF16)
            for b in seqs]
    s = [lax.dot_general(qblk[b], kk[b].astype(BF16), (((1,), (1,)), ((), ())),
                         preferred_element_type=F32) + brow for b in seqs]
    m = [jnp.maximum(jnp.max(s[b], axis=-1, keepdims=True), sink_col) for b in seqs]
    p = [jnp.exp(s[b] - m[b]) for b in seqs]
    den = [jnp.sum(p[b], axis=-1, keepdims=True) + jnp.exp(sink_col - m[b]) for b in seqs]
    pn = [(p[b] / den[b]).astype(BF16) for b in seqs]
    o = [jnp.dot(pn[b], vv[b].astype(BF16), preferred_element_type=F32) for b in seqs]
    for b in seqs:
        swa_scr[i * SAMPLE_BB + b] = jnp.where(own_half[:, :HEAD_DIM], o[b][:, :HEAD_DIM], o[b][:, HEAD_DIM:])

    @pl.when(i == pl.num_programs(0) - 1)
    def _():
        y = jnp.zeros(x_ref.shape, F32)
        for h in range(N_RET_HEADS):
            o = ret_scr[:, h, :]
            mu = jnp.mean(o, axis=-1, keepdims=True)
            c = o - mu
            var = jnp.mean(c * c, axis=-1, keepdims=True)
            r = c * lax.rsqrt(var + LN_EPS) * gain_ref[h:h + 1, :] * _silu(g3_ref[h])
            y = y + jnp.dot(r.astype(BF16), wout_ref[h * HEAD_DIM:(h + 1) * HEAD_DIM, :],
                            preferred_element_type=F32)
        for hh in range(N_SWA_HEADS):
            o = swa_scr[:, hh, :].astype(BF16)
            lo = RET_W + hh * HEAD_DIM
            y = y + jnp.dot(o, wout_ref[lo:lo + HEAD_DIM, :], preferred_element_type=F32)
        x1_ref[...] = _ln(DN_ALPHA * x_ref[...] + (1.0 + gate_ref[0]) * y, lng_ref[...], lnb_ref[...])


def _mix_sample(proj, x, mod, layer, w_out_bf, gamma1, bias_row, sinks, gain8, ln_g, ln_b,
                st, ck, cv):
    q, k, _, v, g, sq, skv = proj
    nb = x.shape[0]
    steps = nb // SAMPLE_BB

    def joined(a):
        a = a.astype(F32).reshape(nb, RET_W // 128, 2, 128)
        return (a[:, :, 0, :] + a[:, :, 1, :]).reshape(nb, RET_W)

    to_cols = lambda a: joined(a).reshape(steps, SAMPLE_BB, RET_W).transpose(0, 2, 1)
    qt, kt = to_cols(q), to_cols(k)
    v3 = joined(v).reshape(nb, N_RET_HEADS, HEAD_DIM)
    sq3 = joined(sq).reshape(nb, GQA_GROUP, N_KV_HEADS, HEAD_DIM).transpose(0, 2, 1, 3)
    sq3 = sq3.reshape(nb, N_SWA_HEADS, HEAD_DIM)
    g3 = g.reshape(nb, N_RET_HEADS, HEAD_DIM).transpose(1, 0, 2)
    knew, vnew = skv[:, :SWA_KVW], skv[:, SWA_KVW:]
    gam =jnp.broadcast_to(jnp.repeat(gamma1, HEAD_DIM)[:, None], (RET_W, HEAD_DIM))
    smem = pl.BlockSpec(memory_space=pltpu.SMEM)
    blk3 = lambda a, b, c: pl.BlockSpec((a, b, c), lambda i: (i, 0, 0))
    lay4 = lambda b, c: pl.BlockSpec((1, SAMPLE_BB, b, c), lambda i: (layer, i, 0, 0))
    const2 = lambda a: pl.BlockSpec(a.shape, lambda i: (0, 0))
    const3 = lambda a: pl.BlockSpec(a.shape, lambda i: (0, 0, 0))
    x1, nst, nk, nv = pl.pallas_call(
        _mix_sample_kernel,
        grid=(steps,),
        in_specs=[smem, const2(gam),
                  blk3(1, RET_W, SAMPLE_BB), blk3(1, RET_W, SAMPLE_BB),
                  blk3(SAMPLE_BB, N_RET_HEADS, HEAD_DIM), blk3(SAMPLE_BB, N_SWA_HEADS, HEAD_DIM),
                  pl.BlockSpec((SAMPLE_BB, SWA_KVW), lambda i: (i, 0)),
                  pl.BlockSpec((SAMPLE_BB, SWA_KVW), lambda i: (i, 0)),
                  blk3(SAMPLE_BB, RET_W, HEAD_DIM), lay4(CHUNK, SWA_KVW), lay4(CHUNK, SWA_KVW),
                  const3(g3), const2(x),
                  pl.BlockSpec((1, nb, D_MODEL), lambda i: (layer, 0, 2)),
                  const2(w_out_bf), const2(bias_row), const2(gain8), const2(ln_g), const2(ln_b)],
        out_specs=[const2(x), blk3(SAMPLE_BB, RET_W, HEAD_DIM),
                   blk3(SAMPLE_BB, CHUNK, SWA_KVW), blk3(SAMPLE_BB, CHUNK, SWA_KVW)],
        out_shape=[jax.ShapeDtypeStruct(x.shape, F32), jax.ShapeDtypeStruct(st.shape, F32),
                   jax.ShapeDtypeStruct(ck.shape[1:], F32), jax.ShapeDtypeStruct(cv.shape[1:], F32)],
        scratch_shapes=[pltpu.VMEM((nb, N_RET_HEADS, HEAD_DIM), F32),
                        pltpu.VMEM((nb, N_SWA_HEADS, HEAD_DIM), F32)],
        compiler_params=_cparams(("arbitrary",)),
        name="mix_sample",
    )(sinks, gam, qt, kt, v3, sq3, knew, vnew, st, ck, cv, g3, x, mod,
      w_out_bf, bias_row, gain8, ln_g, ln_b)
    cache_shape = (nb, CHUNK, N_KV_HEADS, HEAD_DIM)
    return (x1, nst.reshape(nb, N_RET_HEADS, HEAD_DIM, HEAD_DIM), nk.reshape(cache_shape),
            nv.reshape(cache_shape))


def _router_kernel(x_ref, sh_ref, sc_ref, rwt_ref, rb_ref, tri_ref,
                   oi_ref, of_ref, cnt_ref, run_ref, *, mod_rows):
    h2 = x_ref[...] * (1.0 + sc_ref[0][:mod_rows]) + sh_ref[0][:mod_rows]
    _route_rows(h2.astype(BF16), rwt_ref, rb_ref, tri_ref, oi_ref, of_ref, cnt_ref, run_ref)


def _post_mix_kernel(mixed_ref, x_ref, gate_ref, wout_ref, lng_ref, lnb_ref,
                     sh_ref, sc_ref, rwt_ref, rb_ref, tri_ref,
                     x1_ref, oi_ref, of_ref, cnt_ref, run_ref):
    y = jnp.dot(mixed_ref[...], wout_ref[...], preferred_element_type=F32)
    x1 = _ln(DN_ALPHA * x_ref[...] + (1.0 + gate_ref[0][:1]) * y, lng_ref[...], lnb_ref[...])
    x1_ref[...] = x1
    h2 = x1 * (1.0 + sc_ref[0][:1]) + sh_ref[0][:1]
    _route_rows(h2.astype(BF16), rwt_ref, rb_ref, tri_ref, oi_ref, of_ref, cnt_ref, run_ref)


def _route_rows(h2b, rwt_ref, rb_ref, tri_ref, oi_ref, of_ref, cnt_ref, run_ref):
    i = pl.program_id(0)

    @pl.when(i == 0)
    def _():
        run_ref[...] = jnp.zeros_like(run_ref)

    logits = lax.dot_general(rwt_ref[...].astype(BF16), h2b, (((1,), (1,)), ((), ())),
                             preferred_element_type=F32)
    aff = 1.0 / (1.0 + jnp.exp(-logits))
    sel = aff + rb_ref[...]
    s = [sel[e:e + 1, :] for e in range(N_EXPERTS)]
    a = [aff[e:e + 1, :] for e in range(N_EXPERTS)]

    def top2sum(v0, v1, v2, v3):
        hi01, lo01 = jnp.maximum(v0, v1), jnp.minimum(v0, v1)
        hi23, lo23 = jnp.maximum(v2, v3), jnp.minimum(v2, v3)
        return jnp.maximum(hi01, hi23) + jnp.maximum(jnp.minimum(hi01, hi23),
                                                     jnp.maximum(lo01, lo23))

    def argmax_first(vals):
        best, idx = vals[0], jnp.zeros(vals[0].shape, I32)
        for j in range(1, len(vals)):
            upd = vals[j] > best
            idx = jnp.where(upd, j, idx)
            best = jnp.where(upd, vals[j], best)
        return idx

    def pick(idx, vals):
        out = vals[-1]
        for j in range(len(vals) - 2, -1, -1):
            out = jnp.where(idx == j, vals[j], out)
        return out

    gi = argmax_first([top2sum(*s[4 * g:4 * g + 4]) for g in range(N_GROUPS)])
    sv = [pick(gi, [s[4 * g + j] for g in range(N_GROUPS)]) for j in range(EXPERTS_PER_GROUP)]
    av = [pick(gi, [a[4 * g + j] for g in range(N_GROUPS)]) for j in range(EXPERTS_PER_GROUP)]
    i1 = argmax_first(sv)
    i2 = argmax_first([jnp.where(i1 == j, -jnp.inf, sv[j]) for j in range(EXPERTS_PER_GROUP)])
    w1, w2 = pick(i1, av), pick(i2, av)
    wsum = w1 + w2
    w1, w2 = w1 / wsum, w2 / wsum
    lo, hi = jnp.minimum(i1, i2), jnp.maximum(i1, i2)
    w_lo = jnp.where(i1 < i2, w1, w2)
    w_hi = jnp.where(i1 < i2, w2, w1)
    pair = jnp.where(lo == 0, hi - 1, jnp.where(lo == 1, 6 - hi, 5))
    bin_id = gi * N_PAIRS + pair

    tm = bin_id.shape[1]
    onehot = lax.broadcasted_iota(I32, (BIN_ROWS, tm), 0) == bin_id
    oh_f = jnp.where(onehot, 1.0, 0.0)
    before = jnp.dot(oh_f.astype(BF16), tri_ref[...], preferred_element_type=F32)
    run = run_ref[...]
    run_t = jnp.concatenate([run] * (tm // 128), axis=-1)
    rank = jnp.sum(oh_f * (before + run_t), axis=0, keepdims=True)
    run_new = run + jnp.sum(oh_f, axis=1, keepdims=True)
    run_ref[...] = run_new
    cnt_ref[...] = run_new.astype(I32)

    zi = jnp.zeros_like(bin_id)
    oi_ref[0] = jnp.concatenate([gi * 4 + lo, gi * 4 + hi, bin_id, rank.astype(I32), zi, zi, zi, zi], axis=0)
    zf = jnp.zeros_like(w_lo)
    of_ref[0] = jnp.concatenate([w_lo, w_hi, zf, zf, zf, zf, zf, zf], axis=0)


def _router(x1, mod, layer, mod_row0, mod_rows, router_wt, router_b, tm):
    rows = x1.shape[0]
    nt = rows // tm
    mblk = 8 if mod_rows == 1 else mod_rows
    mrow = mod_row0 // mblk
    tri = jnp.asarray(np.triu(np.ones((tm, tm), np.float32), 1), BF16)
    rb = jnp.broadcast_to(router_b.astype(F32)[:, None], (N_EXPERTS, tm))
    mod_spec = lambda c: pl.BlockSpec((1, mblk, D_MODEL), lambda i: (layer, mrow, c))
    oi, of, cnt = pl.pallas_call(
        functools.partial(_router_kernel, mod_rows=mod_rows),
        grid=(nt,),
        in_specs=[pl.BlockSpec((tm, D_MODEL), lambda i: (i, 0)), mod_spec(3), mod_spec(4),
                  pl.BlockSpec((N_EXPERTS, D_MODEL), lambda i: (0, 0)),
                  pl.BlockSpec((N_EXPERTS, tm), lambda i: (0, 0)),
                  pl.BlockSpec((tm, tm), lambda i: (0, 0))],
        out_specs=[pl.BlockSpec((1, 8, tm), lambda i: (i, 0, 0)),
                   pl.BlockSpec((1, 8, tm), lambda i: (i, 0, 0)),
                   pl.BlockSpec((BIN_ROWS, 128), lambda i: (0, 0))],
        out_shape=[jax.ShapeDtypeStruct((nt, 8, tm), I32),
                   jax.ShapeDtypeStruct((nt, 8, tm), F32),
                   jax.ShapeDtypeStruct((BIN_ROWS, 128), I32)],
        scratch_shapes=[pltpu.VMEM((BIN_ROWS, 128), F32)],
        compiler_params=_cparams(("arbitrary",)),
        name="router",
    )(x1, mod, mod, router_wt, rb, tri)
    return _unpack_route(oi, of, cnt)


def _unpack_route(oi, of, cnt):
    flat = lambda a, r: a[:, r, :].reshape(-1)
    return (flat(oi, 0), flat(oi, 1), flat(oi, 2), flat(oi, 3), flat(of, 0), flat(of, 1),
            cnt[:N_BINS, 0])


POST_TM = 512


def _post_mix(mixed, x, mod, layer, mod_row0, w_out_bf, ln_g, ln_b, router_wt, router_b):
    rows = x.shape[0]
    tm = POST_TM
    nt = rows // tm
    mrow = mod_row0 // 8
    tri = jnp.asarray(np.triu(np.ones((tm, tm), np.float32), 1), BF16)
    rb = jnp.broadcast_to(router_b.astype(F32)[:, None], (N_EXPERTS, tm))
    mod_spec = lambda c: pl.BlockSpec((1, 8, D_MODEL), lambda i: (layer, mrow, c))
    row_spec = pl.BlockSpec((tm, D_MODEL), lambda i: (i, 0))
    route_spec = pl.BlockSpec((1, 8, tm), lambda i: (i, 0, 0))
    const2 = lambda a: pl.BlockSpec(a.shape, lambda i: (0, 0))
    x1, oi, of, cnt = pl.pallas_call(
        _post_mix_kernel,
        grid=(nt,),
        in_specs=[row_spec, row_spec, mod_spec(2), const2(w_out_bf), const2(ln_g), const2(ln_b),
                  mod_spec(3), mod_spec(4), const2(router_wt), const2(rb), const2(tri)],
        out_specs=[row_spec, route_spec, route_spec,
                   pl.BlockSpec((BIN_ROWS, 128), lambda i: (0, 0))],
        out_shape=[jax.ShapeDtypeStruct((rows, D_MODEL), F32),
                   jax.ShapeDtypeStruct((nt, 8, tm), I32),
                   jax.ShapeDtypeStruct((nt, 8, tm), F32),
                   jax.ShapeDtypeStruct((BIN_ROWS, 128), I32)],
        scratch_shapes=[pltpu.VMEM((BIN_ROWS, 128), F32)],
        compiler_params=_cparams(("arbitrary",)),
        name="post_mix",
    )(mixed, x, mod, w_out_bf, ln_g, ln_b, mod, mod, router_wt, rb, tri)
    return x1, _unpack_route(oi, of, cnt)


DMA_UNROLL = 8
ROW_GROUPS = MOE_TM // DMA_UNROLL


GROUPS_PER_ITER = 2


def _for_rows(n, fn):
    per_iter = GROUPS_PER_ITER * DMA_UNROLL

    def body_u(jj, carry):
        for g in range(GROUPS_PER_ITER):
            for u in range(DMA_UNROLL):
                fn(jj * GROUPS_PER_ITER + g, u)
        return carry
    lax.fori_loop(0, n // per_iter, body_u, 0)

    def body_1(r, carry):
        fn(r // DMA_UNROLL, r % DMA_UNROLL)
        return carry
    lax.fori_loop((n // per_iter) * per_iter, n, body_1, 0)


def _wait_rows(n, buf, sem):
    p = ROW_GROUPS
    while p >= 1:
        @pl.when((n & (p * DMA_UNROLL)) != 0)
        def _(p=p):
            pltpu.make_async_copy(buf.at[pl.ds(0, p)], buf.at[pl.ds(0, p)], sem).wait()
        p //= 2
    p = DMA_UNROLL // 2
    while p >= 1:
        @pl.when((n & p) != 0)
        def _(p=p):
            pltpu.make_async_copy(buf.at[0, pl.ds(0, p)], buf.at[0, pl.ds(0, p)], sem).wait()
        p //= 2


def _moe_sorted_kernel(ea_ref, eb_ref, nvalid_ref, pos_ref,
                       x_hbm, sh_ref, sc_ref, gate_ref, rw_ref,
                       wga_ref, wua_ref, wda_ref, wgb_ref, wub_ref, wdb_ref,
                       lng_ref, lnb_ref,
                       out_hbm,
                       inv, xbuf, obuf, wg_a, wu_a, wd_a, wg_b, wu_b, wd_b, gsem, ssem):
    i = pl.program_id(0)
    nt = pl.num_programs(0)
    slot = i % 2
    n_tok = pos_ref.shape[0]

    def start_gather(t, dst_slot):
        def one(j, u):
            tok = inv[t * MOE_TM + j * DMA_UNROLL + u]
            pltpu.make_async_copy(x_hbm.at[pl.ds(tok, 1)], xbuf.at[dst_slot, j, pl.ds(u, 1)],
                                  gsem.at[dst_slot]).start()
        _for_rows(nvalid_ref[t], one)

    @pl.when(i == 0)
    def _():
        def body(j, carry):
            for u in range(DMA_UNROLL):
                t = j * DMA_UNROLL + u
                inv[pos_ref[t]] = t
            return carry
        lax.fori_loop(0, n_tok // DMA_UNROLL, body, 0)
        xbuf[...] = jnp.zeros_like(xbuf)
        start_gather(0, 0)

    @pl.when(i >= 2)
    def _():
        _wait_rows(nvalid_ref[jnp.maximum(i - 2, 0)], obuf.at[slot], ssem.at[slot])

    @pl.when(nvalid_ref[i] > 0)
    def _():
        _wait_rows(nvalid_ref[i], xbuf.at[slot], gsem.at[slot])

        @pl.when(i + 1 < nt)
        def _():
            start_gather(jnp.minimum(i + 1, nt - 1), 1 - slot)

        @pl.when(jnp.logical_or(i == 0, ea_ref[i] != ea_ref[jnp.maximum(i - 1, 0)]))
        def _():
            wg_a[...] = wga_ref[0, 0].astype(BF16)
            wu_a[...] = wua_ref[0, 0].astype(BF16)
            wd_a[...] = wda_ref[0, 0].astype(BF16)

        @pl.when(jnp.logical_or(i == 0, eb_ref[i] != eb_ref[jnp.maximum(i - 1, 0)]))
        def _():
            wg_b[...] = wgb_ref[0, 0].astype(BF16)
            wu_b[...] = wub_ref[0, 0].astype(BF16)
            wd_b[...] = wdb_ref[0, 0].astype(BF16)

        x = xbuf[slot].reshape(MOE_TM, D_MODEL)
        h2 = (x * (1.0 + sc_ref[0][:1]) + sh_ref[0][:1]).astype(BF16)

        dot = functools.partial(jnp.dot, preferred_element_type=F32)
        logits = dot(h2, rw_ref[...])
        gate_a, gate_b = dot(h2, wg_a[...]), dot(h2, wg_b[...])
        up_a, up_b = dot(h2, wu_a[...]), dot(h2, wu_b[...])
        act_a = (_silu(gate_a) * up_a).astype(BF16)
        act_b = (_silu(gate_b) * up_b).astype(BF16)
        y_a, y_b = dot(act_a, wd_a[...]), dot(act_b, wd_b[...])

        aff = 1.0 / (1.0 + jnp.exp(-logits))
        lane = lax.broadcasted_iota(I32, aff.shape, 1)
        a_lo = jnp.sum(jnp.where(lane == ea_ref[i], aff, 0.0), axis=-1, keepdims=True)
        a_hi = jnp.sum(jnp.where(lane == eb_ref[i], aff, 0.0), axis=-1, keepdims=True)
        a_sum = a_lo + a_hi
        y = (a_lo / a_sum) * y_a + (a_hi / a_sum) * y_b
        out = _ln(DN_ALPHA * x + (1.0 + gate_ref[0][:1]) * y, lng_ref[...], lnb_ref[...])
        obuf[slot] = out.reshape(ROW_GROUPS, DMA_UNROLL, D_MODEL)

        def scatter_one(j, u):
            tok = inv[i * MOE_TM + j * DMA_UNROLL + u]
            pltpu.make_async_copy(obuf.at[slot, j, pl.ds(u, 1)], out_hbm.at[pl.ds(tok, 1)],
                                  ssem.at[slot]).start()
        _for_rows(nvalid_ref[i], scatter_one)

    @pl.when(i == nt - 1)
    def _():
        _wait_rows(nvalid_ref[jnp.maximum(i - 1, 0)], obuf.at[1 - slot], ssem.at[1 - slot])
        _wait_rows(nvalid_ref[i], obuf.at[slot], ssem.at[slot])


def _moe_sorted(x1, mod, layer, mod_row0, route, rw_pad, wg, wu, wd, ln_g, ln_b):
    _, _, bin_id, rank, _, _, counts = route
    rows = x1.shape[0]
    nt = rows // MOE_TM + N_BINS
    ns = nt * MOE_TM
    tiles = (counts + MOE_TM - 1) // MOE_TM
    tile_end = jnp.cumsum(tiles)
    row_start = ((tile_end - tiles) * MOE_TM).astype(I32)
    pos = jnp.sum(jnp.where(bin_id[:, None] == jnp.arange(N_BINS, dtype=I32)[None, :],
                            row_start[None, :], 0), axis=1) + rank
    total = tile_end[-1]
    t = jnp.arange(nt, dtype=I32)
    tile_bin = jnp.sum(jnp.minimum(t, total - 1)[:, None] >= tile_end[None, :], axis=1).astype(I32)
    tile_bin = jnp.clip(tile_bin, 0, N_BINS - 1)
    bin_lo = jnp.asarray([g * 4 + _PAIR_LO[p] for g in range(N_GROUPS) for p in range(N_PAIRS)], I32)
    bin_hi = jnp.asarray([g * 4 + _PAIR_HI[p] for g in range(N_GROUPS) for p in range(N_PAIRS)], I32)
    tile_ea, tile_eb = bin_lo[tile_bin], bin_hi[tile_bin]
    nvalid = jnp.clip(counts[tile_bin] - (t - (tile_end - tiles)[tile_bin]) * MOE_TM, 0, MOE_TM)
    nvalid = jnp.where(t < total, nvalid, 0).astype(I32)
    mrow = mod_row0 // 8
    mod_spec = lambda c: pl.BlockSpec((1, 8, D_MODEL), lambda i, *_: (layer, mrow, c))
    w_spec = lambda shape, ref_idx: pl.BlockSpec(
        (1, 1) + shape, lambda i, ea_r, eb_r, *_: (layer, (ea_r, eb_r)[ref_idx][i], 0, 0))
    any_spec = pl.BlockSpec(memory_space=pl.ANY)
    w_bf = lambda shape: pltpu.VMEM(shape, BF16)
    grid_spec = pltpu.PrefetchScalarGridSpec(
        num_scalar_prefetch=4,
        grid=(nt,),
        in_specs=[any_spec,
                  mod_spec(3), mod_spec(4), mod_spec(5),
                  pl.BlockSpec((D_MODEL, 128), lambda i, *_: (0, 0)),
                  w_spec((D_MODEL, D_FF), 0), w_spec((D_MODEL, D_FF), 0), w_spec((D_FF, D_MODEL), 0),
                  w_spec((D_MODEL, D_FF), 1), w_spec((D_MODEL, D_FF), 1), w_spec((D_FF, D_MODEL), 1),
                  pl.BlockSpec((1, D_MODEL), lambda i, *_: (0, 0)),
                  pl.BlockSpec((1, D_MODEL), lambda i, *_: (0, 0))],
        out_specs=any_spec,
        scratch_shapes=[pltpu.SMEM((ns,), I32),
                        pltpu.VMEM((2, ROW_GROUPS, DMA_UNROLL, D_MODEL), F32),
                        pltpu.VMEM((2, ROW_GROUPS, DMA_UNROLL, D_MODEL), F32),
                        w_bf((D_MODEL, D_FF)), w_bf((D_MODEL, D_FF)), w_bf((D_FF, D_MODEL)),
                        w_bf((D_MODEL, D_FF)), w_bf((D_MODEL, D_FF)), w_bf((D_FF, D_MODEL)),
                        pltpu.SemaphoreType.DMA((2,)),
                        pltpu.SemaphoreType.DMA((2,))],
    )
    return pl.pallas_call(
        _moe_sorted_kernel,
        grid_spec=grid_spec,
        out_shape=jax.ShapeDtypeStruct((rows, D_MODEL), F32),
        compiler_params=_cparams(("arbitrary",)),
        name="moe_sorted",
    )(tile_ea, tile_eb, nvalid, pos, x1, mod, mod, mod, rw_pad,
      wg, wu, wd, wg, wu, wd, ln_g, ln_b)


def _moe_dense_kernel(x_ref, dw_ref, sh_ref, sc_ref, gate_ref, wg_ref, wu_ref, wd_ref,
                      lng_ref, lnb_ref, o_ref, acc):
    e = pl.program_id(0)

    @pl.when(e == 0)
    def _():
        acc[...] = jnp.zeros_like(acc)

    x = x_ref[...]
    h2 = (x * (1.0 + sc_ref[0]) + sh_ref[0]).astype(BF16)
    a = _silu(jnp.dot(h2, wg_ref[0, 0].astype(BF16), preferred_element_type=F32)) * \
        jnp.dot(h2, wu_ref[0, 0].astype(BF16), preferred_element_type=F32)
    y = jnp.dot(a.astype(BF16), wd_ref[0, 0].astype(BF16), preferred_element_type=F32)
    acc[...] = acc[...] + dw_ref[0][:, :1] * y

    @pl.when(e == pl.num_programs(0) - 1)
    def _():
        o_ref[...] = _ln(DN_ALPHA * x + (1.0 + gate_ref[0]) * acc[...], lng_ref[...], lnb_ref[...])


def _moe_dense(x1, mod, layer, route, wg, wu, wd, ln_g, ln_b):
    ea, eb, _, _, w_lo, w_hi, _ = route
    nb = x1.shape[0]
    eids = jnp.arange(N_EXPERTS, dtype=I32)[:, None]
    dw = jnp.where(eids == ea[None, :], w_lo[None, :], 0.0) + \
        jnp.where(eids == eb[None, :], w_hi[None, :], 0.0)
    dw = jnp.broadcast_to(dw[:, :, None], (N_EXPERTS, nb, 128))
    mod_spec = lambda c: pl.BlockSpec((1, nb, D_MODEL), lambda e: (layer, 0, c))
    const2 = lambda a: pl.BlockSpec(a.shape, lambda e: (0, 0))
    return pl.pallas_call(
        _moe_dense_kernel,
        grid=(N_EXPERTS,),
        in_specs=[const2(x1), pl.BlockSpec((1, nb, 128), lambda e: (e, 0, 0)),
                  mod_spec(3), mod_spec(4), mod_spec(5),
                  pl.BlockSpec((1, 1, D_MODEL, D_FF), lambda e: (layer, e, 0, 0)),
                  pl.BlockSpec((1, 1, D_MODEL, D_FF), lambda e: (layer, e, 0, 0)),
                  pl.BlockSpec((1, 1, D_FF, D_MODEL), lambda e: (layer, e, 0, 0)),
                  const2(ln_g), const2(ln_b)],
        out_specs=const2(x1),
        out_shape=jax.ShapeDtypeStruct(x1.shape, F32),
        scratch_shapes=[pltpu.VMEM(x1.shape, F32)],
        compiler_params=_cparams(("arbitrary",)),
        name="moe_dense",
    )(x1, dw, mod, mod, mod, wg, wu, wd, ln_g, ln_b)


def kernel(x_prompt, x_sample, state_ret, cache_swa_k, cache_swa_v, c_prompt, c_sample, w_in, w_out, ret_gn_gain, swa_sinks, rel_bias_table, ada_w, ada_b, ln1_g, ln1_b, ln2_g, ln2_b, router_w, router_b, exp_w_gate, exp_w_up, exp_w_down):
    seq = x_prompt.shape[1]
    nb = x_sample.shape[0]
    past_len = 16384
    assert x_prompt.shape[0] == 1 and x_sample.shape[1] == 1

    c_all = jnp.concatenate([c_sample, c_prompt, jnp.zeros((7, D_MODEL), F32)], axis=0)
    mod = _ada(c_all, ada_w, ada_b)
    prompt_row = nb

    bias_tabs = _bias_tables(rel_bias_table.astype(F32))
    gl, decay, xi, zeta, gamma1 = _ret_consts()
    rope_p = _rope_tables(jnp.arange(seq, dtype=I32))
    rope_s = _rope_tables(jnp.full((nb,), past_len, I32))
    router_wt = router_w.astype(F32).T
    rw_pad = jnp.pad(router_w.astype(BF16), ((0, 0), (0, 128 - N_EXPERTS)))
    zeta_p = jnp.tile(zeta, (INPROJ_TM // CHUNK, 1))
    zeta_s = jnp.ones((nb, RET_W), F32)

    ck_all = cache_swa_k.astype(F32).reshape(DEPTH, nb, CHUNK, SWA_KVW)
    cv_all = cache_swa_v.astype(F32).reshape(DEPTH, nb, CHUNK, SWA_KVW)

    xp = x_prompt.reshape(seq, D_MODEL)
    xs = x_sample.reshape(nb, D_MODEL)
    st_p, k_p, v_p, st_s, k_s, v_s = [], [], [], [], [], []
    for l in range(DEPTH):
        w_in_bf = w_in[l].astype(BF16)
        w_out_bf = w_out[l].astype(BF16)
        order = jnp.asarray(_SWA_HEAD_ORDER)
        swa_rows = w_out[l][RET_W:].reshape(N_SWA_HEADS, HEAD_DIM, D_MODEL)[order]
        w_out_prompt = jnp.concatenate([w_out[l][:RET_W], swa_rows.reshape(SWA_QW, D_MODEL)],
                                       axis=0).astype(BF16)
        experts = (exp_w_gate.astype(F32), exp_w_up.astype(F32), exp_w_down.astype(F32))
        gain = ret_gn_gain[l].astype(F32)
        sinks = swa_sinks[l].astype(F32)
        row = lambda a: a[l].astype(F32).reshape(1, D_MODEL)

        proj = _inproj(xp, mod, l, prompt_row, 1, w_in_bf, rope_p, zeta_p, INPROJ_TM)
        mixed, st = _mix_prompt(proj, (gl, decay, xi), bias_tabs, sinks, gain.reshape(1, RET_W))
        skv = proj[6]
        st_p.append(st.reshape(1, N_RET_HEADS, HEAD_DIM, HEAD_DIM))
        k_p.append(skv[seq - CHUNK:, :SWA_KVW].reshape(1, CHUNK, N_KV_HEADS, HEAD_DIM))
        v_p.append(skv[seq - CHUNK:, SWA_KVW:].reshape(1, CHUNK, N_KV_HEADS, HEAD_DIM))
        x1, route = _post_mix(mixed, xp, mod, l, prompt_row, w_out_prompt, row(ln1_g), row(ln1_b),
                              router_wt, router_b)
        xp = _moe_sorted(x1, mod, l, prompt_row, route, rw_pad, *experts, row(ln2_g), row(ln2_b))

        proj = _inproj(xs, mod, l, 0, nb, w_in_bf, rope_s, zeta_s, nb)
        x1, nst, nk, nv = _mix_sample(proj, xs, mod, l, w_out_bf, gamma1, bias_tabs[1], sinks, gain,
                                      row(ln1_g), row(ln1_b),
                                      state_ret[l].astype(F32).reshape(nb, RET_W, HEAD_DIM), ck_all, cv_all)
        st_s.append(nst)
        k_s.append(nk)
        v_s.append(nv)
        route = _router(x1, mod, l, 0, nb, router_wt, router_b, nb)
        xs = _moe_dense(x1, mod, l, route, *experts, row(ln2_g), row(ln2_b))

    return (xp.reshape(1, seq, D_MODEL), xs.reshape(nb, 1, D_MODEL),
            jnp.stack(st_p), jnp.stack(k_p), jnp.stack(v_p),
            jnp.stack(st_s), jnp.stack(k_s), jnp.stack(v_s))
```

```python
import functools
import math

import numpy as np
import jax
import jax.numpy as jnp
from jax import lax
from jax.experimental import pallas as pl
from jax.experimental.pallas import tpu as pltpu

F32 = jnp.float32
BF16 = jnp.bfloat16
I32 = jnp.int32

D_MODEL = 1024
DEPTH = 2
HEAD_DIM = 64
N_RET_HEADS = 8
N_SWA_HEADS = 8
N_KV_HEADS = 2
GQA_GROUP = N_SWA_HEADS // N_KV_HEADS
RET_W = N_RET_HEADS * HEAD_DIM
SWA_QW = N_SWA_HEADS * HEAD_DIM
SWA_KVW = N_KV_HEADS * HEAD_DIM
PROJ_W = 4 * RET_W + SWA_QW + 2 * SWA_KVW
CHUNK = 128
ROPE_BASE = 10000.0
N_BUCKETS = 32
MAX_DISTANCE = 128
N_EXPERTS = 16
N_GROUPS = 4
EXPERTS_PER_GROUP = 4
D_FF = 512
LN_EPS = 1e-5
DN_ALPHA = (2 * DEPTH) ** 0.25
QK_SCALE = HEAD_DIM ** -0.5
NEG_BIG = -1e30

N_PAIRS = 6
N_BINS = N_GROUPS * N_PAIRS
BIN_ROWS = 32
MOE_TM = 256
INPROJ_TM = 512
MIX_SUB = 4
SPLIT_W = 128 * N_RET_HEADS
_SWA_HEAD_ORDER = (0, 4, 1, 5, 2, 6, 3, 7)
VMEM_LIMIT = 56 * 1024 * 1024

_PAIR_LO = (0, 0, 0, 1, 1, 3)
_PAIR_HI = (1, 2, 3, 3, 2, 2)


def _cparams(sem):
    return pltpu.CompilerParams(dimension_semantics=sem, vmem_limit_bytes=VMEM_LIMIT)


def _ln(v, g, b):
    mu = jnp.mean(v, axis=-1, keepdims=True)
    c = v - mu
    var = jnp.mean(c * c, axis=-1, keepdims=True)
    return c * lax.rsqrt(var + LN_EPS) * g + b


def _silu(v):
    return v * (1.0 / (1.0 + jnp.exp(-v)))


def _ada_kernel(c_ref, w_ref, b_ref, o_ref):
    o_ref[0] = jnp.dot(c_ref[...].astype(BF16), w_ref[0].astype(BF16),
                       preferred_element_type=F32) + b_ref[0]


def _ada(c_all, ada_w, ada_b):
    rows = c_all.shape[0]
    nt = 6 * D_MODEL // 1024
    return pl.pallas_call(
        _ada_kernel,
        grid=(DEPTH, nt),
        in_specs=[pl.BlockSpec((rows, D_MODEL), lambda l, j: (0, 0)),
                  pl.BlockSpec((1, D_MODEL, 1024), lambda l, j: (l, 0, j)),
                  pl.BlockSpec((1, 1, 1024), lambda l, j: (l, 0, j))],
        out_specs=pl.BlockSpec((1, rows, 1024), lambda l, j: (l, 0, j)),
        out_shape=jax.ShapeDtypeStruct((DEPTH, rows, 6 * D_MODEL), F32),
        compiler_params=_cparams(("arbitrary", "arbitrary")),
        name="ada",
    )(c_all, ada_w, ada_b.reshape(DEPTH, 1, 6 * D_MODEL))


def _bias_kernel(tab_ref, bkt_ref, fold_ref, row_ref):
    bkt = bkt_ref[...]
    rows = lax.broadcasted_iota(I32, (CHUNK, CHUNK), 0)
    cols = lax.broadcasted_iota(I32, (CHUNK, CHUNK), 1)
    from_prev = cols > rows
    for h in range(N_SWA_HEADS):
        acc = jnp.zeros(bkt.shape, F32)
        for b in range(N_BUCKETS):
            acc = jnp.where(bkt == b, tab_ref[b, h], acc)
        own = acc[:, CHUNK:]
        fold_ref[0, h] = jnp.where(from_prev, NEG_BIG, own)
        fold_ref[1, h] = jnp.where(from_prev, acc[:, :CHUNK], own)
        row_ref[h:h + 1, :] = own[CHUNK - 1:CHUNK, :]


def _t5_bucket(rel):
    max_exact = N_BUCKETS // 2
    relf = jnp.maximum(rel, 1).astype(F32)
    large = max_exact + (jnp.log(relf / max_exact) / math.log(MAX_DISTANCE / max_exact)
                         * (N_BUCKETS - max_exact)).astype(I32)
    large = jnp.minimum(large, N_BUCKETS - 1)
    return jnp.where(rel < max_exact, rel, large)


def _bias_tables(rel_bias_table):
    qi = jnp.arange(CHUNK)
    si = jnp.arange(2 * CHUNK)
    rel = CHUNK + qi[:, None] - si[None, :]
    bkt = _t5_bucket(jnp.maximum(rel, 0)).astype(I32)
    return pl.pallas_call(
        _bias_kernel,
        in_specs=[pl.BlockSpec(memory_space=pltpu.SMEM),
                  pl.BlockSpec((CHUNK, 2 * CHUNK), lambda: (0, 0))],
        out_specs=[pl.BlockSpec((2, N_SWA_HEADS, CHUNK, CHUNK), lambda: (0, 0, 0, 0)),
                   pl.BlockSpec((N_SWA_HEADS, CHUNK), lambda: (0, 0))],
        out_shape=[jax.ShapeDtypeStruct((2, N_SWA_HEADS, CHUNK, CHUNK), F32),
                   jax.ShapeDtypeStruct((N_SWA_HEADS, CHUNK), F32)],
        name="t5_bias",
    )(rel_bias_table, bkt)


def _rotary(v, cos, s_lo, s_hi):
    outs = []
    for j in range(RET_W // 128):
        blk = v[:, j * 128:(j + 1) * 128]
        outs.append(blk * cos + pltpu.roll(blk, 96, 1) * s_lo + pltpu.roll(blk, 32, 1) * s_hi)
    return jnp.concatenate(outs, axis=-1)


def _inproj_kernel(x_ref, sh_ref, sc_ref, w_ref, cs_ref, zeta_ref,
                   q_ref, k_ref, kz_ref, v_ref, g_ref, sq_ref, skv_ref, *, mod_rows):
    sh = sh_ref[0][:mod_rows]
    sc = sc_ref[0][:mod_rows]
    h = (x_ref[...] * (1.0 + sc) + sh).astype(BF16)
    cs = jnp.concatenate([cs_ref[...], cs_ref[...]], axis=-1)
    first = (lax.broadcasted_iota(I32, cs.shape, 1) % HEAD_DIM) < (HEAD_DIM // 2)
    cos = jnp.where(first, cs, pltpu.roll(cs, 32, 1))
    s_lo = jnp.where(first, -pltpu.roll(cs, 96, 1), 0.0)
    s_hi = jnp.where(first, 0.0, cs)

    def proj(lo, hi):
        return jnp.dot(h, w_ref[:, lo:hi], preferred_element_type=F32)

    left = lax.broadcasted_iota(I32, cs.shape, 1) < HEAD_DIM

    def store_split(ref, val):
        for b in range(RET_W // 128):
            blk = val[:, b * 128:(b + 1) * 128]
            ref[:, (2 * b) * 128:(2 * b + 1) * 128] = jnp.where(left, blk, 0.0).astype(BF16)
            ref[:, (2 * b + 1) * 128:(2 * b + 2) * 128] = jnp.where(left, 0.0, blk).astype(BF16)

    store_split(q_ref, _rotary(proj(0, RET_W), cos, s_lo, s_hi))
    k = _rotary(proj(RET_W, 2 * RET_W), cos, s_lo, s_hi) * QK_SCALE
    store_split(k_ref, k)
    store_split(kz_ref, k * zeta_ref[...])
    store_split(v_ref, proj(2 * RET_W, 3 * RET_W))
    g_ref[...] = proj(3 * RET_W, 4 * RET_W)
    sq = proj(4 * RET_W, 4 * RET_W + SWA_QW) * QK_SCALE
    for hh in range(N_SWA_HEADS):
        kv, j = divmod(hh, GQA_GROUP)
        blk = sq[:, (hh // 2) * 128:(hh // 2 + 1) * 128]
        if hh % 2 != kv:
            blk = pltpu.roll(blk, HEAD_DIM, 1)
        keep = left if kv == 0 else jnp.logical_not(left)
        sq_ref[:, (2 * j + kv) * 128:(2 * j + kv + 1) * 128] = jnp.where(keep, blk, 0.0).astype(BF16)
    skv_ref[...] = proj(4 * RET_W + SWA_QW, PROJ_W)


def _inproj(x, mod, layer, mod_row0, mod_rows, w_in_bf, rope, zeta_tile, tm):
    rows = x.shape[0]
    mblk = 8 if mod_rows == 1 else mod_rows
    mrow = mod_row0 // mblk
    row_spec = lambda w: pl.BlockSpec((tm, w), lambda i: (i, 0))
    mod_spec = lambda c: pl.BlockSpec((1, mblk, D_MODEL), lambda i: (layer, mrow, c))
    return pl.pallas_call(
        functools.partial(_inproj_kernel, mod_rows=mod_rows),
        grid=(rows // tm,),
        in_specs=[row_spec(D_MODEL), mod_spec(0), mod_spec(1),
                  pl.BlockSpec((D_MODEL, PROJ_W), lambda i: (0, 0)),
                  row_spec(HEAD_DIM),
                  pl.BlockSpec((tm, RET_W), lambda i: (0, 0))],
        out_specs=[row_spec(SPLIT_W), row_spec(SPLIT_W), row_spec(SPLIT_W), row_spec(SPLIT_W),
                   row_spec(RET_W), row_spec(SPLIT_W), row_spec(2 * SWA_KVW)],
        out_shape=[jax.ShapeDtypeStruct((rows, SPLIT_W), BF16),
                   jax.ShapeDtypeStruct((rows, SPLIT_W), BF16),
                   jax.ShapeDtypeStruct((rows, SPLIT_W), BF16),
                   jax.ShapeDtypeStruct((rows, SPLIT_W), BF16),
                   jax.ShapeDtypeStruct((rows, RET_W), F32),
                   jax.ShapeDtypeStruct((rows, SPLIT_W), BF16),
                   jax.ShapeDtypeStruct((rows, 2 * SWA_KVW), F32)],
        compiler_params=_cparams(("arbitrary",)),
        name="inproj",
    )(x, mod, mod, w_in_bf, rope, zeta_tile)


def _rope_tables(pos):
    half = HEAD_DIM // 2
    inv = ROPE_BASE ** (-jnp.arange(half, dtype=F32) / half)
    ang = pos.astype(F32)[:, None] * inv[None, :]
    cos, sin = jnp.cos(ang), jnp.sin(ang)
    return jnp.concatenate([cos, sin], axis=-1)


def _mix_prompt_kernel(gl_ref, sink_ref,
                       q_ref, k_ref, kz_ref, v_ref, g_ref, sq_ref, kvc_ref, kvp_ref,
                       decay_ref, xi_ref, bias_ref, gain_ref,
                       mixed_ref, st_ref, state):
    i = pl.program_id(0)

    @pl.when(i == 0)
    def _():
        state[...] = jnp.zeros_like(state)

    gain = gain_ref[...]
    left = lax.broadcasted_iota(I32, (CHUNK, 128), 1) < HEAD_DIM
    from_prev = (lax.broadcasted_iota(I32, (CHUNK, CHUNK), 1) >
                 lax.broadcasted_iota(I32, (CHUNK, CHUNK), 0))
    inv_dim = 1.0 / HEAD_DIM

    for sub in range(MIX_SUB):
        rows = slice(sub * CHUNK, (sub + 1) * CHUNK)

        heads = range(N_RET_HEADS)
        hsl = [slice(h * 128, (h + 1) * 128) for h in heads]
        scores_r = [lax.dot_general(q_ref[rows, hsl[h]], k_ref[rows, hsl[h]], (((1,), (1,)), ((), ())),
                                    preferred_element_type=F32) for h in heads]
        s_olds = [state[h] for h in heads]
        inter = [jnp.dot(q_ref[rows, hsl[h]], s_olds[h].astype(BF16), preferred_element_type=F32)
                 for h in heads]
        upds = [lax.dot_general(kz_ref[rows, hsl[h]], v_ref[rows, hsl[h]], (((0,), (0,)), ((), ())),
                                preferred_element_type=F32) for h in heads]
        for h in heads:
            state[h] = gl_ref[h] * s_olds[h] + upds[h]
        probs_r = [(scores_r[h] * decay_ref[h]).astype(BF16) for h in heads]
        intra = [jnp.dot(probs_r[h], v_ref[rows, hsl[h]], preferred_element_type=F32) for h in heads]

        owns = [left if h % 2 == 0 else jnp.logical_not(left) for h in heads]
        outs = [intra[h] + inter[h] * xi_ref[:, hsl[h]] for h in heads]
        mus = [jnp.sum(outs[h], axis=-1, keepdims=True) * inv_dim for h in heads]
        cen = [jnp.where(owns[h], outs[h] - mus[h], 0.0) for h in heads]
        var = [jnp.sum(cen[h] * cen[h], axis=-1, keepdims=True) * inv_dim for h in heads]
        nrm = [cen[h] * lax.rsqrt(var[h] + LN_EPS) for h in heads]
        ret_blocks = [nrm[2 * b] + nrm[2 * b + 1] for b in range(N_RET_HEADS // 2)]
        ret_o = jnp.concatenate(ret_blocks, axis=-1) * gain * _silu(g_ref[rows, :])

        kvc = kvc_ref[rows, :]
        kvp = kvp_ref[...] if sub == 0 else kvc_ref[(sub - 1) * CHUNK:sub * CHUNK, :]
        kcat = jnp.concatenate([kvp[:, :SWA_KVW], kvc[:, :SWA_KVW]], axis=0)
        vcat = jnp.concatenate([kvp[:, SWA_KVW:], kvc[:, SWA_KVW:]], axis=0)
        left2 = jnp.concatenate([left, left], axis=0)
        table = jnp.minimum(i, 1) if sub == 0 else 1
        swa_blocks = [None] * GQA_GROUP
        kks, vvs = [], []
        for kv in range(N_KV_HEADS):
            own2 = left2 if kv == 0 else jnp.logical_not(left2)
            kks.append(jnp.where(own2, kcat, 0.0).astype(BF16))
            vvs.append(jnp.where(own2, vcat, 0.0).astype(BF16))
        scores = []
        for hh in range(N_SWA_HEADS):
            kv, j = divmod(hh, GQA_GROUP)
            blk = 2 * j + kv
            qh = sq_ref[rows, blk * 128:(blk + 1) * 128]
            scores.append(lax.dot_general(qh, kks[kv], (((1,), (1,)), ((), ())),
                                          preferred_element_type=F32))
        probs = []
        for hh in range(N_SWA_HEADS):
            s2 = scores[hh]
            s = jnp.where(from_prev, s2[:, :CHUNK], s2[:, CHUNK:]) + bias_ref[table, hh]
            sink = sink_ref[hh]
            m = jnp.maximum(jnp.max(s, axis=-1, keepdims=True), sink)
            p = jnp.exp(s - m)
            den = jnp.sum(p, axis=-1, keepdims=True) + jnp.exp(sink - m)
            pn = p / den
            p2 = jnp.concatenate([jnp.where(from_prev, pn, 0.0), jnp.where(from_prev, 0.0, pn)], axis=-1)
            probs.append(p2.astype(BF16))
        for hh in range(N_SWA_HEADS):
            kv, j = divmod(hh, GQA_GROUP)
            o = jnp.dot(probs[hh], vvs[kv], preferred_element_type=F32)
            swa_blocks[j] = o if kv == 0 else swa_blocks[j] + o
        swa_o = jnp.concatenate(swa_blocks, axis=-1)

        mixed_ref[rows, :] = jnp.concatenate([ret_o, swa_o], axis=-1).astype(BF16)

    @pl.when(i == pl.num_programs(0) - 1)
    def _():
        st_ref[...] = state[...]


def _mix_prompt(proj, ret_consts, bias_tabs, sinks, gain):
    q, k, kz, v, g, sq, skv = proj
    rows = q.shape[0]
    tm = MIX_SUB * CHUNK
    gl, decay, xi = ret_consts
    bias_fold = bias_tabs[0]
    row_spec = lambda w: pl.BlockSpec((tm, w), lambda i: (i, 0))
    const2 = lambda a: pl.BlockSpec(a.shape, lambda i: (0, 0))
    const3 = lambda a: pl.BlockSpec(a.shape, lambda i: (0, 0, 0))
    smem = pl.BlockSpec(memory_space=pltpu.SMEM)
    mixed, st_full = pl.pallas_call(
        _mix_prompt_kernel,
        grid=(rows // tm,),
        in_specs=[smem, smem,
                  row_spec(SPLIT_W), row_spec(SPLIT_W), row_spec(SPLIT_W), row_spec(SPLIT_W), row_spec(RET_W),
                  row_spec(SPLIT_W), row_spec(2 * SWA_KVW),
                  pl.BlockSpec((CHUNK, 2 * SWA_KVW), lambda i: (jnp.maximum(MIX_SUB * i - 1, 0), 0)),
                  const3(decay), const2(xi),
                  pl.BlockSpec(bias_fold.shape, lambda i: (0, 0, 0, 0)),
                  const2(gain)],
        out_specs=[row_spec(D_MODEL),
                   pl.BlockSpec((N_RET_HEADS, 128, 128), lambda i: (0, 0, 0))],
        out_shape=[jax.ShapeDtypeStruct((rows, D_MODEL), BF16),
                   jax.ShapeDtypeStruct((N_RET_HEADS, 128, 128), F32)],
        scratch_shapes=[pltpu.VMEM((N_RET_HEADS, 128, 128), F32)],
        compiler_params=_cparams(("arbitrary",)),
        name="mix_prompt",
    )(gl, sinks, q, k, kz, v, g, sq, skv, skv, decay, xi, bias_fold, gain)
    lo, hi = slice(0, HEAD_DIM), slice(HEAD_DIM, 128)
    st = jnp.stack([st_full[h, lo, lo] if h % 2 == 0 else st_full[h, hi, hi] for h in range(N_RET_HEADS)])
    return mixed, st


def _ret_consts():
    lg = jnp.log(1.0 - 2.0 ** (-5.0 - jnp.arange(N_RET_HEADS, dtype=F32)))
    idx = jnp.arange(CHUNK, dtype=F32)
    diff = idx[:, None] - idx[None, :]
    decay = jnp.where(diff >= 0, jnp.exp(jnp.maximum(diff, 0.0)[None] * lg[:, None, None]), 0.0)
    xi = jnp.exp((idx + 1.0)[:, None] * lg[None, :])
    zeta = jnp.exp((CHUNK - 1.0 - idx)[:, None] * lg[None, :])
    gl = jnp.exp(CHUNK * lg)
    return gl, decay, jnp.repeat(xi, 128, axis=1), jnp.repeat(zeta, HEAD_DIM, axis=1), jnp.exp(1.0 * lg)


SAMPLE_BB = 8


def _mix_sample_kernel(sink_ref, gam_ref,
                       qt_ref, k_ref, v3_ref, sq3_ref, knew_ref, vnew_ref,
                       st_ref, ck_ref, cv_ref, g3_ref, x_ref, gate_ref,
                       wout_ref, brow_ref, gain_ref, lng_ref, lnb_ref,
                       x1_ref, nst_ref, nk_ref, nv_ref, ret_scr, swa_scr):
    i = pl.program_id(0)
    row = lax.broadcasted_iota(I32, (N_SWA_HEADS, 2 * HEAD_DIM), 0)
    lane = lax.broadcasted_iota(I32, (N_SWA_HEADS, 2 * HEAD_DIM), 1)
    own_half = (row // GQA_GROUP) == (lane // HEAD_DIM)
    sink_col = jnp.concatenate(
        [jnp.full((1, 1), sink_ref[hh], F32) for hh in range(N_SWA_HEADS)], axis=0)
    brow = brow_ref[...]
    qt = qt_ref[0]
    gam = gam_ref[...]
    seqs = range(SAMPLE_BB)
    state_shape = (N_RET_HEADS, HEAD_DIM, HEAD_DIM)

    s_old = [st_ref[b] for b in seqs]
    own_cols = (lax.broadcasted_iota(I32, (N_RET_HEADS, RET_W), 1) // HEAD_DIM ==
                lax.broadcasted_iota(I32, (N_RET_HEADS, RET_W), 0))
    k_blk = [jnp.where(own_cols, k_ref[b:b + 1, :], 0.0).astype(BF16) for b in seqs]
    outer = [lax.dot_general(k_blk[b], v3_ref[b].astype(BF16), (((0,), (0,)), ((), ())),
                             preferred_element_type=F32) for b in seqs]
    s_new = [gam * s_old[b] + outer[b] for b in seqs]
    for b in seqs:
        nst_ref[b] = s_new[b]
    outs = [jnp.sum((qt[:, b:b + 1] * s_new[b]).reshape(state_shape), axis=1) for b in seqs]
    for b in seqs:
        ret_scr[i * SAMPLE_BB + b] = outs[b]

    kk = [jnp.concatenate([ck_ref[0, b, 1:, :], knew_ref[b:b + 1, :]], axis=0) for b in seqs]
    vv = [jnp.concatenate([cv_ref[0, b, 1:, :], vnew_ref[b:b + 1, :]], axis=0) for b in seqs]
    for b in seqs:
        nk_ref[b] = kk[b]
        nv_ref[b] = vv[b]
    qblk = [jnp.where(own_half, jnp.concatenate([sq3_ref[b], sq3_ref[b]], axis=-1), 0.0).astype(BF16)
            for b in seqs]
    s = [lax.dot_general(qblk[b], kk[b].astype(BF16), (((1,), (1,)), ((), ())),
                         preferred_element_type=F32) + brow for b in seqs]
    m = [jnp.maximum(jnp.max(s[b], axis=-1, keepdims=True), sink_col) for b in seqs]
    p = [jnp.exp(s[b] - m[b]) for b in seqs]
    den = [jnp.sum(p[b], axis=-1, keepdims=True) + jnp.exp(sink_col - m[b]) for b in seqs]
    pn = [(p[b] / den[b]).astype(BF16) for b in seqs]
    o = [jnp.dot(pn[b], vv[b].astype(BF16), preferred_element_type=F32) for b in seqs]
    for b in seqs:
        swa_scr[i * SAMPLE_BB + b] = jnp.where(own_half[:, :HEAD_DIM], o[b][:, :HEAD_DIM], o[b][:, HEAD_DIM:])

    @pl.when(i == pl.num_programs(0) - 1)
    def _():
        y = jnp.zeros(x_ref.shape, F32)
        for h in range(N_RET_HEADS):
            o = ret_scr[:, h, :]
            mu = jnp.mean(o, axis=-1, keepdims=True)
            c = o - mu
            var = jnp.mean(c * c, axis=-1, keepdims=True)
            r = c * lax.rsqrt(var + LN_EPS) * gain_ref[h:h + 1, :] * _silu(g3_ref[h])
            y = y + jnp.dot(r.astype(BF16), wout_ref[h * HEAD_DIM:(h + 1) * HEAD_DIM, :],
                            preferred_element_type=F32)
        for hh in range(N_SWA_HEADS):
            o = swa_scr[:, hh, :].astype(BF16)
            lo = RET_W + hh * HEAD_DIM
            y = y + jnp.dot(o, wout_ref[lo:lo + HEAD_DIM, :], preferred_element_type=F32)
        x1_ref[...] = _ln(DN_ALPHA * x_ref[...] + (1.0 + gate_ref[0]) * y, lng_ref[...], lnb_ref[...])


def _mix_sample(proj, x, mod, layer, w_out_bf, gamma1, bias_row, sinks, gain8, ln_g, ln_b,
                st, ck, cv):
    q, k, _, v, g, sq, skv = proj
    nb = x.shape[0]
    steps = nb // SAMPLE_BB

    def joined(a):
        a = a.astype(F32).reshape(nb, RET_W // 128, 2, 128)
        return (a[:, :, 0, :] + a[:, :, 1, :]).reshape(nb, RET_W)

    to_cols = lambda a: joined(a).reshape(steps, SAMPLE_BB, RET_W).transpose(0, 2, 1)
    qt = to_cols(q)
    v3 = joined(v).reshape(nb, N_RET_HEADS, HEAD_DIM)
    sq3 = joined(sq).reshape(nb, GQA_GROUP, N_KV_HEADS, HEAD_DIM).transpose(0, 2, 1, 3)
    sq3 = sq3.reshape(nb, N_SWA_HEADS, HEAD_DIM)
    g3 = g.reshape(nb, N_RET_HEADS, HEAD_DIM).transpose(1, 0, 2)
    knew, vnew = skv[:, :SWA_KVW], skv[:, SWA_KVW:]
    gam =jnp.broadcast_to(jnp.repeat(gamma1, HEAD_DIM)[:, None], (RET_W, HEAD_DIM))
    smem = pl.BlockSpec(memory_space=pltpu.SMEM)
    blk3 = lambda a, b, c: pl.BlockSpec((a, b, c), lambda i: (i, 0, 0))
    lay4 = lambda b, c: pl.BlockSpec((1, SAMPLE_BB, b, c), lambda i: (layer, i, 0, 0))
    const2 = lambda a: pl.BlockSpec(a.shape, lambda i: (0, 0))
    const3 = lambda a: pl.BlockSpec(a.shape, lambda i: (0, 0, 0))
    x1, nst, nk, nv = pl.pallas_call(
        _mix_sample_kernel,
        grid=(steps,),
        in_specs=[smem, const2(gam),
                  blk3(1, RET_W, SAMPLE_BB), pl.BlockSpec((SAMPLE_BB, RET_W), lambda i: (i, 0)),
                  blk3(SAMPLE_BB, N_RET_HEADS, HEAD_DIM), blk3(SAMPLE_BB, N_SWA_HEADS, HEAD_DIM),
                  pl.BlockSpec((SAMPLE_BB, SWA_KVW), lambda i: (i, 0)),
                  pl.BlockSpec((SAMPLE_BB, SWA_KVW), lambda i: (i, 0)),
                  blk3(SAMPLE_BB, RET_W, HEAD_DIM), lay4(CHUNK, SWA_KVW), lay4(CHUNK, SWA_KVW),
                  const3(g3), const2(x),
                  pl.BlockSpec((1, nb, D_MODEL), lambda i: (layer, 0, 2)),
                  const2(w_out_bf), const2(bias_row), const2(gain8), const2(ln_g), const2(ln_b)],
        out_specs=[const2(x), blk3(SAMPLE_BB, RET_W, HEAD_DIM),
                   blk3(SAMPLE_BB, CHUNK, SWA_KVW), blk3(SAMPLE_BB, CHUNK, SWA_KVW)],
        out_shape=[jax.ShapeDtypeStruct(x.shape, F32), jax.ShapeDtypeStruct(st.shape, F32),
                   jax.ShapeDtypeStruct(ck.shape[1:], F32), jax.ShapeDtypeStruct(cv.shape[1:], F32)],
        scratch_shapes=[pltpu.VMEM((nb, N_RET_HEADS, HEAD_DIM), F32),
                        pltpu.VMEM((nb, N_SWA_HEADS, HEAD_DIM), F32)],
        compiler_params=_cparams(("arbitrary",)),
        name="mix_sample",
    )(sinks, gam, qt, joined(k), v3, sq3, knew, vnew, st, ck, cv, g3, x, mod,
      w_out_bf, bias_row, gain8, ln_g, ln_b)
    cache_shape = (nb, CHUNK, N_KV_HEADS, HEAD_DIM)
    return (x1, nst.reshape(nb, N_RET_HEADS, HEAD_DIM, HEAD_DIM), nk.reshape(cache_shape),
            nv.reshape(cache_shape))


def _router_kernel(x_ref, sh_ref, sc_ref, rwt_ref, rb_ref, tri_ref,
                   oi_ref, of_ref, cnt_ref, run_ref, *, mod_rows):
    h2 = x_ref[...] * (1.0 + sc_ref[0][:mod_rows]) + sh_ref[0][:mod_rows]
    _route_rows(h2.astype(BF16), rwt_ref, rb_ref, tri_ref, oi_ref, of_ref, cnt_ref, run_ref)


def _post_mix_kernel(mixed_ref, x_ref, gate_ref, wout_ref, lng_ref, lnb_ref,
                     sh_ref, sc_ref, rwt_ref, rb_ref, tri_ref,
                     x1_ref, oi_ref, of_ref, cnt_ref, run_ref):
    y = jnp.dot(mixed_ref[...], wout_ref[...], preferred_element_type=F32)
    x1 = _ln(DN_ALPHA * x_ref[...] + (1.0 + gate_ref[0][:1]) * y, lng_ref[...], lnb_ref[...])
    x1_ref[...] = x1
    h2 = x1 * (1.0 + sc_ref[0][:1]) + sh_ref[0][:1]
    _route_rows(h2.astype(BF16), rwt_ref, rb_ref, tri_ref, oi_ref, of_ref, cnt_ref, run_ref)


def _route_rows(h2b, rwt_ref, rb_ref, tri_ref, oi_ref, of_ref, cnt_ref, run_ref):
    i = pl.program_id(0)

    @pl.when(i == 0)
    def _():
        run_ref[...] = jnp.zeros_like(run_ref)

    logits = lax.dot_general(rwt_ref[...].astype(BF16), h2b, (((1,), (1,)), ((), ())),
                             preferred_element_type=F32)
    aff = 1.0 / (1.0 + jnp.exp(-logits))
    sel = aff + rb_ref[...]
    s = [sel[e:e + 1, :] for e in range(N_EXPERTS)]
    a = [aff[e:e + 1, :] for e in range(N_EXPERTS)]

    def top2sum(v0, v1, v2, v3):
        hi01, lo01 = jnp.maximum(v0, v1), jnp.minimum(v0, v1)
        hi23, lo23 = jnp.maximum(v2, v3), jnp.minimum(v2, v3)
        return jnp.maximum(hi01, hi23) + jnp.maximum(jnp.minimum(hi01, hi23),
                                                     jnp.maximum(lo01, lo23))

    def argmax_first(vals):
        best, idx = vals[0], jnp.zeros(vals[0].shape, I32)
        for j in range(1, len(vals)):
            upd = vals[j] > best
            idx = jnp.where(upd, j, idx)
            best = jnp.where(upd, vals[j], best)
        return idx

    def pick(idx, vals):
        out = vals[-1]
        for j in range(len(vals) - 2, -1, -1):
            out = jnp.where(idx == j, vals[j], out)
        return out

    gi = argmax_first([top2sum(*s[4 * g:4 * g + 4]) for g in range(N_GROUPS)])
    sv = [pick(gi, [s[4 * g + j] for g in range(N_GROUPS)]) for j in range(EXPERTS_PER_GROUP)]
    av = [pick(gi, [a[4 * g + j] for g in range(N_GROUPS)]) for j in range(EXPERTS_PER_GROUP)]
    i1 = argmax_first(sv)
    i2 = argmax_first([jnp.where(i1 == j, -jnp.inf, sv[j]) for j in range(EXPERTS_PER_GROUP)])
    w1, w2 = pick(i1, av), pick(i2, av)
    wsum = w1 + w2
    w1, w2 = w1 / wsum, w2 / wsum
    lo, hi = jnp.minimum(i1, i2), jnp.maximum(i1, i2)
    w_lo = jnp.where(i1 < i2, w1, w2)
    w_hi = jnp.where(i1 < i2, w2, w1)
    pair = jnp.where(lo == 0, hi - 1, jnp.where(lo == 1, 6 - hi, 5))
    bin_id = gi * N_PAIRS + pair

    tm = bin_id.shape[1]
    onehot = lax.broadcasted_iota(I32, (BIN_ROWS, tm), 0) == bin_id
    oh_f = jnp.where(onehot, 1.0, 0.0)
    before = jnp.dot(oh_f.astype(BF16), tri_ref[...], preferred_element_type=F32)
    run = run_ref[...]
    run_t = jnp.concatenate([run] * (tm // 128), axis=-1)
    rank = jnp.sum(oh_f * (before + run_t), axis=0, keepdims=True)
    run_new = run + jnp.sum(oh_f, axis=1, keepdims=True)
    run_ref[...] = run_new
    cnt_ref[...] = run_new.astype(I32)

    zi = jnp.zeros_like(bin_id)
    oi_ref[0] = jnp.concatenate([gi * 4 + lo, gi * 4 + hi, bin_id, rank.astype(I32), zi, zi, zi, zi], axis=0)
    zf = jnp.zeros_like(w_lo)
    of_ref[0] = jnp.concatenate([w_lo, w_hi, zf, zf, zf, zf, zf, zf], axis=0)


def _router(x1, mod, layer, mod_row0, mod_rows, router_wt, router_b, tm):
    rows = x1.shape[0]
    nt = rows // tm
    mblk = 8 if mod_rows == 1 else mod_rows
    mrow = mod_row0 // mblk
    tri = jnp.asarray(np.triu(np.ones((tm, tm), np.float32), 1), BF16)
    rb = jnp.broadcast_to(router_b.astype(F32)[:, None], (N_EXPERTS, tm))
    mod_spec = lambda c: pl.BlockSpec((1, mblk, D_MODEL), lambda i: (layer, mrow, c))
    oi, of, cnt = pl.pallas_call(
        functools.partial(_router_kernel, mod_rows=mod_rows),
        grid=(nt,),
        in_specs=[pl.BlockSpec((tm, D_MODEL), lambda i: (i, 0)), mod_spec(3), mod_spec(4),
                  pl.BlockSpec((N_EXPERTS, D_MODEL), lambda i: (0, 0)),
                  pl.BlockSpec((N_EXPERTS, tm), lambda i: (0, 0)),
                  pl.BlockSpec((tm, tm), lambda i: (0, 0))],
        out_specs=[pl.BlockSpec((1, 8, tm), lambda i: (i, 0, 0)),
                   pl.BlockSpec((1, 8, tm), lambda i: (i, 0, 0)),
                   pl.BlockSpec((BIN_ROWS, 128), lambda i: (0, 0))],
        out_shape=[jax.ShapeDtypeStruct((nt, 8, tm), I32),
                   jax.ShapeDtypeStruct((nt, 8, tm), F32),
                   jax.ShapeDtypeStruct((BIN_ROWS, 128), I32)],
        scratch_shapes=[pltpu.VMEM((BIN_ROWS, 128), F32)],
        compiler_params=_cparams(("arbitrary",)),
        name="router",
    )(x1, mod, mod, router_wt, rb, tri)
    return _unpack_route(oi, of, cnt)


def _unpack_route(oi, of, cnt):
    flat = lambda a, r: a[:, r, :].reshape(-1)
    return (flat(oi, 0), flat(oi, 1), flat(oi, 2), flat(oi, 3), flat(of, 0), flat(of, 1),
            cnt[:N_BINS, 0])


POST_TM = 512


def _post_mix(mixed, x, mod, layer, mod_row0, w_out_bf, ln_g, ln_b, router_wt, router_b):
    rows = x.shape[0]
    tm = POST_TM
    nt = rows // tm
    mrow = mod_row0 // 8
    tri = jnp.asarray(np.triu(np.ones((tm, tm), np.float32), 1), BF16)
    rb = jnp.broadcast_to(router_b.astype(F32)[:, None], (N_EXPERTS, tm))
    mod_spec = lambda c: pl.BlockSpec((1, 8, D_MODEL), lambda i: (layer, mrow, c))
    row_spec = pl.BlockSpec((tm, D_MODEL), lambda i: (i, 0))
    route_spec = pl.BlockSpec((1, 8, tm), lambda i: (i, 0, 0))
    const2 = lambda a: pl.BlockSpec(a.shape, lambda i: (0, 0))
    x1, oi, of, cnt = pl.pallas_call(
        _post_mix_kernel,
        grid=(nt,),
        in_specs=[row_spec, row_spec, mod_spec(2), const2(w_out_bf), const2(ln_g), const2(ln_b),
                  mod_spec(3), mod_spec(4), const2(router_wt), const2(rb), const2(tri)],
        out_specs=[row_spec, route_spec, route_spec,
                   pl.BlockSpec((BIN_ROWS, 128), lambda i: (0, 0))],
        out_shape=[jax.ShapeDtypeStruct((rows, D_MODEL), F32),
                   jax.ShapeDtypeStruct((nt, 8, tm), I32),
                   jax.ShapeDtypeStruct((nt, 8, tm), F32),
                   jax.ShapeDtypeStruct((BIN_ROWS, 128), I32)],
        scratch_shapes=[pltpu.VMEM((BIN_ROWS, 128), F32)],
        compiler_params=_cparams(("arbitrary",)),
        name="post_mix",
    )(mixed, x, mod, w_out_bf, ln_g, ln_b, mod, mod, router_wt, rb, tri)
    return x1, _unpack_route(oi, of, cnt)


DMA_UNROLL = 8
ROW_GROUPS = MOE_TM // DMA_UNROLL


GROUPS_PER_ITER = 2


def _for_rows(n, fn):
    per_iter = GROUPS_PER_ITER * DMA_UNROLL

    def body_u(jj, carry):
        for g in range(GROUPS_PER_ITER):
            for u in range(DMA_UNROLL):
                fn(jj * GROUPS_PER_ITER + g, u)
        return carry
    lax.fori_loop(0, n // per_iter, body_u, 0)

    def body_1(r, carry):
        fn(r // DMA_UNROLL, r % DMA_UNROLL)
        return carry
    lax.fori_loop((n // per_iter) * per_iter, n, body_1, 0)


def _wait_rows(n, buf, sem):
    p = ROW_GROUPS
    while p >= 1:
        @pl.when((n & (p * DMA_UNROLL)) != 0)
        def _(p=p):
            pltpu.make_async_copy(buf.at[pl.ds(0, p)], buf.at[pl.ds(0, p)], sem).wait()
        p //= 2
    p = DMA_UNROLL // 2
    while p >= 1:
        @pl.when((n & p) != 0)
        def _(p=p):
            pltpu.make_async_copy(buf.at[0, pl.ds(0, p)], buf.at[0, pl.ds(0, p)], sem).wait()
        p //= 2


def _moe_sorted_kernel(ea_ref, eb_ref, nvalid_ref, pos_ref,
                       x_hbm, sh_ref, sc_ref, gate_ref, rw_ref,
                       wga_ref, wua_ref, wda_ref, wgb_ref, wub_ref, wdb_ref,
                       lng_ref, lnb_ref,
                       out_hbm,
                       inv, xbuf, obuf, wg_a, wu_a, wd_a, wg_b, wu_b, wd_b, gsem, ssem):
    i = pl.program_id(0)
    nt = pl.num_programs(0)
    slot = i % 2
    n_tok = pos_ref.shape[0]

    def start_gather(t, dst_slot):
        def one(j, u):
            tok = inv[t * MOE_TM + j * DMA_UNROLL + u]
            pltpu.make_async_copy(x_hbm.at[pl.ds(tok, 1)], xbuf.at[dst_slot, j, pl.ds(u, 1)],
                                  gsem.at[dst_slot]).start()
        _for_rows(nvalid_ref[t], one)

    @pl.when(i == 0)
    def _():
        def body(j, carry):
            for u in range(DMA_UNROLL):
                t = j * DMA_UNROLL + u
                inv[pos_ref[t]] = t
            return carry
        lax.fori_loop(0, n_tok // DMA_UNROLL, body, 0)
        xbuf[...] = jnp.zeros_like(xbuf)
        start_gather(0, 0)

    @pl.when(i >= 2)
    def _():
        _wait_rows(nvalid_ref[jnp.maximum(i - 2, 0)], obuf.at[slot], ssem.at[slot])

    @pl.when(nvalid_ref[i] > 0)
    def _():
        _wait_rows(nvalid_ref[i], xbuf.at[slot], gsem.at[slot])

        @pl.when(i + 1 < nt)
        def _():
            start_gather(jnp.minimum(i + 1, nt - 1), 1 - slot)

        @pl.when(jnp.logical_or(i == 0, ea_ref[i] != ea_ref[jnp.maximum(i - 1, 0)]))
        def _():
            wg_a[...] = wga_ref[0, 0].astype(BF16)
            wu_a[...] = wua_ref[0, 0].astype(BF16)
            wd_a[...] = wda_ref[0, 0].astype(BF16)

        @pl.when(jnp.logical_or(i == 0, eb_ref[i] != eb_ref[jnp.maximum(i - 1, 0)]))
        def _():
            wg_b[...] = wgb_ref[0, 0].astype(BF16)
            wu_b[...] = wub_ref[0, 0].astype(BF16)
            wd_b[...] = wdb_ref[0, 0].astype(BF16)

        x = xbuf[slot].reshape(MOE_TM, D_MODEL)
        h2 = (x * (1.0 + sc_ref[0][:1]) + sh_ref[0][:1]).astype(BF16)

        dot = functools.partial(jnp.dot, preferred_element_type=F32)
        logits = dot(h2, rw_ref[...])
        gate_a, gate_b = dot(h2, wg_a[...]), dot(h2, wg_b[...])
        up_a, up_b = dot(h2, wu_a[...]), dot(h2, wu_b[...])
        act_a = (_silu(gate_a) * up_a).astype(BF16)
        act_b = (_silu(gate_b) * up_b).astype(BF16)
        y_a, y_b = dot(act_a, wd_a[...]), dot(act_b, wd_b[...])

        aff = 1.0 / (1.0 + jnp.exp(-logits))
        lane = lax.broadcasted_iota(I32, aff.shape, 1)
        a_lo = jnp.sum(jnp.where(lane == ea_ref[i], aff, 0.0), axis=-1, keepdims=True)
        a_hi = jnp.sum(jnp.where(lane == eb_ref[i], aff, 0.0), axis=-1, keepdims=True)
        a_sum = a_lo + a_hi
        y = (a_lo / a_sum) * y_a + (a_hi / a_sum) * y_b
        out = _ln(DN_ALPHA * x + (1.0 + gate_ref[0][:1]) * y, lng_ref[...], lnb_ref[...])
        obuf[slot] = out.reshape(ROW_GROUPS, DMA_UNROLL, D_MODEL)

        def scatter_one(j, u):
            tok = inv[i * MOE_TM + j * DMA_UNROLL + u]
            pltpu.make_async_copy(obuf.at[slot, j, pl.ds(u, 1)], out_hbm.at[pl.ds(tok, 1)],
                                  ssem.at[slot]).start()
        _for_rows(nvalid_ref[i], scatter_one)

    @pl.when(i == nt - 1)
    def _():
        _wait_rows(nvalid_ref[jnp.maximum(i - 1, 0)], obuf.at[1 - slot], ssem.at[1 - slot])
        _wait_rows(nvalid_ref[i], obuf.at[slot], ssem.at[slot])


def _moe_sorted(x1, mod, layer, mod_row0, route, rw_pad, wg, wu, wd, ln_g, ln_b):
    _, _, bin_id, rank, _, _, counts = route
    rows = x1.shape[0]
    nt = rows // MOE_TM + N_BINS
    ns = nt * MOE_TM
    tiles = (counts + MOE_TM - 1) // MOE_TM
    tile_end = jnp.cumsum(tiles)
    row_start = ((tile_end - tiles) * MOE_TM).astype(I32)
    pos = jnp.sum(jnp.where(bin_id[:, None] == jnp.arange(N_BINS, dtype=I32)[None, :],
                            row_start[None, :], 0), axis=1) + rank
    total = tile_end[-1]
    t = jnp.arange(nt, dtype=I32)
    tile_bin = jnp.sum(jnp.minimum(t, total - 1)[:, None] >= tile_end[None, :], axis=1).astype(I32)
    tile_bin = jnp.clip(tile_bin, 0, N_BINS - 1)
    bin_lo = jnp.asarray([g * 4 + _PAIR_LO[p] for g in range(N_GROUPS) for p in range(N_PAIRS)], I32)
    bin_hi = jnp.asarray([g * 4 + _PAIR_HI[p] for g in range(N_GROUPS) for p in range(N_PAIRS)], I32)
    tile_ea, tile_eb = bin_lo[tile_bin], bin_hi[tile_bin]
    nvalid = jnp.clip(counts[tile_bin] - (t - (tile_end - tiles)[tile_bin]) * MOE_TM, 0, MOE_TM)
    nvalid = jnp.where(t < total, nvalid, 0).astype(I32)
    mrow = mod_row0 // 8
    mod_spec = lambda c: pl.BlockSpec((1, 8, D_MODEL), lambda i, *_: (layer, mrow, c))
    w_spec = lambda shape, ref_idx: pl.BlockSpec(
        (1, 1) + shape, lambda i, ea_r, eb_r, *_: (layer, (ea_r, eb_r)[ref_idx][i], 0, 0))
    any_spec = pl.BlockSpec(memory_space=pl.ANY)
    w_bf = lambda shape: pltpu.VMEM(shape, BF16)
    grid_spec = pltpu.PrefetchScalarGridSpec(
        num_scalar_prefetch=4,
        grid=(nt,),
        in_specs=[any_spec,
                  mod_spec(3), mod_spec(4), mod_spec(5),
                  pl.BlockSpec((D_MODEL, 128), lambda i, *_: (0, 0)),
                  w_spec((D_MODEL, D_FF), 0), w_spec((D_MODEL, D_FF), 0), w_spec((D_FF, D_MODEL), 0),
                  w_spec((D_MODEL, D_FF), 1), w_spec((D_MODEL, D_FF), 1), w_spec((D_FF, D_MODEL), 1),
                  pl.BlockSpec((1, D_MODEL), lambda i, *_: (0, 0)),
                  pl.BlockSpec((1, D_MODEL), lambda i, *_: (0, 0))],
        out_specs=any_spec,
        scratch_shapes=[pltpu.SMEM((ns,), I32),
                        pltpu.VMEM((2, ROW_GROUPS, DMA_UNROLL, D_MODEL), F32),
                        pltpu.VMEM((2, ROW_GROUPS, DMA_UNROLL, D_MODEL), F32),
                        w_bf((D_MODEL, D_FF)), w_bf((D_MODEL, D_FF)), w_bf((D_FF, D_MODEL)),
                        w_bf((D_MODEL, D_FF)), w_bf((D_MODEL, D_FF)), w_bf((D_FF, D_MODEL)),
                        pltpu.SemaphoreType.DMA((2,)),
                        pltpu.SemaphoreType.DMA((2,))],
    )
    return pl.pallas_call(
        _moe_sorted_kernel,
        grid_spec=grid_spec,
        out_shape=jax.ShapeDtypeStruct((rows, D_MODEL), F32),
        compiler_params=_cparams(("arbitrary",)),
        name="moe_sorted",
    )(tile_ea, tile_eb, nvalid, pos, x1, mod, mod, mod, rw_pad,
      wg, wu, wd, wg, wu, wd, ln_g, ln_b)


def _moe_dense_kernel(x_ref, dw_ref, sh_ref, sc_ref, gate_ref, wg_ref, wu_ref, wd_ref,
                      lng_ref, lnb_ref, o_ref, acc):
    e = pl.program_id(0)

    @pl.when(e == 0)
    def _():
        acc[...] = jnp.zeros_like(acc)

    x = x_ref[...]
    h2 = (x * (1.0 + sc_ref[0]) + sh_ref[0]).astype(BF16)
    a = _silu(jnp.dot(h2, wg_ref[0, 0].astype(BF16), preferred_element_type=F32)) * \
        jnp.dot(h2, wu_ref[0, 0].astype(BF16), preferred_element_type=F32)
    y = jnp.dot(a.astype(BF16), wd_ref[0, 0].astype(BF16), preferred_element_type=F32)
    acc[...] = acc[...] + dw_ref[0][:, :1] * y

    @pl.when(e == pl.num_programs(0) - 1)
    def _():
        o_ref[...] = _ln(DN_ALPHA * x + (1.0 + gate_ref[0]) * acc[...], lng_ref[...], lnb_ref[...])


def _moe_dense(x1, mod, layer, route, wg, wu, wd, ln_g, ln_b):
    ea, eb, _, _, w_lo, w_hi, _ = route
    nb = x1.shape[0]
    eids = jnp.arange(N_EXPERTS, dtype=I32)[:, None]
    dw = jnp.where(eids == ea[None, :], w_lo[None, :], 0.0) + \
        jnp.where(eids == eb[None, :], w_hi[None, :], 0.0)
    dw = jnp.broadcast_to(dw[:, :, None], (N_EXPERTS, nb, 128))
    mod_spec = lambda c: pl.BlockSpec((1, nb, D_MODEL), lambda e: (layer, 0, c))
    const2 = lambda a: pl.BlockSpec(a.shape, lambda e: (0, 0))
    return pl.pallas_call(
        _moe_dense_kernel,
        grid=(N_EXPERTS,),
        in_specs=[const2(x1), pl.BlockSpec((1, nb, 128), lambda e: (e, 0, 0)),
                  mod_spec(3), mod_spec(4), mod_spec(5),
                  pl.BlockSpec((1, 1, D_MODEL, D_FF), lambda e: (layer, e, 0, 0)),
                  pl.BlockSpec((1, 1, D_MODEL, D_FF), lambda e: (layer, e, 0, 0)),
                  pl.BlockSpec((1, 1, D_FF, D_MODEL), lambda e: (layer, e, 0, 0)),
                  const2(ln_g), const2(ln_b)],
        out_specs=const2(x1),
        out_shape=jax.ShapeDtypeStruct(x1.shape, F32),
        scratch_shapes=[pltpu.VMEM(x1.shape, F32)],
        compiler_params=_cparams(("arbitrary",)),
        name="moe_dense",
    )(x1, dw, mod, mod, mod, wg, wu, wd, ln_g, ln_b)


def kernel(x_prompt, x_sample, state_ret, cache_swa_k, cache_swa_v, c_prompt, c_sample, w_in, w_out, ret_gn_gain, swa_sinks, rel_bias_table, ada_w, ada_b, ln1_g, ln1_b, ln2_g, ln2_b, router_w, router_b, exp_w_gate, exp_w_up, exp_w_down):
    seq = x_prompt.shape[1]
    nb = x_sample.shape[0]
    past_len = 16384
    assert x_prompt.shape[0] == 1 and x_sample.shape[1] == 1

    c_all = jnp.concatenate([c_sample, c_prompt, jnp.zeros((7, D_MODEL), F32)], axis=0)
    mod = _ada(c_all, ada_w, ada_b)
    prompt_row = nb

    bias_tabs = _bias_tables(rel_bias_table.astype(F32))
    gl, decay, xi, zeta, gamma1 = _ret_consts()
    rope_p = _rope_tables(jnp.arange(seq, dtype=I32))
    rope_s = _rope_tables(jnp.full((nb,), past_len, I32))
    router_wt = router_w.astype(F32).T
    rw_pad = jnp.pad(router_w.astype(BF16), ((0, 0), (0, 128 - N_EXPERTS)))
    zeta_p = jnp.tile(zeta, (INPROJ_TM // CHUNK, 1))
    zeta_s = jnp.ones((nb, RET_W), F32)

    ck_all = cache_swa_k.astype(F32).reshape(DEPTH, nb, CHUNK, SWA_KVW)
    cv_all = cache_swa_v.astype(F32).reshape(DEPTH, nb, CHUNK, SWA_KVW)

    xp = x_prompt.reshape(seq, D_MODEL)
    xs = x_sample.reshape(nb, D_MODEL)
    st_p, k_p, v_p, st_s, k_s, v_s = [], [], [], [], [], []
    for l in range(DEPTH):
        w_in_bf = w_in[l].astype(BF16)
        w_out_bf = w_out[l].astype(BF16)
        order = jnp.asarray(_SWA_HEAD_ORDER)
        swa_rows = w_out[l][RET_W:].reshape(N_SWA_HEADS, HEAD_DIM, D_MODEL)[order]
        w_out_prompt = jnp.concatenate([w_out[l][:RET_W], swa_rows.reshape(SWA_QW, D_MODEL)],
                                       axis=0).astype(BF16)
        experts = (exp_w_gate.astype(F32), exp_w_up.astype(F32), exp_w_down.astype(F32))
        gain = ret_gn_gain[l].astype(F32)
        sinks = swa_sinks[l].astype(F32)
        row = lambda a: a[l].astype(F32).reshape(1, D_MODEL)

        proj = _inproj(xp, mod, l, prompt_row, 1, w_in_bf, rope_p, zeta_p, INPROJ_TM)
        mixed, st = _mix_prompt(proj, (gl, decay, xi), bias_tabs, sinks, gain.reshape(1, RET_W))
        skv = proj[6]
        st_p.append(st.reshape(1, N_RET_HEADS, HEAD_DIM, HEAD_DIM))
        k_p.append(skv[seq - CHUNK:, :SWA_KVW].reshape(1, CHUNK, N_KV_HEADS, HEAD_DIM))
        v_p.append(skv[seq - CHUNK:, SWA_KVW:].reshape(1, CHUNK, N_KV_HEADS, HEAD_DIM))
        x1, route = _post_mix(mixed, xp, mod, l, prompt_row, w_out_prompt, row(ln1_g), row(ln1_b),
                              router_wt, router_b)
        xp = _moe_sorted(x1, mod, l, prompt_row, route, rw_pad, *experts, row(ln2_g), row(ln2_b))

        proj = _inproj(xs, mod, l, 0, nb, w_in_bf, rope_s, zeta_s, nb)
        x1, nst, nk, nv = _mix_sample(proj, xs, mod, l, w_out_bf, gamma1, bias_tabs[1], sinks, gain,
                                      row(ln1_g), row(ln1_b),
                                      state_ret[l].astype(F32).reshape(nb, RET_W, HEAD_DIM), ck_all, cv_all)
        st_s.append(nst)
        k_s.append(nk)
        v_s.append(nv)
        route = _router(x1, mod, l, 0, nb, router_wt, router_b, nb)
        xs = _moe_dense(x1, mod, l, route, *experts, row(ln2_g), row(ln2_b))

    return (xp.reshape(1, seq, D_MODEL), xs.reshape(nb, 1, D_MODEL),
            jnp.stack(st_p), jnp.stack(k_p), jnp.stack(v_p),
            jnp.stack(st_s), jnp.stack(k_s), jnp.stack(v_s))
```

```python
import functools
import math

import numpy as np
import jax
import jax.numpy as jnp
from jax import lax
from jax.experimental import pallas as pl
from jax.experimental.pallas import tpu as pltpu

F32 = jnp.float32
BF16 = jnp.bfloat16
I32 = jnp.int32

D_MODEL = 1024
DEPTH = 2
HEAD_DIM = 64
N_RET_HEADS = 8
N_SWA_HEADS = 8
N_KV_HEADS = 2
GQA_GROUP = N_SWA_HEADS // N_KV_HEADS
RET_W = N_RET_HEADS * HEAD_DIM
SWA_QW = N_SWA_HEADS * HEAD_DIM
SWA_KVW = N_KV_HEADS * HEAD_DIM
PROJ_W = 4 * RET_W + SWA_QW + 2 * SWA_KVW
CHUNK = 128
ROPE_BASE = 10000.0
N_BUCKETS = 32
MAX_DISTANCE = 128
N_EXPERTS = 16
N_GROUPS = 4
EXPERTS_PER_GROUP = 4
D_FF = 512
LN_EPS = 1e-5
DN_ALPHA = (2 * DEPTH) ** 0.25
QK_SCALE = HEAD_DIM ** -0.5
NEG_BIG = -1e30

N_PAIRS = 6
N_BINS = N_GROUPS * N_PAIRS
BIN_ROWS = 32
MOE_TM = 256
INPROJ_TM = 512
MIX_SUB = 4
SPLIT_W = 128 * N_RET_HEADS
_SWA_HEAD_ORDER = (0, 4, 1, 5, 2, 6, 3, 7)
VMEM_LIMIT = 56 * 1024 * 1024

_PAIR_LO = (0, 0, 0, 1, 1, 3)
_PAIR_HI = (1, 2, 3, 3, 2, 2)


def _cparams(sem):
    return pltpu.CompilerParams(dimension_semantics=sem, vmem_limit_bytes=VMEM_LIMIT)


def _ln(v, g, b):
    mu = jnp.mean(v, axis=-1, keepdims=True)
    c = v - mu
    var = jnp.mean(c * c, axis=-1, keepdims=True)
    return c * lax.rsqrt(var + LN_EPS) * g + b


def _silu(v):
    return v * (1.0 / (1.0 + jnp.exp(-v)))


def _ada_kernel(c_ref, w_ref, b_ref, o_ref):
    o_ref[0] = jnp.dot(c_ref[...].astype(BF16), w_ref[0].astype(BF16),
                       preferred_element_type=F32) + b_ref[0]


def _ada(c_all, ada_w, ada_b):
    rows = c_all.shape[0]
    nt = 6 * D_MODEL // 1024
    return pl.pallas_call(
        _ada_kernel,
        grid=(DEPTH, nt),
        in_specs=[pl.BlockSpec((rows, D_MODEL), lambda l, j: (0, 0)),
                  pl.BlockSpec((1, D_MODEL, 1024), lambda l, j: (l, 0, j)),
                  pl.BlockSpec((1, 1, 1024), lambda l, j: (l, 0, j))],
        out_specs=pl.BlockSpec((1, rows, 1024), lambda l, j: (l, 0, j)),
        out_shape=jax.ShapeDtypeStruct((DEPTH, rows, 6 * D_MODEL), F32),
        compiler_params=_cparams(("arbitrary", "arbitrary")),
        name="ada",
    )(c_all, ada_w, ada_b.reshape(DEPTH, 1, 6 * D_MODEL))


def _bias_kernel(tab_ref, bkt_ref, fold_ref, row_ref):
    bkt = bkt_ref[...]
    rows = lax.broadcasted_iota(I32, (CHUNK, CHUNK), 0)
    cols = lax.broadcasted_iota(I32, (CHUNK, CHUNK), 1)
    from_prev = cols > rows
    for h in range(N_SWA_HEADS):
        acc = jnp.zeros(bkt.shape, F32)
        for b in range(N_BUCKETS):
            acc = jnp.where(bkt == b, tab_ref[b, h], acc)
        own = acc[:, CHUNK:]
        fold_ref[0, h] = jnp.where(from_prev, NEG_BIG, own)
        fold_ref[1, h] = jnp.where(from_prev, acc[:, :CHUNK], own)
        row_ref[h:h + 1, :] = own[CHUNK - 1:CHUNK, :]


def _t5_bucket(rel):
    max_exact = N_BUCKETS // 2
    relf = jnp.maximum(rel, 1).astype(F32)
    large = max_exact + (jnp.log(relf / max_exact) / math.log(MAX_DISTANCE / max_exact)
                         * (N_BUCKETS - max_exact)).astype(I32)
    large = jnp.minimum(large, N_BUCKETS - 1)
    return jnp.where(rel < max_exact, rel, large)


def _bias_tables(rel_bias_table):
    qi = jnp.arange(CHUNK)
    si = jnp.arange(2 * CHUNK)
    rel = CHUNK + qi[:, None] - si[None, :]
    bkt = _t5_bucket(jnp.maximum(rel, 0)).astype(I32)
    return pl.pallas_call(
        _bias_kernel,
        in_specs=[pl.BlockSpec(memory_space=pltpu.SMEM),
                  pl.BlockSpec((CHUNK, 2 * CHUNK), lambda: (0, 0))],
        out_specs=[pl.BlockSpec((2, N_SWA_HEADS, CHUNK, CHUNK), lambda: (0, 0, 0, 0)),
                   pl.BlockSpec((N_SWA_HEADS, CHUNK), lambda: (0, 0))],
        out_shape=[jax.ShapeDtypeStruct((2, N_SWA_HEADS, CHUNK, CHUNK), F32),
                   jax.ShapeDtypeStruct((N_SWA_HEADS, CHUNK), F32)],
        name="t5_bias",
    )(rel_bias_table, bkt)


def _rotary(v, cos, s_lo, s_hi):
    outs = []
    for j in range(RET_W // 128):
        blk = v[:, j * 128:(j + 1) * 128]
        outs.append(blk * cos + pltpu.roll(blk, 96, 1) * s_lo + pltpu.roll(blk, 32, 1) * s_hi)
    return jnp.concatenate(outs, axis=-1)


def _inproj_kernel(x_ref, sh_ref, sc_ref, w_ref, cs_ref, zeta_ref,
                   q_ref, k_ref, kz_ref, v_ref, g_ref, sq_ref, skv_ref, *, mod_rows):
    sh = sh_ref[0][:mod_rows]
    sc = sc_ref[0][:mod_rows]
    h = (x_ref[...] * (1.0 + sc) + sh).astype(BF16)
    cs = jnp.concatenate([cs_ref[...], cs_ref[...]], axis=-1)
    first = (lax.broadcasted_iota(I32, cs.shape, 1) % HEAD_DIM) < (HEAD_DIM // 2)
    cos = jnp.where(first, cs, pltpu.roll(cs, 32, 1))
    s_lo = jnp.where(first, -pltpu.roll(cs, 96, 1), 0.0)
    s_hi = jnp.where(first, 0.0, cs)

    def proj(lo, hi):
        return jnp.dot(h, w_ref[:, lo:hi], preferred_element_type=F32)

    left = lax.broadcasted_iota(I32, cs.shape, 1) < HEAD_DIM

    def store_split(ref, val):
        for b in range(RET_W // 128):
            blk = val[:, b * 128:(b + 1) * 128]
            ref[:, (2 * b) * 128:(2 * b + 1) * 128] = jnp.where(left, blk, 0.0).astype(BF16)
            ref[:, (2 * b + 1) * 128:(2 * b + 2) * 128] = jnp.where(left, 0.0, blk).astype(BF16)

    store_split(q_ref, _rotary(proj(0, RET_W), cos, s_lo, s_hi))
    k = _rotary(proj(RET_W, 2 * RET_W), cos, s_lo, s_hi) * QK_SCALE
    store_split(k_ref, k)
    store_split(kz_ref, k * zeta_ref[...])
    store_split(v_ref, proj(2 * RET_W, 3 * RET_W))
    g_ref[...] = proj(3 * RET_W, 4 * RET_W)
    sq = proj(4 * RET_W, 4 * RET_W + SWA_QW) * QK_SCALE
    for hh in range(N_SWA_HEADS):
        kv, j = divmod(hh, GQA_GROUP)
        blk = sq[:, (hh // 2) * 128:(hh // 2 + 1) * 128]
        if hh % 2 != kv:
            blk = pltpu.roll(blk, HEAD_DIM, 1)
        keep = left if kv == 0 else jnp.logical_not(left)
        sq_ref[:, (2 * j + kv) * 128:(2 * j + kv + 1) * 128] = jnp.where(keep, blk, 0.0).astype(BF16)
    skv_ref[...] = proj(4 * RET_W + SWA_QW, PROJ_W)


def _inproj(x, mod, layer, mod_row0, mod_rows, w_in_bf, rope, zeta_tile, tm):
    rows = x.shape[0]
    mblk = 8 if mod_rows == 1 else mod_rows
    mrow = mod_row0 // mblk
    row_spec = lambda w: pl.BlockSpec((tm, w), lambda i: (i, 0))
    mod_spec = lambda c: pl.BlockSpec((1, mblk, D_MODEL), lambda i: (layer, mrow, c))
    return pl.pallas_call(
        functools.partial(_inproj_kernel, mod_rows=mod_rows),
        grid=(rows // tm,),
        in_specs=[row_spec(D_MODEL), mod_spec(0), mod_spec(1),
                  pl.BlockSpec((D_MODEL, PROJ_W), lambda i: (0, 0)),
                  row_spec(HEAD_DIM),
                  pl.BlockSpec((tm, RET_W), lambda i: (0, 0))],
        out_specs=[row_spec(SPLIT_W), row_spec(SPLIT_W), row_spec(SPLIT_W), row_spec(SPLIT_W),
                   row_spec(RET_W), row_spec(SPLIT_W), row_spec(2 * SWA_KVW)],
        out_shape=[jax.ShapeDtypeStruct((rows, SPLIT_W), BF16),
                   jax.ShapeDtypeStruct((rows, SPLIT_W), BF16),
                   jax.ShapeDtypeStruct((rows, SPLIT_W), BF16),
                   jax.ShapeDtypeStruct((rows, SPLIT_W), BF16),
                   jax.ShapeDtypeStruct((rows, RET_W), F32),
                   jax.ShapeDtypeStruct((rows, SPLIT_W), BF16),
                   jax.ShapeDtypeStruct((rows, 2 * SWA_KVW), F32)],
        compiler_params=_cparams(("arbitrary",)),
        name="inproj",
    )(x, mod, mod, w_in_bf, rope, zeta_tile)


def _rope_tables(pos):
    half = HEAD_DIM // 2
    inv = ROPE_BASE ** (-jnp.arange(half, dtype=F32) / half)
    ang = pos.astype(F32)[:, None] * inv[None, :]
    cos, sin = jnp.cos(ang), jnp.sin(ang)
    return jnp.concatenate([cos, sin], axis=-1)


def _mix_prompt_kernel(gl_ref, sink_ref,
                       q_ref, k_ref, kz_ref, v_ref, g_ref, sq_ref, kvc_ref, kvp_ref,
                       decay_ref, xi_ref, bias_ref, gain_ref,
                       mixed_ref, st_ref, state):
    i = pl.program_id(0)

    @pl.when(i == 0)
    def _():
        state[...] = jnp.zeros_like(state)

    gain = gain_ref[...]
    left = lax.broadcasted_iota(I32, (CHUNK, 128), 1) < HEAD_DIM
    from_prev = (lax.broadcasted_iota(I32, (CHUNK, CHUNK), 1) >
                 lax.broadcasted_iota(I32, (CHUNK, CHUNK), 0))
    inv_dim = 1.0 / HEAD_DIM

    for sub in range(MIX_SUB):
        rows = slice(sub * CHUNK, (sub + 1) * CHUNK)

        kvc = kvc_ref[rows, :]
        kvp = kvp_ref[...] if sub == 0 else kvc_ref[(sub - 1) * CHUNK:sub * CHUNK, :]
        kcat = jnp.concatenate([kvp[:, :SWA_KVW], kvc[:, :SWA_KVW]], axis=0)
        vcat = jnp.concatenate([kvp[:, SWA_KVW:], kvc[:, SWA_KVW:]], axis=0)
        left2 = jnp.concatenate([left, left], axis=0)
        table = jnp.minimum(i, 1) if sub == 0 else 1
        swa_blocks = [None] * GQA_GROUP
        kks, vvs = [], []
        for kv in range(N_KV_HEADS):
            own2 = left2 if kv == 0 else jnp.logical_not(left2)
            kks.append(jnp.where(own2, kcat, 0.0).astype(BF16))
            vvs.append(jnp.where(own2, vcat, 0.0).astype(BF16))
        scores = []
        for hh in range(N_SWA_HEADS):
            kv, j = divmod(hh, GQA_GROUP)
            blk = 2 * j + kv
            qh = sq_ref[rows, blk * 128:(blk + 1) * 128]
            scores.append(lax.dot_general(qh, kks[kv], (((1,), (1,)), ((), ())),
                                          preferred_element_type=F32))

        heads = range(N_RET_HEADS)
        hsl = [slice(h * 128, (h + 1) * 128) for h in heads]
        s_olds = [state[h] for h in heads]
        inter = [jnp.dot(q_ref[rows, hsl[h]], s_olds[h].astype(BF16), preferred_element_type=F32)
                 for h in heads]
        upds = [lax.dot_general(kz_ref[rows, hsl[h]], v_ref[rows, hsl[h]], (((0,), (0,)), ((), ())),
                                preferred_element_type=F32) for h in heads]
        for h in heads:
            state[h] = gl_ref[h] * s_olds[h] + upds[h]
        scores_r = [lax.dot_general(q_ref[rows, hsl[h]], k_ref[rows, hsl[h]], (((1,), (1,)), ((), ())),
                                    preferred_element_type=F32) for h in heads]
        probs_r = [(scores_r[h] * decay_ref[h]).astype(BF16) for h in heads]
        intra = [jnp.dot(probs_r[h], v_ref[rows, hsl[h]], preferred_element_type=F32) for h in heads]

        owns = [left if h % 2 == 0 else jnp.logical_not(left) for h in heads]
        outs = [intra[h] + inter[h] * xi_ref[:, hsl[h]] for h in heads]
        mus = [jnp.sum(outs[h], axis=-1, keepdims=True) * inv_dim for h in heads]
        cen = [jnp.where(owns[h], outs[h] - mus[h], 0.0) for h in heads]
        var = [jnp.sum(cen[h] * cen[h], axis=-1, keepdims=True) * inv_dim for h in heads]
        nrm = [cen[h] * lax.rsqrt(var[h] + LN_EPS) for h in heads]
        ret_blocks = [nrm[2 * b] + nrm[2 * b + 1] for b in range(N_RET_HEADS // 2)]
        ret_o = jnp.concatenate(ret_blocks, axis=-1) * gain * _silu(g_ref[rows, :])

        swa_heads = range(N_SWA_HEADS)
        ss = [jnp.where(from_prev, scores[hh][:, :CHUNK], scores[hh][:, CHUNK:]) + bias_ref[table, hh]
              for hh in swa_heads]
        ms = [jnp.maximum(jnp.max(ss[hh], axis=-1, keepdims=True), sink_ref[hh]) for hh in swa_heads]
        ps = [jnp.exp(ss[hh] - ms[hh]) for hh in swa_heads]
        dens = [jnp.sum(ps[hh], axis=-1, keepdims=True) + jnp.exp(sink_ref[hh] - ms[hh]) for hh in swa_heads]
        pns = [ps[hh] / dens[hh] for hh in swa_heads]
        probs = [jnp.concatenate([jnp.where(from_prev, pns[hh], 0.0), jnp.where(from_prev, 0.0, pns[hh])],
                                 axis=-1).astype(BF16) for hh in swa_heads]
        for hh in range(N_SWA_HEADS):
            kv, j = divmod(hh, GQA_GROUP)
            o = jnp.dot(probs[hh], vvs[kv], preferred_element_type=F32)
            swa_blocks[j] = o if kv == 0 else swa_blocks[j] + o
        swa_o = jnp.concatenate(swa_blocks, axis=-1)

        mixed_ref[rows, :] = jnp.concatenate([ret_o, swa_o], axis=-1).astype(BF16)

    @pl.when(i == pl.num_programs(0) - 1)
    def _():
        st_ref[...] = state[...]


def _mix_prompt(proj, ret_consts, bias_tabs, sinks, gain):
    q, k, kz, v, g, sq, skv = proj
    rows = q.shape[0]
    tm = MIX_SUB * CHUNK
    gl, decay, xi = ret_consts
    bias_fold = bias_tabs[0]
    row_spec = lambda w: pl.BlockSpec((tm, w), lambda i: (i, 0))
    const2 = lambda a: pl.BlockSpec(a.shape, lambda i: (0, 0))
    const3 = lambda a: pl.BlockSpec(a.shape, lambda i: (0, 0, 0))
    smem = pl.BlockSpec(memory_space=pltpu.SMEM)
    mixed, st_full = pl.pallas_call(
        _mix_prompt_kernel,
        grid=(rows // tm,),
        in_specs=[smem, smem,
                  row_spec(SPLIT_W), row_spec(SPLIT_W), row_spec(SPLIT_W), row_spec(SPLIT_W), row_spec(RET_W),
                  row_spec(SPLIT_W), row_spec(2 * SWA_KVW),
                  pl.BlockSpec((CHUNK, 2 * SWA_KVW), lambda i: (jnp.maximum(MIX_SUB * i - 1, 0), 0)),
                  const3(decay), const2(xi),
                  pl.BlockSpec(bias_fold.shape, lambda i: (0, 0, 0, 0)),
                  const2(gain)],
        out_specs=[row_spec(D_MODEL),
                   pl.BlockSpec((N_RET_HEADS, 128, 128), lambda i: (0, 0, 0))],
        out_shape=[jax.ShapeDtypeStruct((rows, D_MODEL), BF16),
                   jax.ShapeDtypeStruct((N_RET_HEADS, 128, 128), F32)],
        scratch_shapes=[pltpu.VMEM((N_RET_HEADS, 128, 128), F32)],
        compiler_params=_cparams(("arbitrary",)),
        name="mix_prompt",
    )(gl, sinks, q, k, kz, v, g, sq, skv, skv, decay, xi, bias_fold, gain)
    lo, hi = slice(0, HEAD_DIM), slice(HEAD_DIM, 128)
    st = jnp.stack([st_full[h, lo, lo] if h % 2 == 0 else st_full[h, hi, hi] for h in range(N_RET_HEADS)])
    return mixed, st


def _ret_consts():
    lg = jnp.log(1.0 - 2.0 ** (-5.0 - jnp.arange(N_RET_HEADS, dtype=F32)))
    idx = jnp.arange(CHUNK, dtype=F32)
    diff = idx[:, None] - idx[None, :]
    decay = jnp.where(diff >= 0, jnp.exp(jnp.maximum(diff, 0.0)[None] * lg[:, None, None]), 0.0)
    xi = jnp.exp((idx + 1.0)[:, None] * lg[None, :])
    zeta = jnp.exp((CHUNK - 1.0 - idx)[:, None] * lg[None, :])
    gl = jnp.exp(CHUNK * lg)
    return gl, decay, jnp.repeat(xi, 128, axis=1), jnp.repeat(zeta, HEAD_DIM, axis=1), jnp.exp(1.0 * lg)


SAMPLE_BB = 8


def _mix_sample_kernel(sink_ref, gam_ref,
                       qt_ref, k_ref, v3_ref, sq3_ref, knew_ref, vnew_ref,
                       st_ref, ck_ref, cv_ref, g3_ref, x_ref, gate_ref,
                       wout_ref, brow_ref, gain_ref, lng_ref, lnb_ref,
                       x1_ref, nst_ref, nk_ref, nv_ref, ret_scr, swa_scr):
    i = pl.program_id(0)
    row = lax.broadcasted_iota(I32, (N_SWA_HEADS, 2 * HEAD_DIM), 0)
    lane = lax.broadcasted_iota(I32, (N_SWA_HEADS, 2 * HEAD_DIM), 1)
    own_half = (row // GQA_GROUP) == (lane // HEAD_DIM)
    sink_col = jnp.concatenate(
        [jnp.full((1, 1), sink_ref[hh], F32) for hh in range(N_SWA_HEADS)], axis=0)
    brow = brow_ref[...]
    qt = qt_ref[0]
    gam = gam_ref[...]
    seqs = range(SAMPLE_BB)
    state_shape = (N_RET_HEADS, HEAD_DIM, HEAD_DIM)

    s_old = [st_ref[b] for b in seqs]
    own_cols = (lax.broadcasted_iota(I32, (N_RET_HEADS, RET_W), 1) // HEAD_DIM ==
                lax.broadcasted_iota(I32, (N_RET_HEADS, RET_W), 0))
    k_blk = [jnp.where(own_cols, k_ref[b:b + 1, :], 0.0).astype(BF16) for b in seqs]
    outer = [lax.dot_general(k_blk[b], v3_ref[b].astype(BF16), (((0,), (0,)), ((), ())),
                             preferred_element_type=F32) for b in seqs]
    s_new = [gam * s_old[b] + outer[b] for b in seqs]
    for b in seqs:
        nst_ref[b] = s_new[b]
    outs = [jnp.sum((qt[:, b:b + 1] * s_new[b]).reshape(state_shape), axis=1) for b in seqs]
    for b in seqs:
        ret_scr[i * SAMPLE_BB + b] = outs[b]

    kk = [jnp.concatenate([ck_ref[0, b, 1:, :], knew_ref[b:b + 1, :]], axis=0) for b in seqs]
    vv = [jnp.concatenate([cv_ref[0, b, 1:, :], vnew_ref[b:b + 1, :]], axis=0) for b in seqs]
    for b in seqs:
        nk_ref[b] = kk[b]
        nv_ref[b] = vv[b]
    qblk = [jnp.where(own_half, jnp.concatenate([sq3_ref[b], sq3_ref[b]], axis=-1), 0.0).astype(BF16)
            for b in seqs]
    s = [lax.dot_general(qblk[b], kk[b].astype(BF16), (((1,), (1,)), ((), ())),
                         preferred_element_type=F32) + brow for b in seqs]
    m = [jnp.maximum(jnp.max(s[b], axis=-1, keepdims=True), sink_col) for b in seqs]
    p = [jnp.exp(s[b] - m[b]) for b in seqs]
    den = [jnp.sum(p[b], axis=-1, keepdims=True) + jnp.exp(sink_col - m[b]) for b in seqs]
    pn = [(p[b] / den[b]).astype(BF16) for b in seqs]
    o = [jnp.dot(pn[b], vv[b].astype(BF16), preferred_element_type=F32) for b in seqs]
    for b in seqs:
        swa_scr[i * SAMPLE_BB + b] = jnp.where(own_half[:, :HEAD_DIM], o[b][:, :HEAD_DIM], o[b][:, HEAD_DIM:])

    @pl.when(i == pl.num_programs(0) - 1)
    def _():
        y = jnp.zeros(x_ref.shape, F32)
        for h in range(N_RET_HEADS):
            o = ret_scr[:, h, :]
            mu = jnp.mean(o, axis=-1, keepdims=True)
            c = o - mu
            var = jnp.mean(c * c, axis=-1, keepdims=True)
            r = c * lax.rsqrt(var + LN_EPS) * gain_ref[h:h + 1, :] * _silu(g3_ref[h])
            y = y + jnp.dot(r.astype(BF16), wout_ref[h * HEAD_DIM:(h + 1) * HEAD_DIM, :],
                            preferred_element_type=F32)
        for hh in range(N_SWA_HEADS):
            o = swa_scr[:, hh, :].astype(BF16)
            lo = RET_W + hh * HEAD_DIM
            y = y + jnp.dot(o, wout_ref[lo:lo + HEAD_DIM, :], preferred_element_type=F32)
        x1_ref[...] = _ln(DN_ALPHA * x_ref[...] + (1.0 + gate_ref[0]) * y, lng_ref[...], lnb_ref[...])


def _mix_sample(proj, x, mod, layer, w_out_bf, gamma1, bias_row, sinks, gain8, ln_g, ln_b,
                st, ck, cv):
    q, k, _, v, g, sq, skv = proj
    nb = x.shape[0]
    steps = nb // SAMPLE_BB

    def joined(a):
        a = a.astype(F32).reshape(nb, RET_W // 128, 2, 128)
        return (a[:, :, 0, :] + a[:, :, 1, :]).reshape(nb, RET_W)

    to_cols = lambda a: joined(a).reshape(steps, SAMPLE_BB, RET_W).transpose(0, 2, 1)
    qt = to_cols(q)
    v3 = joined(v).reshape(nb, N_RET_HEADS, HEAD_DIM)
    sq3 = joined(sq).reshape(nb, GQA_GROUP, N_KV_HEADS, HEAD_DIM).transpose(0, 2, 1, 3)
    sq3 = sq3.reshape(nb, N_SWA_HEADS, HEAD_DIM)
    g3 = g.reshape(nb, N_RET_HEADS, HEAD_DIM).transpose(1, 0, 2)
    knew, vnew = skv[:, :SWA_KVW], skv[:, SWA_KVW:]
    gam =jnp.broadcast_to(jnp.repeat(gamma1, HEAD_DIM)[:, None], (RET_W, HEAD_DIM))
    smem = pl.BlockSpec(memory_space=pltpu.SMEM)
    blk3 = lambda a, b, c: pl.BlockSpec((a, b, c), lambda i: (i, 0, 0))
    lay4 = lambda b, c: pl.BlockSpec((1, SAMPLE_BB, b, c), lambda i: (layer, i, 0, 0))
    const2 = lambda a: pl.BlockSpec(a.shape, lambda i: (0, 0))
    const3 = lambda a: pl.BlockSpec(a.shape, lambda i: (0, 0, 0))
    x1, nst, nk, nv = pl.pallas_call(
        _mix_sample_kernel,
        grid=(steps,),
        in_specs=[smem, const2(gam),
                  blk3(1, RET_W, SAMPLE_BB), pl.BlockSpec((SAMPLE_BB, RET_W), lambda i: (i, 0)),
                  blk3(SAMPLE_BB, N_RET_HEADS, HEAD_DIM), blk3(SAMPLE_BB, N_SWA_HEADS, HEAD_DIM),
                  pl.BlockSpec((SAMPLE_BB, SWA_KVW), lambda i: (i, 0)),
                  pl.BlockSpec((SAMPLE_BB, SWA_KVW), lambda i: (i, 0)),
                  blk3(SAMPLE_BB, RET_W, HEAD_DIM), lay4(CHUNK, SWA_KVW), lay4(CHUNK, SWA_KVW),
                  const3(g3), const2(x),
                  pl.BlockSpec((1, nb, D_MODEL), lambda i: (layer, 0, 2)),
                  const2(w_out_bf), const2(bias_row), const2(gain8), const2(ln_g), const2(ln_b)],
        out_specs=[const2(x), blk3(SAMPLE_BB, RET_W, HEAD_DIM),
                   blk3(SAMPLE_BB, CHUNK, SWA_KVW), blk3(SAMPLE_BB, CHUNK, SWA_KVW)],
        out_shape=[jax.ShapeDtypeStruct(x.shape, F32), jax.ShapeDtypeStruct(st.shape, F32),
                   jax.ShapeDtypeStruct(ck.shape[1:], F32), jax.ShapeDtypeStruct(cv.shape[1:], F32)],
        scratch_shapes=[pltpu.VMEM((nb, N_RET_HEADS, HEAD_DIM), F32),
                        pltpu.VMEM((nb, N_SWA_HEADS, HEAD_DIM), F32)],
        compiler_params=_cparams(("arbitrary",)),
        name="mix_sample",
    )(sinks, gam, qt, joined(k), v3, sq3, knew, vnew, st, ck, cv, g3, x, mod,
      w_out_bf, bias_row, gain8, ln_g, ln_b)
    cache_shape = (nb, CHUNK, N_KV_HEADS, HEAD_DIM)
    return (x1, nst.reshape(nb, N_RET_HEADS, HEAD_DIM, HEAD_DIM), nk.reshape(cache_shape),
            nv.reshape(cache_shape))


def _router_kernel(x_ref, sh_ref, sc_ref, rwt_ref, rb_ref, tri_ref,
                   oi_ref, of_ref, cnt_ref, run_ref, *, mod_rows):
    h2 = x_ref[...] * (1.0 + sc_ref[0][:mod_rows]) + sh_ref[0][:mod_rows]
    _route_rows(h2.astype(BF16), rwt_ref, rb_ref, tri_ref, oi_ref, of_ref, cnt_ref, run_ref)


def _post_mix_kernel(mixed_ref, x_ref, gate_ref, wout_ref, lng_ref, lnb_ref,
                     sh_ref, sc_ref, rwt_ref, rb_ref, tri_ref,
                     x1_ref, oi_ref, of_ref, cnt_ref, run_ref):
    y = jnp.dot(mixed_ref[...], wout_ref[...], preferred_element_type=F32)
    x1 = _ln(DN_ALPHA * x_ref[...] + (1.0 + gate_ref[0][:1]) * y, lng_ref[...], lnb_ref[...])
    x1_ref[...] = x1
    h2 = x1 * (1.0 + sc_ref[0][:1]) + sh_ref[0][:1]
    _route_rows(h2.astype(BF16), rwt_ref, rb_ref, tri_ref, oi_ref, of_ref, cnt_ref, run_ref)


def _route_rows(h2b, rwt_ref, rb_ref, tri_ref, oi_ref, of_ref, cnt_ref, run_ref):
    i = pl.program_id(0)

    @pl.when(i == 0)
    def _():
        run_ref[...] = jnp.zeros_like(run_ref)

    logits = lax.dot_general(rwt_ref[...].astype(BF16), h2b, (((1,), (1,)), ((), ())),
                             preferred_element_type=F32)
    aff = 1.0 / (1.0 + jnp.exp(-logits))
    sel = aff + rb_ref[...]
    s = [sel[e:e + 1, :] for e in range(N_EXPERTS)]
    a = [aff[e:e + 1, :] for e in range(N_EXPERTS)]

    def top2sum(v0, v1, v2, v3):
        hi01, lo01 = jnp.maximum(v0, v1), jnp.minimum(v0, v1)
        hi23, lo23 = jnp.maximum(v2, v3), jnp.minimum(v2, v3)
        return jnp.maximum(hi01, hi23) + jnp.maximum(jnp.minimum(hi01, hi23),
                                                     jnp.maximum(lo01, lo23))

    def argmax_first(vals):
        best, idx = vals[0], jnp.zeros(vals[0].shape, I32)
        for j in range(1, len(vals)):
            upd = vals[j] > best
            idx = jnp.where(upd, j, idx)
            best = jnp.where(upd, vals[j], best)
        return idx

    def pick(idx, vals):
        out = vals[-1]
        for j in range(len(vals) - 2, -1, -1):
            out = jnp.where(idx == j, vals[j], out)
        return out

    gi = argmax_first([top2sum(*s[4 * g:4 * g + 4]) for g in range(N_GROUPS)])
    sv = [pick(gi, [s[4 * g + j] for g in range(N_GROUPS)]) for j in range(EXPERTS_PER_GROUP)]
    av = [pick(gi, [a[4 * g + j] for g in range(N_GROUPS)]) for j in range(EXPERTS_PER_GROUP)]
    i1 = argmax_first(sv)
    i2 = argmax_first([jnp.where(i1 == j, -jnp.inf, sv[j]) for j in range(EXPERTS_PER_GROUP)])
    w1, w2 = pick(i1, av), pick(i2, av)
    wsum = w1 + w2
    w1, w2 = w1 / wsum, w2 / wsum
    lo, hi = jnp.minimum(i1, i2), jnp.maximum(i1, i2)
    w_lo = jnp.where(i1 < i2, w1, w2)
    w_hi = jnp.where(i1 < i2, w2, w1)
    pair = jnp.where(lo == 0, hi - 1, jnp.where(lo == 1, 6 - hi, 5))
    bin_id = gi * N_PAIRS + pair

    tm = bin_id.shape[1]
    onehot = lax.broadcasted_iota(I32, (BIN_ROWS, tm), 0) == bin_id
    oh_f = jnp.where(onehot, 1.0, 0.0)
    before = jnp.dot(oh_f.astype(BF16), tri_ref[...], preferred_element_type=F32)
    run = run_ref[...]
    run_t = jnp.concatenate([run] * (tm // 128), axis=-1)
    rank = jnp.sum(oh_f * (before + run_t), axis=0, keepdims=True)
    run_new = run + jnp.sum(oh_f, axis=1, keepdims=True)
    run_ref[...] = run_new
    cnt_ref[...] = run_new.astype(I32)

    zi = jnp.zeros_like(bin_id)
    oi_ref[0] = jnp.concatenate([gi * 4 + lo, gi * 4 + hi, bin_id, rank.astype(I32), zi, zi, zi, zi], axis=0)
    zf = jnp.zeros_like(w_lo)
    of_ref[0] = jnp.concatenate([w_lo, w_hi, zf, zf, zf, zf, zf, zf], axis=0)


def _router(x1, mod, layer, mod_row0, mod_rows, router_wt, router_b, tm):
    rows = x1.shape[0]
    nt = rows // tm
    mblk = 8 if mod_rows == 1 else mod_rows
    mrow = mod_row0 // mblk
    tri = jnp.asarray(np.triu(np.ones((tm, tm), np.float32), 1), BF16)
    rb = jnp.broadcast_to(router_b.astype(F32)[:, None], (N_EXPERTS, tm))
    mod_spec = lambda c: pl.BlockSpec((1, mblk, D_MODEL), lambda i: (layer, mrow, c))
    oi, of, cnt = pl.pallas_call(
        functools.partial(_router_kernel, mod_rows=mod_rows),
        grid=(nt,),
        in_specs=[pl.BlockSpec((tm, D_MODEL), lambda i: (i, 0)), mod_spec(3), mod_spec(4),
                  pl.BlockSpec((N_EXPERTS, D_MODEL), lambda i: (0, 0)),
                  pl.BlockSpec((N_EXPERTS, tm), lambda i: (0, 0)),
                  pl.BlockSpec((tm, tm), lambda i: (0, 0))],
        out_specs=[pl.BlockSpec((1, 8, tm), lambda i: (i, 0, 0)),
                   pl.BlockSpec((1, 8, tm), lambda i: (i, 0, 0)),
                   pl.BlockSpec((BIN_ROWS, 128), lambda i: (0, 0))],
        out_shape=[jax.ShapeDtypeStruct((nt, 8, tm), I32),
                   jax.ShapeDtypeStruct((nt, 8, tm), F32),
                   jax.ShapeDtypeStruct((BIN_ROWS, 128), I32)],
        scratch_shapes=[pltpu.VMEM((BIN_ROWS, 128), F32)],
        compiler_params=_cparams(("arbitrary",)),
        name="router",
    )(x1, mod, mod, router_wt, rb, tri)
    return _unpack_route(oi, of, cnt)


def _unpack_route(oi, of, cnt):
    flat = lambda a, r: a[:, r, :].reshape(-1)
    return (flat(oi, 0), flat(oi, 1), flat(oi, 2), flat(oi, 3), flat(of, 0), flat(of, 1),
            cnt[:N_BINS, 0])


POST_TM = 512


def _post_mix(mixed, x, mod, layer, mod_row0, w_out_bf, ln_g, ln_b, router_wt, router_b):
    rows = x.shape[0]
    tm = POST_TM
    nt = rows // tm
    mrow = mod_row0 // 8
    tri = jnp.asarray(np.triu(np.ones((tm, tm), np.float32), 1), BF16)
    rb = jnp.broadcast_to(router_b.astype(F32)[:, None], (N_EXPERTS, tm))
    mod_spec = lambda c: pl.BlockSpec((1, 8, D_MODEL), lambda i: (layer, mrow, c))
    row_spec = pl.BlockSpec((tm, D_MODEL), lambda i: (i, 0))
    route_spec = pl.BlockSpec((1, 8, tm), lambda i: (i, 0, 0))
    const2 = lambda a: pl.BlockSpec(a.shape, lambda i: (0, 0))
    x1, oi, of, cnt = pl.pallas_call(
        _post_mix_kernel,
        grid=(nt,),
        in_specs=[row_spec, row_spec, mod_spec(2), const2(w_out_bf), const2(ln_g), const2(ln_b),
                  mod_spec(3), mod_spec(4), const2(router_wt), const2(rb), const2(tri)],
        out_specs=[row_spec, route_spec, route_spec,
                   pl.BlockSpec((BIN_ROWS, 128), lambda i: (0, 0))],
        out_shape=[jax.ShapeDtypeStruct((rows, D_MODEL), F32),
                   jax.ShapeDtypeStruct((nt, 8, tm), I32),
                   jax.ShapeDtypeStruct((nt, 8, tm), F32),
                   jax.ShapeDtypeStruct((BIN_ROWS, 128), I32)],
        scratch_shapes=[pltpu.VMEM((BIN_ROWS, 128), F32)],
        compiler_params=_cparams(("arbitrary",)),
        name="post_mix",
    )(mixed, x, mod, w_out_bf, ln_g, ln_b, mod, mod, router_wt, rb, tri)
    return x1, _unpack_route(oi, of, cnt)


DMA_UNROLL = 8
ROW_GROUPS = MOE_TM // DMA_UNROLL


GROUPS_PER_ITER = 2


def _for_rows(n, fn):
    per_iter = GROUPS_PER_ITER * DMA_UNROLL

    def body_u(jj, carry):
        for g in range(GROUPS_PER_ITER):
            for u in range(DMA_UNROLL):
                fn(jj * GROUPS_PER_ITER + g, u)
        return carry
    lax.fori_loop(0, n // per_iter, body_u, 0)

    def body_1(r, carry):
        fn(r // DMA_UNROLL, r % DMA_UNROLL)
        return carry
    lax.fori_loop((n // per_iter) * per_iter, n, body_1, 0)


def _wait_rows(n, buf, sem):
    p = ROW_GROUPS
    while p >= 1:
        @pl.when((n & (p * DMA_UNROLL)) != 0)
        def _(p=p):
            pltpu.make_async_copy(buf.at[pl.ds(0, p)], buf.at[pl.ds(0, p)], sem).wait()
        p //= 2
    p = DMA_UNROLL // 2
    while p >= 1:
        @pl.when((n & p) != 0)
        def _(p=p):
            pltpu.make_async_copy(buf.at[0, pl.ds(0, p)], buf.at[0, pl.ds(0, p)], sem).wait()
        p //= 2


def _moe_sorted_kernel(ea_ref, eb_ref, nvalid_ref, pos_ref,
                       x_hbm, sh_ref, sc_ref, gate_ref, rw_ref,
                       wga_ref, wua_ref, wda_ref, wgb_ref, wub_ref, wdb_ref,
                       lng_ref, lnb_ref,
                       out_hbm,
                       inv, xbuf, obuf, wg_a, wu_a, wd_a, wg_b, wu_b, wd_b, gsem, ssem):
    i = pl.program_id(0)
    nt = pl.num_programs(0)
    slot = i % 2
    n_tok = pos_ref.shape[0]

    def start_gather(t, dst_slot):
        def one(j, u):
            tok = inv[t * MOE_TM + j * DMA_UNROLL + u]
            pltpu.make_async_copy(x_hbm.at[pl.ds(tok, 1)], xbuf.at[dst_slot, j, pl.ds(u, 1)],
                                  gsem.at[dst_slot]).start()
        _for_rows(nvalid_ref[t], one)

    @pl.when(i == 0)
    def _():
        def body(j, carry):
            for u in range(DMA_UNROLL):
                t = j * DMA_UNROLL + u
                inv[pos_ref[t]] = t
            return carry
        lax.fori_loop(0, n_tok // DMA_UNROLL, body, 0)
        xbuf[...] = jnp.zeros_like(xbuf)
        start_gather(0, 0)

    @pl.when(i >= 2)
    def _():
        _wait_rows(nvalid_ref[jnp.maximum(i - 2, 0)], obuf.at[slot], ssem.at[slot])

    @pl.when(nvalid_ref[i] > 0)
    def _():
        _wait_rows(nvalid_ref[i], xbuf.at[slot], gsem.at[slot])

        @pl.when(i + 1 < nt)
        def _():
            start_gather(jnp.minimum(i + 1, nt - 1), 1 - slot)

        @pl.when(jnp.logical_or(i == 0, ea_ref[i] != ea_ref[jnp.maximum(i - 1, 0)]))
        def _():
            wg_a[...] = wga_ref[0, 0].astype(BF16)
            wu_a[...] = wua_ref[0, 0].astype(BF16)
            wd_a[...] = wda_ref[0, 0].astype(BF16)

        @pl.when(jnp.logical_or(i == 0, eb_ref[i] != eb_ref[jnp.maximum(i - 1, 0)]))
        def _():
            wg_b[...] = wgb_ref[0, 0].astype(BF16)
            wu_b[...] = wub_ref[0, 0].astype(BF16)
            wd_b[...] = wdb_ref[0, 0].astype(BF16)

        x = xbuf[slot].reshape(MOE_TM, D_MODEL)
        h2 = (x * (1.0 + sc_ref[0][:1]) + sh_ref[0][:1]).astype(BF16)

        dot = functools.partial(jnp.dot, preferred_element_type=F32)
        logits = dot(h2, rw_ref[...])
        gate_a, gate_b = dot(h2, wg_a[...]), dot(h2, wg_b[...])
        up_a, up_b = dot(h2, wu_a[...]), dot(h2, wu_b[...])
        act_a = (_silu(gate_a) * up_a).astype(BF16)
        act_b = (_silu(gate_b) * up_b).astype(BF16)
        y_a, y_b = dot(act_a, wd_a[...]), dot(act_b, wd_b[...])

        aff = 1.0 / (1.0 + jnp.exp(-logits))
        lane = lax.broadcasted_iota(I32, aff.shape, 1)
        a_lo = jnp.sum(jnp.where(lane == ea_ref[i], aff, 0.0), axis=-1, keepdims=True)
        a_hi = jnp.sum(jnp.where(lane == eb_ref[i], aff, 0.0), axis=-1, keepdims=True)
        a_sum = a_lo + a_hi
        y = (a_lo / a_sum) * y_a + (a_hi / a_sum) * y_b
        out = _ln(DN_ALPHA * x + (1.0 + gate_ref[0][:1]) * y, lng_ref[...], lnb_ref[...])
        obuf[slot] = out.reshape(ROW_GROUPS, DMA_UNROLL, D_MODEL)

        def scatter_one(j, u):
            tok = inv[i * MOE_TM + j * DMA_UNROLL + u]
            pltpu.make_async_copy(obuf.at[slot, j, pl.ds(u, 1)], out_hbm.at[pl.ds(tok, 1)],
                                  ssem.at[slot]).start()
        _for_rows(nvalid_ref[i], scatter_one)

    @pl.when(i == nt - 1)
    def _():
        _wait_rows(nvalid_ref[jnp.maximum(i - 1, 0)], obuf.at[1 - slot], ssem.at[1 - slot])
        _wait_rows(nvalid_ref[i], obuf.at[slot], ssem.at[slot])


def _moe_sorted(x1, mod, layer, mod_row0, route, rw_pad, wg, wu, wd, ln_g, ln_b):
    _, _, bin_id, rank, _, _, counts = route
    rows = x1.shape[0]
    nt = rows // MOE_TM + N_BINS
    ns = nt * MOE_TM
    tiles = (counts + MOE_TM - 1) // MOE_TM
    tile_end = jnp.cumsum(tiles)
    row_start = ((tile_end - tiles) * MOE_TM).astype(I32)
    pos = jnp.sum(jnp.where(bin_id[:, None] == jnp.arange(N_BINS, dtype=I32)[None, :],
                            row_start[None, :], 0), axis=1) + rank
    total = tile_end[-1]
    t = jnp.arange(nt, dtype=I32)
    tile_bin = jnp.sum(jnp.minimum(t, total - 1)[:, None] >= tile_end[None, :], axis=1).astype(I32)
    tile_bin = jnp.clip(tile_bin, 0, N_BINS - 1)
    bin_lo = jnp.asarray([g * 4 + _PAIR_LO[p] for g in range(N_GROUPS) for p in range(N_PAIRS)], I32)
    bin_hi = jnp.asarray([g * 4 + _PAIR_HI[p] for g in range(N_GROUPS) for p in range(N_PAIRS)], I32)
    tile_ea, tile_eb = bin_lo[tile_bin], bin_hi[tile_bin]
    nvalid = jnp.clip(counts[tile_bin] - (t - (tile_end - tiles)[tile_bin]) * MOE_TM, 0, MOE_TM)
    nvalid = jnp.where(t < total, nvalid, 0).astype(I32)
    mrow = mod_row0 // 8
    mod_spec = lambda c: pl.BlockSpec((1, 8, D_MODEL), lambda i, *_: (layer, mrow, c))
    w_spec = lambda shape, ref_idx: pl.BlockSpec(
        (1, 1) + shape, lambda i, ea_r, eb_r, *_: (layer, (ea_r, eb_r)[ref_idx][i], 0, 0))
    any_spec = pl.BlockSpec(memory_space=pl.ANY)
    w_bf = lambda shape: pltpu.VMEM(shape, BF16)
    grid_spec = pltpu.PrefetchScalarGridSpec(
        num_scalar_prefetch=4,
        grid=(nt,),
        in_specs=[any_spec,
                  mod_spec(3), mod_spec(4), mod_spec(5),
                  pl.BlockSpec((D_MODEL, 128), lambda i, *_: (0, 0)),
                  w_spec((D_MODEL, D_FF), 0), w_spec((D_MODEL, D_FF), 0), w_spec((D_FF, D_MODEL), 0),
                  w_spec((D_MODEL, D_FF), 1), w_spec((D_MODEL, D_FF), 1), w_spec((D_FF, D_MODEL), 1),
                  pl.BlockSpec((1, D_MODEL), lambda i, *_: (0, 0)),
                  pl.BlockSpec((1, D_MODEL), lambda i, *_: (0, 0))],
        out_specs=any_spec,
        scratch_shapes=[pltpu.SMEM((ns,), I32),
                        pltpu.VMEM((2, ROW_GROUPS, DMA_UNROLL, D_MODEL), F32),
                        pltpu.VMEM((2, ROW_GROUPS, DMA_UNROLL, D_MODEL), F32),
                        w_bf((D_MODEL, D_FF)), w_bf((D_MODEL, D_FF)), w_bf((D_FF, D_MODEL)),
                        w_bf((D_MODEL, D_FF)), w_bf((D_MODEL, D_FF)), w_bf((D_FF, D_MODEL)),
                        pltpu.SemaphoreType.DMA((2,)),
                        pltpu.SemaphoreType.DMA((2,))],
    )
    return pl.pallas_call(
        _moe_sorted_kernel,
        grid_spec=grid_spec,
        out_shape=jax.ShapeDtypeStruct((rows, D_MODEL), F32),
        compiler_params=_cparams(("arbitrary",)),
        name="moe_sorted",
    )(tile_ea, tile_eb, nvalid, pos, x1, mod, mod, mod, rw_pad,
      wg, wu, wd, wg, wu, wd, ln_g, ln_b)


def _moe_dense_kernel(x_ref, dw_ref, sh_ref, sc_ref, gate_ref, wg_ref, wu_ref, wd_ref,
                      lng_ref, lnb_ref, o_ref, acc):
    e = pl.program_id(0)

    @pl.when(e == 0)
    def _():
        acc[...] = jnp.zeros_like(acc)

    x = x_ref[...]
    h2 = (x * (1.0 + sc_ref[0]) + sh_ref[0]).astype(BF16)
    a = _silu(jnp.dot(h2, wg_ref[0, 0].astype(BF16), preferred_element_type=F32)) * \
        jnp.dot(h2, wu_ref[0, 0].astype(BF16), preferred_element_type=F32)
    y = jnp.dot(a.astype(BF16), wd_ref[0, 0].astype(BF16), preferred_element_type=F32)
    acc[...] = acc[...] + dw_ref[0][:, :1] * y

    @pl.when(e == pl.num_programs(0) - 1)
    def _():
        o_ref[...] = _ln(DN_ALPHA * x + (1.0 + gate_ref[0]) * acc[...], lng_ref[...], lnb_ref[...])


def _moe_dense(x1, mod, layer, route, wg, wu, wd, ln_g, ln_b):
    ea, eb, _, _, w_lo, w_hi, _ = route
    nb = x1.shape[0]
    eids = jnp.arange(N_EXPERTS, dtype=I32)[:, None]
    dw = jnp.where(eids == ea[None, :], w_lo[None, :], 0.0) + \
        jnp.where(eids == eb[None, :], w_hi[None, :], 0.0)
    dw = jnp.broadcast_to(dw[:, :, None], (N_EXPERTS, nb, 128))
    mod_spec = lambda c: pl.BlockSpec((1, nb, D_MODEL), lambda e: (layer, 0, c))
    const2 = lambda a: pl.BlockSpec(a.shape, lambda e: (0, 0))
    return pl.pallas_call(
        _moe_dense_kernel,
        grid=(N_EXPERTS,),
        in_specs=[const2(x1), pl.BlockSpec((1, nb, 128), lambda e: (e, 0, 0)),
                  mod_spec(3), mod_spec(4), mod_spec(5),
                  pl.BlockSpec((1, 1, D_MODEL, D_FF), lambda e: (layer, e, 0, 0)),
                  pl.BlockSpec((1, 1, D_MODEL, D_FF), lambda e: (layer, e, 0, 0)),
                  pl.BlockSpec((1, 1, D_FF, D_MODEL), lambda e: (layer, e, 0, 0)),
                  const2(ln_g), const2(ln_b)],
        out_specs=const2(x1),
        out_shape=jax.ShapeDtypeStruct(x1.shape, F32),
        scratch_shapes=[pltpu.VMEM(x1.shape, F32)],
        compiler_params=_cparams(("arbitrary",)),
        name="moe_dense",
    )(x1, dw, mod, mod, mod, wg, wu, wd, ln_g, ln_b)


def kernel(x_prompt, x_sample, state_ret, cache_swa_k, cache_swa_v, c_prompt, c_sample, w_in, w_out, ret_gn_gain, swa_sinks, rel_bias_table, ada_w, ada_b, ln1_g, ln1_b, ln2_g, ln2_b, router_w, router_b, exp_w_gate, exp_w_up, exp_w_down):
    seq = x_prompt.shape[1]
    nb = x_sample.shape[0]
    past_len = 16384
    assert x_prompt.shape[0] == 1 and x_sample.shape[1] == 1

    c_all = jnp.concatenate([c_sample, c_prompt, jnp.zeros((7, D_MODEL), F32)], axis=0)
    mod = _ada(c_all, ada_w, ada_b)
    prompt_row = nb

    bias_tabs = _bias_tables(rel_bias_table.astype(F32))
    gl, decay, xi, zeta, gamma1 = _ret_consts()
    rope_p = _rope_tables(jnp.arange(seq, dtype=I32))
    rope_s = _rope_tables(jnp.full((nb,), past_len, I32))
    router_wt = router_w.astype(F32).T
    rw_pad = jnp.pad(router_w.astype(BF16), ((0, 0), (0, 128 - N_EXPERTS)))
    zeta_p = jnp.tile(zeta, (INPROJ_TM // CHUNK, 1))
    zeta_s = jnp.ones((nb, RET_W), F32)

    ck_all = cache_swa_k.astype(F32).reshape(DEPTH, nb, CHUNK, SWA_KVW)
    cv_all = cache_swa_v.astype(F32).reshape(DEPTH, nb, CHUNK, SWA_KVW)

    xp = x_prompt.reshape(seq, D_MODEL)
    xs = x_sample.reshape(nb, D_MODEL)
    st_p, k_p, v_p, st_s, k_s, v_s = [], [], [], [], [], []
    for l in range(DEPTH):
        w_in_bf = w_in[l].astype(BF16)
        w_out_bf = w_out[l].astype(BF16)
        order = jnp.asarray(_SWA_HEAD_ORDER)
        swa_rows = w_out[l][RET_W:].reshape(N_SWA_HEADS, HEAD_DIM, D_MODEL)[order]
        w_out_prompt = jnp.concatenate([w_out[l][:RET_W], swa_rows.reshape(SWA_QW, D_MODEL)],
                                       axis=0).astype(BF16)
        experts = (exp_w_gate.astype(F32), exp_w_up.astype(F32), exp_w_down.astype(F32))
        gain = ret_gn_gain[l].astype(F32)
        sinks = swa_sinks[l].astype(F32)
        row = lambda a: a[l].astype(F32).reshape(1, D_MODEL)

        proj = _inproj(xp, mod, l, prompt_row, 1, w_in_bf, rope_p, zeta_p, INPROJ_TM)
        mixed, st = _mix_prompt(proj, (gl, decay, xi), bias_tabs, sinks, gain.reshape(1, RET_W))
        skv = proj[6]
        st_p.append(st.reshape(1, N_RET_HEADS, HEAD_DIM, HEAD_DIM))
        k_p.append(skv[seq - CHUNK:, :SWA_KVW].reshape(1, CHUNK, N_KV_HEADS, HEAD_DIM))
        v_p.append(skv[seq - CHUNK:, SWA_KVW:].reshape(1, CHUNK, N_KV_HEADS, HEAD_DIM))
        x1, route = _post_mix(mixed, xp, mod, l, prompt_row, w_out_prompt, row(ln1_g), row(ln1_b),
                              router_wt, router_b)
        xp = _moe_sorted(x1, mod, l, prompt_row, route, rw_pad, *experts, row(ln2_g), row(ln2_b))

        proj = _inproj(xs, mod, l, 0, nb, w_in_bf, rope_s, zeta_s, nb)
        x1, nst, nk, nv = _mix_sample(proj, xs, mod, l, w_out_bf, gamma1, bias_tabs[1], sinks, gain,
                                      row(ln1_g), row(ln1_b),
                                      state_ret[l].astype(F32).reshape(nb, RET_W, HEAD_DIM), ck_all, cv_all)
        st_s.append(nst)
        k_s.append(nk)
        v_s.append(nv)
        route = _router(x1, mod, l, 0, nb, router_wt, router_b, nb)
        xs = _moe_dense(x1, mod, l, route, *experts, row(ln2_g), row(ln2_b))

    return (xp.reshape(1, seq, D_MODEL), xs.reshape(nb, 1, D_MODEL),
            jnp.stack(st_p), jnp.stack(k_p), jnp.stack(v_p),
            jnp.stack(st_s), jnp.stack(k_s), jnp.stack(v_s))
```

```python
import functools
import math

import numpy as np
import jax
import jax.numpy as jnp
from jax import lax
from jax.experimental import pallas as pl
from jax.experimental.pallas import tpu as pltpu

F32 = jnp.float32
BF16 = jnp.bfloat16
I32 = jnp.int32

D_MODEL = 1024
DEPTH = 2
HEAD_DIM = 64
N_RET_HEADS = 8
N_SWA_HEADS = 8
N_KV_HEADS = 2
GQA_GROUP = N_SWA_HEADS // N_KV_HEADS
RET_W = N_RET_HEADS * HEAD_DIM
SWA_QW = N_SWA_HEADS * HEAD_DIM
SWA_KVW = N_KV_HEADS * HEAD_DIM
PROJ_W = 4 * RET_W + SWA_QW + 2 * SWA_KVW
CHUNK = 128
ROPE_BASE = 10000.0
N_BUCKETS = 32
MAX_DISTANCE = 128
N_EXPERTS = 16
N_GROUPS = 4
EXPERTS_PER_GROUP = 4
D_FF = 512
LN_EPS = 1e-5
DN_ALPHA = (2 * DEPTH) ** 0.25
QK_SCALE = HEAD_DIM ** -0.5
NEG_BIG = -1e30

N_PAIRS = 6
N_BINS = N_GROUPS * N_PAIRS
BIN_ROWS = 32
MOE_TM = 256
INPROJ_TM = 512
MIX_SUB = 4
SPLIT_W = 128 * N_RET_HEADS
_SWA_HEAD_ORDER = (0, 4, 1, 5, 2, 6, 3, 7)
VMEM_LIMIT = 56 * 1024 * 1024

_PAIR_LO = (0, 0, 0, 1, 1, 3)
_PAIR_HI = (1, 2, 3, 3, 2, 2)


def _cparams(sem):
    return pltpu.CompilerParams(dimension_semantics=sem, vmem_limit_bytes=VMEM_LIMIT)


def _ln(v, g, b):
    mu = jnp.mean(v, axis=-1, keepdims=True)
    c = v - mu
    var = jnp.mean(c * c, axis=-1, keepdims=True)
    return c * lax.rsqrt(var + LN_EPS) * g + b


def _silu(v):
    return v * (1.0 / (1.0 + jnp.exp(-v)))


def _ada_kernel(c_ref, w_ref, b_ref, o_ref):
    o_ref[0] = jnp.dot(c_ref[...].astype(BF16), w_ref[0].astype(BF16),
                       preferred_element_type=F32) + b_ref[0]


def _ada(c_all, ada_w, ada_b):
    rows = c_all.shape[0]
    nt = 6 * D_MODEL // 1024
    return pl.pallas_call(
        _ada_kernel,
        grid=(DEPTH, nt),
        in_specs=[pl.BlockSpec((rows, D_MODEL), lambda l, j: (0, 0)),
                  pl.BlockSpec((1, D_MODEL, 1024), lambda l, j: (l, 0, j)),
                  pl.BlockSpec((1, 1, 1024), lambda l, j: (l, 0, j))],
        out_specs=pl.BlockSpec((1, rows, 1024), lambda l, j: (l, 0, j)),
        out_shape=jax.ShapeDtypeStruct((DEPTH, rows, 6 * D_MODEL), F32),
        compiler_params=_cparams(("arbitrary", "arbitrary")),
        name="ada",
    )(c_all, ada_w, ada_b.reshape(DEPTH, 1, 6 * D_MODEL))


def _bias_kernel(tab_ref, bkt_ref, fold_ref, row_ref):
    bkt = bkt_ref[...]
    rows = lax.broadcasted_iota(I32, (CHUNK, CHUNK), 0)
    cols = lax.broadcasted_iota(I32, (CHUNK, CHUNK), 1)
    from_prev = cols > rows
    for h in range(N_SWA_HEADS):
        acc = jnp.zeros(bkt.shape, F32)
        for b in range(N_BUCKETS):
            acc = jnp.where(bkt == b, tab_ref[b, h], acc)
        own = acc[:, CHUNK:]
        fold_ref[0, h] = jnp.where(from_prev, NEG_BIG, own)
        fold_ref[1, h] = jnp.where(from_prev, acc[:, :CHUNK], own)
        row_ref[h:h + 1, :] = own[CHUNK - 1:CHUNK, :]


def _t5_bucket(rel):
    max_exact = N_BUCKETS // 2
    relf = jnp.maximum(rel, 1).astype(F32)
    large = max_exact + (jnp.log(relf / max_exact) / math.log(MAX_DISTANCE / max_exact)
                         * (N_BUCKETS - max_exact)).astype(I32)
    large = jnp.minimum(large, N_BUCKETS - 1)
    return jnp.where(rel < max_exact, rel, large)


def _bias_tables(rel_bias_table):
    qi = jnp.arange(CHUNK)
    si = jnp.arange(2 * CHUNK)
    rel = CHUNK + qi[:, None] - si[None, :]
    bkt = _t5_bucket(jnp.maximum(rel, 0)).astype(I32)
    return pl.pallas_call(
        _bias_kernel,
        in_specs=[pl.BlockSpec(memory_space=pltpu.SMEM),
                  pl.BlockSpec((CHUNK, 2 * CHUNK), lambda: (0, 0))],
        out_specs=[pl.BlockSpec((2, N_SWA_HEADS, CHUNK, CHUNK), lambda: (0, 0, 0, 0)),
                   pl.BlockSpec((N_SWA_HEADS, CHUNK), lambda: (0, 0))],
        out_shape=[jax.ShapeDtypeStruct((2, N_SWA_HEADS, CHUNK, CHUNK), F32),
                   jax.ShapeDtypeStruct((N_SWA_HEADS, CHUNK), F32)],
        name="t5_bias",
    )(rel_bias_table, bkt)


def _rotary(v, cos, s_lo, s_hi):
    outs = []
    for j in range(RET_W // 128):
        blk = v[:, j * 128:(j + 1) * 128]
        outs.append(blk * cos + pltpu.roll(blk, 96, 1) * s_lo + pltpu.roll(blk, 32, 1) * s_hi)
    return jnp.concatenate(outs, axis=-1)


def _inproj_kernel(x_ref, sh_ref, sc_ref, w_ref, cs_ref, zeta_ref,
                   q_ref, k_ref, kz_ref, v_ref, g_ref, sq_ref, skv_ref, *, mod_rows):
    sh = sh_ref[0][:mod_rows]
    sc = sc_ref[0][:mod_rows]
    h = (x_ref[...] * (1.0 + sc) + sh).astype(BF16)
    cs = jnp.concatenate([cs_ref[...], cs_ref[...]], axis=-1)
    first = (lax.broadcasted_iota(I32, cs.shape, 1) % HEAD_DIM) < (HEAD_DIM // 2)
    cos = jnp.where(first, cs, pltpu.roll(cs, 32, 1))
    s_lo = jnp.where(first, -pltpu.roll(cs, 96, 1), 0.0)
    s_hi = jnp.where(first, 0.0, cs)

    def proj(lo, hi):
        return jnp.dot(h, w_ref[:, lo:hi], preferred_element_type=F32)

    left = lax.broadcasted_iota(I32, cs.shape, 1) < HEAD_DIM

    def store_split(ref, val):
        for b in range(RET_W // 128):
            blk = val[:, b * 128:(b + 1) * 128]
            ref[:, (2 * b) * 128:(2 * b + 1) * 128] = jnp.where(left, blk, 0.0).astype(BF16)
            ref[:, (2 * b + 1) * 128:(2 * b + 2) * 128] = jnp.where(left, 0.0, blk).astype(BF16)

    store_split(q_ref, _rotary(proj(0, RET_W), cos, s_lo, s_hi))
    k = _rotary(proj(RET_W, 2 * RET_W), cos, s_lo, s_hi) * QK_SCALE
    store_split(k_ref, k)
    store_split(kz_ref, k * zeta_ref[...])
    store_split(v_ref, proj(2 * RET_W, 3 * RET_W))
    g_ref[...] = proj(3 * RET_W, 4 * RET_W)
    sq = proj(4 * RET_W, 4 * RET_W + SWA_QW) * QK_SCALE
    for hh in range(N_SWA_HEADS):
        kv, j = divmod(hh, GQA_GROUP)
        blk = sq[:, (hh // 2) * 128:(hh // 2 + 1) * 128]
        if hh % 2 != kv:
            blk = pltpu.roll(blk, HEAD_DIM, 1)
        keep = left if kv == 0 else jnp.logical_not(left)
        sq_ref[:, (2 * j + kv) * 128:(2 * j + kv + 1) * 128] = jnp.where(keep, blk, 0.0).astype(BF16)
    skv_ref[...] = proj(4 * RET_W + SWA_QW, PROJ_W)


def _inproj(x, mod, layer, mod_row0, mod_rows, w_in_bf, rope, zeta_tile, tm):
    rows = x.shape[0]
    mblk = 8 if mod_rows == 1 else mod_rows
    mrow = mod_row0 // mblk
    row_spec = lambda w: pl.BlockSpec((tm, w), lambda i: (i, 0))
    mod_spec = lambda c: pl.BlockSpec((1, mblk, D_MODEL), lambda i: (layer, mrow, c))
    return pl.pallas_call(
        functools.partial(_inproj_kernel, mod_rows=mod_rows),
        grid=(rows // tm,),
        in_specs=[row_spec(D_MODEL), mod_spec(0), mod_spec(1),
                  pl.BlockSpec((D_MODEL, PROJ_W), lambda i: (0, 0)),
                  row_spec(HEAD_DIM),
                  pl.BlockSpec((tm, RET_W), lambda i: (0, 0))],
        out_specs=[row_spec(SPLIT_W), row_spec(SPLIT_W), row_spec(SPLIT_W), row_spec(SPLIT_W),
                   row_spec(RET_W), row_spec(SPLIT_W), row_spec(2 * SWA_KVW)],
        out_shape=[jax.ShapeDtypeStruct((rows, SPLIT_W), BF16),
                   jax.ShapeDtypeStruct((rows, SPLIT_W), BF16),
                   jax.ShapeDtypeStruct((rows, SPLIT_W), BF16),
                   jax.ShapeDtypeStruct((rows, SPLIT_W), BF16),
                   jax.ShapeDtypeStruct((rows, RET_W), F32),
                   jax.ShapeDtypeStruct((rows, SPLIT_W), BF16),
                   jax.ShapeDtypeStruct((rows, 2 * SWA_KVW), F32)],
        compiler_params=_cparams(("arbitrary",)),
        name="inproj",
    )(x, mod, mod, w_in_bf, rope, zeta_tile)


def _rope_tables(pos):
    half = HEAD_DIM // 2
    inv = ROPE_BASE ** (-jnp.arange(half, dtype=F32) / half)
    ang = pos.astype(F32)[:, None] * inv[None, :]
    cos, sin = jnp.cos(ang), jnp.sin(ang)
    return jnp.concatenate([cos, sin], axis=-1)


def _mix_prompt_kernel(gl_ref, sink_ref,
                       q_ref, k_ref, kz_ref, v_ref, g_ref, sq_ref, kvc_ref, kvp_ref,
                       decay_ref, xi_ref, bias_ref, gain_ref,
                       mixed_ref, st_ref, state):
    i = pl.program_id(0)

    @pl.when(i == 0)
    def _():
        state[...] = jnp.zeros_like(state)

    gain = gain_ref[...]
    left = lax.broadcasted_iota(I32, (CHUNK, 128), 1) < HEAD_DIM
    from_prev = (lax.broadcasted_iota(I32, (CHUNK, CHUNK), 1) >
                 lax.broadcasted_iota(I32, (CHUNK, CHUNK), 0))
    inv_dim = 1.0 / HEAD_DIM

    for sub in range(MIX_SUB):
        rows = slice(sub * CHUNK, (sub + 1) * CHUNK)

        kvc = kvc_ref[rows, :]
        kvp = kvp_ref[...] if sub == 0 else kvc_ref[(sub - 1) * CHUNK:sub * CHUNK, :]
        kcat = jnp.concatenate([kvp[:, :SWA_KVW], kvc[:, :SWA_KVW]], axis=0)
        vcat = jnp.concatenate([kvp[:, SWA_KVW:], kvc[:, SWA_KVW:]], axis=0)
        left2 = jnp.concatenate([left, left], axis=0)
        table = jnp.minimum(i, 1) if sub == 0 else 1
        swa_blocks = [None] * GQA_GROUP
        kks, vvs = [], []
        for kv in range(N_KV_HEADS):
            own2 = left2 if kv == 0 else jnp.logical_not(left2)
            kks.append(jnp.where(own2, kcat, 0.0).astype(BF16))
            vvs.append(jnp.where(own2, vcat, 0.0).astype(BF16))
        scores = []
        for hh in range(N_SWA_HEADS):
            kv, j = divmod(hh, GQA_GROUP)
            blk = 2 * j + kv
            qh = sq_ref[rows, blk * 128:(blk + 1) * 128]
            scores.append(lax.dot_general(qh, kks[kv], (((1,), (1,)), ((), ())),
                                          preferred_element_type=F32))

        heads = range(N_RET_HEADS)
        hsl = [slice(h * 128, (h + 1) * 128) for h in heads]
        s_olds = [state[h] for h in heads]
        upds = [lax.dot_general(kz_ref[rows, hsl[h]], v_ref[rows, hsl[h]], (((0,), (0,)), ((), ())),
                                preferred_element_type=F32) for h in heads]
        inter = [jnp.dot(q_ref[rows, hsl[h]], s_olds[h].astype(BF16), preferred_element_type=F32)
                 for h in heads]
        for h in heads:
            state[h] = gl_ref[h] * s_olds[h] + upds[h]
        scores_r = [lax.dot_general(q_ref[rows, hsl[h]], k_ref[rows, hsl[h]], (((1,), (1,)), ((), ())),
                                    preferred_element_type=F32) for h in heads]
        probs_r = [(scores_r[h] * decay_ref[h]).astype(BF16) for h in heads]
        intra = [jnp.dot(probs_r[h], v_ref[rows, hsl[h]], preferred_element_type=F32) for h in heads]

        owns = [left if h % 2 == 0 else jnp.logical_not(left) for h in heads]
        outs = [intra[h] + inter[h] * xi_ref[:, hsl[h]] for h in heads]
        mus = [jnp.sum(outs[h], axis=-1, keepdims=True) * inv_dim for h in heads]
        cen = [jnp.where(owns[h], outs[h] - mus[h], 0.0) for h in heads]
        var = [jnp.sum(cen[h] * cen[h], axis=-1, keepdims=True) * inv_dim for h in heads]
        nrm = [cen[h] * lax.rsqrt(var[h] + LN_EPS) for h in heads]
        ret_blocks = [nrm[2 * b] + nrm[2 * b + 1] for b in range(N_RET_HEADS // 2)]
        ret_o = jnp.concatenate(ret_blocks, axis=-1) * gain * _silu(g_ref[rows, :])

        swa_heads = range(N_SWA_HEADS)
        ss = [jnp.where(from_prev, scores[hh][:, :CHUNK], scores[hh][:, CHUNK:]) + bias_ref[table, hh]
              for hh in swa_heads]
        ms = [jnp.maximum(jnp.max(ss[hh], axis=-1, keepdims=True), sink_ref[hh]) for hh in swa_heads]
        ps = [jnp.exp(ss[hh] - ms[hh]) for hh in swa_heads]
        dens = [jnp.sum(ps[hh], axis=-1, keepdims=True) + jnp.exp(sink_ref[hh] - ms[hh]) for hh in swa_heads]
        pns = [ps[hh] / dens[hh] for hh in swa_heads]
        probs = [jnp.concatenate([jnp.where(from_prev, pns[hh], 0.0), jnp.where(from_prev, 0.0, pns[hh])],
                                 axis=-1).astype(BF16) for hh in swa_heads]
        for hh in range(N_SWA_HEADS):
            kv, j = divmod(hh, GQA_GROUP)
            o = jnp.dot(probs[hh], vvs[kv], preferred_element_type=F32)
            swa_blocks[j] = o if kv == 0 else swa_blocks[j] + o
        swa_o = jnp.concatenate(swa_blocks, axis=-1)

        mixed_ref[rows, :] = jnp.concatenate([ret_o, swa_o], axis=-1).astype(BF16)

    @pl.when(i == pl.num_programs(0) - 1)
    def _():
        st_ref[...] = state[...]


def _mix_prompt(proj, ret_consts, bias_tabs, sinks, gain):
    q, k, kz, v, g, sq, skv = proj
    rows = q.shape[0]
    tm = MIX_SUB * CHUNK
    gl, decay, xi = ret_consts
    bias_fold = bias_tabs[0]
    row_spec = lambda w: pl.BlockSpec((tm, w), lambda i: (i, 0))
    const2 = lambda a: pl.BlockSpec(a.shape, lambda i: (0, 0))
    const3 = lambda a: pl.BlockSpec(a.shape, lambda i: (0, 0, 0))
    smem = pl.BlockSpec(memory_space=pltpu.SMEM)
    mixed, st_full = pl.pallas_call(
        _mix_prompt_kernel,
        grid=(rows // tm,),
        in_specs=[smem, smem,
                  row_spec(SPLIT_W), row_spec(SPLIT_W), row_spec(SPLIT_W), row_spec(SPLIT_W), row_spec(RET_W),
                  row_spec(SPLIT_W), row_spec(2 * SWA_KVW),
                  pl.BlockSpec((CHUNK, 2 * SWA_KVW), lambda i: (jnp.maximum(MIX_SUB * i - 1, 0), 0)),
                  const3(decay), const2(xi),
                  pl.BlockSpec(bias_fold.shape, lambda i: (0, 0, 0, 0)),
                  const2(gain)],
        out_specs=[row_spec(D_MODEL),
                   pl.BlockSpec((N_RET_HEADS, 128, 128), lambda i: (0, 0, 0))],
        out_shape=[jax.ShapeDtypeStruct((rows, D_MODEL), BF16),
                   jax.ShapeDtypeStruct((N_RET_HEADS, 128, 128), F32)],
        scratch_shapes=[pltpu.VMEM((N_RET_HEADS, 128, 128), F32)],
        compiler_params=_cparams(("arbitrary",)),
        name="mix_prompt",
    )(gl, sinks, q, k, kz, v, g, sq, skv, skv, decay, xi, bias_fold, gain)
    lo, hi = slice(0, HEAD_DIM), slice(HEAD_DIM, 128)
    st = jnp.stack([st_full[h, lo, lo] if h % 2 == 0 else st_full[h, hi, hi] for h in range(N_RET_HEADS)])
    return mixed, st


def _ret_consts():
    lg = jnp.log(1.0 - 2.0 ** (-5.0 - jnp.arange(N_RET_HEADS, dtype=F32)))
    idx = jnp.arange(CHUNK, dtype=F32)
    diff = idx[:, None] - idx[None, :]
    decay = jnp.where(diff >= 0, jnp.exp(jnp.maximum(diff, 0.0)[None] * lg[:, None, None]), 0.0)
    xi = jnp.exp((idx + 1.0)[:, None] * lg[None, :])
    zeta = jnp.exp((CHUNK - 1.0 - idx)[:, None] * lg[None, :])
    gl = jnp.exp(CHUNK * lg)
    return gl, decay, jnp.repeat(xi, 128, axis=1), jnp.repeat(zeta, HEAD_DIM, axis=1), jnp.exp(1.0 * lg)


SAMPLE_BB = 8


def _mix_sample_kernel(sink_ref, gam_ref,
                       qt_ref, k_ref, v3_ref, sq3_ref, knew_ref, vnew_ref,
                       st_ref, ck_ref, cv_ref, g3_ref, x_ref, gate_ref,
                       wout_ref, brow_ref, gain_ref, lng_ref, lnb_ref,
                       x1_ref, nst_ref, nk_ref, nv_ref, ret_scr, swa_scr):
    i = pl.program_id(0)
    row = lax.broadcasted_iota(I32, (N_SWA_HEADS, 2 * HEAD_DIM), 0)
    lane = lax.broadcasted_iota(I32, (N_SWA_HEADS, 2 * HEAD_DIM), 1)
    own_half = (row // GQA_GROUP) == (lane // HEAD_DIM)
    sink_col = jnp.concatenate(
        [jnp.full((1, 1), sink_ref[hh], F32) for hh in range(N_SWA_HEADS)], axis=0)
    brow = brow_ref[...]
    qt = qt_ref[0]
    gam = gam_ref[...]
    seqs = range(SAMPLE_BB)
    state_shape = (N_RET_HEADS, HEAD_DIM, HEAD_DIM)

    s_old = [st_ref[b] for b in seqs]
    own_cols = (lax.broadcasted_iota(I32, (N_RET_HEADS, RET_W), 1) // HEAD_DIM ==
                lax.broadcasted_iota(I32, (N_RET_HEADS, RET_W), 0))
    k_blk = [jnp.where(own_cols, k_ref[b:b + 1, :], 0.0).astype(BF16) for b in seqs]
    outer = [lax.dot_general(k_blk[b], v3_ref[b].astype(BF16), (((0,), (0,)), ((), ())),
                             preferred_element_type=F32) for b in seqs]
    s_new = [gam * s_old[b] + outer[b] for b in seqs]
    for b in seqs:
        nst_ref[b] = s_new[b]
    outs = [jnp.sum((qt[:, b:b + 1] * s_new[b]).reshape(state_shape), axis=1) for b in seqs]
    for b in seqs:
        ret_scr[i * SAMPLE_BB + b] = outs[b]

    kk = [jnp.concatenate([ck_ref[0, b, 1:, :], knew_ref[b:b + 1, :]], axis=0) for b in seqs]
    vv = [jnp.concatenate([cv_ref[0, b, 1:, :], vnew_ref[b:b + 1, :]], axis=0) for b in seqs]
    for b in seqs:
        nk_ref[b] = kk[b]
        nv_ref[b] = vv[b]
    qblk = [jnp.where(own_half, jnp.concatenate([sq3_ref[b], sq3_ref[b]], axis=-1), 0.0).astype(BF16)
            for b in seqs]
    s = [lax.dot_general(qblk[b], kk[b].astype(BF16), (((1,), (1,)), ((), ())),
                         preferred_element_type=F32) + brow for b in seqs]
    m = [jnp.maximum(jnp.max(s[b], axis=-1, keepdims=True), sink_col) for b in seqs]
    p = [jnp.exp(s[b] - m[b]) for b in seqs]
    den = [jnp.sum(p[b], axis=-1, keepdims=True) + jnp.exp(sink_col - m[b]) for b in seqs]
    pn = [(p[b] / den[b]).astype(BF16) for b in seqs]
    o = [jnp.dot(pn[b], vv[b].astype(BF16), preferred_element_type=F32) for b in seqs]
    for b in seqs:
        swa_scr[i * SAMPLE_BB + b] = jnp.where(own_half[:, :HEAD_DIM], o[b][:, :HEAD_DIM], o[b][:, HEAD_DIM:])

    @pl.when(i == pl.num_programs(0) - 1)
    def _():
        y = jnp.zeros(x_ref.shape, F32)
        for h in range(N_RET_HEADS):
            o = ret_scr[:, h, :]
            mu = jnp.mean(o, axis=-1, keepdims=True)
            c = o - mu
            var = jnp.mean(c * c, axis=-1, keepdims=True)
            r = c * lax.rsqrt(var + LN_EPS) * gain_ref[h:h + 1, :] * _silu(g3_ref[h])
            y = y + jnp.dot(r.astype(BF16), wout_ref[h * HEAD_DIM:(h + 1) * HEAD_DIM, :],
                            preferred_element_type=F32)
        for hh in range(N_SWA_HEADS):
            o = swa_scr[:, hh, :].astype(BF16)
            lo = RET_W + hh * HEAD_DIM
            y = y + jnp.dot(o, wout_ref[lo:lo + HEAD_DIM, :], preferred_element_type=F32)
        x1_ref[...] = _ln(DN_ALPHA * x_ref[...] + (1.0 + gate_ref[0]) * y, lng_ref[...], lnb_ref[...])


def _mix_sample(proj, x, mod, layer, w_out_bf, gamma1, bias_row, sinks, gain8, ln_g, ln_b,
                st, ck, cv):
    q, k, _, v, g, sq, skv = proj
    nb = x.shape[0]
    steps = nb // SAMPLE_BB

    def joined(a):
        a = a.astype(F32).reshape(nb, RET_W // 128, 2, 128)
        return (a[:, :, 0, :] + a[:, :, 1, :]).reshape(nb, RET_W)

    to_cols = lambda a: joined(a).reshape(steps, SAMPLE_BB, RET_W).transpose(0, 2, 1)
    qt = to_cols(q)
    v3 = joined(v).reshape(nb, N_RET_HEADS, HEAD_DIM)
    sq3 = joined(sq).reshape(nb, GQA_GROUP, N_KV_HEADS, HEAD_DIM).transpose(0, 2, 1, 3)
    sq3 = sq3.reshape(nb, N_SWA_HEADS, HEAD_DIM)
    g3 = g.reshape(nb, N_RET_HEADS, HEAD_DIM).transpose(1, 0, 2)
    knew, vnew = skv[:, :SWA_KVW], skv[:, SWA_KVW:]
    gam =jnp.broadcast_to(jnp.repeat(gamma1, HEAD_DIM)[:, None], (RET_W, HEAD_DIM))
    smem = pl.BlockSpec(memory_space=pltpu.SMEM)
    blk3 = lambda a, b, c: pl.BlockSpec((a, b, c), lambda i: (i, 0, 0))
    lay4 = lambda b, c: pl.BlockSpec((1, SAMPLE_BB, b, c), lambda i: (layer, i, 0, 0))
    const2 = lambda a: pl.BlockSpec(a.shape, lambda i: (0, 0))
    const3 = lambda a: pl.BlockSpec(a.shape, lambda i: (0, 0, 0))
    x1, nst, nk, nv = pl.pallas_call(
        _mix_sample_kernel,
        grid=(steps,),
        in_specs=[smem, const2(gam),
                  blk3(1, RET_W, SAMPLE_BB), pl.BlockSpec((SAMPLE_BB, RET_W), lambda i: (i, 0)),
                  blk3(SAMPLE_BB, N_RET_HEADS, HEAD_DIM), blk3(SAMPLE_BB, N_SWA_HEADS, HEAD_DIM),
                  pl.BlockSpec((SAMPLE_BB, SWA_KVW), lambda i: (i, 0)),
                  pl.BlockSpec((SAMPLE_BB, SWA_KVW), lambda i: (i, 0)),
                  blk3(SAMPLE_BB, RET_W, HEAD_DIM), lay4(CHUNK, SWA_KVW), lay4(CHUNK, SWA_KVW),
                  const3(g3), const2(x),
                  pl.BlockSpec((1, nb, D_MODEL), lambda i: (layer, 0, 2)),
                  const2(w_out_bf), const2(bias_row), const2(gain8), const2(ln_g), const2(ln_b)],
        out_specs=[const2(x), blk3(SAMPLE_BB, RET_W, HEAD_DIM),
                   blk3(SAMPLE_BB, CHUNK, SWA_KVW), blk3(SAMPLE_BB, CHUNK, SWA_KVW)],
        out_shape=[jax.ShapeDtypeStruct(x.shape, F32), jax.ShapeDtypeStruct(st.shape, F32),
                   jax.ShapeDtypeStruct(ck.shape[1:], F32), jax.ShapeDtypeStruct(cv.shape[1:], F32)],
        scratch_shapes=[pltpu.VMEM((nb, N_RET_HEADS, HEAD_DIM), F32),
                        pltpu.VMEM((nb, N_SWA_HEADS, HEAD_DIM), F32)],
        compiler_params=_cparams(("arbitrary",)),
        name="mix_sample",
    )(sinks, gam, qt, joined(k), v3, sq3, knew, vnew, st, ck, cv, g3, x, mod,
      w_out_bf, bias_row, gain8, ln_g, ln_b)
    cache_shape = (nb, CHUNK, N_KV_HEADS, HEAD_DIM)
    return (x1, nst.reshape(nb, N_RET_HEADS, HEAD_DIM, HEAD_DIM), nk.reshape(cache_shape),
            nv.reshape(cache_shape))


def _router_kernel(x_ref, sh_ref, sc_ref, rwt_ref, rb_ref, tri_ref,
                   oi_ref, of_ref, cnt_ref, run_ref, *, mod_rows):
    h2 = x_ref[...] * (1.0 + sc_ref[0][:mod_rows]) + sh_ref[0][:mod_rows]
    _route_rows(h2.astype(BF16), rwt_ref, rb_ref, tri_ref, oi_ref, of_ref, cnt_ref, run_ref)


def _post_mix_kernel(mixed_ref, x_ref, gate_ref, wout_ref, lng_ref, lnb_ref,
                     sh_ref, sc_ref, rwt_ref, rb_ref, tri_ref,
                     x1_ref, oi_ref, of_ref, cnt_ref, run_ref):
    n_chunks = mixed_ref.shape[0] // CHUNK
    chunks = [slice(c * CHUNK, (c + 1) * CHUNK) for c in range(n_chunks)]
    ys = [jnp.dot(mixed_ref[r, :], wout_ref[...], preferred_element_type=F32) for r in chunks]
    x1s = [_ln(DN_ALPHA * x_ref[r, :] + (1.0 + gate_ref[0][:1]) * y, lng_ref[...], lnb_ref[...])
           for r, y in zip(chunks, ys)]
    for r, x1 in zip(chunks, x1s):
        x1_ref[r, :] = x1
    h2 = jnp.concatenate([(x1 * (1.0 + sc_ref[0][:1]) + sh_ref[0][:1]).astype(BF16) for x1 in x1s], axis=0)
    _route_rows(h2, rwt_ref, rb_ref, tri_ref, oi_ref, of_ref, cnt_ref, run_ref)


def _route_rows(h2b, rwt_ref, rb_ref, tri_ref, oi_ref, of_ref, cnt_ref, run_ref):
    i = pl.program_id(0)

    @pl.when(i == 0)
    def _():
        run_ref[...] = jnp.zeros_like(run_ref)

    logits = lax.dot_general(rwt_ref[...].astype(BF16), h2b, (((1,), (1,)), ((), ())),
                             preferred_element_type=F32)
    aff = 1.0 / (1.0 + jnp.exp(-logits))
    sel = aff + rb_ref[...]
    s = [sel[e:e + 1, :] for e in range(N_EXPERTS)]
    a = [aff[e:e + 1, :] for e in range(N_EXPERTS)]

    def top2sum(v0, v1, v2, v3):
        hi01, lo01 = jnp.maximum(v0, v1), jnp.minimum(v0, v1)
        hi23, lo23 = jnp.maximum(v2, v3), jnp.minimum(v2, v3)
        return jnp.maximum(hi01, hi23) + jnp.maximum(jnp.minimum(hi01, hi23),
                                                     jnp.maximum(lo01, lo23))

    def argmax_first(vals):
        best, idx = vals[0], jnp.zeros(vals[0].shape, I32)
        for j in range(1, len(vals)):
            upd = vals[j] > best
            idx = jnp.where(upd, j, idx)
            best = jnp.where(upd, vals[j], best)
        return idx

    def pick(idx, vals):
        out = vals[-1]
        for j in range(len(vals) - 2, -1, -1):
            out = jnp.where(idx == j, vals[j], out)
        return out

    gi = argmax_first([top2sum(*s[4 * g:4 * g + 4]) for g in range(N_GROUPS)])
    sv = [pick(gi, [s[4 * g + j] for g in range(N_GROUPS)]) for j in range(EXPERTS_PER_GROUP)]
    av = [pick(gi, [a[4 * g + j] for g in range(N_GROUPS)]) for j in range(EXPERTS_PER_GROUP)]
    i1 = argmax_first(sv)
    i2 = argmax_first([jnp.where(i1 == j, -jnp.inf, sv[j]) for j in range(EXPERTS_PER_GROUP)])
    w1, w2 = pick(i1, av), pick(i2, av)
    wsum = w1 + w2
    w1, w2 = w1 / wsum, w2 / wsum
    lo, hi = jnp.minimum(i1, i2), jnp.maximum(i1, i2)
    w_lo = jnp.where(i1 < i2, w1, w2)
    w_hi = jnp.where(i1 < i2, w2, w1)
    pair = jnp.where(lo == 0, hi - 1, jnp.where(lo == 1, 6 - hi, 5))
    bin_id = gi * N_PAIRS + pair

    tm = bin_id.shape[1]
    onehot = lax.broadcasted_iota(I32, (BIN_ROWS, tm), 0) == bin_id
    oh_f = jnp.where(onehot, 1.0, 0.0)
    before = jnp.dot(oh_f.astype(BF16), tri_ref[...], preferred_element_type=F32)
    run = run_ref[...]
    run_t = jnp.concatenate([run] * (tm // 128), axis=-1)
    rank = jnp.sum(oh_f * (before + run_t), axis=0, keepdims=True)
    run_new = run + jnp.sum(oh_f, axis=1, keepdims=True)
    run_ref[...] = run_new
    cnt_ref[...] = run_new.astype(I32)

    zi = jnp.zeros_like(bin_id)
    oi_ref[0] = jnp.concatenate([gi * 4 + lo, gi * 4 + hi, bin_id, rank.astype(I32), zi, zi, zi, zi], axis=0)
    zf = jnp.zeros_like(w_lo)
    of_ref[0] = jnp.concatenate([w_lo, w_hi, zf, zf, zf, zf, zf, zf], axis=0)


def _router(x1, mod, layer, mod_row0, mod_rows, router_wt, router_b, tm):
    rows = x1.shape[0]
    nt = rows // tm
    mblk = 8 if mod_rows == 1 else mod_rows
    mrow = mod_row0 // mblk
    tri = jnp.asarray(np.triu(np.ones((tm, tm), np.float32), 1), BF16)
    rb = jnp.broadcast_to(router_b.astype(F32)[:, None], (N_EXPERTS, tm))
    mod_spec = lambda c: pl.BlockSpec((1, mblk, D_MODEL), lambda i: (layer, mrow, c))
    oi, of, cnt = pl.pallas_call(
        functools.partial(_router_kernel, mod_rows=mod_rows),
        grid=(nt,),
        in_specs=[pl.BlockSpec((tm, D_MODEL), lambda i: (i, 0)), mod_spec(3), mod_spec(4),
                  pl.BlockSpec((N_EXPERTS, D_MODEL), lambda i: (0, 0)),
                  pl.BlockSpec((N_EXPERTS, tm), lambda i: (0, 0)),
                  pl.BlockSpec((tm, tm), lambda i: (0, 0))],
        out_specs=[pl.BlockSpec((1, 8, tm), lambda i: (i, 0, 0)),
                   pl.BlockSpec((1, 8, tm), lambda i: (i, 0, 0)),
                   pl.BlockSpec((BIN_ROWS, 128), lambda i: (0, 0))],
        out_shape=[jax.ShapeDtypeStruct((nt, 8, tm), I32),
                   jax.ShapeDtypeStruct((nt, 8, tm), F32),
                   jax.ShapeDtypeStruct((BIN_ROWS, 128), I32)],
        scratch_shapes=[pltpu.VMEM((BIN_ROWS, 128), F32)],
        compiler_params=_cparams(("arbitrary",)),
        name="router",
    )(x1, mod, mod, router_wt, rb, tri)
    return _unpack_route(oi, of, cnt)


def _unpack_route(oi, of, cnt):
    flat = lambda a, r: a[:, r, :].reshape(-1)
    return (flat(oi, 0), flat(oi, 1), flat(oi, 2), flat(oi, 3), flat(of, 0), flat(of, 1),
            cnt[:N_BINS, 0])


POST_TM = 512


def _post_mix(mixed, x, mod, layer, mod_row0, w_out_bf, ln_g, ln_b, router_wt, router_b):
    rows = x.shape[0]
    tm = POST_TM
    nt = rows // tm
    mrow = mod_row0 // 8
    tri = jnp.asarray(np.triu(np.ones((tm, tm), np.float32), 1), BF16)
    rb = jnp.broadcast_to(router_b.astype(F32)[:, None], (N_EXPERTS, tm))
    mod_spec = lambda c: pl.BlockSpec((1, 8, D_MODEL), lambda i: (layer, mrow, c))
    row_spec = pl.BlockSpec((tm, D_MODEL), lambda i: (i, 0))
    route_spec = pl.BlockSpec((1, 8, tm), lambda i: (i, 0, 0))
    const2 = lambda a: pl.BlockSpec(a.shape, lambda i: (0, 0))
    x1, oi, of, cnt = pl.pallas_call(
        _post_mix_kernel,
        grid=(nt,),
        in_specs=[row_spec, row_spec, mod_spec(2), const2(w_out_bf), const2(ln_g), const2(ln_b),
                  mod_spec(3), mod_spec(4), const2(router_wt), const2(rb), const2(tri)],
        out_specs=[row_spec, route_spec, route_spec,
                   pl.BlockSpec((BIN_ROWS, 128), lambda i: (0, 0))],
        out_shape=[jax.ShapeDtypeStruct((rows, D_MODEL), F32),
                   jax.ShapeDtypeStruct((nt, 8, tm), I32),
                   jax.ShapeDtypeStruct((nt, 8, tm), F32),
                   jax.ShapeDtypeStruct((BIN_ROWS, 128), I32)],
        scratch_shapes=[pltpu.VMEM((BIN_ROWS, 128), F32)],
        compiler_params=_cparams(("arbitrary",)),
        name="post_mix",
    )(mixed, x, mod, w_out_bf, ln_g, ln_b, mod, mod, router_wt, rb, tri)
    return x1, _unpack_route(oi, of, cnt)


DMA_UNROLL = 8
ROW_GROUPS = MOE_TM // DMA_UNROLL


GROUPS_PER_ITER = 2


def _for_rows(n, fn):
    per_iter = GROUPS_PER_ITER * DMA_UNROLL

    def body_u(jj, carry):
        for g in range(GROUPS_PER_ITER):
            for u in range(DMA_UNROLL):
                fn(jj * GROUPS_PER_ITER + g, u)
        return carry
    lax.fori_loop(0, n // per_iter, body_u, 0)

    def body_1(r, carry):
        fn(r // DMA_UNROLL, r % DMA_UNROLL)
        return carry
    lax.fori_loop((n // per_iter) * per_iter, n, body_1, 0)


def _wait_rows(n, buf, sem):
    p = ROW_GROUPS
    while p >= 1:
        @pl.when((n & (p * DMA_UNROLL)) != 0)
        def _(p=p):
            pltpu.make_async_copy(buf.at[pl.ds(0, p)], buf.at[pl.ds(0, p)], sem).wait()
        p //= 2
    p = DMA_UNROLL // 2
    while p >= 1:
        @pl.when((n & p) != 0)
        def _(p=p):
            pltpu.make_async_copy(buf.at[0, pl.ds(0, p)], buf.at[0, pl.ds(0, p)], sem).wait()
        p //= 2


def _moe_sorted_kernel(ea_ref, eb_ref, nvalid_ref, pos_ref,
                       x_hbm, sh_ref, sc_ref, gate_ref, rw_ref,
                       wga_ref, wua_ref, wda_ref, wgb_ref, wub_ref, wdb_ref,
                       lng_ref, lnb_ref,
                       out_hbm,
                       inv, xbuf, obuf, wg_a, wu_a, wd_a, wg_b, wu_b, wd_b, gsem, ssem):
    i = pl.program_id(0)
    nt = pl.num_programs(0)
    slot = i % 2
    n_tok = pos_ref.shape[0]

    def start_gather(t, dst_slot):
        def one(j, u):
            tok = inv[t * MOE_TM + j * DMA_UNROLL + u]
            pltpu.make_async_copy(x_hbm.at[pl.ds(tok, 1)], xbuf.at[dst_slot, j, pl.ds(u, 1)],
                                  gsem.at[dst_slot]).start()
        _for_rows(nvalid_ref[t], one)

    @pl.when(i == 0)
    def _():
        def body(j, carry):
            for u in range(DMA_UNROLL):
                t = j * DMA_UNROLL + u
                inv[pos_ref[t]] = t
            return carry
        lax.fori_loop(0, n_tok // DMA_UNROLL, body, 0)
        xbuf[...] = jnp.zeros_like(xbuf)
        start_gather(0, 0)

    @pl.when(i >= 2)
    def _():
        _wait_rows(nvalid_ref[jnp.maximum(i - 2, 0)], obuf.at[slot], ssem.at[slot])

    @pl.when(nvalid_ref[i] > 0)
    def _():
        _wait_rows(nvalid_ref[i], xbuf.at[slot], gsem.at[slot])

        @pl.when(i + 1 < nt)
        def _():
            start_gather(jnp.minimum(i + 1, nt - 1), 1 - slot)

        @pl.when(jnp.logical_or(i == 0, ea_ref[i] != ea_ref[jnp.maximum(i - 1, 0)]))
        def _():
            wg_a[...] = wga_ref[0, 0].astype(BF16)
            wu_a[...] = wua_ref[0, 0].astype(BF16)
            wd_a[...] = wda_ref[0, 0].astype(BF16)

        @pl.when(jnp.logical_or(i == 0, eb_ref[i] != eb_ref[jnp.maximum(i - 1, 0)]))
        def _():
            wg_b[...] = wgb_ref[0, 0].astype(BF16)
            wu_b[...] = wub_ref[0, 0].astype(BF16)
            wd_b[...] = wdb_ref[0, 0].astype(BF16)

        x = xbuf[slot].reshape(MOE_TM, D_MODEL)
        h2 = (x * (1.0 + sc_ref[0][:1]) + sh_ref[0][:1]).astype(BF16)

        dot = functools.partial(jnp.dot, preferred_element_type=F32)
        logits = dot(h2, rw_ref[...])
        gate_a, gate_b = dot(h2, wg_a[...]), dot(h2, wg_b[...])
        up_a, up_b = dot(h2, wu_a[...]), dot(h2, wu_b[...])
        act_a = (_silu(gate_a) * up_a).astype(BF16)
        act_b = (_silu(gate_b) * up_b).astype(BF16)
        y_a, y_b = dot(act_a, wd_a[...]), dot(act_b, wd_b[...])

        aff = 1.0 / (1.0 + jnp.exp(-logits))
        lane = lax.broadcasted_iota(I32, aff.shape, 1)
        a_lo = jnp.sum(jnp.where(lane == ea_ref[i], aff, 0.0), axis=-1, keepdims=True)
        a_hi = jnp.sum(jnp.where(lane == eb_ref[i], aff, 0.0), axis=-1, keepdims=True)
        a_sum = a_lo + a_hi
        y = (a_lo / a_sum) * y_a + (a_hi / a_sum) * y_b
        out = _ln(DN_ALPHA * x + (1.0 + gate_ref[0][:1]) * y, lng_ref[...], lnb_ref[...])
        obuf[slot] = out.reshape(ROW_GROUPS, DMA_UNROLL, D_MODEL)

        def scatter_one(j, u):
            tok = inv[i * MOE_TM + j * DMA_UNROLL + u]
            pltpu.make_async_copy(obuf.at[slot, j, pl.ds(u, 1)], out_hbm.at[pl.ds(tok, 1)],
                                  ssem.at[slot]).start()
        _for_rows(nvalid_ref[i], scatter_one)

    @pl.when(i == nt - 1)
    def _():
        _wait_rows(nvalid_ref[jnp.maximum(i - 1, 0)], obuf.at[1 - slot], ssem.at[1 - slot])
        _wait_rows(nvalid_ref[i], obuf.at[slot], ssem.at[slot])


def _moe_sorted(x1, mod, layer, mod_row0, route, rw_pad, wg, wu, wd, ln_g, ln_b):
    _, _, bin_id, rank, _, _, counts = route
    rows = x1.shape[0]
    nt = rows // MOE_TM + N_BINS
    ns = nt * MOE_TM
    tiles = (counts + MOE_TM - 1) // MOE_TM
    tile_end = jnp.cumsum(tiles)
    row_start = ((tile_end - tiles) * MOE_TM).astype(I32)
    pos = jnp.sum(jnp.where(bin_id[:, None] == jnp.arange(N_BINS, dtype=I32)[None, :],
                            row_start[None, :], 0), axis=1) + rank
    total = tile_end[-1]
    t = jnp.arange(nt, dtype=I32)
    tile_bin = jnp.sum(jnp.minimum(t, total - 1)[:, None] >= tile_end[None, :], axis=1).astype(I32)
    tile_bin = jnp.clip(tile_bin, 0, N_BINS - 1)
    bin_lo = jnp.asarray([g * 4 + _PAIR_LO[p] for g in range(N_GROUPS) for p in range(N_PAIRS)], I32)
    bin_hi = jnp.asarray([g * 4 + _PAIR_HI[p] for g in range(N_GROUPS) for p in range(N_PAIRS)], I32)
    tile_ea, tile_eb = bin_lo[tile_bin], bin_hi[tile_bin]
    nvalid = jnp.clip(counts[tile_bin] - (t - (tile_end - tiles)[tile_bin]) * MOE_TM, 0, MOE_TM)
    nvalid = jnp.where(t < total, nvalid, 0).astype(I32)
    mrow = mod_row0 // 8
    mod_spec = lambda c: pl.BlockSpec((1, 8, D_MODEL), lambda i, *_: (layer, mrow, c))
    w_spec = lambda shape, ref_idx: pl.BlockSpec(
        (1, 1) + shape, lambda i, ea_r, eb_r, *_: (layer, (ea_r, eb_r)[ref_idx][i], 0, 0))
    any_spec = pl.BlockSpec(memory_space=pl.ANY)
    w_bf = lambda shape: pltpu.VMEM(shape, BF16)
    grid_spec = pltpu.PrefetchScalarGridSpec(
        num_scalar_prefetch=4,
        grid=(nt,),
        in_specs=[any_spec,
                  mod_spec(3), mod_spec(4), mod_spec(5),
                  pl.BlockSpec((D_MODEL, 128), lambda i, *_: (0, 0)),
                  w_spec((D_MODEL, D_FF), 0), w_spec((D_MODEL, D_FF), 0), w_spec((D_FF, D_MODEL), 0),
                  w_spec((D_MODEL, D_FF), 1), w_spec((D_MODEL, D_FF), 1), w_spec((D_FF, D_MODEL), 1),
                  pl.BlockSpec((1, D_MODEL), lambda i, *_: (0, 0)),
                  pl.BlockSpec((1, D_MODEL), lambda i, *_: (0, 0))],
        out_specs=any_spec,
        scratch_shapes=[pltpu.SMEM((ns,), I32),
                        pltpu.VMEM((2, ROW_GROUPS, DMA_UNROLL, D_MODEL), F32),
                        pltpu.VMEM((2, ROW_GROUPS, DMA_UNROLL, D_MODEL), F32),
                        w_bf((D_MODEL, D_FF)), w_bf((D_MODEL, D_FF)), w_bf((D_FF, D_MODEL)),
                        w_bf((D_MODEL, D_FF)), w_bf((D_MODEL, D_FF)), w_bf((D_FF, D_MODEL)),
                        pltpu.SemaphoreType.DMA((2,)),
                        pltpu.SemaphoreType.DMA((2,))],
    )
    return pl.pallas_call(
        _moe_sorted_kernel,
        grid_spec=grid_spec,
        out_shape=jax.ShapeDtypeStruct((rows, D_MODEL), F32),
        compiler_params=_cparams(("arbitrary",)),
        name="moe_sorted",
    )(tile_ea, tile_eb, nvalid, pos, x1, mod, mod, mod, rw_pad,
      wg, wu, wd, wg, wu, wd, ln_g, ln_b)


def _moe_dense_kernel(x_ref, dw_ref, sh_ref, sc_ref, gate_ref, wg_ref, wu_ref, wd_ref,
                      lng_ref, lnb_ref, o_ref, acc):
    e = pl.program_id(0)

    @pl.when(e == 0)
    def _():
        acc[...] = jnp.zeros_like(acc)

    x = x_ref[...]
    h2 = (x * (1.0 + sc_ref[0]) + sh_ref[0]).astype(BF16)
    a = _silu(jnp.dot(h2, wg_ref[0, 0].astype(BF16), preferred_element_type=F32)) * \
        jnp.dot(h2, wu_ref[0, 0].astype(BF16), preferred_element_type=F32)
    y = jnp.dot(a.astype(BF16), wd_ref[0, 0].astype(BF16), preferred_element_type=F32)
    acc[...] = acc[...] + dw_ref[0][:, :1] * y

    @pl.when(e == pl.num_programs(0) - 1)
    def _():
        o_ref[...] = _ln(DN_ALPHA * x + (1.0 + gate_ref[0]) * acc[...], lng_ref[...], lnb_ref[...])


def _moe_dense(x1, mod, layer, route, wg, wu, wd, ln_g, ln_b):
    ea, eb, _, _, w_lo, w_hi, _ = route
    nb = x1.shape[0]
    eids = jnp.arange(N_EXPERTS, dtype=I32)[:, None]
    dw = jnp.where(eids == ea[None, :], w_lo[None, :], 0.0) + \
        jnp.where(eids == eb[None, :], w_hi[None, :], 0.0)
    dw = jnp.broadcast_to(dw[:, :, None], (N_EXPERTS, nb, 128))
    mod_spec = lambda c: pl.BlockSpec((1, nb, D_MODEL), lambda e: (layer, 0, c))
    const2 = lambda a: pl.BlockSpec(a.shape, lambda e: (0, 0))
    return pl.pallas_call(
        _moe_dense_kernel,
        grid=(N_EXPERTS,),
        in_specs=[const2(x1), pl.BlockSpec((1, nb, 128), lambda e: (e, 0, 0)),
                  mod_spec(3), mod_spec(4), mod_spec(5),
                  pl.BlockSpec((1, 1, D_MODEL, D_FF), lambda e: (layer, e, 0, 0)),
                  pl.BlockSpec((1, 1, D_MODEL, D_FF), lambda e: (layer, e, 0, 0)),
                  pl.BlockSpec((1, 1, D_FF, D_MODEL), lambda e: (layer, e, 0, 0)),
                  const2(ln_g), const2(ln_b)],
        out_specs=const2(x1),
        out_shape=jax.ShapeDtypeStruct(x1.shape, F32),
        scratch_shapes=[pltpu.VMEM(x1.shape, F32)],
        compiler_params=_cparams(("arbitrary",)),
        name="moe_dense",
    )(x1, dw, mod, mod, mod, wg, wu, wd, ln_g, ln_b)


def kernel(x_prompt, x_sample, state_ret, cache_swa_k, cache_swa_v, c_prompt, c_sample, w_in, w_out, ret_gn_gain, swa_sinks, rel_bias_table, ada_w, ada_b, ln1_g, ln1_b, ln2_g, ln2_b, router_w, router_b, exp_w_gate, exp_w_up, exp_w_down):
    seq = x_prompt.shape[1]
    nb = x_sample.shape[0]
    past_len = 16384
    assert x_prompt.shape[0] == 1 and x_sample.shape[1] == 1

    c_all = jnp.concatenate([c_sample, c_prompt, jnp.zeros((7, D_MODEL), F32)], axis=0)
    mod = _ada(c_all, ada_w, ada_b)
    prompt_row = nb

    bias_tabs = _bias_tables(rel_bias_table.astype(F32))
    gl, decay, xi, zeta, gamma1 = _ret_consts()
    rope_p = _rope_tables(jnp.arange(seq, dtype=I32))
    rope_s = _rope_tables(jnp.full((nb,), past_len, I32))
    router_wt = router_w.astype(F32).T
    rw_pad = jnp.pad(router_w.astype(BF16), ((0, 0), (0, 128 - N_EXPERTS)))
    zeta_p = jnp.tile(zeta, (INPROJ_TM // CHUNK, 1))
    zeta_s = jnp.ones((nb, RET_W), F32)

    ck_all = cache_swa_k.astype(F32).reshape(DEPTH, nb, CHUNK, SWA_KVW)
    cv_all = cache_swa_v.astype(F32).reshape(DEPTH, nb, CHUNK, SWA_KVW)

    xp = x_prompt.reshape(seq, D_MODEL)
    xs = x_sample.reshape(nb, D_MODEL)
    st_p, k_p, v_p, st_s, k_s, v_s = [], [], [], [], [], []
    for l in range(DEPTH):
        w_in_bf = w_in[l].astype(BF16)
        w_out_bf = w_out[l].astype(BF16)
        order = jnp.asarray(_SWA_HEAD_ORDER)
        swa_rows = w_out[l][RET_W:].reshape(N_SWA_HEADS, HEAD_DIM, D_MODEL)[order]
        w_out_prompt = jnp.concatenate([w_out[l][:RET_W], swa_rows.reshape(SWA_QW, D_MODEL)],
                                       axis=0).astype(BF16)
        experts = (exp_w_gate.astype(F32), exp_w_up.astype(F32), exp_w_down.astype(F32))
        gain = ret_gn_gain[l].astype(F32)
        sinks = swa_sinks[l].astype(F32)
        row = lambda a: a[l].astype(F32).reshape(1, D_MODEL)

        proj = _inproj(xp, mod, l, prompt_row, 1, w_in_bf, rope_p, zeta_p, INPROJ_TM)
        mixed, st = _mix_prompt(proj, (gl, decay, xi), bias_tabs, sinks, gain.reshape(1, RET_W))
        skv = proj[6]
        st_p.append(st.reshape(1, N_RET_HEADS, HEAD_DIM, HEAD_DIM))
        k_p.append(skv[seq - CHUNK:, :SWA_KVW].reshape(1, CHUNK, N_KV_HEADS, HEAD_DIM))
        v_p.append(skv[seq - CHUNK:, SWA_KVW:].reshape(1, CHUNK, N_KV_HEADS, HEAD_DIM))
        x1, route = _post_mix(mixed, xp, mod, l, prompt_row, w_out_prompt, row(ln1_g), row(ln1_b),
                              router_wt, router_b)
        xp = _moe_sorted(x1, mod, l, prompt_row, route, rw_pad, *experts, row(ln2_g), row(ln2_b))

        proj = _inproj(xs, mod, l, 0, nb, w_in_bf, rope_s, zeta_s, nb)
        x1, nst, nk, nv = _mix_sample(proj, xs, mod, l, w_out_bf, gamma1, bias_tabs[1], sinks, gain,
                                      row(ln1_g), row(ln1_b),
                                      state_ret[l].astype(F32).reshape(nb, RET_W, HEAD_DIM), ck_all, cv_all)
        st_s.append(nst)
        k_s.append(nk)
        v_s.append(nv)
        route = _router(x1, mod, l, 0, nb, router_wt, router_b, nb)
        xs = _moe_dense(x1, mod, l, route, *experts, row(ln2_g), row(ln2_b))

    return (xp.reshape(1, seq, D_MODEL), xs.reshape(nb, 1, D_MODEL),
            jnp.stack(st_p), jnp.stack(k_p), jnp.stack(v_p),
            jnp.stack(st_s), jnp.stack(k_s), jnp.stack(v_s))
```

```python
import functools
import math

import numpy as np
import jax
import jax.numpy as jnp
from jax import lax
from jax.experimental import pallas as pl
from jax.experimental.pallas import tpu as pltpu

F32 = jnp.float32
BF16 = jnp.bfloat16
I32 = jnp.int32

D_MODEL = 1024
DEPTH = 2
HEAD_DIM = 64
N_RET_HEADS = 8
N_SWA_HEADS = 8
N_KV_HEADS = 2
GQA_GROUP = N_SWA_HEADS // N_KV_HEADS
RET_W = N_RET_HEADS * HEAD_DIM
SWA_QW = N_SWA_HEADS * HEAD_DIM
SWA_KVW = N_KV_HEADS * HEAD_DIM
PROJ_W = 4 * RET_W + SWA_QW + 2 * SWA_KVW
CHUNK = 128
ROPE_BASE = 10000.0
N_BUCKETS = 32
MAX_DISTANCE = 128
N_EXPERTS = 16
N_GROUPS = 4
EXPERTS_PER_GROUP = 4
D_FF = 512
LN_EPS = 1e-5
DN_ALPHA = (2 * DEPTH) ** 0.25
QK_SCALE = HEAD_DIM ** -0.5
NEG_BIG = -1e30

N_PAIRS = 6
N_BINS = N_GROUPS * N_PAIRS
BIN_ROWS = 32
MOE_TM = 256
INPROJ_TM = 512
MIX_SUB = 4
SPLIT_W = 128 * N_RET_HEADS
_SWA_HEAD_ORDER = (0, 4, 1, 5, 2, 6, 3, 7)
VMEM_LIMIT = 56 * 1024 * 1024

_PAIR_LO = (0, 0, 0, 1, 1, 3)
_PAIR_HI = (1, 2, 3, 3, 2, 2)


def _cparams(sem):
    return pltpu.CompilerParams(dimension_semantics=sem, vmem_limit_bytes=VMEM_LIMIT)


def _ln(v, g, b):
    mu = jnp.mean(v, axis=-1, keepdims=True)
    c = v - mu
    var = jnp.mean(c * c, axis=-1, keepdims=True)
    return c * lax.rsqrt(var + LN_EPS) * g + b


def _silu(v):
    return v * (1.0 / (1.0 + jnp.exp(-v)))


def _ada_kernel(c_ref, w_ref, b_ref, o_ref):
    o_ref[0] = jnp.dot(c_ref[...].astype(BF16), w_ref[0].astype(BF16),
                       preferred_element_type=F32) + b_ref[0]


def _ada(c_all, ada_w, ada_b):
    rows = c_all.shape[0]
    tn = 2 * D_MODEL
    nt = 6 * D_MODEL // tn
    return pl.pallas_call(
        _ada_kernel,
        grid=(DEPTH, nt),
        in_specs=[pl.BlockSpec((rows, D_MODEL), lambda l, j: (0, 0)),
                  pl.BlockSpec((1, D_MODEL, tn), lambda l, j: (l, 0, j)),
                  pl.BlockSpec((1, 1, tn), lambda l, j: (l, 0, j))],
        out_specs=pl.BlockSpec((1, rows, tn), lambda l, j: (l, 0, j)),
        out_shape=jax.ShapeDtypeStruct((DEPTH, rows, 6 * D_MODEL), F32),
        compiler_params=_cparams(("arbitrary", "arbitrary")),
        name="ada",
    )(c_all, ada_w, ada_b.reshape(DEPTH, 1, 6 * D_MODEL))


def _bias_kernel(tab_ref, bkt_ref, fold_ref, row_ref):
    bkt = bkt_ref[...]
    rows = lax.broadcasted_iota(I32, (CHUNK, CHUNK), 0)
    cols = lax.broadcasted_iota(I32, (CHUNK, CHUNK), 1)
    from_prev = cols > rows
    for h in range(N_SWA_HEADS):
        acc = jnp.zeros(bkt.shape, F32)
        for b in range(N_BUCKETS):
            acc = jnp.where(bkt == b, tab_ref[b, h], acc)
        own = acc[:, CHUNK:]
        fold_ref[0, h] = jnp.where(from_prev, NEG_BIG, own)
        fold_ref[1, h] = jnp.where(from_prev, acc[:, :CHUNK], own)
        row_ref[h:h + 1, :] = own[CHUNK - 1:CHUNK, :]


def _t5_bucket(rel):
    max_exact = N_BUCKETS // 2
    relf = jnp.maximum(rel, 1).astype(F32)
    large = max_exact + (jnp.log(relf / max_exact) / math.log(MAX_DISTANCE / max_exact)
                         * (N_BUCKETS - max_exact)).astype(I32)
    large = jnp.minimum(large, N_BUCKETS - 1)
    return jnp.where(rel < max_exact, rel, large)


def _bias_tables(rel_bias_table):
    qi = jnp.arange(CHUNK)
    si = jnp.arange(2 * CHUNK)
    rel = CHUNK + qi[:, None] - si[None, :]
    bkt = _t5_bucket(jnp.maximum(rel, 0)).astype(I32)
    return pl.pallas_call(
        _bias_kernel,
        in_specs=[pl.BlockSpec(memory_space=pltpu.SMEM),
                  pl.BlockSpec((CHUNK, 2 * CHUNK), lambda: (0, 0))],
        out_specs=[pl.BlockSpec((2, N_SWA_HEADS, CHUNK, CHUNK), lambda: (0, 0, 0, 0)),
                   pl.BlockSpec((N_SWA_HEADS, CHUNK), lambda: (0, 0))],
        out_shape=[jax.ShapeDtypeStruct((2, N_SWA_HEADS, CHUNK, CHUNK), F32),
                   jax.ShapeDtypeStruct((N_SWA_HEADS, CHUNK), F32)],
        name="t5_bias",
    )(rel_bias_table, bkt)


def _rotary(v, cos, s_lo, s_hi):
    outs = []
    for j in range(RET_W // 128):
        blk = v[:, j * 128:(j + 1) * 128]
        outs.append(blk * cos + pltpu.roll(blk, 96, 1) * s_lo + pltpu.roll(blk, 32, 1) * s_hi)
    return jnp.concatenate(outs, axis=-1)


def _inproj_kernel(x_ref, sh_ref, sc_ref, w_ref, cs_ref, zeta_ref,
                   q_ref, k_ref, kz_ref, v_ref, g_ref, sq_ref, skv_ref, *, mod_rows):
    sh = sh_ref[0][:mod_rows]
    sc = sc_ref[0][:mod_rows]
    h = (x_ref[...] * (1.0 + sc) + sh).astype(BF16)
    cs = jnp.concatenate([cs_ref[...], cs_ref[...]], axis=-1)
    first = (lax.broadcasted_iota(I32, cs.shape, 1) % HEAD_DIM) < (HEAD_DIM // 2)
    cos = jnp.where(first, cs, pltpu.roll(cs, 32, 1))
    s_lo = jnp.where(first, -pltpu.roll(cs, 96, 1), 0.0)
    s_hi = jnp.where(first, 0.0, cs)

    def proj(lo, hi):
        return jnp.dot(h, w_ref[:, lo:hi], preferred_element_type=F32)

    left = lax.broadcasted_iota(I32, cs.shape, 1) < HEAD_DIM

    def store_split(ref, val):
        for b in range(RET_W // 128):
            blk = val[:, b * 128:(b + 1) * 128]
            ref[:, (2 * b) * 128:(2 * b + 1) * 128] = jnp.where(left, blk, 0.0).astype(BF16)
            ref[:, (2 * b + 1) * 128:(2 * b + 2) * 128] = jnp.where(left, 0.0, blk).astype(BF16)

    store_split(q_ref, _rotary(proj(0, RET_W), cos, s_lo, s_hi))
    k = _rotary(proj(RET_W, 2 * RET_W), cos, s_lo, s_hi) * QK_SCALE
    store_split(k_ref, k)
    store_split(kz_ref, k * zeta_ref[...])
    store_split(v_ref, proj(2 * RET_W, 3 * RET_W))
    g_ref[...] = proj(3 * RET_W, 4 * RET_W)
    sq = proj(4 * RET_W, 4 * RET_W + SWA_QW) * QK_SCALE
    for hh in range(N_SWA_HEADS):
        kv, j = divmod(hh, GQA_GROUP)
        blk = sq[:, (hh // 2) * 128:(hh // 2 + 1) * 128]
        if hh % 2 != kv:
            blk = pltpu.roll(blk, HEAD_DIM, 1)
        keep = left if kv == 0 else jnp.logical_not(left)
        sq_ref[:, (2 * j + kv) * 128:(2 * j + kv + 1) * 128] = jnp.where(keep, blk, 0.0).astype(BF16)
    skv_ref[...] = proj(4 * RET_W + SWA_QW, PROJ_W)


def _inproj(x, mod, layer, mod_row0, mod_rows, w_in_bf, rope, zeta_tile, tm):
    rows = x.shape[0]
    mblk = 8 if mod_rows == 1 else mod_rows
    mrow = mod_row0 // mblk
    row_spec = lambda w: pl.BlockSpec((tm, w), lambda i: (i, 0))
    mod_spec = lambda c: pl.BlockSpec((1, mblk, D_MODEL), lambda i: (layer, mrow, c))
    return pl.pallas_call(
        functools.partial(_inproj_kernel, mod_rows=mod_rows),
        grid=(rows // tm,),
        in_specs=[row_spec(D_MODEL), mod_spec(0), mod_spec(1),
                  pl.BlockSpec((D_MODEL, PROJ_W), lambda i: (0, 0)),
                  row_spec(HEAD_DIM),
                  pl.BlockSpec((tm, RET_W), lambda i: (0, 0))],
        out_specs=[row_spec(SPLIT_W), row_spec(SPLIT_W), row_spec(SPLIT_W), row_spec(SPLIT_W),
                   row_spec(RET_W), row_spec(SPLIT_W), row_spec(2 * SWA_KVW)],
        out_shape=[jax.ShapeDtypeStruct((rows, SPLIT_W), BF16),
                   jax.ShapeDtypeStruct((rows, SPLIT_W), BF16),
                   jax.ShapeDtypeStruct((rows, SPLIT_W), BF16),
                   jax.ShapeDtypeStruct((rows, SPLIT_W), BF16),
                   jax.ShapeDtypeStruct((rows, RET_W), F32),
                   jax.ShapeDtypeStruct((rows, SPLIT_W), BF16),
                   jax.ShapeDtypeStruct((rows, 2 * SWA_KVW), F32)],
        compiler_params=_cparams(("arbitrary",)),
        name="inproj",
    )(x, mod, mod, w_in_bf, rope, zeta_tile)


def _rope_tables(pos):
    half = HEAD_DIM // 2
    inv = ROPE_BASE ** (-jnp.arange(half, dtype=F32) / half)
    ang = pos.astype(F32)[:, None] * inv[None, :]
    cos, sin = jnp.cos(ang), jnp.sin(ang)
    return jnp.concatenate([cos, sin], axis=-1)


def _mix_prompt_kernel(gl_ref, sink_ref,
                       q_ref, k_ref, kz_ref, v_ref, g_ref, sq_ref, kvc_ref, kvp_ref,
                       decay_ref, xi_ref, bias_ref, gain_ref,
                       mixed_ref, st_ref, state):
    i = pl.program_id(0)

    @pl.when(i == 0)
    def _():
        state[...] = jnp.zeros_like(state)

    gain = gain_ref[...]
    left = lax.broadcasted_iota(I32, (CHUNK, 128), 1) < HEAD_DIM
    from_prev = (lax.broadcasted_iota(I32, (CHUNK, CHUNK), 1) >
                 lax.broadcasted_iota(I32, (CHUNK, CHUNK), 0))
    inv_dim = 1.0 / HEAD_DIM

    for sub in range(MIX_SUB):
        rows = slice(sub * CHUNK, (sub + 1) * CHUNK)

        kvc = kvc_ref[rows, :]
        kvp = kvp_ref[...] if sub == 0 else kvc_ref[(sub - 1) * CHUNK:sub * CHUNK, :]
        kcat = jnp.concatenate([kvp[:, :SWA_KVW], kvc[:, :SWA_KVW]], axis=0)
        vcat = jnp.concatenate([kvp[:, SWA_KVW:], kvc[:, SWA_KVW:]], axis=0)
        left2 = jnp.concatenate([left, left], axis=0)
        table = jnp.minimum(i, 1) if sub == 0 else 1
        swa_blocks = [None] * GQA_GROUP
        kks, vvs = [], []
        for kv in range(N_KV_HEADS):
            own2 = left2 if kv == 0 else jnp.logical_not(left2)
            kks.append(jnp.where(own2, kcat, 0.0).astype(BF16))
            vvs.append(jnp.where(own2, vcat, 0.0).astype(BF16))
        scores = []
        for hh in range(N_SWA_HEADS):
            kv, j = divmod(hh, GQA_GROUP)
            blk = 2 * j + kv
            qh = sq_ref[rows, blk * 128:(blk + 1) * 128]
            scores.append(lax.dot_general(qh, kks[kv], (((1,), (1,)), ((), ())),
                                          preferred_element_type=F32))

        heads = range(N_RET_HEADS)
        hsl = [slice(h * 128, (h + 1) * 128) for h in heads]
        s_olds = [state[h] for h in heads]
        upds = [lax.dot_general(kz_ref[rows, hsl[h]], v_ref[rows, hsl[h]], (((0,), (0,)), ((), ())),
                                preferred_element_type=F32) for h in heads]
        inter = [jnp.dot(q_ref[rows, hsl[h]], s_olds[h].astype(BF16), preferred_element_type=F32)
                 for h in heads]
        for h in heads:
            state[h] = gl_ref[h] * s_olds[h] + upds[h]
        scores_r = [lax.dot_general(q_ref[rows, hsl[h]], k_ref[rows, hsl[h]], (((1,), (1,)), ((), ())),
                                    preferred_element_type=F32) for h in heads]
        probs_r = [(scores_r[h] * decay_ref[h]).astype(BF16) for h in heads]
        intra = [jnp.dot(probs_r[h], v_ref[rows, hsl[h]], preferred_element_type=F32) for h in heads]

        owns = [left if h % 2 == 0 else jnp.logical_not(left) for h in heads]
        outs = [intra[h] + inter[h] * xi_ref[:, hsl[h]] for h in heads]
        mus = [jnp.sum(outs[h], axis=-1, keepdims=True) * inv_dim for h in heads]
        cen = [jnp.where(owns[h], outs[h] - mus[h], 0.0) for h in heads]
        var = [jnp.sum(cen[h] * cen[h], axis=-1, keepdims=True) * inv_dim for h in heads]
        nrm = [cen[h] * lax.rsqrt(var[h] + LN_EPS) for h in heads]
        ret_blocks = [nrm[2 * b] + nrm[2 * b + 1] for b in range(N_RET_HEADS // 2)]
        ret_o = jnp.concatenate(ret_blocks, axis=-1) * gain * _silu(g_ref[rows, :])

        swa_heads = range(N_SWA_HEADS)
        ss = [jnp.where(from_prev, scores[hh][:, :CHUNK], scores[hh][:, CHUNK:]) + bias_ref[table, hh]
              for hh in swa_heads]
        ms = [jnp.maximum(jnp.max(ss[hh], axis=-1, keepdims=True), sink_ref[hh]) for hh in swa_heads]
        ps = [jnp.exp(ss[hh] - ms[hh]) for hh in swa_heads]
        dens = [jnp.sum(ps[hh], axis=-1, keepdims=True) + jnp.exp(sink_ref[hh] - ms[hh]) for hh in swa_heads]
        pns = [ps[hh] / dens[hh] for hh in swa_heads]
        probs = [jnp.concatenate([jnp.where(from_prev, pns[hh], 0.0), jnp.where(from_prev, 0.0, pns[hh])],
                                 axis=-1).astype(BF16) for hh in swa_heads]
        for hh in range(N_SWA_HEADS):
            kv, j = divmod(hh, GQA_GROUP)
            o = jnp.dot(probs[hh], vvs[kv], preferred_element_type=F32)
            swa_blocks[j] = o if kv == 0 else swa_blocks[j] + o
        swa_o = jnp.concatenate(swa_blocks, axis=-1)

        mixed_ref[rows, :] = jnp.concatenate([ret_o, swa_o], axis=-1).astype(BF16)

    @pl.when(i == pl.num_programs(0) - 1)
    def _():
        st_ref[...] = state[...]


def _mix_prompt(proj, ret_consts, bias_tabs, sinks, gain):
    q, k, kz, v, g, sq, skv = proj
    rows = q.shape[0]
    tm = MIX_SUB * CHUNK
    gl, decay, xi = ret_consts
    bias_fold = bias_tabs[0]
    row_spec = lambda w: pl.BlockSpec((tm, w), lambda i: (i, 0))
    const2 = lambda a: pl.BlockSpec(a.shape, lambda i: (0, 0))
    const3 = lambda a: pl.BlockSpec(a.shape, lambda i: (0, 0, 0))
    smem = pl.BlockSpec(memory_space=pltpu.SMEM)
    mixed, st_full = pl.pallas_call(
        _mix_prompt_kernel,
        grid=(rows // tm,),
        in_specs=[smem, smem,
                  row_spec(SPLIT_W), row_spec(SPLIT_W), row_spec(SPLIT_W), row_spec(SPLIT_W), row_spec(RET_W),
                  row_spec(SPLIT_W), row_spec(2 * SWA_KVW),
                  pl.BlockSpec((CHUNK, 2 * SWA_KVW), lambda i: (jnp.maximum(MIX_SUB * i - 1, 0), 0)),
                  const3(decay), const2(xi),
                  pl.BlockSpec(bias_fold.shape, lambda i: (0, 0, 0, 0)),
                  const2(gain)],
        out_specs=[row_spec(D_MODEL),
                   pl.BlockSpec((N_RET_HEADS, 128, 128), lambda i: (0, 0, 0))],
        out_shape=[jax.ShapeDtypeStruct((rows, D_MODEL), BF16),
                   jax.ShapeDtypeStruct((N_RET_HEADS, 128, 128), F32)],
        scratch_shapes=[pltpu.VMEM((N_RET_HEADS, 128, 128), F32)],
        compiler_params=_cparams(("arbitrary",)),
        name="mix_prompt",
    )(gl, sinks, q, k, kz, v, g, sq, skv, skv, decay, xi, bias_fold, gain)
    lo, hi = slice(0, HEAD_DIM), slice(HEAD_DIM, 128)
    st = jnp.stack([st_full[h, lo, lo] if h % 2 == 0 else st_full[h, hi, hi] for h in range(N_RET_HEADS)])
    return mixed, st


def _ret_consts():
    lg = jnp.log(1.0 - 2.0 ** (-5.0 - jnp.arange(N_RET_HEADS, dtype=F32)))
    idx = jnp.arange(CHUNK, dtype=F32)
    diff = idx[:, None] - idx[None, :]
    decay = jnp.where(diff >= 0, jnp.exp(jnp.maximum(diff, 0.0)[None] * lg[:, None, None]), 0.0)
    xi = jnp.exp((idx + 1.0)[:, None] * lg[None, :])
    zeta = jnp.exp((CHUNK - 1.0 - idx)[:, None] * lg[None, :])
    gl = jnp.exp(CHUNK * lg)
    return gl, decay, jnp.repeat(xi, 128, axis=1), jnp.repeat(zeta, HEAD_DIM, axis=1), jnp.exp(1.0 * lg)


SAMPLE_BB = 8


def _mix_sample_kernel(sink_ref, gam_ref,
                       qt_ref, k_ref, v3_ref, sq3_ref, knew_ref, vnew_ref,
                       st_ref, ck_ref, cv_ref, g3_ref, x_ref, gate_ref,
                       wout_ref, brow_ref, gain_ref, lng_ref, lnb_ref,
                       x1_ref, nst_ref, nk_ref, nv_ref, ret_scr, swa_scr):
    i = pl.program_id(0)
    row = lax.broadcasted_iota(I32, (N_SWA_HEADS, 2 * HEAD_DIM), 0)
    lane = lax.broadcasted_iota(I32, (N_SWA_HEADS, 2 * HEAD_DIM), 1)
    own_half = (row // GQA_GROUP) == (lane // HEAD_DIM)
    sink_col = jnp.concatenate(
        [jnp.full((1, 1), sink_ref[hh], F32) for hh in range(N_SWA_HEADS)], axis=0)
    brow = brow_ref[...]
    qt = qt_ref[0]
    gam = gam_ref[...]
    seqs = range(SAMPLE_BB)
    state_shape = (N_RET_HEADS, HEAD_DIM, HEAD_DIM)

    s_old = [st_ref[b] for b in seqs]
    own_cols = (lax.broadcasted_iota(I32, (N_RET_HEADS, RET_W), 1) // HEAD_DIM ==
                lax.broadcasted_iota(I32, (N_RET_HEADS, RET_W), 0))
    k_blk = [jnp.where(own_cols, k_ref[b:b + 1, :], 0.0).astype(BF16) for b in seqs]
    outer = [lax.dot_general(k_blk[b], v3_ref[b].astype(BF16), (((0,), (0,)), ((), ())),
                             preferred_element_type=F32) for b in seqs]
    s_new = [gam * s_old[b] + outer[b] for b in seqs]
    for b in seqs:
        nst_ref[b] = s_new[b]
    outs = [jnp.sum((qt[:, b:b + 1] * s_new[b]).reshape(state_shape), axis=1) for b in seqs]
    for b in seqs:
        ret_scr[i * SAMPLE_BB + b] = outs[b]

    kk = [jnp.concatenate([ck_ref[0, b, 1:, :], knew_ref[b:b + 1, :]], axis=0) for b in seqs]
    vv = [jnp.concatenate([cv_ref[0, b, 1:, :], vnew_ref[b:b + 1, :]], axis=0) for b in seqs]
    for b in seqs:
        nk_ref[b] = kk[b]
        nv_ref[b] = vv[b]
    qblk = [jnp.where(own_half, jnp.concatenate([sq3_ref[b], sq3_ref[b]], axis=-1), 0.0).astype(BF16)
            for b in seqs]
    s = [lax.dot_general(qblk[b], kk[b].astype(BF16), (((1,), (1,)), ((), ())),
                         preferred_element_type=F32) + brow for b in seqs]
    m = [jnp.maximum(jnp.max(s[b], axis=-1, keepdims=True), sink_col) for b in seqs]
    p = [jnp.exp(s[b] - m[b]) for b in seqs]
    den = [jnp.sum(p[b], axis=-1, keepdims=True) + jnp.exp(sink_col - m[b]) for b in seqs]
    pn = [(p[b] / den[b]).astype(BF16) for b in seqs]
    o = [jnp.dot(pn[b], vv[b].astype(BF16), preferred_element_type=F32) for b in seqs]
    for b in seqs:
        swa_scr[i * SAMPLE_BB + b] = jnp.where(own_half[:, :HEAD_DIM], o[b][:, :HEAD_DIM], o[b][:, HEAD_DIM:])

    @pl.when(i == pl.num_programs(0) - 1)
    def _():
        y = jnp.zeros(x_ref.shape, F32)
        for h in range(N_RET_HEADS):
            o = ret_scr[:, h, :]
            mu = jnp.mean(o, axis=-1, keepdims=True)
            c = o - mu
            var = jnp.mean(c * c, axis=-1, keepdims=True)
            r = c * lax.rsqrt(var + LN_EPS) * gain_ref[h:h + 1, :] * _silu(g3_ref[h])
            y = y + jnp.dot(r.astype(BF16), wout_ref[h * HEAD_DIM:(h + 1) * HEAD_DIM, :],
                            preferred_element_type=F32)
        for hh in range(N_SWA_HEADS):
            o = swa_scr[:, hh, :].astype(BF16)
            lo = RET_W + hh * HEAD_DIM
            y = y + jnp.dot(o, wout_ref[lo:lo + HEAD_DIM, :], preferred_element_type=F32)
        x1_ref[...] = _ln(DN_ALPHA * x_ref[...] + (1.0 + gate_ref[0]) * y, lng_ref[...], lnb_ref[...])


def _mix_sample(proj, x, mod, layer, w_out_bf, gamma1, bias_row, sinks, gain8, ln_g, ln_b,
                st, ck, cv):
    q, k, _, v, g, sq, skv = proj
    nb = x.shape[0]
    steps = nb // SAMPLE_BB

    def joined(a):
        a = a.astype(F32).reshape(nb, RET_W // 128, 2, 128)
        return (a[:, :, 0, :] + a[:, :, 1, :]).reshape(nb, RET_W)

    to_cols = lambda a: joined(a).reshape(steps, SAMPLE_BB, RET_W).transpose(0, 2, 1)
    qt = to_cols(q)
    v3 = joined(v).reshape(nb, N_RET_HEADS, HEAD_DIM)
    sq3 = joined(sq).reshape(nb, GQA_GROUP, N_KV_HEADS, HEAD_DIM).transpose(0, 2, 1, 3)
    sq3 = sq3.reshape(nb, N_SWA_HEADS, HEAD_DIM)
    g3 = g.reshape(nb, N_RET_HEADS, HEAD_DIM).transpose(1, 0, 2)
    knew, vnew = skv[:, :SWA_KVW], skv[:, SWA_KVW:]
    gam =jnp.broadcast_to(jnp.repeat(gamma1, HEAD_DIM)[:, None], (RET_W, HEAD_DIM))
    smem = pl.BlockSpec(memory_space=pltpu.SMEM)
    blk3 = lambda a, b, c: pl.BlockSpec((a, b, c), lambda i: (i, 0, 0))
    lay4 = lambda b, c: pl.BlockSpec((1, SAMPLE_BB, b, c), lambda i: (layer, i, 0, 0))
    const2 = lambda a: pl.BlockSpec(a.shape, lambda i: (0, 0))
    const3 = lambda a: pl.BlockSpec(a.shape, lambda i: (0, 0, 0))
    x1, nst, nk, nv = pl.pallas_call(
        _mix_sample_kernel,
        grid=(steps,),
        in_specs=[smem, const2(gam),
                  blk3(1, RET_W, SAMPLE_BB), pl.BlockSpec((SAMPLE_BB, RET_W), lambda i: (i, 0)),
                  blk3(SAMPLE_BB, N_RET_HEADS, HEAD_DIM), blk3(SAMPLE_BB, N_SWA_HEADS, HEAD_DIM),
                  pl.BlockSpec((SAMPLE_BB, SWA_KVW), lambda i: (i, 0)),
                  pl.BlockSpec((SAMPLE_BB, SWA_KVW), lambda i: (i, 0)),
                  blk3(SAMPLE_BB, RET_W, HEAD_DIM), lay4(CHUNK, SWA_KVW), lay4(CHUNK, SWA_KVW),
                  const3(g3), const2(x),
                  pl.BlockSpec((1, nb, D_MODEL), lambda i: (layer, 0, 2)),
                  const2(w_out_bf), const2(bias_row), const2(gain8), const2(ln_g), const2(ln_b)],
        out_specs=[const2(x), blk3(SAMPLE_BB, RET_W, HEAD_DIM),
                   blk3(SAMPLE_BB, CHUNK, SWA_KVW), blk3(SAMPLE_BB, CHUNK, SWA_KVW)],
        out_shape=[jax.ShapeDtypeStruct(x.shape, F32), jax.ShapeDtypeStruct(st.shape, F32),
                   jax.ShapeDtypeStruct(ck.shape[1:], F32), jax.ShapeDtypeStruct(cv.shape[1:], F32)],
        scratch_shapes=[pltpu.VMEM((nb, N_RET_HEADS, HEAD_DIM), F32),
                        pltpu.VMEM((nb, N_SWA_HEADS, HEAD_DIM), F32)],
        compiler_params=_cparams(("arbitrary",)),
        name="mix_sample",
    )(sinks, gam, qt, joined(k), v3, sq3, knew, vnew, st, ck, cv, g3, x, mod,
      w_out_bf, bias_row, gain8, ln_g, ln_b)
    cache_shape = (nb, CHUNK, N_KV_HEADS, HEAD_DIM)
    return (x1, nst.reshape(nb, N_RET_HEADS, HEAD_DIM, HEAD_DIM), nk.reshape(cache_shape),
            nv.reshape(cache_shape))


def _router_kernel(x_ref, sh_ref, sc_ref, rwt_ref, rb_ref, tri_ref,
                   oi_ref, of_ref, cnt_ref, run_ref, *, mod_rows):
    h2 = x_ref[...] * (1.0 + sc_ref[0][:mod_rows]) + sh_ref[0][:mod_rows]
    _route_rows(h2.astype(BF16), rwt_ref, rb_ref, tri_ref, oi_ref, of_ref, cnt_ref, run_ref)


def _post_mix_kernel(mixed_ref, x_ref, gate_ref, wout_ref, lng_ref, lnb_ref,
                     sh_ref, sc_ref, rwt_ref, rb_ref, tri_ref,
                     x1_ref, oi_ref, of_ref, cnt_ref, run_ref):
    n_chunks = mixed_ref.shape[0] // CHUNK
    chunks = [slice(c * CHUNK, (c + 1) * CHUNK) for c in range(n_chunks)]
    ys = [jnp.dot(mixed_ref[r, :], wout_ref[...], preferred_element_type=F32) for r in chunks]
    x1s = [_ln(DN_ALPHA * x_ref[r, :] + (1.0 + gate_ref[0][:1]) * y, lng_ref[...], lnb_ref[...])
           for r, y in zip(chunks, ys)]
    for r, x1 in zip(chunks, x1s):
        x1_ref[r, :] = x1
    h2 = jnp.concatenate([(x1 * (1.0 + sc_ref[0][:1]) + sh_ref[0][:1]).astype(BF16) for x1 in x1s], axis=0)
    _route_rows(h2, rwt_ref, rb_ref, tri_ref, oi_ref, of_ref, cnt_ref, run_ref)


def _route_rows(h2b, rwt_ref, rb_ref, tri_ref, oi_ref, of_ref, cnt_ref, run_ref):
    i = pl.program_id(0)

    @pl.when(i == 0)
    def _():
        run_ref[...] = jnp.zeros_like(run_ref)

    logits = lax.dot_general(rwt_ref[...].astype(BF16), h2b, (((1,), (1,)), ((), ())),
                             preferred_element_type=F32)
    aff = 1.0 / (1.0 + jnp.exp(-logits))
    sel = aff + rb_ref[...]
    s = [sel[e:e + 1, :] for e in range(N_EXPERTS)]
    a = [aff[e:e + 1, :] for e in range(N_EXPERTS)]

    def top2sum(v0, v1, v2, v3):
        hi01, lo01 = jnp.maximum(v0, v1), jnp.minimum(v0, v1)
        hi23, lo23 = jnp.maximum(v2, v3), jnp.minimum(v2, v3)
        return jnp.maximum(hi01, hi23) + jnp.maximum(jnp.minimum(hi01, hi23),
                                                     jnp.maximum(lo01, lo23))

    def argmax_first(vals):
        best, idx = vals[0], jnp.zeros(vals[0].shape, I32)
        for j in range(1, len(vals)):
            upd = vals[j] > best
            idx = jnp.where(upd, j, idx)
            best = jnp.where(upd, vals[j], best)
        return idx

    def pick(idx, vals):
        out = vals[-1]
        for j in range(len(vals) - 2, -1, -1):
            out = jnp.where(idx == j, vals[j], out)
        return out

    gi = argmax_first([top2sum(*s[4 * g:4 * g + 4]) for g in range(N_GROUPS)])
    sv = [pick(gi, [s[4 * g + j] for g in range(N_GROUPS)]) for j in range(EXPERTS_PER_GROUP)]
    av = [pick(gi, [a[4 * g + j] for g in range(N_GROUPS)]) for j in range(EXPERTS_PER_GROUP)]
    i1 = argmax_first(sv)
    i2 = argmax_first([jnp.where(i1 == j, -jnp.inf, sv[j]) for j in range(EXPERTS_PER_GROUP)])
    w1, w2 = pick(i1, av), pick(i2, av)
    wsum = w1 + w2
    w1, w2 = w1 / wsum, w2 / wsum
    lo, hi = jnp.minimum(i1, i2), jnp.maximum(i1, i2)
    w_lo = jnp.where(i1 < i2, w1, w2)
    w_hi = jnp.where(i1 < i2, w2, w1)
    pair = jnp.where(lo == 0, hi - 1, jnp.where(lo == 1, 6 - hi, 5))
    bin_id = gi * N_PAIRS + pair

    tm = bin_id.shape[1]
    onehot = lax.broadcasted_iota(I32, (BIN_ROWS, tm), 0) == bin_id
    oh_f = jnp.where(onehot, 1.0, 0.0)
    before = jnp.dot(oh_f.astype(BF16), tri_ref[...], preferred_element_type=F32)
    run = run_ref[...]
    run_t = jnp.concatenate([run] * (tm // 128), axis=-1)
    rank = jnp.sum(oh_f * (before + run_t), axis=0, keepdims=True)
    run_new = run + jnp.sum(oh_f, axis=1, keepdims=True)
    run_ref[...] = run_new
    cnt_ref[...] = run_new.astype(I32)

    zi = jnp.zeros_like(bin_id)
    oi_ref[0] = jnp.concatenate([gi * 4 + lo, gi * 4 + hi, bin_id, rank.astype(I32), zi, zi, zi, zi], axis=0)
    zf = jnp.zeros_like(w_lo)
    of_ref[0] = jnp.concatenate([w_lo, w_hi, zf, zf, zf, zf, zf, zf], axis=0)


def _router(x1, mod, layer, mod_row0, mod_rows, router_wt, router_b, tm):
    rows = x1.shape[0]
    nt = rows // tm
    mblk = 8 if mod_rows == 1 else mod_rows
    mrow = mod_row0 // mblk
    tri = jnp.asarray(np.triu(np.ones((tm, tm), np.float32), 1), BF16)
    rb = jnp.broadcast_to(router_b.astype(F32)[:, None], (N_EXPERTS, tm))
    mod_spec = lambda c: pl.BlockSpec((1, mblk, D_MODEL), lambda i: (layer, mrow, c))
    oi, of, cnt = pl.pallas_call(
        functools.partial(_router_kernel, mod_rows=mod_rows),
        grid=(nt,),
        in_specs=[pl.BlockSpec((tm, D_MODEL), lambda i: (i, 0)), mod_spec(3), mod_spec(4),
                  pl.BlockSpec((N_EXPERTS, D_MODEL), lambda i: (0, 0)),
                  pl.BlockSpec((N_EXPERTS, tm), lambda i: (0, 0)),
                  pl.BlockSpec((tm, tm), lambda i: (0, 0))],
        out_specs=[pl.BlockSpec((1, 8, tm), lambda i: (i, 0, 0)),
                   pl.BlockSpec((1, 8, tm), lambda i: (i, 0, 0)),
                   pl.BlockSpec((BIN_ROWS, 128), lambda i: (0, 0))],
        out_shape=[jax.ShapeDtypeStruct((nt, 8, tm), I32),
                   jax.ShapeDtypeStruct((nt, 8, tm), F32),
                   jax.ShapeDtypeStruct((BIN_ROWS, 128), I32)],
        scratch_shapes=[pltpu.VMEM((BIN_ROWS, 128), F32)],
        compiler_params=_cparams(("arbitrary",)),
        name="router",
    )(x1, mod, mod, router_wt, rb, tri)
    return _unpack_route(oi, of, cnt)


def _unpack_route(oi, of, cnt):
    flat = lambda a, r: a[:, r, :].reshape(-1)
    return (flat(oi, 0), flat(oi, 1), flat(oi, 2), flat(oi, 3), flat(of, 0), flat(of, 1),
            cnt[:N_BINS, 0])


POST_TM = 512


def _post_mix(mixed, x, mod, layer, mod_row0, w_out_bf, ln_g, ln_b, router_wt, router_b):
    rows = x.shape[0]
    tm = POST_TM
    nt = rows // tm
    mrow = mod_row0 // 8
    tri = jnp.asarray(np.triu(np.ones((tm, tm), np.float32), 1), BF16)
    rb = jnp.broadcast_to(router_b.astype(F32)[:, None], (N_EXPERTS, tm))
    mod_spec = lambda c: pl.BlockSpec((1, 8, D_MODEL), lambda i: (layer, mrow, c))
    row_spec = pl.BlockSpec((tm, D_MODEL), lambda i: (i, 0))
    route_spec = pl.BlockSpec((1, 8, tm), lambda i: (i, 0, 0))
    const2 = lambda a: pl.BlockSpec(a.shape, lambda i: (0, 0))
    x1, oi, of, cnt = pl.pallas_call(
        _post_mix_kernel,
        grid=(nt,),
        in_specs=[row_spec, row_spec, mod_spec(2), const2(w_out_bf), const2(ln_g), const2(ln_b),
                  mod_spec(3), mod_spec(4), const2(router_wt), const2(rb), const2(tri)],
        out_specs=[row_spec, route_spec, route_spec,
                   pl.BlockSpec((BIN_ROWS, 128), lambda i: (0, 0))],
        out_shape=[jax.ShapeDtypeStruct((rows, D_MODEL), F32),
                   jax.ShapeDtypeStruct((nt, 8, tm), I32),
                   jax.ShapeDtypeStruct((nt, 8, tm), F32),
                   jax.ShapeDtypeStruct((BIN_ROWS, 128), I32)],
        scratch_shapes=[pltpu.VMEM((BIN_ROWS, 128), F32)],
        compiler_params=_cparams(("arbitrary",)),
        name="post_mix",
    )(mixed, x, mod, w_out_bf, ln_g, ln_b, mod, mod, router_wt, rb, tri)
    return x1, _unpack_route(oi, of, cnt)


DMA_UNROLL = 8
ROW_GROUPS = MOE_TM // DMA_UNROLL


GROUPS_PER_ITER = 4


def _for_rows(n, fn):
    per_iter = GROUPS_PER_ITER * DMA_UNROLL

    def body_u(jj, carry):
        for g in range(GROUPS_PER_ITER):
            for u in range(DMA_UNROLL):
                fn(jj * GROUPS_PER_ITER + g, u)
        return carry
    lax.fori_loop(0, n // per_iter, body_u, 0)

    def body_1(r, carry):
        fn(r // DMA_UNROLL, r % DMA_UNROLL)
        return carry
    lax.fori_loop((n // per_iter) * per_iter, n, body_1, 0)


def _wait_rows(n, buf, sem):
    p = ROW_GROUPS
    while p >= 1:
        @pl.when((n & (p * DMA_UNROLL)) != 0)
        def _(p=p):
            pltpu.make_async_copy(buf.at[pl.ds(0, p)], buf.at[pl.ds(0, p)], sem).wait()
        p //= 2
    p = DMA_UNROLL // 2
    while p >= 1:
        @pl.when((n & p) != 0)
        def _(p=p):
            pltpu.make_async_copy(buf.at[0, pl.ds(0, p)], buf.at[0, pl.ds(0, p)], sem).wait()
        p //= 2


def _moe_sorted_kernel(ea_ref, eb_ref, nvalid_ref, pos_ref,
                       x_hbm, sh_ref, sc_ref, gate_ref, rw_ref,
                       wga_ref, wua_ref, wda_ref, wgb_ref, wub_ref, wdb_ref,
                       lng_ref, lnb_ref,
                       out_hbm,
                       inv, xbuf, obuf, wg_a, wu_a, wd_a, wg_b, wu_b, wd_b, gsem, ssem):
    i = pl.program_id(0)
    nt = pl.num_programs(0)
    slot = i % 2
    n_tok = pos_ref.shape[0]

    def start_gather(t, dst_slot):
        def one(j, u):
            tok = inv[t * MOE_TM + j * DMA_UNROLL + u]
            pltpu.make_async_copy(x_hbm.at[pl.ds(tok, 1)], xbuf.at[dst_slot, j, pl.ds(u, 1)],
                                  gsem.at[dst_slot]).start()
        _for_rows(nvalid_ref[t], one)

    @pl.when(i == 0)
    def _():
        def body(j, carry):
            for u in range(DMA_UNROLL):
                t = j * DMA_UNROLL + u
                inv[pos_ref[t]] = t
            return carry
        lax.fori_loop(0, n_tok // DMA_UNROLL, body, 0)
        xbuf[...] = jnp.zeros_like(xbuf)
        start_gather(0, 0)

    @pl.when(i >= 2)
    def _():
        _wait_rows(nvalid_ref[jnp.maximum(i - 2, 0)], obuf.at[slot], ssem.at[slot])

    @pl.when(nvalid_ref[i] > 0)
    def _():
        _wait_rows(nvalid_ref[i], xbuf.at[slot], gsem.at[slot])

        @pl.when(i + 1 < nt)
        def _():
            start_gather(jnp.minimum(i + 1, nt - 1), 1 - slot)

        @pl.when(jnp.logical_or(i == 0, ea_ref[i] != ea_ref[jnp.maximum(i - 1, 0)]))
        def _():
            wg_a[...] = wga_ref[0, 0].astype(BF16)
            wu_a[...] = wua_ref[0, 0].astype(BF16)
            wd_a[...] = wda_ref[0, 0].astype(BF16)

        @pl.when(jnp.logical_or(i == 0, eb_ref[i] != eb_ref[jnp.maximum(i - 1, 0)]))
        def _():
            wg_b[...] = wgb_ref[0, 0].astype(BF16)
            wu_b[...] = wub_ref[0, 0].astype(BF16)
            wd_b[...] = wdb_ref[0, 0].astype(BF16)

        x = xbuf[slot].reshape(MOE_TM, D_MODEL)
        h2 = (x * (1.0 + sc_ref[0][:1]) + sh_ref[0][:1]).astype(BF16)

        dot = functools.partial(jnp.dot, preferred_element_type=F32)
        logits = dot(h2, rw_ref[...])
        gate_a, gate_b = dot(h2, wg_a[...]), dot(h2, wg_b[...])
        up_a, up_b = dot(h2, wu_a[...]), dot(h2, wu_b[...])
        act_a = (_silu(gate_a) * up_a).astype(BF16)
        act_b = (_silu(gate_b) * up_b).astype(BF16)
        y_a, y_b = dot(act_a, wd_a[...]), dot(act_b, wd_b[...])

        aff = 1.0 / (1.0 + jnp.exp(-logits))
        lane = lax.broadcasted_iota(I32, aff.shape, 1)
        a_lo = jnp.sum(jnp.where(lane == ea_ref[i], aff, 0.0), axis=-1, keepdims=True)
        a_hi = jnp.sum(jnp.where(lane == eb_ref[i], aff, 0.0), axis=-1, keepdims=True)
        a_sum = a_lo + a_hi
        y = (a_lo / a_sum) * y_a + (a_hi / a_sum) * y_b
        out = _ln(DN_ALPHA * x + (1.0 + gate_ref[0][:1]) * y, lng_ref[...], lnb_ref[...])
        obuf[slot] = out.reshape(ROW_GROUPS, DMA_UNROLL, D_MODEL)

        def scatter_one(j, u):
            tok = inv[i * MOE_TM + j * DMA_UNROLL + u]
            pltpu.make_async_copy(obuf.at[slot, j, pl.ds(u, 1)], out_hbm.at[pl.ds(tok, 1)],
                                  ssem.at[slot]).start()
        _for_rows(nvalid_ref[i], scatter_one)

    @pl.when(i == nt - 1)
    def _():
        _wait_rows(nvalid_ref[jnp.maximum(i - 1, 0)], obuf.at[1 - slot], ssem.at[1 - slot])
        _wait_rows(nvalid_ref[i], obuf.at[slot], ssem.at[slot])


def _moe_sorted(x1, mod, layer, mod_row0, route, rw_pad, wg, wu, wd, ln_g, ln_b):
    _, _, bin_id, rank, _, _, counts = route
    rows = x1.shape[0]
    nt = rows // MOE_TM + N_BINS
    ns = nt * MOE_TM
    tiles = (counts + MOE_TM - 1) // MOE_TM
    tile_end = jnp.cumsum(tiles)
    row_start = ((tile_end - tiles) * MOE_TM).astype(I32)
    pos = jnp.sum(jnp.where(bin_id[:, None] == jnp.arange(N_BINS, dtype=I32)[None, :],
                            row_start[None, :], 0), axis=1) + rank
    total = tile_end[-1]
    t = jnp.arange(nt, dtype=I32)
    tile_bin = jnp.sum(jnp.minimum(t, total - 1)[:, None] >= tile_end[None, :], axis=1).astype(I32)
    tile_bin = jnp.clip(tile_bin, 0, N_BINS - 1)
    bin_lo = jnp.asarray([g * 4 + _PAIR_LO[p] for g in range(N_GROUPS) for p in range(N_PAIRS)], I32)
    bin_hi = jnp.asarray([g * 4 + _PAIR_HI[p] for g in range(N_GROUPS) for p in range(N_PAIRS)], I32)
    tile_ea, tile_eb = bin_lo[tile_bin], bin_hi[tile_bin]
    nvalid = jnp.clip(counts[tile_bin] - (t - (tile_end - tiles)[tile_bin]) * MOE_TM, 0, MOE_TM)
    nvalid = jnp.where(t < total, nvalid, 0).astype(I32)
    mrow = mod_row0 // 8
    mod_spec = lambda c: pl.BlockSpec((1, 8, D_MODEL), lambda i, *_: (layer, mrow, c))
    w_spec = lambda shape, ref_idx: pl.BlockSpec(
        (1, 1) + shape, lambda i, ea_r, eb_r, *_: (layer, (ea_r, eb_r)[ref_idx][i], 0, 0))
    any_spec = pl.BlockSpec(memory_space=pl.ANY)
    w_bf = lambda shape: pltpu.VMEM(shape, BF16)
    grid_spec = pltpu.PrefetchScalarGridSpec(
        num_scalar_prefetch=4,
        grid=(nt,),
        in_specs=[any_spec,
                  mod_spec(3), mod_spec(4), mod_spec(5),
                  pl.BlockSpec((D_MODEL, 128), lambda i, *_: (0, 0)),
                  w_spec((D_MODEL, D_FF), 0), w_spec((D_MODEL, D_FF), 0), w_spec((D_FF, D_MODEL), 0),
                  w_spec((D_MODEL, D_FF), 1), w_spec((D_MODEL, D_FF), 1), w_spec((D_FF, D_MODEL), 1),
                  pl.BlockSpec((1, D_MODEL), lambda i, *_: (0, 0)),
                  pl.BlockSpec((1, D_MODEL), lambda i, *_: (0, 0))],
        out_specs=any_spec,
        scratch_shapes=[pltpu.SMEM((ns,), I32),
                        pltpu.VMEM((2, ROW_GROUPS, DMA_UNROLL, D_MODEL), F32),
                        pltpu.VMEM((2, ROW_GROUPS, DMA_UNROLL, D_MODEL), F32),
                        w_bf((D_MODEL, D_FF)), w_bf((D_MODEL, D_FF)), w_bf((D_FF, D_MODEL)),
                        w_bf((D_MODEL, D_FF)), w_bf((D_MODEL, D_FF)), w_bf((D_FF, D_MODEL)),
                        pltpu.SemaphoreType.DMA((2,)),
                        pltpu.SemaphoreType.DMA((2,))],
    )
    return pl.pallas_call(
        _moe_sorted_kernel,
        grid_spec=grid_spec,
        out_shape=jax.ShapeDtypeStruct((rows, D_MODEL), F32),
        compiler_params=_cparams(("arbitrary",)),
        name="moe_sorted",
    )(tile_ea, tile_eb, nvalid, pos, x1, mod, mod, mod, rw_pad,
      wg, wu, wd, wg, wu, wd, ln_g, ln_b)


def _moe_dense_kernel(x_ref, dw_ref, sh_ref, sc_ref, gate_ref, wg_ref, wu_ref, wd_ref,
                      lng_ref, lnb_ref, o_ref, acc):
    e = pl.program_id(0)

    @pl.when(e == 0)
    def _():
        acc[...] = jnp.zeros_like(acc)

    x = x_ref[...]
    h2 = (x * (1.0 + sc_ref[0]) + sh_ref[0]).astype(BF16)
    a = _silu(jnp.dot(h2, wg_ref[0, 0].astype(BF16), preferred_element_type=F32)) * \
        jnp.dot(h2, wu_ref[0, 0].astype(BF16), preferred_element_type=F32)
    y = jnp.dot(a.astype(BF16), wd_ref[0, 0].astype(BF16), preferred_element_type=F32)
    acc[...] = acc[...] + dw_ref[0][:, :1] * y

    @pl.when(e == pl.num_programs(0) - 1)
    def _():
        o_ref[...] = _ln(DN_ALPHA * x + (1.0 + gate_ref[0]) * acc[...], lng_ref[...], lnb_ref[...])


def _moe_dense(x1, mod, layer, route, wg, wu, wd, ln_g, ln_b):
    ea, eb, _, _, w_lo, w_hi, _ = route
    nb = x1.shape[0]
    eids = jnp.arange(N_EXPERTS, dtype=I32)[:, None]
    dw = jnp.where(eids == ea[None, :], w_lo[None, :], 0.0) + \
        jnp.where(eids == eb[None, :], w_hi[None, :], 0.0)
    dw = jnp.broadcast_to(dw[:, :, None], (N_EXPERTS, nb, 128))
    mod_spec = lambda c: pl.BlockSpec((1, nb, D_MODEL), lambda e: (layer, 0, c))
    const2 = lambda a: pl.BlockSpec(a.shape, lambda e: (0, 0))
    return pl.pallas_call(
        _moe_dense_kernel,
        grid=(N_EXPERTS,),
        in_specs=[const2(x1), pl.BlockSpec((1, nb, 128), lambda e: (e, 0, 0)),
                  mod_spec(3), mod_spec(4), mod_spec(5),
                  pl.BlockSpec((1, 1, D_MODEL, D_FF), lambda e: (layer, e, 0, 0)),
                  pl.BlockSpec((1, 1, D_MODEL, D_FF), lambda e: (layer, e, 0, 0)),
                  pl.BlockSpec((1, 1, D_FF, D_MODEL), lambda e: (layer, e, 0, 0)),
                  const2(ln_g), const2(ln_b)],
        out_specs=const2(x1),
        out_shape=jax.ShapeDtypeStruct(x1.shape, F32),
        scratch_shapes=[pltpu.VMEM(x1.shape, F32)],
        compiler_params=_cparams(("arbitrary",)),
        name="moe_dense",
    )(x1, dw, mod, mod, mod, wg, wu, wd, ln_g, ln_b)


def kernel(x_prompt, x_sample, state_ret, cache_swa_k, cache_swa_v, c_prompt, c_sample, w_in, w_out, ret_gn_gain, swa_sinks, rel_bias_table, ada_w, ada_b, ln1_g, ln1_b, ln2_g, ln2_b, router_w, router_b, exp_w_gate, exp_w_up, exp_w_down):
    seq = x_prompt.shape[1]
    nb = x_sample.shape[0]
    past_len = 16384
    assert x_prompt.shape[0] == 1 and x_sample.shape[1] == 1

    c_all = jnp.concatenate([c_sample, c_prompt, jnp.zeros((7, D_MODEL), F32)], axis=0)
    mod = _ada(c_all, ada_w, ada_b)
    prompt_row = nb

    bias_tabs = _bias_tables(rel_bias_table.astype(F32))
    gl, decay, xi, zeta, gamma1 = _ret_consts()
    rope_p = _rope_tables(jnp.arange(seq, dtype=I32))
    rope_s = _rope_tables(jnp.full((nb,), past_len, I32))
    router_wt = router_w.astype(F32).T
    rw_pad = jnp.pad(router_w.astype(BF16), ((0, 0), (0, 128 - N_EXPERTS)))
    zeta_p = jnp.tile(zeta, (INPROJ_TM // CHUNK, 1))
    zeta_s = jnp.ones((nb, RET_W), F32)

    ck_all = cache_swa_k.astype(F32).reshape(DEPTH, nb, CHUNK, SWA_KVW)
    cv_all = cache_swa_v.astype(F32).reshape(DEPTH, nb, CHUNK, SWA_KVW)

    xp = x_prompt.reshape(seq, D_MODEL)
    xs = x_sample.reshape(nb, D_MODEL)
    st_p, k_p, v_p, st_s, k_s, v_s = [], [], [], [], [], []
    for l in range(DEPTH):
        w_in_bf = w_in[l].astype(BF16)
        w_out_bf = w_out[l].astype(BF16)
        order = jnp.asarray(_SWA_HEAD_ORDER)
        swa_rows = w_out[l][RET_W:].reshape(N_SWA_HEADS, HEAD_DIM, D_MODEL)[order]
        w_out_prompt = jnp.concatenate([w_out[l][:RET_W], swa_rows.reshape(SWA_QW, D_MODEL)],
                                       axis=0).astype(BF16)
        experts = (exp_w_gate.astype(F32), exp_w_up.astype(F32), exp_w_down.astype(F32))
        gain = ret_gn_gain[l].astype(F32)
        sinks = swa_sinks[l].astype(F32)
        row = lambda a: a[l].astype(F32).reshape(1, D_MODEL)

        proj = _inproj(xp, mod, l, prompt_row, 1, w_in_bf, rope_p, zeta_p, INPROJ_TM)
        mixed, st = _mix_prompt(proj, (gl, decay, xi), bias_tabs, sinks, gain.reshape(1, RET_W))
        skv = proj[6]
        st_p.append(st.reshape(1, N_RET_HEADS, HEAD_DIM, HEAD_DIM))
        k_p.append(skv[seq - CHUNK:, :SWA_KVW].reshape(1, CHUNK, N_KV_HEADS, HEAD_DIM))
        v_p.append(skv[seq - CHUNK:, SWA_KVW:].reshape(1, CHUNK, N_KV_HEADS, HEAD_DIM))
        x1, route = _post_mix(mixed, xp, mod, l, prompt_row, w_out_prompt, row(ln1_g), row(ln1_b),
                              router_wt, router_b)
        xp = _moe_sorted(x1, mod, l, prompt_row, route, rw_pad, *experts, row(ln2_g), row(ln2_b))

        proj = _inproj(xs, mod, l, 0, nb, w_in_bf, rope_s, zeta_s, nb)
        x1, nst, nk, nv = _mix_sample(proj, xs, mod, l, w_out_bf, gamma1, bias_tabs[1], sinks, gain,
                                      row(ln1_g), row(ln1_b),
                                      state_ret[l].astype(F32).reshape(nb, RET_W, HEAD_DIM), ck_all, cv_all)
        st_s.append(nst)
        k_s.append(nk)
        v_s.append(nv)
        route = _router(x1, mod, l, 0, nb, router_wt, router_b, nb)
        xs = _moe_dense(x1, mod, l, route, *experts, row(ln2_g), row(ln2_b))

    return (xp.reshape(1, seq, D_MODEL), xs.reshape(nb, 1, D_MODEL),
            jnp.stack(st_p), jnp.stack(k_p), jnp.stack(v_p),
            jnp.stack(st_s), jnp.stack(k_s), jnp.stack(v_s))
```
